```python
import math
import jax, jax.numpy as jnp
from jax import lax
import numpy as np

D_MODEL = 1024
BATCH = 8
SEQ = 8192
DEPTH = 1

N_META = 16
MLA_HEADS = 8
QK_NOPE_DIM = 128
QK_ROPE_DIM = 64
V_HEAD_DIM = 128
Q_LORA_RANK = 384
KV_LORA_RANK = 256
ROPE_THETA = 10000.0
Q_BLOCK = 128
SOFTMAX_SCALE = (QK_NOPE_DIM + QK_ROPE_DIM) ** -0.5
D_ATTN = MLA_HEADS * V_HEAD_DIM
SSM_HEADS = 16
SSM_HEAD_DIM = 64
SSM_GROUPS = 2
HEADS_PER_GROUP = SSM_HEADS // SSM_GROUPS
SSM_STATE = 128
SSM_CONV = 4
CHUNK = 128
D_SSM = SSM_HEADS * SSM_HEAD_DIM
D_XBC = D_SSM + 2 * SSM_GROUPS * SSM_STATE
D_MIX = D_ATTN + D_SSM
D_FF = 2816
FFN_CONV = 3
EPS = 1e-6
IN_SPLITS = (Q_LORA_RANK, KV_LORA_RANK, QK_ROPE_DIM, D_SSM, D_XBC, SSM_HEADS)
D_IN = sum(IN_SPLITS)

kernel_name = 'hymba_mla_ssd_convffn_layer'


def rms_norm(x, gain):
    xf = x.astype(jnp.float32)
    y = xf * lax.rsqrt(jnp.mean(xf * xf, axis=-1, keepdims=True) + EPS)
    return (y * gain.astype(jnp.float32)).astype(x.dtype)


def causal_dwconv(x, w, b):
    k = w.shape[0]
    y = lax.conv_general_dilated(x, w[:, None, :].astype(x.dtype), window_strides=(1,),
                                 padding=[(k - 1, 0)], dimension_numbers=('NWC', 'WIO', 'NWC'),
                                 feature_group_count=x.shape[-1])
    return y + b.astype(x.dtype)


def split_cols(t, sizes):
    idx = np.cumsum(sizes)[:-1].tolist()
    return jnp.split(t, idx, axis=-1)


def rope_tables(n):
    inv = ROPE_THETA ** (-jnp.arange(0, QK_ROPE_DIM, 2, dtype=jnp.float32) / QK_ROPE_DIM)
    ang = jnp.arange(n, dtype=jnp.float32)[:, None] * inv[None, :]
    return jnp.cos(ang), jnp.sin(ang)


def apply_rope(x, cos, sin):
    x1, x2 = jnp.split(x, 2, axis=-1)
    cos = cos.astype(x.dtype)
    sin = sin.astype(x.dtype)
    return jnp.concatenate([x1 * cos - x2 * sin, x1 * sin + x2 * cos], axis=-1)


def attend_block(q_nope, q_pe, q_pos, k_nope, k_pe, v, k_pos):
    s = jnp.einsum('bqhd,bkhd->bhqk', q_nope, k_nope) + jnp.einsum('bqhr,bkr->bhqk', q_pe, k_pe)
    s = s.astype(jnp.float32) * SOFTMAX_SCALE
    s = jnp.where(k_pos[None, :] <= q_pos[:, None], s, -jnp.inf)
    p = jax.nn.softmax(s, axis=-1).astype(v.dtype)
    return jnp.einsum('bhqk,bkhd->bqhd', p, v)


def mla_group(q_c, kv_c, k_pe_raw, q_a_norm, w_uq, kv_a_norm, w_ukv, cos, sin):
    bsz, L = q_c.shape[0], q_c.shape[1]
    S = L - N_META
    q = jnp.einsum('blr,rhd->blhd', rms_norm(q_c, q_a_norm), w_uq)
    q_nope = q[..., :QK_NOPE_DIM]
    q_pe = apply_rope(q[..., QK_NOPE_DIM:], cos[:, None, :], sin[:, None, :])
    kv = jnp.einsum('blc,chd->blhd', rms_norm(kv_c, kv_a_norm), w_ukv)
    k_nope, v = kv[..., :QK_NOPE_DIM], kv[..., QK_NOPE_DIM:]
    k_pe = apply_rope(k_pe_raw, cos, sin)
    pos = jnp.arange(L)
    o_meta = attend_block(q_nope[:, :N_META], q_pe[:, :N_META], pos[:N_META],
                          k_nope[:, :N_META], k_pe[:, :N_META], v[:, :N_META], pos[:N_META])
    nb = S // Q_BLOCK

    def to_blocks(t):
        t = t[:, N_META:].reshape((bsz, nb, Q_BLOCK) + t.shape[2:])
        return jnp.moveaxis(t, 1, 0)

    o_real = lax.map(lambda blk: attend_block(blk[0], blk[1], blk[2], k_nope, k_pe, v, pos),
                     (to_blocks(q_nope), to_blocks(q_pe), pos[N_META:].reshape(nb, Q_BLOCK)))
    o_real = jnp.moveaxis(o_real, 0, 1).reshape(bsz, S, MLA_HEADS, V_HEAD_DIM)
    o = jnp.concatenate([o_meta, o_real], axis=1)
    return o.reshape(bsz, L, D_ATTN)


def ssd_chunks(xs, dt, A, Bm, Cm, init):
    Acs = jnp.cumsum(dt * A, axis=2)
    Q = xs.shape[2]
    causal = jnp.tril(jnp.ones((Q, Q), dtype=bool))[:, :, None, None]
    seg = Acs[:, :, :, None] - Acs[:, :, None, :]
    decay_in = jnp.exp(jnp.where(causal, seg, -jnp.inf))
    xdt = xs * dt[..., None]
    cb = jnp.einsum('bclgn,bcsgn->bclsg', Cm, Bm)
    y_diag = jnp.einsum('bclsgh,bcsghp->bclghp', cb[..., None] * decay_in, xdt)
    decay_to_end = jnp.exp(Acs[:, :, -1:] - Acs)
    states = jnp.einsum('bcsgn,bcsghp->bcghpn', Bm, xdt * decay_to_end[..., None])
    chunk_decay = jnp.exp(Acs[:, :, -1])

    def step(carry, inp):
        st, dec = inp
        return carry * dec[..., None, None] + st, carry

    final, prev = lax.scan(step, init, (jnp.moveaxis(states, 1, 0), jnp.moveaxis(chunk_decay, 1, 0)))
    prev = jnp.moveaxis(prev, 0, 1)
    y_off = jnp.einsum('bclgn,bcghpn->bclghp', Cm, prev) * jnp.exp(Acs)[..., None]
    return y_diag + y_off, final


def ssd_group(z, xbc, dt_raw, conv_w, conv_b, dt_bias, A_log, D, norm_gain):
    bsz, L = z.shape[0], z.shape[1]
    S = L - N_META
    nc = S // CHUNK
    f32 = jnp.float32
    xbc = jax.nn.silu(causal_dwconv(xbc, conv_w, conv_b)).astype(f32)
    xs, Bm, Cm = split_cols(xbc, (D_SSM, SSM_GROUPS * SSM_STATE, SSM_GROUPS * SSM_STATE))
    xs = xs.reshape(bsz, L, SSM_GROUPS, HEADS_PER_GROUP, SSM_HEAD_DIM)
    Bm = Bm.reshape(bsz, L, SSM_GROUPS, SSM_STATE)
    Cm = Cm.reshape(bsz, L, SSM_GROUPS, SSM_STATE)
    dt = jax.nn.softplus(dt_raw.astype(f32) + dt_bias.astype(f32)).reshape(bsz, L, SSM_GROUPS, HEADS_PER_GROUP)
    A = -jnp.exp(A_log.astype(f32)).reshape(SSM_GROUPS, HEADS_PER_GROUP)

    def meta_part(t):
        return t[:, None, :N_META]

    def real_part(t):
        return t[:, N_META:].reshape((bsz, nc, CHUNK) + t.shape[2:])

    init = jnp.zeros((bsz, SSM_GROUPS, HEADS_PER_GROUP, SSM_HEAD_DIM, SSM_STATE), f32)
    y_meta, h_meta = ssd_chunks(meta_part(xs), meta_part(dt), A, meta_part(Bm), meta_part(Cm), init)
    y_real, _ = ssd_chunks(real_part(xs), real_part(dt), A, real_part(Bm), real_part(Cm), h_meta)
    y = jnp.concatenate([y_meta.reshape(bsz, N_META, SSM_GROUPS, HEADS_PER_GROUP, SSM_HEAD_DIM),
                         y_real.reshape(bsz, S, SSM_GROUPS, HEADS_PER_GROUP, SSM_HEAD_DIM)], axis=1)
    y = y + D.astype(f32).reshape(SSM_GROUPS, HEADS_PER_GROUP)[..., None] * xs
    gsz = HEADS_PER_GROUP * SSM_HEAD_DIM
    y = y.reshape(bsz, L, SSM_GROUPS, gsz) * jax.nn.silu(z.astype(f32)).reshape(bsz, L, SSM_GROUPS, gsz)
    y = y * lax.rsqrt(jnp.mean(y * y, axis=-1, keepdims=True) + EPS)
    return (y.reshape(bsz, L, D_SSM) * norm_gain.astype(f32)).astype(z.dtype)


def conv_ffn(h, w_up, conv_w, conv_b, w_down):
    u = causal_dwconv(h @ w_up, conv_w, conv_b)
    g, v = jnp.split(u, 2, axis=-1)
    return (jax.nn.silu(g) * v) @ w_down


def _fwd_setup_inputs(seed: int = 0) -> dict:
    key = jax.random.key(seed)
    ks = jax.random.split(key, 26)
    f32 = jnp.float32

    def nrm(k, shape, scale):
        return scale * jax.random.normal(k, shape, f32)

    def gain(k, n):
        return 1.0 + 0.05 * jax.random.normal(k, (DEPTH, n), f32)

    dt0 = jnp.exp(jax.random.uniform(ks[17], (DEPTH, SSM_HEADS), f32, math.log(1e-3), math.log(1e-1)))
    return {
        'x': nrm(ks[0], (BATCH, SEQ, D_MODEL), 1.0),
        'meta_tokens': nrm(ks[1], (N_META, D_MODEL), 1.0),
        'norm_mix_pre': gain(ks[2], D_MODEL),
        'norm_mix_post': gain(ks[3], D_MODEL),
        'norm_ffn_pre': gain(ks[4], D_MODEL),
        'norm_ffn_post': gain(ks[5], D_MODEL),
        'w_in': nrm(ks[6], (DEPTH, D_MODEL, D_IN), D_MODEL ** -0.5),
        'q_a_norm': gain(ks[7], Q_LORA_RANK),
        'w_uq': nrm(ks[8], (DEPTH, Q_LORA_RANK, MLA_HEADS, QK_NOPE_DIM + QK_ROPE_DIM), Q_LORA_RANK ** -0.5),
        'kv_a_norm': gain(ks[9], KV_LORA_RANK),
        'w_ukv': nrm(ks[10], (DEPTH, KV_LORA_RANK, MLA_HEADS, QK_NOPE_DIM + V_HEAD_DIM), KV_LORA_RANK ** -0.5),
        'attn_out_norm': gain(ks[11], D_ATTN),
        'ssm_conv_w': nrm(ks[12], (DEPTH, SSM_CONV, D_XBC), SSM_CONV ** -0.5),
        'ssm_conv_b': nrm(ks[13], (DEPTH, D_XBC), 0.02),
        'ssm_dt_bias': dt0 + jnp.log(-jnp.expm1(-dt0)),
        'ssm_A_log': jnp.log(jax.random.uniform(ks[14], (DEPTH, SSM_HEADS), f32, 1.0, 16.0)),
        'ssm_D': 1.0 + 0.1 * jax.random.normal(ks[15], (DEPTH, SSM_HEADS), f32),
        'ssm_norm': gain(ks[16], D_SSM),
        'w_out': nrm(ks[18], (DEPTH, D_MIX, D_MODEL), D_MIX ** -0.5),
        'w_up': nrm(ks[19], (DEPTH, D_MODEL, 2 * D_FF), D_MODEL ** -0.5),
        'ffn_conv_w': nrm(ks[20], (DEPTH, FFN_CONV, 2 * D_FF), FFN_CONV ** -0.5),
        'ffn_conv_b': nrm(ks[21], (DEPTH, 2 * D_FF), 0.02),
        'w_down': nrm(ks[22], (DEPTH, D_FF, D_MODEL), D_FF ** -0.5),
    }


def _fwd_reference(x, meta_tokens, norm_mix_pre, norm_mix_post, norm_ffn_pre, norm_ffn_post, w_in,
              q_a_norm, w_uq, kv_a_norm, w_ukv, attn_out_norm, ssm_conv_w, ssm_conv_b, ssm_dt_bias,
              ssm_A_log, ssm_D, ssm_norm, w_out, w_up, ffn_conv_w, ffn_conv_b, w_down):
    bsz = x.shape[0]
    L = N_META + x.shape[1]
    meta = jnp.broadcast_to(meta_tokens[None].astype(x.dtype), (bsz, N_META, D_MODEL))
    h = jnp.concatenate([meta, x], axis=1)
    cos, sin = rope_tables(L)
    for l in range(DEPTH):
        hn = rms_norm(h, norm_mix_pre[l])
        q_c, kv_c, k_pe, z, xbc, dt_raw = split_cols(hn @ w_in[l], IN_SPLITS)
        attn = mla_group(q_c, kv_c, k_pe, q_a_norm[l], w_uq[l], kv_a_norm[l], w_ukv[l], cos, sin)
        ssm = ssd_group(z, xbc, dt_raw, ssm_conv_w[l], ssm_conv_b[l], ssm_dt_bias[l], ssm_A_log[l],
                        ssm_D[l], ssm_norm[l])
        mix = jnp.concatenate([rms_norm(attn, attn_out_norm[l]), ssm], axis=-1) @ w_out[l]
        h = h + rms_norm(mix, norm_mix_post[l])
        hn = rms_norm(h, norm_ffn_pre[l])
        h = h + rms_norm(conv_ffn(hn, w_up[l], ffn_conv_w[l], ffn_conv_b[l], w_down[l]), norm_ffn_post[l])
    return h[:, N_META:]


import jax as _jax
import jax.numpy as _jnp

TWIN_FORMAT = 'train_step'
FWD_PARAMS = ['x', 'meta_tokens', 'norm_mix_pre', 'norm_mix_post', 'norm_ffn_pre', 'norm_ffn_post', 'w_in', 'q_a_norm', 'w_uq', 'kv_a_norm', 'w_ukv', 'attn_out_norm', 'ssm_conv_w', 'ssm_conv_b', 'ssm_dt_bias', 'ssm_A_log', 'ssm_D', 'ssm_norm', 'w_out', 'w_up', 'ffn_conv_w', 'ffn_conv_b', 'w_down']
TWIN_WEIGHTS = ['meta_tokens', 'norm_mix_pre', 'norm_mix_post', 'norm_ffn_pre', 'norm_ffn_post', 'w_in', 'q_a_norm', 'w_uq', 'kv_a_norm', 'w_ukv', 'attn_out_norm', 'ssm_conv_w', 'ssm_conv_b', 'ssm_dt_bias', 'ssm_A_log', 'ssm_D', 'ssm_norm', 'w_out', 'w_up', 'ffn_conv_w', 'ffn_conv_b', 'w_down']
TWIN_DIFF_INPUT = 'x'
TWIN_INPUTS = ['x', 'meta_tokens', 'norm_mix_pre', 'norm_mix_post', 'norm_ffn_pre', 'norm_ffn_post', 'w_in', 'q_a_norm', 'w_uq', 'kv_a_norm', 'w_ukv', 'attn_out_norm', 'ssm_conv_w', 'ssm_conv_b', 'ssm_dt_bias', 'ssm_A_log', 'ssm_D', 'ssm_norm', 'w_out', 'w_up', 'ffn_conv_w', 'ffn_conv_b', 'w_down', 'loss_target', 'm_meta_tokens', 'm_norm_mix_pre', 'm_norm_mix_post', 'm_norm_ffn_pre', 'm_norm_ffn_post', 'm_w_in', 'm_q_a_norm', 'm_w_uq', 'm_kv_a_norm', 'm_w_ukv', 'm_attn_out_norm', 'm_ssm_conv_w', 'm_ssm_conv_b', 'm_ssm_dt_bias', 'm_ssm_A_log', 'm_ssm_D', 'm_ssm_norm', 'm_w_out', 'm_w_up', 'm_ffn_conv_w', 'm_ffn_conv_b', 'm_w_down', 'v_meta_tokens', 'v_norm_mix_pre', 'v_norm_mix_post', 'v_norm_ffn_pre', 'v_norm_ffn_post', 'v_w_in', 'v_q_a_norm', 'v_w_uq', 'v_kv_a_norm', 'v_w_ukv', 'v_attn_out_norm', 'v_ssm_conv_w', 'v_ssm_conv_b', 'v_ssm_dt_bias', 'v_ssm_A_log', 'v_ssm_D', 'v_ssm_norm', 'v_w_out', 'v_w_up', 'v_ffn_conv_w', 'v_ffn_conv_b', 'v_w_down']
TWIN_OUTPUTS = ['loss', 'grad_x', 'grad_meta_tokens', 'grad_norm_mix_pre', 'grad_norm_mix_post', 'grad_norm_ffn_pre', 'grad_norm_ffn_post', 'grad_w_in', 'grad_q_a_norm', 'grad_w_uq', 'grad_kv_a_norm', 'grad_w_ukv', 'grad_attn_out_norm', 'grad_ssm_conv_w', 'grad_ssm_conv_b', 'grad_ssm_dt_bias', 'grad_ssm_A_log', 'grad_ssm_D', 'grad_ssm_norm', 'grad_w_out', 'grad_w_up', 'grad_ffn_conv_w', 'grad_ffn_conv_b', 'grad_w_down', 'delta_meta_tokens', 'delta_norm_mix_pre', 'delta_norm_mix_post', 'delta_norm_ffn_pre', 'delta_norm_ffn_post', 'delta_w_in', 'delta_q_a_norm', 'delta_w_uq', 'delta_kv_a_norm', 'delta_w_ukv', 'delta_attn_out_norm', 'delta_ssm_conv_w', 'delta_ssm_conv_b', 'delta_ssm_dt_bias', 'delta_ssm_A_log', 'delta_ssm_D', 'delta_ssm_norm', 'delta_w_out', 'delta_w_up', 'delta_ffn_conv_w', 'delta_ffn_conv_b', 'delta_w_down', 'new_m_meta_tokens', 'new_m_norm_mix_pre', 'new_m_norm_mix_post', 'new_m_norm_ffn_pre', 'new_m_norm_ffn_post', 'new_m_w_in', 'new_m_q_a_norm', 'new_m_w_uq', 'new_m_kv_a_norm', 'new_m_w_ukv', 'new_m_attn_out_norm', 'new_m_ssm_conv_w', 'new_m_ssm_conv_b', 'new_m_ssm_dt_bias', 'new_m_ssm_A_log', 'new_m_ssm_D', 'new_m_ssm_norm', 'new_m_w_out', 'new_m_w_up', 'new_m_ffn_conv_w', 'new_m_ffn_conv_b', 'new_m_w_down', 'new_v_meta_tokens', 'new_v_norm_mix_pre', 'new_v_norm_mix_post', 'new_v_norm_ffn_pre', 'new_v_norm_ffn_post', 'new_v_w_in', 'new_v_q_a_norm', 'new_v_w_uq', 'new_v_kv_a_norm', 'new_v_w_ukv', 'new_v_attn_out_norm', 'new_v_ssm_conv_w', 'new_v_ssm_conv_b', 'new_v_ssm_dt_bias', 'new_v_ssm_A_log', 'new_v_ssm_D', 'new_v_ssm_norm', 'new_v_w_out', 'new_v_w_up', 'new_v_ffn_conv_w', 'new_v_ffn_conv_b', 'new_v_w_down']
TWIN_LEAF_KINDS = {'loss': 'loss', 'grad_x': 'grad_x', 'grad_meta_tokens': 'grad_w', 'grad_norm_mix_pre': 'grad_w', 'grad_norm_mix_post': 'grad_w', 'grad_norm_ffn_pre': 'grad_w', 'grad_norm_ffn_post': 'grad_w', 'grad_w_in': 'grad_w', 'grad_q_a_norm': 'grad_w', 'grad_w_uq': 'grad_w', 'grad_kv_a_norm': 'grad_w', 'grad_w_ukv': 'grad_w', 'grad_attn_out_norm': 'grad_w', 'grad_ssm_conv_w': 'grad_w', 'grad_ssm_conv_b': 'grad_w', 'grad_ssm_dt_bias': 'grad_w', 'grad_ssm_A_log': 'grad_w', 'grad_ssm_D': 'grad_w', 'grad_ssm_norm': 'grad_w', 'grad_w_out': 'grad_w', 'grad_w_up': 'grad_w', 'grad_ffn_conv_w': 'grad_w', 'grad_ffn_conv_b': 'grad_w', 'grad_w_down': 'grad_w', 'delta_meta_tokens': 'delta_w', 'delta_norm_mix_pre': 'delta_w', 'delta_norm_mix_post': 'delta_w', 'delta_norm_ffn_pre': 'delta_w', 'delta_norm_ffn_post': 'delta_w', 'delta_w_in': 'delta_w', 'delta_q_a_norm': 'delta_w', 'delta_w_uq': 'delta_w', 'delta_kv_a_norm': 'delta_w', 'delta_w_ukv': 'delta_w', 'delta_attn_out_norm': 'delta_w', 'delta_ssm_conv_w': 'delta_w', 'delta_ssm_conv_b': 'delta_w', 'delta_ssm_dt_bias': 'delta_w', 'delta_ssm_A_log': 'delta_w', 'delta_ssm_D': 'delta_w', 'delta_ssm_norm': 'delta_w', 'delta_w_out': 'delta_w', 'delta_w_up': 'delta_w', 'delta_ffn_conv_w': 'delta_w', 'delta_ffn_conv_b': 'delta_w', 'delta_w_down': 'delta_w', 'new_m_meta_tokens': 'new_m', 'new_m_norm_mix_pre': 'new_m', 'new_m_norm_mix_post': 'new_m', 'new_m_norm_ffn_pre': 'new_m', 'new_m_norm_ffn_post': 'new_m', 'new_m_w_in': 'new_m', 'new_m_q_a_norm': 'new_m', 'new_m_w_uq': 'new_m', 'new_m_kv_a_norm': 'new_m', 'new_m_w_ukv': 'new_m', 'new_m_attn_out_norm': 'new_m', 'new_m_ssm_conv_w': 'new_m', 'new_m_ssm_conv_b': 'new_m', 'new_m_ssm_dt_bias': 'new_m', 'new_m_ssm_A_log': 'new_m', 'new_m_ssm_D': 'new_m', 'new_m_ssm_norm': 'new_m', 'new_m_w_out': 'new_m', 'new_m_w_up': 'new_m', 'new_m_ffn_conv_w': 'new_m', 'new_m_ffn_conv_b': 'new_m', 'new_m_w_down': 'new_m', 'new_v_meta_tokens': 'new_v', 'new_v_norm_mix_pre': 'new_v', 'new_v_norm_mix_post': 'new_v', 'new_v_norm_ffn_pre': 'new_v', 'new_v_norm_ffn_post': 'new_v', 'new_v_w_in': 'new_v', 'new_v_q_a_norm': 'new_v', 'new_v_w_uq': 'new_v', 'new_v_kv_a_norm': 'new_v', 'new_v_w_ukv': 'new_v', 'new_v_attn_out_norm': 'new_v', 'new_v_ssm_conv_w': 'new_v', 'new_v_ssm_conv_b': 'new_v', 'new_v_ssm_dt_bias': 'new_v', 'new_v_ssm_A_log': 'new_v', 'new_v_ssm_D': 'new_v', 'new_v_ssm_norm': 'new_v', 'new_v_w_out': 'new_v', 'new_v_w_up': 'new_v', 'new_v_ffn_conv_w': 'new_v', 'new_v_ffn_conv_b': 'new_v', 'new_v_w_down': 'new_v'}


def _forward(args):
    return _fwd_reference(*[args[k] for k in FWD_PARAMS])


def _output_shape():
    def fwd():
        inp = _fwd_setup_inputs(0)
        return _fwd_reference(*[inp[k] for k in FWD_PARAMS])
    out = _jax.eval_shape(fwd)
    return out.shape, out.dtype

N_MICROBATCH = 1
ADAM_LR = 0.001
ADAM_B1 = 0.9
ADAM_B2 = 0.999
ADAM_EPS = 1e-08
ADAM_WD = 0.01
ADAM_STEP = 10
PER_EXAMPLE_BATCH_AXIS = {'x': 0, 'loss_target': 0}
SHARED_INPUTS = []
_WEIGHT_DTYPES = {'meta_tokens': _jnp.float32, 'norm_mix_pre': _jnp.float32, 'norm_mix_post': _jnp.float32, 'norm_ffn_pre': _jnp.float32, 'norm_ffn_post': _jnp.float32, 'w_in': _jnp.float32, 'q_a_norm': _jnp.float32, 'w_uq': _jnp.float32, 'kv_a_norm': _jnp.float32, 'w_ukv': _jnp.float32, 'attn_out_norm': _jnp.float32, 'ssm_conv_w': _jnp.float32, 'ssm_conv_b': _jnp.float32, 'ssm_dt_bias': _jnp.float32, 'ssm_A_log': _jnp.float32, 'ssm_D': _jnp.float32, 'ssm_norm': _jnp.float32, 'w_out': _jnp.float32, 'w_up': _jnp.float32, 'ffn_conv_w': _jnp.float32, 'ffn_conv_b': _jnp.float32, 'w_down': _jnp.float32}
MOMENT_SCALE = {'meta_tokens': 8.162817e-02, 'norm_mix_pre': 1.338995e+00, 'norm_mix_post': 6.384720e+01, 'norm_ffn_pre': 9.113014e-01, 'norm_ffn_post': 6.412429e+01, 'w_in': 7.223229e-01, 'q_a_norm': 1.189949e+00, 'w_uq': 6.143168e-01, 'kv_a_norm': 3.177940e+00, 'w_ukv': 8.095655e-01, 'attn_out_norm': 9.123046e-01, 'ssm_conv_w': 5.006005e-01, 'ssm_conv_b': 1.317852e+00, 'ssm_dt_bias': 8.612069e-01, 'ssm_A_log': 4.003474e+00, 'ssm_D': 4.398383e+00, 'ssm_norm': 7.523357e-01, 'w_out': 1.220171e+00, 'w_up': 3.988278e-01, 'ffn_conv_w': 4.858442e-01, 'ffn_conv_b': 1.798750e+00, 'w_down': 7.817818e-01}


def _to_microbatches(a, axis):
    t = _jnp.moveaxis(a, axis, 0)
    t = t.reshape((N_MICROBATCH, t.shape[0] // N_MICROBATCH) + t.shape[1:])
    return _jnp.moveaxis(t, 1, axis + 1)


def setup_inputs(seed: int = 0) -> dict:
    inp = _fwd_setup_inputs(seed)
    key = _jax.random.fold_in(_jax.random.key(seed), 7919)
    shape, _ = _output_shape()
    out = dict(inp)
    out["loss_target"] = _jax.random.normal(_jax.random.fold_in(key, 0), shape, _jnp.float32)
    for i, name in enumerate(TWIN_WEIGHTS):
        w = inp[name].astype(_jnp.float32)
        if MOMENT_SCALE is None:
            s = _jnp.sqrt(_jnp.mean(_jnp.square(w)) + 1e-30)
        else:
            s = MOMENT_SCALE[name]
        km, kv = _jax.random.split(_jax.random.fold_in(key, i + 1))
        out[name] = w
        out["m_" + name] = s * _jax.random.normal(km, w.shape, _jnp.float32)
        out["v_" + name] = (s * s) * _jax.random.uniform(kv, w.shape, _jnp.float32, 0.5, 1.5)
    if N_MICROBATCH > 1:
        for name, axis in PER_EXAMPLE_BATCH_AXIS.items():
            out[name] = _to_microbatches(out[name], axis)
    return {'x': out['x'], 'meta_tokens': out['meta_tokens'], 'norm_mix_pre': out['norm_mix_pre'], 'norm_mix_post': out['norm_mix_post'], 'norm_ffn_pre': out['norm_ffn_pre'], 'norm_ffn_post': out['norm_ffn_post'], 'w_in': out['w_in'], 'q_a_norm': out['q_a_norm'], 'w_uq': out['w_uq'], 'kv_a_norm': out['kv_a_norm'], 'w_ukv': out['w_ukv'], 'attn_out_norm': out['attn_out_norm'], 'ssm_conv_w': out['ssm_conv_w'], 'ssm_conv_b': out['ssm_conv_b'], 'ssm_dt_bias': out['ssm_dt_bias'], 'ssm_A_log': out['ssm_A_log'], 'ssm_D': out['ssm_D'], 'ssm_norm': out['ssm_norm'], 'w_out': out['w_out'], 'w_up': out['w_up'], 'ffn_conv_w': out['ffn_conv_w'], 'ffn_conv_b': out['ffn_conv_b'], 'w_down': out['w_down'], 'loss_target': out['loss_target'], 'm_meta_tokens': out['m_meta_tokens'], 'm_norm_mix_pre': out['m_norm_mix_pre'], 'm_norm_mix_post': out['m_norm_mix_post'], 'm_norm_ffn_pre': out['m_norm_ffn_pre'], 'm_norm_ffn_post': out['m_norm_ffn_post'], 'm_w_in': out['m_w_in'], 'm_q_a_norm': out['m_q_a_norm'], 'm_w_uq': out['m_w_uq'], 'm_kv_a_norm': out['m_kv_a_norm'], 'm_w_ukv': out['m_w_ukv'], 'm_attn_out_norm': out['m_attn_out_norm'], 'm_ssm_conv_w': out['m_ssm_conv_w'], 'm_ssm_conv_b': out['m_ssm_conv_b'], 'm_ssm_dt_bias': out['m_ssm_dt_bias'], 'm_ssm_A_log': out['m_ssm_A_log'], 'm_ssm_D': out['m_ssm_D'], 'm_ssm_norm': out['m_ssm_norm'], 'm_w_out': out['m_w_out'], 'm_w_up': out['m_w_up'], 'm_ffn_conv_w': out['m_ffn_conv_w'], 'm_ffn_conv_b': out['m_ffn_conv_b'], 'm_w_down': out['m_w_down'], 'v_meta_tokens': out['v_meta_tokens'], 'v_norm_mix_pre': out['v_norm_mix_pre'], 'v_norm_mix_post': out['v_norm_mix_post'], 'v_norm_ffn_pre': out['v_norm_ffn_pre'], 'v_norm_ffn_post': out['v_norm_ffn_post'], 'v_w_in': out['v_w_in'], 'v_q_a_norm': out['v_q_a_norm'], 'v_w_uq': out['v_w_uq'], 'v_kv_a_norm': out['v_kv_a_norm'], 'v_w_ukv': out['v_w_ukv'], 'v_attn_out_norm': out['v_attn_out_norm'], 'v_ssm_conv_w': out['v_ssm_conv_w'], 'v_ssm_conv_b': out['v_ssm_conv_b'], 'v_ssm_dt_bias': out['v_ssm_dt_bias'], 'v_ssm_A_log': out['v_ssm_A_log'], 'v_ssm_D': out['v_ssm_D'], 'v_ssm_norm': out['v_ssm_norm'], 'v_w_out': out['v_w_out'], 'v_w_up': out['v_w_up'], 'v_ffn_conv_w': out['v_ffn_conv_w'], 'v_ffn_conv_b': out['v_ffn_conv_b'], 'v_w_down': out['v_w_down']}


def _loss(weights, diff, rest, loss_target):
    with _jax.named_scope("forward"):
        args = {**rest, TWIN_DIFF_INPUT: diff, **{k: w.astype(_WEIGHT_DTYPES[k]) for k, w in weights.items()}}
        y = _forward(args)
    with _jax.named_scope("loss_head"):
        err = _jnp.square(y.astype(_jnp.float32) - loss_target)
        return 0.5 * _jnp.sum(_jnp.mean(err, axis=-1)) if err.ndim else 0.5 * err


def _adamw(w, g, m, v):
    m = ADAM_B1 * m + (1.0 - ADAM_B1) * g
    v = ADAM_B2 * v + (1.0 - ADAM_B2) * _jnp.square(g)
    m_hat = m / (1.0 - ADAM_B1 ** ADAM_STEP)
    v_hat = v / (1.0 - ADAM_B2 ** ADAM_STEP)
    delta = -ADAM_LR * (m_hat / (_jnp.sqrt(v_hat) + ADAM_EPS) + ADAM_WD * w)
    return delta, m, v


def reference(x, meta_tokens, norm_mix_pre, norm_mix_post, norm_ffn_pre, norm_ffn_post, w_in, q_a_norm, w_uq, kv_a_norm, w_ukv, attn_out_norm, ssm_conv_w, ssm_conv_b, ssm_dt_bias, ssm_A_log, ssm_D, ssm_norm, w_out, w_up, ffn_conv_w, ffn_conv_b, w_down, loss_target, m_meta_tokens, m_norm_mix_pre, m_norm_mix_post, m_norm_ffn_pre, m_norm_ffn_post, m_w_in, m_q_a_norm, m_w_uq, m_kv_a_norm, m_w_ukv, m_attn_out_norm, m_ssm_conv_w, m_ssm_conv_b, m_ssm_dt_bias, m_ssm_A_log, m_ssm_D, m_ssm_norm, m_w_out, m_w_up, m_ffn_conv_w, m_ffn_conv_b, m_w_down, v_meta_tokens, v_norm_mix_pre, v_norm_mix_post, v_norm_ffn_pre, v_norm_ffn_post, v_w_in, v_q_a_norm, v_w_uq, v_kv_a_norm, v_w_ukv, v_attn_out_norm, v_ssm_conv_w, v_ssm_conv_b, v_ssm_dt_bias, v_ssm_A_log, v_ssm_D, v_ssm_norm, v_w_out, v_w_up, v_ffn_conv_w, v_ffn_conv_b, v_w_down):
    given = dict(x=x, meta_tokens=meta_tokens, norm_mix_pre=norm_mix_pre, norm_mix_post=norm_mix_post, norm_ffn_pre=norm_ffn_pre, norm_ffn_post=norm_ffn_post, w_in=w_in, q_a_norm=q_a_norm, w_uq=w_uq, kv_a_norm=kv_a_norm, w_ukv=w_ukv, attn_out_norm=attn_out_norm, ssm_conv_w=ssm_conv_w, ssm_conv_b=ssm_conv_b, ssm_dt_bias=ssm_dt_bias, ssm_A_log=ssm_A_log, ssm_D=ssm_D, ssm_norm=ssm_norm, w_out=w_out, w_up=w_up, ffn_conv_w=ffn_conv_w, ffn_conv_b=ffn_conv_b, w_down=w_down, loss_target=loss_target, m_meta_tokens=m_meta_tokens, m_norm_mix_pre=m_norm_mix_pre, m_norm_mix_post=m_norm_mix_post, m_norm_ffn_pre=m_norm_ffn_pre, m_norm_ffn_post=m_norm_ffn_post, m_w_in=m_w_in, m_q_a_norm=m_q_a_norm, m_w_uq=m_w_uq, m_kv_a_norm=m_kv_a_norm, m_w_ukv=m_w_ukv, m_attn_out_norm=m_attn_out_norm, m_ssm_conv_w=m_ssm_conv_w, m_ssm_conv_b=m_ssm_conv_b, m_ssm_dt_bias=m_ssm_dt_bias, m_ssm_A_log=m_ssm_A_log, m_ssm_D=m_ssm_D, m_ssm_norm=m_ssm_norm, m_w_out=m_w_out, m_w_up=m_w_up, m_ffn_conv_w=m_ffn_conv_w, m_ffn_conv_b=m_ffn_conv_b, m_w_down=m_w_down, v_meta_tokens=v_meta_tokens, v_norm_mix_pre=v_norm_mix_pre, v_norm_mix_post=v_norm_mix_post, v_norm_ffn_pre=v_norm_ffn_pre, v_norm_ffn_post=v_norm_ffn_post, v_w_in=v_w_in, v_q_a_norm=v_q_a_norm, v_w_uq=v_w_uq, v_kv_a_norm=v_kv_a_norm, v_w_ukv=v_w_ukv, v_attn_out_norm=v_attn_out_norm, v_ssm_conv_w=v_ssm_conv_w, v_ssm_conv_b=v_ssm_conv_b, v_ssm_dt_bias=v_ssm_dt_bias, v_ssm_A_log=v_ssm_A_log, v_ssm_D=v_ssm_D, v_ssm_norm=v_ssm_norm, v_w_out=v_w_out, v_w_up=v_w_up, v_ffn_conv_w=v_ffn_conv_w, v_ffn_conv_b=v_ffn_conv_b, v_w_down=v_w_down)
    weights = {n: given[n] for n in TWIN_WEIGHTS}
    shared = {n: given[n] for n in SHARED_INPUTS}
    per_example = {n: given[n] for n in ['x']}
    grad_fn = _jax.value_and_grad(_loss, argnums=(0, 1))

    def one_microbatch(ex, loss_target):
        ex = dict(ex)
        diff = ex.pop(TWIN_DIFF_INPUT)
        return grad_fn(weights, diff, {**shared, **ex}, loss_target)

    if N_MICROBATCH == 1:
        loss, (grad_w, grad_x) = one_microbatch(per_example, given["loss_target"])
    else:
        def body(carry, xs):
            loss_sum, grad_sum = carry
            l_k, (gw_k, gx_k) = one_microbatch(xs[0], xs[1])
            with _jax.named_scope("update"):
                return (loss_sum + l_k, _jax.tree.map(_jnp.add, grad_sum, gw_k)), gx_k

        init = (_jnp.zeros((), _jnp.float32), _jax.tree.map(_jnp.zeros_like, weights))
        (loss, grad_w), grad_x = _jax.lax.scan(body, init, (per_example, given["loss_target"]))
    with _jax.named_scope("update"):
        delta_w, new_m, new_v = {}, {}, {}
        for n in TWIN_WEIGHTS:
            delta_w[n], new_m[n], new_v[n] = _adamw(weights[n], grad_w[n], given["m_" + n], given["v_" + n])
    return (loss, grad_x, *[grad_w[n] for n in TWIN_WEIGHTS], *[delta_w[n] for n in TWIN_WEIGHTS],
            *[new_m[n] for n in TWIN_WEIGHTS], *[new_v[n] for n in TWIN_WEIGHTS])
```

```python
import functools
import math

import jax
import jax.numpy as jnp
import numpy as np
from jax import lax
from jax.experimental import pallas as pl
from jax.experimental.pallas import tpu as pltpu

f32 = jnp.float32
bf16 = jnp.bfloat16

D_MODEL = 1024
SEQ = 8192
N_META = 16
MLA_HEADS = 8
QK_NOPE = 128
QK_ROPE = 64
V_DIM = 128
Q_RANK = 384
KV_RANK = 256
ROPE_THETA = 10000.0
SOFTMAX_SCALE = (QK_NOPE + QK_ROPE) ** -0.5
D_ATTN = MLA_HEADS * V_DIM
SSM_HEADS = 16
SSM_P = 64
SSM_GROUPS = 2
SSM_HPG = SSM_HEADS // SSM_GROUPS
SSM_N = 128
SSM_CONV = 4
CHUNK = 128
D_SSM = SSM_HEADS * SSM_P
D_BC = SSM_GROUPS * SSM_N
D_XBC = D_SSM + 2 * D_BC
D_FF = 2816
FFN_CONV = 3
EPS = 1e-6
D_IN = Q_RANK + KV_RANK + QK_ROPE + D_SSM + D_XBC + SSM_HEADS
QK_PAD = 256
N_DEV = 8

ADAM_LR = 0.001
ADAM_B1 = 0.9
ADAM_B2 = 0.999
ADAM_EPS = 1e-08
ADAM_WD = 0.01
ADAM_STEP = 10

LANES = 128
SUBLANES = 8
ROW_TILE = 256
VMEM_LIMIT = 56 * 1024 * 1024
PACK_W = 1024
PACK_ROW_TILE = 128
NEG = -1e30

_MESH = pl.DeviceIdType.MESH


def _pick(n, prefs):
    for p in prefs:
        if n % p == 0:
            return p
    return n


def _cparams(sem):
    return pltpu.CompilerParams(dimension_semantics=sem, vmem_limit_bytes=VMEM_LIMIT)


def _row(spec_cols, tm):
    return pl.BlockSpec((tm, spec_cols), lambda i: (i, 0))


def _full(shape):
    nd = len(shape)
    return pl.BlockSpec(shape, lambda *a: (0,) * nd)


def _sigmoid(x):
    return 1.0 / (1.0 + jnp.exp(-x))


def _silu(x):
    return x * _sigmoid(x)


def _dsilu(x):
    s = _sigmoid(x)
    return s * (1.0 + x * (1.0 - s))


def _dot(a, b):
    return jnp.dot(a, b, preferred_element_type=f32)


def _dot_nt(a, b):
    return lax.dot_general(a, b, (((1,), (1,)), ((), ())), preferred_element_type=f32)


def _dot_tn(a, b):
    return lax.dot_general(a, b, (((0,), (0,)), ((), ())), preferred_element_type=f32)


def _dot_hi(a, b):
    return jnp.dot(a, b, precision=lax.Precision.HIGHEST, preferred_element_type=f32)


def _mm(pairs, out_dtype, trans_b, name):
    n = len(pairs)
    M = pairs[0][0].shape[0]
    N = pairs[0][1].shape[0] if trans_b else pairs[0][1].shape[1]
    tm = _pick(M, (768, 512, 256))
    tn = _pick(N, (512, 384, 256, 128))

    def body(*refs):
        o_ref = refs[2 * n]
        acc = None
        for p in range(n):
            a = refs[2 * p][...].astype(bf16)
            b = refs[2 * p + 1][...].astype(bf16)
            r = _dot_nt(a, b) if trans_b else _dot(a, b)
            acc = r if acc is None else acc + r
        o_ref[...] = acc.astype(out_dtype)

    in_specs, args = [], []
    for a, b in pairs:
        k = a.shape[1]
        in_specs.append(pl.BlockSpec((tm, k), lambda i, j: (i, 0)))
        if trans_b:
            in_specs.append(pl.BlockSpec((tn, k), lambda i, j: (j, 0)))
        else:
            in_specs.append(pl.BlockSpec((k, tn), lambda i, j: (0, j)))
        args += [a, b]
    return pl.pallas_call(
        body, name=name, grid=(M // tm, N // tn), in_specs=in_specs,
        out_specs=pl.BlockSpec((tm, tn), lambda i, j: (i, j)),
        out_shape=jax.ShapeDtypeStruct((M, N), out_dtype),
        compiler_params=_cparams(("parallel", "parallel")),
    )(*args)


def _mm_tn(a, g, name):
    M, K = a.shape
    N = g.shape[1]
    tm = _pick(M, (768, 512, 256))
    tk = _pick(K, (1024, 1408, 512, 384, 256))
    tn = _pick(N, (1024, 1408, 512, 384, 256, 128))

    def body(a_ref, g_ref, o_ref):
        @pl.when(pl.program_id(2) == 0)
        def _():
            o_ref[...] = jnp.zeros_like(o_ref)

        o_ref[...] += _dot_tn(a_ref[...].astype(bf16), g_ref[...].astype(bf16))

    return pl.pallas_call(
        body, name=name, grid=(K // tk, N // tn, M // tm),
        in_specs=[pl.BlockSpec((tm, tk), lambda k, j, m: (m, k)),
                  pl.BlockSpec((tm, tn), lambda k, j, m: (m, j))],
        out_specs=pl.BlockSpec((tk, tn), lambda k, j, m: (k, j)),
        out_shape=jax.ShapeDtypeStruct((K, N), f32),
        compiler_params=_cparams(("parallel", "parallel", "arbitrary")),
    )(a, g)


def _rstd(x):
    return lax.rsqrt(jnp.mean(x * x, axis=-1, keepdims=True) + EPS)


def _rms_bwd_math(x, g, dy):
    r = _rstd(x)
    xh = x * r
    dn = dy * g
    dx = r * (dn - xh * jnp.mean(dn * xh, axis=-1, keepdims=True))
    return dx, dy * xh


def _rms_fwd(x, g, out_dtype, name):
    M, K = x.shape
    tm = ROW_TILE

    def body(x_ref, g_ref, o_ref):
        xv = x_ref[...]
        o_ref[...] = (xv * _rstd(xv) * g_ref[...]).astype(out_dtype)

    return pl.pallas_call(
        body, name=name, grid=(M // tm,), in_specs=[_row(K, tm), _full((1, K))],
        out_specs=_row(K, tm), out_shape=jax.ShapeDtypeStruct((M, K), out_dtype),
        compiler_params=_cparams(("parallel",)),
    )(x, g)


def _rms_bwd(x, g, dy, out_dtype, name, residual=None):
    M, K = x.shape
    tm = ROW_TILE
    has_res = residual is not None

    def body(*refs):
        if has_res:
            x_ref, g_ref, dy_ref, r_ref, dx_ref, dg_ref = refs
        else:
            x_ref, g_ref, dy_ref, dx_ref, dg_ref = refs

        @pl.when(pl.program_id(0) == 0)
        def _():
            dg_ref[...] = jnp.zeros_like(dg_ref)

        dx, dgp = _rms_bwd_math(x_ref[...], g_ref[...], dy_ref[...].astype(f32))
        if has_res:
            dx = dx + r_ref[...]
        dx_ref[...] = dx.astype(out_dtype)
        dg_ref[...] += jnp.sum(dgp, axis=0, keepdims=True)

    ins = [x, g, dy] + ([residual] if has_res else [])
    in_specs = [_row(K, tm), _full((1, K)), _row(K, tm)] + ([_row(K, tm)] if has_res else [])
    return pl.pallas_call(
        body, name=name, grid=(M // tm,), in_specs=in_specs,
        out_specs=[_row(K, tm), _full((1, K))],
        out_shape=[jax.ShapeDtypeStruct((M, K), out_dtype), jax.ShapeDtypeStruct((1, K), f32)],
        compiler_params=_cparams(("arbitrary",)),
    )(*ins)


def _resid_norm(h0, mix, g2, g3):
    M, K = h0.shape
    tm = ROW_TILE

    def body(h_ref, m_ref, g2_ref, g3_ref, h1_ref, hn_ref):
        mv = m_ref[...]
        h1 = h_ref[...] + mv * _rstd(mv) * g2_ref[...]
        h1_ref[...] = h1
        hn_ref[...] = (h1 * _rstd(h1) * g3_ref[...]).astype(bf16)

    return pl.pallas_call(
        body, name="resid_norm", grid=(M // tm,),
        in_specs=[_row(K, tm), _row(K, tm), _full((1, K)), _full((1, K))],
        out_specs=[_row(K, tm), _row(K, tm)],
        out_shape=[jax.ShapeDtypeStruct((M, K), f32), jax.ShapeDtypeStruct((M, K), bf16)],
        compiler_params=_cparams(("parallel",)),
    )(h0, mix, g2, g3)


def _final(h1, down, g4, tgt, n_real):
    M, K = h1.shape
    tm = ROW_TILE
    nt = M // tm

    def body(h_ref, d_ref, g_ref, t_ref, dh_ref, dd_ref, dg_ref, ls_ref, acc_ref):
        i = pl.program_id(0)

        @pl.when(i == 0)
        def _():
            dg_ref[...] = jnp.zeros_like(dg_ref)
            acc_ref[...] = jnp.zeros_like(acc_ref)

        dv = d_ref[...]
        g = g_ref[...]
        r = _rstd(dv)
        n = dv * r
        h2 = h_ref[...] + n * g
        rows = i * tm + lax.broadcasted_iota(jnp.int32, (tm, 1), 0)
        mask = ((rows >= N_META) & (rows < n_real)).astype(f32)
        diff = (h2 - t_ref[...]) * mask
        acc_ref[...] += jnp.sum(diff * diff, axis=0, keepdims=True)
        dh = diff * (1.0 / K)
        dh_ref[...] = dh
        dn = dh * g
        dd_ref[...] = (r * (dn - n * jnp.mean(dn * n, axis=-1, keepdims=True))).astype(bf16)
        dg_ref[...] += jnp.sum(dh * n, axis=0, keepdims=True)

        @pl.when(i == nt - 1)
        def _():
            ls_ref[...] = jnp.zeros((1, LANES), f32) + jnp.sum(acc_ref[...]) * (0.5 / K)

    return pl.pallas_call(
        body, name="final_loss", grid=(nt,),
        in_specs=[_row(K, tm), _row(K, tm), _full((1, K)), _row(K, tm)],
        out_specs=[_row(K, tm), _row(K, tm), _full((1, K)), _full((1, LANES))],
        out_shape=[jax.ShapeDtypeStruct((M, K), f32), jax.ShapeDtypeStruct((M, K), bf16),
                   jax.ShapeDtypeStruct((1, K), f32), jax.ShapeDtypeStruct((1, LANES), f32)],
        scratch_shapes=[pltpu.VMEM((1, K), f32)],
        compiler_params=_cparams(("arbitrary",)),
    )(h1, down, g4, tgt)


def _mid_bwd(h1, g3, d_hn2, dh2, mix, g2):
    M, K = h1.shape
    tm = ROW_TILE

    def body(h_ref, g3_ref, dn_ref, dh2_ref, m_ref, g2_ref, dh1_ref, dm_ref, dg3_ref, dg2_ref):
        @pl.when(pl.program_id(0) == 0)
        def _():
            dg3_ref[...] = jnp.zeros_like(dg3_ref)
            dg2_ref[...] = jnp.zeros_like(dg2_ref)

        dx, dgp = _rms_bwd_math(h_ref[...], g3_ref[...], dn_ref[...])
        dh1 = dh2_ref[...] + dx
        dh1_ref[...] = dh1
        dg3_ref[...] += jnp.sum(dgp, axis=0, keepdims=True)
        dm, dgp2 = _rms_bwd_math(m_ref[...], g2_ref[...], dh1)
        dm_ref[...] = dm.astype(bf16)
        dg2_ref[...] += jnp.sum(dgp2, axis=0, keepdims=True)

    return pl.pallas_call(
        body, name="mid_bwd", grid=(M // tm,),
        in_specs=[_row(K, tm), _full((1, K)), _row(K, tm), _row(K, tm), _row(K, tm), _full((1, K))],
        out_specs=[_row(K, tm), _row(K, tm), _full((1, K)), _full((1, K))],
        out_shape=[jax.ShapeDtypeStruct((M, K), f32), jax.ShapeDtypeStruct((M, K), bf16),
                   jax.ShapeDtypeStruct((1, K), f32), jax.ShapeDtypeStruct((1, K), f32)],
        compiler_params=_cparams(("arbitrary",)),
    )(h1, g3, d_hn2, dh2, mix, g2)


def _conv_taps(ext_ref, w_ref, kw, tm, first):
    u = None
    for k in range(kw):
        t = ext_ref[pl.ds(first + k, tm), :] * w_ref[k:k + 1, :]
        u = t if u is None else u + t
    return u


def _fill_prev(ext_ref, x_ref, halo_ref, i, tm):
    ext_ref[0:SUBLANES, :] = jnp.where(i == 0, 0.0, halo_ref[...])
    ext_ref[SUBLANES:SUBLANES + tm, :] = x_ref[...]


def _prev_spec(tm, tc, col_of, row_axis):
    def imap(*ids):
        i = ids[row_axis]
        return (jnp.maximum(i * (tm // SUBLANES) - 1, 0), col_of(*ids))
    return pl.BlockSpec((SUBLANES, tc), imap)


def _ssm_conv_fwd(xbc, w, b):
    M, C = xbc.shape
    tm, tc, kw = ROW_TILE, 512, SSM_CONV

    def body(x_ref, h_ref, w_ref, b_ref, o_ref, ext_ref):
        _fill_prev(ext_ref, x_ref, h_ref, pl.program_id(0), tm)
        u = _conv_taps(ext_ref, w_ref, kw, tm, SUBLANES - (kw - 1)) + b_ref[...]
        o_ref[...] = _silu(u)

    return pl.pallas_call(
        body, name="ssm_conv_fwd", grid=(M // tm, C // tc),
        in_specs=[pl.BlockSpec((tm, tc), lambda i, j: (i, j)),
                  _prev_spec(tm, tc, lambda i, j: j, 0),
                  pl.BlockSpec((SUBLANES, tc), lambda i, j: (0, j)),
                  pl.BlockSpec((1, tc), lambda i, j: (0, j))],
        out_specs=pl.BlockSpec((tm, tc), lambda i, j: (i, j)),
        out_shape=jax.ShapeDtypeStruct((M, C), f32),
        scratch_shapes=[pltpu.VMEM((tm + SUBLANES, tc), f32)],
        compiler_params=_cparams(("parallel", "parallel")),
    )(xbc, xbc, w, b)


def _ssm_conv_bwd(xbc, w, b, dout):
    M, C = xbc.shape
    tm, tc, kw = ROW_TILE, 512, SSM_CONV

    def body(x_ref, h_ref, w_ref, b_ref, d_ref, du_ref, dw_ref, db_ref, ext_ref):
        i = pl.program_id(1)

        @pl.when(i == 0)
        def _():
            dw_ref[...] = jnp.zeros_like(dw_ref)
            db_ref[...] = jnp.zeros_like(db_ref)

        _fill_prev(ext_ref, x_ref, h_ref, i, tm)
        first = SUBLANES - (kw - 1)
        u = _conv_taps(ext_ref, w_ref, kw, tm, first) + b_ref[...]
        du = d_ref[...] * _dsilu(u)
        du_ref[...] = du
        db_ref[...] += jnp.sum(du, axis=0, keepdims=True)
        for k in range(kw):
            dw_ref[k:k + 1, :] += jnp.sum(du * ext_ref[pl.ds(first + k, tm), :], axis=0, keepdims=True)

    return pl.pallas_call(
        body, name="ssm_conv_bwd", grid=(C // tc, M // tm),
        in_specs=[pl.BlockSpec((tm, tc), lambda j, i: (i, j)),
                  _prev_spec(tm, tc, lambda j, i: j, 1),
                  pl.BlockSpec((SUBLANES, tc), lambda j, i: (0, j)),
                  pl.BlockSpec((1, tc), lambda j, i: (0, j)),
                  pl.BlockSpec((tm, tc), lambda j, i: (i, j))],
        out_specs=[pl.BlockSpec((tm, tc), lambda j, i: (i, j)),
                   pl.BlockSpec((SUBLANES, tc), lambda j, i: (0, j)),
                   pl.BlockSpec((1, tc), lambda j, i: (0, j))],
        out_shape=[jax.ShapeDtypeStruct((M, C), f32), jax.ShapeDtypeStruct((SUBLANES, C), f32),
                   jax.ShapeDtypeStruct((1, C), f32)],
        scratch_shapes=[pltpu.VMEM((tm + SUBLANES, tc), f32)],
        compiler_params=_cparams(("parallel", "arbitrary")),
    )(xbc, xbc, w, b, dout)


def _ffn_gate_fwd(up, w, b):
    M = up.shape[0]
    tm, tc, kw = ROW_TILE, 256, FFN_CONV
    nc = D_FF // tc

    def body(xg_ref, hg_ref, xv_ref, hv_ref, wg_ref, wv_ref, bg_ref, bv_ref, o_ref, eg_ref, ev_ref):
        i = pl.program_id(0)
        first = SUBLANES - (kw - 1)
        _fill_prev(eg_ref, xg_ref, hg_ref, i, tm)
        _fill_prev(ev_ref, xv_ref, hv_ref, i, tm)
        ug = _conv_taps(eg_ref, wg_ref, kw, tm, first) + bg_ref[...]
        uv = _conv_taps(ev_ref, wv_ref, kw, tm, first) + bv_ref[...]
        o_ref[...] = (_silu(ug) * uv).astype(bf16)

    return pl.pallas_call(
        body, name="ffn_gate_fwd", grid=(M // tm, nc),
        in_specs=[pl.BlockSpec((tm, tc), lambda i, j: (i, j)),
                  _prev_spec(tm, tc, lambda i, j: j, 0),
                  pl.BlockSpec((tm, tc), lambda i, j: (i, j + nc)),
                  _prev_spec(tm, tc, lambda i, j: j + nc, 0),
                  pl.BlockSpec((SUBLANES, tc), lambda i, j: (0, j)),
                  pl.BlockSpec((SUBLANES, tc), lambda i, j: (0, j + nc)),
                  pl.BlockSpec((1, tc), lambda i, j: (0, j)),
                  pl.BlockSpec((1, tc), lambda i, j: (0, j + nc))],
        out_specs=pl.BlockSpec((tm, tc), lambda i, j: (i, j)),
        out_shape=jax.ShapeDtypeStruct((M, D_FF), bf16),
        scratch_shapes=[pltpu.VMEM((tm + SUBLANES, tc), f32), pltpu.VMEM((tm + SUBLANES, tc), f32)],
        compiler_params=_cparams(("parallel", "parallel")),
    )(up, up, up, up, w, w, b, b)


def _ffn_gate_bwd(up, w, b, d_act):
    M = up.shape[0]
    tm, tc, kw = ROW_TILE, 256, FFN_CONV
    nc = D_FF // tc

    def body(xg_ref, hg_ref, xv_ref, hv_ref, wg_ref, wv_ref, bg_ref, bv_ref, d_ref,
             dug_ref, duv_ref, dwg_ref, dwv_ref, dbg_ref, dbv_ref, eg_ref, ev_ref):
        i = pl.program_id(1)

        @pl.when(i == 0)
        def _():
            for r in (dwg_ref, dwv_ref, dbg_ref, dbv_ref):
                r[...] = jnp.zeros_like(r)

        first = SUBLANES - (kw - 1)
        _fill_prev(eg_ref, xg_ref, hg_ref, i, tm)
        _fill_prev(ev_ref, xv_ref, hv_ref, i, tm)
        ug = _conv_taps(eg_ref, wg_ref, kw, tm, first) + bg_ref[...]
        uv = _conv_taps(ev_ref, wv_ref, kw, tm, first) + bv_ref[...]
        da = d_ref[...]
        dug = da * uv * _dsilu(ug)
        duv = da * _silu(ug)
        dug_ref[...] = dug
        duv_ref[...] = duv
        dbg_ref[...] += jnp.sum(dug, axis=0, keepdims=True)
        dbv_ref[...] += jnp.sum(duv, axis=0, keepdims=True)
        for k in range(kw):
            dwg_ref[k:k + 1, :] += jnp.sum(dug * eg_ref[pl.ds(first + k, tm), :], axis=0, keepdims=True)
            dwv_ref[k:k + 1, :] += jnp.sum(duv * ev_ref[pl.ds(first + k, tm), :], axis=0, keepdims=True)

    C = 2 * D_FF
    outs = pl.pallas_call(
        body, name="ffn_gate_bwd", grid=(nc, M // tm),
        in_specs=[pl.BlockSpec((tm, tc), lambda j, i: (i, j)),
                  _prev_spec(tm, tc, lambda j, i: j, 1),
                  pl.BlockSpec((tm, tc), lambda j, i: (i, j + nc)),
                  _prev_spec(tm, tc, lambda j, i: j + nc, 1),
                  pl.BlockSpec((SUBLANES, tc), lambda j, i: (0, j)),
                  pl.BlockSpec((SUBLANES, tc), lambda j, i: (0, j + nc)),
                  pl.BlockSpec((1, tc), lambda j, i: (0, j)),
                  pl.BlockSpec((1, tc), lambda j, i: (0, j + nc)),
                  pl.BlockSpec((tm, tc), lambda j, i: (i, j))],
        out_specs=[pl.BlockSpec((tm, tc), lambda j, i: (i, j)),
                   pl.BlockSpec((tm, tc), lambda j, i: (i, j)),
                   pl.BlockSpec((SUBLANES, tc), lambda j, i: (0, j)),
                   pl.BlockSpec((SUBLANES, tc), lambda j, i: (0, j)),
                   pl.BlockSpec((1, tc), lambda j, i: (0, j)),
                   pl.BlockSpec((1, tc), lambda j, i: (0, j))],
        out_shape=[jax.ShapeDtypeStruct((M, D_FF), f32), jax.ShapeDtypeStruct((M, D_FF), f32),
                   jax.ShapeDtypeStruct((SUBLANES, D_FF), f32), jax.ShapeDtypeStruct((SUBLANES, D_FF), f32),
                   jax.ShapeDtypeStruct((1, D_FF), f32), jax.ShapeDtypeStruct((1, D_FF), f32)],
        scratch_shapes=[pltpu.VMEM((tm + SUBLANES, tc), f32), pltpu.VMEM((tm + SUBLANES, tc), f32)],
        compiler_params=_cparams(("parallel", "arbitrary")),
    )(up, up, up, up, w, w, b, b, d_act)
    return outs


def _conv_bwd_dx(dus, w, kw, name):
    M = dus[0].shape[0]
    widths = [d.shape[1] for d in dus]
    C = sum(widths)
    tm = ROW_TILE
    tc = _pick(widths[0], (512, 256))
    per = [wd // tc for wd in widths]
    nrb = M // SUBLANES
    nt = M // tm
    outs = []
    col0 = 0
    for s, du in enumerate(dus):
        cb0 = col0 // tc

        def body(x_ref, h_ref, w_ref, o_ref, ext_ref):
            i = pl.program_id(0)
            ext_ref[0:tm, :] = x_ref[...]
            ext_ref[tm:tm + SUBLANES, :] = jnp.where(i == nt - 1, 0.0, h_ref[...])
            acc = None
            for k in range(kw):
                t = ext_ref[pl.ds(kw - 1 - k, tm), :] * w_ref[k:k + 1, :]
                acc = t if acc is None else acc + t
            o_ref[...] = acc.astype(bf16)

        outs.append(pl.pallas_call(
            body, name=f"{name}_{s}", grid=(nt, per[s]),
            in_specs=[pl.BlockSpec((tm, tc), lambda i, j: (i, j)),
                      pl.BlockSpec((SUBLANES, tc),
                                   lambda i, j: (jnp.minimum((i + 1) * (tm // SUBLANES), nrb - 1), j)),
                      pl.BlockSpec((SUBLANES, tc), functools.partial(lambda i, j, c: (0, j + c), c=cb0))],
            out_specs=pl.BlockSpec((tm, tc), lambda i, j: (i, j)),
            out_shape=jax.ShapeDtypeStruct((M, widths[s]), bf16),
            scratch_shapes=[pltpu.VMEM((tm + SUBLANES, tc), f32)],
            compiler_params=_cparams(("parallel", "parallel")),
        )(du, du, w))
        col0 += widths[s]
    return outs


def _rope_apply(blk, cos, sin):
    lane = lax.broadcasted_iota(jnp.int32, blk.shape, 1)
    half = QK_ROPE // 2
    partner = jnp.where(lane < half, pltpu.roll(blk, LANES - half, 1), pltpu.roll(blk, half, 1))
    return blk * cos + partner * sin


def _rope_unapply(d, cos, sin):
    t = d * sin
    lane = lax.broadcasted_iota(jnp.int32, d.shape, 1)
    half = QK_ROPE // 2
    partner = jnp.where(lane < half, pltpu.roll(t, LANES - half, 1), pltpu.roll(t, half, 1))
    return d * cos + partner


def _rope_q_fwd(q_full, cos, sin):
    M = q_full.shape[0]
    tm = ROW_TILE

    def body(q_ref, c_ref, s_ref, o_ref):
        qv = q_ref[...]
        o_ref[0, :, 0:QK_NOPE] = qv[:, 0:QK_NOPE].astype(bf16)
        o_ref[0, :, QK_NOPE:QK_PAD] = _rope_apply(qv[:, QK_NOPE:QK_PAD], c_ref[...], s_ref[...]).astype(bf16)

    return pl.pallas_call(
        body, name="rope_q_fwd", grid=(M // tm, MLA_HEADS),
        in_specs=[pl.BlockSpec((tm, QK_PAD), lambda i, h: (i, h)),
                  pl.BlockSpec((tm, LANES), lambda i, h: (i, 0)),
                  pl.BlockSpec((tm, LANES), lambda i, h: (i, 0))],
        out_specs=pl.BlockSpec((1, tm, QK_PAD), lambda i, h: (h, i, 0)),
        out_shape=jax.ShapeDtypeStruct((MLA_HEADS, M, QK_PAD), bf16),
        compiler_params=_cparams(("parallel", "parallel")),
    )(q_full, cos, sin)


def _rope_k_fwd(kv_full, kpe_raw, cos, sin):
    M = kv_full.shape[0]
    tm = ROW_TILE

    def body(kv_ref, pe_ref, c_ref, s_ref, k_ref, v_ref):
        kv = kv_ref[...]
        k_ref[0, :, 0:QK_NOPE] = kv[:, 0:QK_NOPE].astype(bf16)
        k_ref[0, :, QK_NOPE:QK_PAD] = _rope_apply(pe_ref[...], c_ref[...], s_ref[...]).astype(bf16)
        v_ref[0] = kv[:, QK_NOPE:QK_NOPE + V_DIM].astype(bf16)

    return pl.pallas_call(
        body, name="rope_k_fwd", grid=(M // tm, MLA_HEADS),
        in_specs=[pl.BlockSpec((tm, QK_NOPE + V_DIM), lambda i, h: (i, h)),
                  pl.BlockSpec((tm, LANES), lambda i, h: (i, 0)),
                  pl.BlockSpec((tm, LANES), lambda i, h: (i, 0)),
                  pl.BlockSpec((tm, LANES), lambda i, h: (i, 0))],
        out_specs=[pl.BlockSpec((1, tm, QK_PAD), lambda i, h: (h, i, 0)),
                   pl.BlockSpec((1, tm, V_DIM), lambda i, h: (h, i, 0))],
        out_shape=[jax.ShapeDtypeStruct((MLA_HEADS, M, QK_PAD), bf16),
                   jax.ShapeDtypeStruct((MLA_HEADS, M, V_DIM), bf16)],
        compiler_params=_cparams(("parallel", "parallel")),
    )(kv_full, kpe_raw, cos, sin)


def _rope_q_bwd(dq, cos, sin):
    M = dq.shape[1]
    tm = ROW_TILE

    def body(d_ref, c_ref, s_ref, o_ref):
        dv = d_ref[0]
        o_ref[:, 0:QK_NOPE] = dv[:, 0:QK_NOPE].astype(bf16)
        o_ref[:, QK_NOPE:QK_PAD] = _rope_unapply(dv[:, QK_NOPE:QK_PAD], c_ref[...], s_ref[...]).astype(bf16)

    return pl.pallas_call(
        body, name="rope_q_bwd", grid=(M // tm, MLA_HEADS),
        in_specs=[pl.BlockSpec((1, tm, QK_PAD), lambda i, h: (h, i, 0)),
                  pl.BlockSpec((tm, LANES), lambda i, h: (i, 0)),
                  pl.BlockSpec((tm, LANES), lambda i, h: (i, 0))],
        out_specs=pl.BlockSpec((tm, QK_PAD), lambda i, h: (i, h)),
        out_shape=jax.ShapeDtypeStruct((M, MLA_HEADS * QK_PAD), bf16),
        compiler_params=_cparams(("parallel", "parallel")),
    )(dq, cos, sin)


def _rope_k_bwd(dk, dv, cos, sin):
    M = dk.shape[1]
    tm = ROW_TILE

    def body(dk_ref, dv_ref, c_ref, s_ref, o_ref, pe_ref):
        h = pl.program_id(1)
        d = dk_ref[0]
        o_ref[:, 0:QK_NOPE] = d[:, 0:QK_NOPE].astype(bf16)
        o_ref[:, QK_NOPE:QK_NOPE + V_DIM] = dv_ref[0].astype(bf16)

        @pl.when(h == 0)
        def _():
            pe_ref[...] = jnp.zeros_like(pe_ref)

        pe_ref[...] += d[:, QK_NOPE:QK_PAD]

        @pl.when(h == MLA_HEADS - 1)
        def _():
            pe_ref[...] = _rope_unapply(pe_ref[...], c_ref[...], s_ref[...])

    return pl.pallas_call(
        body, name="rope_k_bwd", grid=(M // tm, MLA_HEADS),
        in_specs=[pl.BlockSpec((1, tm, QK_PAD), lambda i, h: (h, i, 0)),
                  pl.BlockSpec((1, tm, V_DIM), lambda i, h: (h, i, 0)),
                  pl.BlockSpec((tm, LANES), lambda i, h: (i, 0)),
                  pl.BlockSpec((tm, LANES), lambda i, h: (i, 0))],
        out_specs=[pl.BlockSpec((tm, QK_NOPE + V_DIM), lambda i, h: (i, h)),
                   pl.BlockSpec((tm, LANES), lambda i, h: (i, 0))],
        out_shape=[jax.ShapeDtypeStruct((M, MLA_HEADS * (QK_NOPE + V_DIM)), bf16),
                   jax.ShapeDtypeStruct((M, LANES), f32)],
        compiler_params=_cparams(("parallel", "arbitrary")),
    )(dk, dv, cos, sin)


def _attn_tile(M):
    return _pick(M, (768, 512, 256))


def _col_to_row(col):
    return col.T[0:1, :]


def _flash_fwd(q, k, v):
    H, M, _ = q.shape
    T = _attn_tile(M)
    nq = M // T

    def body(q_ref, k_ref, v_ref, o_ref, lse_ref, m_sc, l_sc, acc_sc):
        i = pl.program_id(1)
        qv = q_ref[0]
        m_sc[...] = jnp.full_like(m_sc, NEG)
        l_sc[...] = jnp.zeros_like(l_sc)
        acc_sc[...] = jnp.zeros_like(acc_sc)

        def step(j, masked):
            off = pl.multiple_of(j * T, T)
            kt = k_ref[0, pl.ds(off, T), :]
            vt = v_ref[0, pl.ds(off, T), :]
            s = _dot_nt(qv, kt) * SOFTMAX_SCALE
            if masked:
                r = lax.broadcasted_iota(jnp.int32, (T, T), 0)
                c = lax.broadcasted_iota(jnp.int32, (T, T), 1)
                s = jnp.where(r >= c, s, NEG)
            m_prev = m_sc[...]
            m_new = jnp.maximum(m_prev, jnp.max(s, axis=1, keepdims=True))
            alpha = jnp.exp(m_prev - m_new)
            p = jnp.exp(s - m_new[:, 0:1])
            l_sc[...] = alpha * l_sc[...] + jnp.sum(p, axis=1, keepdims=True)
            acc_sc[...] = alpha * acc_sc[...] + _dot(p.astype(bf16), vt)
            m_sc[...] = m_new

        def loop_body(j, c):
            step(j, False)
            return c

        lax.fori_loop(0, i, loop_body, 0)
        step(i, True)
        l = l_sc[...]
        o_ref[...] = acc_sc[...] / l
        lse_ref[0, 0] = _col_to_row(m_sc[...] + jnp.log(l))

    return pl.pallas_call(
        body, name="flash_fwd", grid=(H, nq),
        in_specs=[pl.BlockSpec((1, T, QK_PAD), lambda h, i: (h, i, 0)),
                  pl.BlockSpec((1, M, QK_PAD), lambda h, i: (h, 0, 0)),
                  pl.BlockSpec((1, M, V_DIM), lambda h, i: (h, 0, 0))],
        out_specs=[pl.BlockSpec((T, V_DIM), lambda h, i: (i, h)),
                   pl.BlockSpec((1, 1, 1, T), lambda h, i: (h, i, 0, 0))],
        out_shape=[jax.ShapeDtypeStruct((M, H * V_DIM), f32),
                   jax.ShapeDtypeStruct((H, nq, 1, T), f32)],
        scratch_shapes=[pltpu.VMEM((T, LANES), f32), pltpu.VMEM((T, LANES), f32), pltpu.VMEM((T, V_DIM), f32)],
        compiler_params=_cparams(("parallel", "parallel")),
    )(q, k, v)


def _attn_delta(o, do):
    M = o.shape[0]
    H = MLA_HEADS
    T = _attn_tile(M)

    def body(o_ref, d_ref, dh_ref, dl_ref):
        dv = d_ref[...]
        dh_ref[0] = dv.astype(bf16)
        col = jnp.sum(o_ref[...] * dv, axis=1, keepdims=True) + jnp.zeros((T, LANES), f32)
        dl_ref[0, 0] = _col_to_row(col)

    return pl.pallas_call(
        body, name="attn_delta", grid=(M // T, H),
        in_specs=[pl.BlockSpec((T, V_DIM), lambda i, h: (i, h)),
                  pl.BlockSpec((T, V_DIM), lambda i, h: (i, h))],
        out_specs=[pl.BlockSpec((1, T, V_DIM), lambda i, h: (h, i, 0)),
                   pl.BlockSpec((1, 1, 1, T), lambda i, h: (h, i, 0, 0))],
        out_shape=[jax.ShapeDtypeStruct((H, M, V_DIM), bf16),
                   jax.ShapeDtypeStruct((H, M // T, 1, T), f32)],
        compiler_params=_cparams(("parallel", "parallel")),
    )(o, do)


def _flash_bwd(q, k, v, do, lse, delta):
    H, M, _ = q.shape
    T = _attn_tile(M)
    nq = M // T

    def body(q_ref, do_ref, lse_ref, dl_ref, k_ref, v_ref, dq_ref, dk_ref, dv_ref, dk_sc, dv_sc):
        j = pl.program_id(1)

        @pl.when(j == 0)
        def _():
            dq_ref[...] = jnp.zeros_like(dq_ref)

        kt = k_ref[0]
        vt = v_ref[0]
        dk_sc[...] = jnp.zeros_like(dk_sc)
        dv_sc[...] = jnp.zeros_like(dv_sc)

        def step(i, masked):
            off = pl.multiple_of(i * T, T)
            qt = q_ref[0, pl.ds(off, T), :]
            dot_ = do_ref[0, pl.ds(off, T), :]
            st = _dot_nt(kt, qt) * SOFTMAX_SCALE
            if masked:
                r = lax.broadcasted_iota(jnp.int32, (T, T), 0)
                c = lax.broadcasted_iota(jnp.int32, (T, T), 1)
                st = jnp.where(c >= r, st, NEG)
            pt = jnp.exp(st - lse_ref[0, i])
            dv_sc[...] += _dot(pt.astype(bf16), dot_)
            dpt = _dot_nt(vt, dot_)
            dst = (pt * (dpt - dl_ref[0, i]) * SOFTMAX_SCALE).astype(bf16)
            dk_sc[...] += _dot(dst, qt)
            dq_ref[0, pl.ds(off, T), :] += _dot_tn(dst, kt)

        step(j, True)

        def loop_body(i, c):
            step(i, False)
            return c

        lax.fori_loop(j + 1, nq, loop_body, 0)
        dk_ref[0] = dk_sc[...]
        dv_ref[0] = dv_sc[...]

    return pl.pallas_call(
        body, name="flash_bwd", grid=(H, nq),
        in_specs=[pl.BlockSpec((1, M, QK_PAD), lambda h, j: (h, 0, 0)),
                  pl.BlockSpec((1, M, V_DIM), lambda h, j: (h, 0, 0)),
                  pl.BlockSpec((1, nq, 1, T), lambda h, j: (h, 0, 0, 0)),
                  pl.BlockSpec((1, nq, 1, T), lambda h, j: (h, 0, 0, 0)),
                  pl.BlockSpec((1, T, QK_PAD), lambda h, j: (h, j, 0)),
                  pl.BlockSpec((1, T, V_DIM), lambda h, j: (h, j, 0))],
        out_specs=[pl.BlockSpec((1, M, QK_PAD), lambda h, j: (h, 0, 0)),
                   pl.BlockSpec((1, T, QK_PAD), lambda h, j: (h, j, 0)),
                   pl.BlockSpec((1, T, V_DIM), lambda h, j: (h, j, 0))],
        out_shape=[jax.ShapeDtypeStruct((H, M, QK_PAD), f32),
                   jax.ShapeDtypeStruct((H, M, QK_PAD), f32),
                   jax.ShapeDtypeStruct((H, M, V_DIM), f32)],
        scratch_shapes=[pltpu.VMEM((T, QK_PAD), f32), pltpu.VMEM((T, V_DIM), f32)],
        compiler_params=_cparams(("parallel", "arbitrary")),
    )(q, do, lse, delta, k, v)


def _dt_fwd(dt_raw, bias):
    M = dt_raw.shape[0]
    tm = ROW_TILE

    def body(x_ref, b_ref, o_ref):
        u = x_ref[...] + b_ref[...]
        sp = jnp.maximum(u, 0.0) + jnp.log(1.0 + jnp.exp(-jnp.abs(u)))
        lane = lax.broadcasted_iota(jnp.int32, u.shape, 1)
        o_ref[...] = jnp.where(lane < SSM_HEADS, sp, 0.0)

    return pl.pallas_call(
        body, name="dt_fwd", grid=(M // tm,), in_specs=[_row(LANES, tm), _full((1, LANES))],
        out_specs=_row(LANES, tm), out_shape=jax.ShapeDtypeStruct((M, LANES), f32),
        compiler_params=_cparams(("parallel",)),
    )(dt_raw, bias)


def _dt_bwd(dt_raw, bias, ddt):
    M = dt_raw.shape[0]
    tm = ROW_TILE

    def body(x_ref, b_ref, d_ref, o_ref, db_ref):
        @pl.when(pl.program_id(0) == 0)
        def _():
            db_ref[...] = jnp.zeros_like(db_ref)

        u = x_ref[...] + b_ref[...]
        lane = lax.broadcasted_iota(jnp.int32, u.shape, 1)
        g = jnp.where(lane < SSM_HEADS, d_ref[...] * _sigmoid(u), 0.0)
        o_ref[...] = g
        db_ref[...] += jnp.sum(g, axis=0, keepdims=True)

    return pl.pallas_call(
        body, name="dt_bwd", grid=(M // tm,),
        in_specs=[_row(LANES, tm), _full((1, LANES)), _row(LANES, tm)],
        out_specs=[_row(LANES, tm), _full((1, LANES))],
        out_shape=[jax.ShapeDtypeStruct((M, LANES), f32), jax.ShapeDtypeStruct((1, LANES), f32)],
        compiler_params=_cparams(("arbitrary",)),
    )(dt_raw, bias, ddt)


def _ssd_common(dtp_ref, dtt_ref, arow_ref, acol_ref):
    Q = CHUNK
    r = lax.broadcasted_iota(jnp.int32, (Q, Q), 0)
    c = lax.broadcasted_iota(jnp.int32, (Q, Q), 1)
    causal = r >= c
    tril = causal.astype(f32)
    triu = (r <= c).astype(f32)
    dt = dtp_ref[...]
    cs = _dot_hi(tril, dt * arow_ref[...])
    cst = _dot_hi(dtt_ref[...] * acol_ref[...], triu)
    return causal, triu, dt, cs, cst


def _ssd_fwd(xbc_c, dtp, dtt, a_row, a_col):
    M = xbc_c.shape[0]
    Q = CHUNK
    nch = M // Q

    def body(x_ref, dtp_ref, dtt_ref, arow_ref, acol_ref, y_ref, hin_ref, ht_sc):
        @pl.when(pl.program_id(0) == 0)
        def _():
            ht_sc[...] = jnp.zeros_like(ht_sc)

        causal, _, dt, cs, cst = _ssd_common(dtp_ref, dtt_ref, arow_ref, acol_ref)
        ecs = jnp.exp(cs)
        cs_last = cs[Q - 1:Q, :]
        dte = jnp.exp(cs_last - cs)
        e_last = jnp.exp(cs_last)
        for g in range(SSM_GROUPS):
            bg = x_ref[:, D_SSM + g * SSM_N:D_SSM + (g + 1) * SSM_N]
            cg = x_ref[:, D_SSM + D_BC + g * SSM_N:D_SSM + D_BC + (g + 1) * SSM_N]
            bg_b = bg.astype(bf16)
            cg_b = cg.astype(bf16)
            cb = _dot_nt(cg_b, bg_b)
            bgt_b = bg.T.astype(bf16)
            for hh in range(SSM_HPG):
                h = g * SSM_HPG + hh
                seg = cs[:, h:h + 1] - cst[h:h + 1, :]
                lm = jnp.exp(jnp.where(causal, seg, -jnp.inf))
                mb = (cb * lm).astype(bf16)
                xh = x_ref[:, h * SSM_P:(h + 1) * SSM_P]
                xdt = xh * dt[:, h:h + 1]
                hth = ht_sc[h]
                hin_ref[0, h] = hth
                y_off = _dot(cg_b, hth.astype(bf16)) * ecs[:, h:h + 1]
                y_ref[:, h * SSM_P:(h + 1) * SSM_P] = _dot(mb, xdt.astype(bf16)) + y_off
                st = _dot(bgt_b, (xdt * dte[:, h:h + 1]).astype(bf16))
                ht_sc[h] = hth * e_last[:, h:h + 1] + st

    return pl.pallas_call(
        body, name="ssd_fwd", grid=(nch,),
        in_specs=[pl.BlockSpec((Q, D_XBC), lambda c: (c, 0)),
                  pl.BlockSpec((Q, LANES), lambda c: (c, 0)),
                  pl.BlockSpec((SSM_HEADS, Q), lambda c: (0, c)),
                  _full((1, LANES)), _full((SSM_HEADS, LANES))],
        out_specs=[pl.BlockSpec((Q, D_SSM), lambda c: (c, 0)),
                   pl.BlockSpec((1, SSM_HEADS, SSM_N, SSM_P), lambda c: (c, 0, 0, 0))],
        out_shape=[jax.ShapeDtypeStruct((M, D_SSM), f32),
                   jax.ShapeDtypeStruct((nch, SSM_HEADS, SSM_N, SSM_P), f32)],
        scratch_shapes=[pltpu.VMEM((SSM_HEADS, SSM_N, SSM_P), f32)],
        compiler_params=_cparams(("arbitrary",)),
    )(xbc_c, dtp, dtt, a_row, a_col)


def _ssd_bwd(xbc_c, dtp, dtt, a_row, a_col, hin, dy, d_exp):
    M = xbc_c.shape[0]
    Q = CHUNK
    nch = M // Q
    rev = lambda c: nch - 1 - c

    def body(x_ref, dtp_ref, dtt_ref, arow_ref, acol_ref, hin_ref, dy_ref, dexp_ref,
             dx_ref, ddt_ref, da_ref, dht_sc):
        @pl.when(pl.program_id(0) == 0)
        def _():
            dht_sc[...] = jnp.zeros_like(dht_sc)
            da_ref[...] = jnp.zeros_like(da_ref)

        causal, triu, dt, cs, cst = _ssd_common(dtp_ref, dtt_ref, arow_ref, acol_ref)
        ecs = jnp.exp(cs)
        cs_last = cs[Q - 1:Q, :]
        dte = jnp.exp(cs_last - cs)
        e_last = jnp.exp(cs_last)
        lane = lax.broadcasted_iota(jnp.int32, (Q, LANES), 1)
        row = lax.broadcasted_iota(jnp.int32, (Q, LANES), 0)
        dcs = jnp.zeros((Q, LANES), f32)
        ddt = jnp.zeros((Q, LANES), f32)
        for g in range(SSM_GROUPS):
            b0 = D_SSM + g * SSM_N
            c0 = D_SSM + D_BC + g * SSM_N
            bg = x_ref[:, b0:b0 + SSM_N]
            cg = x_ref[:, c0:c0 + SSM_N]
            bg_b = bg.astype(bf16)
            cg_b = cg.astype(bf16)
            cgt_b = cg.T.astype(bf16)
            cb = _dot_nt(cg_b, bg_b)
            dg_acc = jnp.zeros((Q, Q), f32)
            dc_acc = jnp.zeros((Q, SSM_N), f32)
            db_acc = jnp.zeros((Q, SSM_N), f32)
            for hh in range(SSM_HPG):
                h = g * SSM_HPG + hh
                sl = slice(h * SSM_P, (h + 1) * SSM_P)
                seg = cs[:, h:h + 1] - cst[h:h + 1, :]
                lm = jnp.exp(jnp.where(causal, seg, -jnp.inf))
                mm = cb * lm
                xh = x_ref[:, sl]
                dth = dt[:, h:h + 1]
                xdt = xh * dth
                xdt_b = xdt.astype(bf16)
                dyh = dy_ref[:, sl]
                dyh_b = dyh.astype(bf16)
                ecs_h = ecs[:, h:h + 1]
                dte_h = dte[:, h:h + 1]
                hth = hin_ref[0, h]
                hth_b = hth.astype(bf16)
                dht = dht_sc[h]
                dht_b = dht.astype(bf16)
                y_off = _dot(cg_b, hth_b) * ecs_h
                dcs_h = jnp.sum(dyh * y_off, axis=1, keepdims=True)
                dye_b = (dyh * ecs_h).astype(bf16)
                dc_acc = dc_acc + _dot_nt(dye_b, hth_b)
                dht_new = dht * e_last[:, h:h + 1] + _dot(cgt_b, dye_b)
                dm = _dot_nt(dyh_b, xdt_b)
                w = dm * mm
                dcs_h = dcs_h + jnp.sum(w, axis=1, keepdims=True) - jnp.sum(w.T, axis=1, keepdims=True)
                dxdt = _dot_tn(mm.astype(bf16), dyh_b)
                dg_acc = dg_acc + dm * lm
                e = _dot(bg_b, dht_b)
                dxdt = dxdt + e * dte_h
                t = jnp.sum(e * xdt, axis=1, keepdims=True) * dte_h
                dcs_h = dcs_h - t
                dlast = jnp.sum(t) + jnp.sum(e_last[:, h:h + 1]) * jnp.sum(dht * hth)
                db_acc = db_acc + _dot_nt((xdt * dte_h).astype(bf16), dht_b)
                dht_sc[h] = dht_new
                dx_ref[:, sl] = dxdt * dth + dexp_ref[:, sl] * dyh
                onehot = lane == h
                ddt = ddt + jnp.where(onehot, jnp.sum(dxdt * xh, axis=1, keepdims=True), 0.0)
                dcs = dcs + jnp.where(onehot, dcs_h, 0.0) + jnp.where(onehot & (row == Q - 1), dlast, 0.0)
            dg_b = dg_acc.astype(bf16)
            dx_ref[:, c0:c0 + SSM_N] = dc_acc + _dot(dg_b, bg_b)
            dx_ref[:, b0:b0 + SSM_N] = db_acc + _dot_tn(dg_b, cg_b)
        da = _dot_hi(triu, dcs)
        ddt_ref[...] = ddt + da * arow_ref[...]
        da_ref[...] += jnp.sum(da * dt, axis=0, keepdims=True)

    return pl.pallas_call(
        body, name="ssd_bwd", grid=(nch,),
        in_specs=[pl.BlockSpec((Q, D_XBC), lambda c: (rev(c), 0)),
                  pl.BlockSpec((Q, LANES), lambda c: (rev(c), 0)),
                  pl.BlockSpec((SSM_HEADS, Q), lambda c: (0, rev(c))),
                  _full((1, LANES)), _full((SSM_HEADS, LANES)),
                  pl.BlockSpec((1, SSM_HEADS, SSM_N, SSM_P), lambda c: (rev(c), 0, 0, 0)),
                  pl.BlockSpec((Q, D_SSM), lambda c: (rev(c), 0)),
                  _full((1, D_SSM))],
        out_specs=[pl.BlockSpec((Q, D_XBC), lambda c: (rev(c), 0)),
                   pl.BlockSpec((Q, LANES), lambda c: (rev(c), 0)),
                   _full((1, LANES))],
        out_shape=[jax.ShapeDtypeStruct((M, D_XBC), f32), jax.ShapeDtypeStruct((M, LANES), f32),
                   jax.ShapeDtypeStruct((1, LANES), f32)],
        scratch_shapes=[pltpu.VMEM((SSM_HEADS, SSM_N, SSM_P), f32)],
        compiler_params=_cparams(("arbitrary",)),
    )(xbc_c, dtp, dtt, a_row, a_col, hin, dy, d_exp)


def _gate_norm_fwd(y, xbc_c, z, d_exp, g):
    M = y.shape[0]
    tm = ROW_TILE
    gw = D_SSM // SSM_GROUPS

    def body(y_ref, x_ref, z_ref, d_ref, g_ref, o_ref):
        yg = (y_ref[...] + d_ref[...] * x_ref[...]) * _silu(z_ref[...])
        for gi in range(SSM_GROUPS):
            blk = yg[:, gi * gw:(gi + 1) * gw]
            o_ref[:, gi * gw:(gi + 1) * gw] = (blk * _rstd(blk) * g_ref[:, gi * gw:(gi + 1) * gw]).astype(bf16)

    return pl.pallas_call(
        body, name="gate_norm_fwd", grid=(M // tm,),
        in_specs=[_row(D_SSM, tm), _row(D_SSM, tm), _row(D_SSM, tm), _full((1, D_SSM)), _full((1, D_SSM))],
        out_specs=_row(D_SSM, tm), out_shape=jax.ShapeDtypeStruct((M, D_SSM), bf16),
        compiler_params=_cparams(("parallel",)),
    )(y, xbc_c, z, d_exp, g)


def _gate_norm_bwd(y, xbc_c, z, d_exp, g, dout, head_ind):
    M = y.shape[0]
    tm = ROW_TILE
    nt = M // tm
    gw = D_SSM // SSM_GROUPS

    def body(y_ref, x_ref, z_ref, d_ref, g_ref, do_ref, ind_ref, dy_ref, dz_ref, dg_ref, dd_ref, ddc_sc):
        i = pl.program_id(0)

        @pl.when(i == 0)
        def _():
            dg_ref[...] = jnp.zeros_like(dg_ref)
            ddc_sc[...] = jnp.zeros_like(ddc_sc)

        zv = z_ref[...]
        xv = x_ref[...]
        s = _silu(zv)
        yd = y_ref[...] + d_ref[...] * xv
        yg = yd * s
        dov = do_ref[...]
        for gi in range(SSM_GROUPS):
            sl = slice(gi * gw, (gi + 1) * gw)
            dyg, dgp = _rms_bwd_math(yg[:, sl], g_ref[:, sl], dov[:, sl])
            dg_ref[:, sl] += jnp.sum(dgp, axis=0, keepdims=True)
            dyd = dyg * s[:, sl]
            dy_ref[:, sl] = dyd
            dz_ref[:, sl] = (dyg * yd[:, sl] * _dsilu(zv[:, sl])).astype(bf16)
            ddc_sc[:, sl] += jnp.sum(dyd * xv[:, sl], axis=0, keepdims=True)

        @pl.when(i == nt - 1)
        def _():
            dd_ref[...] = _dot_hi(ddc_sc[...], ind_ref[...])

    return pl.pallas_call(
        body, name="gate_norm_bwd", grid=(nt,),
        in_specs=[_row(D_SSM, tm), _row(D_SSM, tm), _row(D_SSM, tm), _full((1, D_SSM)), _full((1, D_SSM)),
                  _row(D_SSM, tm), _full((D_SSM, LANES))],
        out_specs=[_row(D_SSM, tm), _row(D_SSM, tm), _full((1, D_SSM)), _full((1, LANES))],
        out_shape=[jax.ShapeDtypeStruct((M, D_SSM), f32), jax.ShapeDtypeStruct((M, D_SSM), bf16),
                   jax.ShapeDtypeStruct((1, D_SSM), f32), jax.ShapeDtypeStruct((1, LANES), f32)],
        scratch_shapes=[pltpu.VMEM((1, D_SSM), f32)],
        compiler_params=_cparams(("arbitrary",)),
    )(y, xbc_c, z, d_exp, g, dout, head_ind)


_PEER_FLIPS = [(0, 0, 1), (0, 1, 0), (0, 1, 1), (1, 0, 0), (1, 0, 1), (1, 1, 0), (1, 1, 1)]


def _exchange(arrays, scatter, name):
    n = len(arrays)

    def body(*refs):
        ins, outs = refs[:n], refs[n:2 * n]
        send_sems, recv_sems, loc_sems = refs[2 * n:]
        x, y, c = lax.axis_index("x"), lax.axis_index("y"), lax.axis_index("c")
        me = 4 * x + 2 * y + c
        local = []
        for a in range(n):
            src = ins[a].at[me] if scatter else ins[a]
            lc = pltpu.make_async_copy(src, outs[a].at[me], loc_sems.at[a])
            lc.start()
            local.append(lc)
        remote = []
        for p, (fx, fy, fc) in enumerate(_PEER_FLIPS):
            tx = 1 - x if fx else x
            ty = 1 - y if fy else y
            tc = 1 - c if fc else c
            tgt = 4 * tx + 2 * ty + tc
            for a in range(n):
                src = ins[a].at[tgt] if scatter else ins[a]
                cp = pltpu.make_async_remote_copy(
                    src_ref=src, dst_ref=outs[a].at[me],
                    send_sem=send_sems.at[p * n + a], recv_sem=recv_sems.at[p * n + a],
                    device_id=(tx, ty, tc), device_id_type=_MESH)
                cp.start()
                remote.append(cp)
        for cp in remote:
            cp.wait()
        for lc in local:
            lc.wait()

    any_spec = pl.BlockSpec(memory_space=pl.ANY)
    out_shape = []
    for arr in arrays:
        shp = arr.shape if scatter else (N_DEV,) + arr.shape
        out_shape.append(jax.ShapeDtypeStruct(shp, arr.dtype))
    return pl.pallas_call(
        body, name=name, in_specs=[any_spec] * n, out_specs=[any_spec] * n, out_shape=out_shape,
        scratch_shapes=[pltpu.SemaphoreType.DMA((7 * n,)), pltpu.SemaphoreType.DMA((7 * n,)),
                        pltpu.SemaphoreType.DMA((n,))],
    )(*arrays)


def _adamw(parts, w, m, v, name):
    R, C = w.shape
    tr = _pick(R, (PACK_ROW_TILE, 16, 8))
    c1 = 1.0 - ADAM_B1 ** ADAM_STEP
    c2 = 1.0 - ADAM_B2 ** ADAM_STEP

    def body(p_ref, w_ref, m_ref, v_ref, g_ref, d_ref, nm_ref, nv_ref):
        g = p_ref[0]
        for s in range(1, N_DEV):
            g = g + p_ref[s]
        mn = ADAM_B1 * m_ref[...] + (1.0 - ADAM_B1) * g
        vn = ADAM_B2 * v_ref[...] + (1.0 - ADAM_B2) * (g * g)
        m_hat = mn / c1
        v_hat = vn / c2
        g_ref[...] = g
        d_ref[...] = -ADAM_LR * (m_hat / (jnp.sqrt(v_hat) + ADAM_EPS) + ADAM_WD * w_ref[...])
        nm_ref[...] = mn
        nv_ref[...] = vn

    spec = pl.BlockSpec((tr, C), lambda i: (i, 0))
    return pl.pallas_call(
        body, name=name, grid=(R // tr,),
        in_specs=[pl.BlockSpec((N_DEV, tr, C), lambda i: (0, i, 0)), spec, spec, spec],
        out_specs=[spec] * 4, out_shape=[jax.ShapeDtypeStruct((R, C), f32)] * 4,
        compiler_params=_cparams(("parallel",)),
    )(parts, w, m, v)


def _flat_rows(a, lead_ndim):
    lead = a.shape[:lead_ndim]
    n = int(np.prod(a.shape[lead_ndim:]))
    a = a.reshape(lead + (n,))
    pad = (-n) % PACK_W
    if pad:
        a = jnp.pad(a, [(0, 0)] * lead_ndim + [(0, pad)])
    return a.reshape(lead + ((n + pad) // PACK_W, PACK_W))


def _pack(arrays, lead_ndim, total_rows, dtype):
    rows = [_flat_rows(a.astype(dtype), lead_ndim) for a in arrays]
    cat = jnp.concatenate(rows, axis=lead_ndim)
    pad = total_rows - cat.shape[lead_ndim]
    if pad:
        cat = jnp.pad(cat, [(0, 0)] * lead_ndim + [(0, pad), (0, 0)])
    return cat


def _unpack(buf, shapes, lead_ndim):
    out = []
    r = 0
    lead = buf.shape[:lead_ndim]
    for shp in shapes:
        n = int(np.prod(shp))
        nr = -(-n // PACK_W)
        piece = lax.slice_in_dim(buf, r, r + nr, axis=lead_ndim)
        piece = piece.reshape(lead + (nr * PACK_W,))
        piece = lax.slice_in_dim(piece, 0, n, axis=lead_ndim)
        out.append(piece.reshape(lead + tuple(shp)))
        r += nr
    return out


def _round_up(n, m):
    return -(-n // m) * m


def kernel(x, meta_tokens, norm_mix_pre, norm_mix_post, norm_ffn_pre, norm_ffn_post, w_in, q_a_norm, w_uq, kv_a_norm, w_ukv, attn_out_norm, ssm_conv_w, ssm_conv_b, ssm_dt_bias, ssm_A_log, ssm_D, ssm_norm, w_out, w_up, ffn_conv_w, ffn_conv_b, w_down, loss_target, m_meta_tokens, m_norm_mix_pre, m_norm_mix_post, m_norm_ffn_pre, m_norm_ffn_post, m_w_in, m_q_a_norm, m_w_uq, m_kv_a_norm, m_w_ukv, m_attn_out_norm, m_ssm_conv_w, m_ssm_conv_b, m_ssm_dt_bias, m_ssm_A_log, m_ssm_D, m_ssm_norm, m_w_out, m_w_up, m_ffn_conv_w, m_ffn_conv_b, m_w_down, v_meta_tokens, v_norm_mix_pre, v_norm_mix_post, v_norm_ffn_pre, v_norm_ffn_post, v_w_in, v_q_a_norm, v_w_uq, v_kv_a_norm, v_w_ukv, v_attn_out_norm, v_ssm_conv_w, v_ssm_conv_b, v_ssm_dt_bias, v_ssm_A_log, v_ssm_D, v_ssm_norm, v_w_out, v_w_up, v_ffn_conv_w, v_ffn_conv_b, v_w_down):
    seq = x.shape[1]
    n_real = N_META + seq
    Lp = _round_up(n_real, 768) if n_real > 2048 else _round_up(n_real, ROW_TILE)
    D = D_MODEL

    sharded_w = [w_in, w_uq, w_ukv, w_out, w_up, w_down]
    sharded_s = [meta_tokens, ssm_conv_w, ffn_conv_w]
    sharded_names = sharded_w + sharded_s
    sharded_m = [m_w_in, m_w_uq, m_w_ukv, m_w_out, m_w_up, m_w_down, m_meta_tokens, m_ssm_conv_w, m_ffn_conv_w]
    sharded_v = [v_w_in, v_w_uq, v_w_ukv, v_w_out, v_w_up, v_w_down, v_meta_tokens, v_ssm_conv_w, v_ffn_conv_w]
    repl_w = [norm_mix_pre, norm_mix_post, norm_ffn_pre, norm_ffn_post, q_a_norm, kv_a_norm, attn_out_norm,
              ssm_conv_b, ssm_dt_bias, ssm_A_log, ssm_D, ssm_norm, ffn_conv_b]
    repl_m = [m_norm_mix_pre, m_norm_mix_post, m_norm_ffn_pre, m_norm_ffn_post, m_q_a_norm, m_kv_a_norm,
              m_attn_out_norm, m_ssm_conv_b, m_ssm_dt_bias, m_ssm_A_log, m_ssm_D, m_ssm_norm, m_ffn_conv_b]
    repl_v = [v_norm_mix_pre, v_norm_mix_post, v_norm_ffn_pre, v_norm_ffn_post, v_q_a_norm, v_kv_a_norm,
              v_attn_out_norm, v_ssm_conv_b, v_ssm_dt_bias, v_ssm_A_log, v_ssm_D, v_ssm_norm, v_ffn_conv_b]

    big_rows = _round_up(sum(-(-int(np.prod(a.shape)) // PACK_W) for a in sharded_w), PACK_ROW_TILE)
    small_rows = _round_up(sum(-(-int(np.prod(a.shape)) // PACK_W) for a in sharded_s), 16)
    wb = _pack(sharded_w, 0, big_rows, bf16)
    ws = _pack(sharded_s, 0, small_rows, f32)
    wb_all, ws_all = _exchange([wb, ws], False, "gather_weights")
    g_w_in, g_w_uq, g_w_ukv, g_w_out, g_w_up, g_w_down = _unpack(wb_all, [a.shape for a in sharded_w], 1)
    g_meta, g_sconv, g_fconv = _unpack(ws_all, [a.shape for a in sharded_s], 1)

    def cols(gathered):
        t = gathered[:, 0]
        return jnp.transpose(t, (1, 0, 2)).reshape(t.shape[1], N_DEV * t.shape[2])

    win = cols(g_w_in)
    o = np.cumsum((0, Q_RANK, KV_RANK, QK_ROPE, D_SSM, D_XBC, SSM_HEADS))
    w_q, w_kv = win[:, o[0]:o[1]], win[:, o[1]:o[2]]
    w_rope = jnp.pad(win[:, o[2]:o[3]], ((0, 0), (0, LANES - QK_ROPE)))
    w_z, w_xbc = win[:, o[3]:o[4]], win[:, o[4]:o[5]]
    w_dt = jnp.pad(win[:, o[5]:o[6]], ((0, 0), (0, LANES - SSM_HEADS)))
    wuq = g_w_uq.reshape(Q_RANK, MLA_HEADS, QK_NOPE + QK_ROPE)
    wuq = jnp.pad(wuq, ((0, 0), (0, 0), (0, QK_PAD - QK_NOPE - QK_ROPE))).reshape(Q_RANK, MLA_HEADS * QK_PAD)
    wukv = g_w_ukv.reshape(KV_RANK, MLA_HEADS * (QK_NOPE + V_DIM))
    wout = g_w_out.reshape(D_ATTN + D_SSM, D)
    wout_a, wout_s = wout[:D_ATTN], wout[D_ATTN:]
    wup = cols(g_w_up)
    wdown = g_w_down.reshape(D_FF, D)
    meta_full = jnp.transpose(g_meta, (1, 0, 2)).reshape(N_META, D)
    sconv_w = jnp.pad(cols(g_sconv), ((0, SUBLANES - SSM_CONV), (0, 0)))
    fconv_w = jnp.pad(cols(g_fconv), ((0, SUBLANES - FFN_CONV), (0, 0)))

    pos = jnp.arange(Lp, dtype=f32)
    inv = ROPE_THETA ** (-jnp.arange(0, QK_ROPE, 2, dtype=f32) / QK_ROPE)
    ang = pos[:, None] * inv[None, :]
    cs_, sn_ = jnp.cos(ang), jnp.sin(ang)
    zpad = jnp.zeros((Lp, LANES - QK_ROPE), f32)
    cos_t = jnp.concatenate([cs_, cs_, zpad], axis=1)
    sin_t = jnp.concatenate([-sn_, sn_, zpad], axis=1)
    dt_bias_p = jnp.pad(ssm_dt_bias, ((0, 0), (0, LANES - SSM_HEADS)))
    a_neg = -jnp.exp(ssm_A_log)
    a_row = jnp.pad(a_neg, ((0, 0), (0, LANES - SSM_HEADS)))
    a_col = jnp.broadcast_to(a_neg.reshape(SSM_HEADS, 1), (SSM_HEADS, LANES))
    d_exp = jnp.repeat(ssm_D, SSM_P, axis=1)
    head_ind = (jnp.arange(D_SSM)[:, None] // SSM_P == jnp.arange(LANES)[None, :]).astype(f32)

    xb = x[0]
    h0 = jnp.concatenate([meta_full, xb, jnp.zeros((Lp - n_real, D), f32)], axis=0)
    tgt = jnp.pad(loss_target[0], ((N_META, Lp - n_real), (0, 0)))
    hn1 = _rms_fwd(h0, norm_mix_pre, bf16, "norm_mix_pre")
    q_c = _mm([(hn1, w_q)], f32, False, "proj_q")
    kv_c = _mm([(hn1, w_kv)], f32, False, "proj_kv")
    kpe_raw = _mm([(hn1, w_rope)], f32, False, "proj_rope")
    z = _mm([(hn1, w_z)], f32, False, "proj_z")
    xbc = _mm([(hn1, w_xbc)], f32, False, "proj_xbc")
    dt_raw = _mm([(hn1, w_dt)], f32, False, "proj_dt")

    qn = _rms_fwd(q_c, q_a_norm, bf16, "norm_q")
    kvn = _rms_fwd(kv_c, kv_a_norm, bf16, "norm_kv")
    q_full = _mm([(qn, wuq)], f32, False, "up_q")
    kv_full = _mm([(kvn, wukv)], f32, False, "up_kv")
    qh = _rope_q_fwd(q_full, cos_t, sin_t)
    kh, vh = _rope_k_fwd(kv_full, kpe_raw, cos_t, sin_t)
    attn, lse = _flash_fwd(qh, kh, vh)
    an = _rms_fwd(attn, attn_out_norm, bf16, "norm_attn_out")

    xbc_c = _ssm_conv_fwd(xbc, sconv_w, ssm_conv_b)
    dtp = _dt_fwd(dt_raw, dt_bias_p)
    dtt = jnp.transpose(dtp[:, :SSM_HEADS])
    y_ssd, hin = _ssd_fwd(xbc_c, dtp, dtt, a_row, a_col)
    ssm = _gate_norm_fwd(y_ssd, xbc_c, z, d_exp, ssm_norm)

    mix = _mm([(an, wout_a), (ssm, wout_s)], f32, False, "out_proj")
    h1, hn2 = _resid_norm(h0, mix, norm_mix_post, norm_ffn_pre)
    up = _mm([(hn2, wup)], f32, False, "ffn_up")
    act = _ffn_gate_fwd(up, fconv_w, ffn_conv_b)
    down = _mm([(act, wdown)], f32, False, "ffn_down")
    dh2, d_down, dg_ffn_post, loss_part = _final(h1, down, norm_ffn_post, tgt, n_real)

    d_act = _mm([(d_down, wdown)], f32, True, "ffn_down_dx")
    dw_down = _mm_tn(act, d_down, "ffn_down_dw")
    du_g, du_v, dwc_g, dwc_v, dbc_g, dbc_v = _ffn_gate_bwd(up, fconv_w, ffn_conv_b, d_act)
    dup_g, dup_v = _conv_bwd_dx([du_g, du_v], fconv_w, FFN_CONV, "ffn_conv_dx")
    d_hn2 = _mm([(dup_g, wup[:, :D_FF]), (dup_v, wup[:, D_FF:])], f32, True, "ffn_up_dx")
    dw_up = jnp.concatenate([_mm_tn(hn2, dup_g, "ffn_up_dw_g"), _mm_tn(hn2, dup_v, "ffn_up_dw_v")], axis=1)
    dh1, d_mix, dg_ffn_pre, dg_mix_post = _mid_bwd(h1, norm_ffn_pre, d_hn2, dh2, mix, norm_mix_post)
    d_an = _mm([(d_mix, wout_a)], f32, True, "out_proj_dx_a")
    d_ssm = _mm([(d_mix, wout_s)], f32, True, "out_proj_dx_s")
    dw_out = jnp.concatenate([_mm_tn(an, d_mix, "out_proj_dw_a"), _mm_tn(ssm, d_mix, "out_proj_dw_s")], axis=0)

    d_attn, dg_attn_out = _rms_bwd(attn, attn_out_norm, d_an, f32, "norm_attn_out_bwd")
    do_h, delta = _attn_delta(attn, d_attn)
    dqh, dkh, dvh = _flash_bwd(qh, kh, vh, do_h, lse, delta)
    dq_full = _rope_q_bwd(dqh, cos_t, sin_t)
    dkv_full, d_kpe_raw = _rope_k_bwd(dkh, dvh, cos_t, sin_t)
    d_qn = _mm([(dq_full, wuq)], f32, True, "up_q_dx")
    dw_uq = _mm_tn(qn, dq_full, "up_q_dw")
    d_kvn = _mm([(dkv_full, wukv)], f32, True, "up_kv_dx")
    dw_ukv = _mm_tn(kvn, dkv_full, "up_kv_dw")
    d_q_c, dg_q = _rms_bwd(q_c, q_a_norm, d_qn, bf16, "norm_q_bwd")
    d_kv_c, dg_kv = _rms_bwd(kv_c, kv_a_norm, d_kvn, bf16, "norm_kv_bwd")

    dy_ssd, dz, dg_ssm, dd_heads = _gate_norm_bwd(y_ssd, xbc_c, z, d_exp, ssm_norm, d_ssm, head_ind)
    d_xbc_c, ddt, da_heads = _ssd_bwd(xbc_c, dtp, dtt, a_row, a_col, hin, dy_ssd, d_exp)
    du_s, dw_sconv, db_sconv = _ssm_conv_bwd(xbc, sconv_w, ssm_conv_b, d_xbc_c)
    (d_xbc,) = _conv_bwd_dx([du_s], sconv_w, SSM_CONV, "ssm_conv_dx")
    d_dt_raw, d_dt_bias = _dt_bwd(dt_raw, dt_bias_p, ddt)

    segs = [(d_q_c, w_q), (d_kv_c, w_kv), (d_kpe_raw, w_rope), (dz, w_z), (d_xbc, w_xbc), (d_dt_raw, w_dt)]
    d_hn1 = _mm(segs, f32, True, "proj_dx")
    dw_q = _mm_tn(hn1, d_q_c, "proj_dw_q")
    dw_kv = _mm_tn(hn1, d_kv_c, "proj_dw_kv")
    dw_rope = _mm_tn(hn1, d_kpe_raw, "proj_dw_rope")
    dw_z = _mm_tn(hn1, dz, "proj_dw_z")
    dw_xbc = _mm_tn(hn1, d_xbc, "proj_dw_xbc")
    dw_dt = _mm_tn(hn1, d_dt_raw, "proj_dw_dt")
    dh0, dg_mix_pre = _rms_bwd(h0, norm_mix_pre, d_hn1, f32, "norm_mix_pre_bwd", residual=dh1)

    grad_x = dh0[N_META:n_real][None]
    d_meta = dh0[:N_META]

    dw_in = jnp.concatenate([dw_q, dw_kv, dw_rope[:, :QK_ROPE], dw_z, dw_xbc, dw_dt[:, :SSM_HEADS]], axis=1)

    def col_blocks(gm):
        r, cc = gm.shape
        return jnp.transpose(gm.reshape(r, N_DEV, cc // N_DEV), (1, 0, 2))

    dw_uq3 = dw_uq.reshape(Q_RANK, MLA_HEADS, QK_PAD)[:, :, :QK_NOPE + QK_ROPE]
    dest_blocks = [
        col_blocks(dw_in),
        dw_uq3.reshape(N_DEV, Q_RANK // N_DEV, MLA_HEADS, QK_NOPE + QK_ROPE),
        dw_ukv.reshape(N_DEV, KV_RANK // N_DEV, MLA_HEADS, QK_NOPE + V_DIM),
        dw_out.reshape(N_DEV, (D_ATTN + D_SSM) // N_DEV, D),
        col_blocks(dw_up),
        dw_down.reshape(N_DEV, D_FF // N_DEV, D),
        col_blocks(d_meta),
        col_blocks(dw_sconv[:SSM_CONV]),
        col_blocks(jnp.concatenate([dwc_g, dwc_v], axis=1)[:FFN_CONV]),
    ]
    grad_rows = _round_up(sum(-(-int(np.prod(a.shape[1:])) // PACK_W) for a in dest_blocks), PACK_ROW_TILE)
    gpack = _pack(dest_blocks, 1, grad_rows, f32)
    (gparts,) = _exchange([gpack], True, "scatter_grads")
    wpack = _pack([a[None] for a in sharded_names], 1, grad_rows, f32)[0]
    mpack = _pack([a[None] for a in sharded_m], 1, grad_rows, f32)[0]
    vpack = _pack([a[None] for a in sharded_v], 1, grad_rows, f32)[0]
    outs_big = _adamw(gparts, wpack, mpack, vpack, "adamw_sharded")
    shard_shapes = [a.shape for a in sharded_names]
    g_sh, d_sh, m_sh, v_sh = [_unpack(b[None], shard_shapes, 1) for b in outs_big]
    g_sh, d_sh, m_sh, v_sh = [[t[0] for t in lst] for lst in (g_sh, d_sh, m_sh, v_sh)]

    dg_alog = da_heads[:, :SSM_HEADS] * a_neg
    repl_g = [dg_mix_pre, dg_mix_post, dg_ffn_pre, dg_ffn_post, dg_q, dg_kv, dg_attn_out, db_sconv,
              d_dt_bias[:, :SSM_HEADS], dg_alog, dd_heads[:, :SSM_HEADS], dg_ssm,
              jnp.concatenate([dbc_g, dbc_v], axis=1)]
    loss_vec = loss_part[:, :1]
    small_total = _round_up(sum(-(-int(np.prod(a.shape)) // PACK_W) for a in repl_g) + 1, 16)
    spack = _pack(repl_g + [loss_vec], 0, small_total, f32)
    (sparts,) = _exchange([spack], False, "gather_small_grads")
    zero1 = jnp.zeros((1, 1), f32)
    rw = _pack(repl_w + [zero1], 0, small_total, f32)
    rm = _pack(repl_m + [zero1], 0, small_total, f32)
    rv = _pack(repl_v + [zero1], 0, small_total, f32)
    outs_small = _adamw(sparts, rw, rm, rv, "adamw_replicated")
    repl_shapes = [a.shape for a in repl_w] + [(1, 1)]
    g_rp, d_rp, m_rp, v_rp = [_unpack(b, repl_shapes, 0) for b in outs_small]
    loss = g_rp[-1][0, 0]

    order = ["meta_tokens", "norm_mix_pre", "norm_mix_post", "norm_ffn_pre", "norm_ffn_post", "w_in", "q_a_norm",
             "w_uq", "kv_a_norm", "w_ukv", "attn_out_norm", "ssm_conv_w", "ssm_conv_b", "ssm_dt_bias", "ssm_A_log",
             "ssm_D", "ssm_norm", "w_out", "w_up", "ffn_conv_w", "ffn_conv_b", "w_down"]
    sh_names = ["w_in", "w_uq", "w_ukv", "w_out", "w_up", "w_down", "meta_tokens", "ssm_conv_w", "ffn_conv_w"]
    rp_names = ["norm_mix_pre", "norm_mix_post", "norm_ffn_pre", "norm_ffn_post", "q_a_norm", "kv_a_norm",
                "attn_out_norm", "ssm_conv_b", "ssm_dt_bias", "ssm_A_log", "ssm_D", "ssm_norm", "ffn_conv_b"]

    def lookup(sh_list, rp_list):
        d = {n: t for n, t in zip(sh_names, sh_list)}
        d.update({n: t for n, t in zip(rp_names, rp_list)})
        return [d[n] for n in order]

    return (loss, grad_x, *lookup(g_sh, g_rp), *lookup(d_sh, d_rp), *lookup(m_sh, m_rp), *lookup(v_sh, v_rp))
```

```python
import functools
import math

import jax
import jax.numpy as jnp
import numpy as np
from jax import lax
from jax.experimental import pallas as pl
from jax.experimental.pallas import tpu as pltpu

f32 = jnp.float32
bf16 = jnp.bfloat16

D_MODEL = 1024
SEQ = 8192
N_META = 16
MLA_HEADS = 8
QK_NOPE = 128
QK_ROPE = 64
V_DIM = 128
Q_RANK = 384
KV_RANK = 256
ROPE_THETA = 10000.0
SOFTMAX_SCALE = (QK_NOPE + QK_ROPE) ** -0.5
D_ATTN = MLA_HEADS * V_DIM
SSM_HEADS = 16
SSM_P = 64
SSM_GROUPS = 2
SSM_HPG = SSM_HEADS // SSM_GROUPS
SSM_N = 128
SSM_CONV = 4
CHUNK = 128
D_SSM = SSM_HEADS * SSM_P
D_BC = SSM_GROUPS * SSM_N
D_XBC = D_SSM + 2 * D_BC
D_FF = 2816
FFN_CONV = 3
EPS = 1e-6
D_IN = Q_RANK + KV_RANK + QK_ROPE + D_SSM + D_XBC + SSM_HEADS
QK_PAD = 256
N_DEV = 8

ADAM_LR = 0.001
ADAM_B1 = 0.9
ADAM_B2 = 0.999
ADAM_EPS = 1e-08
ADAM_WD = 0.01
ADAM_STEP = 10

LANES = 128
SUBLANES = 8
ROW_TILE = 256
VMEM_LIMIT = 56 * 1024 * 1024
PACK_W = 1024
PACK_ROW_TILE = 128
NEG = -1e30

_MESH = pl.DeviceIdType.MESH


def _pick(n, prefs):
    for p in prefs:
        if n % p == 0:
            return p
    return n


def _rt(m):
    return _pick(m, (384, ROW_TILE))


def _cparams(sem):
    return pltpu.CompilerParams(dimension_semantics=sem, vmem_limit_bytes=VMEM_LIMIT)


def _row(spec_cols, tm):
    return pl.BlockSpec((tm, spec_cols), lambda i: (i, 0))


def _full(shape):
    nd = len(shape)
    return pl.BlockSpec(shape, lambda *a: (0,) * nd)


def _sigmoid(x):
    return 1.0 / (1.0 + jnp.exp(-x))


def _silu(x):
    return x * _sigmoid(x)


def _dsilu(x):
    s = _sigmoid(x)
    return s * (1.0 + x * (1.0 - s))


def _dot(a, b):
    return jnp.dot(a, b, preferred_element_type=f32)


def _dot_nt(a, b):
    return lax.dot_general(a, b, (((1,), (1,)), ((), ())), preferred_element_type=f32)


def _dot_tn(a, b):
    return lax.dot_general(a, b, (((0,), (0,)), ((), ())), preferred_element_type=f32)


def _dot_hi(a, b):
    return jnp.dot(a, b, precision=lax.Precision.HIGHEST, preferred_element_type=f32)


def _mm(pairs, out_dtype, trans_b, name):
    n = len(pairs)
    M = pairs[0][0].shape[0]
    N = pairs[0][1].shape[0] if trans_b else pairs[0][1].shape[1]
    tm = _pick(M, (768, 512, 256))
    tn = _pick(N, (512, 384, 256, 128))

    def body(*refs):
        o_ref = refs[2 * n]
        acc = None
        for p in range(n):
            a = refs[2 * p][...].astype(bf16)
            b = refs[2 * p + 1][...].astype(bf16)
            r = _dot_nt(a, b) if trans_b else _dot(a, b)
            acc = r if acc is None else acc + r
        o_ref[...] = acc.astype(out_dtype)

    in_specs, args = [], []
    for a, b in pairs:
        k = a.shape[1]
        in_specs.append(pl.BlockSpec((tm, k), lambda i, j: (i, 0)))
        if trans_b:
            in_specs.append(pl.BlockSpec((tn, k), lambda i, j: (j, 0)))
        else:
            in_specs.append(pl.BlockSpec((k, tn), lambda i, j: (0, j)))
        args += [a, b]
    return pl.pallas_call(
        body, name=name, grid=(M // tm, N // tn), in_specs=in_specs,
        out_specs=pl.BlockSpec((tm, tn), lambda i, j: (i, j)),
        out_shape=jax.ShapeDtypeStruct((M, N), out_dtype),
        compiler_params=_cparams(("parallel", "parallel")),
    )(*args)


def _mm_tn(a, g, name):
    M, K = a.shape
    N = g.shape[1]
    tm = _pick(M, (768, 512, 256))
    tk = _pick(K, (1024, 1408, 512, 384, 256))
    tn = _pick(N, (1024, 1408, 512, 384, 256, 128))

    def body(a_ref, g_ref, o_ref):
        @pl.when(pl.program_id(2) == 0)
        def _():
            o_ref[...] = jnp.zeros_like(o_ref)

        o_ref[...] += _dot_tn(a_ref[...].astype(bf16), g_ref[...].astype(bf16))

    return pl.pallas_call(
        body, name=name, grid=(K // tk, N // tn, M // tm),
        in_specs=[pl.BlockSpec((tm, tk), lambda k, j, m: (m, k)),
                  pl.BlockSpec((tm, tn), lambda k, j, m: (m, j))],
        out_specs=pl.BlockSpec((tk, tn), lambda k, j, m: (k, j)),
        out_shape=jax.ShapeDtypeStruct((K, N), f32),
        compiler_params=_cparams(("parallel", "parallel", "arbitrary")),
    )(a, g)


def _rstd(x):
    return lax.rsqrt(jnp.mean(x * x, axis=-1, keepdims=True) + EPS)


def _rms_bwd_math(x, g, dy):
    r = _rstd(x)
    xh = x * r
    dn = dy * g
    dx = r * (dn - xh * jnp.mean(dn * xh, axis=-1, keepdims=True))
    return dx, dy * xh


def _rms_fwd(x, g, out_dtype, name):
    M, K = x.shape
    tm = _rt(M)

    def body(x_ref, g_ref, o_ref):
        xv = x_ref[...]
        o_ref[...] = (xv * _rstd(xv) * g_ref[...]).astype(out_dtype)

    return pl.pallas_call(
        body, name=name, grid=(M // tm,), in_specs=[_row(K, tm), _full((1, K))],
        out_specs=_row(K, tm), out_shape=jax.ShapeDtypeStruct((M, K), out_dtype),
        compiler_params=_cparams(("parallel",)),
    )(x, g)


def _rms_bwd(x, g, dy, out_dtype, name, residual=None):
    M, K = x.shape
    tm = _rt(M)
    has_res = residual is not None

    def body(*refs):
        if has_res:
            x_ref, g_ref, dy_ref, r_ref, dx_ref, dg_ref = refs
        else:
            x_ref, g_ref, dy_ref, dx_ref, dg_ref = refs

        @pl.when(pl.program_id(0) == 0)
        def _():
            dg_ref[...] = jnp.zeros_like(dg_ref)

        dx, dgp = _rms_bwd_math(x_ref[...], g_ref[...], dy_ref[...].astype(f32))
        if has_res:
            dx = dx + r_ref[...]
        dx_ref[...] = dx.astype(out_dtype)
        dg_ref[...] += jnp.sum(dgp, axis=0, keepdims=True)

    ins = [x, g, dy] + ([residual] if has_res else [])
    in_specs = [_row(K, tm), _full((1, K)), _row(K, tm)] + ([_row(K, tm)] if has_res else [])
    return pl.pallas_call(
        body, name=name, grid=(M // tm,), in_specs=in_specs,
        out_specs=[_row(K, tm), _full((1, K))],
        out_shape=[jax.ShapeDtypeStruct((M, K), out_dtype), jax.ShapeDtypeStruct((1, K), f32)],
        compiler_params=_cparams(("arbitrary",)),
    )(*ins)


def _resid_norm(h0, mix, g2, g3):
    M, K = h0.shape
    tm = _rt(M)

    def body(h_ref, m_ref, g2_ref, g3_ref, h1_ref, hn_ref):
        mv = m_ref[...]
        h1 = h_ref[...] + mv * _rstd(mv) * g2_ref[...]
        h1_ref[...] = h1
        hn_ref[...] = (h1 * _rstd(h1) * g3_ref[...]).astype(bf16)

    return pl.pallas_call(
        body, name="resid_norm", grid=(M // tm,),
        in_specs=[_row(K, tm), _row(K, tm), _full((1, K)), _full((1, K))],
        out_specs=[_row(K, tm), _row(K, tm)],
        out_shape=[jax.ShapeDtypeStruct((M, K), f32), jax.ShapeDtypeStruct((M, K), bf16)],
        compiler_params=_cparams(("parallel",)),
    )(h0, mix, g2, g3)


def _final(h1, down, g4, tgt, n_real):
    M, K = h1.shape
    tm = _rt(M)
    nt = M // tm

    def body(h_ref, d_ref, g_ref, t_ref, dh_ref, dd_ref, dg_ref, ls_ref, acc_ref):
        i = pl.program_id(0)

        @pl.when(i == 0)
        def _():
            dg_ref[...] = jnp.zeros_like(dg_ref)
            acc_ref[...] = jnp.zeros_like(acc_ref)

        dv = d_ref[...]
        g = g_ref[...]
        r = _rstd(dv)
        n = dv * r
        h2 = h_ref[...] + n * g
        rows = i * tm + lax.broadcasted_iota(jnp.int32, (tm, 1), 0)
        mask = ((rows >= N_META) & (rows < n_real)).astype(f32)
        diff = (h2 - t_ref[...]) * mask
        acc_ref[...] += jnp.sum(diff * diff, axis=0, keepdims=True)
        dh = diff * (1.0 / K)
        dh_ref[...] = dh
        dn = dh * g
        dd_ref[...] = (r * (dn - n * jnp.mean(dn * n, axis=-1, keepdims=True))).astype(bf16)
        dg_ref[...] += jnp.sum(dh * n, axis=0, keepdims=True)

        @pl.when(i == nt - 1)
        def _():
            ls_ref[...] = jnp.zeros((1, LANES), f32) + jnp.sum(acc_ref[...]) * (0.5 / K)

    return pl.pallas_call(
        body, name="final_loss", grid=(nt,),
        in_specs=[_row(K, tm), _row(K, tm), _full((1, K)), _row(K, tm)],
        out_specs=[_row(K, tm), _row(K, tm), _full((1, K)), _full((1, LANES))],
        out_shape=[jax.ShapeDtypeStruct((M, K), f32), jax.ShapeDtypeStruct((M, K), bf16),
                   jax.ShapeDtypeStruct((1, K), f32), jax.ShapeDtypeStruct((1, LANES), f32)],
        scratch_shapes=[pltpu.VMEM((1, K), f32)],
        compiler_params=_cparams(("arbitrary",)),
    )(h1, down, g4, tgt)


def _mid_bwd(h1, g3, d_hn2, dh2, mix, g2):
    M, K = h1.shape
    tm = _rt(M)

    def body(h_ref, g3_ref, dn_ref, dh2_ref, m_ref, g2_ref, dh1_ref, dm_ref, dg3_ref, dg2_ref):
        @pl.when(pl.program_id(0) == 0)
        def _():
            dg3_ref[...] = jnp.zeros_like(dg3_ref)
            dg2_ref[...] = jnp.zeros_like(dg2_ref)

        dx, dgp = _rms_bwd_math(h_ref[...], g3_ref[...], dn_ref[...])
        dh1 = dh2_ref[...] + dx
        dh1_ref[...] = dh1
        dg3_ref[...] += jnp.sum(dgp, axis=0, keepdims=True)
        dm, dgp2 = _rms_bwd_math(m_ref[...], g2_ref[...], dh1)
        dm_ref[...] = dm.astype(bf16)
        dg2_ref[...] += jnp.sum(dgp2, axis=0, keepdims=True)

    return pl.pallas_call(
        body, name="mid_bwd", grid=(M // tm,),
        in_specs=[_row(K, tm), _full((1, K)), _row(K, tm), _row(K, tm), _row(K, tm), _full((1, K))],
        out_specs=[_row(K, tm), _row(K, tm), _full((1, K)), _full((1, K))],
        out_shape=[jax.ShapeDtypeStruct((M, K), f32), jax.ShapeDtypeStruct((M, K), bf16),
                   jax.ShapeDtypeStruct((1, K), f32), jax.ShapeDtypeStruct((1, K), f32)],
        compiler_params=_cparams(("arbitrary",)),
    )(h1, g3, d_hn2, dh2, mix, g2)


def _conv_taps(ext_ref, w_ref, kw, tm, first):
    u = None
    for k in range(kw):
        t = ext_ref[pl.ds(first + k, tm), :] * w_ref[k:k + 1, :]
        u = t if u is None else u + t
    return u


def _fill_prev(ext_ref, x_ref, halo_ref, i, tm):
    ext_ref[0:SUBLANES, :] = jnp.where(i == 0, 0.0, halo_ref[...])
    ext_ref[SUBLANES:SUBLANES + tm, :] = x_ref[...]


def _prev_spec(tm, tc, col_of, row_axis, reversed_tiles=0):
    def imap(*ids):
        i = ids[row_axis]
        if reversed_tiles:
            i = reversed_tiles - 1 - i
        return (jnp.maximum(i * (tm // SUBLANES) - 1, 0), col_of(*ids))
    return pl.BlockSpec((SUBLANES, tc), imap)


def _conv_dx_carry(edu_ref, du, w_ref, kw, tm, first_step):
    @pl.when(first_step)
    def _():
        edu_ref[tm:tm + SUBLANES, :] = jnp.zeros((SUBLANES, edu_ref.shape[1]), f32)

    edu_ref[0:tm, :] = du
    acc = None
    for k in range(kw):
        t = edu_ref[pl.ds(kw - 1 - k, tm), :] * w_ref[k:k + 1, :]
        acc = t if acc is None else acc + t
    edu_ref[tm:tm + SUBLANES, :] = edu_ref[0:SUBLANES, :]
    return acc


def _ssm_conv_fwd(xbc, w, b):
    M, C = xbc.shape
    tm, tc, kw = ROW_TILE, C, SSM_CONV

    def body(x_ref, h_ref, w_ref, b_ref, o_ref, ext_ref):
        _fill_prev(ext_ref, x_ref, h_ref, pl.program_id(0), tm)
        u = _conv_taps(ext_ref, w_ref, kw, tm, SUBLANES - (kw - 1)) + b_ref[...]
        o_ref[...] = _silu(u)

    return pl.pallas_call(
        body, name="ssm_conv_fwd", grid=(M // tm, C // tc),
        in_specs=[pl.BlockSpec((tm, tc), lambda i, j: (i, j)),
                  _prev_spec(tm, tc, lambda i, j: j, 0),
                  pl.BlockSpec((SUBLANES, tc), lambda i, j: (0, j)),
                  pl.BlockSpec((1, tc), lambda i, j: (0, j))],
        out_specs=pl.BlockSpec((tm, tc), lambda i, j: (i, j)),
        out_shape=jax.ShapeDtypeStruct((M, C), f32),
        scratch_shapes=[pltpu.VMEM((tm + SUBLANES, tc), f32)],
        compiler_params=_cparams(("parallel", "parallel")),
    )(xbc, xbc, w, b)


def _ssm_conv_bwd(xbc, w, b, dout):
    M, C = xbc.shape
    tm, tc, kw = ROW_TILE, C // 3, SSM_CONV
    nt = M // tm

    def body(x_ref, h_ref, w_ref, b_ref, d_ref, dx_ref, dw_ref, db_ref, ext_ref, edu_ref):
        i = pl.program_id(1)

        @pl.when(i == 0)
        def _():
            dw_ref[...] = jnp.zeros_like(dw_ref)
            db_ref[...] = jnp.zeros_like(db_ref)

        _fill_prev(ext_ref, x_ref, h_ref, nt - 1 - i, tm)
        first = SUBLANES - (kw - 1)
        u = _conv_taps(ext_ref, w_ref, kw, tm, first) + b_ref[...]
        du = d_ref[...] * _dsilu(u)
        db_ref[...] += jnp.sum(du, axis=0, keepdims=True)
        for k in range(kw):
            dw_ref[k:k + 1, :] += jnp.sum(du * ext_ref[pl.ds(first + k, tm), :], axis=0, keepdims=True)
        dx_ref[...] = _conv_dx_carry(edu_ref, du, w_ref, kw, tm, i == 0).astype(bf16)

    tile = pl.BlockSpec((tm, tc), lambda j, i: (nt - 1 - i, j))
    return pl.pallas_call(
        body, name="ssm_conv_bwd", grid=(C // tc, nt),
        in_specs=[tile, _prev_spec(tm, tc, lambda j, i: j, 1, nt),
                  pl.BlockSpec((SUBLANES, tc), lambda j, i: (0, j)),
                  pl.BlockSpec((1, tc), lambda j, i: (0, j)), tile],
        out_specs=[tile, pl.BlockSpec((SUBLANES, tc), lambda j, i: (0, j)),
                   pl.BlockSpec((1, tc), lambda j, i: (0, j))],
        out_shape=[jax.ShapeDtypeStruct((M, C), bf16), jax.ShapeDtypeStruct((SUBLANES, C), f32),
                   jax.ShapeDtypeStruct((1, C), f32)],
        scratch_shapes=[pltpu.VMEM((tm + SUBLANES, tc), f32), pltpu.VMEM((tm + SUBLANES, tc), f32)],
        compiler_params=_cparams(("parallel", "arbitrary")),
    )(xbc, xbc, w, b, dout)


def _ffn_gate_fwd(up, w, b):
    M = up.shape[0]
    tm, tc, kw = ROW_TILE, D_FF // 2, FFN_CONV
    nc = D_FF // tc

    def body(xg_ref, hg_ref, xv_ref, hv_ref, wg_ref, wv_ref, bg_ref, bv_ref, o_ref, eg_ref, ev_ref):
        i = pl.program_id(0)
        first = SUBLANES - (kw - 1)
        _fill_prev(eg_ref, xg_ref, hg_ref, i, tm)
        _fill_prev(ev_ref, xv_ref, hv_ref, i, tm)
        ug = _conv_taps(eg_ref, wg_ref, kw, tm, first) + bg_ref[...]
        uv = _conv_taps(ev_ref, wv_ref, kw, tm, first) + bv_ref[...]
        o_ref[...] = (_silu(ug) * uv).astype(bf16)

    return pl.pallas_call(
        body, name="ffn_gate_fwd", grid=(M // tm, nc),
        in_specs=[pl.BlockSpec((tm, tc), lambda i, j: (i, j)),
                  _prev_spec(tm, tc, lambda i, j: j, 0),
                  pl.BlockSpec((tm, tc), lambda i, j: (i, j + nc)),
                  _prev_spec(tm, tc, lambda i, j: j + nc, 0),
                  pl.BlockSpec((SUBLANES, tc), lambda i, j: (0, j)),
                  pl.BlockSpec((SUBLANES, tc), lambda i, j: (0, j + nc)),
                  pl.BlockSpec((1, tc), lambda i, j: (0, j)),
                  pl.BlockSpec((1, tc), lambda i, j: (0, j + nc))],
        out_specs=pl.BlockSpec((tm, tc), lambda i, j: (i, j)),
        out_shape=jax.ShapeDtypeStruct((M, D_FF), bf16),
        scratch_shapes=[pltpu.VMEM((tm + SUBLANES, tc), f32), pltpu.VMEM((tm + SUBLANES, tc), f32)],
        compiler_params=_cparams(("parallel", "parallel")),
    )(up, up, up, up, w, w, b, b)


def _ffn_gate_bwd(up, w, b, d_act):
    M = up.shape[0]
    tm, tc, kw = ROW_TILE, D_FF // 2, FFN_CONV
    nc = D_FF // tc
    nt = M // tm

    def body(xg_ref, hg_ref, xv_ref, hv_ref, wg_ref, wv_ref, bg_ref, bv_ref, d_ref,
             dxg_ref, dxv_ref, dwg_ref, dwv_ref, dbg_ref, dbv_ref, eg_ref, ev_ref, edg_ref, edv_ref):
        i = pl.program_id(1)

        @pl.when(i == 0)
        def _():
            for r in (dwg_ref, dwv_ref, dbg_ref, dbv_ref):
                r[...] = jnp.zeros_like(r)

        first = SUBLANES - (kw - 1)
        _fill_prev(eg_ref, xg_ref, hg_ref, nt - 1 - i, tm)
        _fill_prev(ev_ref, xv_ref, hv_ref, nt - 1 - i, tm)
        ug = _conv_taps(eg_ref, wg_ref, kw, tm, first) + bg_ref[...]
        uv = _conv_taps(ev_ref, wv_ref, kw, tm, first) + bv_ref[...]
        da = d_ref[...]
        dug = da * uv * _dsilu(ug)
        duv = da * _silu(ug)
        dbg_ref[...] += jnp.sum(dug, axis=0, keepdims=True)
        dbv_ref[...] += jnp.sum(duv, axis=0, keepdims=True)
        for k in range(kw):
            dwg_ref[k:k + 1, :] += jnp.sum(dug * eg_ref[pl.ds(first + k, tm), :], axis=0, keepdims=True)
            dwv_ref[k:k + 1, :] += jnp.sum(duv * ev_ref[pl.ds(first + k, tm), :], axis=0, keepdims=True)
        dxg_ref[...] = _conv_dx_carry(edg_ref, dug, wg_ref, kw, tm, i == 0).astype(bf16)
        dxv_ref[...] = _conv_dx_carry(edv_ref, duv, wv_ref, kw, tm, i == 0).astype(bf16)

    tile_g = pl.BlockSpec((tm, tc), lambda j, i: (nt - 1 - i, j))
    tile_v = pl.BlockSpec((tm, tc), lambda j, i: (nt - 1 - i, j + nc))
    ext = pltpu.VMEM((tm + SUBLANES, tc), f32)
    return pl.pallas_call(
        body, name="ffn_gate_bwd", grid=(nc, nt),
        in_specs=[tile_g, _prev_spec(tm, tc, lambda j, i: j, 1, nt),
                  tile_v, _prev_spec(tm, tc, lambda j, i: j + nc, 1, nt),
                  pl.BlockSpec((SUBLANES, tc), lambda j, i: (0, j)),
                  pl.BlockSpec((SUBLANES, tc), lambda j, i: (0, j + nc)),
                  pl.BlockSpec((1, tc), lambda j, i: (0, j)),
                  pl.BlockSpec((1, tc), lambda j, i: (0, j + nc)),
                  tile_g],
        out_specs=[tile_g, tile_g,
                   pl.BlockSpec((SUBLANES, tc), lambda j, i: (0, j)),
                   pl.BlockSpec((SUBLANES, tc), lambda j, i: (0, j)),
                   pl.BlockSpec((1, tc), lambda j, i: (0, j)),
                   pl.BlockSpec((1, tc), lambda j, i: (0, j))],
        out_shape=[jax.ShapeDtypeStruct((M, D_FF), bf16), jax.ShapeDtypeStruct((M, D_FF), bf16),
                   jax.ShapeDtypeStruct((SUBLANES, D_FF), f32), jax.ShapeDtypeStruct((SUBLANES, D_FF), f32),
                   jax.ShapeDtypeStruct((1, D_FF), f32), jax.ShapeDtypeStruct((1, D_FF), f32)],
        scratch_shapes=[ext, ext, ext, ext],
        compiler_params=_cparams(("parallel", "arbitrary")),
    )(up, up, up, up, w, w, b, b, d_act)


def _rope_apply(blk, cos, sin):
    lane = lax.broadcasted_iota(jnp.int32, blk.shape, 1)
    half = QK_ROPE // 2
    partner = jnp.where(lane < half, pltpu.roll(blk, LANES - half, 1), pltpu.roll(blk, half, 1))
    return blk * cos + partner * sin


def _rope_unapply(d, cos, sin):
    t = d * sin
    lane = lax.broadcasted_iota(jnp.int32, d.shape, 1)
    half = QK_ROPE // 2
    partner = jnp.where(lane < half, pltpu.roll(t, LANES - half, 1), pltpu.roll(t, half, 1))
    return d * cos + partner


def _up_q_rope(qn, wuq, cos, sin):
    M, K = qn.shape
    tm = _pick(M, (768, 512, 256))

    def body(a_ref, b_ref, c_ref, s_ref, o_ref):
        r = _dot(a_ref[...], b_ref[...])
        o_ref[0, :, 0:QK_NOPE] = r[:, 0:QK_NOPE].astype(bf16)
        o_ref[0, :, QK_NOPE:QK_PAD] = _rope_apply(r[:, QK_NOPE:QK_PAD], c_ref[...], s_ref[...]).astype(bf16)

    return pl.pallas_call(
        body, name="up_q_rope", grid=(M // tm, MLA_HEADS),
        in_specs=[pl.BlockSpec((tm, K), lambda i, h: (i, 0)),
                  pl.BlockSpec((K, QK_PAD), lambda i, h: (0, h)),
                  pl.BlockSpec((tm, LANES), lambda i, h: (i, 0)),
                  pl.BlockSpec((tm, LANES), lambda i, h: (i, 0))],
        out_specs=pl.BlockSpec((1, tm, QK_PAD), lambda i, h: (h, i, 0)),
        out_shape=jax.ShapeDtypeStruct((MLA_HEADS, M, QK_PAD), bf16),
        compiler_params=_cparams(("parallel", "parallel")),
    )(qn, wuq, cos, sin)


def _up_kv_rope(kvn, wukv, kpe_raw, cos, sin):
    M, K = kvn.shape
    tm = _pick(M, (768, 512, 256))

    def body(a_ref, b_ref, pe_ref, c_ref, s_ref, k_ref, v_ref):
        r = _dot(a_ref[...], b_ref[...])
        k_ref[0, :, 0:QK_NOPE] = r[:, 0:QK_NOPE].astype(bf16)
        k_ref[0, :, QK_NOPE:QK_PAD] = _rope_apply(pe_ref[...], c_ref[...], s_ref[...]).astype(bf16)
        v_ref[0] = r[:, QK_NOPE:QK_NOPE + V_DIM].astype(bf16)

    return pl.pallas_call(
        body, name="up_kv_rope", grid=(M // tm, MLA_HEADS),
        in_specs=[pl.BlockSpec((tm, K), lambda i, h: (i, 0)),
                  pl.BlockSpec((K, QK_NOPE + V_DIM), lambda i, h: (0, h)),
                  pl.BlockSpec((tm, LANES), lambda i, h: (i, 0)),
                  pl.BlockSpec((tm, LANES), lambda i, h: (i, 0)),
                  pl.BlockSpec((tm, LANES), lambda i, h: (i, 0))],
        out_specs=[pl.BlockSpec((1, tm, QK_PAD), lambda i, h: (h, i, 0)),
                   pl.BlockSpec((1, tm, V_DIM), lambda i, h: (h, i, 0))],
        out_shape=[jax.ShapeDtypeStruct((MLA_HEADS, M, QK_PAD), bf16),
                   jax.ShapeDtypeStruct((MLA_HEADS, M, V_DIM), bf16)],
        compiler_params=_cparams(("parallel", "parallel")),
    )(kvn, wukv, kpe_raw, cos, sin)


def _rope_q_bwd(dq, cos, sin):
    M = dq.shape[1]
    tm = _rt(M)

    def body(d_ref, c_ref, s_ref, o_ref):
        c, s = c_ref[...], s_ref[...]
        for h in range(MLA_HEADS):
            o_ref[:, h * QK_PAD:h * QK_PAD + QK_NOPE] = d_ref[h, :, 0:QK_NOPE].astype(bf16)
            o_ref[:, h * QK_PAD + QK_NOPE:(h + 1) * QK_PAD] = _rope_unapply(
                d_ref[h, :, QK_NOPE:QK_PAD], c, s).astype(bf16)

    return pl.pallas_call(
        body, name="rope_q_bwd", grid=(M // tm,),
        in_specs=[pl.BlockSpec((MLA_HEADS, tm, QK_PAD), lambda i: (0, i, 0)),
                  _row(LANES, tm), _row(LANES, tm)],
        out_specs=_row(MLA_HEADS * QK_PAD, tm),
        out_shape=jax.ShapeDtypeStruct((M, MLA_HEADS * QK_PAD), bf16),
        compiler_params=_cparams(("parallel",)),
    )(dq, cos, sin)


def _rope_k_bwd(dk, dv, cos, sin):
    M = dk.shape[1]
    tm = _rt(M)
    w = QK_NOPE + V_DIM

    def body(dk_ref, dv_ref, c_ref, s_ref, o_ref, pe_ref):
        pe = None
        for h in range(MLA_HEADS):
            o_ref[:, h * w:h * w + QK_NOPE] = dk_ref[h, :, 0:QK_NOPE].astype(bf16)
            o_ref[:, h * w + QK_NOPE:(h + 1) * w] = dv_ref[h].astype(bf16)
            t = dk_ref[h, :, QK_NOPE:QK_PAD]
            pe = t if pe is None else pe + t
        pe_ref[...] = _rope_unapply(pe, c_ref[...], s_ref[...])

    return pl.pallas_call(
        body, name="rope_k_bwd", grid=(M // tm,),
        in_specs=[pl.BlockSpec((MLA_HEADS, tm, QK_PAD), lambda i: (0, i, 0)),
                  pl.BlockSpec((MLA_HEADS, tm, V_DIM), lambda i: (0, i, 0)),
                  _row(LANES, tm), _row(LANES, tm)],
        out_specs=[_row(MLA_HEADS * w, tm), _row(LANES, tm)],
        out_shape=[jax.ShapeDtypeStruct((M, MLA_HEADS * w), bf16), jax.ShapeDtypeStruct((M, LANES), f32)],
        compiler_params=_cparams(("parallel",)),
    )(dk, dv, cos, sin)


def _attn_tile(M):
    return _pick(M, (768, 512, 256))


def _col_to_row(col):
    return col.T[0:1, :]


def _flash_fwd(q, k, v):
    H, M, _ = q.shape
    T = _attn_tile(M)
    nq = M // T

    def body(q_ref, k_ref, v_ref, o_ref, lse_ref, m_sc, l_sc, acc_sc):
        i = pl.program_id(1)
        qv = q_ref[0]
        m_sc[...] = jnp.full_like(m_sc, NEG)
        l_sc[...] = jnp.zeros_like(l_sc)
        acc_sc[...] = jnp.zeros_like(acc_sc)

        def step(j, masked):
            off = pl.multiple_of(j * T, T)
            kt = k_ref[0, pl.ds(off, T), :]
            vt = v_ref[0, pl.ds(off, T), :]
            s = _dot_nt(qv, kt) * SOFTMAX_SCALE
            if masked:
                r = lax.broadcasted_iota(jnp.int32, (T, T), 0)
                c = lax.broadcasted_iota(jnp.int32, (T, T), 1)
                s = jnp.where(r >= c, s, NEG)
            m_prev = m_sc[...]
            m_new = jnp.maximum(m_prev, jnp.max(s, axis=1, keepdims=True))
            alpha = jnp.exp(m_prev - m_new)
            p = jnp.exp(s - m_new[:, 0:1])
            l_sc[...] = alpha * l_sc[...] + jnp.sum(p, axis=1, keepdims=True)
            acc_sc[...] = alpha * acc_sc[...] + _dot(p.astype(bf16), vt)
            m_sc[...] = m_new

        def loop_body(j, c):
            step(j, False)
            return c

        lax.fori_loop(0, i, loop_body, 0)
        step(i, True)
        l = l_sc[...]
        o_ref[...] = acc_sc[...] / l
        lse_ref[0, 0] = _col_to_row(m_sc[...] + jnp.log(l))

    return pl.pallas_call(
        body, name="flash_fwd", grid=(H, nq),
        in_specs=[pl.BlockSpec((1, T, QK_PAD), lambda h, i: (h, i, 0)),
                  pl.BlockSpec((1, M, QK_PAD), lambda h, i: (h, 0, 0)),
                  pl.BlockSpec((1, M, V_DIM), lambda h, i: (h, 0, 0))],
        out_specs=[pl.BlockSpec((T, V_DIM), lambda h, i: (i, h)),
                   pl.BlockSpec((1, 1, 1, T), lambda h, i: (h, i, 0, 0))],
        out_shape=[jax.ShapeDtypeStruct((M, H * V_DIM), f32),
                   jax.ShapeDtypeStruct((H, nq, 1, T), f32)],
        scratch_shapes=[pltpu.VMEM((T, LANES), f32), pltpu.VMEM((T, LANES), f32), pltpu.VMEM((T, V_DIM), f32)],
        compiler_params=_cparams(("parallel", "parallel")),
    )(q, k, v)


def _attn_delta(o, do):
    M = o.shape[0]
    H = MLA_HEADS
    T = _attn_tile(M)

    def body(o_ref, d_ref, dh_ref, dl_ref):
        dv = d_ref[...]
        dh_ref[0] = dv.astype(bf16)
        col = jnp.sum(o_ref[...] * dv, axis=1, keepdims=True) + jnp.zeros((T, LANES), f32)
        dl_ref[0, 0] = _col_to_row(col)

    return pl.pallas_call(
        body, name="attn_delta", grid=(M // T, H),
        in_specs=[pl.BlockSpec((T, V_DIM), lambda i, h: (i, h)),
                  pl.BlockSpec((T, V_DIM), lambda i, h: (i, h))],
        out_specs=[pl.BlockSpec((1, T, V_DIM), lambda i, h: (h, i, 0)),
                   pl.BlockSpec((1, 1, 1, T), lambda i, h: (h, i, 0, 0))],
        out_shape=[jax.ShapeDtypeStruct((H, M, V_DIM), bf16),
                   jax.ShapeDtypeStruct((H, M // T, 1, T), f32)],
        compiler_params=_cparams(("parallel", "parallel")),
    )(o, do)


def _flash_bwd(q, k, v, do, lse, delta):
    H, M, _ = q.shape
    T = _attn_tile(M)
    nq = M // T

    def body(q_ref, do_ref, lse_ref, dl_ref, k_ref, v_ref, dq_ref, dk_ref, dv_ref, dk_sc, dv_sc):
        j = pl.program_id(1)

        @pl.when(j == 0)
        def _():
            dq_ref[...] = jnp.zeros_like(dq_ref)

        kt = k_ref[0]
        vt = v_ref[0]
        dk_sc[...] = jnp.zeros_like(dk_sc)
        dv_sc[...] = jnp.zeros_like(dv_sc)

        def step(i, masked):
            off = pl.multiple_of(i * T, T)
            qt = q_ref[0, pl.ds(off, T), :]
            dot_ = do_ref[0, pl.ds(off, T), :]
            st = _dot_nt(kt, qt) * SOFTMAX_SCALE
            if masked:
                r = lax.broadcasted_iota(jnp.int32, (T, T), 0)
                c = lax.broadcasted_iota(jnp.int32, (T, T), 1)
                st = jnp.where(c >= r, st, NEG)
            pt = jnp.exp(st - lse_ref[0, i])
            dv_sc[...] += _dot(pt.astype(bf16), dot_)
            dpt = _dot_nt(vt, dot_)
            dst = (pt * (dpt - dl_ref[0, i]) * SOFTMAX_SCALE).astype(bf16)
            dk_sc[...] += _dot(dst, qt)
            dq_ref[0, pl.ds(off, T), :] += _dot_tn(dst, kt)

        step(j, True)

        def loop_body(i, c):
            step(i, False)
            return c

        lax.fori_loop(j + 1, nq, loop_body, 0)
        dk_ref[0] = dk_sc[...]
        dv_ref[0] = dv_sc[...]

    return pl.pallas_call(
        body, name="flash_bwd", grid=(H, nq),
        in_specs=[pl.BlockSpec((1, M, QK_PAD), lambda h, j: (h, 0, 0)),
                  pl.BlockSpec((1, M, V_DIM), lambda h, j: (h, 0, 0)),
                  pl.BlockSpec((1, nq, 1, T), lambda h, j: (h, 0, 0, 0)),
                  pl.BlockSpec((1, nq, 1, T), lambda h, j: (h, 0, 0, 0)),
                  pl.BlockSpec((1, T, QK_PAD), lambda h, j: (h, j, 0)),
                  pl.BlockSpec((1, T, V_DIM), lambda h, j: (h, j, 0))],
        out_specs=[pl.BlockSpec((1, M, QK_PAD), lambda h, j: (h, 0, 0)),
                   pl.BlockSpec((1, T, QK_PAD), lambda h, j: (h, j, 0)),
                   pl.BlockSpec((1, T, V_DIM), lambda h, j: (h, j, 0))],
        out_shape=[jax.ShapeDtypeStruct((H, M, QK_PAD), f32),
                   jax.ShapeDtypeStruct((H, M, QK_PAD), f32),
                   jax.ShapeDtypeStruct((H, M, V_DIM), f32)],
        scratch_shapes=[pltpu.VMEM((T, QK_PAD), f32), pltpu.VMEM((T, V_DIM), f32)],
        compiler_params=_cparams(("parallel", "arbitrary")),
    )(q, do, lse, delta, k, v)


def _dt_fwd(dt_raw, bias):
    M = dt_raw.shape[0]
    tm = _rt(M)

    def body(x_ref, b_ref, o_ref):
        u = x_ref[...] + b_ref[...]
        sp = jnp.maximum(u, 0.0) + jnp.log(1.0 + jnp.exp(-jnp.abs(u)))
        lane = lax.broadcasted_iota(jnp.int32, u.shape, 1)
        o_ref[...] = jnp.where(lane < SSM_HEADS, sp, 0.0)

    return pl.pallas_call(
        body, name="dt_fwd", grid=(M // tm,), in_specs=[_row(LANES, tm), _full((1, LANES))],
        out_specs=_row(LANES, tm), out_shape=jax.ShapeDtypeStruct((M, LANES), f32),
        compiler_params=_cparams(("parallel",)),
    )(dt_raw, bias)


def _dt_bwd(dt_raw, bias, ddt):
    M = dt_raw.shape[0]
    tm = _rt(M)

    def body(x_ref, b_ref, d_ref, o_ref, db_ref):
        @pl.when(pl.program_id(0) == 0)
        def _():
            db_ref[...] = jnp.zeros_like(db_ref)

        u = x_ref[...] + b_ref[...]
        lane = lax.broadcasted_iota(jnp.int32, u.shape, 1)
        g = jnp.where(lane < SSM_HEADS, d_ref[...] * _sigmoid(u), 0.0)
        o_ref[...] = g
        db_ref[...] += jnp.sum(g, axis=0, keepdims=True)

    return pl.pallas_call(
        body, name="dt_bwd", grid=(M // tm,),
        in_specs=[_row(LANES, tm), _full((1, LANES)), _row(LANES, tm)],
        out_specs=[_row(LANES, tm), _full((1, LANES))],
        out_shape=[jax.ShapeDtypeStruct((M, LANES), f32), jax.ShapeDtypeStruct((1, LANES), f32)],
        compiler_params=_cparams(("arbitrary",)),
    )(dt_raw, bias, ddt)


def _ssd_common(dtp_ref, dtt_ref, arow_ref, acol_ref):
    Q = CHUNK
    r = lax.broadcasted_iota(jnp.int32, (Q, Q), 0)
    c = lax.broadcasted_iota(jnp.int32, (Q, Q), 1)
    causal = r >= c
    tril = causal.astype(f32)
    triu = (r <= c).astype(f32)
    dt = dtp_ref[...]
    cs = _dot_hi(tril, dt * arow_ref[...])
    cst = _dot_hi(dtt_ref[...] * acol_ref[...], triu)
    return causal, triu, dt, cs, cst


def _ssd_fwd(xbc_c, dtp, dtt, a_row, a_col):
    M = xbc_c.shape[0]
    Q = CHUNK
    nch = M // Q

    def body(x_ref, dtp_ref, dtt_ref, arow_ref, acol_ref, y_ref, hin_ref, ht_sc):
        @pl.when(pl.program_id(0) == 0)
        def _():
            ht_sc[...] = jnp.zeros_like(ht_sc)

        causal, _, dt, cs, cst = _ssd_common(dtp_ref, dtt_ref, arow_ref, acol_ref)
        ecs = jnp.exp(cs)
        cs_last = cs[Q - 1:Q, :]
        dte = jnp.exp(cs_last - cs)
        e_last = jnp.exp(cs_last)
        for g in range(SSM_GROUPS):
            bg = x_ref[:, D_SSM + g * SSM_N:D_SSM + (g + 1) * SSM_N]
            cg = x_ref[:, D_SSM + D_BC + g * SSM_N:D_SSM + D_BC + (g + 1) * SSM_N]
            bg_b = bg.astype(bf16)
            cg_b = cg.astype(bf16)
            cb = _dot_nt(cg_b, bg_b)
            bgt_b = bg.T.astype(bf16)
            for hh in range(SSM_HPG):
                h = g * SSM_HPG + hh
                seg = cs[:, h:h + 1] - cst[h:h + 1, :]
                lm = jnp.exp(jnp.where(causal, seg, -jnp.inf))
                mb = (cb * lm).astype(bf16)
                xh = x_ref[:, h * SSM_P:(h + 1) * SSM_P]
                xdt = xh * dt[:, h:h + 1]
                hth = ht_sc[h]
                hin_ref[0, h] = hth
                y_off = _dot(cg_b, hth.astype(bf16)) * ecs[:, h:h + 1]
                y_ref[:, h * SSM_P:(h + 1) * SSM_P] = _dot(mb, xdt.astype(bf16)) + y_off
                st = _dot(bgt_b, (xdt * dte[:, h:h + 1]).astype(bf16))
                ht_sc[h] = hth * e_last[:, h:h + 1] + st

    return pl.pallas_call(
        body, name="ssd_fwd", grid=(nch,),
        in_specs=[pl.BlockSpec((Q, D_XBC), lambda c: (c, 0)),
                  pl.BlockSpec((Q, LANES), lambda c: (c, 0)),
                  pl.BlockSpec((SSM_HEADS, Q), lambda c: (0, c)),
                  _full((1, LANES)), _full((SSM_HEADS, LANES))],
        out_specs=[pl.BlockSpec((Q, D_SSM), lambda c: (c, 0)),
                   pl.BlockSpec((1, SSM_HEADS, SSM_N, SSM_P), lambda c: (c, 0, 0, 0))],
        out_shape=[jax.ShapeDtypeStruct((M, D_SSM), f32),
                   jax.ShapeDtypeStruct((nch, SSM_HEADS, SSM_N, SSM_P), f32)],
        scratch_shapes=[pltpu.VMEM((SSM_HEADS, SSM_N, SSM_P), f32)],
        compiler_params=_cparams(("arbitrary",)),
    )(xbc_c, dtp, dtt, a_row, a_col)


def _ssd_bwd(xbc_c, dtp, dtt, a_row, a_col, hin, dy, d_exp):
    M = xbc_c.shape[0]
    Q = CHUNK
    nch = M // Q
    rev = lambda c: nch - 1 - c

    def body(x_ref, dtp_ref, dtt_ref, arow_ref, acol_ref, hin_ref, dy_ref, dexp_ref,
             dx_ref, ddt_ref, da_ref, dht_sc):
        @pl.when(pl.program_id(0) == 0)
        def _():
            dht_sc[...] = jnp.zeros_like(dht_sc)
            da_ref[...] = jnp.zeros_like(da_ref)

        causal, triu, dt, cs, cst = _ssd_common(dtp_ref, dtt_ref, arow_ref, acol_ref)
        ecs = jnp.exp(cs)
        cs_last = cs[Q - 1:Q, :]
        dte = jnp.exp(cs_last - cs)
        e_last = jnp.exp(cs_last)
        lane = lax.broadcasted_iota(jnp.int32, (Q, LANES), 1)
        row = lax.broadcasted_iota(jnp.int32, (Q, LANES), 0)
        dcs = jnp.zeros((Q, LANES), f32)
        ddt = jnp.zeros((Q, LANES), f32)
        for g in range(SSM_GROUPS):
            b0 = D_SSM + g * SSM_N
            c0 = D_SSM + D_BC + g * SSM_N
            bg = x_ref[:, b0:b0 + SSM_N]
            cg = x_ref[:, c0:c0 + SSM_N]
            bg_b = bg.astype(bf16)
            cg_b = cg.astype(bf16)
            cgt_b = cg.T.astype(bf16)
            cb = _dot_nt(cg_b, bg_b)
            dg_acc = jnp.zeros((Q, Q), f32)
            dc_acc = jnp.zeros((Q, SSM_N), f32)
            db_acc = jnp.zeros((Q, SSM_N), f32)
            for hh in range(SSM_HPG):
                h = g * SSM_HPG + hh
                sl = slice(h * SSM_P, (h + 1) * SSM_P)
                seg = cs[:, h:h + 1] - cst[h:h + 1, :]
                lm = jnp.exp(jnp.where(causal, seg, -jnp.inf))
                mm = cb * lm
                xh = x_ref[:, sl]
                dth = dt[:, h:h + 1]
                xdt = xh * dth
                xdt_b = xdt.astype(bf16)
                dyh = dy_ref[:, sl]
                dyh_b = dyh.astype(bf16)
                ecs_h = ecs[:, h:h + 1]
                dte_h = dte[:, h:h + 1]
                hth = hin_ref[0, h]
                hth_b = hth.astype(bf16)
                dht = dht_sc[h]
                dht_b = dht.astype(bf16)
                y_off = _dot(cg_b, hth_b) * ecs_h
                dcs_h = jnp.sum(dyh * y_off, axis=1, keepdims=True)
                dye_b = (dyh * ecs_h).astype(bf16)
                dc_acc = dc_acc + _dot_nt(dye_b, hth_b)
                dht_new = dht * e_last[:, h:h + 1] + _dot(cgt_b, dye_b)
                dm = _dot_nt(dyh_b, xdt_b)
                w = dm * mm
                dcs_h = dcs_h + jnp.sum(w, axis=1, keepdims=True) - jnp.sum(w.T, axis=1, keepdims=True)
                dxdt = _dot_tn(mm.astype(bf16), dyh_b)
                dg_acc = dg_acc + dm * lm
                e = _dot(bg_b, dht_b)
                dxdt = dxdt + e * dte_h
                t = jnp.sum(e * xdt, axis=1, keepdims=True) * dte_h
                dcs_h = dcs_h - t
                dlast = jnp.sum(t) + jnp.sum(e_last[:, h:h + 1]) * jnp.sum(dht * hth)
                db_acc = db_acc + _dot_nt((xdt * dte_h).astype(bf16), dht_b)
                dht_sc[h] = dht_new
                dx_ref[:, sl] = dxdt * dth + dexp_ref[:, sl] * dyh
                onehot = lane == h
                ddt = ddt + jnp.where(onehot, jnp.sum(dxdt * xh, axis=1, keepdims=True), 0.0)
                dcs = dcs + jnp.where(onehot, dcs_h, 0.0) + jnp.where(onehot & (row == Q - 1), dlast, 0.0)
            dg_b = dg_acc.astype(bf16)
            dx_ref[:, c0:c0 + SSM_N] = dc_acc + _dot(dg_b, bg_b)
            dx_ref[:, b0:b0 + SSM_N] = db_acc + _dot_tn(dg_b, cg_b)
        da = _dot_hi(triu, dcs)
        ddt_ref[...] = ddt + da * arow_ref[...]
        da_ref[...] += jnp.sum(da * dt, axis=0, keepdims=True)

    return pl.pallas_call(
        body, name="ssd_bwd", grid=(nch,),
        in_specs=[pl.BlockSpec((Q, D_XBC), lambda c: (rev(c), 0)),
                  pl.BlockSpec((Q, LANES), lambda c: (rev(c), 0)),
                  pl.BlockSpec((SSM_HEADS, Q), lambda c: (0, rev(c))),
                  _full((1, LANES)), _full((SSM_HEADS, LANES)),
                  pl.BlockSpec((1, SSM_HEADS, SSM_N, SSM_P), lambda c: (rev(c), 0, 0, 0)),
                  pl.BlockSpec((Q, D_SSM), lambda c: (rev(c), 0)),
                  _full((1, D_SSM))],
        out_specs=[pl.BlockSpec((Q, D_XBC), lambda c: (rev(c), 0)),
                   pl.BlockSpec((Q, LANES), lambda c: (rev(c), 0)),
                   _full((1, LANES))],
        out_shape=[jax.ShapeDtypeStruct((M, D_XBC), f32), jax.ShapeDtypeStruct((M, LANES), f32),
                   jax.ShapeDtypeStruct((1, LANES), f32)],
        scratch_shapes=[pltpu.VMEM((SSM_HEADS, SSM_N, SSM_P), f32)],
        compiler_params=_cparams(("arbitrary",)),
    )(xbc_c, dtp, dtt, a_row, a_col, hin, dy, d_exp)


def _gate_norm_fwd(y, xbc_c, z, d_exp, g):
    M = y.shape[0]
    tm = _rt(M)
    gw = D_SSM // SSM_GROUPS

    def body(y_ref, x_ref, z_ref, d_ref, g_ref, o_ref):
        yg = (y_ref[...] + d_ref[...] * x_ref[...]) * _silu(z_ref[...])
        for gi in range(SSM_GROUPS):
            blk = yg[:, gi * gw:(gi + 1) * gw]
            o_ref[:, gi * gw:(gi + 1) * gw] = (blk * _rstd(blk) * g_ref[:, gi * gw:(gi + 1) * gw]).astype(bf16)

    return pl.pallas_call(
        body, name="gate_norm_fwd", grid=(M // tm,),
        in_specs=[_row(D_SSM, tm), _row(D_SSM, tm), _row(D_SSM, tm), _full((1, D_SSM)), _full((1, D_SSM))],
        out_specs=_row(D_SSM, tm), out_shape=jax.ShapeDtypeStruct((M, D_SSM), bf16),
        compiler_params=_cparams(("parallel",)),
    )(y, xbc_c, z, d_exp, g)


def _gate_norm_bwd(y, xbc_c, z, d_exp, g, dout, head_ind):
    M = y.shape[0]
    tm = _rt(M)
    nt = M // tm
    gw = D_SSM // SSM_GROUPS

    def body(y_ref, x_ref, z_ref, d_ref, g_ref, do_ref, ind_ref, dy_ref, dz_ref, dg_ref, dd_ref, ddc_sc):
        i = pl.program_id(0)

        @pl.when(i == 0)
        def _():
            dg_ref[...] = jnp.zeros_like(dg_ref)
            ddc_sc[...] = jnp.zeros_like(ddc_sc)

        zv = z_ref[...]
        xv = x_ref[...]
        s = _silu(zv)
        yd = y_ref[...] + d_ref[...] * xv
        yg = yd * s
        dov = do_ref[...]
        for gi in range(SSM_GROUPS):
            sl = slice(gi * gw, (gi + 1) * gw)
            dyg, dgp = _rms_bwd_math(yg[:, sl], g_ref[:, sl], dov[:, sl])
            dg_ref[:, sl] += jnp.sum(dgp, axis=0, keepdims=True)
            dyd = dyg * s[:, sl]
            dy_ref[:, sl] = dyd
            dz_ref[:, sl] = (dyg * yd[:, sl] * _dsilu(zv[:, sl])).astype(bf16)
            ddc_sc[:, sl] += jnp.sum(dyd * xv[:, sl], axis=0, keepdims=True)

        @pl.when(i == nt - 1)
        def _():
            dd_ref[...] = _dot_hi(ddc_sc[...], ind_ref[...])

    return pl.pallas_call(
        body, name="gate_norm_bwd", grid=(nt,),
        in_specs=[_row(D_SSM, tm), _row(D_SSM, tm), _row(D_SSM, tm), _full((1, D_SSM)), _full((1, D_SSM)),
                  _row(D_SSM, tm), _full((D_SSM, LANES))],
        out_specs=[_row(D_SSM, tm), _row(D_SSM, tm), _full((1, D_SSM)), _full((1, LANES))],
        out_shape=[jax.ShapeDtypeStruct((M, D_SSM), f32), jax.ShapeDtypeStruct((M, D_SSM), bf16),
                   jax.ShapeDtypeStruct((1, D_SSM), f32), jax.ShapeDtypeStruct((1, LANES), f32)],
        scratch_shapes=[pltpu.VMEM((1, D_SSM), f32)],
        compiler_params=_cparams(("arbitrary",)),
    )(y, xbc_c, z, d_exp, g, dout, head_ind)


_PEER_FLIPS = [(0, 0, 1), (0, 1, 0), (0, 1, 1), (1, 0, 0), (1, 0, 1), (1, 1, 0), (1, 1, 1)]


def _exchange(arrays, scatter, name):
    n = len(arrays)

    def body(*refs):
        ins, outs = refs[:n], refs[n:2 * n]
        send_sems, recv_sems, loc_sems = refs[2 * n:]
        x, y, c = lax.axis_index("x"), lax.axis_index("y"), lax.axis_index("c")
        me = 4 * x + 2 * y + c
        local = []
        for a in range(n):
            src = ins[a].at[me] if scatter else ins[a]
            lc = pltpu.make_async_copy(src, outs[a].at[me], loc_sems.at[a])
            lc.start()
            local.append(lc)
        remote = []
        for p, (fx, fy, fc) in enumerate(_PEER_FLIPS):
            tx = 1 - x if fx else x
            ty = 1 - y if fy else y
            tc = 1 - c if fc else c
            tgt = 4 * tx + 2 * ty + tc
            for a in range(n):
                src = ins[a].at[tgt] if scatter else ins[a]
                cp = pltpu.make_async_remote_copy(
                    src_ref=src, dst_ref=outs[a].at[me],
                    send_sem=send_sems.at[p * n + a], recv_sem=recv_sems.at[p * n + a],
                    device_id=(tx, ty, tc), device_id_type=_MESH)
                cp.start()
                remote.append(cp)
        for cp in remote:
            cp.wait()
        for lc in local:
            lc.wait()

    any_spec = pl.BlockSpec(memory_space=pl.ANY)
    out_shape = []
    for arr in arrays:
        shp = arr.shape if scatter else (N_DEV,) + arr.shape
        out_shape.append(jax.ShapeDtypeStruct(shp, arr.dtype))
    return pl.pallas_call(
        body, name=name, in_specs=[any_spec] * n, out_specs=[any_spec] * n, out_shape=out_shape,
        scratch_shapes=[pltpu.SemaphoreType.DMA((7 * n,)), pltpu.SemaphoreType.DMA((7 * n,)),
                        pltpu.SemaphoreType.DMA((n,))],
    )(*arrays)


def _adamw(parts, w, m, v, name):
    R, C = w.shape
    tr = _pick(R, (PACK_ROW_TILE, 16, 8))
    c1 = 1.0 - ADAM_B1 ** ADAM_STEP
    c2 = 1.0 - ADAM_B2 ** ADAM_STEP

    def body(p_ref, w_ref, m_ref, v_ref, g_ref, d_ref, nm_ref, nv_ref):
        g = p_ref[0].astype(f32)
        for s in range(1, N_DEV):
            g = g + p_ref[s].astype(f32)
        mn = ADAM_B1 * m_ref[...] + (1.0 - ADAM_B1) * g
        vn = ADAM_B2 * v_ref[...] + (1.0 - ADAM_B2) * (g * g)
        m_hat = mn / c1
        v_hat = vn / c2
        g_ref[...] = g
        d_ref[...] = -ADAM_LR * (m_hat / (jnp.sqrt(v_hat) + ADAM_EPS) + ADAM_WD * w_ref[...])
        nm_ref[...] = mn
        nv_ref[...] = vn

    spec = pl.BlockSpec((tr, C), lambda i: (i, 0))
    return pl.pallas_call(
        body, name=name, grid=(R // tr,),
        in_specs=[pl.BlockSpec((N_DEV, tr, C), lambda i: (0, i, 0)), spec, spec, spec],
        out_specs=[spec] * 4, out_shape=[jax.ShapeDtypeStruct((R, C), f32)] * 4,
        compiler_params=_cparams(("parallel",)),
    )(parts, w, m, v)


def _flat_rows(a, lead_ndim):
    lead = a.shape[:lead_ndim]
    n = int(np.prod(a.shape[lead_ndim:]))
    a = a.reshape(lead + (n,))
    pad = (-n) % PACK_W
    if pad:
        a = jnp.pad(a, [(0, 0)] * lead_ndim + [(0, pad)])
    return a.reshape(lead + ((n + pad) // PACK_W, PACK_W))


def _pack(arrays, lead_ndim, total_rows, dtype):
    rows = [_flat_rows(a.astype(dtype), lead_ndim) for a in arrays]
    cat = jnp.concatenate(rows, axis=lead_ndim)
    pad = total_rows - cat.shape[lead_ndim]
    if pad:
        cat = jnp.pad(cat, [(0, 0)] * lead_ndim + [(0, pad), (0, 0)])
    return cat


def _unpack(buf, shapes, lead_ndim):
    out = []
    r = 0
    lead = buf.shape[:lead_ndim]
    for shp in shapes:
        n = int(np.prod(shp))
        nr = -(-n // PACK_W)
        piece = lax.slice_in_dim(buf, r, r + nr, axis=lead_ndim)
        piece = piece.reshape(lead + (nr * PACK_W,))
        piece = lax.slice_in_dim(piece, 0, n, axis=lead_ndim)
        out.append(piece.reshape(lead + tuple(shp)))
        r += nr
    return out


def _round_up(n, m):
    return -(-n // m) * m


def kernel(x, meta_tokens, norm_mix_pre, norm_mix_post, norm_ffn_pre, norm_ffn_post, w_in, q_a_norm, w_uq, kv_a_norm, w_ukv, attn_out_norm, ssm_conv_w, ssm_conv_b, ssm_dt_bias, ssm_A_log, ssm_D, ssm_norm, w_out, w_up, ffn_conv_w, ffn_conv_b, w_down, loss_target, m_meta_tokens, m_norm_mix_pre, m_norm_mix_post, m_norm_ffn_pre, m_norm_ffn_post, m_w_in, m_q_a_norm, m_w_uq, m_kv_a_norm, m_w_ukv, m_attn_out_norm, m_ssm_conv_w, m_ssm_conv_b, m_ssm_dt_bias, m_ssm_A_log, m_ssm_D, m_ssm_norm, m_w_out, m_w_up, m_ffn_conv_w, m_ffn_conv_b, m_w_down, v_meta_tokens, v_norm_mix_pre, v_norm_mix_post, v_norm_ffn_pre, v_norm_ffn_post, v_w_in, v_q_a_norm, v_w_uq, v_kv_a_norm, v_w_ukv, v_attn_out_norm, v_ssm_conv_w, v_ssm_conv_b, v_ssm_dt_bias, v_ssm_A_log, v_ssm_D, v_ssm_norm, v_w_out, v_w_up, v_ffn_conv_w, v_ffn_conv_b, v_w_down):
    seq = x.shape[1]
    n_real = N_META + seq
    Lp = _round_up(n_real, 768) if n_real > 2048 else _round_up(n_real, ROW_TILE)
    D = D_MODEL

    sharded_w = [w_in, w_uq, w_ukv, w_out, w_up, w_down]
    sharded_s = [meta_tokens, ssm_conv_w, ffn_conv_w]
    sharded_names = sharded_w + sharded_s
    sharded_m = [m_w_in, m_w_uq, m_w_ukv, m_w_out, m_w_up, m_w_down, m_meta_tokens, m_ssm_conv_w, m_ffn_conv_w]
    sharded_v = [v_w_in, v_w_uq, v_w_ukv, v_w_out, v_w_up, v_w_down, v_meta_tokens, v_ssm_conv_w, v_ffn_conv_w]
    repl_w = [norm_mix_pre, norm_mix_post, norm_ffn_pre, norm_ffn_post, q_a_norm, kv_a_norm, attn_out_norm,
              ssm_conv_b, ssm_dt_bias, ssm_A_log, ssm_D, ssm_norm, ffn_conv_b]
    repl_m = [m_norm_mix_pre, m_norm_mix_post, m_norm_ffn_pre, m_norm_ffn_post, m_q_a_norm, m_kv_a_norm,
              m_attn_out_norm, m_ssm_conv_b, m_ssm_dt_bias, m_ssm_A_log, m_ssm_D, m_ssm_norm, m_ffn_conv_b]
    repl_v = [v_norm_mix_pre, v_norm_mix_post, v_norm_ffn_pre, v_norm_ffn_post, v_q_a_norm, v_kv_a_norm,
              v_attn_out_norm, v_ssm_conv_b, v_ssm_dt_bias, v_ssm_A_log, v_ssm_D, v_ssm_norm, v_ffn_conv_b]

    big_rows = _round_up(sum(-(-int(np.prod(a.shape)) // PACK_W) for a in sharded_w), PACK_ROW_TILE)
    small_rows = _round_up(sum(-(-int(np.prod(a.shape)) // PACK_W) for a in sharded_s), 16)
    wb = _pack(sharded_w, 0, big_rows, bf16)
    ws = _pack(sharded_s, 0, small_rows, f32)
    wb_all, ws_all = _exchange([wb, ws], False, "gather_weights")
    g_w_in, g_w_uq, g_w_ukv, g_w_out, g_w_up, g_w_down = _unpack(wb_all, [a.shape for a in sharded_w], 1)
    g_meta, g_sconv, g_fconv = _unpack(ws_all, [a.shape for a in sharded_s], 1)

    def cols(gathered):
        t = gathered[:, 0]
        return jnp.transpose(t, (1, 0, 2)).reshape(t.shape[1], N_DEV * t.shape[2])

    win = cols(g_w_in)
    o = np.cumsum((0, Q_RANK, KV_RANK, QK_ROPE, D_SSM, D_XBC, SSM_HEADS))
    w_q, w_kv = win[:, o[0]:o[1]], win[:, o[1]:o[2]]
    w_rope = jnp.pad(win[:, o[2]:o[3]], ((0, 0), (0, LANES - QK_ROPE)))
    w_z, w_xbc = win[:, o[3]:o[4]], win[:, o[4]:o[5]]
    w_dt = jnp.pad(win[:, o[5]:o[6]], ((0, 0), (0, LANES - SSM_HEADS)))
    wuq = g_w_uq.reshape(Q_RANK, MLA_HEADS, QK_NOPE + QK_ROPE)
    wuq = jnp.pad(wuq, ((0, 0), (0, 0), (0, QK_PAD - QK_NOPE - QK_ROPE))).reshape(Q_RANK, MLA_HEADS * QK_PAD)
    wukv = g_w_ukv.reshape(KV_RANK, MLA_HEADS * (QK_NOPE + V_DIM))
    wout = g_w_out.reshape(D_ATTN + D_SSM, D)
    wout_a, wout_s = wout[:D_ATTN], wout[D_ATTN:]
    wup = cols(g_w_up)
    wdown = g_w_down.reshape(D_FF, D)
    meta_full = jnp.transpose(g_meta, (1, 0, 2)).reshape(N_META, D)
    sconv_w = jnp.pad(cols(g_sconv), ((0, SUBLANES - SSM_CONV), (0, 0)))
    fconv_w = jnp.pad(cols(g_fconv), ((0, SUBLANES - FFN_CONV), (0, 0)))

    pos = jnp.arange(Lp, dtype=f32)
    inv = ROPE_THETA ** (-jnp.arange(0, QK_ROPE, 2, dtype=f32) / QK_ROPE)
    ang = pos[:, None] * inv[None, :]
    cs_, sn_ = jnp.cos(ang), jnp.sin(ang)
    zpad = jnp.zeros((Lp, LANES - QK_ROPE), f32)
    cos_t = jnp.concatenate([cs_, cs_, zpad], axis=1)
    sin_t = jnp.concatenate([-sn_, sn_, zpad], axis=1)
    dt_bias_p = jnp.pad(ssm_dt_bias, ((0, 0), (0, LANES - SSM_HEADS)))
    a_neg = -jnp.exp(ssm_A_log)
    a_row = jnp.pad(a_neg, ((0, 0), (0, LANES - SSM_HEADS)))
    a_col = jnp.broadcast_to(a_neg.reshape(SSM_HEADS, 1), (SSM_HEADS, LANES))
    d_exp = jnp.repeat(ssm_D, SSM_P, axis=1)
    head_ind = (jnp.arange(D_SSM)[:, None] // SSM_P == jnp.arange(LANES)[None, :]).astype(f32)

    xb = x[0]
    h0 = jnp.concatenate([meta_full, xb, jnp.zeros((Lp - n_real, D), f32)], axis=0)
    tgt = jnp.pad(loss_target[0], ((N_META, Lp - n_real), (0, 0)))
    hn1 = _rms_fwd(h0, norm_mix_pre, bf16, "norm_mix_pre")
    q_c = _mm([(hn1, w_q)], f32, False, "proj_q")
    kv_c = _mm([(hn1, w_kv)], f32, False, "proj_kv")
    kpe_raw = _mm([(hn1, w_rope)], f32, False, "proj_rope")
    z = _mm([(hn1, w_z)], f32, False, "proj_z")
    xbc = _mm([(hn1, w_xbc)], f32, False, "proj_xbc")
    dt_raw = _mm([(hn1, w_dt)], f32, False, "proj_dt")

    qn = _rms_fwd(q_c, q_a_norm, bf16, "norm_q")
    kvn = _rms_fwd(kv_c, kv_a_norm, bf16, "norm_kv")
    qh = _up_q_rope(qn, wuq, cos_t, sin_t)
    kh, vh = _up_kv_rope(kvn, wukv, kpe_raw, cos_t, sin_t)
    attn, lse = _flash_fwd(qh, kh, vh)
    an = _rms_fwd(attn, attn_out_norm, bf16, "norm_attn_out")

    xbc_c = _ssm_conv_fwd(xbc, sconv_w, ssm_conv_b)
    dtp = _dt_fwd(dt_raw, dt_bias_p)
    dtt = jnp.transpose(dtp[:, :SSM_HEADS])
    y_ssd, hin = _ssd_fwd(xbc_c, dtp, dtt, a_row, a_col)
    ssm = _gate_norm_fwd(y_ssd, xbc_c, z, d_exp, ssm_norm)

    mix = _mm([(an, wout_a), (ssm, wout_s)], f32, False, "out_proj")
    h1, hn2 = _resid_norm(h0, mix, norm_mix_post, norm_ffn_pre)
    up = _mm([(hn2, wup)], f32, False, "ffn_up")
    act = _ffn_gate_fwd(up, fconv_w, ffn_conv_b)
    down = _mm([(act, wdown)], f32, False, "ffn_down")
    dh2, d_down, dg_ffn_post, loss_part = _final(h1, down, norm_ffn_post, tgt, n_real)

    d_act = _mm([(d_down, wdown)], f32, True, "ffn_down_dx")
    dw_down = _mm_tn(act, d_down, "ffn_down_dw")
    dup_g, dup_v, dwc_g, dwc_v, dbc_g, dbc_v = _ffn_gate_bwd(up, fconv_w, ffn_conv_b, d_act)
    d_hn2 = _mm([(dup_g, wup[:, :D_FF]), (dup_v, wup[:, D_FF:])], f32, True, "ffn_up_dx")
    dw_up = jnp.concatenate([_mm_tn(hn2, dup_g, "ffn_up_dw_g"), _mm_tn(hn2, dup_v, "ffn_up_dw_v")], axis=1)
    dh1, d_mix, dg_ffn_pre, dg_mix_post = _mid_bwd(h1, norm_ffn_pre, d_hn2, dh2, mix, norm_mix_post)
    d_an = _mm([(d_mix, wout_a)], f32, True, "out_proj_dx_a")
    d_ssm = _mm([(d_mix, wout_s)], f32, True, "out_proj_dx_s")
    dw_out = jnp.concatenate([_mm_tn(an, d_mix, "out_proj_dw_a"), _mm_tn(ssm, d_mix, "out_proj_dw_s")], axis=0)

    d_attn, dg_attn_out = _rms_bwd(attn, attn_out_norm, d_an, f32, "norm_attn_out_bwd")
    do_h, delta = _attn_delta(attn, d_attn)
    dqh, dkh, dvh = _flash_bwd(qh, kh, vh, do_h, lse, delta)
    dq_full = _rope_q_bwd(dqh, cos_t, sin_t)
    dkv_full, d_kpe_raw = _rope_k_bwd(dkh, dvh, cos_t, sin_t)
    d_qn = _mm([(dq_full, wuq)], f32, True, "up_q_dx")
    dw_uq = _mm_tn(qn, dq_full, "up_q_dw")
    d_kvn = _mm([(dkv_full, wukv)], f32, True, "up_kv_dx")
    dw_ukv = _mm_tn(kvn, dkv_full, "up_kv_dw")
    d_q_c, dg_q = _rms_bwd(q_c, q_a_norm, d_qn, bf16, "norm_q_bwd")
    d_kv_c, dg_kv = _rms_bwd(kv_c, kv_a_norm, d_kvn, bf16, "norm_kv_bwd")

    dy_ssd, dz, dg_ssm, dd_heads = _gate_norm_bwd(y_ssd, xbc_c, z, d_exp, ssm_norm, d_ssm, head_ind)
    d_xbc_c, ddt, da_heads = _ssd_bwd(xbc_c, dtp, dtt, a_row, a_col, hin, dy_ssd, d_exp)
    d_xbc, dw_sconv, db_sconv = _ssm_conv_bwd(xbc, sconv_w, ssm_conv_b, d_xbc_c)
    d_dt_raw, d_dt_bias = _dt_bwd(dt_raw, dt_bias_p, ddt)

    segs = [(d_q_c, w_q), (d_kv_c, w_kv), (d_kpe_raw, w_rope), (dz, w_z), (d_xbc, w_xbc), (d_dt_raw, w_dt)]
    d_hn1 = _mm(segs, f32, True, "proj_dx")
    dw_q = _mm_tn(hn1, d_q_c, "proj_dw_q")
    dw_kv = _mm_tn(hn1, d_kv_c, "proj_dw_kv")
    dw_rope = _mm_tn(hn1, d_kpe_raw, "proj_dw_rope")
    dw_z = _mm_tn(hn1, dz, "proj_dw_z")
    dw_xbc = _mm_tn(hn1, d_xbc, "proj_dw_xbc")
    dw_dt = _mm_tn(hn1, d_dt_raw, "proj_dw_dt")
    dh0, dg_mix_pre = _rms_bwd(h0, norm_mix_pre, d_hn1, f32, "norm_mix_pre_bwd", residual=dh1)

    grad_x = dh0[N_META:n_real][None]
    d_meta = dh0[:N_META]

    dw_in = jnp.concatenate([dw_q, dw_kv, dw_rope[:, :QK_ROPE], dw_z, dw_xbc, dw_dt[:, :SSM_HEADS]], axis=1)

    def col_blocks(gm):
        r, cc = gm.shape
        return jnp.transpose(gm.reshape(r, N_DEV, cc // N_DEV), (1, 0, 2))

    dw_uq3 = dw_uq.reshape(Q_RANK, MLA_HEADS, QK_PAD)[:, :, :QK_NOPE + QK_ROPE]
    dest_blocks = [
        col_blocks(dw_in),
        dw_uq3.reshape(N_DEV, Q_RANK // N_DEV, MLA_HEADS, QK_NOPE + QK_ROPE),
        dw_ukv.reshape(N_DEV, KV_RANK // N_DEV, MLA_HEADS, QK_NOPE + V_DIM),
        dw_out.reshape(N_DEV, (D_ATTN + D_SSM) // N_DEV, D),
        col_blocks(dw_up),
        dw_down.reshape(N_DEV, D_FF // N_DEV, D),
        col_blocks(d_meta),
        col_blocks(dw_sconv[:SSM_CONV]),
        col_blocks(jnp.concatenate([dwc_g, dwc_v], axis=1)[:FFN_CONV]),
    ]
    grad_rows = _round_up(sum(-(-int(np.prod(a.shape[1:])) // PACK_W) for a in dest_blocks), PACK_ROW_TILE)
    gpack = _pack(dest_blocks, 1, grad_rows, bf16)
    (gparts,) = _exchange([gpack], True, "scatter_grads")
    wpack = _pack([a[None] for a in sharded_names], 1, grad_rows, f32)[0]
    mpack = _pack([a[None] for a in sharded_m], 1, grad_rows, f32)[0]
    vpack = _pack([a[None] for a in sharded_v], 1, grad_rows, f32)[0]
    outs_big = _adamw(gparts, wpack, mpack, vpack, "adamw_sharded")
    shard_shapes = [a.shape for a in sharded_names]
    g_sh, d_sh, m_sh, v_sh = [_unpack(b[None], shard_shapes, 1) for b in outs_big]
    g_sh, d_sh, m_sh, v_sh = [[t[0] for t in lst] for lst in (g_sh, d_sh, m_sh, v_sh)]

    dg_alog = da_heads[:, :SSM_HEADS] * a_neg
    repl_g = [dg_mix_pre, dg_mix_post, dg_ffn_pre, dg_ffn_post, dg_q, dg_kv, dg_attn_out, db_sconv,
              d_dt_bias[:, :SSM_HEADS], dg_alog, dd_heads[:, :SSM_HEADS], dg_ssm,
              jnp.concatenate([dbc_g, dbc_v], axis=1)]
    loss_vec = loss_part[:, :1]
    small_total = _round_up(sum(-(-int(np.prod(a.shape)) // PACK_W) for a in repl_g) + 1, 16)
    spack = _pack(repl_g + [loss_vec], 0, small_total, f32)
    (sparts,) = _exchange([spack], False, "gather_small_grads")
    zero1 = jnp.zeros((1, 1), f32)
    rw = _pack(repl_w + [zero1], 0, small_total, f32)
    rm = _pack(repl_m + [zero1], 0, small_total, f32)
    rv = _pack(repl_v + [zero1], 0, small_total, f32)
    outs_small = _adamw(sparts, rw, rm, rv, "adamw_replicated")
    repl_shapes = [a.shape for a in repl_w] + [(1, 1)]
    g_rp, d_rp, m_rp, v_rp = [_unpack(b, repl_shapes, 0) for b in outs_small]
    loss = g_rp[-1][0, 0]

    order = ["meta_tokens", "norm_mix_pre", "norm_mix_post", "norm_ffn_pre", "norm_ffn_post", "w_in", "q_a_norm",
             "w_uq", "kv_a_norm", "w_ukv", "attn_out_norm", "ssm_conv_w", "ssm_conv_b", "ssm_dt_bias", "ssm_A_log",
             "ssm_D", "ssm_norm", "w_out", "w_up", "ffn_conv_w", "ffn_conv_b", "w_down"]
    sh_names = ["w_in", "w_uq", "w_ukv", "w_out", "w_up", "w_down", "meta_tokens", "ssm_conv_w", "ffn_conv_w"]
    rp_names = ["norm_mix_pre", "norm_mix_post", "norm_ffn_pre", "norm_ffn_post", "q_a_norm", "kv_a_norm",
                "attn_out_norm", "ssm_conv_b", "ssm_dt_bias", "ssm_A_log", "ssm_D", "ssm_norm", "ffn_conv_b"]

    def lookup(sh_list, rp_list):
        d = {n: t for n, t in zip(sh_names, sh_list)}
        d.update({n: t for n, t in zip(rp_names, rp_list)})
        return [d[n] for n in order]

    return (loss, grad_x, *lookup(g_sh, g_rp), *lookup(d_sh, d_rp), *lookup(m_sh, m_rp), *lookup(v_sh, v_rp))
```

```python
import functools
import math

import jax
import jax.numpy as jnp
import numpy as np
from jax import lax
from jax.experimental import pallas as pl
from jax.experimental.pallas import tpu as pltpu

f32 = jnp.float32
bf16 = jnp.bfloat16

D_MODEL = 1024
SEQ = 8192
N_META = 16
MLA_HEADS = 8
QK_NOPE = 128
QK_ROPE = 64
V_DIM = 128
Q_RANK = 384
KV_RANK = 256
ROPE_THETA = 10000.0
SOFTMAX_SCALE = (QK_NOPE + QK_ROPE) ** -0.5
D_ATTN = MLA_HEADS * V_DIM
SSM_HEADS = 16
SSM_P = 64
SSM_GROUPS = 2
SSM_HPG = SSM_HEADS // SSM_GROUPS
SSM_N = 128
SSM_CONV = 4
CHUNK = 128
D_SSM = SSM_HEADS * SSM_P
D_BC = SSM_GROUPS * SSM_N
D_XBC = D_SSM + 2 * D_BC
D_FF = 2816
FFN_CONV = 3
EPS = 1e-6
D_IN = Q_RANK + KV_RANK + QK_ROPE + D_SSM + D_XBC + SSM_HEADS
QK_PAD = 256
N_DEV = 8

ADAM_LR = 0.001
ADAM_B1 = 0.9
ADAM_B2 = 0.999
ADAM_EPS = 1e-08
ADAM_WD = 0.01
ADAM_STEP = 10

LANES = 128
SUBLANES = 8
ROW_TILE = 256
VMEM_LIMIT = 56 * 1024 * 1024
PACK_W = 1024
PACK_ROW_TILE = 128
NEG = -1e30
LOG2E = math.log2(math.e)
LN2 = math.log(2.0)
Q_PRESCALE = SOFTMAX_SCALE * LOG2E

_MESH = pl.DeviceIdType.MESH


def _pick(n, prefs):
    for p in prefs:
        if n % p == 0:
            return p
    return n


def _rt(m):
    return _pick(m, (384, ROW_TILE))


def _cparams(sem):
    return pltpu.CompilerParams(dimension_semantics=sem, vmem_limit_bytes=VMEM_LIMIT)


def _row(spec_cols, tm):
    return pl.BlockSpec((tm, spec_cols), lambda i: (i, 0))


def _full(shape):
    nd = len(shape)
    return pl.BlockSpec(shape, lambda *a: (0,) * nd)


def _sigmoid(x):
    return 1.0 / (1.0 + jnp.exp(-x))


def _silu(x):
    return x * _sigmoid(x)


def _dsilu(x):
    s = _sigmoid(x)
    return s * (1.0 + x * (1.0 - s))


def _dot(a, b):
    return jnp.dot(a, b, preferred_element_type=f32)


def _dot_nt(a, b):
    return lax.dot_general(a, b, (((1,), (1,)), ((), ())), preferred_element_type=f32)


def _dot_tn(a, b):
    return lax.dot_general(a, b, (((0,), (0,)), ((), ())), preferred_element_type=f32)


def _dot_hi(a, b):
    return jnp.dot(a, b, precision=lax.Precision.HIGHEST, preferred_element_type=f32)


def _mm(pairs, out_dtype, trans_b, name):
    n = len(pairs)
    M = pairs[0][0].shape[0]
    N = pairs[0][1].shape[0] if trans_b else pairs[0][1].shape[1]
    tm = _pick(M, (768, 512, 256))
    tn = _pick(N, (512, 1408, 384, 256, 128))

    def body(*refs):
        o_ref = refs[2 * n]
        acc = None
        for p in range(n):
            a = refs[2 * p][...].astype(bf16)
            b = refs[2 * p + 1][...].astype(bf16)
            r = _dot_nt(a, b) if trans_b else _dot(a, b)
            acc = r if acc is None else acc + r
        o_ref[...] = acc.astype(out_dtype)

    in_specs, args = [], []
    for a, b in pairs:
        k = a.shape[1]
        in_specs.append(pl.BlockSpec((tm, k), lambda i, j: (i, 0)))
        if trans_b:
            in_specs.append(pl.BlockSpec((tn, k), lambda i, j: (j, 0)))
        else:
            in_specs.append(pl.BlockSpec((k, tn), lambda i, j: (0, j)))
        args += [a, b]
    return pl.pallas_call(
        body, name=name, grid=(M // tm, N // tn), in_specs=in_specs,
        out_specs=pl.BlockSpec((tm, tn), lambda i, j: (i, j)),
        out_shape=jax.ShapeDtypeStruct((M, N), out_dtype),
        compiler_params=_cparams(("parallel", "parallel")),
    )(*args)


def _mm_tn(a, g, name):
    M, K = a.shape
    N = g.shape[1]
    tm = _pick(M, (768, 512, 256))
    tk = _pick(K, (1024, 1408, 512, 384, 256))
    tn = _pick(N, (1024, 1408, 512, 384, 256, 128))

    def body(a_ref, g_ref, o_ref):
        @pl.when(pl.program_id(2) == 0)
        def _():
            o_ref[...] = jnp.zeros_like(o_ref)

        o_ref[...] += _dot_tn(a_ref[...].astype(bf16), g_ref[...].astype(bf16))

    return pl.pallas_call(
        body, name=name, grid=(K // tk, N // tn, M // tm),
        in_specs=[pl.BlockSpec((tm, tk), lambda k, j, m: (m, k)),
                  pl.BlockSpec((tm, tn), lambda k, j, m: (m, j))],
        out_specs=pl.BlockSpec((tk, tn), lambda k, j, m: (k, j)),
        out_shape=jax.ShapeDtypeStruct((K, N), f32),
        compiler_params=_cparams(("parallel", "parallel", "arbitrary")),
    )(a, g)


def _rstd(x):
    return lax.rsqrt(jnp.mean(x * x, axis=-1, keepdims=True) + EPS)


def _rms_bwd_math(x, g, dy):
    r = _rstd(x)
    xh = x * r
    dn = dy * g
    dx = r * (dn - xh * jnp.mean(dn * xh, axis=-1, keepdims=True))
    return dx, dy * xh


def _rms_fwd(x, g, out_dtype, name):
    M, K = x.shape
    tm = _rt(M)

    def body(x_ref, g_ref, o_ref):
        xv = x_ref[...]
        o_ref[...] = (xv * _rstd(xv) * g_ref[...]).astype(out_dtype)

    return pl.pallas_call(
        body, name=name, grid=(M // tm,), in_specs=[_row(K, tm), _full((1, K))],
        out_specs=_row(K, tm), out_shape=jax.ShapeDtypeStruct((M, K), out_dtype),
        compiler_params=_cparams(("parallel",)),
    )(x, g)


def _rms_bwd(x, g, dy, out_dtype, name, residual=None):
    M, K = x.shape
    tm = _rt(M)
    has_res = residual is not None

    def body(*refs):
        if has_res:
            x_ref, g_ref, dy_ref, r_ref, dx_ref, dg_ref = refs
        else:
            x_ref, g_ref, dy_ref, dx_ref, dg_ref = refs

        @pl.when(pl.program_id(0) == 0)
        def _():
            dg_ref[...] = jnp.zeros_like(dg_ref)

        dx, dgp = _rms_bwd_math(x_ref[...], g_ref[...], dy_ref[...].astype(f32))
        if has_res:
            dx = dx + r_ref[...]
        dx_ref[...] = dx.astype(out_dtype)
        dg_ref[...] += jnp.sum(dgp, axis=0, keepdims=True)

    ins = [x, g, dy] + ([residual] if has_res else [])
    in_specs = [_row(K, tm), _full((1, K)), _row(K, tm)] + ([_row(K, tm)] if has_res else [])
    return pl.pallas_call(
        body, name=name, grid=(M // tm,), in_specs=in_specs,
        out_specs=[_row(K, tm), _full((1, K))],
        out_shape=[jax.ShapeDtypeStruct((M, K), out_dtype), jax.ShapeDtypeStruct((1, K), f32)],
        compiler_params=_cparams(("arbitrary",)),
    )(*ins)


def _resid_norm(h0, mix, g2, g3):
    M, K = h0.shape
    tm = _rt(M)

    def body(h_ref, m_ref, g2_ref, g3_ref, h1_ref, hn_ref):
        mv = m_ref[...]
        h1 = h_ref[...] + mv * _rstd(mv) * g2_ref[...]
        h1_ref[...] = h1
        hn_ref[...] = (h1 * _rstd(h1) * g3_ref[...]).astype(bf16)

    return pl.pallas_call(
        body, name="resid_norm", grid=(M // tm,),
        in_specs=[_row(K, tm), _row(K, tm), _full((1, K)), _full((1, K))],
        out_specs=[_row(K, tm), _row(K, tm)],
        out_shape=[jax.ShapeDtypeStruct((M, K), f32), jax.ShapeDtypeStruct((M, K), bf16)],
        compiler_params=_cparams(("parallel",)),
    )(h0, mix, g2, g3)


def _final(h1, down, g4, tgt, n_real):
    M, K = h1.shape
    tm = _rt(M)
    nt = M // tm

    def body(h_ref, d_ref, g_ref, t_ref, dh_ref, dd_ref, dg_ref, ls_ref, acc_ref):
        i = pl.program_id(0)

        @pl.when(i == 0)
        def _():
            dg_ref[...] = jnp.zeros_like(dg_ref)
            acc_ref[...] = jnp.zeros_like(acc_ref)

        dv = d_ref[...]
        g = g_ref[...]
        r = _rstd(dv)
        n = dv * r
        h2 = h_ref[...] + n * g
        rows = i * tm + lax.broadcasted_iota(jnp.int32, (tm, 1), 0)
        mask = ((rows >= N_META) & (rows < n_real)).astype(f32)
        diff = (h2 - t_ref[...]) * mask
        acc_ref[...] += jnp.sum(diff * diff, axis=0, keepdims=True)
        dh = diff * (1.0 / K)
        dh_ref[...] = dh
        dn = dh * g
        dd_ref[...] = (r * (dn - n * jnp.mean(dn * n, axis=-1, keepdims=True))).astype(bf16)
        dg_ref[...] += jnp.sum(dh * n, axis=0, keepdims=True)

        @pl.when(i == nt - 1)
        def _():
            ls_ref[...] = jnp.zeros((1, LANES), f32) + jnp.sum(acc_ref[...]) * (0.5 / K)

    return pl.pallas_call(
        body, name="final_loss", grid=(nt,),
        in_specs=[_row(K, tm), _row(K, tm), _full((1, K)), _row(K, tm)],
        out_specs=[_row(K, tm), _row(K, tm), _full((1, K)), _full((1, LANES))],
        out_shape=[jax.ShapeDtypeStruct((M, K), f32), jax.ShapeDtypeStruct((M, K), bf16),
                   jax.ShapeDtypeStruct((1, K), f32), jax.ShapeDtypeStruct((1, LANES), f32)],
        scratch_shapes=[pltpu.VMEM((1, K), f32)],
        compiler_params=_cparams(("arbitrary",)),
    )(h1, down, g4, tgt)


def _mid_bwd(h1, g3, d_hn2, dh2, mix, g2):
    M, K = h1.shape
    tm = _rt(M)

    def body(h_ref, g3_ref, dn_ref, dh2_ref, m_ref, g2_ref, dh1_ref, dm_ref, dg3_ref, dg2_ref):
        @pl.when(pl.program_id(0) == 0)
        def _():
            dg3_ref[...] = jnp.zeros_like(dg3_ref)
            dg2_ref[...] = jnp.zeros_like(dg2_ref)

        dx, dgp = _rms_bwd_math(h_ref[...], g3_ref[...], dn_ref[...])
        dh1 = dh2_ref[...] + dx
        dh1_ref[...] = dh1
        dg3_ref[...] += jnp.sum(dgp, axis=0, keepdims=True)
        dm, dgp2 = _rms_bwd_math(m_ref[...], g2_ref[...], dh1)
        dm_ref[...] = dm.astype(bf16)
        dg2_ref[...] += jnp.sum(dgp2, axis=0, keepdims=True)

    return pl.pallas_call(
        body, name="mid_bwd", grid=(M // tm,),
        in_specs=[_row(K, tm), _full((1, K)), _row(K, tm), _row(K, tm), _row(K, tm), _full((1, K))],
        out_specs=[_row(K, tm), _row(K, tm), _full((1, K)), _full((1, K))],
        out_shape=[jax.ShapeDtypeStruct((M, K), f32), jax.ShapeDtypeStruct((M, K), bf16),
                   jax.ShapeDtypeStruct((1, K), f32), jax.ShapeDtypeStruct((1, K), f32)],
        compiler_params=_cparams(("arbitrary",)),
    )(h1, g3, d_hn2, dh2, mix, g2)


def _conv_taps(ext_ref, w_ref, kw, tm, first):
    u = None
    for k in range(kw):
        t = ext_ref[pl.ds(first + k, tm), :] * w_ref[k:k + 1, :]
        u = t if u is None else u + t
    return u


def _fill_prev(ext_ref, x_ref, halo_ref, i, tm):
    ext_ref[0:SUBLANES, :] = jnp.where(i == 0, 0.0, halo_ref[...])
    ext_ref[SUBLANES:SUBLANES + tm, :] = x_ref[...]


def _prev_spec(tm, tc, col_of, row_axis, reversed_tiles=0):
    def imap(*ids):
        i = ids[row_axis]
        if reversed_tiles:
            i = reversed_tiles - 1 - i
        return (jnp.maximum(i * (tm // SUBLANES) - 1, 0), col_of(*ids))
    return pl.BlockSpec((SUBLANES, tc), imap)


def _conv_dx_carry(edu_ref, du, w_ref, kw, tm, first_step):
    @pl.when(first_step)
    def _():
        edu_ref[tm:tm + SUBLANES, :] = jnp.zeros((SUBLANES, edu_ref.shape[1]), f32)

    edu_ref[0:tm, :] = du
    acc = None
    for k in range(kw):
        t = edu_ref[pl.ds(kw - 1 - k, tm), :] * w_ref[k:k + 1, :]
        acc = t if acc is None else acc + t
    edu_ref[tm:tm + SUBLANES, :] = edu_ref[0:SUBLANES, :]
    return acc


def _ssm_conv_fwd(xbc, w, b):
    M, C = xbc.shape
    tm, tc, kw = ROW_TILE, C, SSM_CONV

    def body(x_ref, h_ref, w_ref, b_ref, o_ref, ext_ref):
        _fill_prev(ext_ref, x_ref, h_ref, pl.program_id(0), tm)
        u = _conv_taps(ext_ref, w_ref, kw, tm, SUBLANES - (kw - 1)) + b_ref[...]
        o_ref[...] = _silu(u)

    return pl.pallas_call(
        body, name="ssm_conv_fwd", grid=(M // tm, C // tc),
        in_specs=[pl.BlockSpec((tm, tc), lambda i, j: (i, j)),
                  _prev_spec(tm, tc, lambda i, j: j, 0),
                  pl.BlockSpec((SUBLANES, tc), lambda i, j: (0, j)),
                  pl.BlockSpec((1, tc), lambda i, j: (0, j))],
        out_specs=pl.BlockSpec((tm, tc), lambda i, j: (i, j)),
        out_shape=jax.ShapeDtypeStruct((M, C), f32),
        scratch_shapes=[pltpu.VMEM((tm + SUBLANES, tc), f32)],
        compiler_params=_cparams(("parallel", "parallel")),
    )(xbc, xbc, w, b)


def _ssm_conv_bwd(xbc, w, b, dout):
    M, C = xbc.shape
    tm, tc, kw = ROW_TILE, C // 3, SSM_CONV
    nt = M // tm

    def body(x_ref, h_ref, w_ref, b_ref, d_ref, dx_ref, dw_ref, db_ref, ext_ref, edu_ref):
        i = pl.program_id(1)

        @pl.when(i == 0)
        def _():
            dw_ref[...] = jnp.zeros_like(dw_ref)
            db_ref[...] = jnp.zeros_like(db_ref)

        _fill_prev(ext_ref, x_ref, h_ref, nt - 1 - i, tm)
        first = SUBLANES - (kw - 1)
        u = _conv_taps(ext_ref, w_ref, kw, tm, first) + b_ref[...]
        du = d_ref[...] * _dsilu(u)
        db_ref[...] += jnp.sum(du, axis=0, keepdims=True)
        for k in range(kw):
            dw_ref[k:k + 1, :] += jnp.sum(du * ext_ref[pl.ds(first + k, tm), :], axis=0, keepdims=True)
        dx_ref[...] = _conv_dx_carry(edu_ref, du, w_ref, kw, tm, i == 0).astype(bf16)

    tile = pl.BlockSpec((tm, tc), lambda j, i: (nt - 1 - i, j))
    return pl.pallas_call(
        body, name="ssm_conv_bwd", grid=(C // tc, nt),
        in_specs=[tile, _prev_spec(tm, tc, lambda j, i: j, 1, nt),
                  pl.BlockSpec((SUBLANES, tc), lambda j, i: (0, j)),
                  pl.BlockSpec((1, tc), lambda j, i: (0, j)), tile],
        out_specs=[tile, pl.BlockSpec((SUBLANES, tc), lambda j, i: (0, j)),
                   pl.BlockSpec((1, tc), lambda j, i: (0, j))],
        out_shape=[jax.ShapeDtypeStruct((M, C), bf16), jax.ShapeDtypeStruct((SUBLANES, C), f32),
                   jax.ShapeDtypeStruct((1, C), f32)],
        scratch_shapes=[pltpu.VMEM((tm + SUBLANES, tc), f32), pltpu.VMEM((tm + SUBLANES, tc), f32)],
        compiler_params=_cparams(("parallel", "arbitrary")),
    )(xbc, xbc, w, b, dout)


def _ffn_gate_fwd(up, w, b):
    M = up.shape[0]
    tm, tc, kw = ROW_TILE, D_FF // 2, FFN_CONV
    nc = D_FF // tc

    def body(xg_ref, hg_ref, xv_ref, hv_ref, wg_ref, wv_ref, bg_ref, bv_ref, o_ref, eg_ref, ev_ref):
        i = pl.program_id(0)
        first = SUBLANES - (kw - 1)
        _fill_prev(eg_ref, xg_ref, hg_ref, i, tm)
        _fill_prev(ev_ref, xv_ref, hv_ref, i, tm)
        ug = _conv_taps(eg_ref, wg_ref, kw, tm, first) + bg_ref[...]
        uv = _conv_taps(ev_ref, wv_ref, kw, tm, first) + bv_ref[...]
        o_ref[...] = (_silu(ug) * uv).astype(bf16)

    return pl.pallas_call(
        body, name="ffn_gate_fwd", grid=(M // tm, nc),
        in_specs=[pl.BlockSpec((tm, tc), lambda i, j: (i, j)),
                  _prev_spec(tm, tc, lambda i, j: j, 0),
                  pl.BlockSpec((tm, tc), lambda i, j: (i, j + nc)),
                  _prev_spec(tm, tc, lambda i, j: j + nc, 0),
                  pl.BlockSpec((SUBLANES, tc), lambda i, j: (0, j)),
                  pl.BlockSpec((SUBLANES, tc), lambda i, j: (0, j + nc)),
                  pl.BlockSpec((1, tc), lambda i, j: (0, j)),
                  pl.BlockSpec((1, tc), lambda i, j: (0, j + nc))],
        out_specs=pl.BlockSpec((tm, tc), lambda i, j: (i, j)),
        out_shape=jax.ShapeDtypeStruct((M, D_FF), bf16),
        scratch_shapes=[pltpu.VMEM((tm + SUBLANES, tc), f32), pltpu.VMEM((tm + SUBLANES, tc), f32)],
        compiler_params=_cparams(("parallel", "parallel")),
    )(up, up, up, up, w, w, b, b)


def _ffn_gate_bwd(up, w, b, d_act):
    M = up.shape[0]
    tm, tc, kw = ROW_TILE, D_FF // 2, FFN_CONV
    nc = D_FF // tc
    nt = M // tm

    def body(xg_ref, hg_ref, xv_ref, hv_ref, wg_ref, wv_ref, bg_ref, bv_ref, d_ref,
             dxg_ref, dxv_ref, dwg_ref, dwv_ref, dbg_ref, dbv_ref, eg_ref, ev_ref, edg_ref, edv_ref):
        i = pl.program_id(1)

        @pl.when(i == 0)
        def _():
            for r in (dwg_ref, dwv_ref, dbg_ref, dbv_ref):
                r[...] = jnp.zeros_like(r)

        first = SUBLANES - (kw - 1)
        _fill_prev(eg_ref, xg_ref, hg_ref, nt - 1 - i, tm)
        _fill_prev(ev_ref, xv_ref, hv_ref, nt - 1 - i, tm)
        ug = _conv_taps(eg_ref, wg_ref, kw, tm, first) + bg_ref[...]
        uv = _conv_taps(ev_ref, wv_ref, kw, tm, first) + bv_ref[...]
        da = d_ref[...]
        dug = da * uv * _dsilu(ug)
        duv = da * _silu(ug)
        dbg_ref[...] += jnp.sum(dug, axis=0, keepdims=True)
        dbv_ref[...] += jnp.sum(duv, axis=0, keepdims=True)
        for k in range(kw):
            dwg_ref[k:k + 1, :] += jnp.sum(dug * eg_ref[pl.ds(first + k, tm), :], axis=0, keepdims=True)
            dwv_ref[k:k + 1, :] += jnp.sum(duv * ev_ref[pl.ds(first + k, tm), :], axis=0, keepdims=True)
        dxg_ref[...] = _conv_dx_carry(edg_ref, dug, wg_ref, kw, tm, i == 0).astype(bf16)
        dxv_ref[...] = _conv_dx_carry(edv_ref, duv, wv_ref, kw, tm, i == 0).astype(bf16)

    tile_g = pl.BlockSpec((tm, tc), lambda j, i: (nt - 1 - i, j))
    tile_v = pl.BlockSpec((tm, tc), lambda j, i: (nt - 1 - i, j + nc))
    ext = pltpu.VMEM((tm + SUBLANES, tc), f32)
    return pl.pallas_call(
        body, name="ffn_gate_bwd", grid=(nc, nt),
        in_specs=[tile_g, _prev_spec(tm, tc, lambda j, i: j, 1, nt),
                  tile_v, _prev_spec(tm, tc, lambda j, i: j + nc, 1, nt),
                  pl.BlockSpec((SUBLANES, tc), lambda j, i: (0, j)),
                  pl.BlockSpec((SUBLANES, tc), lambda j, i: (0, j + nc)),
                  pl.BlockSpec((1, tc), lambda j, i: (0, j)),
                  pl.BlockSpec((1, tc), lambda j, i: (0, j + nc)),
                  tile_g],
        out_specs=[tile_g, tile_g,
                   pl.BlockSpec((SUBLANES, tc), lambda j, i: (0, j)),
                   pl.BlockSpec((SUBLANES, tc), lambda j, i: (0, j)),
                   pl.BlockSpec((1, tc), lambda j, i: (0, j)),
                   pl.BlockSpec((1, tc), lambda j, i: (0, j))],
        out_shape=[jax.ShapeDtypeStruct((M, D_FF), bf16), jax.ShapeDtypeStruct((M, D_FF), bf16),
                   jax.ShapeDtypeStruct((SUBLANES, D_FF), f32), jax.ShapeDtypeStruct((SUBLANES, D_FF), f32),
                   jax.ShapeDtypeStruct((1, D_FF), f32), jax.ShapeDtypeStruct((1, D_FF), f32)],
        scratch_shapes=[ext, ext, ext, ext],
        compiler_params=_cparams(("parallel", "arbitrary")),
    )(up, up, up, up, w, w, b, b, d_act)


def _rope_apply(blk, cos, sin):
    lane = lax.broadcasted_iota(jnp.int32, blk.shape, 1)
    half = QK_ROPE // 2
    partner = jnp.where(lane < half, pltpu.roll(blk, LANES - half, 1), pltpu.roll(blk, half, 1))
    return blk * cos + partner * sin


def _rope_unapply(d, cos, sin):
    t = d * sin
    lane = lax.broadcasted_iota(jnp.int32, d.shape, 1)
    half = QK_ROPE // 2
    partner = jnp.where(lane < half, pltpu.roll(t, LANES - half, 1), pltpu.roll(t, half, 1))
    return d * cos + partner


def _up_q_rope(qn, wuq, cos, sin):
    M, K = qn.shape
    tm = _pick(M, (768, 512, 256))

    def body(a_ref, b_ref, c_ref, s_ref, o_ref):
        r = _dot(a_ref[...], b_ref[...]) * Q_PRESCALE
        o_ref[0, :, 0:QK_NOPE] = r[:, 0:QK_NOPE].astype(bf16)
        o_ref[0, :, QK_NOPE:QK_PAD] = _rope_apply(r[:, QK_NOPE:QK_PAD], c_ref[...], s_ref[...]).astype(bf16)

    return pl.pallas_call(
        body, name="up_q_rope", grid=(M // tm, MLA_HEADS),
        in_specs=[pl.BlockSpec((tm, K), lambda i, h: (i, 0)),
                  pl.BlockSpec((K, QK_PAD), lambda i, h: (0, h)),
                  pl.BlockSpec((tm, LANES), lambda i, h: (i, 0)),
                  pl.BlockSpec((tm, LANES), lambda i, h: (i, 0))],
        out_specs=pl.BlockSpec((1, tm, QK_PAD), lambda i, h: (h, i, 0)),
        out_shape=jax.ShapeDtypeStruct((MLA_HEADS, M, QK_PAD), bf16),
        compiler_params=_cparams(("parallel", "parallel")),
    )(qn, wuq, cos, sin)


def _up_kv_rope(kvn, wukv, kpe_raw, cos, sin):
    M, K = kvn.shape
    tm = _pick(M, (768, 512, 256))

    def body(a_ref, b_ref, pe_ref, c_ref, s_ref, k_ref, v_ref):
        r = _dot(a_ref[...], b_ref[...])
        k_ref[0, :, 0:QK_NOPE] = r[:, 0:QK_NOPE].astype(bf16)
        k_ref[0, :, QK_NOPE:QK_PAD] = _rope_apply(pe_ref[...], c_ref[...], s_ref[...]).astype(bf16)
        v_ref[0] = r[:, QK_NOPE:QK_NOPE + V_DIM].astype(bf16)

    return pl.pallas_call(
        body, name="up_kv_rope", grid=(M // tm, MLA_HEADS),
        in_specs=[pl.BlockSpec((tm, K), lambda i, h: (i, 0)),
                  pl.BlockSpec((K, QK_NOPE + V_DIM), lambda i, h: (0, h)),
                  pl.BlockSpec((tm, LANES), lambda i, h: (i, 0)),
                  pl.BlockSpec((tm, LANES), lambda i, h: (i, 0)),
                  pl.BlockSpec((tm, LANES), lambda i, h: (i, 0))],
        out_specs=[pl.BlockSpec((1, tm, QK_PAD), lambda i, h: (h, i, 0)),
                   pl.BlockSpec((1, tm, V_DIM), lambda i, h: (h, i, 0))],
        out_shape=[jax.ShapeDtypeStruct((MLA_HEADS, M, QK_PAD), bf16),
                   jax.ShapeDtypeStruct((MLA_HEADS, M, V_DIM), bf16)],
        compiler_params=_cparams(("parallel", "parallel")),
    )(kvn, wukv, kpe_raw, cos, sin)


def _rope_q_bwd(dq, cos, sin):
    M = dq.shape[1]
    tm = _rt(M)

    def body(d_ref, c_ref, s_ref, o_ref):
        c, s = c_ref[...], s_ref[...]
        for h in range(MLA_HEADS):
            o_ref[:, h * QK_PAD:h * QK_PAD + QK_NOPE] = (d_ref[h, :, 0:QK_NOPE] * SOFTMAX_SCALE).astype(bf16)
            o_ref[:, h * QK_PAD + QK_NOPE:(h + 1) * QK_PAD] = (_rope_unapply(
                d_ref[h, :, QK_NOPE:QK_PAD], c, s) * SOFTMAX_SCALE).astype(bf16)

    return pl.pallas_call(
        body, name="rope_q_bwd", grid=(M // tm,),
        in_specs=[pl.BlockSpec((MLA_HEADS, tm, QK_PAD), lambda i: (0, i, 0)),
                  _row(LANES, tm), _row(LANES, tm)],
        out_specs=_row(MLA_HEADS * QK_PAD, tm),
        out_shape=jax.ShapeDtypeStruct((M, MLA_HEADS * QK_PAD), bf16),
        compiler_params=_cparams(("parallel",)),
    )(dq, cos, sin)


def _rope_k_bwd(dk, dv, cos, sin):
    M = dk.shape[1]
    tm = _rt(M)
    w = QK_NOPE + V_DIM

    def body(dk_ref, dv_ref, c_ref, s_ref, o_ref, pe_ref):
        pe = None
        for h in range(MLA_HEADS):
            o_ref[:, h * w:h * w + QK_NOPE] = dk_ref[h, :, 0:QK_NOPE].astype(bf16)
            o_ref[:, h * w + QK_NOPE:(h + 1) * w] = dv_ref[h].astype(bf16)
            t = dk_ref[h, :, QK_NOPE:QK_PAD]
            pe = t if pe is None else pe + t
        pe_ref[...] = _rope_unapply(pe, c_ref[...], s_ref[...])

    return pl.pallas_call(
        body, name="rope_k_bwd", grid=(M // tm,),
        in_specs=[pl.BlockSpec((MLA_HEADS, tm, QK_PAD), lambda i: (0, i, 0)),
                  pl.BlockSpec((MLA_HEADS, tm, V_DIM), lambda i: (0, i, 0)),
                  _row(LANES, tm), _row(LANES, tm)],
        out_specs=[_row(MLA_HEADS * w, tm), _row(LANES, tm)],
        out_shape=[jax.ShapeDtypeStruct((M, MLA_HEADS * w), bf16), jax.ShapeDtypeStruct((M, LANES), f32)],
        compiler_params=_cparams(("parallel",)),
    )(dk, dv, cos, sin)


def _attn_tile(M):
    return 768 if (M % 768 == 0 and M >= 4 * 768) else ROW_TILE


def _col_to_row(col):
    return col.T[0:1, :]


def _flash_fwd(q, k, v):
    H, M, _ = q.shape
    T = _attn_tile(M)
    nq = M // T

    def body(q_ref, k_ref, v_ref, o_ref, lse_ref, sa_ref, sb_ref, m_sc, l_sc, acc_sc):
        i = pl.program_id(1)
        qv = q_ref[0]
        m_sc[...] = jnp.full_like(m_sc, NEG)
        l_sc[...] = jnp.zeros_like(l_sc)
        acc_sc[...] = jnp.zeros_like(acc_sc)

        def scores(j, s_ref):
            off = pl.multiple_of(j * T, T)
            s_ref[...] = _dot_nt(qv, k_ref[0, pl.ds(off, T), :])

        def softmax_pv(j, s_ref, masked):
            off = pl.multiple_of(j * T, T)
            s = s_ref[...]
            if masked:
                r = lax.broadcasted_iota(jnp.int32, (T, T), 0)
                c = lax.broadcasted_iota(jnp.int32, (T, T), 1)
                s = jnp.where(r >= c, s, NEG)
            m_prev = m_sc[...]
            m_new = jnp.maximum(m_prev, jnp.max(s, axis=1, keepdims=True))
            alpha = jnp.exp2(m_prev - m_new)
            p = jnp.exp2(s - m_new[:, 0:1])
            l_sc[...] = alpha * l_sc[...] + jnp.sum(p, axis=1, keepdims=True)
            acc_sc[...] = alpha * acc_sc[...] + _dot(p.astype(bf16), v_ref[0, pl.ds(off, T), :])
            m_sc[...] = m_new

        scores(0, sa_ref)

        def pair(jj, c):
            j0 = 2 * jj
            scores(j0 + 1, sb_ref)
            softmax_pv(j0, sa_ref, False)
            scores(j0 + 2, sa_ref)
            softmax_pv(j0 + 1, sb_ref, False)
            return c

        lax.fori_loop(0, i // 2, pair, 0)

        @pl.when(i % 2 == 0)
        def _():
            softmax_pv(i, sa_ref, True)

        @pl.when(i % 2 == 1)
        def _():
            scores(i, sb_ref)
            softmax_pv(i - 1, sa_ref, False)
            softmax_pv(i, sb_ref, True)

        l = l_sc[...]
        o_ref[...] = acc_sc[...] / l
        lse_ref[0, 0] = _col_to_row(m_sc[...] + jnp.log2(l))

    return pl.pallas_call(
        body, name="flash_fwd", grid=(H, nq),
        in_specs=[pl.BlockSpec((1, T, QK_PAD), lambda h, i: (h, i, 0)),
                  pl.BlockSpec((1, M, QK_PAD), lambda h, i: (h, 0, 0)),
                  pl.BlockSpec((1, M, V_DIM), lambda h, i: (h, 0, 0))],
        out_specs=[pl.BlockSpec((T, V_DIM), lambda h, i: (i, h)),
                   pl.BlockSpec((1, 1, 1, T), lambda h, i: (h, i, 0, 0))],
        out_shape=[jax.ShapeDtypeStruct((M, H * V_DIM), f32),
                   jax.ShapeDtypeStruct((H, nq, 1, T), f32)],
        scratch_shapes=[pltpu.VMEM((T, T), f32), pltpu.VMEM((T, T), f32),
                        pltpu.VMEM((T, LANES), f32), pltpu.VMEM((T, LANES), f32), pltpu.VMEM((T, V_DIM), f32)],
        compiler_params=_cparams(("parallel", "parallel")),
    )(q, k, v)


def _attn_delta(o, do):
    M = o.shape[0]
    H = MLA_HEADS
    T = _attn_tile(M)

    def body(o_ref, d_ref, dh_ref, dl_ref):
        dv = d_ref[...]
        dh_ref[0] = dv.astype(bf16)
        col = jnp.sum(o_ref[...] * dv, axis=1, keepdims=True) + jnp.zeros((T, LANES), f32)
        dl_ref[0, 0] = _col_to_row(col)

    return pl.pallas_call(
        body, name="attn_delta", grid=(M // T, H),
        in_specs=[pl.BlockSpec((T, V_DIM), lambda i, h: (i, h)),
                  pl.BlockSpec((T, V_DIM), lambda i, h: (i, h))],
        out_specs=[pl.BlockSpec((1, T, V_DIM), lambda i, h: (h, i, 0)),
                   pl.BlockSpec((1, 1, 1, T), lambda i, h: (h, i, 0, 0))],
        out_shape=[jax.ShapeDtypeStruct((H, M, V_DIM), bf16),
                   jax.ShapeDtypeStruct((H, M // T, 1, T), f32)],
        compiler_params=_cparams(("parallel", "parallel")),
    )(o, do)


def _flash_bwd(q, k, v, do, lse, delta):
    H, M, _ = q.shape
    T = _attn_tile(M)
    nq = M // T

    def body(q_ref, do_ref, lse_ref, dl_ref, k_ref, v_ref, dq_ref, dk_ref, dv_ref, dk_sc, dv_sc):
        j = pl.program_id(1)

        @pl.when(j == 0)
        def _():
            dq_ref[...] = jnp.zeros_like(dq_ref)

        kt = k_ref[0]
        vt = v_ref[0]
        dk_sc[...] = jnp.zeros_like(dk_sc)
        dv_sc[...] = jnp.zeros_like(dv_sc)

        def step(i, masked):
            off = pl.multiple_of(i * T, T)
            qt = q_ref[0, pl.ds(off, T), :]
            dot_ = do_ref[0, pl.ds(off, T), :]
            st = _dot_nt(kt, qt)
            if masked:
                r = lax.broadcasted_iota(jnp.int32, (T, T), 0)
                c = lax.broadcasted_iota(jnp.int32, (T, T), 1)
                st = jnp.where(c >= r, st, NEG)
            pt = jnp.exp2(st - lse_ref[0, i])
            dv_sc[...] += _dot(pt.astype(bf16), dot_)
            dpt = _dot_nt(vt, dot_)
            dst = (pt * (dpt - dl_ref[0, i])).astype(bf16)
            dk_sc[...] += _dot(dst, qt)
            dq_ref[0, pl.ds(off, T), :] += _dot_tn(dst, kt)

        step(j, True)

        def loop_body(i, c):
            step(i, False)
            return c

        lax.fori_loop(j + 1, nq, loop_body, 0)
        dk_ref[0] = dk_sc[...] * LN2
        dv_ref[0] = dv_sc[...]

    return pl.pallas_call(
        body, name="flash_bwd", grid=(H, nq),
        in_specs=[pl.BlockSpec((1, M, QK_PAD), lambda h, j: (h, 0, 0)),
                  pl.BlockSpec((1, M, V_DIM), lambda h, j: (h, 0, 0)),
                  pl.BlockSpec((1, nq, 1, T), lambda h, j: (h, 0, 0, 0)),
                  pl.BlockSpec((1, nq, 1, T), lambda h, j: (h, 0, 0, 0)),
                  pl.BlockSpec((1, T, QK_PAD), lambda h, j: (h, j, 0)),
                  pl.BlockSpec((1, T, V_DIM), lambda h, j: (h, j, 0))],
        out_specs=[pl.BlockSpec((1, M, QK_PAD), lambda h, j: (h, 0, 0)),
                   pl.BlockSpec((1, T, QK_PAD), lambda h, j: (h, j, 0)),
                   pl.BlockSpec((1, T, V_DIM), lambda h, j: (h, j, 0))],
        out_shape=[jax.ShapeDtypeStruct((H, M, QK_PAD), f32),
                   jax.ShapeDtypeStruct((H, M, QK_PAD), f32),
                   jax.ShapeDtypeStruct((H, M, V_DIM), f32)],
        scratch_shapes=[pltpu.VMEM((T, QK_PAD), f32), pltpu.VMEM((T, V_DIM), f32)],
        compiler_params=_cparams(("parallel", "arbitrary")),
    )(q, do, lse, delta, k, v)


def _dt_fwd(dt_raw, bias):
    M = dt_raw.shape[0]
    tm = _rt(M)

    def body(x_ref, b_ref, o_ref):
        u = x_ref[...] + b_ref[...]
        sp = jnp.maximum(u, 0.0) + jnp.log(1.0 + jnp.exp(-jnp.abs(u)))
        lane = lax.broadcasted_iota(jnp.int32, u.shape, 1)
        o_ref[...] = jnp.where(lane < SSM_HEADS, sp, 0.0)

    return pl.pallas_call(
        body, name="dt_fwd", grid=(M // tm,), in_specs=[_row(LANES, tm), _full((1, LANES))],
        out_specs=_row(LANES, tm), out_shape=jax.ShapeDtypeStruct((M, LANES), f32),
        compiler_params=_cparams(("parallel",)),
    )(dt_raw, bias)


def _dt_bwd(dt_raw, bias, ddt):
    M = dt_raw.shape[0]
    tm = _rt(M)

    def body(x_ref, b_ref, d_ref, o_ref, db_ref):
        @pl.when(pl.program_id(0) == 0)
        def _():
            db_ref[...] = jnp.zeros_like(db_ref)

        u = x_ref[...] + b_ref[...]
        lane = lax.broadcasted_iota(jnp.int32, u.shape, 1)
        g = jnp.where(lane < SSM_HEADS, d_ref[...] * _sigmoid(u), 0.0)
        o_ref[...] = g
        db_ref[...] += jnp.sum(g, axis=0, keepdims=True)

    return pl.pallas_call(
        body, name="dt_bwd", grid=(M // tm,),
        in_specs=[_row(LANES, tm), _full((1, LANES)), _row(LANES, tm)],
        out_specs=[_row(LANES, tm), _full((1, LANES))],
        out_shape=[jax.ShapeDtypeStruct((M, LANES), f32), jax.ShapeDtypeStruct((1, LANES), f32)],
        compiler_params=_cparams(("arbitrary",)),
    )(dt_raw, bias, ddt)


def _ssd_common(dtp_ref, dtt_ref, arow_ref, acol_ref):
    Q = CHUNK
    r = lax.broadcasted_iota(jnp.int32, (Q, Q), 0)
    c = lax.broadcasted_iota(jnp.int32, (Q, Q), 1)
    causal = r >= c
    tril = causal.astype(f32)
    triu = (r <= c).astype(f32)
    dt = dtp_ref[...]
    cs = _dot_hi(tril, dt * arow_ref[...])
    cst = _dot_hi(dtt_ref[...] * acol_ref[...], triu)
    return causal, triu, dt, cs, cst


def _ssd_fwd(xbc_c, dtp, dtt, a_row, a_col):
    M = xbc_c.shape[0]
    Q = CHUNK
    nch = M // Q

    def body(x_ref, dtp_ref, dtt_ref, arow_ref, acol_ref, y_ref, hin_ref, ht_sc):
        @pl.when(pl.program_id(0) == 0)
        def _():
            ht_sc[...] = jnp.zeros_like(ht_sc)

        causal, _, dt, cs, cst = _ssd_common(dtp_ref, dtt_ref, arow_ref, acol_ref)
        ecs = jnp.exp(cs)
        cs_last = cs[Q - 1:Q, :]
        dte = jnp.exp(cs_last - cs)
        e_last = jnp.exp(cs_last)
        for g in range(SSM_GROUPS):
            bg = x_ref[:, D_SSM + g * SSM_N:D_SSM + (g + 1) * SSM_N]
            cg = x_ref[:, D_SSM + D_BC + g * SSM_N:D_SSM + D_BC + (g + 1) * SSM_N]
            bg_b = bg.astype(bf16)
            cg_b = cg.astype(bf16)
            cb = _dot_nt(cg_b, bg_b)
            bgt_b = bg.T.astype(bf16)
            for hh in range(SSM_HPG):
                h = g * SSM_HPG + hh
                seg = cs[:, h:h + 1] - cst[h:h + 1, :]
                lm = jnp.exp(jnp.where(causal, seg, -jnp.inf))
                mb = (cb * lm).astype(bf16)
                xh = x_ref[:, h * SSM_P:(h + 1) * SSM_P]
                xdt = xh * dt[:, h:h + 1]
                hth = ht_sc[h]
                hin_ref[0, h] = hth
                y_off = _dot(cg_b, hth.astype(bf16)) * ecs[:, h:h + 1]
                y_ref[:, h * SSM_P:(h + 1) * SSM_P] = _dot(mb, xdt.astype(bf16)) + y_off
                st = _dot(bgt_b, (xdt * dte[:, h:h + 1]).astype(bf16))
                ht_sc[h] = hth * e_last[:, h:h + 1] + st

    return pl.pallas_call(
        body, name="ssd_fwd", grid=(nch,),
        in_specs=[pl.BlockSpec((Q, D_XBC), lambda c: (c, 0)),
                  pl.BlockSpec((Q, LANES), lambda c: (c, 0)),
                  pl.BlockSpec((SSM_HEADS, Q), lambda c: (0, c)),
                  _full((1, LANES)), _full((SSM_HEADS, LANES))],
        out_specs=[pl.BlockSpec((Q, D_SSM), lambda c: (c, 0)),
                   pl.BlockSpec((1, SSM_HEADS, SSM_N, SSM_P), lambda c: (c, 0, 0, 0))],
        out_shape=[jax.ShapeDtypeStruct((M, D_SSM), f32),
                   jax.ShapeDtypeStruct((nch, SSM_HEADS, SSM_N, SSM_P), f32)],
        scratch_shapes=[pltpu.VMEM((SSM_HEADS, SSM_N, SSM_P), f32)],
        compiler_params=_cparams(("arbitrary",)),
    )(xbc_c, dtp, dtt, a_row, a_col)


def _ssd_bwd(xbc_c, dtp, dtt, a_row, a_col, hin, dy, d_exp):
    M = xbc_c.shape[0]
    Q = CHUNK
    nch = M // Q
    rev = lambda c: nch - 1 - c

    def body(x_ref, dtp_ref, dtt_ref, arow_ref, acol_ref, hin_ref, dy_ref, dexp_ref,
             dx_ref, ddt_ref, da_ref, dht_sc):
        @pl.when(pl.program_id(0) == 0)
        def _():
            dht_sc[...] = jnp.zeros_like(dht_sc)
            da_ref[...] = jnp.zeros_like(da_ref)

        causal, triu, dt, cs, cst = _ssd_common(dtp_ref, dtt_ref, arow_ref, acol_ref)
        ecs = jnp.exp(cs)
        cs_last = cs[Q - 1:Q, :]
        dte = jnp.exp(cs_last - cs)
        e_last = jnp.exp(cs_last)
        lane = lax.broadcasted_iota(jnp.int32, (Q, LANES), 1)
        row = lax.broadcasted_iota(jnp.int32, (Q, LANES), 0)
        dcs = jnp.zeros((Q, LANES), f32)
        ddt = jnp.zeros((Q, LANES), f32)
        for g in range(SSM_GROUPS):
            b0 = D_SSM + g * SSM_N
            c0 = D_SSM + D_BC + g * SSM_N
            bg = x_ref[:, b0:b0 + SSM_N]
            cg = x_ref[:, c0:c0 + SSM_N]
            bg_b = bg.astype(bf16)
            cg_b = cg.astype(bf16)
            cgt_b = cg.T.astype(bf16)
            cb = _dot_nt(cg_b, bg_b)
            dg_acc = jnp.zeros((Q, Q), f32)
            dc_acc = jnp.zeros((Q, SSM_N), f32)
            db_acc = jnp.zeros((Q, SSM_N), f32)
            for hh in range(SSM_HPG):
                h = g * SSM_HPG + hh
                sl = slice(h * SSM_P, (h + 1) * SSM_P)
                seg = cs[:, h:h + 1] - cst[h:h + 1, :]
                lm = jnp.exp(jnp.where(causal, seg, -jnp.inf))
                mm = cb * lm
                xh = x_ref[:, sl]
                dth = dt[:, h:h + 1]
                xdt = xh * dth
                xdt_b = xdt.astype(bf16)
                dyh = dy_ref[:, sl]
                dyh_b = dyh.astype(bf16)
                ecs_h = ecs[:, h:h + 1]
                dte_h = dte[:, h:h + 1]
                hth = hin_ref[0, h]
                hth_b = hth.astype(bf16)
                dht = dht_sc[h]
                dht_b = dht.astype(bf16)
                y_off = _dot(cg_b, hth_b) * ecs_h
                dcs_h = jnp.sum(dyh * y_off, axis=1, keepdims=True)
                dye_b = (dyh * ecs_h).astype(bf16)
                dc_acc = dc_acc + _dot_nt(dye_b, hth_b)
                dht_new = dht * e_last[:, h:h + 1] + _dot(cgt_b, dye_b)
                dm = _dot_nt(dyh_b, xdt_b)
                w = dm * mm
                dcs_h = dcs_h + jnp.sum(w, axis=1, keepdims=True) - jnp.sum(w.T, axis=1, keepdims=True)
                dxdt = _dot_tn(mm.astype(bf16), dyh_b)
                dg_acc = dg_acc + dm * lm
                e = _dot(bg_b, dht_b)
                dxdt = dxdt + e * dte_h
                t = jnp.sum(e * xdt, axis=1, keepdims=True) * dte_h
                dcs_h = dcs_h - t
                dlast = jnp.sum(t) + jnp.sum(e_last[:, h:h + 1]) * jnp.sum(dht * hth)
                db_acc = db_acc + _dot_nt((xdt * dte_h).astype(bf16), dht_b)
                dht_sc[h] = dht_new
                dx_ref[:, sl] = dxdt * dth + dexp_ref[:, sl] * dyh
                onehot = lane == h
                ddt = ddt + jnp.where(onehot, jnp.sum(dxdt * xh, axis=1, keepdims=True), 0.0)
                dcs = dcs + jnp.where(onehot, dcs_h, 0.0) + jnp.where(onehot & (row == Q - 1), dlast, 0.0)
            dg_b = dg_acc.astype(bf16)
            dx_ref[:, c0:c0 + SSM_N] = dc_acc + _dot(dg_b, bg_b)
            dx_ref[:, b0:b0 + SSM_N] = db_acc + _dot_tn(dg_b, cg_b)
        da = _dot_hi(triu, dcs)
        ddt_ref[...] = ddt + da * arow_ref[...]
        da_ref[...] += jnp.sum(da * dt, axis=0, keepdims=True)

    return pl.pallas_call(
        body, name="ssd_bwd", grid=(nch,),
        in_specs=[pl.BlockSpec((Q, D_XBC), lambda c: (rev(c), 0)),
                  pl.BlockSpec((Q, LANES), lambda c: (rev(c), 0)),
                  pl.BlockSpec((SSM_HEADS, Q), lambda c: (0, rev(c))),
                  _full((1, LANES)), _full((SSM_HEADS, LANES)),
                  pl.BlockSpec((1, SSM_HEADS, SSM_N, SSM_P), lambda c: (rev(c), 0, 0, 0)),
                  pl.BlockSpec((Q, D_SSM), lambda c: (rev(c), 0)),
                  _full((1, D_SSM))],
        out_specs=[pl.BlockSpec((Q, D_XBC), lambda c: (rev(c), 0)),
                   pl.BlockSpec((Q, LANES), lambda c: (rev(c), 0)),
                   _full((1, LANES))],
        out_shape=[jax.ShapeDtypeStruct((M, D_XBC), f32), jax.ShapeDtypeStruct((M, LANES), f32),
                   jax.ShapeDtypeStruct((1, LANES), f32)],
        scratch_shapes=[pltpu.VMEM((SSM_HEADS, SSM_N, SSM_P), f32)],
        compiler_params=_cparams(("arbitrary",)),
    )(xbc_c, dtp, dtt, a_row, a_col, hin, dy, d_exp)


def _gate_norm_fwd(y, xbc_c, z, d_exp, g):
    M = y.shape[0]
    tm = _rt(M)
    gw = D_SSM // SSM_GROUPS

    def body(y_ref, x_ref, z_ref, d_ref, g_ref, o_ref):
        yg = (y_ref[...] + d_ref[...] * x_ref[...]) * _silu(z_ref[...])
        for gi in range(SSM_GROUPS):
            blk = yg[:, gi * gw:(gi + 1) * gw]
            o_ref[:, gi * gw:(gi + 1) * gw] = (blk * _rstd(blk) * g_ref[:, gi * gw:(gi + 1) * gw]).astype(bf16)

    return pl.pallas_call(
        body, name="gate_norm_fwd", grid=(M // tm,),
        in_specs=[_row(D_SSM, tm), _row(D_SSM, tm), _row(D_SSM, tm), _full((1, D_SSM)), _full((1, D_SSM))],
        out_specs=_row(D_SSM, tm), out_shape=jax.ShapeDtypeStruct((M, D_SSM), bf16),
        compiler_params=_cparams(("parallel",)),
    )(y, xbc_c, z, d_exp, g)


def _gate_norm_bwd(y, xbc_c, z, d_exp, g, dout, head_ind):
    M = y.shape[0]
    tm = _rt(M)
    nt = M // tm
    gw = D_SSM // SSM_GROUPS

    def body(y_ref, x_ref, z_ref, d_ref, g_ref, do_ref, ind_ref, dy_ref, dz_ref, dg_ref, dd_ref, ddc_sc):
        i = pl.program_id(0)

        @pl.when(i == 0)
        def _():
            dg_ref[...] = jnp.zeros_like(dg_ref)
            ddc_sc[...] = jnp.zeros_like(ddc_sc)

        zv = z_ref[...]
        xv = x_ref[...]
        s = _silu(zv)
        yd = y_ref[...] + d_ref[...] * xv
        yg = yd * s
        dov = do_ref[...]
        for gi in range(SSM_GROUPS):
            sl = slice(gi * gw, (gi + 1) * gw)
            dyg, dgp = _rms_bwd_math(yg[:, sl], g_ref[:, sl], dov[:, sl])
            dg_ref[:, sl] += jnp.sum(dgp, axis=0, keepdims=True)
            dyd = dyg * s[:, sl]
            dy_ref[:, sl] = dyd
            dz_ref[:, sl] = (dyg * yd[:, sl] * _dsilu(zv[:, sl])).astype(bf16)
            ddc_sc[:, sl] += jnp.sum(dyd * xv[:, sl], axis=0, keepdims=True)

        @pl.when(i == nt - 1)
        def _():
            dd_ref[...] = _dot_hi(ddc_sc[...], ind_ref[...])

    return pl.pallas_call(
        body, name="gate_norm_bwd", grid=(nt,),
        in_specs=[_row(D_SSM, tm), _row(D_SSM, tm), _row(D_SSM, tm), _full((1, D_SSM)), _full((1, D_SSM)),
                  _row(D_SSM, tm), _full((D_SSM, LANES))],
        out_specs=[_row(D_SSM, tm), _row(D_SSM, tm), _full((1, D_SSM)), _full((1, LANES))],
        out_shape=[jax.ShapeDtypeStruct((M, D_SSM), f32), jax.ShapeDtypeStruct((M, D_SSM), bf16),
                   jax.ShapeDtypeStruct((1, D_SSM), f32), jax.ShapeDtypeStruct((1, LANES), f32)],
        scratch_shapes=[pltpu.VMEM((1, D_SSM), f32)],
        compiler_params=_cparams(("arbitrary",)),
    )(y, xbc_c, z, d_exp, g, dout, head_ind)


_PEER_FLIPS = [(0, 0, 1), (0, 1, 0), (0, 1, 1), (1, 0, 0), (1, 0, 1), (1, 1, 0), (1, 1, 1)]


def _exchange(arrays, scatter, name):
    n = len(arrays)

    def body(*refs):
        ins, outs = refs[:n], refs[n:2 * n]
        send_sems, recv_sems, loc_sems = refs[2 * n:]
        x, y, c = lax.axis_index("x"), lax.axis_index("y"), lax.axis_index("c")
        me = 4 * x + 2 * y + c
        local = []
        for a in range(n):
            src = ins[a].at[me] if scatter else ins[a]
            lc = pltpu.make_async_copy(src, outs[a].at[me], loc_sems.at[a])
            lc.start()
            local.append(lc)
        remote = []
        for p, (fx, fy, fc) in enumerate(_PEER_FLIPS):
            tx = 1 - x if fx else x
            ty = 1 - y if fy else y
            tc = 1 - c if fc else c
            tgt = 4 * tx + 2 * ty + tc
            for a in range(n):
                src = ins[a].at[tgt] if scatter else ins[a]
                cp = pltpu.make_async_remote_copy(
                    src_ref=src, dst_ref=outs[a].at[me],
                    send_sem=send_sems.at[p * n + a], recv_sem=recv_sems.at[p * n + a],
                    device_id=(tx, ty, tc), device_id_type=_MESH)
                cp.start()
                remote.append(cp)
        for cp in remote:
            cp.wait()
        for lc in local:
            lc.wait()

    any_spec = pl.BlockSpec(memory_space=pl.ANY)
    out_shape = []
    for arr in arrays:
        shp = arr.shape if scatter else (N_DEV,) + arr.shape
        out_shape.append(jax.ShapeDtypeStruct(shp, arr.dtype))
    return pl.pallas_call(
        body, name=name, in_specs=[any_spec] * n, out_specs=[any_spec] * n, out_shape=out_shape,
        scratch_shapes=[pltpu.SemaphoreType.DMA((7 * n,)), pltpu.SemaphoreType.DMA((7 * n,)),
                        pltpu.SemaphoreType.DMA((n,))],
    )(*arrays)


def _adamw(parts, w, m, v, name):
    R, C = w.shape
    tr = _pick(R, (PACK_ROW_TILE, 16, 8))
    c1 = 1.0 - ADAM_B1 ** ADAM_STEP
    c2 = 1.0 - ADAM_B2 ** ADAM_STEP

    def body(p_ref, w_ref, m_ref, v_ref, g_ref, d_ref, nm_ref, nv_ref):
        g = p_ref[0].astype(f32)
        for s in range(1, N_DEV):
            g = g + p_ref[s].astype(f32)
        mn = ADAM_B1 * m_ref[...] + (1.0 - ADAM_B1) * g
        vn = ADAM_B2 * v_ref[...] + (1.0 - ADAM_B2) * (g * g)
        m_hat = mn / c1
        v_hat = vn / c2
        g_ref[...] = g
        d_ref[...] = -ADAM_LR * (m_hat / (jnp.sqrt(v_hat) + ADAM_EPS) + ADAM_WD * w_ref[...])
        nm_ref[...] = mn
        nv_ref[...] = vn

    spec = pl.BlockSpec((tr, C), lambda i: (i, 0))
    return pl.pallas_call(
        body, name=name, grid=(R // tr,),
        in_specs=[pl.BlockSpec((N_DEV, tr, C), lambda i: (0, i, 0)), spec, spec, spec],
        out_specs=[spec] * 4, out_shape=[jax.ShapeDtypeStruct((R, C), f32)] * 4,
        compiler_params=_cparams(("parallel",)),
    )(parts, w, m, v)


def _flat_rows(a, lead_ndim):
    lead = a.shape[:lead_ndim]
    n = int(np.prod(a.shape[lead_ndim:]))
    a = a.reshape(lead + (n,))
    pad = (-n) % PACK_W
    if pad:
        a = jnp.pad(a, [(0, 0)] * lead_ndim + [(0, pad)])
    return a.reshape(lead + ((n + pad) // PACK_W, PACK_W))


def _pack(arrays, lead_ndim, total_rows, dtype):
    rows = [_flat_rows(a.astype(dtype), lead_ndim) for a in arrays]
    cat = jnp.concatenate(rows, axis=lead_ndim)
    pad = total_rows - cat.shape[lead_ndim]
    if pad:
        cat = jnp.pad(cat, [(0, 0)] * lead_ndim + [(0, pad), (0, 0)])
    return cat


def _unpack(buf, shapes, lead_ndim):
    out = []
    r = 0
    lead = buf.shape[:lead_ndim]
    for shp in shapes:
        n = int(np.prod(shp))
        nr = -(-n // PACK_W)
        piece = lax.slice_in_dim(buf, r, r + nr, axis=lead_ndim)
        piece = piece.reshape(lead + (nr * PACK_W,))
        piece = lax.slice_in_dim(piece, 0, n, axis=lead_ndim)
        out.append(piece.reshape(lead + tuple(shp)))
        r += nr
    return out


def _round_up(n, m):
    return -(-n // m) * m


def kernel(x, meta_tokens, norm_mix_pre, norm_mix_post, norm_ffn_pre, norm_ffn_post, w_in, q_a_norm, w_uq, kv_a_norm, w_ukv, attn_out_norm, ssm_conv_w, ssm_conv_b, ssm_dt_bias, ssm_A_log, ssm_D, ssm_norm, w_out, w_up, ffn_conv_w, ffn_conv_b, w_down, loss_target, m_meta_tokens, m_norm_mix_pre, m_norm_mix_post, m_norm_ffn_pre, m_norm_ffn_post, m_w_in, m_q_a_norm, m_w_uq, m_kv_a_norm, m_w_ukv, m_attn_out_norm, m_ssm_conv_w, m_ssm_conv_b, m_ssm_dt_bias, m_ssm_A_log, m_ssm_D, m_ssm_norm, m_w_out, m_w_up, m_ffn_conv_w, m_ffn_conv_b, m_w_down, v_meta_tokens, v_norm_mix_pre, v_norm_mix_post, v_norm_ffn_pre, v_norm_ffn_post, v_w_in, v_q_a_norm, v_w_uq, v_kv_a_norm, v_w_ukv, v_attn_out_norm, v_ssm_conv_w, v_ssm_conv_b, v_ssm_dt_bias, v_ssm_A_log, v_ssm_D, v_ssm_norm, v_w_out, v_w_up, v_ffn_conv_w, v_ffn_conv_b, v_w_down):
    seq = x.shape[1]
    n_real = N_META + seq
    Lp = _round_up(n_real, 768) if n_real > 2048 else _round_up(n_real, ROW_TILE)
    D = D_MODEL

    sharded_w = [w_in, w_uq, w_ukv, w_out, w_up, w_down]
    sharded_s = [meta_tokens, ssm_conv_w, ffn_conv_w]
    sharded_names = sharded_w + sharded_s
    sharded_m = [m_w_in, m_w_uq, m_w_ukv, m_w_out, m_w_up, m_w_down, m_meta_tokens, m_ssm_conv_w, m_ffn_conv_w]
    sharded_v = [v_w_in, v_w_uq, v_w_ukv, v_w_out, v_w_up, v_w_down, v_meta_tokens, v_ssm_conv_w, v_ffn_conv_w]
    repl_w = [norm_mix_pre, norm_mix_post, norm_ffn_pre, norm_ffn_post, q_a_norm, kv_a_norm, attn_out_norm,
              ssm_conv_b, ssm_dt_bias, ssm_A_log, ssm_D, ssm_norm, ffn_conv_b]
    repl_m = [m_norm_mix_pre, m_norm_mix_post, m_norm_ffn_pre, m_norm_ffn_post, m_q_a_norm, m_kv_a_norm,
              m_attn_out_norm, m_ssm_conv_b, m_ssm_dt_bias, m_ssm_A_log, m_ssm_D, m_ssm_norm, m_ffn_conv_b]
    repl_v = [v_norm_mix_pre, v_norm_mix_post, v_norm_ffn_pre, v_norm_ffn_post, v_q_a_norm, v_kv_a_norm,
              v_attn_out_norm, v_ssm_conv_b, v_ssm_dt_bias, v_ssm_A_log, v_ssm_D, v_ssm_norm, v_ffn_conv_b]

    big_rows = _round_up(sum(-(-int(np.prod(a.shape)) // PACK_W) for a in sharded_w), PACK_ROW_TILE)
    small_rows = _round_up(sum(-(-int(np.prod(a.shape)) // PACK_W) for a in sharded_s), 16)
    wb = _pack(sharded_w, 0, big_rows, bf16)
    ws = _pack(sharded_s, 0, small_rows, f32)
    wb_all, ws_all = _exchange([wb, ws], False, "gather_weights")
    g_w_in, g_w_uq, g_w_ukv, g_w_out, g_w_up, g_w_down = _unpack(wb_all, [a.shape for a in sharded_w], 1)
    g_meta, g_sconv, g_fconv = _unpack(ws_all, [a.shape for a in sharded_s], 1)

    def cols(gathered):
        t = gathered[:, 0]
        return jnp.transpose(t, (1, 0, 2)).reshape(t.shape[1], N_DEV * t.shape[2])

    win = cols(g_w_in)
    o = np.cumsum((0, Q_RANK, KV_RANK, QK_ROPE, D_SSM, D_XBC, SSM_HEADS))
    w_q, w_kv = win[:, o[0]:o[1]], win[:, o[1]:o[2]]
    w_rope = jnp.pad(win[:, o[2]:o[3]], ((0, 0), (0, LANES - QK_ROPE)))
    w_z, w_xbc = win[:, o[3]:o[4]], win[:, o[4]:o[5]]
    w_dt = jnp.pad(win[:, o[5]:o[6]], ((0, 0), (0, LANES - SSM_HEADS)))
    wuq = g_w_uq.reshape(Q_RANK, MLA_HEADS, QK_NOPE + QK_ROPE)
    wuq = jnp.pad(wuq, ((0, 0), (0, 0), (0, QK_PAD - QK_NOPE - QK_ROPE))).reshape(Q_RANK, MLA_HEADS * QK_PAD)
    wukv = g_w_ukv.reshape(KV_RANK, MLA_HEADS * (QK_NOPE + V_DIM))
    wout = g_w_out.reshape(D_ATTN + D_SSM, D)
    wout_a, wout_s = wout[:D_ATTN], wout[D_ATTN:]
    wup = cols(g_w_up)
    wdown = g_w_down.reshape(D_FF, D)
    meta_full = jnp.transpose(g_meta, (1, 0, 2)).reshape(N_META, D)
    sconv_w = jnp.pad(cols(g_sconv), ((0, SUBLANES - SSM_CONV), (0, 0)))
    fconv_w = jnp.pad(cols(g_fconv), ((0, SUBLANES - FFN_CONV), (0, 0)))

    pos = jnp.arange(Lp, dtype=f32)
    inv = ROPE_THETA ** (-jnp.arange(0, QK_ROPE, 2, dtype=f32) / QK_ROPE)
    ang = pos[:, None] * inv[None, :]
    cs_, sn_ = jnp.cos(ang), jnp.sin(ang)
    zpad = jnp.zeros((Lp, LANES - QK_ROPE), f32)
    cos_t = jnp.concatenate([cs_, cs_, zpad], axis=1)
    sin_t = jnp.concatenate([-sn_, sn_, zpad], axis=1)
    dt_bias_p = jnp.pad(ssm_dt_bias, ((0, 0), (0, LANES - SSM_HEADS)))
    a_neg = -jnp.exp(ssm_A_log)
    a_row = jnp.pad(a_neg, ((0, 0), (0, LANES - SSM_HEADS)))
    a_col = jnp.broadcast_to(a_neg.reshape(SSM_HEADS, 1), (SSM_HEADS, LANES))
    d_exp = jnp.repeat(ssm_D, SSM_P, axis=1)
    head_ind = (jnp.arange(D_SSM)[:, None] // SSM_P == jnp.arange(LANES)[None, :]).astype(f32)

    xb = x[0]
    h0 = jnp.concatenate([meta_full, xb, jnp.zeros((Lp - n_real, D), f32)], axis=0)
    tgt = jnp.pad(loss_target[0], ((N_META, Lp - n_real), (0, 0)))
    hn1 = _rms_fwd(h0, norm_mix_pre, bf16, "norm_mix_pre")
    q_c = _mm([(hn1, w_q)], f32, False, "proj_q")
    kv_c = _mm([(hn1, w_kv)], f32, False, "proj_kv")
    kpe_raw = _mm([(hn1, w_rope)], f32, False, "proj_rope")
    z = _mm([(hn1, w_z)], f32, False, "proj_z")
    xbc = _mm([(hn1, w_xbc)], f32, False, "proj_xbc")
    dt_raw = _mm([(hn1, w_dt)], f32, False, "proj_dt")

    qn = _rms_fwd(q_c, q_a_norm, bf16, "norm_q")
    kvn = _rms_fwd(kv_c, kv_a_norm, bf16, "norm_kv")
    qh = _up_q_rope(qn, wuq, cos_t, sin_t)
    kh, vh = _up_kv_rope(kvn, wukv, kpe_raw, cos_t, sin_t)
    attn, lse = _flash_fwd(qh, kh, vh)
    an = _rms_fwd(attn, attn_out_norm, bf16, "norm_attn_out")

    xbc_c = _ssm_conv_fwd(xbc, sconv_w, ssm_conv_b)
    dtp = _dt_fwd(dt_raw, dt_bias_p)
    dtt = jnp.transpose(dtp[:, :SSM_HEADS])
    y_ssd, hin = _ssd_fwd(xbc_c, dtp, dtt, a_row, a_col)
    ssm = _gate_norm_fwd(y_ssd, xbc_c, z, d_exp, ssm_norm)

    mix = _mm([(an, wout_a), (ssm, wout_s)], f32, False, "out_proj")
    h1, hn2 = _resid_norm(h0, mix, norm_mix_post, norm_ffn_pre)
    up = _mm([(hn2, wup)], f32, False, "ffn_up")
    act = _ffn_gate_fwd(up, fconv_w, ffn_conv_b)
    down = _mm([(act, wdown)], f32, False, "ffn_down")
    dh2, d_down, dg_ffn_post, loss_part = _final(h1, down, norm_ffn_post, tgt, n_real)

    d_act = _mm([(d_down, wdown)], f32, True, "ffn_down_dx")
    dw_down = _mm_tn(act, d_down, "ffn_down_dw")
    dup_g, dup_v, dwc_g, dwc_v, dbc_g, dbc_v = _ffn_gate_bwd(up, fconv_w, ffn_conv_b, d_act)
    d_hn2 = _mm([(dup_g, wup[:, :D_FF]), (dup_v, wup[:, D_FF:])], f32, True, "ffn_up_dx")
    dw_up = jnp.concatenate([_mm_tn(hn2, dup_g, "ffn_up_dw_g"), _mm_tn(hn2, dup_v, "ffn_up_dw_v")], axis=1)
    dh1, d_mix, dg_ffn_pre, dg_mix_post = _mid_bwd(h1, norm_ffn_pre, d_hn2, dh2, mix, norm_mix_post)
    d_an = _mm([(d_mix, wout_a)], f32, True, "out_proj_dx_a")
    d_ssm = _mm([(d_mix, wout_s)], f32, True, "out_proj_dx_s")
    dw_out = jnp.concatenate([_mm_tn(an, d_mix, "out_proj_dw_a"), _mm_tn(ssm, d_mix, "out_proj_dw_s")], axis=0)

    d_attn, dg_attn_out = _rms_bwd(attn, attn_out_norm, d_an, f32, "norm_attn_out_bwd")
    do_h, delta = _attn_delta(attn, d_attn)
    dqh, dkh, dvh = _flash_bwd(qh, kh, vh, do_h, lse, delta)
    dq_full = _rope_q_bwd(dqh, cos_t, sin_t)
    dkv_full, d_kpe_raw = _rope_k_bwd(dkh, dvh, cos_t, sin_t)
    d_qn = _mm([(dq_full, wuq)], f32, True, "up_q_dx")
    dw_uq = _mm_tn(qn, dq_full, "up_q_dw")
    d_kvn = _mm([(dkv_full, wukv)], f32, True, "up_kv_dx")
    dw_ukv = _mm_tn(kvn, dkv_full, "up_kv_dw")
    d_q_c, dg_q = _rms_bwd(q_c, q_a_norm, d_qn, bf16, "norm_q_bwd")
    d_kv_c, dg_kv = _rms_bwd(kv_c, kv_a_norm, d_kvn, bf16, "norm_kv_bwd")

    dy_ssd, dz, dg_ssm, dd_heads = _gate_norm_bwd(y_ssd, xbc_c, z, d_exp, ssm_norm, d_ssm, head_ind)
    d_xbc_c, ddt, da_heads = _ssd_bwd(xbc_c, dtp, dtt, a_row, a_col, hin, dy_ssd, d_exp)
    d_xbc, dw_sconv, db_sconv = _ssm_conv_bwd(xbc, sconv_w, ssm_conv_b, d_xbc_c)
    d_dt_raw, d_dt_bias = _dt_bwd(dt_raw, dt_bias_p, ddt)

    segs = [(d_q_c, w_q), (d_kv_c, w_kv), (d_kpe_raw, w_rope), (dz, w_z), (d_xbc, w_xbc), (d_dt_raw, w_dt)]
    d_hn1 = _mm(segs, f32, True, "proj_dx")
    dw_q = _mm_tn(hn1, d_q_c, "proj_dw_q")
    dw_kv = _mm_tn(hn1, d_kv_c, "proj_dw_kv")
    dw_rope = _mm_tn(hn1, d_kpe_raw, "proj_dw_rope")
    dw_z = _mm_tn(hn1, dz, "proj_dw_z")
    dw_xbc = _mm_tn(hn1, d_xbc, "proj_dw_xbc")
    dw_dt = _mm_tn(hn1, d_dt_raw, "proj_dw_dt")
    dh0, dg_mix_pre = _rms_bwd(h0, norm_mix_pre, d_hn1, f32, "norm_mix_pre_bwd", residual=dh1)

    grad_x = dh0[N_META:n_real][None]
    d_meta = dh0[:N_META]

    dw_in = jnp.concatenate([dw_q, dw_kv, dw_rope[:, :QK_ROPE], dw_z, dw_xbc, dw_dt[:, :SSM_HEADS]], axis=1)

    def col_blocks(gm):
        r, cc = gm.shape
        return jnp.transpose(gm.reshape(r, N_DEV, cc // N_DEV), (1, 0, 2))

    dw_uq3 = dw_uq.reshape(Q_RANK, MLA_HEADS, QK_PAD)[:, :, :QK_NOPE + QK_ROPE]
    dest_blocks = [
        col_blocks(dw_in),
        dw_uq3.reshape(N_DEV, Q_RANK // N_DEV, MLA_HEADS, QK_NOPE + QK_ROPE),
        dw_ukv.reshape(N_DEV, KV_RANK // N_DEV, MLA_HEADS, QK_NOPE + V_DIM),
        dw_out.reshape(N_DEV, (D_ATTN + D_SSM) // N_DEV, D),
        col_blocks(dw_up),
        dw_down.reshape(N_DEV, D_FF // N_DEV, D),
        col_blocks(d_meta),
        col_blocks(dw_sconv[:SSM_CONV]),
        col_blocks(jnp.concatenate([dwc_g, dwc_v], axis=1)[:FFN_CONV]),
    ]
    grad_rows = _round_up(sum(-(-int(np.prod(a.shape[1:])) // PACK_W) for a in dest_blocks), PACK_ROW_TILE)
    gpack = _pack(dest_blocks, 1, grad_rows, bf16)
    (gparts,) = _exchange([gpack], True, "scatter_grads")
    wpack = _pack([a[None] for a in sharded_names], 1, grad_rows, f32)[0]
    mpack = _pack([a[None] for a in sharded_m], 1, grad_rows, f32)[0]
    vpack = _pack([a[None] for a in sharded_v], 1, grad_rows, f32)[0]
    outs_big = _adamw(gparts, wpack, mpack, vpack, "adamw_sharded")
    shard_shapes = [a.shape for a in sharded_names]
    g_sh, d_sh, m_sh, v_sh = [_unpack(b[None], shard_shapes, 1) for b in outs_big]
    g_sh, d_sh, m_sh, v_sh = [[t[0] for t in lst] for lst in (g_sh, d_sh, m_sh, v_sh)]

    dg_alog = da_heads[:, :SSM_HEADS] * a_neg
    repl_g = [dg_mix_pre, dg_mix_post, dg_ffn_pre, dg_ffn_post, dg_q, dg_kv, dg_attn_out, db_sconv,
              d_dt_bias[:, :SSM_HEADS], dg_alog, dd_heads[:, :SSM_HEADS], dg_ssm,
              jnp.concatenate([dbc_g, dbc_v], axis=1)]
    loss_vec = loss_part[:, :1]
    small_total = _round_up(sum(-(-int(np.prod(a.shape)) // PACK_W) for a in repl_g) + 1, 16)
    spack = _pack(repl_g + [loss_vec], 0, small_total, f32)
    (sparts,) = _exchange([spack], False, "gather_small_grads")
    zero1 = jnp.zeros((1, 1), f32)
    rw = _pack(repl_w + [zero1], 0, small_total, f32)
    rm = _pack(repl_m + [zero1], 0, small_total, f32)
    rv = _pack(repl_v + [zero1], 0, small_total, f32)
    outs_small = _adamw(sparts, rw, rm, rv, "adamw_replicated")
    repl_shapes = [a.shape for a in repl_w] + [(1, 1)]
    g_rp, d_rp, m_rp, v_rp = [_unpack(b, repl_shapes, 0) for b in outs_small]
    loss = g_rp[-1][0, 0]

    order = ["meta_tokens", "norm_mix_pre", "norm_mix_post", "norm_ffn_pre", "norm_ffn_post", "w_in", "q_a_norm",
             "w_uq", "kv_a_norm", "w_ukv", "attn_out_norm", "ssm_conv_w", "ssm_conv_b", "ssm_dt_bias", "ssm_A_log",
             "ssm_D", "ssm_norm", "w_out", "w_up", "ffn_conv_w", "ffn_conv_b", "w_down"]
    sh_names = ["w_in", "w_uq", "w_ukv", "w_out", "w_up", "w_down", "meta_tokens", "ssm_conv_w", "ffn_conv_w"]
    rp_names = ["norm_mix_pre", "norm_mix_post", "norm_ffn_pre", "norm_ffn_post", "q_a_norm", "kv_a_norm",
                "attn_out_norm", "ssm_conv_b", "ssm_dt_bias", "ssm_A_log", "ssm_D", "ssm_norm", "ffn_conv_b"]

    def lookup(sh_list, rp_list):
        d = {n: t for n, t in zip(sh_names, sh_list)}
        d.update({n: t for n, t in zip(rp_names, rp_list)})
        return [d[n] for n in order]

    return (loss, grad_x, *lookup(g_sh, g_rp), *lookup(d_sh, d_rp), *lookup(m_sh, m_rp), *lookup(v_sh, v_rp))
```

```python
import functools
import math

import jax
import jax.numpy as jnp
import numpy as np
from jax import lax
from jax.experimental import pallas as pl
from jax.experimental.pallas import tpu as pltpu

f32 = jnp.float32
bf16 = jnp.bfloat16

D_MODEL = 1024
SEQ = 8192
N_META = 16
MLA_HEADS = 8
QK_NOPE = 128
QK_ROPE = 64
V_DIM = 128
Q_RANK = 384
KV_RANK = 256
ROPE_THETA = 10000.0
SOFTMAX_SCALE = (QK_NOPE + QK_ROPE) ** -0.5
D_ATTN = MLA_HEADS * V_DIM
SSM_HEADS = 16
SSM_P = 64
SSM_GROUPS = 2
SSM_HPG = SSM_HEADS // SSM_GROUPS
SSM_N = 128
SSM_CONV = 4
CHUNK = 128
D_SSM = SSM_HEADS * SSM_P
D_BC = SSM_GROUPS * SSM_N
D_XBC = D_SSM + 2 * D_BC
D_FF = 2816
FFN_CONV = 3
EPS = 1e-6
D_IN = Q_RANK + KV_RANK + QK_ROPE + D_SSM + D_XBC + SSM_HEADS
QK_PAD = 256
N_DEV = 8

ADAM_LR = 0.001
ADAM_B1 = 0.9
ADAM_B2 = 0.999
ADAM_EPS = 1e-08
ADAM_WD = 0.01
ADAM_STEP = 10

LANES = 128
SUBLANES = 8
ROW_TILE = 256
VMEM_LIMIT = 56 * 1024 * 1024
PACK_W = 1024
PACK_ROW_TILE = 128
NEG = -1e30
LOG2E = math.log2(math.e)
LN2 = math.log(2.0)
Q_PRESCALE = SOFTMAX_SCALE * LOG2E

_MESH = pl.DeviceIdType.MESH


def _pick(n, prefs):
    for p in prefs:
        if n % p == 0:
            return p
    return n


def _rt(m):
    return _pick(m, (384, ROW_TILE))


def _cparams(sem):
    return pltpu.CompilerParams(dimension_semantics=sem, vmem_limit_bytes=VMEM_LIMIT)


def _row(spec_cols, tm):
    return pl.BlockSpec((tm, spec_cols), lambda i: (i, 0))


def _full(shape):
    nd = len(shape)
    return pl.BlockSpec(shape, lambda *a: (0,) * nd)


def _sigmoid(x):
    return 1.0 / (1.0 + jnp.exp(-x))


def _silu(x):
    return x * _sigmoid(x)


def _dsilu(x):
    s = _sigmoid(x)
    return s * (1.0 + x * (1.0 - s))


def _dot(a, b):
    return jnp.dot(a, b, preferred_element_type=f32)


def _dot_nt(a, b):
    return lax.dot_general(a, b, (((1,), (1,)), ((), ())), preferred_element_type=f32)


def _dot_tn(a, b):
    return lax.dot_general(a, b, (((0,), (0,)), ((), ())), preferred_element_type=f32)


def _dot_hi(a, b):
    return jnp.dot(a, b, precision=lax.Precision.HIGHEST, preferred_element_type=f32)


def _mm(pairs, out_dtype, trans_b, name):
    n = len(pairs)
    M = pairs[0][0].shape[0]
    N = pairs[0][1].shape[0] if trans_b else pairs[0][1].shape[1]
    tm = _pick(M, (768, 512, 256))
    tn = _pick(N, (512, 1408, 384, 256, 128))

    def body(*refs):
        o_ref = refs[2 * n]
        acc = None
        for p in range(n):
            a = refs[2 * p][...].astype(bf16)
            b = refs[2 * p + 1][...].astype(bf16)
            r = _dot_nt(a, b) if trans_b else _dot(a, b)
            acc = r if acc is None else acc + r
        o_ref[...] = acc.astype(out_dtype)

    in_specs, args = [], []
    for a, b in pairs:
        k = a.shape[1]
        in_specs.append(pl.BlockSpec((tm, k), lambda i, j: (i, 0)))
        if trans_b:
            in_specs.append(pl.BlockSpec((tn, k), lambda i, j: (j, 0)))
        else:
            in_specs.append(pl.BlockSpec((k, tn), lambda i, j: (0, j)))
        args += [a, b]
    return pl.pallas_call(
        body, name=name, grid=(M // tm, N // tn), in_specs=in_specs,
        out_specs=pl.BlockSpec((tm, tn), lambda i, j: (i, j)),
        out_shape=jax.ShapeDtypeStruct((M, N), out_dtype),
        compiler_params=_cparams(("parallel", "parallel")),
    )(*args)


def _mm_tn(a, g, name):
    M, K = a.shape
    N = g.shape[1]
    tm = _pick(M, (768, 512, 256))
    tk = _pick(K, (1024, 1408, 512, 384, 256))
    tn = _pick(N, (1024, 1408, 512, 384, 256, 128))

    def body(a_ref, g_ref, o_ref):
        @pl.when(pl.program_id(2) == 0)
        def _():
            o_ref[...] = jnp.zeros_like(o_ref)

        o_ref[...] += _dot_tn(a_ref[...].astype(bf16), g_ref[...].astype(bf16))

    return pl.pallas_call(
        body, name=name, grid=(K // tk, N // tn, M // tm),
        in_specs=[pl.BlockSpec((tm, tk), lambda k, j, m: (m, k)),
                  pl.BlockSpec((tm, tn), lambda k, j, m: (m, j))],
        out_specs=pl.BlockSpec((tk, tn), lambda k, j, m: (k, j)),
        out_shape=jax.ShapeDtypeStruct((K, N), f32),
        compiler_params=_cparams(("parallel", "parallel", "arbitrary")),
    )(a, g)


def _rstd(x):
    return lax.rsqrt(jnp.mean(x * x, axis=-1, keepdims=True) + EPS)


def _rms_bwd_math(x, g, dy):
    r = _rstd(x)
    xh = x * r
    dn = dy * g
    dx = r * (dn - xh * jnp.mean(dn * xh, axis=-1, keepdims=True))
    return dx, dy * xh


def _rms_fwd(x, g, out_dtype, name):
    M, K = x.shape
    tm = _rt(M)

    def body(x_ref, g_ref, o_ref):
        xv = x_ref[...]
        o_ref[...] = (xv * _rstd(xv) * g_ref[...]).astype(out_dtype)

    return pl.pallas_call(
        body, name=name, grid=(M // tm,), in_specs=[_row(K, tm), _full((1, K))],
        out_specs=_row(K, tm), out_shape=jax.ShapeDtypeStruct((M, K), out_dtype),
        compiler_params=_cparams(("parallel",)),
    )(x, g)


def _rms_bwd(x, g, dy, out_dtype, name, residual=None):
    M, K = x.shape
    tm = _rt(M)
    has_res = residual is not None

    def body(*refs):
        if has_res:
            x_ref, g_ref, dy_ref, r_ref, dx_ref, dg_ref = refs
        else:
            x_ref, g_ref, dy_ref, dx_ref, dg_ref = refs

        @pl.when(pl.program_id(0) == 0)
        def _():
            dg_ref[...] = jnp.zeros_like(dg_ref)

        dx, dgp = _rms_bwd_math(x_ref[...], g_ref[...], dy_ref[...].astype(f32))
        if has_res:
            dx = dx + r_ref[...]
        dx_ref[...] = dx.astype(out_dtype)
        dg_ref[...] += jnp.sum(dgp, axis=0, keepdims=True)

    ins = [x, g, dy] + ([residual] if has_res else [])
    in_specs = [_row(K, tm), _full((1, K)), _row(K, tm)] + ([_row(K, tm)] if has_res else [])
    return pl.pallas_call(
        body, name=name, grid=(M // tm,), in_specs=in_specs,
        out_specs=[_row(K, tm), _full((1, K))],
        out_shape=[jax.ShapeDtypeStruct((M, K), out_dtype), jax.ShapeDtypeStruct((1, K), f32)],
        compiler_params=_cparams(("arbitrary",)),
    )(*ins)


def _resid_norm(h0, mix, g2, g3):
    M, K = h0.shape
    tm = _rt(M)

    def body(h_ref, m_ref, g2_ref, g3_ref, h1_ref, hn_ref):
        mv = m_ref[...]
        h1 = h_ref[...] + mv * _rstd(mv) * g2_ref[...]
        h1_ref[...] = h1
        hn_ref[...] = (h1 * _rstd(h1) * g3_ref[...]).astype(bf16)

    return pl.pallas_call(
        body, name="resid_norm", grid=(M // tm,),
        in_specs=[_row(K, tm), _row(K, tm), _full((1, K)), _full((1, K))],
        out_specs=[_row(K, tm), _row(K, tm)],
        out_shape=[jax.ShapeDtypeStruct((M, K), f32), jax.ShapeDtypeStruct((M, K), bf16)],
        compiler_params=_cparams(("parallel",)),
    )(h0, mix, g2, g3)


def _final(h1, down, g4, tgt, n_real):
    M, K = h1.shape
    tm = _rt(M)
    nt = M // tm

    def body(h_ref, d_ref, g_ref, t_ref, dh_ref, dd_ref, dg_ref, ls_ref, acc_ref):
        i = pl.program_id(0)

        @pl.when(i == 0)
        def _():
            dg_ref[...] = jnp.zeros_like(dg_ref)
            acc_ref[...] = jnp.zeros_like(acc_ref)

        dv = d_ref[...]
        g = g_ref[...]
        r = _rstd(dv)
        n = dv * r
        h2 = h_ref[...] + n * g
        rows = i * tm + lax.broadcasted_iota(jnp.int32, (tm, 1), 0)
        mask = ((rows >= N_META) & (rows < n_real)).astype(f32)
        diff = (h2 - t_ref[...]) * mask
        acc_ref[...] += jnp.sum(diff * diff, axis=0, keepdims=True)
        dh = diff * (1.0 / K)
        dh_ref[...] = dh
        dn = dh * g
        dd_ref[...] = (r * (dn - n * jnp.mean(dn * n, axis=-1, keepdims=True))).astype(bf16)
        dg_ref[...] += jnp.sum(dh * n, axis=0, keepdims=True)

        @pl.when(i == nt - 1)
        def _():
            ls_ref[...] = jnp.zeros((1, LANES), f32) + jnp.sum(acc_ref[...]) * (0.5 / K)

    return pl.pallas_call(
        body, name="final_loss", grid=(nt,),
        in_specs=[_row(K, tm), _row(K, tm), _full((1, K)), _row(K, tm)],
        out_specs=[_row(K, tm), _row(K, tm), _full((1, K)), _full((1, LANES))],
        out_shape=[jax.ShapeDtypeStruct((M, K), f32), jax.ShapeDtypeStruct((M, K), bf16),
                   jax.ShapeDtypeStruct((1, K), f32), jax.ShapeDtypeStruct((1, LANES), f32)],
        scratch_shapes=[pltpu.VMEM((1, K), f32)],
        compiler_params=_cparams(("arbitrary",)),
    )(h1, down, g4, tgt)


def _mid_bwd(h1, g3, d_hn2, dh2, mix, g2):
    M, K = h1.shape
    tm = _rt(M)

    def body(h_ref, g3_ref, dn_ref, dh2_ref, m_ref, g2_ref, dh1_ref, dm_ref, dg3_ref, dg2_ref):
        @pl.when(pl.program_id(0) == 0)
        def _():
            dg3_ref[...] = jnp.zeros_like(dg3_ref)
            dg2_ref[...] = jnp.zeros_like(dg2_ref)

        dx, dgp = _rms_bwd_math(h_ref[...], g3_ref[...], dn_ref[...])
        dh1 = dh2_ref[...] + dx
        dh1_ref[...] = dh1
        dg3_ref[...] += jnp.sum(dgp, axis=0, keepdims=True)
        dm, dgp2 = _rms_bwd_math(m_ref[...], g2_ref[...], dh1)
        dm_ref[...] = dm.astype(bf16)
        dg2_ref[...] += jnp.sum(dgp2, axis=0, keepdims=True)

    return pl.pallas_call(
        body, name="mid_bwd", grid=(M // tm,),
        in_specs=[_row(K, tm), _full((1, K)), _row(K, tm), _row(K, tm), _row(K, tm), _full((1, K))],
        out_specs=[_row(K, tm), _row(K, tm), _full((1, K)), _full((1, K))],
        out_shape=[jax.ShapeDtypeStruct((M, K), f32), jax.ShapeDtypeStruct((M, K), bf16),
                   jax.ShapeDtypeStruct((1, K), f32), jax.ShapeDtypeStruct((1, K), f32)],
        compiler_params=_cparams(("arbitrary",)),
    )(h1, g3, d_hn2, dh2, mix, g2)


def _conv_taps(ext_ref, w_ref, kw, tm, first):
    u = None
    for k in range(kw):
        t = ext_ref[pl.ds(first + k, tm), :] * w_ref[k:k + 1, :]
        u = t if u is None else u + t
    return u


def _fill_prev(ext_ref, x_ref, halo_ref, i, tm):
    ext_ref[0:SUBLANES, :] = jnp.where(i == 0, 0.0, halo_ref[...])
    ext_ref[SUBLANES:SUBLANES + tm, :] = x_ref[...]


def _prev_spec(tm, tc, col_of, row_axis, reversed_tiles=0):
    def imap(*ids):
        i = ids[row_axis]
        if reversed_tiles:
            i = reversed_tiles - 1 - i
        return (jnp.maximum(i * (tm // SUBLANES) - 1, 0), col_of(*ids))
    return pl.BlockSpec((SUBLANES, tc), imap)


def _conv_dx_carry(edu_ref, du, w_ref, kw, tm, first_step):
    @pl.when(first_step)
    def _():
        edu_ref[tm:tm + SUBLANES, :] = jnp.zeros((SUBLANES, edu_ref.shape[1]), f32)

    edu_ref[0:tm, :] = du
    acc = None
    for k in range(kw):
        t = edu_ref[pl.ds(kw - 1 - k, tm), :] * w_ref[k:k + 1, :]
        acc = t if acc is None else acc + t
    edu_ref[tm:tm + SUBLANES, :] = edu_ref[0:SUBLANES, :]
    return acc


def _ssm_conv_fwd(xbc, w, b):
    M, C = xbc.shape
    tm, tc, kw = ROW_TILE, C, SSM_CONV

    def body(x_ref, h_ref, w_ref, b_ref, o_ref, ext_ref):
        _fill_prev(ext_ref, x_ref, h_ref, pl.program_id(0), tm)
        u = _conv_taps(ext_ref, w_ref, kw, tm, SUBLANES - (kw - 1)) + b_ref[...]
        o_ref[...] = _silu(u)

    return pl.pallas_call(
        body, name="ssm_conv_fwd", grid=(M // tm, C // tc),
        in_specs=[pl.BlockSpec((tm, tc), lambda i, j: (i, j)),
                  _prev_spec(tm, tc, lambda i, j: j, 0),
                  pl.BlockSpec((SUBLANES, tc), lambda i, j: (0, j)),
                  pl.BlockSpec((1, tc), lambda i, j: (0, j))],
        out_specs=pl.BlockSpec((tm, tc), lambda i, j: (i, j)),
        out_shape=jax.ShapeDtypeStruct((M, C), f32),
        scratch_shapes=[pltpu.VMEM((tm + SUBLANES, tc), f32)],
        compiler_params=_cparams(("parallel", "parallel")),
    )(xbc, xbc, w, b)


def _ssm_conv_bwd(xbc, w, b, dout):
    M, C = xbc.shape
    tm, tc, kw = ROW_TILE, C // 3, SSM_CONV
    nt = M // tm

    def body(x_ref, h_ref, w_ref, b_ref, d_ref, dx_ref, dw_ref, db_ref, ext_ref, edu_ref):
        i = pl.program_id(1)

        @pl.when(i == 0)
        def _():
            dw_ref[...] = jnp.zeros_like(dw_ref)
            db_ref[...] = jnp.zeros_like(db_ref)

        _fill_prev(ext_ref, x_ref, h_ref, nt - 1 - i, tm)
        first = SUBLANES - (kw - 1)
        u = _conv_taps(ext_ref, w_ref, kw, tm, first) + b_ref[...]
        du = d_ref[...] * _dsilu(u)
        db_ref[...] += jnp.sum(du, axis=0, keepdims=True)
        for k in range(kw):
            dw_ref[k:k + 1, :] += jnp.sum(du * ext_ref[pl.ds(first + k, tm), :], axis=0, keepdims=True)
        dx_ref[...] = _conv_dx_carry(edu_ref, du, w_ref, kw, tm, i == 0).astype(bf16)

    tile = pl.BlockSpec((tm, tc), lambda j, i: (nt - 1 - i, j))
    return pl.pallas_call(
        body, name="ssm_conv_bwd", grid=(C // tc, nt),
        in_specs=[tile, _prev_spec(tm, tc, lambda j, i: j, 1, nt),
                  pl.BlockSpec((SUBLANES, tc), lambda j, i: (0, j)),
                  pl.BlockSpec((1, tc), lambda j, i: (0, j)), tile],
        out_specs=[tile, pl.BlockSpec((SUBLANES, tc), lambda j, i: (0, j)),
                   pl.BlockSpec((1, tc), lambda j, i: (0, j))],
        out_shape=[jax.ShapeDtypeStruct((M, C), bf16), jax.ShapeDtypeStruct((SUBLANES, C), f32),
                   jax.ShapeDtypeStruct((1, C), f32)],
        scratch_shapes=[pltpu.VMEM((tm + SUBLANES, tc), f32), pltpu.VMEM((tm + SUBLANES, tc), f32)],
        compiler_params=_cparams(("parallel", "arbitrary")),
    )(xbc, xbc, w, b, dout)


def _ffn_gate_fwd(up, w, b):
    M = up.shape[0]
    tm, tc, kw = ROW_TILE, D_FF // 2, FFN_CONV
    nc = D_FF // tc

    def body(xg_ref, hg_ref, xv_ref, hv_ref, wg_ref, wv_ref, bg_ref, bv_ref, o_ref, eg_ref, ev_ref):
        i = pl.program_id(0)
        first = SUBLANES - (kw - 1)
        _fill_prev(eg_ref, xg_ref, hg_ref, i, tm)
        _fill_prev(ev_ref, xv_ref, hv_ref, i, tm)
        ug = _conv_taps(eg_ref, wg_ref, kw, tm, first) + bg_ref[...]
        uv = _conv_taps(ev_ref, wv_ref, kw, tm, first) + bv_ref[...]
        o_ref[...] = (_silu(ug) * uv).astype(bf16)

    return pl.pallas_call(
        body, name="ffn_gate_fwd", grid=(M // tm, nc),
        in_specs=[pl.BlockSpec((tm, tc), lambda i, j: (i, j)),
                  _prev_spec(tm, tc, lambda i, j: j, 0),
                  pl.BlockSpec((tm, tc), lambda i, j: (i, j + nc)),
                  _prev_spec(tm, tc, lambda i, j: j + nc, 0),
                  pl.BlockSpec((SUBLANES, tc), lambda i, j: (0, j)),
                  pl.BlockSpec((SUBLANES, tc), lambda i, j: (0, j + nc)),
                  pl.BlockSpec((1, tc), lambda i, j: (0, j)),
                  pl.BlockSpec((1, tc), lambda i, j: (0, j + nc))],
        out_specs=pl.BlockSpec((tm, tc), lambda i, j: (i, j)),
        out_shape=jax.ShapeDtypeStruct((M, D_FF), bf16),
        scratch_shapes=[pltpu.VMEM((tm + SUBLANES, tc), f32), pltpu.VMEM((tm + SUBLANES, tc), f32)],
        compiler_params=_cparams(("parallel", "parallel")),
    )(up, up, up, up, w, w, b, b)


def _ffn_gate_bwd(up, w, b, d_act):
    M = up.shape[0]
    tm, tc, kw = ROW_TILE, D_FF // 2, FFN_CONV
    nc = D_FF // tc
    nt = M // tm

    def body(xg_ref, hg_ref, xv_ref, hv_ref, wg_ref, wv_ref, bg_ref, bv_ref, d_ref,
             dxg_ref, dxv_ref, dwg_ref, dwv_ref, dbg_ref, dbv_ref, eg_ref, ev_ref, edg_ref, edv_ref):
        i = pl.program_id(1)

        @pl.when(i == 0)
        def _():
            for r in (dwg_ref, dwv_ref, dbg_ref, dbv_ref):
                r[...] = jnp.zeros_like(r)

        first = SUBLANES - (kw - 1)
        _fill_prev(eg_ref, xg_ref, hg_ref, nt - 1 - i, tm)
        _fill_prev(ev_ref, xv_ref, hv_ref, nt - 1 - i, tm)
        ug = _conv_taps(eg_ref, wg_ref, kw, tm, first) + bg_ref[...]
        uv = _conv_taps(ev_ref, wv_ref, kw, tm, first) + bv_ref[...]
        da = d_ref[...]
        dug = da * uv * _dsilu(ug)
        duv = da * _silu(ug)
        dbg_ref[...] += jnp.sum(dug, axis=0, keepdims=True)
        dbv_ref[...] += jnp.sum(duv, axis=0, keepdims=True)
        for k in range(kw):
            dwg_ref[k:k + 1, :] += jnp.sum(dug * eg_ref[pl.ds(first + k, tm), :], axis=0, keepdims=True)
            dwv_ref[k:k + 1, :] += jnp.sum(duv * ev_ref[pl.ds(first + k, tm), :], axis=0, keepdims=True)
        dxg_ref[...] = _conv_dx_carry(edg_ref, dug, wg_ref, kw, tm, i == 0).astype(bf16)
        dxv_ref[...] = _conv_dx_carry(edv_ref, duv, wv_ref, kw, tm, i == 0).astype(bf16)

    tile_g = pl.BlockSpec((tm, tc), lambda j, i: (nt - 1 - i, j))
    tile_v = pl.BlockSpec((tm, tc), lambda j, i: (nt - 1 - i, j + nc))
    ext = pltpu.VMEM((tm + SUBLANES, tc), f32)
    return pl.pallas_call(
        body, name="ffn_gate_bwd", grid=(nc, nt),
        in_specs=[tile_g, _prev_spec(tm, tc, lambda j, i: j, 1, nt),
                  tile_v, _prev_spec(tm, tc, lambda j, i: j + nc, 1, nt),
                  pl.BlockSpec((SUBLANES, tc), lambda j, i: (0, j)),
                  pl.BlockSpec((SUBLANES, tc), lambda j, i: (0, j + nc)),
                  pl.BlockSpec((1, tc), lambda j, i: (0, j)),
                  pl.BlockSpec((1, tc), lambda j, i: (0, j + nc)),
                  tile_g],
        out_specs=[tile_g, tile_g,
                   pl.BlockSpec((SUBLANES, tc), lambda j, i: (0, j)),
                   pl.BlockSpec((SUBLANES, tc), lambda j, i: (0, j)),
                   pl.BlockSpec((1, tc), lambda j, i: (0, j)),
                   pl.BlockSpec((1, tc), lambda j, i: (0, j))],
        out_shape=[jax.ShapeDtypeStruct((M, D_FF), bf16), jax.ShapeDtypeStruct((M, D_FF), bf16),
                   jax.ShapeDtypeStruct((SUBLANES, D_FF), f32), jax.ShapeDtypeStruct((SUBLANES, D_FF), f32),
                   jax.ShapeDtypeStruct((1, D_FF), f32), jax.ShapeDtypeStruct((1, D_FF), f32)],
        scratch_shapes=[ext, ext, ext, ext],
        compiler_params=_cparams(("parallel", "arbitrary")),
    )(up, up, up, up, w, w, b, b, d_act)


def _rope_apply(blk, cos, sin):
    lane = lax.broadcasted_iota(jnp.int32, blk.shape, 1)
    half = QK_ROPE // 2
    partner = jnp.where(lane < half, pltpu.roll(blk, LANES - half, 1), pltpu.roll(blk, half, 1))
    return blk * cos + partner * sin


def _rope_unapply(d, cos, sin):
    t = d * sin
    lane = lax.broadcasted_iota(jnp.int32, d.shape, 1)
    half = QK_ROPE // 2
    partner = jnp.where(lane < half, pltpu.roll(t, LANES - half, 1), pltpu.roll(t, half, 1))
    return d * cos + partner


def _up_q_rope(qn, wuq, cos, sin):
    M, K = qn.shape
    tm = _pick(M, (768, 512, 256))

    def body(a_ref, b_ref, c_ref, s_ref, o_ref):
        r = _dot(a_ref[...], b_ref[...]) * Q_PRESCALE
        o_ref[0, :, 0:QK_NOPE] = r[:, 0:QK_NOPE].astype(bf16)
        o_ref[0, :, QK_NOPE:QK_PAD] = _rope_apply(r[:, QK_NOPE:QK_PAD], c_ref[...], s_ref[...]).astype(bf16)

    return pl.pallas_call(
        body, name="up_q_rope", grid=(M // tm, MLA_HEADS),
        in_specs=[pl.BlockSpec((tm, K), lambda i, h: (i, 0)),
                  pl.BlockSpec((K, QK_PAD), lambda i, h: (0, h)),
                  pl.BlockSpec((tm, LANES), lambda i, h: (i, 0)),
                  pl.BlockSpec((tm, LANES), lambda i, h: (i, 0))],
        out_specs=pl.BlockSpec((1, tm, QK_PAD), lambda i, h: (h, i, 0)),
        out_shape=jax.ShapeDtypeStruct((MLA_HEADS, M, QK_PAD), bf16),
        compiler_params=_cparams(("parallel", "parallel")),
    )(qn, wuq, cos, sin)


def _up_kv_rope(kvn, wukv, kpe_raw, cos, sin):
    M, K = kvn.shape
    tm = _pick(M, (768, 512, 256))

    def body(a_ref, b_ref, pe_ref, c_ref, s_ref, k_ref, v_ref):
        r = _dot(a_ref[...], b_ref[...])
        k_ref[0, :, 0:QK_NOPE] = r[:, 0:QK_NOPE].astype(bf16)
        k_ref[0, :, QK_NOPE:QK_PAD] = _rope_apply(pe_ref[...], c_ref[...], s_ref[...]).astype(bf16)
        v_ref[0] = r[:, QK_NOPE:QK_NOPE + V_DIM].astype(bf16)

    return pl.pallas_call(
        body, name="up_kv_rope", grid=(M // tm, MLA_HEADS),
        in_specs=[pl.BlockSpec((tm, K), lambda i, h: (i, 0)),
                  pl.BlockSpec((K, QK_NOPE + V_DIM), lambda i, h: (0, h)),
                  pl.BlockSpec((tm, LANES), lambda i, h: (i, 0)),
                  pl.BlockSpec((tm, LANES), lambda i, h: (i, 0)),
                  pl.BlockSpec((tm, LANES), lambda i, h: (i, 0))],
        out_specs=[pl.BlockSpec((1, tm, QK_PAD), lambda i, h: (h, i, 0)),
                   pl.BlockSpec((1, tm, V_DIM), lambda i, h: (h, i, 0))],
        out_shape=[jax.ShapeDtypeStruct((MLA_HEADS, M, QK_PAD), bf16),
                   jax.ShapeDtypeStruct((MLA_HEADS, M, V_DIM), bf16)],
        compiler_params=_cparams(("parallel", "parallel")),
    )(kvn, wukv, kpe_raw, cos, sin)


def _rope_q_bwd(dq, cos, sin):
    M = dq.shape[1]
    tm = _rt(M)

    def body(d_ref, c_ref, s_ref, o_ref):
        c, s = c_ref[...], s_ref[...]
        for h in range(MLA_HEADS):
            o_ref[:, h * QK_PAD:h * QK_PAD + QK_NOPE] = (d_ref[h, :, 0:QK_NOPE] * SOFTMAX_SCALE).astype(bf16)
            o_ref[:, h * QK_PAD + QK_NOPE:(h + 1) * QK_PAD] = (_rope_unapply(
                d_ref[h, :, QK_NOPE:QK_PAD], c, s) * SOFTMAX_SCALE).astype(bf16)

    return pl.pallas_call(
        body, name="rope_q_bwd", grid=(M // tm,),
        in_specs=[pl.BlockSpec((MLA_HEADS, tm, QK_PAD), lambda i: (0, i, 0)),
                  _row(LANES, tm), _row(LANES, tm)],
        out_specs=_row(MLA_HEADS * QK_PAD, tm),
        out_shape=jax.ShapeDtypeStruct((M, MLA_HEADS * QK_PAD), bf16),
        compiler_params=_cparams(("parallel",)),
    )(dq, cos, sin)


def _rope_k_bwd(dk, dv, cos, sin):
    M = dk.shape[1]
    tm = _rt(M)
    w = QK_NOPE + V_DIM

    def body(dk_ref, dv_ref, c_ref, s_ref, o_ref, pe_ref):
        pe = None
        for h in range(MLA_HEADS):
            o_ref[:, h * w:h * w + QK_NOPE] = dk_ref[h, :, 0:QK_NOPE].astype(bf16)
            o_ref[:, h * w + QK_NOPE:(h + 1) * w] = dv_ref[h].astype(bf16)
            t = dk_ref[h, :, QK_NOPE:QK_PAD]
            pe = t if pe is None else pe + t
        pe_ref[...] = _rope_unapply(pe, c_ref[...], s_ref[...])

    return pl.pallas_call(
        body, name="rope_k_bwd", grid=(M // tm,),
        in_specs=[pl.BlockSpec((MLA_HEADS, tm, QK_PAD), lambda i: (0, i, 0)),
                  pl.BlockSpec((MLA_HEADS, tm, V_DIM), lambda i: (0, i, 0)),
                  _row(LANES, tm), _row(LANES, tm)],
        out_specs=[_row(MLA_HEADS * w, tm), _row(LANES, tm)],
        out_shape=[jax.ShapeDtypeStruct((M, MLA_HEADS * w), bf16), jax.ShapeDtypeStruct((M, LANES), f32)],
        compiler_params=_cparams(("parallel",)),
    )(dk, dv, cos, sin)


def _attn_tile(M):
    return 768 if (M % 768 == 0 and M >= 4 * 768) else ROW_TILE


def _col_to_row(col):
    return col.T[0:1, :]


def _flash_fwd(q, k, v):
    H, M, _ = q.shape
    T = _attn_tile(M)
    nq = M // T

    def body(q_ref, k_ref, v_ref, o_ref, lse_ref, sa_ref, sb_ref, m_sc, l_sc, acc_sc):
        i = pl.program_id(1)
        qv = q_ref[0]
        m_sc[...] = jnp.full_like(m_sc, NEG)
        l_sc[...] = jnp.zeros_like(l_sc)
        acc_sc[...] = jnp.zeros_like(acc_sc)

        def scores(j, s_ref):
            off = pl.multiple_of(j * T, T)
            s_ref[...] = _dot_nt(qv, k_ref[0, pl.ds(off, T), :])

        def softmax_pv(j, s_ref, masked):
            off = pl.multiple_of(j * T, T)
            s = s_ref[...]
            if masked:
                r = lax.broadcasted_iota(jnp.int32, (T, T), 0)
                c = lax.broadcasted_iota(jnp.int32, (T, T), 1)
                s = jnp.where(r >= c, s, NEG)
            m_prev = m_sc[...]
            m_new = jnp.maximum(m_prev, jnp.max(s, axis=1, keepdims=True))
            alpha = jnp.exp2(m_prev - m_new)
            p = jnp.exp2(s - m_new[:, 0:1])
            l_sc[...] = alpha * l_sc[...] + jnp.sum(p, axis=1, keepdims=True)
            acc_sc[...] = alpha * acc_sc[...] + _dot(p.astype(bf16), v_ref[0, pl.ds(off, T), :])
            m_sc[...] = m_new

        scores(0, sa_ref)

        def pair(jj, c):
            j0 = 2 * jj
            scores(j0 + 1, sb_ref)
            softmax_pv(j0, sa_ref, False)
            scores(j0 + 2, sa_ref)
            softmax_pv(j0 + 1, sb_ref, False)
            return c

        lax.fori_loop(0, i // 2, pair, 0)

        @pl.when(i % 2 == 0)
        def _():
            softmax_pv(i, sa_ref, True)

        @pl.when(i % 2 == 1)
        def _():
            scores(i, sb_ref)
            softmax_pv(i - 1, sa_ref, False)
            softmax_pv(i, sb_ref, True)

        l = l_sc[...]
        o_ref[...] = acc_sc[...] / l
        lse_ref[0, 0] = _col_to_row(m_sc[...] + jnp.log2(l))

    return pl.pallas_call(
        body, name="flash_fwd", grid=(H, nq),
        in_specs=[pl.BlockSpec((1, T, QK_PAD), lambda h, i: (h, i, 0)),
                  pl.BlockSpec((1, M, QK_PAD), lambda h, i: (h, 0, 0)),
                  pl.BlockSpec((1, M, V_DIM), lambda h, i: (h, 0, 0))],
        out_specs=[pl.BlockSpec((T, V_DIM), lambda h, i: (i, h)),
                   pl.BlockSpec((1, 1, 1, T), lambda h, i: (h, i, 0, 0))],
        out_shape=[jax.ShapeDtypeStruct((M, H * V_DIM), f32),
                   jax.ShapeDtypeStruct((H, nq, 1, T), f32)],
        scratch_shapes=[pltpu.VMEM((T, T), f32), pltpu.VMEM((T, T), f32),
                        pltpu.VMEM((T, LANES), f32), pltpu.VMEM((T, LANES), f32), pltpu.VMEM((T, V_DIM), f32)],
        compiler_params=_cparams(("parallel", "parallel")),
    )(q, k, v)


def _attn_delta(o, do):
    M = o.shape[0]
    H = MLA_HEADS
    T = _attn_tile(M)

    def body(o_ref, d_ref, dh_ref, dl_ref):
        dv = d_ref[...]
        dh_ref[0] = dv.astype(bf16)
        col = jnp.sum(o_ref[...] * dv, axis=1, keepdims=True) + jnp.zeros((T, LANES), f32)
        dl_ref[0, 0] = _col_to_row(col)

    return pl.pallas_call(
        body, name="attn_delta", grid=(M // T, H),
        in_specs=[pl.BlockSpec((T, V_DIM), lambda i, h: (i, h)),
                  pl.BlockSpec((T, V_DIM), lambda i, h: (i, h))],
        out_specs=[pl.BlockSpec((1, T, V_DIM), lambda i, h: (h, i, 0)),
                   pl.BlockSpec((1, 1, 1, T), lambda i, h: (h, i, 0, 0))],
        out_shape=[jax.ShapeDtypeStruct((H, M, V_DIM), bf16),
                   jax.ShapeDtypeStruct((H, M // T, 1, T), f32)],
        compiler_params=_cparams(("parallel", "parallel")),
    )(o, do)


def _flash_bwd(q, k, v, do, lse, delta):
    H, M, _ = q.shape
    T = _attn_tile(M)
    nq = M // T

    def body(q_ref, do_ref, lse_ref, dl_ref, k_ref, v_ref, dq_ref, dk_ref, dv_ref, dk_sc, dv_sc):
        j = pl.program_id(1)

        @pl.when(j == 0)
        def _():
            dq_ref[...] = jnp.zeros_like(dq_ref)

        kt = k_ref[0]
        vt = v_ref[0]
        dk_sc[...] = jnp.zeros_like(dk_sc)
        dv_sc[...] = jnp.zeros_like(dv_sc)

        def step(i, masked):
            off = pl.multiple_of(i * T, T)
            qt = q_ref[0, pl.ds(off, T), :]
            dot_ = do_ref[0, pl.ds(off, T), :]
            st = _dot_nt(kt, qt)
            if masked:
                r = lax.broadcasted_iota(jnp.int32, (T, T), 0)
                c = lax.broadcasted_iota(jnp.int32, (T, T), 1)
                st = jnp.where(c >= r, st, NEG)
            pt = jnp.exp2(st - lse_ref[0, i])
            dv_sc[...] += _dot(pt.astype(bf16), dot_)
            dpt = _dot_nt(vt, dot_)
            dst = (pt * (dpt - dl_ref[0, i])).astype(bf16)
            dk_sc[...] += _dot(dst, qt)
            dq_ref[0, pl.ds(off, T), :] += _dot_tn(dst, kt)

        step(j, True)

        def loop_body(i, c):
            step(i, False)
            return c

        lax.fori_loop(j + 1, nq, loop_body, 0)
        dk_ref[0] = dk_sc[...] * LN2
        dv_ref[0] = dv_sc[...]

    return pl.pallas_call(
        body, name="flash_bwd", grid=(H, nq),
        in_specs=[pl.BlockSpec((1, M, QK_PAD), lambda h, j: (h, 0, 0)),
                  pl.BlockSpec((1, M, V_DIM), lambda h, j: (h, 0, 0)),
                  pl.BlockSpec((1, nq, 1, T), lambda h, j: (h, 0, 0, 0)),
                  pl.BlockSpec((1, nq, 1, T), lambda h, j: (h, 0, 0, 0)),
                  pl.BlockSpec((1, T, QK_PAD), lambda h, j: (h, j, 0)),
                  pl.BlockSpec((1, T, V_DIM), lambda h, j: (h, j, 0))],
        out_specs=[pl.BlockSpec((1, M, QK_PAD), lambda h, j: (h, 0, 0)),
                   pl.BlockSpec((1, T, QK_PAD), lambda h, j: (h, j, 0)),
                   pl.BlockSpec((1, T, V_DIM), lambda h, j: (h, j, 0))],
        out_shape=[jax.ShapeDtypeStruct((H, M, QK_PAD), f32),
                   jax.ShapeDtypeStruct((H, M, QK_PAD), f32),
                   jax.ShapeDtypeStruct((H, M, V_DIM), f32)],
        scratch_shapes=[pltpu.VMEM((T, QK_PAD), f32), pltpu.VMEM((T, V_DIM), f32)],
        compiler_params=_cparams(("parallel", "arbitrary")),
    )(q, do, lse, delta, k, v)


def _dt_fwd(dt_raw, bias, expand):
    M = dt_raw.shape[0]
    tm = _rt(M)

    def body(x_ref, b_ref, e_ref, o_ref, oe_ref):
        u = x_ref[...] + b_ref[...]
        sp = jnp.maximum(u, 0.0) + jnp.log(1.0 + jnp.exp(-jnp.abs(u)))
        lane = lax.broadcasted_iota(jnp.int32, u.shape, 1)
        dtp = jnp.where(lane < SSM_HEADS, sp, 0.0)
        o_ref[...] = dtp
        oe_ref[...] = _dot_hi(dtp, e_ref[...])

    return pl.pallas_call(
        body, name="dt_fwd", grid=(M // tm,),
        in_specs=[_row(LANES, tm), _full((1, LANES)), _full((LANES, D_SSM))],
        out_specs=[_row(LANES, tm), _row(D_SSM, tm)],
        out_shape=[jax.ShapeDtypeStruct((M, LANES), f32), jax.ShapeDtypeStruct((M, D_SSM), f32)],
        compiler_params=_cparams(("parallel",)),
    )(dt_raw, bias, expand)


def _dt_bwd(dt_raw, bias, ddt):
    M = dt_raw.shape[0]
    tm = _rt(M)

    def body(x_ref, b_ref, d_ref, o_ref, db_ref):
        @pl.when(pl.program_id(0) == 0)
        def _():
            db_ref[...] = jnp.zeros_like(db_ref)

        u = x_ref[...] + b_ref[...]
        lane = lax.broadcasted_iota(jnp.int32, u.shape, 1)
        g = jnp.where(lane < SSM_HEADS, d_ref[...] * _sigmoid(u), 0.0)
        o_ref[...] = g
        db_ref[...] += jnp.sum(g, axis=0, keepdims=True)

    return pl.pallas_call(
        body, name="dt_bwd", grid=(M // tm,),
        in_specs=[_row(LANES, tm), _full((1, LANES)), _row(LANES, tm)],
        out_specs=[_row(LANES, tm), _full((1, LANES))],
        out_shape=[jax.ShapeDtypeStruct((M, LANES), f32), jax.ShapeDtypeStruct((1, LANES), f32)],
        compiler_params=_cparams(("arbitrary",)),
    )(dt_raw, bias, ddt)


SSM_GW = SSM_HPG * SSM_P
SSM_PAIRS = SSM_GW // LANES


def _ssd_common(dte_ref, dtt_ref, ae_ref, acol_ref):
    Q = CHUNK
    r = lax.broadcasted_iota(jnp.int32, (Q, Q), 0)
    c = lax.broadcasted_iota(jnp.int32, (Q, Q), 1)
    causal = r >= c
    anti = c >= r
    tril = causal.astype(f32)
    triu = anti.astype(f32)
    dt_e = dte_ref[...]
    cs_e = _dot_hi(tril, dt_e * ae_ref[...])
    cst = _dot_hi(dtt_ref[...] * acol_ref[...], triu)
    cs_last = cs_e[Q - 1:Q, :]
    return causal, anti, triu, dt_e, cs_e, cst, jnp.exp(cs_e), jnp.exp(cs_last - cs_e), jnp.exp(cs_last)


def _half_masks():
    lane = lax.broadcasted_iota(jnp.int32, (CHUNK, LANES), 1)
    lo = lane < SSM_P
    return lo, jnp.logical_not(lo)


def _ssd_fwd(xbc_c, dt_e, dtt, a_e, a_col):
    M = xbc_c.shape[0]
    Q = CHUNK
    nch = M // Q

    def body(x_ref, dte_ref, dtt_ref, ae_ref, acol_ref, y_ref, hin_ref, ht_sc):
        @pl.when(pl.program_id(0) == 0)
        def _():
            ht_sc[...] = jnp.zeros_like(ht_sc)

        causal, _, _, dt_e, cs_e, cst, ecs_e, dte_e, elast_e = _ssd_common(dte_ref, dtt_ref, ae_ref, acol_ref)
        halves = _half_masks()
        for g in range(SSM_GROUPS):
            g0 = g * SSM_GW
            bg = x_ref[:, D_SSM + g * SSM_N:D_SSM + (g + 1) * SSM_N]
            cg = x_ref[:, D_SSM + D_BC + g * SSM_N:D_SSM + D_BC + (g + 1) * SSM_N]
            bg_b = bg.astype(bf16)
            cg_b = cg.astype(bf16)
            cb = _dot_nt(cg_b, bg_b)
            bgt_b = bg.T.astype(bf16)
            xdt_g = x_ref[:, g0:g0 + SSM_GW] * dt_e[:, g0:g0 + SSM_GW]
            ht = ht_sc[g]
            hin_ref[0, g] = ht
            y_off = _dot(cg_b, ht.astype(bf16)) * ecs_e[:, g0:g0 + SSM_GW]
            for pr in range(SSM_PAIRS):
                p0 = pr * LANES
                xdt_p = xdt_g[:, p0:p0 + LANES]
                acc = y_off[:, p0:p0 + LANES]
                for half in range(2):
                    h = g * SSM_HPG + pr * 2 + half
                    seg = cs_e[:, h * SSM_P:h * SSM_P + 1] - cst[h:h + 1, :]
                    lm = jnp.exp(jnp.where(causal, seg, -jnp.inf))
                    xm = jnp.where(halves[half], xdt_p, 0.0).astype(bf16)
                    acc = acc + _dot((cb * lm).astype(bf16), xm)
                y_ref[:, g0 + p0:g0 + p0 + LANES] = acc
            st = _dot(bgt_b, (xdt_g * dte_e[:, g0:g0 + SSM_GW]).astype(bf16))
            ht_sc[g] = ht * elast_e[:, g0:g0 + SSM_GW] + st

    return pl.pallas_call(
        body, name="ssd_fwd", grid=(nch,),
        in_specs=[pl.BlockSpec((Q, D_XBC), lambda c: (c, 0)),
                  pl.BlockSpec((Q, D_SSM), lambda c: (c, 0)),
                  pl.BlockSpec((SSM_HEADS, Q), lambda c: (0, c)),
                  _full((1, D_SSM)), _full((SSM_HEADS, LANES))],
        out_specs=[pl.BlockSpec((Q, D_SSM), lambda c: (c, 0)),
                   pl.BlockSpec((1, SSM_GROUPS, SSM_N, SSM_GW), lambda c: (c, 0, 0, 0))],
        out_shape=[jax.ShapeDtypeStruct((M, D_SSM), f32),
                   jax.ShapeDtypeStruct((nch, SSM_GROUPS, SSM_N, SSM_GW), f32)],
        scratch_shapes=[pltpu.VMEM((SSM_GROUPS, SSM_N, SSM_GW), f32)],
        compiler_params=_cparams(("arbitrary",)),
    )(xbc_c, dt_e, dtt, a_e, a_col)


def _ssd_bwd(xbc_c, dtp, dt_e, dtt, a_row, a_e, a_col, hin, dy, d_exp, head_ind):
    M = xbc_c.shape[0]
    Q = CHUNK
    nch = M // Q
    rev = lambda c: nch - 1 - c

    def body(x_ref, dtp_ref, dte_ref, dtt_ref, arow_ref, ae_ref, acol_ref, hin_ref, dy_ref, dexp_ref,
             ind_ref, dx_ref, ddt_ref, da_ref, dht_sc, z_sc, z1_sc, last_sc, ct_sc):
        @pl.when(pl.program_id(0) == 0)
        def _():
            dht_sc[...] = jnp.zeros_like(dht_sc)
            da_ref[...] = jnp.zeros_like(da_ref)
            last_sc[...] = jnp.zeros_like(last_sc)
            ct_sc[...] = jnp.zeros_like(ct_sc)

        causal, anti, triu, dt_e, cs_e, cst, ecs_e, dte_e, elast_e = _ssd_common(dte_ref, dtt_ref, ae_ref, acol_ref)
        halves = _half_masks()
        lane = lax.broadcasted_iota(jnp.int32, (Q, LANES), 1)
        rsum = jnp.zeros((Q, LANES), f32)
        for g in range(SSM_GROUPS):
            g0 = g * SSM_GW
            gs = slice(g0, g0 + SSM_GW)
            b0 = D_SSM + g * SSM_N
            c0 = D_SSM + D_BC + g * SSM_N
            bg = x_ref[:, b0:b0 + SSM_N]
            cg = x_ref[:, c0:c0 + SSM_N]
            bg_b = bg.astype(bf16)
            cg_b = cg.astype(bf16)
            cgt_b = cg.T.astype(bf16)
            cbt = _dot_nt(bg_b, cg_b)
            cb = _dot_nt(cg_b, bg_b)
            x_g = x_ref[:, gs]
            dt_g = dt_e[:, gs]
            xdt_g = x_g * dt_g
            dy_g = dy_ref[:, gs]
            ht = hin_ref[0, g]
            ht_b = ht.astype(bf16)
            dht = dht_sc[g]
            dht_b = dht.astype(bf16)
            dye_b = (dy_g * ecs_e[:, gs]).astype(bf16)
            dc = _dot_nt(dye_b, ht_b)
            dht_new = dht * elast_e[:, gs] + _dot(cgt_b, dye_b)
            e = _dot(bg_b, dht_b)
            xdtd = xdt_g * dte_e[:, gs]
            db = _dot_nt(xdtd.astype(bf16), dht_b)
            dxdt_state = e * dte_e[:, gs]
            exd = e * xdtd
            z1_sc[:, gs] = dy_g * (_dot(cg_b, ht_b) * ecs_e[:, gs]) - exd
            last_sc[0:1, gs] = (jnp.sum(exd, axis=0, keepdims=True)
                                + jnp.sum(dht * ht, axis=0, keepdims=True) * elast_e[:, gs])
            dg_acc = jnp.zeros((Q, Q), f32)
            for pr in range(SSM_PAIRS):
                p0 = pr * LANES
                ps = slice(g0 + p0, g0 + p0 + LANES)
                dy_p = dy_g[:, p0:p0 + LANES]
                xdt_pb = xdt_g[:, p0:p0 + LANES].astype(bf16)
                acc = dxdt_state[:, p0:p0 + LANES]
                for half in range(2):
                    h = g * SSM_HPG + pr * 2 + half
                    seg = cs_e[:, h * SSM_P:h * SSM_P + 1] - cst[h:h + 1, :]
                    lm = jnp.exp(jnp.where(causal, seg, -jnp.inf))
                    lmt = jnp.exp(jnp.where(anti, -seg, -jnp.inf))
                    dym = jnp.where(halves[half], dy_p, 0.0).astype(bf16)
                    acc = acc + _dot((cbt * lmt).astype(bf16), dym)
                    dml = _dot_nt(dym, xdt_pb) * lm
                    dg_acc = dg_acc + dml
                    w = dml * cb
                    rsum = rsum + jnp.where(lane == h, jnp.sum(w, axis=1, keepdims=True), 0.0)
                    ct_sc[h:h + 1, :] = jnp.sum(w, axis=0, keepdims=True)
                dx_ref[:, ps] = acc * dt_g[:, p0:p0 + LANES] + dexp_ref[:, ps] * dy_p
                z_sc[:, ps] = acc * x_g[:, p0:p0 + LANES]
            dg_b = dg_acc.astype(bf16)
            dx_ref[:, c0:c0 + SSM_N] = dc + _dot(dg_b, bg_b)
            dx_ref[:, b0:b0 + SSM_N] = db + _dot_tn(dg_b, cg_b)
            dht_sc[g] = dht_new
        s1 = _dot_hi(z1_sc[...], ind_ref[...])
        s2 = _dot_hi(z_sc[...], ind_ref[...])
        last = _dot_hi(last_sc[...], ind_ref[...])[0:1, :]
        dtp = dtp_ref[...]
        row = lax.broadcasted_iota(jnp.int32, (Q, LANES), 0)
        dcs = s1 + rsum + jnp.where(row == Q - 1, last, 0.0)
        tril = causal.astype(f32)
        da = _dot_hi(triu, dcs) - _dot_hi(ct_sc[...], tril).T
        ddt_ref[...] = s2 + da * arow_ref[...]
        da_ref[...] += jnp.sum(da * dtp, axis=0, keepdims=True)

    return pl.pallas_call(
        body, name="ssd_bwd", grid=(nch,),
        in_specs=[pl.BlockSpec((Q, D_XBC), lambda c: (rev(c), 0)),
                  pl.BlockSpec((Q, LANES), lambda c: (rev(c), 0)),
                  pl.BlockSpec((Q, D_SSM), lambda c: (rev(c), 0)),
                  pl.BlockSpec((SSM_HEADS, Q), lambda c: (0, rev(c))),
                  _full((1, LANES)), _full((1, D_SSM)), _full((SSM_HEADS, LANES)),
                  pl.BlockSpec((1, SSM_GROUPS, SSM_N, SSM_GW), lambda c: (rev(c), 0, 0, 0)),
                  pl.BlockSpec((Q, D_SSM), lambda c: (rev(c), 0)),
                  _full((1, D_SSM)), _full((D_SSM, LANES))],
        out_specs=[pl.BlockSpec((Q, D_XBC), lambda c: (rev(c), 0)),
                   pl.BlockSpec((Q, LANES), lambda c: (rev(c), 0)),
                   _full((1, LANES))],
        out_shape=[jax.ShapeDtypeStruct((M, D_XBC), f32), jax.ShapeDtypeStruct((M, LANES), f32),
                   jax.ShapeDtypeStruct((1, LANES), f32)],
        scratch_shapes=[pltpu.VMEM((SSM_GROUPS, SSM_N, SSM_GW), f32), pltpu.VMEM((Q, D_SSM), f32),
                        pltpu.VMEM((Q, D_SSM), f32), pltpu.VMEM((SUBLANES, D_SSM), f32),
                        pltpu.VMEM((LANES, Q), f32)],
        compiler_params=_cparams(("arbitrary",)),
    )(xbc_c, dtp, dt_e, dtt, a_row, a_e, a_col, hin, dy, d_exp, head_ind)


def _gate_norm_fwd(y, xbc_c, z, d_exp, g):
    M = y.shape[0]
    tm = _rt(M)
    gw = D_SSM // SSM_GROUPS

    def body(y_ref, x_ref, z_ref, d_ref, g_ref, o_ref):
        yg = (y_ref[...] + d_ref[...] * x_ref[...]) * _silu(z_ref[...])
        for gi in range(SSM_GROUPS):
            blk = yg[:, gi * gw:(gi + 1) * gw]
            o_ref[:, gi * gw:(gi + 1) * gw] = (blk * _rstd(blk) * g_ref[:, gi * gw:(gi + 1) * gw]).astype(bf16)

    return pl.pallas_call(
        body, name="gate_norm_fwd", grid=(M // tm,),
        in_specs=[_row(D_SSM, tm), _row(D_SSM, tm), _row(D_SSM, tm), _full((1, D_SSM)), _full((1, D_SSM))],
        out_specs=_row(D_SSM, tm), out_shape=jax.ShapeDtypeStruct((M, D_SSM), bf16),
        compiler_params=_cparams(("parallel",)),
    )(y, xbc_c, z, d_exp, g)


def _gate_norm_bwd(y, xbc_c, z, d_exp, g, dout, head_ind):
    M = y.shape[0]
    tm = _rt(M)
    nt = M // tm
    gw = D_SSM // SSM_GROUPS

    def body(y_ref, x_ref, z_ref, d_ref, g_ref, do_ref, ind_ref, dy_ref, dz_ref, dg_ref, dd_ref, ddc_sc):
        i = pl.program_id(0)

        @pl.when(i == 0)
        def _():
            dg_ref[...] = jnp.zeros_like(dg_ref)
            ddc_sc[...] = jnp.zeros_like(ddc_sc)

        zv = z_ref[...]
        xv = x_ref[...]
        s = _silu(zv)
        yd = y_ref[...] + d_ref[...] * xv
        yg = yd * s
        dov = do_ref[...]
        for gi in range(SSM_GROUPS):
            sl = slice(gi * gw, (gi + 1) * gw)
            dyg, dgp = _rms_bwd_math(yg[:, sl], g_ref[:, sl], dov[:, sl])
            dg_ref[:, sl] += jnp.sum(dgp, axis=0, keepdims=True)
            dyd = dyg * s[:, sl]
            dy_ref[:, sl] = dyd
            dz_ref[:, sl] = (dyg * yd[:, sl] * _dsilu(zv[:, sl])).astype(bf16)
            ddc_sc[:, sl] += jnp.sum(dyd * xv[:, sl], axis=0, keepdims=True)

        @pl.when(i == nt - 1)
        def _():
            dd_ref[...] = _dot_hi(ddc_sc[...], ind_ref[...])

    return pl.pallas_call(
        body, name="gate_norm_bwd", grid=(nt,),
        in_specs=[_row(D_SSM, tm), _row(D_SSM, tm), _row(D_SSM, tm), _full((1, D_SSM)), _full((1, D_SSM)),
                  _row(D_SSM, tm), _full((D_SSM, LANES))],
        out_specs=[_row(D_SSM, tm), _row(D_SSM, tm), _full((1, D_SSM)), _full((1, LANES))],
        out_shape=[jax.ShapeDtypeStruct((M, D_SSM), f32), jax.ShapeDtypeStruct((M, D_SSM), bf16),
                   jax.ShapeDtypeStruct((1, D_SSM), f32), jax.ShapeDtypeStruct((1, LANES), f32)],
        scratch_shapes=[pltpu.VMEM((1, D_SSM), f32)],
        compiler_params=_cparams(("arbitrary",)),
    )(y, xbc_c, z, d_exp, g, dout, head_ind)


_PEER_FLIPS = [(0, 0, 1), (0, 1, 0), (0, 1, 1), (1, 0, 0), (1, 0, 1), (1, 1, 0), (1, 1, 1)]


def _exchange(arrays, scatter, name):
    n = len(arrays)

    def body(*refs):
        ins, outs = refs[:n], refs[n:2 * n]
        send_sems, recv_sems, loc_sems = refs[2 * n:]
        x, y, c = lax.axis_index("x"), lax.axis_index("y"), lax.axis_index("c")
        me = 4 * x + 2 * y + c
        local = []
        for a in range(n):
            src = ins[a].at[me] if scatter else ins[a]
            lc = pltpu.make_async_copy(src, outs[a].at[me], loc_sems.at[a])
            lc.start()
            local.append(lc)
        remote = []
        for p, (fx, fy, fc) in enumerate(_PEER_FLIPS):
            tx = 1 - x if fx else x
            ty = 1 - y if fy else y
            tc = 1 - c if fc else c
            tgt = 4 * tx + 2 * ty + tc
            for a in range(n):
                src = ins[a].at[tgt] if scatter else ins[a]
                cp = pltpu.make_async_remote_copy(
                    src_ref=src, dst_ref=outs[a].at[me],
                    send_sem=send_sems.at[p * n + a], recv_sem=recv_sems.at[p * n + a],
                    device_id=(tx, ty, tc), device_id_type=_MESH)
                cp.start()
                remote.append(cp)
        for cp in remote:
            cp.wait()
        for lc in local:
            lc.wait()

    any_spec = pl.BlockSpec(memory_space=pl.ANY)
    out_shape = []
    for arr in arrays:
        shp = arr.shape if scatter else (N_DEV,) + arr.shape
        out_shape.append(jax.ShapeDtypeStruct(shp, arr.dtype))
    return pl.pallas_call(
        body, name=name, in_specs=[any_spec] * n, out_specs=[any_spec] * n, out_shape=out_shape,
        scratch_shapes=[pltpu.SemaphoreType.DMA((7 * n,)), pltpu.SemaphoreType.DMA((7 * n,)),
                        pltpu.SemaphoreType.DMA((n,))],
    )(*arrays)


def _adamw(parts, w, m, v, name):
    R, C = w.shape
    tr = _pick(R, (PACK_ROW_TILE, 16, 8))
    c1 = 1.0 - ADAM_B1 ** ADAM_STEP
    c2 = 1.0 - ADAM_B2 ** ADAM_STEP

    def body(p_ref, w_ref, m_ref, v_ref, g_ref, d_ref, nm_ref, nv_ref):
        g = p_ref[0].astype(f32)
        for s in range(1, N_DEV):
            g = g + p_ref[s].astype(f32)
        mn = ADAM_B1 * m_ref[...] + (1.0 - ADAM_B1) * g
        vn = ADAM_B2 * v_ref[...] + (1.0 - ADAM_B2) * (g * g)
        m_hat = mn / c1
        v_hat = vn / c2
        g_ref[...] = g
        d_ref[...] = -ADAM_LR * (m_hat / (jnp.sqrt(v_hat) + ADAM_EPS) + ADAM_WD * w_ref[...])
        nm_ref[...] = mn
        nv_ref[...] = vn

    spec = pl.BlockSpec((tr, C), lambda i: (i, 0))
    return pl.pallas_call(
        body, name=name, grid=(R // tr,),
        in_specs=[pl.BlockSpec((N_DEV, tr, C), lambda i: (0, i, 0)), spec, spec, spec],
        out_specs=[spec] * 4, out_shape=[jax.ShapeDtypeStruct((R, C), f32)] * 4,
        compiler_params=_cparams(("parallel",)),
    )(parts, w, m, v)


def _flat_rows(a, lead_ndim):
    lead = a.shape[:lead_ndim]
    n = int(np.prod(a.shape[lead_ndim:]))
    a = a.reshape(lead + (n,))
    pad = (-n) % PACK_W
    if pad:
        a = jnp.pad(a, [(0, 0)] * lead_ndim + [(0, pad)])
    return a.reshape(lead + ((n + pad) // PACK_W, PACK_W))


def _pack(arrays, lead_ndim, total_rows, dtype):
    rows = [_flat_rows(a.astype(dtype), lead_ndim) for a in arrays]
    cat = jnp.concatenate(rows, axis=lead_ndim)
    pad = total_rows - cat.shape[lead_ndim]
    if pad:
        cat = jnp.pad(cat, [(0, 0)] * lead_ndim + [(0, pad), (0, 0)])
    return cat


def _unpack(buf, shapes, lead_ndim):
    out = []
    r = 0
    lead = buf.shape[:lead_ndim]
    for shp in shapes:
        n = int(np.prod(shp))
        nr = -(-n // PACK_W)
        piece = lax.slice_in_dim(buf, r, r + nr, axis=lead_ndim)
        piece = piece.reshape(lead + (nr * PACK_W,))
        piece = lax.slice_in_dim(piece, 0, n, axis=lead_ndim)
        out.append(piece.reshape(lead + tuple(shp)))
        r += nr
    return out


def _round_up(n, m):
    return -(-n // m) * m


def kernel(x, meta_tokens, norm_mix_pre, norm_mix_post, norm_ffn_pre, norm_ffn_post, w_in, q_a_norm, w_uq, kv_a_norm, w_ukv, attn_out_norm, ssm_conv_w, ssm_conv_b, ssm_dt_bias, ssm_A_log, ssm_D, ssm_norm, w_out, w_up, ffn_conv_w, ffn_conv_b, w_down, loss_target, m_meta_tokens, m_norm_mix_pre, m_norm_mix_post, m_norm_ffn_pre, m_norm_ffn_post, m_w_in, m_q_a_norm, m_w_uq, m_kv_a_norm, m_w_ukv, m_attn_out_norm, m_ssm_conv_w, m_ssm_conv_b, m_ssm_dt_bias, m_ssm_A_log, m_ssm_D, m_ssm_norm, m_w_out, m_w_up, m_ffn_conv_w, m_ffn_conv_b, m_w_down, v_meta_tokens, v_norm_mix_pre, v_norm_mix_post, v_norm_ffn_pre, v_norm_ffn_post, v_w_in, v_q_a_norm, v_w_uq, v_kv_a_norm, v_w_ukv, v_attn_out_norm, v_ssm_conv_w, v_ssm_conv_b, v_ssm_dt_bias, v_ssm_A_log, v_ssm_D, v_ssm_norm, v_w_out, v_w_up, v_ffn_conv_w, v_ffn_conv_b, v_w_down):
    seq = x.shape[1]
    n_real = N_META + seq
    Lp = _round_up(n_real, 768) if n_real > 2048 else _round_up(n_real, ROW_TILE)
    D = D_MODEL

    sharded_w = [w_in, w_uq, w_ukv, w_out, w_up, w_down]
    sharded_s = [meta_tokens, ssm_conv_w, ffn_conv_w]
    sharded_names = sharded_w + sharded_s
    sharded_m = [m_w_in, m_w_uq, m_w_ukv, m_w_out, m_w_up, m_w_down, m_meta_tokens, m_ssm_conv_w, m_ffn_conv_w]
    sharded_v = [v_w_in, v_w_uq, v_w_ukv, v_w_out, v_w_up, v_w_down, v_meta_tokens, v_ssm_conv_w, v_ffn_conv_w]
    repl_w = [norm_mix_pre, norm_mix_post, norm_ffn_pre, norm_ffn_post, q_a_norm, kv_a_norm, attn_out_norm,
              ssm_conv_b, ssm_dt_bias, ssm_A_log, ssm_D, ssm_norm, ffn_conv_b]
    repl_m = [m_norm_mix_pre, m_norm_mix_post, m_norm_ffn_pre, m_norm_ffn_post, m_q_a_norm, m_kv_a_norm,
              m_attn_out_norm, m_ssm_conv_b, m_ssm_dt_bias, m_ssm_A_log, m_ssm_D, m_ssm_norm, m_ffn_conv_b]
    repl_v = [v_norm_mix_pre, v_norm_mix_post, v_norm_ffn_pre, v_norm_ffn_post, v_q_a_norm, v_kv_a_norm,
              v_attn_out_norm, v_ssm_conv_b, v_ssm_dt_bias, v_ssm_A_log, v_ssm_D, v_ssm_norm, v_ffn_conv_b]

    big_rows = _round_up(sum(-(-int(np.prod(a.shape)) // PACK_W) for a in sharded_w), PACK_ROW_TILE)
    small_rows = _round_up(sum(-(-int(np.prod(a.shape)) // PACK_W) for a in sharded_s), 16)
    wb = _pack(sharded_w, 0, big_rows, bf16)
    ws = _pack(sharded_s, 0, small_rows, f32)
    wb_all, ws_all = _exchange([wb, ws], False, "gather_weights")
    g_w_in, g_w_uq, g_w_ukv, g_w_out, g_w_up, g_w_down = _unpack(wb_all, [a.shape for a in sharded_w], 1)
    g_meta, g_sconv, g_fconv = _unpack(ws_all, [a.shape for a in sharded_s], 1)

    def cols(gathered):
        t = gathered[:, 0]
        return jnp.transpose(t, (1, 0, 2)).reshape(t.shape[1], N_DEV * t.shape[2])

    win = cols(g_w_in)
    o = np.cumsum((0, Q_RANK, KV_RANK, QK_ROPE, D_SSM, D_XBC, SSM_HEADS))
    w_q, w_kv = win[:, o[0]:o[1]], win[:, o[1]:o[2]]
    w_rope = jnp.pad(win[:, o[2]:o[3]], ((0, 0), (0, LANES - QK_ROPE)))
    w_z, w_xbc = win[:, o[3]:o[4]], win[:, o[4]:o[5]]
    w_dt = jnp.pad(win[:, o[5]:o[6]], ((0, 0), (0, LANES - SSM_HEADS)))
    wuq = g_w_uq.reshape(Q_RANK, MLA_HEADS, QK_NOPE + QK_ROPE)
    wuq = jnp.pad(wuq, ((0, 0), (0, 0), (0, QK_PAD - QK_NOPE - QK_ROPE))).reshape(Q_RANK, MLA_HEADS * QK_PAD)
    wukv = g_w_ukv.reshape(KV_RANK, MLA_HEADS * (QK_NOPE + V_DIM))
    wout = g_w_out.reshape(D_ATTN + D_SSM, D)
    wout_a, wout_s = wout[:D_ATTN], wout[D_ATTN:]
    wup = cols(g_w_up)
    wdown = g_w_down.reshape(D_FF, D)
    meta_full = jnp.transpose(g_meta, (1, 0, 2)).reshape(N_META, D)
    sconv_w = jnp.pad(cols(g_sconv), ((0, SUBLANES - SSM_CONV), (0, 0)))
    fconv_w = jnp.pad(cols(g_fconv), ((0, SUBLANES - FFN_CONV), (0, 0)))

    pos = jnp.arange(Lp, dtype=f32)
    inv = ROPE_THETA ** (-jnp.arange(0, QK_ROPE, 2, dtype=f32) / QK_ROPE)
    ang = pos[:, None] * inv[None, :]
    cs_, sn_ = jnp.cos(ang), jnp.sin(ang)
    zpad = jnp.zeros((Lp, LANES - QK_ROPE), f32)
    cos_t = jnp.concatenate([cs_, cs_, zpad], axis=1)
    sin_t = jnp.concatenate([-sn_, sn_, zpad], axis=1)
    dt_bias_p = jnp.pad(ssm_dt_bias, ((0, 0), (0, LANES - SSM_HEADS)))
    a_neg = -jnp.exp(ssm_A_log)
    a_row = jnp.pad(a_neg, ((0, 0), (0, LANES - SSM_HEADS)))
    a_col = jnp.broadcast_to(a_neg.reshape(SSM_HEADS, 1), (SSM_HEADS, LANES))
    d_exp = jnp.repeat(ssm_D, SSM_P, axis=1)
    a_e = jnp.repeat(a_neg, SSM_P, axis=1)
    head_ind = (jnp.arange(D_SSM)[:, None] // SSM_P == jnp.arange(LANES)[None, :]).astype(f32)

    xb = x[0]
    h0 = jnp.concatenate([meta_full, xb, jnp.zeros((Lp - n_real, D), f32)], axis=0)
    tgt = jnp.pad(loss_target[0], ((N_META, Lp - n_real), (0, 0)))
    hn1 = _rms_fwd(h0, norm_mix_pre, bf16, "norm_mix_pre")
    q_c = _mm([(hn1, w_q)], f32, False, "proj_q")
    kv_c = _mm([(hn1, w_kv)], f32, False, "proj_kv")
    kpe_raw = _mm([(hn1, w_rope)], f32, False, "proj_rope")
    z = _mm([(hn1, w_z)], f32, False, "proj_z")
    xbc = _mm([(hn1, w_xbc)], f32, False, "proj_xbc")
    dt_raw = _mm([(hn1, w_dt)], f32, False, "proj_dt")

    qn = _rms_fwd(q_c, q_a_norm, bf16, "norm_q")
    kvn = _rms_fwd(kv_c, kv_a_norm, bf16, "norm_kv")
    qh = _up_q_rope(qn, wuq, cos_t, sin_t)
    kh, vh = _up_kv_rope(kvn, wukv, kpe_raw, cos_t, sin_t)
    attn, lse = _flash_fwd(qh, kh, vh)
    an = _rms_fwd(attn, attn_out_norm, bf16, "norm_attn_out")

    xbc_c = _ssm_conv_fwd(xbc, sconv_w, ssm_conv_b)
    dtp, dt_e = _dt_fwd(dt_raw, dt_bias_p, jnp.transpose(head_ind))
    dtt = jnp.transpose(dtp[:, :SSM_HEADS])
    y_ssd, hin = _ssd_fwd(xbc_c, dt_e, dtt, a_e, a_col)
    ssm = _gate_norm_fwd(y_ssd, xbc_c, z, d_exp, ssm_norm)

    mix = _mm([(an, wout_a), (ssm, wout_s)], f32, False, "out_proj")
    h1, hn2 = _resid_norm(h0, mix, norm_mix_post, norm_ffn_pre)
    up = _mm([(hn2, wup)], f32, False, "ffn_up")
    act = _ffn_gate_fwd(up, fconv_w, ffn_conv_b)
    down = _mm([(act, wdown)], f32, False, "ffn_down")
    dh2, d_down, dg_ffn_post, loss_part = _final(h1, down, norm_ffn_post, tgt, n_real)

    d_act = _mm([(d_down, wdown)], f32, True, "ffn_down_dx")
    dw_down = _mm_tn(act, d_down, "ffn_down_dw")
    dup_g, dup_v, dwc_g, dwc_v, dbc_g, dbc_v = _ffn_gate_bwd(up, fconv_w, ffn_conv_b, d_act)
    d_hn2 = _mm([(dup_g, wup[:, :D_FF]), (dup_v, wup[:, D_FF:])], f32, True, "ffn_up_dx")
    dw_up = jnp.concatenate([_mm_tn(hn2, dup_g, "ffn_up_dw_g"), _mm_tn(hn2, dup_v, "ffn_up_dw_v")], axis=1)
    dh1, d_mix, dg_ffn_pre, dg_mix_post = _mid_bwd(h1, norm_ffn_pre, d_hn2, dh2, mix, norm_mix_post)
    d_an = _mm([(d_mix, wout_a)], f32, True, "out_proj_dx_a")
    d_ssm = _mm([(d_mix, wout_s)], f32, True, "out_proj_dx_s")
    dw_out = jnp.concatenate([_mm_tn(an, d_mix, "out_proj_dw_a"), _mm_tn(ssm, d_mix, "out_proj_dw_s")], axis=0)

    d_attn, dg_attn_out = _rms_bwd(attn, attn_out_norm, d_an, f32, "norm_attn_out_bwd")
    do_h, delta = _attn_delta(attn, d_attn)
    dqh, dkh, dvh = _flash_bwd(qh, kh, vh, do_h, lse, delta)
    dq_full = _rope_q_bwd(dqh, cos_t, sin_t)
    dkv_full, d_kpe_raw = _rope_k_bwd(dkh, dvh, cos_t, sin_t)
    d_qn = _mm([(dq_full, wuq)], f32, True, "up_q_dx")
    dw_uq = _mm_tn(qn, dq_full, "up_q_dw")
    d_kvn = _mm([(dkv_full, wukv)], f32, True, "up_kv_dx")
    dw_ukv = _mm_tn(kvn, dkv_full, "up_kv_dw")
    d_q_c, dg_q = _rms_bwd(q_c, q_a_norm, d_qn, bf16, "norm_q_bwd")
    d_kv_c, dg_kv = _rms_bwd(kv_c, kv_a_norm, d_kvn, bf16, "norm_kv_bwd")

    dy_ssd, dz, dg_ssm, dd_heads = _gate_norm_bwd(y_ssd, xbc_c, z, d_exp, ssm_norm, d_ssm, head_ind)
    d_xbc_c, ddt, da_heads = _ssd_bwd(xbc_c, dtp, dt_e, dtt, a_row, a_e, a_col, hin, dy_ssd, d_exp, head_ind)
    d_xbc, dw_sconv, db_sconv = _ssm_conv_bwd(xbc, sconv_w, ssm_conv_b, d_xbc_c)
    d_dt_raw, d_dt_bias = _dt_bwd(dt_raw, dt_bias_p, ddt)

    segs = [(d_q_c, w_q), (d_kv_c, w_kv), (d_kpe_raw, w_rope), (dz, w_z), (d_xbc, w_xbc), (d_dt_raw, w_dt)]
    d_hn1 = _mm(segs, f32, True, "proj_dx")
    dw_q = _mm_tn(hn1, d_q_c, "proj_dw_q")
    dw_kv = _mm_tn(hn1, d_kv_c, "proj_dw_kv")
    dw_rope = _mm_tn(hn1, d_kpe_raw, "proj_dw_rope")
    dw_z = _mm_tn(hn1, dz, "proj_dw_z")
    dw_xbc = _mm_tn(hn1, d_xbc, "proj_dw_xbc")
    dw_dt = _mm_tn(hn1, d_dt_raw, "proj_dw_dt")
    dh0, dg_mix_pre = _rms_bwd(h0, norm_mix_pre, d_hn1, f32, "norm_mix_pre_bwd", residual=dh1)

    grad_x = dh0[N_META:n_real][None]
    d_meta = dh0[:N_META]

    dw_in = jnp.concatenate([dw_q, dw_kv, dw_rope[:, :QK_ROPE], dw_z, dw_xbc, dw_dt[:, :SSM_HEADS]], axis=1)

    def col_blocks(gm):
        r, cc = gm.shape
        return jnp.transpose(gm.reshape(r, N_DEV, cc // N_DEV), (1, 0, 2))

    dw_uq3 = dw_uq.reshape(Q_RANK, MLA_HEADS, QK_PAD)[:, :, :QK_NOPE + QK_ROPE]
    dest_blocks = [
        col_blocks(dw_in),
        dw_uq3.reshape(N_DEV, Q_RANK // N_DEV, MLA_HEADS, QK_NOPE + QK_ROPE),
        dw_ukv.reshape(N_DEV, KV_RANK // N_DEV, MLA_HEADS, QK_NOPE + V_DIM),
        dw_out.reshape(N_DEV, (D_ATTN + D_SSM) // N_DEV, D),
        col_blocks(dw_up),
        dw_down.reshape(N_DEV, D_FF // N_DEV, D),
        col_blocks(d_meta),
        col_blocks(dw_sconv[:SSM_CONV]),
        col_blocks(jnp.concatenate([dwc_g, dwc_v], axis=1)[:FFN_CONV]),
    ]
    grad_rows = _round_up(sum(-(-int(np.prod(a.shape[1:])) // PACK_W) for a in dest_blocks), PACK_ROW_TILE)
    gpack = _pack(dest_blocks, 1, grad_rows, bf16)
    (gparts,) = _exchange([gpack], True, "scatter_grads")
    wpack = _pack([a[None] for a in sharded_names], 1, grad_rows, f32)[0]
    mpack = _pack([a[None] for a in sharded_m], 1, grad_rows, f32)[0]
    vpack = _pack([a[None] for a in sharded_v], 1, grad_rows, f32)[0]
    outs_big = _adamw(gparts, wpack, mpack, vpack, "adamw_sharded")
    shard_shapes = [a.shape for a in sharded_names]
    g_sh, d_sh, m_sh, v_sh = [_unpack(b[None], shard_shapes, 1) for b in outs_big]
    g_sh, d_sh, m_sh, v_sh = [[t[0] for t in lst] for lst in (g_sh, d_sh, m_sh, v_sh)]

    dg_alog = da_heads[:, :SSM_HEADS] * a_neg
    repl_g = [dg_mix_pre, dg_mix_post, dg_ffn_pre, dg_ffn_post, dg_q, dg_kv, dg_attn_out, db_sconv,
              d_dt_bias[:, :SSM_HEADS], dg_alog, dd_heads[:, :SSM_HEADS], dg_ssm,
              jnp.concatenate([dbc_g, dbc_v], axis=1)]
    loss_vec = loss_part[:, :1]
    small_total = _round_up(sum(-(-int(np.prod(a.shape)) // PACK_W) for a in repl_g) + 1, 16)
    spack = _pack(repl_g + [loss_vec], 0, small_total, f32)
    (sparts,) = _exchange([spack], False, "gather_small_grads")
    zero1 = jnp.zeros((1, 1), f32)
    rw = _pack(repl_w + [zero1], 0, small_total, f32)
    rm = _pack(repl_m + [zero1], 0, small_total, f32)
    rv = _pack(repl_v + [zero1], 0, small_total, f32)
    outs_small = _adamw(sparts, rw, rm, rv, "adamw_replicated")
    repl_shapes = [a.shape for a in repl_w] + [(1, 1)]
    g_rp, d_rp, m_rp, v_rp = [_unpack(b, repl_shapes, 0) for b in outs_small]
    loss = g_rp[-1][0, 0]

    order = ["meta_tokens", "norm_mix_pre", "norm_mix_post", "norm_ffn_pre", "norm_ffn_post", "w_in", "q_a_norm",
             "w_uq", "kv_a_norm", "w_ukv", "attn_out_norm", "ssm_conv_w", "ssm_conv_b", "ssm_dt_bias", "ssm_A_log",
             "ssm_D", "ssm_norm", "w_out", "w_up", "ffn_conv_w", "ffn_conv_b", "w_down"]
    sh_names = ["w_in", "w_uq", "w_ukv", "w_out", "w_up", "w_down", "meta_tokens", "ssm_conv_w", "ffn_conv_w"]
    rp_names = ["norm_mix_pre", "norm_mix_post", "norm_ffn_pre", "norm_ffn_post", "q_a_norm", "kv_a_norm",
                "attn_out_norm", "ssm_conv_b", "ssm_dt_bias", "ssm_A_log", "ssm_D", "ssm_norm", "ffn_conv_b"]

    def lookup(sh_list, rp_list):
        d = {n: t for n, t in zip(sh_names, sh_list)}
        d.update({n: t for n, t in zip(rp_names, rp_list)})
        return [d[n] for n in order]

    return (loss, grad_x, *lookup(g_sh, g_rp), *lookup(d_sh, d_rp), *lookup(m_sh, m_rp), *lookup(v_sh, v_rp))
```

```python
import functools
import math

import jax
import jax.numpy as jnp
import numpy as np
from jax import lax
from jax.experimental import pallas as pl
from jax.experimental.pallas import tpu as pltpu

f32 = jnp.float32
bf16 = jnp.bfloat16

D_MODEL = 1024
SEQ = 8192
N_META = 16
MLA_HEADS = 8
QK_NOPE = 128
QK_ROPE = 64
V_DIM = 128
Q_RANK = 384
KV_RANK = 256
ROPE_THETA = 10000.0
SOFTMAX_SCALE = (QK_NOPE + QK_ROPE) ** -0.5
D_ATTN = MLA_HEADS * V_DIM
SSM_HEADS = 16
SSM_P = 64
SSM_GROUPS = 2
SSM_HPG = SSM_HEADS // SSM_GROUPS
SSM_N = 128
SSM_CONV = 4
CHUNK = 128
D_SSM = SSM_HEADS * SSM_P
D_BC = SSM_GROUPS * SSM_N
D_XBC = D_SSM + 2 * D_BC
D_FF = 2816
FFN_CONV = 3
EPS = 1e-6
D_IN = Q_RANK + KV_RANK + QK_ROPE + D_SSM + D_XBC + SSM_HEADS
QK_PAD = 256
N_DEV = 8

ADAM_LR = 0.001
ADAM_B1 = 0.9
ADAM_B2 = 0.999
ADAM_EPS = 1e-08
ADAM_WD = 0.01
ADAM_STEP = 10

LANES = 128
SUBLANES = 8
ROW_TILE = 256
VMEM_LIMIT = 56 * 1024 * 1024
PACK_W = 1024
PACK_ROW_TILE = 128
NEG = -1e30
LOG2E = math.log2(math.e)
LN2 = math.log(2.0)
Q_PRESCALE = SOFTMAX_SCALE * LOG2E

_MESH = pl.DeviceIdType.MESH


def _pick(n, prefs):
    for p in prefs:
        if n % p == 0:
            return p
    return n


def _rt(m):
    return _pick(m, (384, ROW_TILE))


def _cparams(sem):
    return pltpu.CompilerParams(dimension_semantics=sem, vmem_limit_bytes=VMEM_LIMIT)


def _row(spec_cols, tm):
    return pl.BlockSpec((tm, spec_cols), lambda i: (i, 0))


def _full(shape):
    nd = len(shape)
    return pl.BlockSpec(shape, lambda *a: (0,) * nd)


def _sigmoid(x):
    return 1.0 / (1.0 + jnp.exp(-x))


def _silu(x):
    return x * _sigmoid(x)


def _dsilu(x):
    s = _sigmoid(x)
    return s * (1.0 + x * (1.0 - s))


def _dot(a, b):
    return jnp.dot(a, b, preferred_element_type=f32)


def _dot_nt(a, b):
    return lax.dot_general(a, b, (((1,), (1,)), ((), ())), preferred_element_type=f32)


def _dot_tn(a, b):
    return lax.dot_general(a, b, (((0,), (0,)), ((), ())), preferred_element_type=f32)


def _dot_hi(a, b):
    return jnp.dot(a, b, precision=lax.Precision.HIGHEST, preferred_element_type=f32)


def _mm(pairs, out_dtype, trans_b, name):
    n = len(pairs)
    M = pairs[0][0].shape[0]
    N = pairs[0][1].shape[0] if trans_b else pairs[0][1].shape[1]
    tm = _pick(M, (768, 512, 256))
    tn = _pick(N, (512, 1408, 384, 256, 128))

    def body(*refs):
        o_ref = refs[2 * n]
        acc = None
        for p in range(n):
            a = refs[2 * p][...].astype(bf16)
            b = refs[2 * p + 1][...].astype(bf16)
            r = _dot_nt(a, b) if trans_b else _dot(a, b)
            acc = r if acc is None else acc + r
        o_ref[...] = acc.astype(out_dtype)

    in_specs, args = [], []
    for a, b in pairs:
        k = a.shape[1]
        in_specs.append(pl.BlockSpec((tm, k), lambda i, j: (i, 0)))
        if trans_b:
            in_specs.append(pl.BlockSpec((tn, k), lambda i, j: (j, 0)))
        else:
            in_specs.append(pl.BlockSpec((k, tn), lambda i, j: (0, j)))
        args += [a, b]
    return pl.pallas_call(
        body, name=name, grid=(M // tm, N // tn), in_specs=in_specs,
        out_specs=pl.BlockSpec((tm, tn), lambda i, j: (i, j)),
        out_shape=jax.ShapeDtypeStruct((M, N), out_dtype),
        compiler_params=_cparams(("parallel", "parallel")),
    )(*args)


def _mm_tn(a, g, name):
    M, K = a.shape
    N = g.shape[1]
    tm = _pick(M, (768, 512, 256))
    tk = _pick(K, (1024, 1408, 512, 384, 256))
    tn = _pick(N, (1024, 1408, 512, 384, 256, 128))

    def body(a_ref, g_ref, o_ref):
        @pl.when(pl.program_id(2) == 0)
        def _():
            o_ref[...] = jnp.zeros_like(o_ref)

        o_ref[...] += _dot_tn(a_ref[...].astype(bf16), g_ref[...].astype(bf16))

    return pl.pallas_call(
        body, name=name, grid=(K // tk, N // tn, M // tm),
        in_specs=[pl.BlockSpec((tm, tk), lambda k, j, m: (m, k)),
                  pl.BlockSpec((tm, tn), lambda k, j, m: (m, j))],
        out_specs=pl.BlockSpec((tk, tn), lambda k, j, m: (k, j)),
        out_shape=jax.ShapeDtypeStruct((K, N), f32),
        compiler_params=_cparams(("parallel", "parallel", "arbitrary")),
    )(a, g)


def _rstd(x):
    return lax.rsqrt(jnp.mean(x * x, axis=-1, keepdims=True) + EPS)


def _rms_bwd_math(x, g, dy):
    r = _rstd(x)
    xh = x * r
    dn = dy * g
    dx = r * (dn - xh * jnp.mean(dn * xh, axis=-1, keepdims=True))
    return dx, dy * xh


def _rms_fwd(x, g, out_dtype, name):
    M, K = x.shape
    tm = _rt(M)

    def body(x_ref, g_ref, o_ref):
        xv = x_ref[...]
        o_ref[...] = (xv * _rstd(xv) * g_ref[...]).astype(out_dtype)

    return pl.pallas_call(
        body, name=name, grid=(M // tm,), in_specs=[_row(K, tm), _full((1, K))],
        out_specs=_row(K, tm), out_shape=jax.ShapeDtypeStruct((M, K), out_dtype),
        compiler_params=_cparams(("parallel",)),
    )(x, g)


def _rms_bwd(x, g, dy, out_dtype, name, residual=None):
    M, K = x.shape
    tm = _rt(M)
    has_res = residual is not None

    def body(*refs):
        if has_res:
            x_ref, g_ref, dy_ref, r_ref, dx_ref, dg_ref = refs
        else:
            x_ref, g_ref, dy_ref, dx_ref, dg_ref = refs

        @pl.when(pl.program_id(0) == 0)
        def _():
            dg_ref[...] = jnp.zeros_like(dg_ref)

        dx, dgp = _rms_bwd_math(x_ref[...], g_ref[...], dy_ref[...].astype(f32))
        if has_res:
            dx = dx + r_ref[...]
        dx_ref[...] = dx.astype(out_dtype)
        dg_ref[...] += jnp.sum(dgp, axis=0, keepdims=True)

    ins = [x, g, dy] + ([residual] if has_res else [])
    in_specs = [_row(K, tm), _full((1, K)), _row(K, tm)] + ([_row(K, tm)] if has_res else [])
    return pl.pallas_call(
        body, name=name, grid=(M // tm,), in_specs=in_specs,
        out_specs=[_row(K, tm), _full((1, K))],
        out_shape=[jax.ShapeDtypeStruct((M, K), out_dtype), jax.ShapeDtypeStruct((1, K), f32)],
        compiler_params=_cparams(("arbitrary",)),
    )(*ins)


def _resid_norm(h0, mix, g2, g3):
    M, K = h0.shape
    tm = _rt(M)

    def body(h_ref, m_ref, g2_ref, g3_ref, h1_ref, hn_ref):
        mv = m_ref[...]
        h1 = h_ref[...] + mv * _rstd(mv) * g2_ref[...]
        h1_ref[...] = h1
        hn_ref[...] = (h1 * _rstd(h1) * g3_ref[...]).astype(bf16)

    return pl.pallas_call(
        body, name="resid_norm", grid=(M // tm,),
        in_specs=[_row(K, tm), _row(K, tm), _full((1, K)), _full((1, K))],
        out_specs=[_row(K, tm), _row(K, tm)],
        out_shape=[jax.ShapeDtypeStruct((M, K), f32), jax.ShapeDtypeStruct((M, K), bf16)],
        compiler_params=_cparams(("parallel",)),
    )(h0, mix, g2, g3)


def _final(h1, down, g4, tgt, n_real):
    M, K = h1.shape
    tm = _rt(M)
    nt = M // tm

    def body(h_ref, d_ref, g_ref, t_ref, dh_ref, dd_ref, dg_ref, ls_ref, acc_ref):
        i = pl.program_id(0)

        @pl.when(i == 0)
        def _():
            dg_ref[...] = jnp.zeros_like(dg_ref)
            acc_ref[...] = jnp.zeros_like(acc_ref)

        dv = d_ref[...]
        g = g_ref[...]
        r = _rstd(dv)
        n = dv * r
        h2 = h_ref[...] + n * g
        rows = i * tm + lax.broadcasted_iota(jnp.int32, (tm, 1), 0)
        mask = ((rows >= N_META) & (rows < n_real)).astype(f32)
        diff = (h2 - t_ref[...]) * mask
        acc_ref[...] += jnp.sum(diff * diff, axis=0, keepdims=True)
        dh = diff * (1.0 / K)
        dh_ref[...] = dh
        dn = dh * g
        dd_ref[...] = (r * (dn - n * jnp.mean(dn * n, axis=-1, keepdims=True))).astype(bf16)
        dg_ref[...] += jnp.sum(dh * n, axis=0, keepdims=True)

        @pl.when(i == nt - 1)
        def _():
            ls_ref[...] = jnp.zeros((1, LANES), f32) + jnp.sum(acc_ref[...]) * (0.5 / K)

    return pl.pallas_call(
        body, name="final_loss", grid=(nt,),
        in_specs=[_row(K, tm), _row(K, tm), _full((1, K)), _row(K, tm)],
        out_specs=[_row(K, tm), _row(K, tm), _full((1, K)), _full((1, LANES))],
        out_shape=[jax.ShapeDtypeStruct((M, K), f32), jax.ShapeDtypeStruct((M, K), bf16),
                   jax.ShapeDtypeStruct((1, K), f32), jax.ShapeDtypeStruct((1, LANES), f32)],
        scratch_shapes=[pltpu.VMEM((1, K), f32)],
        compiler_params=_cparams(("arbitrary",)),
    )(h1, down, g4, tgt)


def _mid_bwd(h1, g3, d_hn2, dh2, mix, g2):
    M, K = h1.shape
    tm = _rt(M)

    def body(h_ref, g3_ref, dn_ref, dh2_ref, m_ref, g2_ref, dh1_ref, dm_ref, dg3_ref, dg2_ref):
        @pl.when(pl.program_id(0) == 0)
        def _():
            dg3_ref[...] = jnp.zeros_like(dg3_ref)
            dg2_ref[...] = jnp.zeros_like(dg2_ref)

        dx, dgp = _rms_bwd_math(h_ref[...], g3_ref[...], dn_ref[...])
        dh1 = dh2_ref[...] + dx
        dh1_ref[...] = dh1
        dg3_ref[...] += jnp.sum(dgp, axis=0, keepdims=True)
        dm, dgp2 = _rms_bwd_math(m_ref[...], g2_ref[...], dh1)
        dm_ref[...] = dm.astype(bf16)
        dg2_ref[...] += jnp.sum(dgp2, axis=0, keepdims=True)

    return pl.pallas_call(
        body, name="mid_bwd", grid=(M // tm,),
        in_specs=[_row(K, tm), _full((1, K)), _row(K, tm), _row(K, tm), _row(K, tm), _full((1, K))],
        out_specs=[_row(K, tm), _row(K, tm), _full((1, K)), _full((1, K))],
        out_shape=[jax.ShapeDtypeStruct((M, K), f32), jax.ShapeDtypeStruct((M, K), bf16),
                   jax.ShapeDtypeStruct((1, K), f32), jax.ShapeDtypeStruct((1, K), f32)],
        compiler_params=_cparams(("arbitrary",)),
    )(h1, g3, d_hn2, dh2, mix, g2)


def _conv_taps(ext_ref, w_ref, kw, tm, first):
    u = None
    for k in range(kw):
        t = ext_ref[pl.ds(first + k, tm), :] * w_ref[k:k + 1, :]
        u = t if u is None else u + t
    return u


def _fill_prev(ext_ref, x_ref, halo_ref, i, tm):
    ext_ref[0:SUBLANES, :] = jnp.where(i == 0, 0.0, halo_ref[...])
    ext_ref[SUBLANES:SUBLANES + tm, :] = x_ref[...]


def _prev_spec(tm, tc, col_of, row_axis, reversed_tiles=0):
    def imap(*ids):
        i = ids[row_axis]
        if reversed_tiles:
            i = reversed_tiles - 1 - i
        return (jnp.maximum(i * (tm // SUBLANES) - 1, 0), col_of(*ids))
    return pl.BlockSpec((SUBLANES, tc), imap)


def _conv_dx_carry(edu_ref, du, w_ref, kw, tm, first_step):
    @pl.when(first_step)
    def _():
        edu_ref[tm:tm + SUBLANES, :] = jnp.zeros((SUBLANES, edu_ref.shape[1]), f32)

    edu_ref[0:tm, :] = du
    acc = None
    for k in range(kw):
        t = edu_ref[pl.ds(kw - 1 - k, tm), :] * w_ref[k:k + 1, :]
        acc = t if acc is None else acc + t
    edu_ref[tm:tm + SUBLANES, :] = edu_ref[0:SUBLANES, :]
    return acc


def _ssm_conv_fwd(xbc, w, b):
    M, C = xbc.shape
    tm, tc, kw = ROW_TILE, C, SSM_CONV

    def body(x_ref, h_ref, w_ref, b_ref, o_ref, ext_ref):
        _fill_prev(ext_ref, x_ref, h_ref, pl.program_id(0), tm)
        u = _conv_taps(ext_ref, w_ref, kw, tm, SUBLANES - (kw - 1)) + b_ref[...]
        o_ref[...] = _silu(u)

    return pl.pallas_call(
        body, name="ssm_conv_fwd", grid=(M // tm, C // tc),
        in_specs=[pl.BlockSpec((tm, tc), lambda i, j: (i, j)),
                  _prev_spec(tm, tc, lambda i, j: j, 0),
                  pl.BlockSpec((SUBLANES, tc), lambda i, j: (0, j)),
                  pl.BlockSpec((1, tc), lambda i, j: (0, j))],
        out_specs=pl.BlockSpec((tm, tc), lambda i, j: (i, j)),
        out_shape=jax.ShapeDtypeStruct((M, C), f32),
        scratch_shapes=[pltpu.VMEM((tm + SUBLANES, tc), f32)],
        compiler_params=_cparams(("parallel", "parallel")),
    )(xbc, xbc, w, b)


def _ssm_conv_bwd(xbc, w, b, dout):
    M, C = xbc.shape
    tm, tc, kw = ROW_TILE, C // 3, SSM_CONV
    nt = M // tm

    def body(x_ref, h_ref, w_ref, b_ref, d_ref, dx_ref, dw_ref, db_ref, ext_ref, edu_ref):
        i = pl.program_id(1)

        @pl.when(i == 0)
        def _():
            dw_ref[...] = jnp.zeros_like(dw_ref)
            db_ref[...] = jnp.zeros_like(db_ref)

        _fill_prev(ext_ref, x_ref, h_ref, nt - 1 - i, tm)
        first = SUBLANES - (kw - 1)
        u = _conv_taps(ext_ref, w_ref, kw, tm, first) + b_ref[...]
        du = d_ref[...] * _dsilu(u)
        db_ref[...] += jnp.sum(du, axis=0, keepdims=True)
        for k in range(kw):
            dw_ref[k:k + 1, :] += jnp.sum(du * ext_ref[pl.ds(first + k, tm), :], axis=0, keepdims=True)
        dx_ref[...] = _conv_dx_carry(edu_ref, du, w_ref, kw, tm, i == 0).astype(bf16)

    tile = pl.BlockSpec((tm, tc), lambda j, i: (nt - 1 - i, j))
    return pl.pallas_call(
        body, name="ssm_conv_bwd", grid=(C // tc, nt),
        in_specs=[tile, _prev_spec(tm, tc, lambda j, i: j, 1, nt),
                  pl.BlockSpec((SUBLANES, tc), lambda j, i: (0, j)),
                  pl.BlockSpec((1, tc), lambda j, i: (0, j)), tile],
        out_specs=[tile, pl.BlockSpec((SUBLANES, tc), lambda j, i: (0, j)),
                   pl.BlockSpec((1, tc), lambda j, i: (0, j))],
        out_shape=[jax.ShapeDtypeStruct((M, C), bf16), jax.ShapeDtypeStruct((SUBLANES, C), f32),
                   jax.ShapeDtypeStruct((1, C), f32)],
        scratch_shapes=[pltpu.VMEM((tm + SUBLANES, tc), f32), pltpu.VMEM((tm + SUBLANES, tc), f32)],
        compiler_params=_cparams(("parallel", "arbitrary")),
    )(xbc, xbc, w, b, dout)


def _ffn_gate_fwd(up, w, b):
    M = up.shape[0]
    tm, tc, kw = ROW_TILE, D_FF // 2, FFN_CONV
    nc = D_FF // tc

    def body(xg_ref, hg_ref, xv_ref, hv_ref, wg_ref, wv_ref, bg_ref, bv_ref, o_ref, eg_ref, ev_ref):
        i = pl.program_id(0)
        first = SUBLANES - (kw - 1)
        _fill_prev(eg_ref, xg_ref, hg_ref, i, tm)
        _fill_prev(ev_ref, xv_ref, hv_ref, i, tm)
        ug = _conv_taps(eg_ref, wg_ref, kw, tm, first) + bg_ref[...]
        uv = _conv_taps(ev_ref, wv_ref, kw, tm, first) + bv_ref[...]
        o_ref[...] = (_silu(ug) * uv).astype(bf16)

    return pl.pallas_call(
        body, name="ffn_gate_fwd", grid=(M // tm, nc),
        in_specs=[pl.BlockSpec((tm, tc), lambda i, j: (i, j)),
                  _prev_spec(tm, tc, lambda i, j: j, 0),
                  pl.BlockSpec((tm, tc), lambda i, j: (i, j + nc)),
                  _prev_spec(tm, tc, lambda i, j: j + nc, 0),
                  pl.BlockSpec((SUBLANES, tc), lambda i, j: (0, j)),
                  pl.BlockSpec((SUBLANES, tc), lambda i, j: (0, j + nc)),
                  pl.BlockSpec((1, tc), lambda i, j: (0, j)),
                  pl.BlockSpec((1, tc), lambda i, j: (0, j + nc))],
        out_specs=pl.BlockSpec((tm, tc), lambda i, j: (i, j)),
        out_shape=jax.ShapeDtypeStruct((M, D_FF), bf16),
        scratch_shapes=[pltpu.VMEM((tm + SUBLANES, tc), f32), pltpu.VMEM((tm + SUBLANES, tc), f32)],
        compiler_params=_cparams(("parallel", "parallel")),
    )(up, up, up, up, w, w, b, b)


def _ffn_gate_bwd(up, w, b, d_act):
    M = up.shape[0]
    tm, tc, kw = ROW_TILE, D_FF // 2, FFN_CONV
    nc = D_FF // tc
    nt = M // tm

    def body(xg_ref, hg_ref, xv_ref, hv_ref, wg_ref, wv_ref, bg_ref, bv_ref, d_ref,
             dxg_ref, dxv_ref, dwg_ref, dwv_ref, dbg_ref, dbv_ref, eg_ref, ev_ref, edg_ref, edv_ref):
        i = pl.program_id(1)

        @pl.when(i == 0)
        def _():
            for r in (dwg_ref, dwv_ref, dbg_ref, dbv_ref):
                r[...] = jnp.zeros_like(r)

        first = SUBLANES - (kw - 1)
        _fill_prev(eg_ref, xg_ref, hg_ref, nt - 1 - i, tm)
        _fill_prev(ev_ref, xv_ref, hv_ref, nt - 1 - i, tm)
        ug = _conv_taps(eg_ref, wg_ref, kw, tm, first) + bg_ref[...]
        uv = _conv_taps(ev_ref, wv_ref, kw, tm, first) + bv_ref[...]
        da = d_ref[...]
        dug = da * uv * _dsilu(ug)
        duv = da * _silu(ug)
        dbg_ref[...] += jnp.sum(dug, axis=0, keepdims=True)
        dbv_ref[...] += jnp.sum(duv, axis=0, keepdims=True)
        for k in range(kw):
            dwg_ref[k:k + 1, :] += jnp.sum(dug * eg_ref[pl.ds(first + k, tm), :], axis=0, keepdims=True)
            dwv_ref[k:k + 1, :] += jnp.sum(duv * ev_ref[pl.ds(first + k, tm), :], axis=0, keepdims=True)
        dxg_ref[...] = _conv_dx_carry(edg_ref, dug, wg_ref, kw, tm, i == 0).astype(bf16)
        dxv_ref[...] = _conv_dx_carry(edv_ref, duv, wv_ref, kw, tm, i == 0).astype(bf16)

    tile_g = pl.BlockSpec((tm, tc), lambda j, i: (nt - 1 - i, j))
    tile_v = pl.BlockSpec((tm, tc), lambda j, i: (nt - 1 - i, j + nc))
    ext = pltpu.VMEM((tm + SUBLANES, tc), f32)
    return pl.pallas_call(
        body, name="ffn_gate_bwd", grid=(nc, nt),
        in_specs=[tile_g, _prev_spec(tm, tc, lambda j, i: j, 1, nt),
                  tile_v, _prev_spec(tm, tc, lambda j, i: j + nc, 1, nt),
                  pl.BlockSpec((SUBLANES, tc), lambda j, i: (0, j)),
                  pl.BlockSpec((SUBLANES, tc), lambda j, i: (0, j + nc)),
                  pl.BlockSpec((1, tc), lambda j, i: (0, j)),
                  pl.BlockSpec((1, tc), lambda j, i: (0, j + nc)),
                  tile_g],
        out_specs=[tile_g, tile_g,
                   pl.BlockSpec((SUBLANES, tc), lambda j, i: (0, j)),
                   pl.BlockSpec((SUBLANES, tc), lambda j, i: (0, j)),
                   pl.BlockSpec((1, tc), lambda j, i: (0, j)),
                   pl.BlockSpec((1, tc), lambda j, i: (0, j))],
        out_shape=[jax.ShapeDtypeStruct((M, D_FF), bf16), jax.ShapeDtypeStruct((M, D_FF), bf16),
                   jax.ShapeDtypeStruct((SUBLANES, D_FF), f32), jax.ShapeDtypeStruct((SUBLANES, D_FF), f32),
                   jax.ShapeDtypeStruct((1, D_FF), f32), jax.ShapeDtypeStruct((1, D_FF), f32)],
        scratch_shapes=[ext, ext, ext, ext],
        compiler_params=_cparams(("parallel", "arbitrary")),
    )(up, up, up, up, w, w, b, b, d_act)


def _rope_apply(blk, cos, sin):
    lane = lax.broadcasted_iota(jnp.int32, blk.shape, 1)
    half = QK_ROPE // 2
    partner = jnp.where(lane < half, pltpu.roll(blk, LANES - half, 1), pltpu.roll(blk, half, 1))
    return blk * cos + partner * sin


def _rope_unapply(d, cos, sin):
    t = d * sin
    lane = lax.broadcasted_iota(jnp.int32, d.shape, 1)
    half = QK_ROPE // 2
    partner = jnp.where(lane < half, pltpu.roll(t, LANES - half, 1), pltpu.roll(t, half, 1))
    return d * cos + partner


def _up_q_rope(qn, wuq, cos, sin):
    M, K = qn.shape
    tm = _pick(M, (768, 512, 256))

    def body(a_ref, b_ref, c_ref, s_ref, o_ref):
        r = _dot(a_ref[...], b_ref[...]) * Q_PRESCALE
        o_ref[0, :, 0:QK_NOPE] = r[:, 0:QK_NOPE].astype(bf16)
        o_ref[0, :, QK_NOPE:QK_PAD] = _rope_apply(r[:, QK_NOPE:QK_PAD], c_ref[...], s_ref[...]).astype(bf16)

    return pl.pallas_call(
        body, name="up_q_rope", grid=(M // tm, MLA_HEADS),
        in_specs=[pl.BlockSpec((tm, K), lambda i, h: (i, 0)),
                  pl.BlockSpec((K, QK_PAD), lambda i, h: (0, h)),
                  pl.BlockSpec((tm, LANES), lambda i, h: (i, 0)),
                  pl.BlockSpec((tm, LANES), lambda i, h: (i, 0))],
        out_specs=pl.BlockSpec((1, tm, QK_PAD), lambda i, h: (h, i, 0)),
        out_shape=jax.ShapeDtypeStruct((MLA_HEADS, M, QK_PAD), bf16),
        compiler_params=_cparams(("parallel", "parallel")),
    )(qn, wuq, cos, sin)


def _up_kv_rope(kvn, wukv, kpe_raw, cos, sin):
    M, K = kvn.shape
    tm = _pick(M, (768, 512, 256))

    def body(a_ref, b_ref, pe_ref, c_ref, s_ref, k_ref, v_ref):
        r = _dot(a_ref[...], b_ref[...])
        k_ref[0, :, 0:QK_NOPE] = r[:, 0:QK_NOPE].astype(bf16)
        k_ref[0, :, QK_NOPE:QK_PAD] = _rope_apply(pe_ref[...], c_ref[...], s_ref[...]).astype(bf16)
        v_ref[0] = r[:, QK_NOPE:QK_NOPE + V_DIM].astype(bf16)

    return pl.pallas_call(
        body, name="up_kv_rope", grid=(M // tm, MLA_HEADS),
        in_specs=[pl.BlockSpec((tm, K), lambda i, h: (i, 0)),
                  pl.BlockSpec((K, QK_NOPE + V_DIM), lambda i, h: (0, h)),
                  pl.BlockSpec((tm, LANES), lambda i, h: (i, 0)),
                  pl.BlockSpec((tm, LANES), lambda i, h: (i, 0)),
                  pl.BlockSpec((tm, LANES), lambda i, h: (i, 0))],
        out_specs=[pl.BlockSpec((1, tm, QK_PAD), lambda i, h: (h, i, 0)),
                   pl.BlockSpec((1, tm, V_DIM), lambda i, h: (h, i, 0))],
        out_shape=[jax.ShapeDtypeStruct((MLA_HEADS, M, QK_PAD), bf16),
                   jax.ShapeDtypeStruct((MLA_HEADS, M, V_DIM), bf16)],
        compiler_params=_cparams(("parallel", "parallel")),
    )(kvn, wukv, kpe_raw, cos, sin)


def _rope_q_bwd(dq, cos, sin):
    M = dq.shape[1]
    tm = _rt(M)

    def body(d_ref, c_ref, s_ref, o_ref):
        c, s = c_ref[...], s_ref[...]
        for h in range(MLA_HEADS):
            o_ref[:, h * QK_PAD:h * QK_PAD + QK_NOPE] = (d_ref[h, :, 0:QK_NOPE] * SOFTMAX_SCALE).astype(bf16)
            o_ref[:, h * QK_PAD + QK_NOPE:(h + 1) * QK_PAD] = (_rope_unapply(
                d_ref[h, :, QK_NOPE:QK_PAD], c, s) * SOFTMAX_SCALE).astype(bf16)

    return pl.pallas_call(
        body, name="rope_q_bwd", grid=(M // tm,),
        in_specs=[pl.BlockSpec((MLA_HEADS, tm, QK_PAD), lambda i: (0, i, 0)),
                  _row(LANES, tm), _row(LANES, tm)],
        out_specs=_row(MLA_HEADS * QK_PAD, tm),
        out_shape=jax.ShapeDtypeStruct((M, MLA_HEADS * QK_PAD), bf16),
        compiler_params=_cparams(("parallel",)),
    )(dq, cos, sin)


def _rope_k_bwd(dk, dv, cos, sin):
    M = dk.shape[1]
    tm = _rt(M)
    w = QK_NOPE + V_DIM

    def body(dk_ref, dv_ref, c_ref, s_ref, o_ref, pe_ref):
        pe = None
        for h in range(MLA_HEADS):
            o_ref[:, h * w:h * w + QK_NOPE] = dk_ref[h, :, 0:QK_NOPE].astype(bf16)
            o_ref[:, h * w + QK_NOPE:(h + 1) * w] = dv_ref[h].astype(bf16)
            t = dk_ref[h, :, QK_NOPE:QK_PAD]
            pe = t if pe is None else pe + t
        pe_ref[...] = _rope_unapply(pe, c_ref[...], s_ref[...])

    return pl.pallas_call(
        body, name="rope_k_bwd", grid=(M // tm,),
        in_specs=[pl.BlockSpec((MLA_HEADS, tm, QK_PAD), lambda i: (0, i, 0)),
                  pl.BlockSpec((MLA_HEADS, tm, V_DIM), lambda i: (0, i, 0)),
                  _row(LANES, tm), _row(LANES, tm)],
        out_specs=[_row(MLA_HEADS * w, tm), _row(LANES, tm)],
        out_shape=[jax.ShapeDtypeStruct((M, MLA_HEADS * w), bf16), jax.ShapeDtypeStruct((M, LANES), f32)],
        compiler_params=_cparams(("parallel",)),
    )(dk, dv, cos, sin)


def _attn_tile(M):
    return 768 if (M % 768 == 0 and M >= 4 * 768) else ROW_TILE


def _col_to_row(col):
    return col.T[0:1, :]


def _hosted_exchange(refs_in, refs_out, sems, scatter, first, last):
    copies = _exchange_copies(refs_in, refs_out, *sems, scatter)

    @pl.when(first)
    def _():
        for cp in copies:
            cp.start()

    @pl.when(last)
    def _():
        for cp in copies:
            cp.wait()


def _flash_fwd(q, k, v, carried, scatter):
    H, M, _ = q.shape
    T = _attn_tile(M)
    nq = M // T
    nx = len(carried)

    def body(*refs):
        q_ref, k_ref, v_ref = refs[:3]
        o_ref, lse_ref = refs[3 + nx:5 + nx]
        sa_ref, sb_ref, m_sc, l_sc, acc_sc = refs[5 + 2 * nx:10 + 2 * nx]
        h = pl.program_id(0)
        i = pl.program_id(1)
        _hosted_exchange(refs[3:3 + nx], refs[5 + nx:5 + 2 * nx], refs[10 + 2 * nx:], scatter,
                         (h == 0) & (i == 0), (h == H - 1) & (i == nq - 1))
        qv = q_ref[0]
        m_sc[...] = jnp.full_like(m_sc, NEG)
        l_sc[...] = jnp.zeros_like(l_sc)
        acc_sc[...] = jnp.zeros_like(acc_sc)

        def scores(j, s_ref):
            off = pl.multiple_of(j * T, T)
            s_ref[...] = _dot_nt(qv, k_ref[0, pl.ds(off, T), :])

        def softmax_pv(j, s_ref, masked):
            off = pl.multiple_of(j * T, T)
            s = s_ref[...]
            if masked:
                r = lax.broadcasted_iota(jnp.int32, (T, T), 0)
                c = lax.broadcasted_iota(jnp.int32, (T, T), 1)
                s = jnp.where(r >= c, s, NEG)
            m_prev = m_sc[...]
            m_new = jnp.maximum(m_prev, jnp.max(s, axis=1, keepdims=True))
            alpha = jnp.exp2(m_prev - m_new)
            p = jnp.exp2(s - m_new[:, 0:1])
            l_sc[...] = alpha * l_sc[...] + jnp.sum(p, axis=1, keepdims=True)
            acc_sc[...] = alpha * acc_sc[...] + _dot(p.astype(bf16), v_ref[0, pl.ds(off, T), :])
            m_sc[...] = m_new

        scores(0, sa_ref)

        def pair(jj, c):
            j0 = 2 * jj
            scores(j0 + 1, sb_ref)
            softmax_pv(j0, sa_ref, False)
            scores(j0 + 2, sa_ref)
            softmax_pv(j0 + 1, sb_ref, False)
            return c

        lax.fori_loop(0, i // 2, pair, 0)

        @pl.when(i % 2 == 0)
        def _():
            softmax_pv(i, sa_ref, True)

        @pl.when(i % 2 == 1)
        def _():
            scores(i, sb_ref)
            softmax_pv(i - 1, sa_ref, False)
            softmax_pv(i, sb_ref, True)

        l = l_sc[...]
        o_ref[...] = acc_sc[...] / l
        lse_ref[0, 0] = _col_to_row(m_sc[...] + jnp.log2(l))

    any_spec = pl.BlockSpec(memory_space=pl.ANY)
    return pl.pallas_call(
        body, name="flash_fwd", grid=(H, nq),
        in_specs=[pl.BlockSpec((1, T, QK_PAD), lambda h, i: (h, i, 0)),
                  pl.BlockSpec((1, M, QK_PAD), lambda h, i: (h, 0, 0)),
                  pl.BlockSpec((1, M, V_DIM), lambda h, i: (h, 0, 0))] + [any_spec] * nx,
        out_specs=[pl.BlockSpec((T, V_DIM), lambda h, i: (i, h)),
                   pl.BlockSpec((1, 1, 1, T), lambda h, i: (h, i, 0, 0))] + [any_spec] * nx,
        out_shape=[jax.ShapeDtypeStruct((M, H * V_DIM), f32),
                   jax.ShapeDtypeStruct((H, nq, 1, T), f32)] + _exchange_shapes(carried, scatter),
        scratch_shapes=[pltpu.VMEM((T, T), f32), pltpu.VMEM((T, T), f32),
                        pltpu.VMEM((T, LANES), f32), pltpu.VMEM((T, LANES), f32),
                        pltpu.VMEM((T, V_DIM), f32)] + _exchange_sems(nx),
        compiler_params=_cparams(("arbitrary", "arbitrary")),
    )(q, k, v, *carried)


def _attn_delta(o, do):
    M = o.shape[0]
    H = MLA_HEADS
    T = _attn_tile(M)

    def body(o_ref, d_ref, dh_ref, dl_ref):
        dv = d_ref[...]
        dh_ref[0] = dv.astype(bf16)
        col = jnp.sum(o_ref[...] * dv, axis=1, keepdims=True) + jnp.zeros((T, LANES), f32)
        dl_ref[0, 0] = _col_to_row(col)

    return pl.pallas_call(
        body, name="attn_delta", grid=(M // T, H),
        in_specs=[pl.BlockSpec((T, V_DIM), lambda i, h: (i, h)),
                  pl.BlockSpec((T, V_DIM), lambda i, h: (i, h))],
        out_specs=[pl.BlockSpec((1, T, V_DIM), lambda i, h: (h, i, 0)),
                   pl.BlockSpec((1, 1, 1, T), lambda i, h: (h, i, 0, 0))],
        out_shape=[jax.ShapeDtypeStruct((H, M, V_DIM), bf16),
                   jax.ShapeDtypeStruct((H, M // T, 1, T), f32)],
        compiler_params=_cparams(("parallel", "parallel")),
    )(o, do)


def _flash_bwd(q, k, v, do, lse, delta, carried, scatter):
    H, M, _ = q.shape
    T = _attn_tile(M)
    nq = M // T
    nx = len(carried)

    def body(*refs):
        q_ref, do_ref, lse_ref, dl_ref, k_ref, v_ref = refs[:6]
        dq_ref, dk_ref, dv_ref = refs[6 + nx:9 + nx]
        dk_sc, dv_sc = refs[9 + 2 * nx:11 + 2 * nx]
        j = pl.program_id(1)
        _hosted_exchange(refs[6:6 + nx], refs[9 + nx:9 + 2 * nx], refs[11 + 2 * nx:], scatter,
                         (pl.program_id(0) == 0) & (j == 0), (pl.program_id(0) == H - 1) & (j == nq - 1))

        @pl.when(j == 0)
        def _():
            dq_ref[...] = jnp.zeros_like(dq_ref)

        kt = k_ref[0]
        vt = v_ref[0]
        dk_sc[...] = jnp.zeros_like(dk_sc)
        dv_sc[...] = jnp.zeros_like(dv_sc)

        def step(i, masked):
            off = pl.multiple_of(i * T, T)
            qt = q_ref[0, pl.ds(off, T), :]
            dot_ = do_ref[0, pl.ds(off, T), :]
            st = _dot_nt(kt, qt)
            if masked:
                r = lax.broadcasted_iota(jnp.int32, (T, T), 0)
                c = lax.broadcasted_iota(jnp.int32, (T, T), 1)
                st = jnp.where(c >= r, st, NEG)
            pt = jnp.exp2(st - lse_ref[0, i])
            dv_sc[...] += _dot(pt.astype(bf16), dot_)
            dpt = _dot_nt(vt, dot_)
            dst = (pt * (dpt - dl_ref[0, i])).astype(bf16)
            dk_sc[...] += _dot(dst, qt)
            dq_ref[0, pl.ds(off, T), :] += _dot_tn(dst, kt)

        step(j, True)

        def loop_body(i, c):
            step(i, False)
            return c

        lax.fori_loop(j + 1, nq, loop_body, 0)
        dk_ref[0] = dk_sc[...] * LN2
        dv_ref[0] = dv_sc[...]

    any_spec = pl.BlockSpec(memory_space=pl.ANY)
    return pl.pallas_call(
        body, name="flash_bwd", grid=(H, nq),
        in_specs=[pl.BlockSpec((1, M, QK_PAD), lambda h, j: (h, 0, 0)),
                  pl.BlockSpec((1, M, V_DIM), lambda h, j: (h, 0, 0)),
                  pl.BlockSpec((1, nq, 1, T), lambda h, j: (h, 0, 0, 0)),
                  pl.BlockSpec((1, nq, 1, T), lambda h, j: (h, 0, 0, 0)),
                  pl.BlockSpec((1, T, QK_PAD), lambda h, j: (h, j, 0)),
                  pl.BlockSpec((1, T, V_DIM), lambda h, j: (h, j, 0))] + [any_spec] * nx,
        out_specs=[pl.BlockSpec((1, M, QK_PAD), lambda h, j: (h, 0, 0)),
                   pl.BlockSpec((1, T, QK_PAD), lambda h, j: (h, j, 0)),
                   pl.BlockSpec((1, T, V_DIM), lambda h, j: (h, j, 0))] + [any_spec] * nx,
        out_shape=[jax.ShapeDtypeStruct((H, M, QK_PAD), f32),
                   jax.ShapeDtypeStruct((H, M, QK_PAD), f32),
                   jax.ShapeDtypeStruct((H, M, V_DIM), f32)] + _exchange_shapes(carried, scatter),
        scratch_shapes=[pltpu.VMEM((T, QK_PAD), f32), pltpu.VMEM((T, V_DIM), f32)] + _exchange_sems(nx),
        compiler_params=_cparams(("arbitrary", "arbitrary")),
    )(q, do, lse, delta, k, v, *carried)


def _dt_fwd(dt_raw, bias, expand):
    M = dt_raw.shape[0]
    tm = _rt(M)

    def body(x_ref, b_ref, e_ref, o_ref, oe_ref):
        u = x_ref[...] + b_ref[...]
        sp = jnp.maximum(u, 0.0) + jnp.log(1.0 + jnp.exp(-jnp.abs(u)))
        lane = lax.broadcasted_iota(jnp.int32, u.shape, 1)
        dtp = jnp.where(lane < SSM_HEADS, sp, 0.0)
        o_ref[...] = dtp
        oe_ref[...] = _dot_hi(dtp, e_ref[...])

    return pl.pallas_call(
        body, name="dt_fwd", grid=(M // tm,),
        in_specs=[_row(LANES, tm), _full((1, LANES)), _full((LANES, D_SSM))],
        out_specs=[_row(LANES, tm), _row(D_SSM, tm)],
        out_shape=[jax.ShapeDtypeStruct((M, LANES), f32), jax.ShapeDtypeStruct((M, D_SSM), f32)],
        compiler_params=_cparams(("parallel",)),
    )(dt_raw, bias, expand)


def _dt_bwd(dt_raw, bias, ddt):
    M = dt_raw.shape[0]
    tm = _rt(M)

    def body(x_ref, b_ref, d_ref, o_ref, db_ref):
        @pl.when(pl.program_id(0) == 0)
        def _():
            db_ref[...] = jnp.zeros_like(db_ref)

        u = x_ref[...] + b_ref[...]
        lane = lax.broadcasted_iota(jnp.int32, u.shape, 1)
        g = jnp.where(lane < SSM_HEADS, d_ref[...] * _sigmoid(u), 0.0)
        o_ref[...] = g
        db_ref[...] += jnp.sum(g, axis=0, keepdims=True)

    return pl.pallas_call(
        body, name="dt_bwd", grid=(M // tm,),
        in_specs=[_row(LANES, tm), _full((1, LANES)), _row(LANES, tm)],
        out_specs=[_row(LANES, tm), _full((1, LANES))],
        out_shape=[jax.ShapeDtypeStruct((M, LANES), f32), jax.ShapeDtypeStruct((1, LANES), f32)],
        compiler_params=_cparams(("arbitrary",)),
    )(dt_raw, bias, ddt)


SSM_GW = SSM_HPG * SSM_P
SSM_PAIRS = SSM_GW // LANES


def _ssd_common(dte_ref, dtt_ref, ae_ref, acol_ref):
    Q = CHUNK
    r = lax.broadcasted_iota(jnp.int32, (Q, Q), 0)
    c = lax.broadcasted_iota(jnp.int32, (Q, Q), 1)
    causal = r >= c
    anti = c >= r
    tril = causal.astype(f32)
    triu = anti.astype(f32)
    dt_e = dte_ref[...]
    cs_e = _dot_hi(tril, dt_e * ae_ref[...])
    cst = _dot_hi(dtt_ref[...] * acol_ref[...], triu)
    cs_last = cs_e[Q - 1:Q, :]
    return causal, anti, triu, dt_e, cs_e, cst, jnp.exp(cs_e), jnp.exp(cs_last - cs_e), jnp.exp(cs_last)


def _half_masks():
    lane = lax.broadcasted_iota(jnp.int32, (CHUNK, LANES), 1)
    lo = lane < SSM_P
    return lo, jnp.logical_not(lo)


def _ssd_fwd(xbc_c, dt_e, dtt, a_e, a_col):
    M = xbc_c.shape[0]
    Q = CHUNK
    nch = M // Q

    def body(x_ref, dte_ref, dtt_ref, ae_ref, acol_ref, y_ref, hin_ref, ht_sc):
        @pl.when(pl.program_id(0) == 0)
        def _():
            ht_sc[...] = jnp.zeros_like(ht_sc)

        causal, _, _, dt_e, cs_e, cst, ecs_e, dte_e, elast_e = _ssd_common(dte_ref, dtt_ref, ae_ref, acol_ref)
        halves = _half_masks()
        for g in range(SSM_GROUPS):
            g0 = g * SSM_GW
            bg = x_ref[:, D_SSM + g * SSM_N:D_SSM + (g + 1) * SSM_N]
            cg = x_ref[:, D_SSM + D_BC + g * SSM_N:D_SSM + D_BC + (g + 1) * SSM_N]
            bg_b = bg.astype(bf16)
            cg_b = cg.astype(bf16)
            cb = _dot_nt(cg_b, bg_b)
            bgt_b = bg.T.astype(bf16)
            xdt_g = x_ref[:, g0:g0 + SSM_GW] * dt_e[:, g0:g0 + SSM_GW]
            ht = ht_sc[g]
            hin_ref[0, g] = ht
            y_off = _dot(cg_b, ht.astype(bf16)) * ecs_e[:, g0:g0 + SSM_GW]
            for pr in range(SSM_PAIRS):
                p0 = pr * LANES
                xdt_p = xdt_g[:, p0:p0 + LANES]
                acc = y_off[:, p0:p0 + LANES]
                for half in range(2):
                    h = g * SSM_HPG + pr * 2 + half
                    seg = cs_e[:, h * SSM_P:h * SSM_P + 1] - cst[h:h + 1, :]
                    lm = jnp.exp(jnp.where(causal, seg, -jnp.inf))
                    xm = jnp.where(halves[half], xdt_p, 0.0).astype(bf16)
                    acc = acc + _dot((cb * lm).astype(bf16), xm)
                y_ref[:, g0 + p0:g0 + p0 + LANES] = acc
            st = _dot(bgt_b, (xdt_g * dte_e[:, g0:g0 + SSM_GW]).astype(bf16))
            ht_sc[g] = ht * elast_e[:, g0:g0 + SSM_GW] + st

    return pl.pallas_call(
        body, name="ssd_fwd", grid=(nch,),
        in_specs=[pl.BlockSpec((Q, D_XBC), lambda c: (c, 0)),
                  pl.BlockSpec((Q, D_SSM), lambda c: (c, 0)),
                  pl.BlockSpec((SSM_HEADS, Q), lambda c: (0, c)),
                  _full((1, D_SSM)), _full((SSM_HEADS, LANES))],
        out_specs=[pl.BlockSpec((Q, D_SSM), lambda c: (c, 0)),
                   pl.BlockSpec((1, SSM_GROUPS, SSM_N, SSM_GW), lambda c: (c, 0, 0, 0))],
        out_shape=[jax.ShapeDtypeStruct((M, D_SSM), f32),
                   jax.ShapeDtypeStruct((nch, SSM_GROUPS, SSM_N, SSM_GW), f32)],
        scratch_shapes=[pltpu.VMEM((SSM_GROUPS, SSM_N, SSM_GW), f32)],
        compiler_params=_cparams(("arbitrary",)),
    )(xbc_c, dt_e, dtt, a_e, a_col)


def _ssd_bwd(xbc_c, dtp, dt_e, dtt, a_row, a_e, a_col, hin, dy, d_exp, head_ind):
    M = xbc_c.shape[0]
    Q = CHUNK
    nch = M // Q
    rev = lambda c: nch - 1 - c

    def body(x_ref, dtp_ref, dte_ref, dtt_ref, arow_ref, ae_ref, acol_ref, hin_ref, dy_ref, dexp_ref,
             ind_ref, dx_ref, ddt_ref, da_ref, dht_sc, z_sc, z1_sc, last_sc, ct_sc):
        @pl.when(pl.program_id(0) == 0)
        def _():
            dht_sc[...] = jnp.zeros_like(dht_sc)
            da_ref[...] = jnp.zeros_like(da_ref)
            last_sc[...] = jnp.zeros_like(last_sc)
            ct_sc[...] = jnp.zeros_like(ct_sc)

        causal, anti, triu, dt_e, cs_e, cst, ecs_e, dte_e, elast_e = _ssd_common(dte_ref, dtt_ref, ae_ref, acol_ref)
        halves = _half_masks()
        lane = lax.broadcasted_iota(jnp.int32, (Q, LANES), 1)
        rsum = jnp.zeros((Q, LANES), f32)
        for g in range(SSM_GROUPS):
            g0 = g * SSM_GW
            gs = slice(g0, g0 + SSM_GW)
            b0 = D_SSM + g * SSM_N
            c0 = D_SSM + D_BC + g * SSM_N
            bg = x_ref[:, b0:b0 + SSM_N]
            cg = x_ref[:, c0:c0 + SSM_N]
            bg_b = bg.astype(bf16)
            cg_b = cg.astype(bf16)
            cgt_b = cg.T.astype(bf16)
            cbt = _dot_nt(bg_b, cg_b)
            cb = _dot_nt(cg_b, bg_b)
            x_g = x_ref[:, gs]
            dt_g = dt_e[:, gs]
            xdt_g = x_g * dt_g
            dy_g = dy_ref[:, gs]
            ht = hin_ref[0, g]
            ht_b = ht.astype(bf16)
            dht = dht_sc[g]
            dht_b = dht.astype(bf16)
            dye_b = (dy_g * ecs_e[:, gs]).astype(bf16)
            dc = _dot_nt(dye_b, ht_b)
            dht_new = dht * elast_e[:, gs] + _dot(cgt_b, dye_b)
            e = _dot(bg_b, dht_b)
            xdtd = xdt_g * dte_e[:, gs]
            db = _dot_nt(xdtd.astype(bf16), dht_b)
            dxdt_state = e * dte_e[:, gs]
            exd = e * xdtd
            z1_sc[:, gs] = dy_g * (_dot(cg_b, ht_b) * ecs_e[:, gs]) - exd
            last_sc[0:1, gs] = (jnp.sum(exd, axis=0, keepdims=True)
                                + jnp.sum(dht * ht, axis=0, keepdims=True) * elast_e[:, gs])
            dg_acc = jnp.zeros((Q, Q), f32)
            for pr in range(SSM_PAIRS):
                p0 = pr * LANES
                ps = slice(g0 + p0, g0 + p0 + LANES)
                dy_p = dy_g[:, p0:p0 + LANES]
                xdt_pb = xdt_g[:, p0:p0 + LANES].astype(bf16)
                acc = dxdt_state[:, p0:p0 + LANES]
                for half in range(2):
                    h = g * SSM_HPG + pr * 2 + half
                    seg = cs_e[:, h * SSM_P:h * SSM_P + 1] - cst[h:h + 1, :]
                    lm = jnp.exp(jnp.where(causal, seg, -jnp.inf))
                    lmt = jnp.exp(jnp.where(anti, -seg, -jnp.inf))
                    dym = jnp.where(halves[half], dy_p, 0.0).astype(bf16)
                    acc = acc + _dot((cbt * lmt).astype(bf16), dym)
                    dml = _dot_nt(dym, xdt_pb) * lm
                    dg_acc = dg_acc + dml
                    w = dml * cb
                    rsum = rsum + jnp.where(lane == h, jnp.sum(w, axis=1, keepdims=True), 0.0)
                    ct_sc[h:h + 1, :] = jnp.sum(w, axis=0, keepdims=True)
                dx_ref[:, ps] = acc * dt_g[:, p0:p0 + LANES] + dexp_ref[:, ps] * dy_p
                z_sc[:, ps] = acc * x_g[:, p0:p0 + LANES]
            dg_b = dg_acc.astype(bf16)
            dx_ref[:, c0:c0 + SSM_N] = dc + _dot(dg_b, bg_b)
            dx_ref[:, b0:b0 + SSM_N] = db + _dot_tn(dg_b, cg_b)
            dht_sc[g] = dht_new
        s1 = _dot_hi(z1_sc[...], ind_ref[...])
        s2 = _dot_hi(z_sc[...], ind_ref[...])
        last = _dot_hi(last_sc[...], ind_ref[...])[0:1, :]
        dtp = dtp_ref[...]
        row = lax.broadcasted_iota(jnp.int32, (Q, LANES), 0)
        dcs = s1 + rsum + jnp.where(row == Q - 1, last, 0.0)
        tril = causal.astype(f32)
        da = _dot_hi(triu, dcs) - _dot_hi(ct_sc[...], tril).T
        ddt_ref[...] = s2 + da * arow_ref[...]
        da_ref[...] += jnp.sum(da * dtp, axis=0, keepdims=True)

    return pl.pallas_call(
        body, name="ssd_bwd", grid=(nch,),
        in_specs=[pl.BlockSpec((Q, D_XBC), lambda c: (rev(c), 0)),
                  pl.BlockSpec((Q, LANES), lambda c: (rev(c), 0)),
                  pl.BlockSpec((Q, D_SSM), lambda c: (rev(c), 0)),
                  pl.BlockSpec((SSM_HEADS, Q), lambda c: (0, rev(c))),
                  _full((1, LANES)), _full((1, D_SSM)), _full((SSM_HEADS, LANES)),
                  pl.BlockSpec((1, SSM_GROUPS, SSM_N, SSM_GW), lambda c: (rev(c), 0, 0, 0)),
                  pl.BlockSpec((Q, D_SSM), lambda c: (rev(c), 0)),
                  _full((1, D_SSM)), _full((D_SSM, LANES))],
        out_specs=[pl.BlockSpec((Q, D_XBC), lambda c: (rev(c), 0)),
                   pl.BlockSpec((Q, LANES), lambda c: (rev(c), 0)),
                   _full((1, LANES))],
        out_shape=[jax.ShapeDtypeStruct((M, D_XBC), f32), jax.ShapeDtypeStruct((M, LANES), f32),
                   jax.ShapeDtypeStruct((1, LANES), f32)],
        scratch_shapes=[pltpu.VMEM((SSM_GROUPS, SSM_N, SSM_GW), f32), pltpu.VMEM((Q, D_SSM), f32),
                        pltpu.VMEM((Q, D_SSM), f32), pltpu.VMEM((SUBLANES, D_SSM), f32),
                        pltpu.VMEM((LANES, Q), f32)],
        compiler_params=_cparams(("arbitrary",)),
    )(xbc_c, dtp, dt_e, dtt, a_row, a_e, a_col, hin, dy, d_exp, head_ind)


def _gate_norm_fwd(y, xbc_c, z, d_exp, g):
    M = y.shape[0]
    tm = _rt(M)
    gw = D_SSM // SSM_GROUPS

    def body(y_ref, x_ref, z_ref, d_ref, g_ref, o_ref):
        yg = (y_ref[...] + d_ref[...] * x_ref[...]) * _silu(z_ref[...])
        for gi in range(SSM_GROUPS):
            blk = yg[:, gi * gw:(gi + 1) * gw]
            o_ref[:, gi * gw:(gi + 1) * gw] = (blk * _rstd(blk) * g_ref[:, gi * gw:(gi + 1) * gw]).astype(bf16)

    return pl.pallas_call(
        body, name="gate_norm_fwd", grid=(M // tm,),
        in_specs=[_row(D_SSM, tm), _row(D_SSM, tm), _row(D_SSM, tm), _full((1, D_SSM)), _full((1, D_SSM))],
        out_specs=_row(D_SSM, tm), out_shape=jax.ShapeDtypeStruct((M, D_SSM), bf16),
        compiler_params=_cparams(("parallel",)),
    )(y, xbc_c, z, d_exp, g)


def _gate_norm_bwd(y, xbc_c, z, d_exp, g, dout, head_ind):
    M = y.shape[0]
    tm = _rt(M)
    nt = M // tm
    gw = D_SSM // SSM_GROUPS

    def body(y_ref, x_ref, z_ref, d_ref, g_ref, do_ref, ind_ref, dy_ref, dz_ref, dg_ref, dd_ref, ddc_sc):
        i = pl.program_id(0)

        @pl.when(i == 0)
        def _():
            dg_ref[...] = jnp.zeros_like(dg_ref)
            ddc_sc[...] = jnp.zeros_like(ddc_sc)

        zv = z_ref[...]
        xv = x_ref[...]
        s = _silu(zv)
        yd = y_ref[...] + d_ref[...] * xv
        yg = yd * s
        dov = do_ref[...]
        for gi in range(SSM_GROUPS):
            sl = slice(gi * gw, (gi + 1) * gw)
            dyg, dgp = _rms_bwd_math(yg[:, sl], g_ref[:, sl], dov[:, sl])
            dg_ref[:, sl] += jnp.sum(dgp, axis=0, keepdims=True)
            dyd = dyg * s[:, sl]
            dy_ref[:, sl] = dyd
            dz_ref[:, sl] = (dyg * yd[:, sl] * _dsilu(zv[:, sl])).astype(bf16)
            ddc_sc[:, sl] += jnp.sum(dyd * xv[:, sl], axis=0, keepdims=True)

        @pl.when(i == nt - 1)
        def _():
            dd_ref[...] = _dot_hi(ddc_sc[...], ind_ref[...])

    return pl.pallas_call(
        body, name="gate_norm_bwd", grid=(nt,),
        in_specs=[_row(D_SSM, tm), _row(D_SSM, tm), _row(D_SSM, tm), _full((1, D_SSM)), _full((1, D_SSM)),
                  _row(D_SSM, tm), _full((D_SSM, LANES))],
        out_specs=[_row(D_SSM, tm), _row(D_SSM, tm), _full((1, D_SSM)), _full((1, LANES))],
        out_shape=[jax.ShapeDtypeStruct((M, D_SSM), f32), jax.ShapeDtypeStruct((M, D_SSM), bf16),
                   jax.ShapeDtypeStruct((1, D_SSM), f32), jax.ShapeDtypeStruct((1, LANES), f32)],
        scratch_shapes=[pltpu.VMEM((1, D_SSM), f32)],
        compiler_params=_cparams(("arbitrary",)),
    )(y, xbc_c, z, d_exp, g, dout, head_ind)


_PEER_FLIPS = [(0, 0, 1), (0, 1, 0), (0, 1, 1), (1, 0, 0), (1, 0, 1), (1, 1, 0), (1, 1, 1)]


def _exchange_copies(ins, outs, send_sems, recv_sems, loc_sems, scatter):
    n = len(ins)
    x, y, c = lax.axis_index("x"), lax.axis_index("y"), lax.axis_index("c")
    me = 4 * x + 2 * y + c
    copies = []
    for a in range(n):
        src = ins[a].at[me] if scatter else ins[a]
        copies.append(pltpu.make_async_copy(src, outs[a].at[me], loc_sems.at[a]))
    for p, (fx, fy, fc) in enumerate(_PEER_FLIPS):
        tx = 1 - x if fx else x
        ty = 1 - y if fy else y
        tc = 1 - c if fc else c
        tgt = 4 * tx + 2 * ty + tc
        for a in range(n):
            src = ins[a].at[tgt] if scatter else ins[a]
            copies.append(pltpu.make_async_remote_copy(
                src_ref=src, dst_ref=outs[a].at[me],
                send_sem=send_sems.at[p * n + a], recv_sem=recv_sems.at[p * n + a],
                device_id=(tx, ty, tc), device_id_type=_MESH))
    return copies


def _exchange_shapes(arrays, scatter):
    return [jax.ShapeDtypeStruct(a.shape if scatter else (N_DEV,) + a.shape, a.dtype) for a in arrays]


def _exchange_sems(n):
    return [pltpu.SemaphoreType.DMA((7 * n,)), pltpu.SemaphoreType.DMA((7 * n,)), pltpu.SemaphoreType.DMA((n,))]


def _exchange(arrays, scatter, name):
    n = len(arrays)

    def body(*refs):
        copies = _exchange_copies(refs[:n], refs[n:2 * n], *refs[2 * n:], scatter)
        for cp in copies:
            cp.start()
        for cp in copies:
            cp.wait()

    any_spec = pl.BlockSpec(memory_space=pl.ANY)
    return pl.pallas_call(
        body, name=name, in_specs=[any_spec] * n, out_specs=[any_spec] * n,
        out_shape=_exchange_shapes(arrays, scatter), scratch_shapes=_exchange_sems(n),
    )(*arrays)


def _adamw(parts, w, m, v, name):
    R, C = w.shape
    tr = _pick(R, (PACK_ROW_TILE, 64, 32, 16, 8))
    c1 = 1.0 - ADAM_B1 ** ADAM_STEP
    c2 = 1.0 - ADAM_B2 ** ADAM_STEP

    def body(p_ref, w_ref, m_ref, v_ref, g_ref, d_ref, nm_ref, nv_ref):
        g = p_ref[0].astype(f32)
        for s in range(1, N_DEV):
            g = g + p_ref[s].astype(f32)
        mn = ADAM_B1 * m_ref[...] + (1.0 - ADAM_B1) * g
        vn = ADAM_B2 * v_ref[...] + (1.0 - ADAM_B2) * (g * g)
        m_hat = mn / c1
        v_hat = vn / c2
        g_ref[...] = g
        d_ref[...] = -ADAM_LR * (m_hat / (jnp.sqrt(v_hat) + ADAM_EPS) + ADAM_WD * w_ref[...])
        nm_ref[...] = mn
        nv_ref[...] = vn

    spec = pl.BlockSpec((tr, C), lambda i: (i, 0))
    return pl.pallas_call(
        body, name=name, grid=(R // tr,),
        in_specs=[pl.BlockSpec((N_DEV, tr, C), lambda i: (0, i, 0)), spec, spec, spec],
        out_specs=[spec] * 4, out_shape=[jax.ShapeDtypeStruct((R, C), f32)] * 4,
        compiler_params=_cparams(("parallel",)),
    )(parts, w, m, v)


def _flat_rows(a, lead_ndim):
    lead = a.shape[:lead_ndim]
    n = int(np.prod(a.shape[lead_ndim:]))
    a = a.reshape(lead + (n,))
    pad = (-n) % PACK_W
    if pad:
        a = jnp.pad(a, [(0, 0)] * lead_ndim + [(0, pad)])
    return a.reshape(lead + ((n + pad) // PACK_W, PACK_W))


def _pack(arrays, lead_ndim, total_rows, dtype):
    rows = [_flat_rows(a.astype(dtype), lead_ndim) for a in arrays]
    cat = jnp.concatenate(rows, axis=lead_ndim)
    pad = total_rows - cat.shape[lead_ndim]
    if pad:
        cat = jnp.pad(cat, [(0, 0)] * lead_ndim + [(0, pad), (0, 0)])
    return cat


def _unpack(buf, shapes, lead_ndim):
    out = []
    r = 0
    lead = buf.shape[:lead_ndim]
    for shp in shapes:
        n = int(np.prod(shp))
        nr = -(-n // PACK_W)
        piece = lax.slice_in_dim(buf, r, r + nr, axis=lead_ndim)
        piece = piece.reshape(lead + (nr * PACK_W,))
        piece = lax.slice_in_dim(piece, 0, n, axis=lead_ndim)
        out.append(piece.reshape(lead + tuple(shp)))
        r += nr
    return out


def _round_up(n, m):
    return -(-n // m) * m


def kernel(x, meta_tokens, norm_mix_pre, norm_mix_post, norm_ffn_pre, norm_ffn_post, w_in, q_a_norm, w_uq, kv_a_norm, w_ukv, attn_out_norm, ssm_conv_w, ssm_conv_b, ssm_dt_bias, ssm_A_log, ssm_D, ssm_norm, w_out, w_up, ffn_conv_w, ffn_conv_b, w_down, loss_target, m_meta_tokens, m_norm_mix_pre, m_norm_mix_post, m_norm_ffn_pre, m_norm_ffn_post, m_w_in, m_q_a_norm, m_w_uq, m_kv_a_norm, m_w_ukv, m_attn_out_norm, m_ssm_conv_w, m_ssm_conv_b, m_ssm_dt_bias, m_ssm_A_log, m_ssm_D, m_ssm_norm, m_w_out, m_w_up, m_ffn_conv_w, m_ffn_conv_b, m_w_down, v_meta_tokens, v_norm_mix_pre, v_norm_mix_post, v_norm_ffn_pre, v_norm_ffn_post, v_w_in, v_q_a_norm, v_w_uq, v_kv_a_norm, v_w_ukv, v_attn_out_norm, v_ssm_conv_w, v_ssm_conv_b, v_ssm_dt_bias, v_ssm_A_log, v_ssm_D, v_ssm_norm, v_w_out, v_w_up, v_ffn_conv_w, v_ffn_conv_b, v_w_down):
    seq = x.shape[1]
    n_real = N_META + seq
    Lp = _round_up(n_real, 768) if n_real > 2048 else _round_up(n_real, ROW_TILE)
    D = D_MODEL

    early_w = [w_in, w_uq, w_ukv]
    late_w = [w_out, w_up, w_down]
    sharded_s = [meta_tokens, ssm_conv_w, ffn_conv_w]
    grp_a = dict(names=["w_out", "w_up", "w_down"], w=late_w, m=[m_w_out, m_w_up, m_w_down],
                 v=[v_w_out, v_w_up, v_w_down])
    grp_b = dict(names=["w_in", "w_uq", "w_ukv", "meta_tokens", "ssm_conv_w", "ffn_conv_w"],
                 w=early_w + sharded_s,
                 m=[m_w_in, m_w_uq, m_w_ukv, m_meta_tokens, m_ssm_conv_w, m_ffn_conv_w],
                 v=[v_w_in, v_w_uq, v_w_ukv, v_meta_tokens, v_ssm_conv_w, v_ffn_conv_w])
    repl_w = [norm_mix_pre, norm_mix_post, norm_ffn_pre, norm_ffn_post, q_a_norm, kv_a_norm, attn_out_norm,
              ssm_conv_b, ssm_dt_bias, ssm_A_log, ssm_D, ssm_norm, ffn_conv_b]
    repl_m = [m_norm_mix_pre, m_norm_mix_post, m_norm_ffn_pre, m_norm_ffn_post, m_q_a_norm, m_kv_a_norm,
              m_attn_out_norm, m_ssm_conv_b, m_ssm_dt_bias, m_ssm_A_log, m_ssm_D, m_ssm_norm, m_ffn_conv_b]
    repl_v = [v_norm_mix_pre, v_norm_mix_post, v_norm_ffn_pre, v_norm_ffn_post, v_q_a_norm, v_kv_a_norm,
              v_attn_out_norm, v_ssm_conv_b, v_ssm_dt_bias, v_ssm_A_log, v_ssm_D, v_ssm_norm, v_ffn_conv_b]

    def pack_rows(arrs, lead):
        return _round_up(sum(-(-int(np.prod(a.shape[lead:])) // PACK_W) for a in arrs), 16)

    wb = _pack(early_w, 0, pack_rows(early_w, 0), bf16)
    wl = _pack(late_w, 0, pack_rows(late_w, 0), bf16)
    ws = _pack(sharded_s, 0, pack_rows(sharded_s, 0), f32)
    wb_all, ws_all = _exchange([wb, ws], False, "gather_weights")
    g_w_in, g_w_uq, g_w_ukv = _unpack(wb_all, [a.shape for a in early_w], 1)
    g_meta, g_sconv, g_fconv = _unpack(ws_all, [a.shape for a in sharded_s], 1)

    def cols(gathered):
        t = gathered[:, 0]
        return jnp.transpose(t, (1, 0, 2)).reshape(t.shape[1], N_DEV * t.shape[2])

    win = cols(g_w_in)
    o = np.cumsum((0, Q_RANK, KV_RANK, QK_ROPE, D_SSM, D_XBC, SSM_HEADS))
    w_q, w_kv = win[:, o[0]:o[1]], win[:, o[1]:o[2]]
    w_rope = jnp.pad(win[:, o[2]:o[3]], ((0, 0), (0, LANES - QK_ROPE)))
    w_z, w_xbc = win[:, o[3]:o[4]], win[:, o[4]:o[5]]
    w_dt = jnp.pad(win[:, o[5]:o[6]], ((0, 0), (0, LANES - SSM_HEADS)))
    wuq = g_w_uq.reshape(Q_RANK, MLA_HEADS, QK_NOPE + QK_ROPE)
    wuq = jnp.pad(wuq, ((0, 0), (0, 0), (0, QK_PAD - QK_NOPE - QK_ROPE))).reshape(Q_RANK, MLA_HEADS * QK_PAD)
    wukv = g_w_ukv.reshape(KV_RANK, MLA_HEADS * (QK_NOPE + V_DIM))
    meta_full = jnp.transpose(g_meta, (1, 0, 2)).reshape(N_META, D)
    sconv_w = jnp.pad(cols(g_sconv), ((0, SUBLANES - SSM_CONV), (0, 0)))
    fconv_w = jnp.pad(cols(g_fconv), ((0, SUBLANES - FFN_CONV), (0, 0)))

    pos = jnp.arange(Lp, dtype=f32)
    inv = ROPE_THETA ** (-jnp.arange(0, QK_ROPE, 2, dtype=f32) / QK_ROPE)
    ang = pos[:, None] * inv[None, :]
    cs_, sn_ = jnp.cos(ang), jnp.sin(ang)
    zpad = jnp.zeros((Lp, LANES - QK_ROPE), f32)
    cos_t = jnp.concatenate([cs_, cs_, zpad], axis=1)
    sin_t = jnp.concatenate([-sn_, sn_, zpad], axis=1)
    dt_bias_p = jnp.pad(ssm_dt_bias, ((0, 0), (0, LANES - SSM_HEADS)))
    a_neg = -jnp.exp(ssm_A_log)
    a_row = jnp.pad(a_neg, ((0, 0), (0, LANES - SSM_HEADS)))
    a_col = jnp.broadcast_to(a_neg.reshape(SSM_HEADS, 1), (SSM_HEADS, LANES))
    d_exp = jnp.repeat(ssm_D, SSM_P, axis=1)
    a_e = jnp.repeat(a_neg, SSM_P, axis=1)
    head_ind = (jnp.arange(D_SSM)[:, None] // SSM_P == jnp.arange(LANES)[None, :]).astype(f32)

    xb = x[0]
    h0 = jnp.concatenate([meta_full, xb, jnp.zeros((Lp - n_real, D), f32)], axis=0)
    tgt = jnp.pad(loss_target[0], ((N_META, Lp - n_real), (0, 0)))
    hn1 = _rms_fwd(h0, norm_mix_pre, bf16, "norm_mix_pre")
    q_c = _mm([(hn1, w_q)], f32, False, "proj_q")
    kv_c = _mm([(hn1, w_kv)], f32, False, "proj_kv")
    kpe_raw = _mm([(hn1, w_rope)], f32, False, "proj_rope")
    z = _mm([(hn1, w_z)], f32, False, "proj_z")
    xbc = _mm([(hn1, w_xbc)], f32, False, "proj_xbc")
    dt_raw = _mm([(hn1, w_dt)], f32, False, "proj_dt")

    qn = _rms_fwd(q_c, q_a_norm, bf16, "norm_q")
    kvn = _rms_fwd(kv_c, kv_a_norm, bf16, "norm_kv")
    qh = _up_q_rope(qn, wuq, cos_t, sin_t)
    kh, vh = _up_kv_rope(kvn, wukv, kpe_raw, cos_t, sin_t)
    attn, lse, wl_all = _flash_fwd(qh, kh, vh, [wl], False)
    g_w_out, g_w_up, g_w_down = _unpack(wl_all, [a.shape for a in late_w], 1)
    wout = g_w_out.reshape(D_ATTN + D_SSM, D)
    wout_a, wout_s = wout[:D_ATTN], wout[D_ATTN:]
    wup = cols(g_w_up)
    wdown = g_w_down.reshape(D_FF, D)
    an = _rms_fwd(attn, attn_out_norm, bf16, "norm_attn_out")

    xbc_c = _ssm_conv_fwd(xbc, sconv_w, ssm_conv_b)
    dtp, dt_e = _dt_fwd(dt_raw, dt_bias_p, jnp.transpose(head_ind))
    dtt = jnp.transpose(dtp[:, :SSM_HEADS])
    y_ssd, hin = _ssd_fwd(xbc_c, dt_e, dtt, a_e, a_col)
    ssm = _gate_norm_fwd(y_ssd, xbc_c, z, d_exp, ssm_norm)

    mix = _mm([(an, wout_a), (ssm, wout_s)], f32, False, "out_proj")
    h1, hn2 = _resid_norm(h0, mix, norm_mix_post, norm_ffn_pre)
    up = _mm([(hn2, wup)], f32, False, "ffn_up")
    act = _ffn_gate_fwd(up, fconv_w, ffn_conv_b)
    down = _mm([(act, wdown)], f32, False, "ffn_down")
    dh2, d_down, dg_ffn_post, loss_part = _final(h1, down, norm_ffn_post, tgt, n_real)

    d_act = _mm([(d_down, wdown)], f32, True, "ffn_down_dx")
    dw_down = _mm_tn(act, d_down, "ffn_down_dw")
    dup_g, dup_v, dwc_g, dwc_v, dbc_g, dbc_v = _ffn_gate_bwd(up, fconv_w, ffn_conv_b, d_act)
    d_hn2 = _mm([(dup_g, wup[:, :D_FF]), (dup_v, wup[:, D_FF:])], f32, True, "ffn_up_dx")
    dw_up = jnp.concatenate([_mm_tn(hn2, dup_g, "ffn_up_dw_g"), _mm_tn(hn2, dup_v, "ffn_up_dw_v")], axis=1)
    dh1, d_mix, dg_ffn_pre, dg_mix_post = _mid_bwd(h1, norm_ffn_pre, d_hn2, dh2, mix, norm_mix_post)
    d_an = _mm([(d_mix, wout_a)], f32, True, "out_proj_dx_a")
    d_ssm = _mm([(d_mix, wout_s)], f32, True, "out_proj_dx_s")
    dw_out = jnp.concatenate([_mm_tn(an, d_mix, "out_proj_dw_a"), _mm_tn(ssm, d_mix, "out_proj_dw_s")], axis=0)

    d_attn, dg_attn_out = _rms_bwd(attn, attn_out_norm, d_an, f32, "norm_attn_out_bwd")
    do_h, delta = _attn_delta(attn, d_attn)
    def col_blocks(gm):
        r, cc = gm.shape
        return jnp.transpose(gm.reshape(r, N_DEV, cc // N_DEV), (1, 0, 2))

    blocks_a = [dw_out.reshape(N_DEV, (D_ATTN + D_SSM) // N_DEV, D), col_blocks(dw_up),
                dw_down.reshape(N_DEV, D_FF // N_DEV, D)]
    gpack_a = _pack(blocks_a, 1, pack_rows(blocks_a, 1), bf16)
    dqh, dkh, dvh, gparts_a = _flash_bwd(qh, kh, vh, do_h, lse, delta, [gpack_a], True)
    dq_full = _rope_q_bwd(dqh, cos_t, sin_t)
    dkv_full, d_kpe_raw = _rope_k_bwd(dkh, dvh, cos_t, sin_t)
    d_qn = _mm([(dq_full, wuq)], f32, True, "up_q_dx")
    dw_uq = _mm_tn(qn, dq_full, "up_q_dw")
    d_kvn = _mm([(dkv_full, wukv)], f32, True, "up_kv_dx")
    dw_ukv = _mm_tn(kvn, dkv_full, "up_kv_dw")
    d_q_c, dg_q = _rms_bwd(q_c, q_a_norm, d_qn, bf16, "norm_q_bwd")
    d_kv_c, dg_kv = _rms_bwd(kv_c, kv_a_norm, d_kvn, bf16, "norm_kv_bwd")

    dy_ssd, dz, dg_ssm, dd_heads = _gate_norm_bwd(y_ssd, xbc_c, z, d_exp, ssm_norm, d_ssm, head_ind)
    d_xbc_c, ddt, da_heads = _ssd_bwd(xbc_c, dtp, dt_e, dtt, a_row, a_e, a_col, hin, dy_ssd, d_exp, head_ind)
    d_xbc, dw_sconv, db_sconv = _ssm_conv_bwd(xbc, sconv_w, ssm_conv_b, d_xbc_c)
    d_dt_raw, d_dt_bias = _dt_bwd(dt_raw, dt_bias_p, ddt)

    segs = [(d_q_c, w_q), (d_kv_c, w_kv), (d_kpe_raw, w_rope), (dz, w_z), (d_xbc, w_xbc), (d_dt_raw, w_dt)]
    d_hn1 = _mm(segs, f32, True, "proj_dx")
    dw_q = _mm_tn(hn1, d_q_c, "proj_dw_q")
    dw_kv = _mm_tn(hn1, d_kv_c, "proj_dw_kv")
    dw_rope = _mm_tn(hn1, d_kpe_raw, "proj_dw_rope")
    dw_z = _mm_tn(hn1, dz, "proj_dw_z")
    dw_xbc = _mm_tn(hn1, d_xbc, "proj_dw_xbc")
    dw_dt = _mm_tn(hn1, d_dt_raw, "proj_dw_dt")
    dh0, dg_mix_pre = _rms_bwd(h0, norm_mix_pre, d_hn1, f32, "norm_mix_pre_bwd", residual=dh1)

    grad_x = dh0[N_META:n_real][None]
    d_meta = dh0[:N_META]

    dw_in = jnp.concatenate([dw_q, dw_kv, dw_rope[:, :QK_ROPE], dw_z, dw_xbc, dw_dt[:, :SSM_HEADS]], axis=1)

    dw_uq3 = dw_uq.reshape(Q_RANK, MLA_HEADS, QK_PAD)[:, :, :QK_NOPE + QK_ROPE]
    blocks_b = [
        col_blocks(dw_in),
        dw_uq3.reshape(N_DEV, Q_RANK // N_DEV, MLA_HEADS, QK_NOPE + QK_ROPE),
        dw_ukv.reshape(N_DEV, KV_RANK // N_DEV, MLA_HEADS, QK_NOPE + V_DIM),
        col_blocks(d_meta),
        col_blocks(dw_sconv[:SSM_CONV]),
        col_blocks(jnp.concatenate([dwc_g, dwc_v], axis=1)[:FFN_CONV]),
    ]
    gpack_b = _pack(blocks_b, 1, pack_rows(blocks_b, 1), bf16)
    (gparts_b,) = _exchange([gpack_b], True, "scatter_grads")

    def adam_group(parts, grp, name):
        rows = parts.shape[1]
        packs = [_pack([a[None] for a in grp[k]], 1, rows, f32)[0] for k in ("w", "m", "v")]
        outs = _adamw(parts, *packs, name)
        shapes = [a.shape for a in grp["w"]]
        return [dict(zip(grp["names"], [t[0] for t in _unpack(b[None], shapes, 1)])) for b in outs]

    sh_a = adam_group(gparts_a, grp_a, "adamw_sharded_a")
    sh_b = adam_group(gparts_b, grp_b, "adamw_sharded_b")

    dg_alog = da_heads[:, :SSM_HEADS] * a_neg
    repl_g = [dg_mix_pre, dg_mix_post, dg_ffn_pre, dg_ffn_post, dg_q, dg_kv, dg_attn_out, db_sconv,
              d_dt_bias[:, :SSM_HEADS], dg_alog, dd_heads[:, :SSM_HEADS], dg_ssm,
              jnp.concatenate([dbc_g, dbc_v], axis=1)]
    loss_vec = loss_part[:, :1]
    small_total = _round_up(sum(-(-int(np.prod(a.shape)) // PACK_W) for a in repl_g) + 1, 16)
    spack = _pack(repl_g + [loss_vec], 0, small_total, f32)
    (sparts,) = _exchange([spack], False, "gather_small_grads")
    zero1 = jnp.zeros((1, 1), f32)
    rw = _pack(repl_w + [zero1], 0, small_total, f32)
    rm = _pack(repl_m + [zero1], 0, small_total, f32)
    rv = _pack(repl_v + [zero1], 0, small_total, f32)
    outs_small = _adamw(sparts, rw, rm, rv, "adamw_replicated")
    repl_shapes = [a.shape for a in repl_w] + [(1, 1)]
    g_rp, d_rp, m_rp, v_rp = [_unpack(b, repl_shapes, 0) for b in outs_small]
    loss = g_rp[-1][0, 0]

    order = ["meta_tokens", "norm_mix_pre", "norm_mix_post", "norm_ffn_pre", "norm_ffn_post", "w_in", "q_a_norm",
             "w_uq", "kv_a_norm", "w_ukv", "attn_out_norm", "ssm_conv_w", "ssm_conv_b", "ssm_dt_bias", "ssm_A_log",
             "ssm_D", "ssm_norm", "w_out", "w_up", "ffn_conv_w", "ffn_conv_b", "w_down"]
    rp_names = ["norm_mix_pre", "norm_mix_post", "norm_ffn_pre", "norm_ffn_post", "q_a_norm", "kv_a_norm",
                "attn_out_norm", "ssm_conv_b", "ssm_dt_bias", "ssm_A_log", "ssm_D", "ssm_norm", "ffn_conv_b"]

    def lookup(k, rp_list):
        d = {**sh_a[k], **sh_b[k], **dict(zip(rp_names, rp_list))}
        return [d[n] for n in order]

    return (loss, grad_x, *lookup(0, g_rp), *lookup(1, d_rp), *lookup(2, m_rp), *lookup(3, v_rp))
```

```python
import functools
import math

import jax
import jax.numpy as jnp
import numpy as np
from jax import lax
from jax.experimental import pallas as pl
from jax.experimental.pallas import tpu as pltpu

f32 = jnp.float32
bf16 = jnp.bfloat16

D_MODEL = 1024
SEQ = 8192
N_META = 16
MLA_HEADS = 8
QK_NOPE = 128
QK_ROPE = 64
V_DIM = 128
Q_RANK = 384
KV_RANK = 256
ROPE_THETA = 10000.0
SOFTMAX_SCALE = (QK_NOPE + QK_ROPE) ** -0.5
D_ATTN = MLA_HEADS * V_DIM
SSM_HEADS = 16
SSM_P = 64
SSM_GROUPS = 2
SSM_HPG = SSM_HEADS // SSM_GROUPS
SSM_N = 128
SSM_CONV = 4
CHUNK = 128
D_SSM = SSM_HEADS * SSM_P
D_BC = SSM_GROUPS * SSM_N
D_XBC = D_SSM + 2 * D_BC
D_FF = 2816
FFN_CONV = 3
EPS = 1e-6
D_IN = Q_RANK + KV_RANK + QK_ROPE + D_SSM + D_XBC + SSM_HEADS
QK_PAD = 256
N_DEV = 8

ADAM_LR = 0.001
ADAM_B1 = 0.9
ADAM_B2 = 0.999
ADAM_EPS = 1e-08
ADAM_WD = 0.01
ADAM_STEP = 10

LANES = 128
SUBLANES = 8
ROW_TILE = 256
VMEM_LIMIT = 56 * 1024 * 1024
PACK_W = 1024
PACK_ROW_TILE = 128
NEG = -1e30
LOG2E = math.log2(math.e)
LN2 = math.log(2.0)
Q_PRESCALE = SOFTMAX_SCALE * LOG2E

_MESH = pl.DeviceIdType.MESH


def _pick(n, prefs):
    for p in prefs:
        if n % p == 0:
            return p
    return n


def _rt(m):
    return _pick(m, (384, ROW_TILE))


def _cparams(sem):
    return pltpu.CompilerParams(dimension_semantics=sem, vmem_limit_bytes=VMEM_LIMIT)


def _row(spec_cols, tm):
    return pl.BlockSpec((tm, spec_cols), lambda i: (i, 0))


def _full(shape):
    nd = len(shape)
    return pl.BlockSpec(shape, lambda *a: (0,) * nd)


def _sigmoid(x):
    return 1.0 / (1.0 + jnp.exp(-x))


def _silu(x):
    return x * _sigmoid(x)


def _dsilu(x):
    s = _sigmoid(x)
    return s * (1.0 + x * (1.0 - s))


def _dot(a, b):
    return jnp.dot(a, b, preferred_element_type=f32)


def _dot_nt(a, b):
    return lax.dot_general(a, b, (((1,), (1,)), ((), ())), preferred_element_type=f32)


def _dot_tn(a, b):
    return lax.dot_general(a, b, (((0,), (0,)), ((), ())), preferred_element_type=f32)


def _dot_hi(a, b):
    return jnp.dot(a, b, precision=lax.Precision.HIGHEST, preferred_element_type=f32)


def _mm(pairs, out_dtype, trans_b, name):
    n = len(pairs)
    M = pairs[0][0].shape[0]
    N = pairs[0][1].shape[0] if trans_b else pairs[0][1].shape[1]
    tm = _pick(M, (768, 512, 256))
    tn = _pick(N, (512, 1408, 384, 256, 128))

    def body(*refs):
        o_ref = refs[2 * n]
        acc = None
        for p in range(n):
            a = refs[2 * p][...].astype(bf16)
            b = refs[2 * p + 1][...].astype(bf16)
            r = _dot_nt(a, b) if trans_b else _dot(a, b)
            acc = r if acc is None else acc + r
        o_ref[...] = acc.astype(out_dtype)

    in_specs, args = [], []
    for a, b in pairs:
        k = a.shape[1]
        in_specs.append(pl.BlockSpec((tm, k), lambda i, j: (i, 0)))
        if trans_b:
            in_specs.append(pl.BlockSpec((tn, k), lambda i, j: (j, 0)))
        else:
            in_specs.append(pl.BlockSpec((k, tn), lambda i, j: (0, j)))
        args += [a, b]
    return pl.pallas_call(
        body, name=name, grid=(M // tm, N // tn), in_specs=in_specs,
        out_specs=pl.BlockSpec((tm, tn), lambda i, j: (i, j)),
        out_shape=jax.ShapeDtypeStruct((M, N), out_dtype),
        compiler_params=_cparams(("parallel", "parallel")),
    )(*args)


def _mm_tn(a, g, name):
    M, K = a.shape
    N = g.shape[1]
    tm = _pick(M, (768, 512, 256))
    tk = _pick(K, (1024, 1408, 512, 384, 256))
    tn = _pick(N, (1024, 1408, 512, 384, 256, 128))

    def body(a_ref, g_ref, o_ref):
        @pl.when(pl.program_id(2) == 0)
        def _():
            o_ref[...] = jnp.zeros_like(o_ref)

        o_ref[...] += _dot_tn(a_ref[...].astype(bf16), g_ref[...].astype(bf16))

    return pl.pallas_call(
        body, name=name, grid=(K // tk, N // tn, M // tm),
        in_specs=[pl.BlockSpec((tm, tk), lambda k, j, m: (m, k)),
                  pl.BlockSpec((tm, tn), lambda k, j, m: (m, j))],
        out_specs=pl.BlockSpec((tk, tn), lambda k, j, m: (k, j)),
        out_shape=jax.ShapeDtypeStruct((K, N), f32),
        compiler_params=_cparams(("parallel", "parallel", "arbitrary")),
    )(a, g)


def _rstd(x):
    return lax.rsqrt(jnp.mean(x * x, axis=-1, keepdims=True) + EPS)


def _rms_bwd_math(x, g, dy):
    r = _rstd(x)
    xh = x * r
    dn = dy * g
    dx = r * (dn - xh * jnp.mean(dn * xh, axis=-1, keepdims=True))
    return dx, dy * xh


def _rms_fwd(x, g, out_dtype, name):
    M, K = x.shape
    tm = _rt(M)

    def body(x_ref, g_ref, o_ref):
        xv = x_ref[...]
        o_ref[...] = (xv * _rstd(xv) * g_ref[...]).astype(out_dtype)

    return pl.pallas_call(
        body, name=name, grid=(M // tm,), in_specs=[_row(K, tm), _full((1, K))],
        out_specs=_row(K, tm), out_shape=jax.ShapeDtypeStruct((M, K), out_dtype),
        compiler_params=_cparams(("parallel",)),
    )(x, g)


def _rms_bwd(x, g, dy, out_dtype, name, residual=None):
    M, K = x.shape
    tm = _rt(M)
    has_res = residual is not None

    def body(*refs):
        if has_res:
            x_ref, g_ref, dy_ref, r_ref, dx_ref, dg_ref = refs
        else:
            x_ref, g_ref, dy_ref, dx_ref, dg_ref = refs

        @pl.when(pl.program_id(0) == 0)
        def _():
            dg_ref[...] = jnp.zeros_like(dg_ref)

        dx, dgp = _rms_bwd_math(x_ref[...], g_ref[...], dy_ref[...].astype(f32))
        if has_res:
            dx = dx + r_ref[...]
        dx_ref[...] = dx.astype(out_dtype)
        dg_ref[...] += jnp.sum(dgp, axis=0, keepdims=True)

    ins = [x, g, dy] + ([residual] if has_res else [])
    in_specs = [_row(K, tm), _full((1, K)), _row(K, tm)] + ([_row(K, tm)] if has_res else [])
    return pl.pallas_call(
        body, name=name, grid=(M // tm,), in_specs=in_specs,
        out_specs=[_row(K, tm), _full((1, K))],
        out_shape=[jax.ShapeDtypeStruct((M, K), out_dtype), jax.ShapeDtypeStruct((1, K), f32)],
        compiler_params=_cparams(("arbitrary",)),
    )(*ins)


def _resid_norm(h0, mix, g2, g3):
    M, K = h0.shape
    tm = _rt(M)

    def body(h_ref, m_ref, g2_ref, g3_ref, h1_ref, hn_ref):
        mv = m_ref[...]
        h1 = h_ref[...] + mv * _rstd(mv) * g2_ref[...]
        h1_ref[...] = h1
        hn_ref[...] = (h1 * _rstd(h1) * g3_ref[...]).astype(bf16)

    return pl.pallas_call(
        body, name="resid_norm", grid=(M // tm,),
        in_specs=[_row(K, tm), _row(K, tm), _full((1, K)), _full((1, K))],
        out_specs=[_row(K, tm), _row(K, tm)],
        out_shape=[jax.ShapeDtypeStruct((M, K), f32), jax.ShapeDtypeStruct((M, K), bf16)],
        compiler_params=_cparams(("parallel",)),
    )(h0, mix, g2, g3)


def _final(h1, down, g4, tgt, n_real):
    M, K = h1.shape
    tm = _rt(M)
    nt = M // tm

    def body(h_ref, d_ref, g_ref, t_ref, dh_ref, dd_ref, dg_ref, ls_ref, acc_ref):
        i = pl.program_id(0)

        @pl.when(i == 0)
        def _():
            dg_ref[...] = jnp.zeros_like(dg_ref)
            acc_ref[...] = jnp.zeros_like(acc_ref)

        dv = d_ref[...]
        g = g_ref[...]
        r = _rstd(dv)
        n = dv * r
        h2 = h_ref[...] + n * g
        rows = i * tm + lax.broadcasted_iota(jnp.int32, (tm, 1), 0)
        mask = ((rows >= N_META) & (rows < n_real)).astype(f32)
        diff = (h2 - t_ref[...]) * mask
        acc_ref[...] += jnp.sum(diff * diff, axis=0, keepdims=True)
        dh = diff * (1.0 / K)
        dh_ref[...] = dh
        dn = dh * g
        dd_ref[...] = (r * (dn - n * jnp.mean(dn * n, axis=-1, keepdims=True))).astype(bf16)
        dg_ref[...] += jnp.sum(dh * n, axis=0, keepdims=True)

        @pl.when(i == nt - 1)
        def _():
            ls_ref[...] = jnp.zeros((1, LANES), f32) + jnp.sum(acc_ref[...]) * (0.5 / K)

    return pl.pallas_call(
        body, name="final_loss", grid=(nt,),
        in_specs=[_row(K, tm), _row(K, tm), _full((1, K)), _row(K, tm)],
        out_specs=[_row(K, tm), _row(K, tm), _full((1, K)), _full((1, LANES))],
        out_shape=[jax.ShapeDtypeStruct((M, K), f32), jax.ShapeDtypeStruct((M, K), bf16),
                   jax.ShapeDtypeStruct((1, K), f32), jax.ShapeDtypeStruct((1, LANES), f32)],
        scratch_shapes=[pltpu.VMEM((1, K), f32)],
        compiler_params=_cparams(("arbitrary",)),
    )(h1, down, g4, tgt)


def _mid_bwd(h1, g3, d_hn2, dh2, mix, g2):
    M, K = h1.shape
    tm = _rt(M)

    def body(h_ref, g3_ref, dn_ref, dh2_ref, m_ref, g2_ref, dh1_ref, dm_ref, dg3_ref, dg2_ref):
        @pl.when(pl.program_id(0) == 0)
        def _():
            dg3_ref[...] = jnp.zeros_like(dg3_ref)
            dg2_ref[...] = jnp.zeros_like(dg2_ref)

        dx, dgp = _rms_bwd_math(h_ref[...], g3_ref[...], dn_ref[...])
        dh1 = dh2_ref[...] + dx
        dh1_ref[...] = dh1
        dg3_ref[...] += jnp.sum(dgp, axis=0, keepdims=True)
        dm, dgp2 = _rms_bwd_math(m_ref[...], g2_ref[...], dh1)
        dm_ref[...] = dm.astype(bf16)
        dg2_ref[...] += jnp.sum(dgp2, axis=0, keepdims=True)

    return pl.pallas_call(
        body, name="mid_bwd", grid=(M // tm,),
        in_specs=[_row(K, tm), _full((1, K)), _row(K, tm), _row(K, tm), _row(K, tm), _full((1, K))],
        out_specs=[_row(K, tm), _row(K, tm), _full((1, K)), _full((1, K))],
        out_shape=[jax.ShapeDtypeStruct((M, K), f32), jax.ShapeDtypeStruct((M, K), bf16),
                   jax.ShapeDtypeStruct((1, K), f32), jax.ShapeDtypeStruct((1, K), f32)],
        compiler_params=_cparams(("arbitrary",)),
    )(h1, g3, d_hn2, dh2, mix, g2)


CONV_RB = 16


def _conv_block_taps(x_ref, halo, rb, lanes, kw):
    r0 = rb * CONV_RB
    if rb == 0:
        cat = jnp.concatenate([halo, x_ref[0:CONV_RB, lanes]], axis=0)
        first = SUBLANES - (kw - 1)
        return [cat[first + k:first + k + CONV_RB] for k in range(kw)]
    return [x_ref[r0 - (kw - 1) + k:r0 - (kw - 1) + k + CONV_RB, lanes] for k in range(kw)]


def _conv_weighted(taps, w, kw):
    u = None
    for k in range(kw):
        t = taps[k] * w[k:k + 1, :]
        u = t if u is None else u + t
    return u


def _conv_block_dx(du, nxt, w, kw):
    cat = jnp.concatenate([du, nxt], axis=0)
    return _conv_weighted([cat[kw - 1 - k:kw - 1 - k + CONV_RB] for k in range(kw)], w, kw)


def _prev_spec(tm, tc, col_of, row_axis, reversed_tiles=0):
    def imap(*ids):
        i = ids[row_axis]
        if reversed_tiles:
            i = reversed_tiles - 1 - i
        return (jnp.maximum(i * (tm // SUBLANES) - 1, 0), col_of(*ids))
    return pl.BlockSpec((SUBLANES, tc), imap)


def _ssm_conv_fwd(xbc, w, b):
    M, C = xbc.shape
    tm, tc, kw = ROW_TILE, C, SSM_CONV

    def body(x_ref, h_ref, w_ref, b_ref, o_ref):
        i = pl.program_id(0)

        def chunk(j, carry):
            lanes = pl.ds(pl.multiple_of(j * LANES, LANES), LANES)
            halo = jnp.where(i == 0, 0.0, h_ref[:, lanes])
            wv = w_ref[:, lanes]
            bv = b_ref[:, lanes]
            for rb in range(tm // CONV_RB):
                u = _conv_weighted(_conv_block_taps(x_ref, halo, rb, lanes, kw), wv, kw) + bv
                o_ref[rb * CONV_RB:(rb + 1) * CONV_RB, lanes] = _silu(u)
            return carry

        lax.fori_loop(0, tc // LANES, chunk, 0)

    return pl.pallas_call(
        body, name="ssm_conv_fwd", grid=(M // tm, C // tc),
        in_specs=[pl.BlockSpec((tm, tc), lambda i, j: (i, j)),
                  _prev_spec(tm, tc, lambda i, j: j, 0),
                  pl.BlockSpec((SUBLANES, tc), lambda i, j: (0, j)),
                  pl.BlockSpec((1, tc), lambda i, j: (0, j))],
        out_specs=pl.BlockSpec((tm, tc), lambda i, j: (i, j)),
        out_shape=jax.ShapeDtypeStruct((M, C), f32),
        compiler_params=_cparams(("parallel", "parallel")),
    )(xbc, xbc, w, b)


def _ssm_conv_bwd(xbc, w, b, dout):
    M, C = xbc.shape
    tm, tc, kw = ROW_TILE, C // 3, SSM_CONV
    nt = M // tm

    def body(x_ref, h_ref, w_ref, b_ref, d_ref, dx_ref, dw_ref, db_ref, nxt_ref):
        i = pl.program_id(1)

        @pl.when(i == 0)
        def _():
            dw_ref[...] = jnp.zeros_like(dw_ref)
            db_ref[...] = jnp.zeros_like(db_ref)
            nxt_ref[...] = jnp.zeros_like(nxt_ref)

        def chunk(j, carry):
            lanes = pl.ds(pl.multiple_of(j * LANES, LANES), LANES)
            halo = jnp.where(i == nt - 1, 0.0, h_ref[:, lanes])
            wv = w_ref[:, lanes]
            bv = b_ref[:, lanes]
            nxt = nxt_ref[:, lanes]
            db = jnp.zeros((CONV_RB, LANES), f32)
            dw = [jnp.zeros((CONV_RB, LANES), f32) for _ in range(kw)]
            for rb in reversed(range(tm // CONV_RB)):
                rows = slice(rb * CONV_RB, (rb + 1) * CONV_RB)
                taps = _conv_block_taps(x_ref, halo, rb, lanes, kw)
                du = d_ref[rows, lanes] * _dsilu(_conv_weighted(taps, wv, kw) + bv)
                db = db + du
                dw = [dw[k] + du * taps[k] for k in range(kw)]
                dx_ref[rows, lanes] = _conv_block_dx(du, nxt, wv, kw).astype(bf16)
                nxt = du[0:SUBLANES]
            nxt_ref[:, lanes] = nxt
            db_ref[:, lanes] += jnp.sum(db, axis=0, keepdims=True)
            for k in range(kw):
                dw_ref[k:k + 1, lanes] += jnp.sum(dw[k], axis=0, keepdims=True)
            return carry

        lax.fori_loop(0, tc // LANES, chunk, 0)

    tile = pl.BlockSpec((tm, tc), lambda j, i: (nt - 1 - i, j))
    return pl.pallas_call(
        body, name="ssm_conv_bwd", grid=(C // tc, nt),
        in_specs=[tile, _prev_spec(tm, tc, lambda j, i: j, 1, nt),
                  pl.BlockSpec((SUBLANES, tc), lambda j, i: (0, j)),
                  pl.BlockSpec((1, tc), lambda j, i: (0, j)), tile],
        out_specs=[tile, pl.BlockSpec((SUBLANES, tc), lambda j, i: (0, j)),
                   pl.BlockSpec((1, tc), lambda j, i: (0, j))],
        out_shape=[jax.ShapeDtypeStruct((M, C), bf16), jax.ShapeDtypeStruct((SUBLANES, C), f32),
                   jax.ShapeDtypeStruct((1, C), f32)],
        scratch_shapes=[pltpu.VMEM((SUBLANES, tc), f32)],
        compiler_params=_cparams(("parallel", "arbitrary")),
    )(xbc, xbc, w, b, dout)


def _ffn_gate_fwd(up, w, b):
    M = up.shape[0]
    tm, tc, kw = ROW_TILE, D_FF // 2, FFN_CONV
    nc = D_FF // tc

    def body(xg_ref, hg_ref, xv_ref, hv_ref, wg_ref, wv_ref, bg_ref, bv_ref, o_ref):
        i = pl.program_id(0)

        def chunk(j, carry):
            lanes = pl.ds(pl.multiple_of(j * LANES, LANES), LANES)
            halo_g = jnp.where(i == 0, 0.0, hg_ref[:, lanes])
            halo_v = jnp.where(i == 0, 0.0, hv_ref[:, lanes])
            wg, wv = wg_ref[:, lanes], wv_ref[:, lanes]
            bg, bv = bg_ref[:, lanes], bv_ref[:, lanes]
            for rb in range(tm // CONV_RB):
                ug = _conv_weighted(_conv_block_taps(xg_ref, halo_g, rb, lanes, kw), wg, kw) + bg
                uv = _conv_weighted(_conv_block_taps(xv_ref, halo_v, rb, lanes, kw), wv, kw) + bv
                o_ref[rb * CONV_RB:(rb + 1) * CONV_RB, lanes] = (_silu(ug) * uv).astype(bf16)
            return carry

        lax.fori_loop(0, tc // LANES, chunk, 0)

    return pl.pallas_call(
        body, name="ffn_gate_fwd", grid=(M // tm, nc),
        in_specs=[pl.BlockSpec((tm, tc), lambda i, j: (i, j)),
                  _prev_spec(tm, tc, lambda i, j: j, 0),
                  pl.BlockSpec((tm, tc), lambda i, j: (i, j + nc)),
                  _prev_spec(tm, tc, lambda i, j: j + nc, 0),
                  pl.BlockSpec((SUBLANES, tc), lambda i, j: (0, j)),
                  pl.BlockSpec((SUBLANES, tc), lambda i, j: (0, j + nc)),
                  pl.BlockSpec((1, tc), lambda i, j: (0, j)),
                  pl.BlockSpec((1, tc), lambda i, j: (0, j + nc))],
        out_specs=pl.BlockSpec((tm, tc), lambda i, j: (i, j)),
        out_shape=jax.ShapeDtypeStruct((M, D_FF), bf16),
        compiler_params=_cparams(("parallel", "parallel")),
    )(up, up, up, up, w, w, b, b)


def _ffn_gate_bwd(up, w, b, d_act):
    M = up.shape[0]
    tm, tc, kw = ROW_TILE, D_FF // 2, FFN_CONV
    nc = D_FF // tc
    nt = M // tm

    def body(xg_ref, hg_ref, xv_ref, hv_ref, wg_ref, wv_ref, bg_ref, bv_ref, d_ref,
             dxg_ref, dxv_ref, dwg_ref, dwv_ref, dbg_ref, dbv_ref, ng_ref, nv_ref):
        i = pl.program_id(1)

        @pl.when(i == 0)
        def _():
            for r in (dwg_ref, dwv_ref, dbg_ref, dbv_ref, ng_ref, nv_ref):
                r[...] = jnp.zeros_like(r)

        def chunk(j, carry):
            lanes = pl.ds(pl.multiple_of(j * LANES, LANES), LANES)
            halo_g = jnp.where(i == nt - 1, 0.0, hg_ref[:, lanes])
            halo_v = jnp.where(i == nt - 1, 0.0, hv_ref[:, lanes])
            wg, wv = wg_ref[:, lanes], wv_ref[:, lanes]
            bg, bv = bg_ref[:, lanes], bv_ref[:, lanes]
            nxt_g, nxt_v = ng_ref[:, lanes], nv_ref[:, lanes]
            zero = jnp.zeros((CONV_RB, LANES), f32)
            dbg, dbv = zero, zero
            dwg = [zero for _ in range(kw)]
            dwv = [zero for _ in range(kw)]
            for rb in reversed(range(tm // CONV_RB)):
                rows = slice(rb * CONV_RB, (rb + 1) * CONV_RB)
                tg = _conv_block_taps(xg_ref, halo_g, rb, lanes, kw)
                tv = _conv_block_taps(xv_ref, halo_v, rb, lanes, kw)
                ug = _conv_weighted(tg, wg, kw) + bg
                uv = _conv_weighted(tv, wv, kw) + bv
                sg = _sigmoid(ug)
                da = d_ref[rows, lanes]
                dug = da * uv * (sg * (1.0 + ug * (1.0 - sg)))
                duv = da * (ug * sg)
                dbg = dbg + dug
                dbv = dbv + duv
                dwg = [dwg[k] + dug * tg[k] for k in range(kw)]
                dwv = [dwv[k] + duv * tv[k] for k in range(kw)]
                dxg_ref[rows, lanes] = _conv_block_dx(dug, nxt_g, wg, kw).astype(bf16)
                dxv_ref[rows, lanes] = _conv_block_dx(duv, nxt_v, wv, kw).astype(bf16)
                nxt_g, nxt_v = dug[0:SUBLANES], duv[0:SUBLANES]
            ng_ref[:, lanes] = nxt_g
            nv_ref[:, lanes] = nxt_v
            dbg_ref[:, lanes] += jnp.sum(dbg, axis=0, keepdims=True)
            dbv_ref[:, lanes] += jnp.sum(dbv, axis=0, keepdims=True)
            for k in range(kw):
                dwg_ref[k:k + 1, lanes] += jnp.sum(dwg[k], axis=0, keepdims=True)
                dwv_ref[k:k + 1, lanes] += jnp.sum(dwv[k], axis=0, keepdims=True)
            return carry

        lax.fori_loop(0, tc // LANES, chunk, 0)

    tile_g = pl.BlockSpec((tm, tc), lambda j, i: (nt - 1 - i, j))
    tile_v = pl.BlockSpec((tm, tc), lambda j, i: (nt - 1 - i, j + nc))
    ext = pltpu.VMEM((SUBLANES, tc), f32)
    return pl.pallas_call(
        body, name="ffn_gate_bwd", grid=(nc, nt),
        in_specs=[tile_g, _prev_spec(tm, tc, lambda j, i: j, 1, nt),
                  tile_v, _prev_spec(tm, tc, lambda j, i: j + nc, 1, nt),
                  pl.BlockSpec((SUBLANES, tc), lambda j, i: (0, j)),
                  pl.BlockSpec((SUBLANES, tc), lambda j, i: (0, j + nc)),
                  pl.BlockSpec((1, tc), lambda j, i: (0, j)),
                  pl.BlockSpec((1, tc), lambda j, i: (0, j + nc)),
                  tile_g],
        out_specs=[tile_g, tile_g,
                   pl.BlockSpec((SUBLANES, tc), lambda j, i: (0, j)),
                   pl.BlockSpec((SUBLANES, tc), lambda j, i: (0, j)),
                   pl.BlockSpec((1, tc), lambda j, i: (0, j)),
                   pl.BlockSpec((1, tc), lambda j, i: (0, j))],
        out_shape=[jax.ShapeDtypeStruct((M, D_FF), bf16), jax.ShapeDtypeStruct((M, D_FF), bf16),
                   jax.ShapeDtypeStruct((SUBLANES, D_FF), f32), jax.ShapeDtypeStruct((SUBLANES, D_FF), f32),
                   jax.ShapeDtypeStruct((1, D_FF), f32), jax.ShapeDtypeStruct((1, D_FF), f32)],
        scratch_shapes=[ext, ext],
        compiler_params=_cparams(("parallel", "arbitrary")),
    )(up, up, up, up, w, w, b, b, d_act)


def _rope_apply(blk, cos, sin):
    lane = lax.broadcasted_iota(jnp.int32, blk.shape, 1)
    half = QK_ROPE // 2
    partner = jnp.where(lane < half, pltpu.roll(blk, LANES - half, 1), pltpu.roll(blk, half, 1))
    return blk * cos + partner * sin


def _rope_unapply(d, cos, sin):
    t = d * sin
    lane = lax.broadcasted_iota(jnp.int32, d.shape, 1)
    half = QK_ROPE // 2
    partner = jnp.where(lane < half, pltpu.roll(t, LANES - half, 1), pltpu.roll(t, half, 1))
    return d * cos + partner


def _up_q_rope(qn, wuq, cos, sin):
    M, K = qn.shape
    tm = _pick(M, (768, 512, 256))

    def body(a_ref, b_ref, c_ref, s_ref, o_ref):
        r = _dot(a_ref[...], b_ref[...]) * Q_PRESCALE
        o_ref[0, :, 0:QK_NOPE] = r[:, 0:QK_NOPE].astype(bf16)
        o_ref[0, :, QK_NOPE:QK_PAD] = _rope_apply(r[:, QK_NOPE:QK_PAD], c_ref[...], s_ref[...]).astype(bf16)

    return pl.pallas_call(
        body, name="up_q_rope", grid=(M // tm, MLA_HEADS),
        in_specs=[pl.BlockSpec((tm, K), lambda i, h: (i, 0)),
                  pl.BlockSpec((K, QK_PAD), lambda i, h: (0, h)),
                  pl.BlockSpec((tm, LANES), lambda i, h: (i, 0)),
                  pl.BlockSpec((tm, LANES), lambda i, h: (i, 0))],
        out_specs=pl.BlockSpec((1, tm, QK_PAD), lambda i, h: (h, i, 0)),
        out_shape=jax.ShapeDtypeStruct((MLA_HEADS, M, QK_PAD), bf16),
        compiler_params=_cparams(("parallel", "parallel")),
    )(qn, wuq, cos, sin)


def _up_kv_rope(kvn, wukv, kpe_raw, cos, sin):
    M, K = kvn.shape
    tm = _pick(M, (768, 512, 256))

    def body(a_ref, b_ref, pe_ref, c_ref, s_ref, k_ref, v_ref):
        r = _dot(a_ref[...], b_ref[...])
        k_ref[0, :, 0:QK_NOPE] = r[:, 0:QK_NOPE].astype(bf16)
        k_ref[0, :, QK_NOPE:QK_PAD] = _rope_apply(pe_ref[...], c_ref[...], s_ref[...]).astype(bf16)
        v_ref[0] = r[:, QK_NOPE:QK_NOPE + V_DIM].astype(bf16)

    return pl.pallas_call(
        body, name="up_kv_rope", grid=(M // tm, MLA_HEADS),
        in_specs=[pl.BlockSpec((tm, K), lambda i, h: (i, 0)),
                  pl.BlockSpec((K, QK_NOPE + V_DIM), lambda i, h: (0, h)),
                  pl.BlockSpec((tm, LANES), lambda i, h: (i, 0)),
                  pl.BlockSpec((tm, LANES), lambda i, h: (i, 0)),
                  pl.BlockSpec((tm, LANES), lambda i, h: (i, 0))],
        out_specs=[pl.BlockSpec((1, tm, QK_PAD), lambda i, h: (h, i, 0)),
                   pl.BlockSpec((1, tm, V_DIM), lambda i, h: (h, i, 0))],
        out_shape=[jax.ShapeDtypeStruct((MLA_HEADS, M, QK_PAD), bf16),
                   jax.ShapeDtypeStruct((MLA_HEADS, M, V_DIM), bf16)],
        compiler_params=_cparams(("parallel", "parallel")),
    )(kvn, wukv, kpe_raw, cos, sin)


def _rope_q_bwd(dq, cos, sin):
    M = dq.shape[1]
    tm = _rt(M)

    def body(d_ref, c_ref, s_ref, o_ref):
        c, s = c_ref[...], s_ref[...]
        for h in range(MLA_HEADS):
            o_ref[:, h * QK_PAD:h * QK_PAD + QK_NOPE] = (d_ref[h, :, 0:QK_NOPE] * SOFTMAX_SCALE).astype(bf16)
            o_ref[:, h * QK_PAD + QK_NOPE:(h + 1) * QK_PAD] = (_rope_unapply(
                d_ref[h, :, QK_NOPE:QK_PAD], c, s) * SOFTMAX_SCALE).astype(bf16)

    return pl.pallas_call(
        body, name="rope_q_bwd", grid=(M // tm,),
        in_specs=[pl.BlockSpec((MLA_HEADS, tm, QK_PAD), lambda i: (0, i, 0)),
                  _row(LANES, tm), _row(LANES, tm)],
        out_specs=_row(MLA_HEADS * QK_PAD, tm),
        out_shape=jax.ShapeDtypeStruct((M, MLA_HEADS * QK_PAD), bf16),
        compiler_params=_cparams(("parallel",)),
    )(dq, cos, sin)


def _rope_k_bwd(dk, dv, cos, sin):
    M = dk.shape[1]
    tm = _rt(M)
    w = QK_NOPE + V_DIM

    def body(dk_ref, dv_ref, c_ref, s_ref, o_ref, pe_ref):
        pe = None
        for h in range(MLA_HEADS):
            o_ref[:, h * w:h * w + QK_NOPE] = dk_ref[h, :, 0:QK_NOPE].astype(bf16)
            o_ref[:, h * w + QK_NOPE:(h + 1) * w] = dv_ref[h].astype(bf16)
            t = dk_ref[h, :, QK_NOPE:QK_PAD]
            pe = t if pe is None else pe + t
        pe_ref[...] = _rope_unapply(pe, c_ref[...], s_ref[...])

    return pl.pallas_call(
        body, name="rope_k_bwd", grid=(M // tm,),
        in_specs=[pl.BlockSpec((MLA_HEADS, tm, QK_PAD), lambda i: (0, i, 0)),
                  pl.BlockSpec((MLA_HEADS, tm, V_DIM), lambda i: (0, i, 0)),
                  _row(LANES, tm), _row(LANES, tm)],
        out_specs=[_row(MLA_HEADS * w, tm), _row(LANES, tm)],
        out_shape=[jax.ShapeDtypeStruct((M, MLA_HEADS * w), bf16), jax.ShapeDtypeStruct((M, LANES), f32)],
        compiler_params=_cparams(("parallel",)),
    )(dk, dv, cos, sin)


def _attn_tile(M):
    return 768 if (M % 768 == 0 and M >= 4 * 768) else ROW_TILE


def _col_to_row(col):
    return col.T[0:1, :]


def _hosted_exchange(refs_in, refs_out, sems, scatter, first, last):
    copies = _exchange_copies(refs_in, refs_out, *sems, scatter)

    @pl.when(first)
    def _():
        for cp in copies:
            cp.start()

    @pl.when(last)
    def _():
        for cp in copies:
            cp.wait()


def _flash_fwd(q, k, v, carried, scatter):
    H, M, _ = q.shape
    T = _attn_tile(M)
    nq = M // T
    nx = len(carried)

    def body(*refs):
        q_ref, k_ref, v_ref = refs[:3]
        o_ref, lse_ref = refs[3 + nx:5 + nx]
        sa_ref, sb_ref, m_sc, l_sc, acc_sc = refs[5 + 2 * nx:10 + 2 * nx]
        h = pl.program_id(0)
        i = pl.program_id(1)
        _hosted_exchange(refs[3:3 + nx], refs[5 + nx:5 + 2 * nx], refs[10 + 2 * nx:], scatter,
                         (h == 0) & (i == 0), (h == H - 1) & (i == nq - 1))
        qv = q_ref[0]
        m_sc[...] = jnp.full_like(m_sc, NEG)
        l_sc[...] = jnp.zeros_like(l_sc)
        acc_sc[...] = jnp.zeros_like(acc_sc)

        def scores(j, s_ref):
            off = pl.multiple_of(j * T, T)
            s_ref[...] = _dot_nt(qv, k_ref[0, pl.ds(off, T), :])

        def softmax_pv(j, s_ref, masked):
            off = pl.multiple_of(j * T, T)
            s = s_ref[...]
            if masked:
                r = lax.broadcasted_iota(jnp.int32, (T, T), 0)
                c = lax.broadcasted_iota(jnp.int32, (T, T), 1)
                s = jnp.where(r >= c, s, NEG)
            m_prev = m_sc[...]
            m_new = jnp.maximum(m_prev, jnp.max(s, axis=1, keepdims=True))
            alpha = jnp.exp2(m_prev - m_new)
            p = jnp.exp2(s - m_new[:, 0:1])
            l_sc[...] = alpha * l_sc[...] + jnp.sum(p, axis=1, keepdims=True)
            acc_sc[...] = alpha * acc_sc[...] + _dot(p.astype(bf16), v_ref[0, pl.ds(off, T), :])
            m_sc[...] = m_new

        scores(0, sa_ref)

        def pair(jj, c):
            j0 = 2 * jj
            scores(j0 + 1, sb_ref)
            softmax_pv(j0, sa_ref, False)
            scores(j0 + 2, sa_ref)
            softmax_pv(j0 + 1, sb_ref, False)
            return c

        lax.fori_loop(0, i // 2, pair, 0)

        @pl.when(i % 2 == 0)
        def _():
            softmax_pv(i, sa_ref, True)

        @pl.when(i % 2 == 1)
        def _():
            scores(i, sb_ref)
            softmax_pv(i - 1, sa_ref, False)
            softmax_pv(i, sb_ref, True)

        l = l_sc[...]
        o_ref[...] = acc_sc[...] / l
        lse_ref[0, 0] = _col_to_row(m_sc[...] + jnp.log2(l))

    any_spec = pl.BlockSpec(memory_space=pl.ANY)
    return pl.pallas_call(
        body, name="flash_fwd", grid=(H, nq),
        in_specs=[pl.BlockSpec((1, T, QK_PAD), lambda h, i: (h, i, 0)),
                  pl.BlockSpec((1, M, QK_PAD), lambda h, i: (h, 0, 0)),
                  pl.BlockSpec((1, M, V_DIM), lambda h, i: (h, 0, 0))] + [any_spec] * nx,
        out_specs=[pl.BlockSpec((T, V_DIM), lambda h, i: (i, h)),
                   pl.BlockSpec((1, 1, 1, T), lambda h, i: (h, i, 0, 0))] + [any_spec] * nx,
        out_shape=[jax.ShapeDtypeStruct((M, H * V_DIM), f32),
                   jax.ShapeDtypeStruct((H, nq, 1, T), f32)] + _exchange_shapes(carried, scatter),
        scratch_shapes=[pltpu.VMEM((T, T), f32), pltpu.VMEM((T, T), f32),
                        pltpu.VMEM((T, LANES), f32), pltpu.VMEM((T, LANES), f32),
                        pltpu.VMEM((T, V_DIM), f32)] + _exchange_sems(nx),
        compiler_params=_cparams(("arbitrary", "arbitrary")),
    )(q, k, v, *carried)


def _attn_delta(o, do):
    M = o.shape[0]
    H = MLA_HEADS
    T = _attn_tile(M)

    def body(o_ref, d_ref, dh_ref, dl_ref):
        dv = d_ref[...]
        dh_ref[0] = dv.astype(bf16)
        col = jnp.sum(o_ref[...] * dv, axis=1, keepdims=True) + jnp.zeros((T, LANES), f32)
        dl_ref[0, 0] = _col_to_row(col)

    return pl.pallas_call(
        body, name="attn_delta", grid=(M // T, H),
        in_specs=[pl.BlockSpec((T, V_DIM), lambda i, h: (i, h)),
                  pl.BlockSpec((T, V_DIM), lambda i, h: (i, h))],
        out_specs=[pl.BlockSpec((1, T, V_DIM), lambda i, h: (h, i, 0)),
                   pl.BlockSpec((1, 1, 1, T), lambda i, h: (h, i, 0, 0))],
        out_shape=[jax.ShapeDtypeStruct((H, M, V_DIM), bf16),
                   jax.ShapeDtypeStruct((H, M // T, 1, T), f32)],
        compiler_params=_cparams(("parallel", "parallel")),
    )(o, do)


def _flash_bwd(q, k, v, do, lse, delta, carried, scatter):
    H, M, _ = q.shape
    T = _attn_tile(M)
    nq = M // T
    nx = len(carried)

    def body(*refs):
        q_ref, do_ref, lse_ref, dl_ref, k_ref, v_ref = refs[:6]
        dq_ref, dk_ref, dv_ref = refs[6 + nx:9 + nx]
        dk_sc, dv_sc = refs[9 + 2 * nx:11 + 2 * nx]
        j = pl.program_id(1)
        _hosted_exchange(refs[6:6 + nx], refs[9 + nx:9 + 2 * nx], refs[11 + 2 * nx:], scatter,
                         (pl.program_id(0) == 0) & (j == 0), (pl.program_id(0) == H - 1) & (j == nq - 1))

        @pl.when(j == 0)
        def _():
            dq_ref[...] = jnp.zeros_like(dq_ref)

        kt = k_ref[0]
        vt = v_ref[0]
        dk_sc[...] = jnp.zeros_like(dk_sc)
        dv_sc[...] = jnp.zeros_like(dv_sc)

        def step(i, masked):
            off = pl.multiple_of(i * T, T)
            qt = q_ref[0, pl.ds(off, T), :]
            dot_ = do_ref[0, pl.ds(off, T), :]
            st = _dot_nt(kt, qt)
            if masked:
                r = lax.broadcasted_iota(jnp.int32, (T, T), 0)
                c = lax.broadcasted_iota(jnp.int32, (T, T), 1)
                st = jnp.where(c >= r, st, NEG)
            pt = jnp.exp2(st - lse_ref[0, i])
            dv_sc[...] += _dot(pt.astype(bf16), dot_)
            dpt = _dot_nt(vt, dot_)
            dst = (pt * (dpt - dl_ref[0, i])).astype(bf16)
            dk_sc[...] += _dot(dst, qt)
            dq_ref[0, pl.ds(off, T), :] += _dot_tn(dst, kt)

        step(j, True)

        def loop_body(i, c):
            step(i, False)
            return c

        lax.fori_loop(j + 1, nq, loop_body, 0)
        dk_ref[0] = dk_sc[...] * LN2
        dv_ref[0] = dv_sc[...]

    any_spec = pl.BlockSpec(memory_space=pl.ANY)
    return pl.pallas_call(
        body, name="flash_bwd", grid=(H, nq),
        in_specs=[pl.BlockSpec((1, M, QK_PAD), lambda h, j: (h, 0, 0)),
                  pl.BlockSpec((1, M, V_DIM), lambda h, j: (h, 0, 0)),
                  pl.BlockSpec((1, nq, 1, T), lambda h, j: (h, 0, 0, 0)),
                  pl.BlockSpec((1, nq, 1, T), lambda h, j: (h, 0, 0, 0)),
                  pl.BlockSpec((1, T, QK_PAD), lambda h, j: (h, j, 0)),
                  pl.BlockSpec((1, T, V_DIM), lambda h, j: (h, j, 0))] + [any_spec] * nx,
        out_specs=[pl.BlockSpec((1, M, QK_PAD), lambda h, j: (h, 0, 0)),
                   pl.BlockSpec((1, T, QK_PAD), lambda h, j: (h, j, 0)),
                   pl.BlockSpec((1, T, V_DIM), lambda h, j: (h, j, 0))] + [any_spec] * nx,
        out_shape=[jax.ShapeDtypeStruct((H, M, QK_PAD), f32),
                   jax.ShapeDtypeStruct((H, M, QK_PAD), f32),
                   jax.ShapeDtypeStruct((H, M, V_DIM), f32)] + _exchange_shapes(carried, scatter),
        scratch_shapes=[pltpu.VMEM((T, QK_PAD), f32), pltpu.VMEM((T, V_DIM), f32)] + _exchange_sems(nx),
        compiler_params=_cparams(("arbitrary", "arbitrary")),
    )(q, do, lse, delta, k, v, *carried)


def _dt_fwd(dt_raw, bias, expand):
    M = dt_raw.shape[0]
    tm = _rt(M)

    def body(x_ref, b_ref, e_ref, o_ref, oe_ref):
        u = x_ref[...] + b_ref[...]
        sp = jnp.maximum(u, 0.0) + jnp.log(1.0 + jnp.exp(-jnp.abs(u)))
        lane = lax.broadcasted_iota(jnp.int32, u.shape, 1)
        dtp = jnp.where(lane < SSM_HEADS, sp, 0.0)
        o_ref[...] = dtp
        oe_ref[...] = _dot_hi(dtp, e_ref[...])

    return pl.pallas_call(
        body, name="dt_fwd", grid=(M // tm,),
        in_specs=[_row(LANES, tm), _full((1, LANES)), _full((LANES, D_SSM))],
        out_specs=[_row(LANES, tm), _row(D_SSM, tm)],
        out_shape=[jax.ShapeDtypeStruct((M, LANES), f32), jax.ShapeDtypeStruct((M, D_SSM), f32)],
        compiler_params=_cparams(("parallel",)),
    )(dt_raw, bias, expand)


def _dt_bwd(dt_raw, bias, ddt):
    M = dt_raw.shape[0]
    tm = _rt(M)

    def body(x_ref, b_ref, d_ref, o_ref, db_ref):
        @pl.when(pl.program_id(0) == 0)
        def _():
            db_ref[...] = jnp.zeros_like(db_ref)

        u = x_ref[...] + b_ref[...]
        lane = lax.broadcasted_iota(jnp.int32, u.shape, 1)
        g = jnp.where(lane < SSM_HEADS, d_ref[...] * _sigmoid(u), 0.0)
        o_ref[...] = g
        db_ref[...] += jnp.sum(g, axis=0, keepdims=True)

    return pl.pallas_call(
        body, name="dt_bwd", grid=(M // tm,),
        in_specs=[_row(LANES, tm), _full((1, LANES)), _row(LANES, tm)],
        out_specs=[_row(LANES, tm), _full((1, LANES))],
        out_shape=[jax.ShapeDtypeStruct((M, LANES), f32), jax.ShapeDtypeStruct((1, LANES), f32)],
        compiler_params=_cparams(("arbitrary",)),
    )(dt_raw, bias, ddt)


SSM_GW = SSM_HPG * SSM_P
SSM_PAIRS = SSM_GW // LANES


def _ssd_common(dte_ref, dtt_ref, ae_ref, acol_ref):
    Q = CHUNK
    r = lax.broadcasted_iota(jnp.int32, (Q, Q), 0)
    c = lax.broadcasted_iota(jnp.int32, (Q, Q), 1)
    causal = r >= c
    anti = c >= r
    tril = causal.astype(f32)
    triu = anti.astype(f32)
    dt_e = dte_ref[...]
    cs_e = _dot_hi(tril, dt_e * ae_ref[...])
    cst = _dot_hi(dtt_ref[...] * acol_ref[...], triu)
    cs_last = cs_e[Q - 1:Q, :]
    return causal, anti, triu, dt_e, cs_e, cst, jnp.exp(cs_e), jnp.exp(cs_last - cs_e), jnp.exp(cs_last)


def _half_masks():
    lane = lax.broadcasted_iota(jnp.int32, (CHUNK, LANES), 1)
    lo = lane < SSM_P
    return lo, jnp.logical_not(lo)


def _ssd_fwd(xbc_c, dt_e, dtt, a_e, a_col):
    M = xbc_c.shape[0]
    Q = CHUNK
    nch = M // Q

    def body(x_ref, dte_ref, dtt_ref, ae_ref, acol_ref, y_ref, hin_ref, ht_sc):
        @pl.when(pl.program_id(0) == 0)
        def _():
            ht_sc[...] = jnp.zeros_like(ht_sc)

        causal, _, _, dt_e, cs_e, cst, ecs_e, dte_e, elast_e = _ssd_common(dte_ref, dtt_ref, ae_ref, acol_ref)
        halves = _half_masks()
        for g in range(SSM_GROUPS):
            g0 = g * SSM_GW
            bg = x_ref[:, D_SSM + g * SSM_N:D_SSM + (g + 1) * SSM_N]
            cg = x_ref[:, D_SSM + D_BC + g * SSM_N:D_SSM + D_BC + (g + 1) * SSM_N]
            bg_b = bg.astype(bf16)
            cg_b = cg.astype(bf16)
            cb = _dot_nt(cg_b, bg_b)
            bgt_b = bg.T.astype(bf16)
            xdt_g = x_ref[:, g0:g0 + SSM_GW] * dt_e[:, g0:g0 + SSM_GW]
            ht = ht_sc[g]
            hin_ref[0, g] = ht
            y_off = _dot(cg_b, ht.astype(bf16)) * ecs_e[:, g0:g0 + SSM_GW]
            for pr in range(SSM_PAIRS):
                p0 = pr * LANES
                xdt_p = xdt_g[:, p0:p0 + LANES]
                acc = y_off[:, p0:p0 + LANES]
                for half in range(2):
                    h = g * SSM_HPG + pr * 2 + half
                    seg = cs_e[:, h * SSM_P:h * SSM_P + 1] - cst[h:h + 1, :]
                    lm = jnp.exp(jnp.where(causal, seg, -jnp.inf))
                    xm = jnp.where(halves[half], xdt_p, 0.0).astype(bf16)
                    acc = acc + _dot((cb * lm).astype(bf16), xm)
                y_ref[:, g0 + p0:g0 + p0 + LANES] = acc
            st = _dot(bgt_b, (xdt_g * dte_e[:, g0:g0 + SSM_GW]).astype(bf16))
            ht_sc[g] = ht * elast_e[:, g0:g0 + SSM_GW] + st

    return pl.pallas_call(
        body, name="ssd_fwd", grid=(nch,),
        in_specs=[pl.BlockSpec((Q, D_XBC), lambda c: (c, 0)),
                  pl.BlockSpec((Q, D_SSM), lambda c: (c, 0)),
                  pl.BlockSpec((SSM_HEADS, Q), lambda c: (0, c)),
                  _full((1, D_SSM)), _full((SSM_HEADS, LANES))],
        out_specs=[pl.BlockSpec((Q, D_SSM), lambda c: (c, 0)),
                   pl.BlockSpec((1, SSM_GROUPS, SSM_N, SSM_GW), lambda c: (c, 0, 0, 0))],
        out_shape=[jax.ShapeDtypeStruct((M, D_SSM), f32),
                   jax.ShapeDtypeStruct((nch, SSM_GROUPS, SSM_N, SSM_GW), f32)],
        scratch_shapes=[pltpu.VMEM((SSM_GROUPS, SSM_N, SSM_GW), f32)],
        compiler_params=_cparams(("arbitrary",)),
    )(xbc_c, dt_e, dtt, a_e, a_col)


def _ssd_bwd(xbc_c, dtp, dt_e, dtt, a_row, a_e, a_col, hin, dy, d_exp, head_ind):
    M = xbc_c.shape[0]
    Q = CHUNK
    nch = M // Q
    rev = lambda c: nch - 1 - c

    def body(x_ref, dtp_ref, dte_ref, dtt_ref, arow_ref, ae_ref, acol_ref, hin_ref, dy_ref, dexp_ref,
             ind_ref, dx_ref, ddt_ref, da_ref, dht_sc, z_sc, z1_sc, last_sc, ct_sc):
        @pl.when(pl.program_id(0) == 0)
        def _():
            dht_sc[...] = jnp.zeros_like(dht_sc)
            da_ref[...] = jnp.zeros_like(da_ref)
            last_sc[...] = jnp.zeros_like(last_sc)
            ct_sc[...] = jnp.zeros_like(ct_sc)

        causal, anti, triu, dt_e, cs_e, cst, ecs_e, dte_e, elast_e = _ssd_common(dte_ref, dtt_ref, ae_ref, acol_ref)
        halves = _half_masks()
        lane = lax.broadcasted_iota(jnp.int32, (Q, LANES), 1)
        rsum = jnp.zeros((Q, LANES), f32)
        for g in range(SSM_GROUPS):
            g0 = g * SSM_GW
            gs = slice(g0, g0 + SSM_GW)
            b0 = D_SSM + g * SSM_N
            c0 = D_SSM + D_BC + g * SSM_N
            bg = x_ref[:, b0:b0 + SSM_N]
            cg = x_ref[:, c0:c0 + SSM_N]
            bg_b = bg.astype(bf16)
            cg_b = cg.astype(bf16)
            cgt_b = cg.T.astype(bf16)
            cbt = _dot_nt(bg_b, cg_b)
            cb = _dot_nt(cg_b, bg_b)
            x_g = x_ref[:, gs]
            dt_g = dt_e[:, gs]
            xdt_g = x_g * dt_g
            dy_g = dy_ref[:, gs]
            ht = hin_ref[0, g]
            ht_b = ht.astype(bf16)
            dht = dht_sc[g]
            dht_b = dht.astype(bf16)
            dye_b = (dy_g * ecs_e[:, gs]).astype(bf16)
            dc = _dot_nt(dye_b, ht_b)
            dht_new = dht * elast_e[:, gs] + _dot(cgt_b, dye_b)
            e = _dot(bg_b, dht_b)
            xdtd = xdt_g * dte_e[:, gs]
            db = _dot_nt(xdtd.astype(bf16), dht_b)
            dxdt_state = e * dte_e[:, gs]
            exd = e * xdtd
            z1_sc[:, gs] = dy_g * (_dot(cg_b, ht_b) * ecs_e[:, gs]) - exd
            last_sc[0:1, gs] = (jnp.sum(exd, axis=0, keepdims=True)
                                + jnp.sum(dht * ht, axis=0, keepdims=True) * elast_e[:, gs])
            dg_acc = jnp.zeros((Q, Q), f32)
            for pr in range(SSM_PAIRS):
                p0 = pr * LANES
                ps = slice(g0 + p0, g0 + p0 + LANES)
                dy_p = dy_g[:, p0:p0 + LANES]
                xdt_pb = xdt_g[:, p0:p0 + LANES].astype(bf16)
                acc = dxdt_state[:, p0:p0 + LANES]
                for half in range(2):
                    h = g * SSM_HPG + pr * 2 + half
                    seg = cs_e[:, h * SSM_P:h * SSM_P + 1] - cst[h:h + 1, :]
                    lm = jnp.exp(jnp.where(causal, seg, -jnp.inf))
                    lmt = jnp.exp(jnp.where(anti, -seg, -jnp.inf))
                    dym = jnp.where(halves[half], dy_p, 0.0).astype(bf16)
                    acc = acc + _dot((cbt * lmt).astype(bf16), dym)
                    dml = _dot_nt(dym, xdt_pb) * lm
                    dg_acc = dg_acc + dml
                    w = dml * cb
                    rsum = rsum + jnp.where(lane == h, jnp.sum(w, axis=1, keepdims=True), 0.0)
                    ct_sc[h:h + 1, :] = jnp.sum(w, axis=0, keepdims=True)
                dx_ref[:, ps] = acc * dt_g[:, p0:p0 + LANES] + dexp_ref[:, ps] * dy_p
                z_sc[:, ps] = acc * x_g[:, p0:p0 + LANES]
            dg_b = dg_acc.astype(bf16)
            dx_ref[:, c0:c0 + SSM_N] = dc + _dot(dg_b, bg_b)
            dx_ref[:, b0:b0 + SSM_N] = db + _dot_tn(dg_b, cg_b)
            dht_sc[g] = dht_new
        s1 = _dot_hi(z1_sc[...], ind_ref[...])
        s2 = _dot_hi(z_sc[...], ind_ref[...])
        last = _dot_hi(last_sc[...], ind_ref[...])[0:1, :]
        dtp = dtp_ref[...]
        row = lax.broadcasted_iota(jnp.int32, (Q, LANES), 0)
        dcs = s1 + rsum + jnp.where(row == Q - 1, last, 0.0)
        tril = causal.astype(f32)
        da = _dot_hi(triu, dcs) - _dot_hi(ct_sc[...], tril).T
        ddt_ref[...] = s2 + da * arow_ref[...]
        da_ref[...] += jnp.sum(da * dtp, axis=0, keepdims=True)

    return pl.pallas_call(
        body, name="ssd_bwd", grid=(nch,),
        in_specs=[pl.BlockSpec((Q, D_XBC), lambda c: (rev(c), 0)),
                  pl.BlockSpec((Q, LANES), lambda c: (rev(c), 0)),
                  pl.BlockSpec((Q, D_SSM), lambda c: (rev(c), 0)),
                  pl.BlockSpec((SSM_HEADS, Q), lambda c: (0, rev(c))),
                  _full((1, LANES)), _full((1, D_SSM)), _full((SSM_HEADS, LANES)),
                  pl.BlockSpec((1, SSM_GROUPS, SSM_N, SSM_GW), lambda c: (rev(c), 0, 0, 0)),
                  pl.BlockSpec((Q, D_SSM), lambda c: (rev(c), 0)),
                  _full((1, D_SSM)), _full((D_SSM, LANES))],
        out_specs=[pl.BlockSpec((Q, D_XBC), lambda c: (rev(c), 0)),
                   pl.BlockSpec((Q, LANES), lambda c: (rev(c), 0)),
                   _full((1, LANES))],
        out_shape=[jax.ShapeDtypeStruct((M, D_XBC), f32), jax.ShapeDtypeStruct((M, LANES), f32),
                   jax.ShapeDtypeStruct((1, LANES), f32)],
        scratch_shapes=[pltpu.VMEM((SSM_GROUPS, SSM_N, SSM_GW), f32), pltpu.VMEM((Q, D_SSM), f32),
                        pltpu.VMEM((Q, D_SSM), f32), pltpu.VMEM((SUBLANES, D_SSM), f32),
                        pltpu.VMEM((LANES, Q), f32)],
        compiler_params=_cparams(("arbitrary",)),
    )(xbc_c, dtp, dt_e, dtt, a_row, a_e, a_col, hin, dy, d_exp, head_ind)


def _gate_norm_fwd(y, xbc_c, z, d_exp, g):
    M = y.shape[0]
    tm = _rt(M)
    gw = D_SSM // SSM_GROUPS

    def body(y_ref, x_ref, z_ref, d_ref, g_ref, o_ref):
        yg = (y_ref[...] + d_ref[...] * x_ref[...]) * _silu(z_ref[...])
        for gi in range(SSM_GROUPS):
            blk = yg[:, gi * gw:(gi + 1) * gw]
            o_ref[:, gi * gw:(gi + 1) * gw] = (blk * _rstd(blk) * g_ref[:, gi * gw:(gi + 1) * gw]).astype(bf16)

    return pl.pallas_call(
        body, name="gate_norm_fwd", grid=(M // tm,),
        in_specs=[_row(D_SSM, tm), _row(D_SSM, tm), _row(D_SSM, tm), _full((1, D_SSM)), _full((1, D_SSM))],
        out_specs=_row(D_SSM, tm), out_shape=jax.ShapeDtypeStruct((M, D_SSM), bf16),
        compiler_params=_cparams(("parallel",)),
    )(y, xbc_c, z, d_exp, g)


def _gate_norm_bwd(y, xbc_c, z, d_exp, g, dout, head_ind):
    M = y.shape[0]
    tm = _rt(M)
    nt = M // tm
    gw = D_SSM // SSM_GROUPS

    def body(y_ref, x_ref, z_ref, d_ref, g_ref, do_ref, ind_ref, dy_ref, dz_ref, dg_ref, dd_ref, ddc_sc):
        i = pl.program_id(0)

        @pl.when(i == 0)
        def _():
            dg_ref[...] = jnp.zeros_like(dg_ref)
            ddc_sc[...] = jnp.zeros_like(ddc_sc)

        zv = z_ref[...]
        xv = x_ref[...]
        s = _silu(zv)
        yd = y_ref[...] + d_ref[...] * xv
        yg = yd * s
        dov = do_ref[...]
        for gi in range(SSM_GROUPS):
            sl = slice(gi * gw, (gi + 1) * gw)
            dyg, dgp = _rms_bwd_math(yg[:, sl], g_ref[:, sl], dov[:, sl])
            dg_ref[:, sl] += jnp.sum(dgp, axis=0, keepdims=True)
            dyd = dyg * s[:, sl]
            dy_ref[:, sl] = dyd
            dz_ref[:, sl] = (dyg * yd[:, sl] * _dsilu(zv[:, sl])).astype(bf16)
            ddc_sc[:, sl] += jnp.sum(dyd * xv[:, sl], axis=0, keepdims=True)

        @pl.when(i == nt - 1)
        def _():
            dd_ref[...] = _dot_hi(ddc_sc[...], ind_ref[...])

    return pl.pallas_call(
        body, name="gate_norm_bwd", grid=(nt,),
        in_specs=[_row(D_SSM, tm), _row(D_SSM, tm), _row(D_SSM, tm), _full((1, D_SSM)), _full((1, D_SSM)),
                  _row(D_SSM, tm), _full((D_SSM, LANES))],
        out_specs=[_row(D_SSM, tm), _row(D_SSM, tm), _full((1, D_SSM)), _full((1, LANES))],
        out_shape=[jax.ShapeDtypeStruct((M, D_SSM), f32), jax.ShapeDtypeStruct((M, D_SSM), bf16),
                   jax.ShapeDtypeStruct((1, D_SSM), f32), jax.ShapeDtypeStruct((1, LANES), f32)],
        scratch_shapes=[pltpu.VMEM((1, D_SSM), f32)],
        compiler_params=_cparams(("arbitrary",)),
    )(y, xbc_c, z, d_exp, g, dout, head_ind)


_PEER_FLIPS = [(0, 0, 1), (0, 1, 0), (0, 1, 1), (1, 0, 0), (1, 0, 1), (1, 1, 0), (1, 1, 1)]


def _exchange_copies(ins, outs, send_sems, recv_sems, loc_sems, scatter):
    n = len(ins)
    x, y, c = lax.axis_index("x"), lax.axis_index("y"), lax.axis_index("c")
    me = 4 * x + 2 * y + c
    copies = []
    for a in range(n):
        src = ins[a].at[me] if scatter else ins[a]
        copies.append(pltpu.make_async_copy(src, outs[a].at[me], loc_sems.at[a]))
    for p, (fx, fy, fc) in enumerate(_PEER_FLIPS):
        tx = 1 - x if fx else x
        ty = 1 - y if fy else y
        tc = 1 - c if fc else c
        tgt = 4 * tx + 2 * ty + tc
        for a in range(n):
            src = ins[a].at[tgt] if scatter else ins[a]
            copies.append(pltpu.make_async_remote_copy(
                src_ref=src, dst_ref=outs[a].at[me],
                send_sem=send_sems.at[p * n + a], recv_sem=recv_sems.at[p * n + a],
                device_id=(tx, ty, tc), device_id_type=_MESH))
    return copies


def _exchange_shapes(arrays, scatter):
    return [jax.ShapeDtypeStruct(a.shape if scatter else (N_DEV,) + a.shape, a.dtype) for a in arrays]


def _exchange_sems(n):
    return [pltpu.SemaphoreType.DMA((7 * n,)), pltpu.SemaphoreType.DMA((7 * n,)), pltpu.SemaphoreType.DMA((n,))]


def _exchange(arrays, scatter, name):
    n = len(arrays)

    def body(*refs):
        copies = _exchange_copies(refs[:n], refs[n:2 * n], *refs[2 * n:], scatter)
        for cp in copies:
            cp.start()
        for cp in copies:
            cp.wait()

    any_spec = pl.BlockSpec(memory_space=pl.ANY)
    return pl.pallas_call(
        body, name=name, in_specs=[any_spec] * n, out_specs=[any_spec] * n,
        out_shape=_exchange_shapes(arrays, scatter), scratch_shapes=_exchange_sems(n),
    )(*arrays)


def _adamw(parts, w, m, v, name):
    R, C = w.shape
    tr = _pick(R, (PACK_ROW_TILE, 64, 32, 16, 8))
    c1 = 1.0 - ADAM_B1 ** ADAM_STEP
    c2 = 1.0 - ADAM_B2 ** ADAM_STEP

    def body(p_ref, w_ref, m_ref, v_ref, g_ref, d_ref, nm_ref, nv_ref):
        g = p_ref[0].astype(f32)
        for s in range(1, N_DEV):
            g = g + p_ref[s].astype(f32)
        mn = ADAM_B1 * m_ref[...] + (1.0 - ADAM_B1) * g
        vn = ADAM_B2 * v_ref[...] + (1.0 - ADAM_B2) * (g * g)
        m_hat = mn / c1
        v_hat = vn / c2
        g_ref[...] = g
        d_ref[...] = -ADAM_LR * (m_hat / (jnp.sqrt(v_hat) + ADAM_EPS) + ADAM_WD * w_ref[...])
        nm_ref[...] = mn
        nv_ref[...] = vn

    spec = pl.BlockSpec((tr, C), lambda i: (i, 0))
    return pl.pallas_call(
        body, name=name, grid=(R // tr,),
        in_specs=[pl.BlockSpec((N_DEV, tr, C), lambda i: (0, i, 0)), spec, spec, spec],
        out_specs=[spec] * 4, out_shape=[jax.ShapeDtypeStruct((R, C), f32)] * 4,
        compiler_params=_cparams(("parallel",)),
    )(parts, w, m, v)


def _flat_rows(a, lead_ndim):
    lead = a.shape[:lead_ndim]
    n = int(np.prod(a.shape[lead_ndim:]))
    a = a.reshape(lead + (n,))
    pad = (-n) % PACK_W
    if pad:
        a = jnp.pad(a, [(0, 0)] * lead_ndim + [(0, pad)])
    return a.reshape(lead + ((n + pad) // PACK_W, PACK_W))


def _pack(arrays, lead_ndim, total_rows, dtype):
    rows = [_flat_rows(a.astype(dtype), lead_ndim) for a in arrays]
    cat = jnp.concatenate(rows, axis=lead_ndim)
    pad = total_rows - cat.shape[lead_ndim]
    if pad:
        cat = jnp.pad(cat, [(0, 0)] * lead_ndim + [(0, pad), (0, 0)])
    return cat


def _unpack(buf, shapes, lead_ndim):
    out = []
    r = 0
    lead = buf.shape[:lead_ndim]
    for shp in shapes:
        n = int(np.prod(shp))
        nr = -(-n // PACK_W)
        piece = lax.slice_in_dim(buf, r, r + nr, axis=lead_ndim)
        piece = piece.reshape(lead + (nr * PACK_W,))
        piece = lax.slice_in_dim(piece, 0, n, axis=lead_ndim)
        out.append(piece.reshape(lead + tuple(shp)))
        r += nr
    return out


def _round_up(n, m):
    return -(-n // m) * m


def kernel(x, meta_tokens, norm_mix_pre, norm_mix_post, norm_ffn_pre, norm_ffn_post, w_in, q_a_norm, w_uq, kv_a_norm, w_ukv, attn_out_norm, ssm_conv_w, ssm_conv_b, ssm_dt_bias, ssm_A_log, ssm_D, ssm_norm, w_out, w_up, ffn_conv_w, ffn_conv_b, w_down, loss_target, m_meta_tokens, m_norm_mix_pre, m_norm_mix_post, m_norm_ffn_pre, m_norm_ffn_post, m_w_in, m_q_a_norm, m_w_uq, m_kv_a_norm, m_w_ukv, m_attn_out_norm, m_ssm_conv_w, m_ssm_conv_b, m_ssm_dt_bias, m_ssm_A_log, m_ssm_D, m_ssm_norm, m_w_out, m_w_up, m_ffn_conv_w, m_ffn_conv_b, m_w_down, v_meta_tokens, v_norm_mix_pre, v_norm_mix_post, v_norm_ffn_pre, v_norm_ffn_post, v_w_in, v_q_a_norm, v_w_uq, v_kv_a_norm, v_w_ukv, v_attn_out_norm, v_ssm_conv_w, v_ssm_conv_b, v_ssm_dt_bias, v_ssm_A_log, v_ssm_D, v_ssm_norm, v_w_out, v_w_up, v_ffn_conv_w, v_ffn_conv_b, v_w_down):
    seq = x.shape[1]
    n_real = N_META + seq
    Lp = _round_up(n_real, 768) if n_real > 2048 else _round_up(n_real, ROW_TILE)
    D = D_MODEL

    early_w = [w_in, w_uq, w_ukv]
    late_w = [w_out, w_up, w_down]
    sharded_s = [meta_tokens, ssm_conv_w, ffn_conv_w]
    grp_a = dict(names=["w_out", "w_up", "w_down"], w=late_w, m=[m_w_out, m_w_up, m_w_down],
                 v=[v_w_out, v_w_up, v_w_down])
    grp_b = dict(names=["w_in", "w_uq", "w_ukv", "meta_tokens", "ssm_conv_w", "ffn_conv_w"],
                 w=early_w + sharded_s,
                 m=[m_w_in, m_w_uq, m_w_ukv, m_meta_tokens, m_ssm_conv_w, m_ffn_conv_w],
                 v=[v_w_in, v_w_uq, v_w_ukv, v_meta_tokens, v_ssm_conv_w, v_ffn_conv_w])
    repl_w = [norm_mix_pre, norm_mix_post, norm_ffn_pre, norm_ffn_post, q_a_norm, kv_a_norm, attn_out_norm,
              ssm_conv_b, ssm_dt_bias, ssm_A_log, ssm_D, ssm_norm, ffn_conv_b]
    repl_m = [m_norm_mix_pre, m_norm_mix_post, m_norm_ffn_pre, m_norm_ffn_post, m_q_a_norm, m_kv_a_norm,
              m_attn_out_norm, m_ssm_conv_b, m_ssm_dt_bias, m_ssm_A_log, m_ssm_D, m_ssm_norm, m_ffn_conv_b]
    repl_v = [v_norm_mix_pre, v_norm_mix_post, v_norm_ffn_pre, v_norm_ffn_post, v_q_a_norm, v_kv_a_norm,
              v_attn_out_norm, v_ssm_conv_b, v_ssm_dt_bias, v_ssm_A_log, v_ssm_D, v_ssm_norm, v_ffn_conv_b]

    def pack_rows(arrs, lead):
        return _round_up(sum(-(-int(np.prod(a.shape[lead:])) // PACK_W) for a in arrs), 16)

    wb = _pack(early_w, 0, pack_rows(early_w, 0), bf16)
    wl = _pack(late_w, 0, pack_rows(late_w, 0), bf16)
    ws = _pack(sharded_s, 0, pack_rows(sharded_s, 0), f32)
    wb_all, ws_all = _exchange([wb, ws], False, "gather_weights")
    g_w_in, g_w_uq, g_w_ukv = _unpack(wb_all, [a.shape for a in early_w], 1)
    g_meta, g_sconv, g_fconv = _unpack(ws_all, [a.shape for a in sharded_s], 1)

    def cols(gathered):
        t = gathered[:, 0]
        return jnp.transpose(t, (1, 0, 2)).reshape(t.shape[1], N_DEV * t.shape[2])

    win = cols(g_w_in)
    o = np.cumsum((0, Q_RANK, KV_RANK, QK_ROPE, D_SSM, D_XBC, SSM_HEADS))
    w_q, w_kv = win[:, o[0]:o[1]], win[:, o[1]:o[2]]
    w_rope = jnp.pad(win[:, o[2]:o[3]], ((0, 0), (0, LANES - QK_ROPE)))
    w_z, w_xbc = win[:, o[3]:o[4]], win[:, o[4]:o[5]]
    w_dt = jnp.pad(win[:, o[5]:o[6]], ((0, 0), (0, LANES - SSM_HEADS)))
    wuq = g_w_uq.reshape(Q_RANK, MLA_HEADS, QK_NOPE + QK_ROPE)
    wuq = jnp.pad(wuq, ((0, 0), (0, 0), (0, QK_PAD - QK_NOPE - QK_ROPE))).reshape(Q_RANK, MLA_HEADS * QK_PAD)
    wukv = g_w_ukv.reshape(KV_RANK, MLA_HEADS * (QK_NOPE + V_DIM))
    meta_full = jnp.transpose(g_meta, (1, 0, 2)).reshape(N_META, D)
    sconv_w = jnp.pad(cols(g_sconv), ((0, SUBLANES - SSM_CONV), (0, 0)))
    fconv_w = jnp.pad(cols(g_fconv), ((0, SUBLANES - FFN_CONV), (0, 0)))

    pos = jnp.arange(Lp, dtype=f32)
    inv = ROPE_THETA ** (-jnp.arange(0, QK_ROPE, 2, dtype=f32) / QK_ROPE)
    ang = pos[:, None] * inv[None, :]
    cs_, sn_ = jnp.cos(ang), jnp.sin(ang)
    zpad = jnp.zeros((Lp, LANES - QK_ROPE), f32)
    cos_t = jnp.concatenate([cs_, cs_, zpad], axis=1)
    sin_t = jnp.concatenate([-sn_, sn_, zpad], axis=1)
    dt_bias_p = jnp.pad(ssm_dt_bias, ((0, 0), (0, LANES - SSM_HEADS)))
    a_neg = -jnp.exp(ssm_A_log)
    a_row = jnp.pad(a_neg, ((0, 0), (0, LANES - SSM_HEADS)))
    a_col = jnp.broadcast_to(a_neg.reshape(SSM_HEADS, 1), (SSM_HEADS, LANES))
    d_exp = jnp.repeat(ssm_D, SSM_P, axis=1)
    a_e = jnp.repeat(a_neg, SSM_P, axis=1)
    head_ind = (jnp.arange(D_SSM)[:, None] // SSM_P == jnp.arange(LANES)[None, :]).astype(f32)

    xb = x[0]
    h0 = jnp.concatenate([meta_full, xb, jnp.zeros((Lp - n_real, D), f32)], axis=0)
    tgt = jnp.pad(loss_target[0], ((N_META, Lp - n_real), (0, 0)))
    hn1 = _rms_fwd(h0, norm_mix_pre, bf16, "norm_mix_pre")
    q_c = _mm([(hn1, w_q)], f32, False, "proj_q")
    kv_c = _mm([(hn1, w_kv)], f32, False, "proj_kv")
    kpe_raw = _mm([(hn1, w_rope)], f32, False, "proj_rope")
    z = _mm([(hn1, w_z)], f32, False, "proj_z")
    xbc = _mm([(hn1, w_xbc)], f32, False, "proj_xbc")
    dt_raw = _mm([(hn1, w_dt)], f32, False, "proj_dt")

    qn = _rms_fwd(q_c, q_a_norm, bf16, "norm_q")
    kvn = _rms_fwd(kv_c, kv_a_norm, bf16, "norm_kv")
    qh = _up_q_rope(qn, wuq, cos_t, sin_t)
    kh, vh = _up_kv_rope(kvn, wukv, kpe_raw, cos_t, sin_t)
    attn, lse, wl_all = _flash_fwd(qh, kh, vh, [wl], False)
    g_w_out, g_w_up, g_w_down = _unpack(wl_all, [a.shape for a in late_w], 1)
    wout = g_w_out.reshape(D_ATTN + D_SSM, D)
    wout_a, wout_s = wout[:D_ATTN], wout[D_ATTN:]
    wup = cols(g_w_up)
    wdown = g_w_down.reshape(D_FF, D)
    an = _rms_fwd(attn, attn_out_norm, bf16, "norm_attn_out")

    xbc_c = _ssm_conv_fwd(xbc, sconv_w, ssm_conv_b)
    dtp, dt_e = _dt_fwd(dt_raw, dt_bias_p, jnp.transpose(head_ind))
    dtt = jnp.transpose(dtp[:, :SSM_HEADS])
    y_ssd, hin = _ssd_fwd(xbc_c, dt_e, dtt, a_e, a_col)
    ssm = _gate_norm_fwd(y_ssd, xbc_c, z, d_exp, ssm_norm)

    mix = _mm([(an, wout_a), (ssm, wout_s)], f32, False, "out_proj")
    h1, hn2 = _resid_norm(h0, mix, norm_mix_post, norm_ffn_pre)
    up = _mm([(hn2, wup)], f32, False, "ffn_up")
    act = _ffn_gate_fwd(up, fconv_w, ffn_conv_b)
    down = _mm([(act, wdown)], f32, False, "ffn_down")
    dh2, d_down, dg_ffn_post, loss_part = _final(h1, down, norm_ffn_post, tgt, n_real)

    d_act = _mm([(d_down, wdown)], f32, True, "ffn_down_dx")
    dw_down = _mm_tn(act, d_down, "ffn_down_dw")
    dup_g, dup_v, dwc_g, dwc_v, dbc_g, dbc_v = _ffn_gate_bwd(up, fconv_w, ffn_conv_b, d_act)
    d_hn2 = _mm([(dup_g, wup[:, :D_FF]), (dup_v, wup[:, D_FF:])], f32, True, "ffn_up_dx")
    dw_up = jnp.concatenate([_mm_tn(hn2, dup_g, "ffn_up_dw_g"), _mm_tn(hn2, dup_v, "ffn_up_dw_v")], axis=1)
    dh1, d_mix, dg_ffn_pre, dg_mix_post = _mid_bwd(h1, norm_ffn_pre, d_hn2, dh2, mix, norm_mix_post)
    d_an = _mm([(d_mix, wout_a)], f32, True, "out_proj_dx_a")
    d_ssm = _mm([(d_mix, wout_s)], f32, True, "out_proj_dx_s")
    dw_out = jnp.concatenate([_mm_tn(an, d_mix, "out_proj_dw_a"), _mm_tn(ssm, d_mix, "out_proj_dw_s")], axis=0)

    d_attn, dg_attn_out = _rms_bwd(attn, attn_out_norm, d_an, f32, "norm_attn_out_bwd")
    do_h, delta = _attn_delta(attn, d_attn)
    def col_blocks(gm):
        r, cc = gm.shape
        return jnp.transpose(gm.reshape(r, N_DEV, cc // N_DEV), (1, 0, 2))

    blocks_a = [dw_out.reshape(N_DEV, (D_ATTN + D_SSM) // N_DEV, D), col_blocks(dw_up),
                dw_down.reshape(N_DEV, D_FF // N_DEV, D)]
    gpack_a = _pack(blocks_a, 1, pack_rows(blocks_a, 1), bf16)
    dqh, dkh, dvh, gparts_a = _flash_bwd(qh, kh, vh, do_h, lse, delta, [gpack_a], True)
    dq_full = _rope_q_bwd(dqh, cos_t, sin_t)
    dkv_full, d_kpe_raw = _rope_k_bwd(dkh, dvh, cos_t, sin_t)
    d_qn = _mm([(dq_full, wuq)], f32, True, "up_q_dx")
    dw_uq = _mm_tn(qn, dq_full, "up_q_dw")
    d_kvn = _mm([(dkv_full, wukv)], f32, True, "up_kv_dx")
    dw_ukv = _mm_tn(kvn, dkv_full, "up_kv_dw")
    d_q_c, dg_q = _rms_bwd(q_c, q_a_norm, d_qn, bf16, "norm_q_bwd")
    d_kv_c, dg_kv = _rms_bwd(kv_c, kv_a_norm, d_kvn, bf16, "norm_kv_bwd")

    dy_ssd, dz, dg_ssm, dd_heads = _gate_norm_bwd(y_ssd, xbc_c, z, d_exp, ssm_norm, d_ssm, head_ind)
    d_xbc_c, ddt, da_heads = _ssd_bwd(xbc_c, dtp, dt_e, dtt, a_row, a_e, a_col, hin, dy_ssd, d_exp, head_ind)
    d_xbc, dw_sconv, db_sconv = _ssm_conv_bwd(xbc, sconv_w, ssm_conv_b, d_xbc_c)
    d_dt_raw, d_dt_bias = _dt_bwd(dt_raw, dt_bias_p, ddt)

    segs = [(d_q_c, w_q), (d_kv_c, w_kv), (d_kpe_raw, w_rope), (dz, w_z), (d_xbc, w_xbc), (d_dt_raw, w_dt)]
    d_hn1 = _mm(segs, f32, True, "proj_dx")
    dw_q = _mm_tn(hn1, d_q_c, "proj_dw_q")
    dw_kv = _mm_tn(hn1, d_kv_c, "proj_dw_kv")
    dw_rope = _mm_tn(hn1, d_kpe_raw, "proj_dw_rope")
    dw_z = _mm_tn(hn1, dz, "proj_dw_z")
    dw_xbc = _mm_tn(hn1, d_xbc, "proj_dw_xbc")
    dw_dt = _mm_tn(hn1, d_dt_raw, "proj_dw_dt")
    dh0, dg_mix_pre = _rms_bwd(h0, norm_mix_pre, d_hn1, f32, "norm_mix_pre_bwd", residual=dh1)

    grad_x = dh0[N_META:n_real][None]
    d_meta = dh0[:N_META]

    dw_in = jnp.concatenate([dw_q, dw_kv, dw_rope[:, :QK_ROPE], dw_z, dw_xbc, dw_dt[:, :SSM_HEADS]], axis=1)

    dw_uq3 = dw_uq.reshape(Q_RANK, MLA_HEADS, QK_PAD)[:, :, :QK_NOPE + QK_ROPE]
    blocks_b = [
        col_blocks(dw_in),
        dw_uq3.reshape(N_DEV, Q_RANK // N_DEV, MLA_HEADS, QK_NOPE + QK_ROPE),
        dw_ukv.reshape(N_DEV, KV_RANK // N_DEV, MLA_HEADS, QK_NOPE + V_DIM),
        col_blocks(d_meta),
        col_blocks(dw_sconv[:SSM_CONV]),
        col_blocks(jnp.concatenate([dwc_g, dwc_v], axis=1)[:FFN_CONV]),
    ]
    gpack_b = _pack(blocks_b, 1, pack_rows(blocks_b, 1), bf16)
    (gparts_b,) = _exchange([gpack_b], True, "scatter_grads")

    def adam_group(parts, grp, name):
        rows = parts.shape[1]
        packs = [_pack([a[None] for a in grp[k]], 1, rows, f32)[0] for k in ("w", "m", "v")]
        outs = _adamw(parts, *packs, name)
        shapes = [a.shape for a in grp["w"]]
        return [dict(zip(grp["names"], [t[0] for t in _unpack(b[None], shapes, 1)])) for b in outs]

    sh_a = adam_group(gparts_a, grp_a, "adamw_sharded_a")
    sh_b = adam_group(gparts_b, grp_b, "adamw_sharded_b")

    dg_alog = da_heads[:, :SSM_HEADS] * a_neg
    repl_g = [dg_mix_pre, dg_mix_post, dg_ffn_pre, dg_ffn_post, dg_q, dg_kv, dg_attn_out, db_sconv,
              d_dt_bias[:, :SSM_HEADS], dg_alog, dd_heads[:, :SSM_HEADS], dg_ssm,
              jnp.concatenate([dbc_g, dbc_v], axis=1)]
    loss_vec = loss_part[:, :1]
    small_total = _round_up(sum(-(-int(np.prod(a.shape)) // PACK_W) for a in repl_g) + 1, 16)
    spack = _pack(repl_g + [loss_vec], 0, small_total, f32)
    (sparts,) = _exchange([spack], False, "gather_small_grads")
    zero1 = jnp.zeros((1, 1), f32)
    rw = _pack(repl_w + [zero1], 0, small_total, f32)
    rm = _pack(repl_m + [zero1], 0, small_total, f32)
    rv = _pack(repl_v + [zero1], 0, small_total, f32)
    outs_small = _adamw(sparts, rw, rm, rv, "adamw_replicated")
    repl_shapes = [a.shape for a in repl_w] + [(1, 1)]
    g_rp, d_rp, m_rp, v_rp = [_unpack(b, repl_shapes, 0) for b in outs_small]
    loss = g_rp[-1][0, 0]

    order = ["meta_tokens", "norm_mix_pre", "norm_mix_post", "norm_ffn_pre", "norm_ffn_post", "w_in", "q_a_norm",
             "w_uq", "kv_a_norm", "w_ukv", "attn_out_norm", "ssm_conv_w", "ssm_conv_b", "ssm_dt_bias", "ssm_A_log",
             "ssm_D", "ssm_norm", "w_out", "w_up", "ffn_conv_w", "ffn_conv_b", "w_down"]
    rp_names = ["norm_mix_pre", "norm_mix_post", "norm_ffn_pre", "norm_ffn_post", "q_a_norm", "kv_a_norm",
                "attn_out_norm", "ssm_conv_b", "ssm_dt_bias", "ssm_A_log", "ssm_D", "ssm_norm", "ffn_conv_b"]

    def lookup(k, rp_list):
        d = {**sh_a[k], **sh_b[k], **dict(zip(rp_names, rp_list))}
        return [d[n] for n in order]

    return (loss, grad_x, *lookup(0, g_rp), *lookup(1, d_rp), *lookup(2, m_rp), *lookup(3, v_rp))
```

```python
import functools
import math

import jax
import jax.numpy as jnp
import numpy as np
from jax import lax
from jax.experimental import pallas as pl
from jax.experimental.pallas import tpu as pltpu

f32 = jnp.float32
bf16 = jnp.bfloat16

D_MODEL = 1024
SEQ = 8192
N_META = 16
MLA_HEADS = 8
QK_NOPE = 128
QK_ROPE = 64
V_DIM = 128
Q_RANK = 384
KV_RANK = 256
ROPE_THETA = 10000.0
SOFTMAX_SCALE = (QK_NOPE + QK_ROPE) ** -0.5
D_ATTN = MLA_HEADS * V_DIM
SSM_HEADS = 16
SSM_P = 64
SSM_GROUPS = 2
SSM_HPG = SSM_HEADS // SSM_GROUPS
SSM_N = 128
SSM_CONV = 4
CHUNK = 128
D_SSM = SSM_HEADS * SSM_P
D_BC = SSM_GROUPS * SSM_N
D_XBC = D_SSM + 2 * D_BC
D_FF = 2816
FFN_CONV = 3
EPS = 1e-6
D_IN = Q_RANK + KV_RANK + QK_ROPE + D_SSM + D_XBC + SSM_HEADS
QK_PAD = 256
N_DEV = 8

ADAM_LR = 0.001
ADAM_B1 = 0.9
ADAM_B2 = 0.999
ADAM_EPS = 1e-08
ADAM_WD = 0.01
ADAM_STEP = 10

LANES = 128
SUBLANES = 8
ROW_TILE = 256
VMEM_LIMIT = 56 * 1024 * 1024
PACK_W = 1024
PACK_ROW_TILE = 128
NEG = -1e30
LOG2E = math.log2(math.e)
LN2 = math.log(2.0)
Q_PRESCALE = SOFTMAX_SCALE * LOG2E

_MESH = pl.DeviceIdType.MESH


def _pick(n, prefs):
    for p in prefs:
        if n % p == 0:
            return p
    return n


def _rt(m):
    return _pick(m, (384, ROW_TILE))


def _cparams(sem):
    return pltpu.CompilerParams(dimension_semantics=sem, vmem_limit_bytes=VMEM_LIMIT)


def _row(spec_cols, tm):
    return pl.BlockSpec((tm, spec_cols), lambda i: (i, 0))


def _full(shape):
    nd = len(shape)
    return pl.BlockSpec(shape, lambda *a: (0,) * nd)


def _sigmoid(x):
    return 1.0 / (1.0 + jnp.exp(-x))


def _silu(x):
    return x * _sigmoid(x)


def _dsilu(x):
    s = _sigmoid(x)
    return s * (1.0 + x * (1.0 - s))


def _dot(a, b):
    return jnp.dot(a, b, preferred_element_type=f32)


def _dot_nt(a, b):
    return lax.dot_general(a, b, (((1,), (1,)), ((), ())), preferred_element_type=f32)


def _dot_tn(a, b):
    return lax.dot_general(a, b, (((0,), (0,)), ((), ())), preferred_element_type=f32)


def _dot_hi(a, b):
    return jnp.dot(a, b, precision=lax.Precision.HIGHEST, preferred_element_type=f32)


def _mm(pairs, out_dtype, trans_b, name, carried=(), scatter=False):
    n = len(pairs)
    nx = len(carried)
    M = pairs[0][0].shape[0]
    N = pairs[0][1].shape[0] if trans_b else pairs[0][1].shape[1]
    tm = _pick(M, (768, 512, 256))
    tn = _pick(N, (512, 1408, 384, 256, 128))
    ni, nj = M // tm, N // tn

    def body(*refs):
        o_ref = refs[2 * n + nx]
        if nx:
            i, j = pl.program_id(0), pl.program_id(1)
            _hosted_exchange(refs[2 * n:2 * n + nx], refs[2 * n + nx + 1:2 * n + 2 * nx + 1],
                             refs[2 * n + 2 * nx + 1:], scatter,
                             (i == 0) & (j == 0), (i == ni - 1) & (j == nj - 1))
        acc = None
        for p in range(n):
            a = refs[2 * p][...].astype(bf16)
            b = refs[2 * p + 1][...].astype(bf16)
            r = _dot_nt(a, b) if trans_b else _dot(a, b)
            acc = r if acc is None else acc + r
        o_ref[...] = acc.astype(out_dtype)

    in_specs, args = [], []
    for a, b in pairs:
        k = a.shape[1]
        in_specs.append(pl.BlockSpec((tm, k), lambda i, j: (i, 0)))
        if trans_b:
            in_specs.append(pl.BlockSpec((tn, k), lambda i, j: (j, 0)))
        else:
            in_specs.append(pl.BlockSpec((k, tn), lambda i, j: (0, j)))
        args += [a, b]
    out_spec = pl.BlockSpec((tm, tn), lambda i, j: (i, j))
    out_shape = jax.ShapeDtypeStruct((M, N), out_dtype)
    if not nx:
        return pl.pallas_call(
            body, name=name, grid=(ni, nj), in_specs=in_specs, out_specs=out_spec, out_shape=out_shape,
            compiler_params=_cparams(("parallel", "parallel")),
        )(*args)
    any_spec = pl.BlockSpec(memory_space=pl.ANY)
    return pl.pallas_call(
        body, name=name, grid=(ni, nj), in_specs=in_specs + [any_spec] * nx,
        out_specs=[out_spec] + [any_spec] * nx,
        out_shape=[out_shape] + _exchange_shapes(carried, scatter),
        scratch_shapes=_exchange_sems(nx),
        compiler_params=_cparams(("arbitrary", "arbitrary")),
    )(*args, *carried)


def _mm_tn(a, g, name):
    M, K = a.shape
    N = g.shape[1]
    tm = _pick(M, (768, 512, 256))
    tk = _pick(K, (1024, 1408, 512, 384, 256))
    tn = _pick(N, (1024, 1408, 512, 384, 256, 128))

    def body(a_ref, g_ref, o_ref):
        @pl.when(pl.program_id(2) == 0)
        def _():
            o_ref[...] = jnp.zeros_like(o_ref)

        o_ref[...] += _dot_tn(a_ref[...].astype(bf16), g_ref[...].astype(bf16))

    return pl.pallas_call(
        body, name=name, grid=(K // tk, N // tn, M // tm),
        in_specs=[pl.BlockSpec((tm, tk), lambda k, j, m: (m, k)),
                  pl.BlockSpec((tm, tn), lambda k, j, m: (m, j))],
        out_specs=pl.BlockSpec((tk, tn), lambda k, j, m: (k, j)),
        out_shape=jax.ShapeDtypeStruct((K, N), f32),
        compiler_params=_cparams(("parallel", "parallel", "arbitrary")),
    )(a, g)


def _rstd(x):
    return lax.rsqrt(jnp.mean(x * x, axis=-1, keepdims=True) + EPS)


def _rms_bwd_math(x, g, dy):
    r = _rstd(x)
    xh = x * r
    dn = dy * g
    dx = r * (dn - xh * jnp.mean(dn * xh, axis=-1, keepdims=True))
    return dx, dy * xh


def _rms_fwd(x, g, out_dtype, name):
    M, K = x.shape
    tm = _rt(M)

    def body(x_ref, g_ref, o_ref):
        xv = x_ref[...]
        o_ref[...] = (xv * _rstd(xv) * g_ref[...]).astype(out_dtype)

    return pl.pallas_call(
        body, name=name, grid=(M // tm,), in_specs=[_row(K, tm), _full((1, K))],
        out_specs=_row(K, tm), out_shape=jax.ShapeDtypeStruct((M, K), out_dtype),
        compiler_params=_cparams(("parallel",)),
    )(x, g)


def _rms_bwd(x, g, dy, out_dtype, name, residual=None):
    M, K = x.shape
    tm = _rt(M)
    has_res = residual is not None

    def body(*refs):
        if has_res:
            x_ref, g_ref, dy_ref, r_ref, dx_ref, dg_ref = refs
        else:
            x_ref, g_ref, dy_ref, dx_ref, dg_ref = refs

        @pl.when(pl.program_id(0) == 0)
        def _():
            dg_ref[...] = jnp.zeros_like(dg_ref)

        dx, dgp = _rms_bwd_math(x_ref[...], g_ref[...], dy_ref[...].astype(f32))
        if has_res:
            dx = dx + r_ref[...]
        dx_ref[...] = dx.astype(out_dtype)
        dg_ref[...] += jnp.sum(dgp, axis=0, keepdims=True)

    ins = [x, g, dy] + ([residual] if has_res else [])
    in_specs = [_row(K, tm), _full((1, K)), _row(K, tm)] + ([_row(K, tm)] if has_res else [])
    return pl.pallas_call(
        body, name=name, grid=(M // tm,), in_specs=in_specs,
        out_specs=[_row(K, tm), _full((1, K))],
        out_shape=[jax.ShapeDtypeStruct((M, K), out_dtype), jax.ShapeDtypeStruct((1, K), f32)],
        compiler_params=_cparams(("arbitrary",)),
    )(*ins)


def _resid_norm(h0, mix, g2, g3):
    M, K = h0.shape
    tm = _rt(M)

    def body(h_ref, m_ref, g2_ref, g3_ref, h1_ref, hn_ref):
        mv = m_ref[...]
        h1 = h_ref[...] + mv * _rstd(mv) * g2_ref[...]
        h1_ref[...] = h1
        hn_ref[...] = (h1 * _rstd(h1) * g3_ref[...]).astype(bf16)

    return pl.pallas_call(
        body, name="resid_norm", grid=(M // tm,),
        in_specs=[_row(K, tm), _row(K, tm), _full((1, K)), _full((1, K))],
        out_specs=[_row(K, tm), _row(K, tm)],
        out_shape=[jax.ShapeDtypeStruct((M, K), f32), jax.ShapeDtypeStruct((M, K), bf16)],
        compiler_params=_cparams(("parallel",)),
    )(h0, mix, g2, g3)


def _final(h1, down, g4, tgt, n_real):
    M, K = h1.shape
    tm = _rt(M)
    nt = M // tm

    def body(h_ref, d_ref, g_ref, t_ref, dh_ref, dd_ref, dg_ref, ls_ref, acc_ref):
        i = pl.program_id(0)

        @pl.when(i == 0)
        def _():
            dg_ref[...] = jnp.zeros_like(dg_ref)
            acc_ref[...] = jnp.zeros_like(acc_ref)

        dv = d_ref[...]
        g = g_ref[...]
        r = _rstd(dv)
        n = dv * r
        h2 = h_ref[...] + n * g
        rows = i * tm + lax.broadcasted_iota(jnp.int32, (tm, 1), 0)
        mask = ((rows >= N_META) & (rows < n_real)).astype(f32)
        diff = (h2 - t_ref[...]) * mask
        acc_ref[...] += jnp.sum(diff * diff, axis=0, keepdims=True)
        dh = diff * (1.0 / K)
        dh_ref[...] = dh
        dn = dh * g
        dd_ref[...] = (r * (dn - n * jnp.mean(dn * n, axis=-1, keepdims=True))).astype(bf16)
        dg_ref[...] += jnp.sum(dh * n, axis=0, keepdims=True)

        @pl.when(i == nt - 1)
        def _():
            ls_ref[...] = jnp.zeros((1, LANES), f32) + jnp.sum(acc_ref[...]) * (0.5 / K)

    return pl.pallas_call(
        body, name="final_loss", grid=(nt,),
        in_specs=[_row(K, tm), _row(K, tm), _full((1, K)), _row(K, tm)],
        out_specs=[_row(K, tm), _row(K, tm), _full((1, K)), _full((1, LANES))],
        out_shape=[jax.ShapeDtypeStruct((M, K), f32), jax.ShapeDtypeStruct((M, K), bf16),
                   jax.ShapeDtypeStruct((1, K), f32), jax.ShapeDtypeStruct((1, LANES), f32)],
        scratch_shapes=[pltpu.VMEM((1, K), f32)],
        compiler_params=_cparams(("arbitrary",)),
    )(h1, down, g4, tgt)


def _mid_bwd(h1, g3, d_hn2, dh2, mix, g2):
    M, K = h1.shape
    tm = _rt(M)

    def body(h_ref, g3_ref, dn_ref, dh2_ref, m_ref, g2_ref, dh1_ref, dm_ref, dg3_ref, dg2_ref):
        @pl.when(pl.program_id(0) == 0)
        def _():
            dg3_ref[...] = jnp.zeros_like(dg3_ref)
            dg2_ref[...] = jnp.zeros_like(dg2_ref)

        dx, dgp = _rms_bwd_math(h_ref[...], g3_ref[...], dn_ref[...])
        dh1 = dh2_ref[...] + dx
        dh1_ref[...] = dh1
        dg3_ref[...] += jnp.sum(dgp, axis=0, keepdims=True)
        dm, dgp2 = _rms_bwd_math(m_ref[...], g2_ref[...], dh1)
        dm_ref[...] = dm.astype(bf16)
        dg2_ref[...] += jnp.sum(dgp2, axis=0, keepdims=True)

    return pl.pallas_call(
        body, name="mid_bwd", grid=(M // tm,),
        in_specs=[_row(K, tm), _full((1, K)), _row(K, tm), _row(K, tm), _row(K, tm), _full((1, K))],
        out_specs=[_row(K, tm), _row(K, tm), _full((1, K)), _full((1, K))],
        out_shape=[jax.ShapeDtypeStruct((M, K), f32), jax.ShapeDtypeStruct((M, K), bf16),
                   jax.ShapeDtypeStruct((1, K), f32), jax.ShapeDtypeStruct((1, K), f32)],
        compiler_params=_cparams(("arbitrary",)),
    )(h1, g3, d_hn2, dh2, mix, g2)


HEADS_PER_STEP = 4
CONV_RB = 16


def _conv_block_taps(x_ref, halo, rb, lanes, kw):
    r0 = rb * CONV_RB
    if rb == 0:
        cat = jnp.concatenate([halo, x_ref[0:CONV_RB, lanes]], axis=0)
        first = SUBLANES - (kw - 1)
        return [cat[first + k:first + k + CONV_RB] for k in range(kw)]
    return [x_ref[r0 - (kw - 1) + k:r0 - (kw - 1) + k + CONV_RB, lanes] for k in range(kw)]


def _conv_weighted(taps, w, kw):
    u = None
    for k in range(kw):
        t = taps[k] * w[k:k + 1, :]
        u = t if u is None else u + t
    return u


def _conv_block_dx(du, nxt, w, kw):
    cat = jnp.concatenate([du, nxt], axis=0)
    return _conv_weighted([cat[kw - 1 - k:kw - 1 - k + CONV_RB] for k in range(kw)], w, kw)


def _prev_spec(tm, tc, col_of, row_axis, reversed_tiles=0):
    def imap(*ids):
        i = ids[row_axis]
        if reversed_tiles:
            i = reversed_tiles - 1 - i
        return (jnp.maximum(i * (tm // SUBLANES) - 1, 0), col_of(*ids))
    return pl.BlockSpec((SUBLANES, tc), imap)


def _ssm_conv_fwd(xbc, w, b):
    M, C = xbc.shape
    tm, tc, kw = ROW_TILE, C, SSM_CONV

    def body(x_ref, h_ref, w_ref, b_ref, o_ref):
        i = pl.program_id(0)

        def chunk(j, carry):
            lanes = pl.ds(pl.multiple_of(j * LANES, LANES), LANES)
            halo = jnp.where(i == 0, 0.0, h_ref[:, lanes])
            wv = w_ref[:, lanes]
            bv = b_ref[:, lanes]
            for rb in range(tm // CONV_RB):
                u = _conv_weighted(_conv_block_taps(x_ref, halo, rb, lanes, kw), wv, kw) + bv
                o_ref[rb * CONV_RB:(rb + 1) * CONV_RB, lanes] = _silu(u)
            return carry

        lax.fori_loop(0, tc // LANES, chunk, 0)

    return pl.pallas_call(
        body, name="ssm_conv_fwd", grid=(M // tm, C // tc),
        in_specs=[pl.BlockSpec((tm, tc), lambda i, j: (i, j)),
                  _prev_spec(tm, tc, lambda i, j: j, 0),
                  pl.BlockSpec((SUBLANES, tc), lambda i, j: (0, j)),
                  pl.BlockSpec((1, tc), lambda i, j: (0, j))],
        out_specs=pl.BlockSpec((tm, tc), lambda i, j: (i, j)),
        out_shape=jax.ShapeDtypeStruct((M, C), f32),
        compiler_params=_cparams(("parallel", "parallel")),
    )(xbc, xbc, w, b)


def _ssm_conv_bwd(xbc, w, b, dout):
    M, C = xbc.shape
    tm, tc, kw = ROW_TILE, C // 3, SSM_CONV
    nt = M // tm

    def body(x_ref, h_ref, w_ref, b_ref, d_ref, dx_ref, dw_ref, db_ref, nxt_ref):
        i = pl.program_id(1)

        @pl.when(i == 0)
        def _():
            dw_ref[...] = jnp.zeros_like(dw_ref)
            db_ref[...] = jnp.zeros_like(db_ref)
            nxt_ref[...] = jnp.zeros_like(nxt_ref)

        def chunk(j, carry):
            lanes = pl.ds(pl.multiple_of(j * LANES, LANES), LANES)
            halo = jnp.where(i == nt - 1, 0.0, h_ref[:, lanes])
            wv = w_ref[:, lanes]
            bv = b_ref[:, lanes]
            nxt = nxt_ref[:, lanes]
            db = jnp.zeros((CONV_RB, LANES), f32)
            dw = [jnp.zeros((CONV_RB, LANES), f32) for _ in range(kw)]
            for rb in reversed(range(tm // CONV_RB)):
                rows = slice(rb * CONV_RB, (rb + 1) * CONV_RB)
                taps = _conv_block_taps(x_ref, halo, rb, lanes, kw)
                du = d_ref[rows, lanes] * _dsilu(_conv_weighted(taps, wv, kw) + bv)
                db = db + du
                dw = [dw[k] + du * taps[k] for k in range(kw)]
                dx_ref[rows, lanes] = _conv_block_dx(du, nxt, wv, kw).astype(bf16)
                nxt = du[0:SUBLANES]
            nxt_ref[:, lanes] = nxt
            db_ref[:, lanes] += jnp.sum(db, axis=0, keepdims=True)
            for k in range(kw):
                dw_ref[k:k + 1, lanes] += jnp.sum(dw[k], axis=0, keepdims=True)
            return carry

        lax.fori_loop(0, tc // LANES, chunk, 0)

    tile = pl.BlockSpec((tm, tc), lambda j, i: (nt - 1 - i, j))
    return pl.pallas_call(
        body, name="ssm_conv_bwd", grid=(C // tc, nt),
        in_specs=[tile, _prev_spec(tm, tc, lambda j, i: j, 1, nt),
                  pl.BlockSpec((SUBLANES, tc), lambda j, i: (0, j)),
                  pl.BlockSpec((1, tc), lambda j, i: (0, j)), tile],
        out_specs=[tile, pl.BlockSpec((SUBLANES, tc), lambda j, i: (0, j)),
                   pl.BlockSpec((1, tc), lambda j, i: (0, j))],
        out_shape=[jax.ShapeDtypeStruct((M, C), bf16), jax.ShapeDtypeStruct((SUBLANES, C), f32),
                   jax.ShapeDtypeStruct((1, C), f32)],
        scratch_shapes=[pltpu.VMEM((SUBLANES, tc), f32)],
        compiler_params=_cparams(("parallel", "arbitrary")),
    )(xbc, xbc, w, b, dout)


def _ffn_gate_fwd(up, w, b):
    M = up.shape[0]
    tm, tc, kw = ROW_TILE, D_FF // 2, FFN_CONV
    nc = D_FF // tc

    def body(xg_ref, hg_ref, xv_ref, hv_ref, wg_ref, wv_ref, bg_ref, bv_ref, o_ref):
        i = pl.program_id(0)

        def chunk(j, carry):
            lanes = pl.ds(pl.multiple_of(j * LANES, LANES), LANES)
            halo_g = jnp.where(i == 0, 0.0, hg_ref[:, lanes])
            halo_v = jnp.where(i == 0, 0.0, hv_ref[:, lanes])
            wg, wv = wg_ref[:, lanes], wv_ref[:, lanes]
            bg, bv = bg_ref[:, lanes], bv_ref[:, lanes]
            for rb in range(tm // CONV_RB):
                ug = _conv_weighted(_conv_block_taps(xg_ref, halo_g, rb, lanes, kw), wg, kw) + bg
                uv = _conv_weighted(_conv_block_taps(xv_ref, halo_v, rb, lanes, kw), wv, kw) + bv
                o_ref[rb * CONV_RB:(rb + 1) * CONV_RB, lanes] = (_silu(ug) * uv).astype(bf16)
            return carry

        lax.fori_loop(0, tc // LANES, chunk, 0)

    return pl.pallas_call(
        body, name="ffn_gate_fwd", grid=(M // tm, nc),
        in_specs=[pl.BlockSpec((tm, tc), lambda i, j: (i, j)),
                  _prev_spec(tm, tc, lambda i, j: j, 0),
                  pl.BlockSpec((tm, tc), lambda i, j: (i, j + nc)),
                  _prev_spec(tm, tc, lambda i, j: j + nc, 0),
                  pl.BlockSpec((SUBLANES, tc), lambda i, j: (0, j)),
                  pl.BlockSpec((SUBLANES, tc), lambda i, j: (0, j + nc)),
                  pl.BlockSpec((1, tc), lambda i, j: (0, j)),
                  pl.BlockSpec((1, tc), lambda i, j: (0, j + nc))],
        out_specs=pl.BlockSpec((tm, tc), lambda i, j: (i, j)),
        out_shape=jax.ShapeDtypeStruct((M, D_FF), bf16),
        compiler_params=_cparams(("parallel", "parallel")),
    )(up, up, up, up, w, w, b, b)


def _ffn_gate_bwd(up, w, b, d_act):
    M = up.shape[0]
    tm, tc, kw = ROW_TILE, D_FF // 2, FFN_CONV
    nc = D_FF // tc
    nt = M // tm

    def body(xg_ref, hg_ref, xv_ref, hv_ref, wg_ref, wv_ref, bg_ref, bv_ref, d_ref,
             dxg_ref, dxv_ref, dwg_ref, dwv_ref, dbg_ref, dbv_ref, ng_ref, nv_ref):
        i = pl.program_id(1)

        @pl.when(i == 0)
        def _():
            for r in (dwg_ref, dwv_ref, dbg_ref, dbv_ref, ng_ref, nv_ref):
                r[...] = jnp.zeros_like(r)

        def chunk(j, carry):
            lanes = pl.ds(pl.multiple_of(j * LANES, LANES), LANES)
            halo_g = jnp.where(i == nt - 1, 0.0, hg_ref[:, lanes])
            halo_v = jnp.where(i == nt - 1, 0.0, hv_ref[:, lanes])
            wg, wv = wg_ref[:, lanes], wv_ref[:, lanes]
            bg, bv = bg_ref[:, lanes], bv_ref[:, lanes]
            nxt_g, nxt_v = ng_ref[:, lanes], nv_ref[:, lanes]
            zero = jnp.zeros((CONV_RB, LANES), f32)
            dbg, dbv = zero, zero
            dwg = [zero for _ in range(kw)]
            dwv = [zero for _ in range(kw)]
            for rb in reversed(range(tm // CONV_RB)):
                rows = slice(rb * CONV_RB, (rb + 1) * CONV_RB)
                tg = _conv_block_taps(xg_ref, halo_g, rb, lanes, kw)
                tv = _conv_block_taps(xv_ref, halo_v, rb, lanes, kw)
                ug = _conv_weighted(tg, wg, kw) + bg
                uv = _conv_weighted(tv, wv, kw) + bv
                sg = _sigmoid(ug)
                da = d_ref[rows, lanes]
                dug = da * uv * (sg * (1.0 + ug * (1.0 - sg)))
                duv = da * (ug * sg)
                dbg = dbg + dug
                dbv = dbv + duv
                dwg = [dwg[k] + dug * tg[k] for k in range(kw)]
                dwv = [dwv[k] + duv * tv[k] for k in range(kw)]
                dxg_ref[rows, lanes] = _conv_block_dx(dug, nxt_g, wg, kw).astype(bf16)
                dxv_ref[rows, lanes] = _conv_block_dx(duv, nxt_v, wv, kw).astype(bf16)
                nxt_g, nxt_v = dug[0:SUBLANES], duv[0:SUBLANES]
            ng_ref[:, lanes] = nxt_g
            nv_ref[:, lanes] = nxt_v
            dbg_ref[:, lanes] += jnp.sum(dbg, axis=0, keepdims=True)
            dbv_ref[:, lanes] += jnp.sum(dbv, axis=0, keepdims=True)
            for k in range(kw):
                dwg_ref[k:k + 1, lanes] += jnp.sum(dwg[k], axis=0, keepdims=True)
                dwv_ref[k:k + 1, lanes] += jnp.sum(dwv[k], axis=0, keepdims=True)
            return carry

        lax.fori_loop(0, tc // LANES, chunk, 0)

    tile_g = pl.BlockSpec((tm, tc), lambda j, i: (nt - 1 - i, j))
    tile_v = pl.BlockSpec((tm, tc), lambda j, i: (nt - 1 - i, j + nc))
    ext = pltpu.VMEM((SUBLANES, tc), f32)
    return pl.pallas_call(
        body, name="ffn_gate_bwd", grid=(nc, nt),
        in_specs=[tile_g, _prev_spec(tm, tc, lambda j, i: j, 1, nt),
                  tile_v, _prev_spec(tm, tc, lambda j, i: j + nc, 1, nt),
                  pl.BlockSpec((SUBLANES, tc), lambda j, i: (0, j)),
                  pl.BlockSpec((SUBLANES, tc), lambda j, i: (0, j + nc)),
                  pl.BlockSpec((1, tc), lambda j, i: (0, j)),
                  pl.BlockSpec((1, tc), lambda j, i: (0, j + nc)),
                  tile_g],
        out_specs=[tile_g, tile_g,
                   pl.BlockSpec((SUBLANES, tc), lambda j, i: (0, j)),
                   pl.BlockSpec((SUBLANES, tc), lambda j, i: (0, j)),
                   pl.BlockSpec((1, tc), lambda j, i: (0, j)),
                   pl.BlockSpec((1, tc), lambda j, i: (0, j))],
        out_shape=[jax.ShapeDtypeStruct((M, D_FF), bf16), jax.ShapeDtypeStruct((M, D_FF), bf16),
                   jax.ShapeDtypeStruct((SUBLANES, D_FF), f32), jax.ShapeDtypeStruct((SUBLANES, D_FF), f32),
                   jax.ShapeDtypeStruct((1, D_FF), f32), jax.ShapeDtypeStruct((1, D_FF), f32)],
        scratch_shapes=[ext, ext],
        compiler_params=_cparams(("parallel", "arbitrary")),
    )(up, up, up, up, w, w, b, b, d_act)


def _rope_apply(blk, cos, sin):
    lane = lax.broadcasted_iota(jnp.int32, blk.shape, 1)
    half = QK_ROPE // 2
    partner = jnp.where(lane < half, pltpu.roll(blk, LANES - half, 1), pltpu.roll(blk, half, 1))
    return blk * cos + partner * sin


def _rope_unapply(d, cos, sin):
    t = d * sin
    lane = lax.broadcasted_iota(jnp.int32, d.shape, 1)
    half = QK_ROPE // 2
    partner = jnp.where(lane < half, pltpu.roll(t, LANES - half, 1), pltpu.roll(t, half, 1))
    return d * cos + partner


def _up_q_rope(qn, wuq, cos, sin):
    M, K = qn.shape
    tm = _pick(M, (768, 512, 256))

    hs = HEADS_PER_STEP

    def body(a_ref, b_ref, c_ref, s_ref, o_ref):
        r = _dot(a_ref[...], b_ref[...]) * Q_PRESCALE
        c, s = c_ref[...], s_ref[...]
        for u in range(hs):
            o_ref[u, :, 0:QK_NOPE] = r[:, u * QK_PAD:u * QK_PAD + QK_NOPE].astype(bf16)
            o_ref[u, :, QK_NOPE:QK_PAD] = _rope_apply(r[:, u * QK_PAD + QK_NOPE:(u + 1) * QK_PAD], c, s).astype(bf16)

    return pl.pallas_call(
        body, name="up_q_rope", grid=(M // tm, MLA_HEADS // hs),
        in_specs=[pl.BlockSpec((tm, K), lambda i, h: (i, 0)),
                  pl.BlockSpec((K, hs * QK_PAD), lambda i, h: (0, h)),
                  pl.BlockSpec((tm, LANES), lambda i, h: (i, 0)),
                  pl.BlockSpec((tm, LANES), lambda i, h: (i, 0))],
        out_specs=pl.BlockSpec((hs, tm, QK_PAD), lambda i, h: (h, i, 0)),
        out_shape=jax.ShapeDtypeStruct((MLA_HEADS, M, QK_PAD), bf16),
        compiler_params=_cparams(("parallel", "parallel")),
    )(qn, wuq, cos, sin)


def _up_kv_rope(kvn, wukv, kpe_raw, cos, sin):
    M, K = kvn.shape
    tm = _pick(M, (768, 512, 256))

    hs = HEADS_PER_STEP
    w = QK_NOPE + V_DIM

    def body(a_ref, b_ref, pe_ref, c_ref, s_ref, k_ref, v_ref):
        r = _dot(a_ref[...], b_ref[...])
        pe = _rope_apply(pe_ref[...], c_ref[...], s_ref[...]).astype(bf16)
        for u in range(hs):
            k_ref[u, :, 0:QK_NOPE] = r[:, u * w:u * w + QK_NOPE].astype(bf16)
            k_ref[u, :, QK_NOPE:QK_PAD] = pe
            v_ref[u] = r[:, u * w + QK_NOPE:(u + 1) * w].astype(bf16)

    return pl.pallas_call(
        body, name="up_kv_rope", grid=(M // tm, MLA_HEADS // hs),
        in_specs=[pl.BlockSpec((tm, K), lambda i, h: (i, 0)),
                  pl.BlockSpec((K, hs * w), lambda i, h: (0, h)),
                  pl.BlockSpec((tm, LANES), lambda i, h: (i, 0)),
                  pl.BlockSpec((tm, LANES), lambda i, h: (i, 0)),
                  pl.BlockSpec((tm, LANES), lambda i, h: (i, 0))],
        out_specs=[pl.BlockSpec((hs, tm, QK_PAD), lambda i, h: (h, i, 0)),
                   pl.BlockSpec((hs, tm, V_DIM), lambda i, h: (h, i, 0))],
        out_shape=[jax.ShapeDtypeStruct((MLA_HEADS, M, QK_PAD), bf16),
                   jax.ShapeDtypeStruct((MLA_HEADS, M, V_DIM), bf16)],
        compiler_params=_cparams(("parallel", "parallel")),
    )(kvn, wukv, kpe_raw, cos, sin)


def _rope_q_bwd(dq, cos, sin):
    M = dq.shape[1]
    tm = _rt(M)

    def body(d_ref, c_ref, s_ref, o_ref):
        c, s = c_ref[...], s_ref[...]
        for h in range(MLA_HEADS):
            o_ref[:, h * QK_PAD:h * QK_PAD + QK_NOPE] = (d_ref[h, :, 0:QK_NOPE] * SOFTMAX_SCALE).astype(bf16)
            o_ref[:, h * QK_PAD + QK_NOPE:(h + 1) * QK_PAD] = (_rope_unapply(
                d_ref[h, :, QK_NOPE:QK_PAD], c, s) * SOFTMAX_SCALE).astype(bf16)

    return pl.pallas_call(
        body, name="rope_q_bwd", grid=(M // tm,),
        in_specs=[pl.BlockSpec((MLA_HEADS, tm, QK_PAD), lambda i: (0, i, 0)),
                  _row(LANES, tm), _row(LANES, tm)],
        out_specs=_row(MLA_HEADS * QK_PAD, tm),
        out_shape=jax.ShapeDtypeStruct((M, MLA_HEADS * QK_PAD), bf16),
        compiler_params=_cparams(("parallel",)),
    )(dq, cos, sin)


def _rope_k_bwd(dk, dv, cos, sin):
    M = dk.shape[1]
    tm = _rt(M)
    w = QK_NOPE + V_DIM

    def body(dk_ref, dv_ref, c_ref, s_ref, o_ref, pe_ref):
        pe = None
        for h in range(MLA_HEADS):
            o_ref[:, h * w:h * w + QK_NOPE] = dk_ref[h, :, 0:QK_NOPE].astype(bf16)
            o_ref[:, h * w + QK_NOPE:(h + 1) * w] = dv_ref[h].astype(bf16)
            t = dk_ref[h, :, QK_NOPE:QK_PAD]
            pe = t if pe is None else pe + t
        pe_ref[...] = _rope_unapply(pe, c_ref[...], s_ref[...])

    return pl.pallas_call(
        body, name="rope_k_bwd", grid=(M // tm,),
        in_specs=[pl.BlockSpec((MLA_HEADS, tm, QK_PAD), lambda i: (0, i, 0)),
                  pl.BlockSpec((MLA_HEADS, tm, V_DIM), lambda i: (0, i, 0)),
                  _row(LANES, tm), _row(LANES, tm)],
        out_specs=[_row(MLA_HEADS * w, tm), _row(LANES, tm)],
        out_shape=[jax.ShapeDtypeStruct((M, MLA_HEADS * w), bf16), jax.ShapeDtypeStruct((M, LANES), f32)],
        compiler_params=_cparams(("parallel",)),
    )(dk, dv, cos, sin)


def _attn_tile(M):
    return 768 if (M % 768 == 0 and M >= 4 * 768) else ROW_TILE


def _col_to_row(col):
    return col.T[0:1, :]


def _hosted_exchange(refs_in, refs_out, sems, scatter, first, last):
    copies = _exchange_copies(refs_in, refs_out, *sems, scatter)

    @pl.when(first)
    def _():
        for cp in copies:
            cp.start()

    @pl.when(last)
    def _():
        for cp in copies:
            cp.wait()


def _flash_fwd(q, k, v, carried, scatter):
    H, M, _ = q.shape
    T = _attn_tile(M)
    nq = M // T
    nx = len(carried)

    def body(*refs):
        q_ref, k_ref, v_ref = refs[:3]
        o_ref, lse_ref = refs[3 + nx:5 + nx]
        sa_ref, sb_ref, m_sc, l_sc, acc_sc = refs[5 + 2 * nx:10 + 2 * nx]
        h = pl.program_id(0)
        i = pl.program_id(1)
        _hosted_exchange(refs[3:3 + nx], refs[5 + nx:5 + 2 * nx], refs[10 + 2 * nx:], scatter,
                         (h == 0) & (i == 0), (h == H - 1) & (i == nq - 1))
        qv = q_ref[0]
        m_sc[...] = jnp.full_like(m_sc, NEG)
        l_sc[...] = jnp.zeros_like(l_sc)
        acc_sc[...] = jnp.zeros_like(acc_sc)

        def scores(j, s_ref):
            off = pl.multiple_of(j * T, T)
            s_ref[...] = _dot_nt(qv, k_ref[0, pl.ds(off, T), :])

        def softmax_pv(j, s_ref, masked):
            off = pl.multiple_of(j * T, T)
            s = s_ref[...]
            if masked:
                r = lax.broadcasted_iota(jnp.int32, (T, T), 0)
                c = lax.broadcasted_iota(jnp.int32, (T, T), 1)
                s = jnp.where(r >= c, s, NEG)
            m_prev = m_sc[...]
            m_new = jnp.maximum(m_prev, jnp.max(s, axis=1, keepdims=True))
            alpha = jnp.exp2(m_prev - m_new)
            p = jnp.exp2(s - m_new[:, 0:1])
            l_sc[...] = alpha * l_sc[...] + jnp.sum(p, axis=1, keepdims=True)
            acc_sc[...] = alpha * acc_sc[...] + _dot(p.astype(bf16), v_ref[0, pl.ds(off, T), :])
            m_sc[...] = m_new

        scores(0, sa_ref)

        def pair(jj, c):
            j0 = 2 * jj
            scores(j0 + 1, sb_ref)
            softmax_pv(j0, sa_ref, False)
            scores(j0 + 2, sa_ref)
            softmax_pv(j0 + 1, sb_ref, False)
            return c

        lax.fori_loop(0, i // 2, pair, 0)

        @pl.when(i % 2 == 0)
        def _():
            softmax_pv(i, sa_ref, True)

        @pl.when(i % 2 == 1)
        def _():
            scores(i, sb_ref)
            softmax_pv(i - 1, sa_ref, False)
            softmax_pv(i, sb_ref, True)

        l = l_sc[...]
        o_ref[...] = acc_sc[...] / l
        lse_ref[0, 0] = _col_to_row(m_sc[...] + jnp.log2(l))

    any_spec = pl.BlockSpec(memory_space=pl.ANY)
    return pl.pallas_call(
        body, name="flash_fwd", grid=(H, nq),
        in_specs=[pl.BlockSpec((1, T, QK_PAD), lambda h, i: (h, i, 0)),
                  pl.BlockSpec((1, M, QK_PAD), lambda h, i: (h, 0, 0)),
                  pl.BlockSpec((1, M, V_DIM), lambda h, i: (h, 0, 0))] + [any_spec] * nx,
        out_specs=[pl.BlockSpec((T, V_DIM), lambda h, i: (i, h)),
                   pl.BlockSpec((1, 1, 1, T), lambda h, i: (h, i, 0, 0))] + [any_spec] * nx,
        out_shape=[jax.ShapeDtypeStruct((M, H * V_DIM), f32),
                   jax.ShapeDtypeStruct((H, nq, 1, T), f32)] + _exchange_shapes(carried, scatter),
        scratch_shapes=[pltpu.VMEM((T, T), f32), pltpu.VMEM((T, T), f32),
                        pltpu.VMEM((T, LANES), f32), pltpu.VMEM((T, LANES), f32),
                        pltpu.VMEM((T, V_DIM), f32)] + _exchange_sems(nx),
        compiler_params=_cparams(("arbitrary", "arbitrary")),
    )(q, k, v, *carried)


def _attn_out_bwd(o, g, d_an):
    M, K = o.shape
    H = MLA_HEADS
    T = _attn_tile(M)

    def body(o_ref, g_ref, d_ref, dh_ref, dl_ref, dg_ref):
        @pl.when(pl.program_id(0) == 0)
        def _():
            dg_ref[...] = jnp.zeros_like(dg_ref)

        ov = o_ref[...]
        do, dgp = _rms_bwd_math(ov, g_ref[...], d_ref[...])
        dg_ref[...] += jnp.sum(dgp, axis=0, keepdims=True)
        for h in range(H):
            sl = slice(h * V_DIM, (h + 1) * V_DIM)
            doh = do[:, sl]
            dh_ref[h] = doh.astype(bf16)
            col = jnp.sum(ov[:, sl] * doh, axis=1, keepdims=True) + jnp.zeros((T, LANES), f32)
            dl_ref[h, 0] = _col_to_row(col)

    return pl.pallas_call(
        body, name="attn_out_bwd", grid=(M // T,),
        in_specs=[_row(K, T), _full((1, K)), _row(K, T)],
        out_specs=[pl.BlockSpec((H, T, V_DIM), lambda i: (0, i, 0)),
                   pl.BlockSpec((H, 1, 1, T), lambda i: (0, i, 0, 0)),
                   _full((1, K))],
        out_shape=[jax.ShapeDtypeStruct((H, M, V_DIM), bf16),
                   jax.ShapeDtypeStruct((H, M // T, 1, T), f32),
                   jax.ShapeDtypeStruct((1, K), f32)],
        compiler_params=_cparams(("arbitrary",)),
    )(o, g, d_an)


def _flash_bwd(q, k, v, do, lse, delta, carried, scatter):
    H, M, _ = q.shape
    T = _attn_tile(M)
    nq = M // T
    nx = len(carried)

    def body(*refs):
        q_ref, do_ref, lse_ref, dl_ref, k_ref, v_ref = refs[:6]
        dq_ref, dk_ref, dv_ref = refs[6 + nx:9 + nx]
        dk_sc, dv_sc = refs[9 + 2 * nx:11 + 2 * nx]
        j = pl.program_id(1)
        _hosted_exchange(refs[6:6 + nx], refs[9 + nx:9 + 2 * nx], refs[11 + 2 * nx:], scatter,
                         (pl.program_id(0) == 0) & (j == 0), (pl.program_id(0) == H - 1) & (j == nq - 1))

        @pl.when(j == 0)
        def _():
            dq_ref[...] = jnp.zeros_like(dq_ref)

        kt = k_ref[0]
        vt = v_ref[0]
        dk_sc[...] = jnp.zeros_like(dk_sc)
        dv_sc[...] = jnp.zeros_like(dv_sc)

        def step(i, masked):
            off = pl.multiple_of(i * T, T)
            qt = q_ref[0, pl.ds(off, T), :]
            dot_ = do_ref[0, pl.ds(off, T), :]
            st = _dot_nt(kt, qt)
            if masked:
                r = lax.broadcasted_iota(jnp.int32, (T, T), 0)
                c = lax.broadcasted_iota(jnp.int32, (T, T), 1)
                st = jnp.where(c >= r, st, NEG)
            pt = jnp.exp2(st - lse_ref[0, i])
            dv_sc[...] += _dot(pt.astype(bf16), dot_)
            dpt = _dot_nt(vt, dot_)
            dst = (pt * (dpt - dl_ref[0, i])).astype(bf16)
            dk_sc[...] += _dot(dst, qt)
            dq_ref[0, pl.ds(off, T), :] += _dot_tn(dst, kt)

        step(j, True)

        def loop_body(i, c):
            step(i, False)
            return c

        lax.fori_loop(j + 1, nq, loop_body, 0)
        dk_ref[0] = dk_sc[...] * LN2
        dv_ref[0] = dv_sc[...]

    any_spec = pl.BlockSpec(memory_space=pl.ANY)
    return pl.pallas_call(
        body, name="flash_bwd", grid=(H, nq),
        in_specs=[pl.BlockSpec((1, M, QK_PAD), lambda h, j: (h, 0, 0)),
                  pl.BlockSpec((1, M, V_DIM), lambda h, j: (h, 0, 0)),
                  pl.BlockSpec((1, nq, 1, T), lambda h, j: (h, 0, 0, 0)),
                  pl.BlockSpec((1, nq, 1, T), lambda h, j: (h, 0, 0, 0)),
                  pl.BlockSpec((1, T, QK_PAD), lambda h, j: (h, j, 0)),
                  pl.BlockSpec((1, T, V_DIM), lambda h, j: (h, j, 0))] + [any_spec] * nx,
        out_specs=[pl.BlockSpec((1, M, QK_PAD), lambda h, j: (h, 0, 0)),
                   pl.BlockSpec((1, T, QK_PAD), lambda h, j: (h, j, 0)),
                   pl.BlockSpec((1, T, V_DIM), lambda h, j: (h, j, 0))] + [any_spec] * nx,
        out_shape=[jax.ShapeDtypeStruct((H, M, QK_PAD), f32),
                   jax.ShapeDtypeStruct((H, M, QK_PAD), f32),
                   jax.ShapeDtypeStruct((H, M, V_DIM), f32)] + _exchange_shapes(carried, scatter),
        scratch_shapes=[pltpu.VMEM((T, QK_PAD), f32), pltpu.VMEM((T, V_DIM), f32)] + _exchange_sems(nx),
        compiler_params=_cparams(("arbitrary", "arbitrary")),
    )(q, do, lse, delta, k, v, *carried)


def _dt_fwd(dt_raw, bias, expand):
    M = dt_raw.shape[0]
    tm = _rt(M)

    def body(x_ref, b_ref, e_ref, o_ref, oe_ref):
        u = x_ref[...] + b_ref[...]
        sp = jnp.maximum(u, 0.0) + jnp.log(1.0 + jnp.exp(-jnp.abs(u)))
        lane = lax.broadcasted_iota(jnp.int32, u.shape, 1)
        dtp = jnp.where(lane < SSM_HEADS, sp, 0.0)
        o_ref[...] = dtp
        oe_ref[...] = _dot_hi(dtp, e_ref[...])

    return pl.pallas_call(
        body, name="dt_fwd", grid=(M // tm,),
        in_specs=[_row(LANES, tm), _full((1, LANES)), _full((LANES, D_SSM))],
        out_specs=[_row(LANES, tm), _row(D_SSM, tm)],
        out_shape=[jax.ShapeDtypeStruct((M, LANES), f32), jax.ShapeDtypeStruct((M, D_SSM), f32)],
        compiler_params=_cparams(("parallel",)),
    )(dt_raw, bias, expand)


def _dt_bwd(dt_raw, bias, ddt):
    M = dt_raw.shape[0]
    tm = _rt(M)

    def body(x_ref, b_ref, d_ref, o_ref, db_ref):
        @pl.when(pl.program_id(0) == 0)
        def _():
            db_ref[...] = jnp.zeros_like(db_ref)

        u = x_ref[...] + b_ref[...]
        lane = lax.broadcasted_iota(jnp.int32, u.shape, 1)
        g = jnp.where(lane < SSM_HEADS, d_ref[...] * _sigmoid(u), 0.0)
        o_ref[...] = g
        db_ref[...] += jnp.sum(g, axis=0, keepdims=True)

    return pl.pallas_call(
        body, name="dt_bwd", grid=(M // tm,),
        in_specs=[_row(LANES, tm), _full((1, LANES)), _row(LANES, tm)],
        out_specs=[_row(LANES, tm), _full((1, LANES))],
        out_shape=[jax.ShapeDtypeStruct((M, LANES), f32), jax.ShapeDtypeStruct((1, LANES), f32)],
        compiler_params=_cparams(("arbitrary",)),
    )(dt_raw, bias, ddt)


SSM_GW = SSM_HPG * SSM_P
SSM_PAIRS = SSM_GW // LANES


def _ssd_common(dte_ref, dtt_ref, ae_ref, acol_ref):
    Q = CHUNK
    r = lax.broadcasted_iota(jnp.int32, (Q, Q), 0)
    c = lax.broadcasted_iota(jnp.int32, (Q, Q), 1)
    causal = r >= c
    anti = c >= r
    tril = causal.astype(f32)
    triu = anti.astype(f32)
    dt_e = dte_ref[...]
    cs_e = _dot_hi(tril, dt_e * ae_ref[...])
    cst = _dot_hi(dtt_ref[...] * acol_ref[...], triu)
    cs_last = cs_e[Q - 1:Q, :]
    return causal, anti, triu, dt_e, cs_e, cst, jnp.exp(cs_e), jnp.exp(cs_last - cs_e), jnp.exp(cs_last)


def _half_masks():
    lane = lax.broadcasted_iota(jnp.int32, (CHUNK, LANES), 1)
    lo = lane < SSM_P
    return lo, jnp.logical_not(lo)


def _ssd_fwd(xbc_c, dt_e, dtt, a_e, a_col):
    M = xbc_c.shape[0]
    Q = CHUNK
    nch = M // Q

    def body(x_ref, dte_ref, dtt_ref, ae_ref, acol_ref, y_ref, hin_ref, ht_sc):
        @pl.when(pl.program_id(0) == 0)
        def _():
            ht_sc[...] = jnp.zeros_like(ht_sc)

        causal, _, _, dt_e, cs_e, cst, ecs_e, dte_e, elast_e = _ssd_common(dte_ref, dtt_ref, ae_ref, acol_ref)
        halves = _half_masks()
        for g in range(SSM_GROUPS):
            g0 = g * SSM_GW
            bg = x_ref[:, D_SSM + g * SSM_N:D_SSM + (g + 1) * SSM_N]
            cg = x_ref[:, D_SSM + D_BC + g * SSM_N:D_SSM + D_BC + (g + 1) * SSM_N]
            bg_b = bg.astype(bf16)
            cg_b = cg.astype(bf16)
            cb = _dot_nt(cg_b, bg_b)
            bgt_b = bg.T.astype(bf16)
            xdt_g = x_ref[:, g0:g0 + SSM_GW] * dt_e[:, g0:g0 + SSM_GW]
            ht = ht_sc[g]
            hin_ref[0, g] = ht
            y_off = _dot(cg_b, ht.astype(bf16)) * ecs_e[:, g0:g0 + SSM_GW]
            for pr in range(SSM_PAIRS):
                p0 = pr * LANES
                xdt_p = xdt_g[:, p0:p0 + LANES]
                acc = y_off[:, p0:p0 + LANES]
                for half in range(2):
                    h = g * SSM_HPG + pr * 2 + half
                    seg = cs_e[:, h * SSM_P:h * SSM_P + 1] - cst[h:h + 1, :]
                    lm = jnp.exp(jnp.where(causal, seg, -jnp.inf))
                    xm = jnp.where(halves[half], xdt_p, 0.0).astype(bf16)
                    acc = acc + _dot((cb * lm).astype(bf16), xm)
                y_ref[:, g0 + p0:g0 + p0 + LANES] = acc
            st = _dot(bgt_b, (xdt_g * dte_e[:, g0:g0 + SSM_GW]).astype(bf16))
            ht_sc[g] = ht * elast_e[:, g0:g0 + SSM_GW] + st

    return pl.pallas_call(
        body, name="ssd_fwd", grid=(nch,),
        in_specs=[pl.BlockSpec((Q, D_XBC), lambda c: (c, 0)),
                  pl.BlockSpec((Q, D_SSM), lambda c: (c, 0)),
                  pl.BlockSpec((SSM_HEADS, Q), lambda c: (0, c)),
                  _full((1, D_SSM)), _full((SSM_HEADS, LANES))],
        out_specs=[pl.BlockSpec((Q, D_SSM), lambda c: (c, 0)),
                   pl.BlockSpec((1, SSM_GROUPS, SSM_N, SSM_GW), lambda c: (c, 0, 0, 0))],
        out_shape=[jax.ShapeDtypeStruct((M, D_SSM), f32),
                   jax.ShapeDtypeStruct((nch, SSM_GROUPS, SSM_N, SSM_GW), f32)],
        scratch_shapes=[pltpu.VMEM((SSM_GROUPS, SSM_N, SSM_GW), f32)],
        compiler_params=_cparams(("arbitrary",)),
    )(xbc_c, dt_e, dtt, a_e, a_col)


def _ssd_bwd(xbc_c, dtp, dt_e, dtt, a_row, a_e, a_col, hin, dy, d_exp, head_ind):
    M = xbc_c.shape[0]
    Q = CHUNK
    nch = M // Q
    rev = lambda c: nch - 1 - c

    def body(x_ref, dtp_ref, dte_ref, dtt_ref, arow_ref, ae_ref, acol_ref, hin_ref, dy_ref, dexp_ref,
             ind_ref, dx_ref, ddt_ref, da_ref, dht_sc, z_sc, z1_sc, last_sc, ct_sc):
        @pl.when(pl.program_id(0) == 0)
        def _():
            dht_sc[...] = jnp.zeros_like(dht_sc)
            da_ref[...] = jnp.zeros_like(da_ref)
            last_sc[...] = jnp.zeros_like(last_sc)
            ct_sc[...] = jnp.zeros_like(ct_sc)

        causal, anti, triu, dt_e, cs_e, cst, ecs_e, dte_e, elast_e = _ssd_common(dte_ref, dtt_ref, ae_ref, acol_ref)
        halves = _half_masks()
        lane = lax.broadcasted_iota(jnp.int32, (Q, LANES), 1)
        rsum = jnp.zeros((Q, LANES), f32)
        for g in range(SSM_GROUPS):
            g0 = g * SSM_GW
            gs = slice(g0, g0 + SSM_GW)
            b0 = D_SSM + g * SSM_N
            c0 = D_SSM + D_BC + g * SSM_N
            bg = x_ref[:, b0:b0 + SSM_N]
            cg = x_ref[:, c0:c0 + SSM_N]
            bg_b = bg.astype(bf16)
            cg_b = cg.astype(bf16)
            cgt_b = cg.T.astype(bf16)
            cbt = _dot_nt(bg_b, cg_b)
            cb = _dot_nt(cg_b, bg_b)
            x_g = x_ref[:, gs]
            dt_g = dt_e[:, gs]
            xdt_g = x_g * dt_g
            dy_g = dy_ref[:, gs]
            ht = hin_ref[0, g]
            ht_b = ht.astype(bf16)
            dht = dht_sc[g]
            dht_b = dht.astype(bf16)
            dye_b = (dy_g * ecs_e[:, gs]).astype(bf16)
            dc = _dot_nt(dye_b, ht_b)
            dht_new = dht * elast_e[:, gs] + _dot(cgt_b, dye_b)
            e = _dot(bg_b, dht_b)
            xdtd = xdt_g * dte_e[:, gs]
            db = _dot_nt(xdtd.astype(bf16), dht_b)
            dxdt_state = e * dte_e[:, gs]
            exd = e * xdtd
            z1_sc[:, gs] = dy_g * (_dot(cg_b, ht_b) * ecs_e[:, gs]) - exd
            last_sc[0:1, gs] = (jnp.sum(exd, axis=0, keepdims=True)
                                + jnp.sum(dht * ht, axis=0, keepdims=True) * elast_e[:, gs])
            dg_acc = jnp.zeros((Q, Q), f32)
            for pr in range(SSM_PAIRS):
                p0 = pr * LANES
                ps = slice(g0 + p0, g0 + p0 + LANES)
                dy_p = dy_g[:, p0:p0 + LANES]
                xdt_pb = xdt_g[:, p0:p0 + LANES].astype(bf16)
                acc = dxdt_state[:, p0:p0 + LANES]
                for half in range(2):
                    h = g * SSM_HPG + pr * 2 + half
                    seg = cs_e[:, h * SSM_P:h * SSM_P + 1] - cst[h:h + 1, :]
                    lm = jnp.exp(jnp.where(causal, seg, -jnp.inf))
                    lmt = jnp.exp(jnp.where(anti, -seg, -jnp.inf))
                    dym = jnp.where(halves[half], dy_p, 0.0).astype(bf16)
                    acc = acc + _dot((cbt * lmt).astype(bf16), dym)
                    dml = _dot_nt(dym, xdt_pb) * lm
                    dg_acc = dg_acc + dml
                    w = dml * cb
                    rsum = rsum + jnp.where(lane == h, jnp.sum(w, axis=1, keepdims=True), 0.0)
                    ct_sc[h:h + 1, :] = jnp.sum(w, axis=0, keepdims=True)
                dx_ref[:, ps] = acc * dt_g[:, p0:p0 + LANES] + dexp_ref[:, ps] * dy_p
                z_sc[:, ps] = acc * x_g[:, p0:p0 + LANES]
            dg_b = dg_acc.astype(bf16)
            dx_ref[:, c0:c0 + SSM_N] = dc + _dot(dg_b, bg_b)
            dx_ref[:, b0:b0 + SSM_N] = db + _dot_tn(dg_b, cg_b)
            dht_sc[g] = dht_new
        s1 = _dot_hi(z1_sc[...], ind_ref[...])
        s2 = _dot_hi(z_sc[...], ind_ref[...])
        last = _dot_hi(last_sc[...], ind_ref[...])[0:1, :]
        dtp = dtp_ref[...]
        row = lax.broadcasted_iota(jnp.int32, (Q, LANES), 0)
        dcs = s1 + rsum + jnp.where(row == Q - 1, last, 0.0)
        tril = causal.astype(f32)
        da = _dot_hi(triu, dcs) - _dot_hi(ct_sc[...], tril).T
        ddt_ref[...] = s2 + da * arow_ref[...]
        da_ref[...] += jnp.sum(da * dtp, axis=0, keepdims=True)

    return pl.pallas_call(
        body, name="ssd_bwd", grid=(nch,),
        in_specs=[pl.BlockSpec((Q, D_XBC), lambda c: (rev(c), 0)),
                  pl.BlockSpec((Q, LANES), lambda c: (rev(c), 0)),
                  pl.BlockSpec((Q, D_SSM), lambda c: (rev(c), 0)),
                  pl.BlockSpec((SSM_HEADS, Q), lambda c: (0, rev(c))),
                  _full((1, LANES)), _full((1, D_SSM)), _full((SSM_HEADS, LANES)),
                  pl.BlockSpec((1, SSM_GROUPS, SSM_N, SSM_GW), lambda c: (rev(c), 0, 0, 0)),
                  pl.BlockSpec((Q, D_SSM), lambda c: (rev(c), 0)),
                  _full((1, D_SSM)), _full((D_SSM, LANES))],
        out_specs=[pl.BlockSpec((Q, D_XBC), lambda c: (rev(c), 0)),
                   pl.BlockSpec((Q, LANES), lambda c: (rev(c), 0)),
                   _full((1, LANES))],
        out_shape=[jax.ShapeDtypeStruct((M, D_XBC), f32), jax.ShapeDtypeStruct((M, LANES), f32),
                   jax.ShapeDtypeStruct((1, LANES), f32)],
        scratch_shapes=[pltpu.VMEM((SSM_GROUPS, SSM_N, SSM_GW), f32), pltpu.VMEM((Q, D_SSM), f32),
                        pltpu.VMEM((Q, D_SSM), f32), pltpu.VMEM((SUBLANES, D_SSM), f32),
                        pltpu.VMEM((LANES, Q), f32)],
        compiler_params=_cparams(("arbitrary",)),
    )(xbc_c, dtp, dt_e, dtt, a_row, a_e, a_col, hin, dy, d_exp, head_ind)


def _gate_norm_fwd(y, xbc_c, z, d_exp, g):
    M = y.shape[0]
    tm = _rt(M)
    gw = D_SSM // SSM_GROUPS

    def body(y_ref, x_ref, z_ref, d_ref, g_ref, o_ref):
        yg = (y_ref[...] + d_ref[...] * x_ref[...]) * _silu(z_ref[...])
        for gi in range(SSM_GROUPS):
            blk = yg[:, gi * gw:(gi + 1) * gw]
            o_ref[:, gi * gw:(gi + 1) * gw] = (blk * _rstd(blk) * g_ref[:, gi * gw:(gi + 1) * gw]).astype(bf16)

    return pl.pallas_call(
        body, name="gate_norm_fwd", grid=(M // tm,),
        in_specs=[_row(D_SSM, tm), _row(D_SSM, tm), _row(D_SSM, tm), _full((1, D_SSM)), _full((1, D_SSM))],
        out_specs=_row(D_SSM, tm), out_shape=jax.ShapeDtypeStruct((M, D_SSM), bf16),
        compiler_params=_cparams(("parallel",)),
    )(y, xbc_c, z, d_exp, g)


def _gate_norm_bwd(y, xbc_c, z, d_exp, g, dout, head_ind):
    M = y.shape[0]
    tm = _rt(M)
    nt = M // tm
    gw = D_SSM // SSM_GROUPS

    def body(y_ref, x_ref, z_ref, d_ref, g_ref, do_ref, ind_ref, dy_ref, dz_ref, dg_ref, dd_ref, ddc_sc):
        i = pl.program_id(0)

        @pl.when(i == 0)
        def _():
            dg_ref[...] = jnp.zeros_like(dg_ref)
            ddc_sc[...] = jnp.zeros_like(ddc_sc)

        zv = z_ref[...]
        xv = x_ref[...]
        s = _silu(zv)
        yd = y_ref[...] + d_ref[...] * xv
        yg = yd * s
        dov = do_ref[...]
        for gi in range(SSM_GROUPS):
            sl = slice(gi * gw, (gi + 1) * gw)
            dyg, dgp = _rms_bwd_math(yg[:, sl], g_ref[:, sl], dov[:, sl])
            dg_ref[:, sl] += jnp.sum(dgp, axis=0, keepdims=True)
            dyd = dyg * s[:, sl]
            dy_ref[:, sl] = dyd
            dz_ref[:, sl] = (dyg * yd[:, sl] * _dsilu(zv[:, sl])).astype(bf16)
            ddc_sc[:, sl] += jnp.sum(dyd * xv[:, sl], axis=0, keepdims=True)

        @pl.when(i == nt - 1)
        def _():
            dd_ref[...] = _dot_hi(ddc_sc[...], ind_ref[...])

    return pl.pallas_call(
        body, name="gate_norm_bwd", grid=(nt,),
        in_specs=[_row(D_SSM, tm), _row(D_SSM, tm), _row(D_SSM, tm), _full((1, D_SSM)), _full((1, D_SSM)),
                  _row(D_SSM, tm), _full((D_SSM, LANES))],
        out_specs=[_row(D_SSM, tm), _row(D_SSM, tm), _full((1, D_SSM)), _full((1, LANES))],
        out_shape=[jax.ShapeDtypeStruct((M, D_SSM), f32), jax.ShapeDtypeStruct((M, D_SSM), bf16),
                   jax.ShapeDtypeStruct((1, D_SSM), f32), jax.ShapeDtypeStruct((1, LANES), f32)],
        scratch_shapes=[pltpu.VMEM((1, D_SSM), f32)],
        compiler_params=_cparams(("arbitrary",)),
    )(y, xbc_c, z, d_exp, g, dout, head_ind)


_PEER_FLIPS = [(0, 0, 1), (0, 1, 0), (0, 1, 1), (1, 0, 0), (1, 0, 1), (1, 1, 0), (1, 1, 1)]


def _exchange_copies(ins, outs, send_sems, recv_sems, loc_sems, scatter):
    n = len(ins)
    x, y, c = lax.axis_index("x"), lax.axis_index("y"), lax.axis_index("c")
    me = 4 * x + 2 * y + c
    copies = []
    for a in range(n):
        src = ins[a].at[me] if scatter else ins[a]
        copies.append(pltpu.make_async_copy(src, outs[a].at[me], loc_sems.at[a]))
    for p, (fx, fy, fc) in enumerate(_PEER_FLIPS):
        tx = 1 - x if fx else x
        ty = 1 - y if fy else y
        tc = 1 - c if fc else c
        tgt = 4 * tx + 2 * ty + tc
        for a in range(n):
            src = ins[a].at[tgt] if scatter else ins[a]
            copies.append(pltpu.make_async_remote_copy(
                src_ref=src, dst_ref=outs[a].at[me],
                send_sem=send_sems.at[p * n + a], recv_sem=recv_sems.at[p * n + a],
                device_id=(tx, ty, tc), device_id_type=_MESH))
    return copies


def _exchange_shapes(arrays, scatter):
    return [jax.ShapeDtypeStruct(a.shape if scatter else (N_DEV,) + a.shape, a.dtype) for a in arrays]


def _exchange_sems(n):
    return [pltpu.SemaphoreType.DMA((7 * n,)), pltpu.SemaphoreType.DMA((7 * n,)), pltpu.SemaphoreType.DMA((n,))]


def _exchange(arrays, scatter, name):
    n = len(arrays)

    def body(*refs):
        copies = _exchange_copies(refs[:n], refs[n:2 * n], *refs[2 * n:], scatter)
        for cp in copies:
            cp.start()
        for cp in copies:
            cp.wait()

    any_spec = pl.BlockSpec(memory_space=pl.ANY)
    return pl.pallas_call(
        body, name=name, in_specs=[any_spec] * n, out_specs=[any_spec] * n,
        out_shape=_exchange_shapes(arrays, scatter), scratch_shapes=_exchange_sems(n),
    )(*arrays)


def _exchange_tail(scattered, gathered, name):
    ns, ng = len(scattered), len(gathered)
    n = ns + ng

    def body(*refs):
        sems = refs[2 * n:]
        copies = (_exchange_copies(refs[:ns], refs[n:n + ns], *sems[:3], True)
                  + _exchange_copies(refs[ns:n], refs[n + ns:2 * n], *sems[3:], False))
        for cp in copies:
            cp.start()
        for cp in copies:
            cp.wait()

    any_spec = pl.BlockSpec(memory_space=pl.ANY)
    return pl.pallas_call(
        body, name=name, in_specs=[any_spec] * n, out_specs=[any_spec] * n,
        out_shape=_exchange_shapes(scattered, True) + _exchange_shapes(gathered, False),
        scratch_shapes=_exchange_sems(ns) + _exchange_sems(ng),
    )(*scattered, *gathered)


def _adamw(parts, w, m, v, name):
    R, C = w.shape
    tr = _pick(R, (PACK_ROW_TILE, 64, 32, 16, 8))
    c1 = 1.0 - ADAM_B1 ** ADAM_STEP
    c2 = 1.0 - ADAM_B2 ** ADAM_STEP

    def body(p_ref, w_ref, m_ref, v_ref, g_ref, d_ref, nm_ref, nv_ref):
        g = p_ref[0].astype(f32)
        for s in range(1, N_DEV):
            g = g + p_ref[s].astype(f32)
        mn = ADAM_B1 * m_ref[...] + (1.0 - ADAM_B1) * g
        vn = ADAM_B2 * v_ref[...] + (1.0 - ADAM_B2) * (g * g)
        m_hat = mn / c1
        v_hat = vn / c2
        g_ref[...] = g
        d_ref[...] = -ADAM_LR * (m_hat / (jnp.sqrt(v_hat) + ADAM_EPS) + ADAM_WD * w_ref[...])
        nm_ref[...] = mn
        nv_ref[...] = vn

    spec = pl.BlockSpec((tr, C), lambda i: (i, 0))
    return pl.pallas_call(
        body, name=name, grid=(R // tr,),
        in_specs=[pl.BlockSpec((N_DEV, tr, C), lambda i: (0, i, 0)), spec, spec, spec],
        out_specs=[spec] * 4, out_shape=[jax.ShapeDtypeStruct((R, C), f32)] * 4,
        compiler_params=_cparams(("parallel",)),
    )(parts, w, m, v)


def _flat_rows(a, lead_ndim):
    lead = a.shape[:lead_ndim]
    n = int(np.prod(a.shape[lead_ndim:]))
    a = a.reshape(lead + (n,))
    pad = (-n) % PACK_W
    if pad:
        a = jnp.pad(a, [(0, 0)] * lead_ndim + [(0, pad)])
    return a.reshape(lead + ((n + pad) // PACK_W, PACK_W))


def _pack(arrays, lead_ndim, total_rows, dtype):
    rows = [_flat_rows(a.astype(dtype), lead_ndim) for a in arrays]
    cat = jnp.concatenate(rows, axis=lead_ndim)
    pad = total_rows - cat.shape[lead_ndim]
    if pad:
        cat = jnp.pad(cat, [(0, 0)] * lead_ndim + [(0, pad), (0, 0)])
    return cat


def _unpack(buf, shapes, lead_ndim):
    out = []
    r = 0
    lead = buf.shape[:lead_ndim]
    for shp in shapes:
        n = int(np.prod(shp))
        nr = -(-n // PACK_W)
        piece = lax.slice_in_dim(buf, r, r + nr, axis=lead_ndim)
        piece = piece.reshape(lead + (nr * PACK_W,))
        piece = lax.slice_in_dim(piece, 0, n, axis=lead_ndim)
        out.append(piece.reshape(lead + tuple(shp)))
        r += nr
    return out


def _round_up(n, m):
    return -(-n // m) * m


def kernel(x, meta_tokens, norm_mix_pre, norm_mix_post, norm_ffn_pre, norm_ffn_post, w_in, q_a_norm, w_uq, kv_a_norm, w_ukv, attn_out_norm, ssm_conv_w, ssm_conv_b, ssm_dt_bias, ssm_A_log, ssm_D, ssm_norm, w_out, w_up, ffn_conv_w, ffn_conv_b, w_down, loss_target, m_meta_tokens, m_norm_mix_pre, m_norm_mix_post, m_norm_ffn_pre, m_norm_ffn_post, m_w_in, m_q_a_norm, m_w_uq, m_kv_a_norm, m_w_ukv, m_attn_out_norm, m_ssm_conv_w, m_ssm_conv_b, m_ssm_dt_bias, m_ssm_A_log, m_ssm_D, m_ssm_norm, m_w_out, m_w_up, m_ffn_conv_w, m_ffn_conv_b, m_w_down, v_meta_tokens, v_norm_mix_pre, v_norm_mix_post, v_norm_ffn_pre, v_norm_ffn_post, v_w_in, v_q_a_norm, v_w_uq, v_kv_a_norm, v_w_ukv, v_attn_out_norm, v_ssm_conv_w, v_ssm_conv_b, v_ssm_dt_bias, v_ssm_A_log, v_ssm_D, v_ssm_norm, v_w_out, v_w_up, v_ffn_conv_w, v_ffn_conv_b, v_w_down):
    seq = x.shape[1]
    n_real = N_META + seq
    Lp = _round_up(n_real, 768) if n_real > 2048 else _round_up(n_real, ROW_TILE)
    D = D_MODEL

    early_w = [w_in, w_uq, w_ukv]
    late_w = [w_out, w_up, w_down]
    sharded_s = [meta_tokens, ssm_conv_w, ffn_conv_w]
    grp_a = dict(names=["w_out", "w_up", "w_down"], w=late_w, m=[m_w_out, m_w_up, m_w_down],
                 v=[v_w_out, v_w_up, v_w_down])
    grp_b = dict(names=["w_in", "w_uq", "w_ukv", "ssm_conv_w", "ffn_conv_w"],
                 w=early_w + [ssm_conv_w, ffn_conv_w],
                 m=[m_w_in, m_w_uq, m_w_ukv, m_ssm_conv_w, m_ffn_conv_w],
                 v=[v_w_in, v_w_uq, v_w_ukv, v_ssm_conv_w, v_ffn_conv_w])
    grp_meta = dict(names=["meta_tokens"], w=[meta_tokens], m=[m_meta_tokens], v=[v_meta_tokens])
    repl_w = [norm_mix_pre, norm_mix_post, norm_ffn_pre, norm_ffn_post, q_a_norm, kv_a_norm, attn_out_norm,
              ssm_conv_b, ssm_dt_bias, ssm_A_log, ssm_D, ssm_norm, ffn_conv_b]
    repl_m = [m_norm_mix_pre, m_norm_mix_post, m_norm_ffn_pre, m_norm_ffn_post, m_q_a_norm, m_kv_a_norm,
              m_attn_out_norm, m_ssm_conv_b, m_ssm_dt_bias, m_ssm_A_log, m_ssm_D, m_ssm_norm, m_ffn_conv_b]
    repl_v = [v_norm_mix_pre, v_norm_mix_post, v_norm_ffn_pre, v_norm_ffn_post, v_q_a_norm, v_kv_a_norm,
              v_attn_out_norm, v_ssm_conv_b, v_ssm_dt_bias, v_ssm_A_log, v_ssm_D, v_ssm_norm, v_ffn_conv_b]

    def pack_rows(arrs, lead):
        return _round_up(sum(-(-int(np.prod(a.shape[lead:])) // PACK_W) for a in arrs), 16)

    wb = _pack(early_w, 0, pack_rows(early_w, 0), bf16)
    wl = _pack(late_w, 0, pack_rows(late_w, 0), bf16)
    ws = _pack(sharded_s, 0, pack_rows(sharded_s, 0), f32)
    wb_all, ws_all = _exchange([wb, ws], False, "gather_weights")
    g_w_in, g_w_uq, g_w_ukv = _unpack(wb_all, [a.shape for a in early_w], 1)
    g_meta, g_sconv, g_fconv = _unpack(ws_all, [a.shape for a in sharded_s], 1)

    def cols(gathered):
        t = gathered[:, 0]
        return jnp.transpose(t, (1, 0, 2)).reshape(t.shape[1], N_DEV * t.shape[2])

    win = cols(g_w_in)
    o = np.cumsum((0, Q_RANK, KV_RANK, QK_ROPE, D_SSM, D_XBC, SSM_HEADS))
    w_q, w_kv = win[:, o[0]:o[1]], win[:, o[1]:o[2]]
    w_rope = jnp.pad(win[:, o[2]:o[3]], ((0, 0), (0, LANES - QK_ROPE)))
    w_z, w_xbc = win[:, o[3]:o[4]], win[:, o[4]:o[5]]
    w_dt = jnp.pad(win[:, o[5]:o[6]], ((0, 0), (0, LANES - SSM_HEADS)))
    wuq = g_w_uq.reshape(Q_RANK, MLA_HEADS, QK_NOPE + QK_ROPE)
    wuq = jnp.pad(wuq, ((0, 0), (0, 0), (0, QK_PAD - QK_NOPE - QK_ROPE))).reshape(Q_RANK, MLA_HEADS * QK_PAD)
    wukv = g_w_ukv.reshape(KV_RANK, MLA_HEADS * (QK_NOPE + V_DIM))
    meta_full = jnp.transpose(g_meta, (1, 0, 2)).reshape(N_META, D)
    sconv_w = jnp.pad(cols(g_sconv), ((0, SUBLANES - SSM_CONV), (0, 0)))
    fconv_w = jnp.pad(cols(g_fconv), ((0, SUBLANES - FFN_CONV), (0, 0)))

    pos = jnp.arange(Lp, dtype=f32)
    inv = ROPE_THETA ** (-jnp.arange(0, QK_ROPE, 2, dtype=f32) / QK_ROPE)
    ang = pos[:, None] * inv[None, :]
    cs_, sn_ = jnp.cos(ang), jnp.sin(ang)
    zpad = jnp.zeros((Lp, LANES - QK_ROPE), f32)
    cos_t = jnp.concatenate([cs_, cs_, zpad], axis=1)
    sin_t = jnp.concatenate([-sn_, sn_, zpad], axis=1)
    dt_bias_p = jnp.pad(ssm_dt_bias, ((0, 0), (0, LANES - SSM_HEADS)))
    a_neg = -jnp.exp(ssm_A_log)
    a_row = jnp.pad(a_neg, ((0, 0), (0, LANES - SSM_HEADS)))
    a_col = jnp.broadcast_to(a_neg.reshape(SSM_HEADS, 1), (SSM_HEADS, LANES))
    d_exp = jnp.repeat(ssm_D, SSM_P, axis=1)
    a_e = jnp.repeat(a_neg, SSM_P, axis=1)
    head_ind = (jnp.arange(D_SSM)[:, None] // SSM_P == jnp.arange(LANES)[None, :]).astype(f32)

    xb = x[0]
    h0 = jnp.concatenate([meta_full, xb, jnp.zeros((Lp - n_real, D), f32)], axis=0)
    tgt = jnp.pad(loss_target[0], ((N_META, Lp - n_real), (0, 0)))
    hn1 = _rms_fwd(h0, norm_mix_pre, bf16, "norm_mix_pre")
    q_c = _mm([(hn1, w_q)], f32, False, "proj_q")
    kv_c = _mm([(hn1, w_kv)], f32, False, "proj_kv")
    kpe_raw = _mm([(hn1, w_rope)], f32, False, "proj_rope")
    z = _mm([(hn1, w_z)], f32, False, "proj_z")
    xbc = _mm([(hn1, w_xbc)], f32, False, "proj_xbc")
    dt_raw = _mm([(hn1, w_dt)], f32, False, "proj_dt")

    qn = _rms_fwd(q_c, q_a_norm, bf16, "norm_q")
    kvn = _rms_fwd(kv_c, kv_a_norm, bf16, "norm_kv")
    qh = _up_q_rope(qn, wuq, cos_t, sin_t)
    kh, vh = _up_kv_rope(kvn, wukv, kpe_raw, cos_t, sin_t)
    attn, lse, wl_all = _flash_fwd(qh, kh, vh, [wl], False)
    g_w_out, g_w_up, g_w_down = _unpack(wl_all, [a.shape for a in late_w], 1)
    wout = g_w_out.reshape(D_ATTN + D_SSM, D)
    wout_a, wout_s = wout[:D_ATTN], wout[D_ATTN:]
    wup = cols(g_w_up)
    wdown = g_w_down.reshape(D_FF, D)
    an = _rms_fwd(attn, attn_out_norm, bf16, "norm_attn_out")

    xbc_c = _ssm_conv_fwd(xbc, sconv_w, ssm_conv_b)
    dtp, dt_e = _dt_fwd(dt_raw, dt_bias_p, jnp.transpose(head_ind))
    dtt = jnp.transpose(dtp[:, :SSM_HEADS])
    y_ssd, hin = _ssd_fwd(xbc_c, dt_e, dtt, a_e, a_col)
    ssm = _gate_norm_fwd(y_ssd, xbc_c, z, d_exp, ssm_norm)

    mix = _mm([(an, wout_a), (ssm, wout_s)], f32, False, "out_proj")
    h1, hn2 = _resid_norm(h0, mix, norm_mix_post, norm_ffn_pre)
    up = _mm([(hn2, wup)], f32, False, "ffn_up")
    act = _ffn_gate_fwd(up, fconv_w, ffn_conv_b)
    down = _mm([(act, wdown)], f32, False, "ffn_down")
    dh2, d_down, dg_ffn_post, loss_part = _final(h1, down, norm_ffn_post, tgt, n_real)

    d_act = _mm([(d_down, wdown)], f32, True, "ffn_down_dx")
    dw_down = _mm_tn(act, d_down, "ffn_down_dw")
    dup_g, dup_v, dwc_g, dwc_v, dbc_g, dbc_v = _ffn_gate_bwd(up, fconv_w, ffn_conv_b, d_act)
    d_hn2 = _mm([(dup_g, wup[:, :D_FF]), (dup_v, wup[:, D_FF:])], f32, True, "ffn_up_dx")
    dw_up = jnp.concatenate([_mm_tn(hn2, dup_g, "ffn_up_dw_g"), _mm_tn(hn2, dup_v, "ffn_up_dw_v")], axis=1)
    dh1, d_mix, dg_ffn_pre, dg_mix_post = _mid_bwd(h1, norm_ffn_pre, d_hn2, dh2, mix, norm_mix_post)
    d_an = _mm([(d_mix, wout_a)], f32, True, "out_proj_dx_a")
    d_ssm = _mm([(d_mix, wout_s)], f32, True, "out_proj_dx_s")
    dw_out = jnp.concatenate([_mm_tn(an, d_mix, "out_proj_dw_a"), _mm_tn(ssm, d_mix, "out_proj_dw_s")], axis=0)

    do_h, delta, dg_attn_out = _attn_out_bwd(attn, attn_out_norm, d_an)
    def col_blocks(gm):
        r, cc = gm.shape
        return jnp.transpose(gm.reshape(r, N_DEV, cc // N_DEV), (1, 0, 2))

    blocks_a = [dw_out.reshape(N_DEV, (D_ATTN + D_SSM) // N_DEV, D), col_blocks(dw_up),
                dw_down.reshape(N_DEV, D_FF // N_DEV, D)]
    gpack_a = _pack(blocks_a, 1, pack_rows(blocks_a, 1), bf16)
    dqh, dkh, dvh, gparts_a = _flash_bwd(qh, kh, vh, do_h, lse, delta, [gpack_a], True)
    dq_full = _rope_q_bwd(dqh, cos_t, sin_t)
    dkv_full, d_kpe_raw = _rope_k_bwd(dkh, dvh, cos_t, sin_t)
    d_qn = _mm([(dq_full, wuq)], f32, True, "up_q_dx")
    dw_uq = _mm_tn(qn, dq_full, "up_q_dw")
    d_kvn = _mm([(dkv_full, wukv)], f32, True, "up_kv_dx")
    dw_ukv = _mm_tn(kvn, dkv_full, "up_kv_dw")
    d_q_c, dg_q = _rms_bwd(q_c, q_a_norm, d_qn, bf16, "norm_q_bwd")
    d_kv_c, dg_kv = _rms_bwd(kv_c, kv_a_norm, d_kvn, bf16, "norm_kv_bwd")

    dy_ssd, dz, dg_ssm, dd_heads = _gate_norm_bwd(y_ssd, xbc_c, z, d_exp, ssm_norm, d_ssm, head_ind)
    d_xbc_c, ddt, da_heads = _ssd_bwd(xbc_c, dtp, dt_e, dtt, a_row, a_e, a_col, hin, dy_ssd, d_exp, head_ind)
    d_xbc, dw_sconv, db_sconv = _ssm_conv_bwd(xbc, sconv_w, ssm_conv_b, d_xbc_c)
    d_dt_raw, d_dt_bias = _dt_bwd(dt_raw, dt_bias_p, ddt)

    dw_q = _mm_tn(hn1, d_q_c, "proj_dw_q")
    dw_kv = _mm_tn(hn1, d_kv_c, "proj_dw_kv")
    dw_rope = _mm_tn(hn1, d_kpe_raw, "proj_dw_rope")
    dw_z = _mm_tn(hn1, dz, "proj_dw_z")
    dw_xbc = _mm_tn(hn1, d_xbc, "proj_dw_xbc")
    dw_dt = _mm_tn(hn1, d_dt_raw, "proj_dw_dt")
    dw_in = jnp.concatenate([dw_q, dw_kv, dw_rope[:, :QK_ROPE], dw_z, dw_xbc, dw_dt[:, :SSM_HEADS]], axis=1)
    dw_uq3 = dw_uq.reshape(Q_RANK, MLA_HEADS, QK_PAD)[:, :, :QK_NOPE + QK_ROPE]
    blocks_b = [
        col_blocks(dw_in),
        dw_uq3.reshape(N_DEV, Q_RANK // N_DEV, MLA_HEADS, QK_NOPE + QK_ROPE),
        dw_ukv.reshape(N_DEV, KV_RANK // N_DEV, MLA_HEADS, QK_NOPE + V_DIM),
        col_blocks(dw_sconv[:SSM_CONV]),
        col_blocks(jnp.concatenate([dwc_g, dwc_v], axis=1)[:FFN_CONV]),
    ]
    gpack_b = _pack(blocks_b, 1, pack_rows(blocks_b, 1), bf16)
    segs = [(d_q_c, w_q), (d_kv_c, w_kv), (d_kpe_raw, w_rope), (dz, w_z), (d_xbc, w_xbc), (d_dt_raw, w_dt)]
    d_hn1, gparts_b = _mm(segs, f32, True, "proj_dx", carried=[gpack_b], scatter=True)
    dh0, dg_mix_pre = _rms_bwd(h0, norm_mix_pre, d_hn1, f32, "norm_mix_pre_bwd", residual=dh1)

    grad_x = dh0[N_META:n_real][None]
    meta_blocks = col_blocks(dh0[:N_META]).reshape(N_DEV, N_META * D // N_DEV // PACK_W, PACK_W)


    def adam_group(parts, grp, name):
        rows = parts.shape[1]
        packs = [_pack([a[None] for a in grp[k]], 1, rows, f32)[0] for k in ("w", "m", "v")]
        outs = _adamw(parts, *packs, name)
        shapes = [a.shape for a in grp["w"]]
        return [dict(zip(grp["names"], [t[0] for t in _unpack(b[None], shapes, 1)])) for b in outs]

    sh_a = adam_group(gparts_a, grp_a, "adamw_sharded_a")
    sh_b = adam_group(gparts_b, grp_b, "adamw_sharded_b")

    dg_alog = da_heads[:, :SSM_HEADS] * a_neg
    repl_g = [dg_mix_pre, dg_mix_post, dg_ffn_pre, dg_ffn_post, dg_q, dg_kv, dg_attn_out, db_sconv,
              d_dt_bias[:, :SSM_HEADS], dg_alog, dd_heads[:, :SSM_HEADS], dg_ssm,
              jnp.concatenate([dbc_g, dbc_v], axis=1)]
    loss_vec = loss_part[:, :1]
    small_total = _round_up(sum(-(-int(np.prod(a.shape)) // PACK_W) for a in repl_g) + 1, 16)
    spack = _pack(repl_g + [loss_vec], 0, small_total, f32)
    gparts_meta, sparts = _exchange_tail([meta_blocks], [spack], "exchange_tail")
    sh_meta = adam_group(gparts_meta, grp_meta, "adamw_meta")
    zero1 = jnp.zeros((1, 1), f32)
    rw = _pack(repl_w + [zero1], 0, small_total, f32)
    rm = _pack(repl_m + [zero1], 0, small_total, f32)
    rv = _pack(repl_v + [zero1], 0, small_total, f32)
    outs_small = _adamw(sparts, rw, rm, rv, "adamw_replicated")
    repl_shapes = [a.shape for a in repl_w] + [(1, 1)]
    g_rp, d_rp, m_rp, v_rp = [_unpack(b, repl_shapes, 0) for b in outs_small]
    loss = g_rp[-1][0, 0]

    order = ["meta_tokens", "norm_mix_pre", "norm_mix_post", "norm_ffn_pre", "norm_ffn_post", "w_in", "q_a_norm",
             "w_uq", "kv_a_norm", "w_ukv", "attn_out_norm", "ssm_conv_w", "ssm_conv_b", "ssm_dt_bias", "ssm_A_log",
             "ssm_D", "ssm_norm", "w_out", "w_up", "ffn_conv_w", "ffn_conv_b", "w_down"]
    rp_names = ["norm_mix_pre", "norm_mix_post", "norm_ffn_pre", "norm_ffn_post", "q_a_norm", "kv_a_norm",
                "attn_out_norm", "ssm_conv_b", "ssm_dt_bias", "ssm_A_log", "ssm_D", "ssm_norm", "ffn_conv_b"]

    def lookup(k, rp_list):
        d = {**sh_a[k], **sh_b[k], **sh_meta[k], **dict(zip(rp_names, rp_list))}
        return [d[n] for n in order]

    return (loss, grad_x, *lookup(0, g_rp), *lookup(1, d_rp), *lookup(2, m_rp), *lookup(3, v_rp))
```

```python
import functools
import math

import jax
import jax.numpy as jnp
import numpy as np
from jax import lax
from jax.experimental import pallas as pl
from jax.experimental.pallas import tpu as pltpu

f32 = jnp.float32
bf16 = jnp.bfloat16

D_MODEL = 1024
SEQ = 8192
N_META = 16
MLA_HEADS = 8
QK_NOPE = 128
QK_ROPE = 64
V_DIM = 128
Q_RANK = 384
KV_RANK = 256
ROPE_THETA = 10000.0
SOFTMAX_SCALE = (QK_NOPE + QK_ROPE) ** -0.5
D_ATTN = MLA_HEADS * V_DIM
SSM_HEADS = 16
SSM_P = 64
SSM_GROUPS = 2
SSM_HPG = SSM_HEADS // SSM_GROUPS
SSM_N = 128
SSM_CONV = 4
CHUNK = 128
D_SSM = SSM_HEADS * SSM_P
D_BC = SSM_GROUPS * SSM_N
D_XBC = D_SSM + 2 * D_BC
D_FF = 2816
FFN_CONV = 3
EPS = 1e-6
D_IN = Q_RANK + KV_RANK + QK_ROPE + D_SSM + D_XBC + SSM_HEADS
QK_PAD = 256
N_DEV = 8

ADAM_LR = 0.001
ADAM_B1 = 0.9
ADAM_B2 = 0.999
ADAM_EPS = 1e-08
ADAM_WD = 0.01
ADAM_STEP = 10

LANES = 128
SUBLANES = 8
ROW_TILE = 256
VMEM_LIMIT = 56 * 1024 * 1024
PACK_W = 1024
PACK_ROW_TILE = 128
NEG = -1e30
LOG2E = math.log2(math.e)
LN2 = math.log(2.0)
Q_PRESCALE = SOFTMAX_SCALE * LOG2E

_MESH = pl.DeviceIdType.MESH


def _pick(n, prefs):
    for p in prefs:
        if n % p == 0:
            return p
    return n


def _rt(m):
    return _pick(m, (384, ROW_TILE))


def _cparams(sem):
    return pltpu.CompilerParams(dimension_semantics=sem, vmem_limit_bytes=VMEM_LIMIT)


def _row(spec_cols, tm):
    return pl.BlockSpec((tm, spec_cols), lambda i: (i, 0))


def _full(shape):
    nd = len(shape)
    return pl.BlockSpec(shape, lambda *a: (0,) * nd)


def _sigmoid(x):
    return 1.0 / (1.0 + jnp.exp(-x))


def _silu(x):
    return x * _sigmoid(x)


def _dsilu(x):
    s = _sigmoid(x)
    return s * (1.0 + x * (1.0 - s))


def _dot(a, b):
    return jnp.dot(a, b, preferred_element_type=f32)


def _dot_nt(a, b):
    return lax.dot_general(a, b, (((1,), (1,)), ((), ())), preferred_element_type=f32)


def _dot_tn(a, b):
    return lax.dot_general(a, b, (((0,), (0,)), ((), ())), preferred_element_type=f32)


def _dot_hi(a, b):
    return jnp.dot(a, b, precision=lax.Precision.HIGHEST, preferred_element_type=f32)


def _mm(pairs, out_dtype, trans_b, name, carried=(), scatter=False):
    n = len(pairs)
    nx = len(carried)
    M = pairs[0][0].shape[0]
    N = pairs[0][1].shape[0] if trans_b else pairs[0][1].shape[1]
    tm = _pick(M, (768, 512, 256))
    tn = _pick(N, (512, 1408, 384, 256, 128))
    ni, nj = M // tm, N // tn

    def body(*refs):
        o_ref = refs[2 * n + nx]
        if nx:
            i, j = pl.program_id(0), pl.program_id(1)
            _hosted_exchange(refs[2 * n:2 * n + nx], refs[2 * n + nx + 1:2 * n + 2 * nx + 1],
                             refs[2 * n + 2 * nx + 1:], scatter,
                             (i == 0) & (j == 0), (i == ni - 1) & (j == nj - 1))
        acc = None
        for p in range(n):
            a = refs[2 * p][...].astype(bf16)
            b = refs[2 * p + 1][...].astype(bf16)
            r = _dot_nt(a, b) if trans_b else _dot(a, b)
            acc = r if acc is None else acc + r
        o_ref[...] = acc.astype(out_dtype)

    in_specs, args = [], []
    for a, b in pairs:
        k = a.shape[1]
        in_specs.append(pl.BlockSpec((tm, k), lambda i, j: (i, 0)))
        if trans_b:
            in_specs.append(pl.BlockSpec((tn, k), lambda i, j: (j, 0)))
        else:
            in_specs.append(pl.BlockSpec((k, tn), lambda i, j: (0, j)))
        args += [a, b]
    out_spec = pl.BlockSpec((tm, tn), lambda i, j: (i, j))
    out_shape = jax.ShapeDtypeStruct((M, N), out_dtype)
    if not nx:
        return pl.pallas_call(
            body, name=name, grid=(ni, nj), in_specs=in_specs, out_specs=out_spec, out_shape=out_shape,
            compiler_params=_cparams(("parallel", "parallel")),
        )(*args)
    any_spec = pl.BlockSpec(memory_space=pl.ANY)
    return pl.pallas_call(
        body, name=name, grid=(ni, nj), in_specs=in_specs + [any_spec] * nx,
        out_specs=[out_spec] + [any_spec] * nx,
        out_shape=[out_shape] + _exchange_shapes(carried, scatter),
        scratch_shapes=_exchange_sems(nx),
        compiler_params=_cparams(("arbitrary", "arbitrary")),
    )(*args, *carried)


def _mm_tn(a, g, name):
    M, K = a.shape
    N = g.shape[1]
    tm = _pick(M, (768, 512, 256))
    tk = _pick(K, (1024, 1408, 512, 384, 256))
    tn = _pick(N, (1024, 1408, 512, 384, 256, 128))

    def body(a_ref, g_ref, o_ref):
        @pl.when(pl.program_id(2) == 0)
        def _():
            o_ref[...] = jnp.zeros_like(o_ref)

        o_ref[...] += _dot_tn(a_ref[...].astype(bf16), g_ref[...].astype(bf16))

    return pl.pallas_call(
        body, name=name, grid=(K // tk, N // tn, M // tm),
        in_specs=[pl.BlockSpec((tm, tk), lambda k, j, m: (m, k)),
                  pl.BlockSpec((tm, tn), lambda k, j, m: (m, j))],
        out_specs=pl.BlockSpec((tk, tn), lambda k, j, m: (k, j)),
        out_shape=jax.ShapeDtypeStruct((K, N), f32),
        compiler_params=_cparams(("parallel", "parallel", "arbitrary")),
    )(a, g)


def _rstd(x):
    return lax.rsqrt(jnp.mean(x * x, axis=-1, keepdims=True) + EPS)


def _rms_bwd_math(x, g, dy):
    r = _rstd(x)
    xh = x * r
    dn = dy * g
    dx = r * (dn - xh * jnp.mean(dn * xh, axis=-1, keepdims=True))
    return dx, dy * xh


def _rms_fwd(x, g, out_dtype, name):
    M, K = x.shape
    tm = _rt(M)

    def body(x_ref, g_ref, o_ref):
        xv = x_ref[...]
        o_ref[...] = (xv * _rstd(xv) * g_ref[...]).astype(out_dtype)

    return pl.pallas_call(
        body, name=name, grid=(M // tm,), in_specs=[_row(K, tm), _full((1, K))],
        out_specs=_row(K, tm), out_shape=jax.ShapeDtypeStruct((M, K), out_dtype),
        compiler_params=_cparams(("parallel",)),
    )(x, g)


def _rms_bwd(x, g, dy, out_dtype, name, residual=None):
    M, K = x.shape
    tm = _rt(M)
    has_res = residual is not None

    def body(*refs):
        if has_res:
            x_ref, g_ref, dy_ref, r_ref, dx_ref, dg_ref = refs
        else:
            x_ref, g_ref, dy_ref, dx_ref, dg_ref = refs

        @pl.when(pl.program_id(0) == 0)
        def _():
            dg_ref[...] = jnp.zeros_like(dg_ref)

        dx, dgp = _rms_bwd_math(x_ref[...], g_ref[...], dy_ref[...].astype(f32))
        if has_res:
            dx = dx + r_ref[...]
        dx_ref[...] = dx.astype(out_dtype)
        dg_ref[...] += jnp.sum(dgp, axis=0, keepdims=True)

    ins = [x, g, dy] + ([residual] if has_res else [])
    in_specs = [_row(K, tm), _full((1, K)), _row(K, tm)] + ([_row(K, tm)] if has_res else [])
    return pl.pallas_call(
        body, name=name, grid=(M // tm,), in_specs=in_specs,
        out_specs=[_row(K, tm), _full((1, K))],
        out_shape=[jax.ShapeDtypeStruct((M, K), out_dtype), jax.ShapeDtypeStruct((1, K), f32)],
        compiler_params=_cparams(("arbitrary",)),
    )(*ins)


def _resid_norm(h0, mix, g2, g3):
    M, K = h0.shape
    tm = _rt(M)

    def body(h_ref, m_ref, g2_ref, g3_ref, h1_ref, hn_ref):
        mv = m_ref[...]
        h1 = h_ref[...] + mv * _rstd(mv) * g2_ref[...]
        h1_ref[...] = h1
        hn_ref[...] = (h1 * _rstd(h1) * g3_ref[...]).astype(bf16)

    return pl.pallas_call(
        body, name="resid_norm", grid=(M // tm,),
        in_specs=[_row(K, tm), _row(K, tm), _full((1, K)), _full((1, K))],
        out_specs=[_row(K, tm), _row(K, tm)],
        out_shape=[jax.ShapeDtypeStruct((M, K), f32), jax.ShapeDtypeStruct((M, K), bf16)],
        compiler_params=_cparams(("parallel",)),
    )(h0, mix, g2, g3)


def _final(h1, down, g4, tgt, n_real):
    M, K = h1.shape
    tm = _rt(M)
    nt = M // tm

    def body(h_ref, d_ref, g_ref, t_ref, dh_ref, dd_ref, dg_ref, ls_ref, acc_ref):
        i = pl.program_id(0)

        @pl.when(i == 0)
        def _():
            dg_ref[...] = jnp.zeros_like(dg_ref)
            acc_ref[...] = jnp.zeros_like(acc_ref)

        dv = d_ref[...]
        g = g_ref[...]
        r = _rstd(dv)
        n = dv * r
        h2 = h_ref[...] + n * g
        rows = i * tm + lax.broadcasted_iota(jnp.int32, (tm, 1), 0)
        mask = ((rows >= N_META) & (rows < n_real)).astype(f32)
        diff = (h2 - t_ref[...]) * mask
        acc_ref[...] += jnp.sum(diff * diff, axis=0, keepdims=True)
        dh = diff * (1.0 / K)
        dh_ref[...] = dh
        dn = dh * g
        dd_ref[...] = (r * (dn - n * jnp.mean(dn * n, axis=-1, keepdims=True))).astype(bf16)
        dg_ref[...] += jnp.sum(dh * n, axis=0, keepdims=True)

        @pl.when(i == nt - 1)
        def _():
            ls_ref[...] = jnp.zeros((1, LANES), f32) + jnp.sum(acc_ref[...]) * (0.5 / K)

    return pl.pallas_call(
        body, name="final_loss", grid=(nt,),
        in_specs=[_row(K, tm), _row(K, tm), _full((1, K)), _row(K, tm)],
        out_specs=[_row(K, tm), _row(K, tm), _full((1, K)), _full((1, LANES))],
        out_shape=[jax.ShapeDtypeStruct((M, K), f32), jax.ShapeDtypeStruct((M, K), bf16),
                   jax.ShapeDtypeStruct((1, K), f32), jax.ShapeDtypeStruct((1, LANES), f32)],
        scratch_shapes=[pltpu.VMEM((1, K), f32)],
        compiler_params=_cparams(("arbitrary",)),
    )(h1, down, g4, tgt)


def _mid_bwd(h1, g3, d_hn2, dh2, mix, g2):
    M, K = h1.shape
    tm = _rt(M)

    def body(h_ref, g3_ref, dn_ref, dh2_ref, m_ref, g2_ref, dh1_ref, dm_ref, dg3_ref, dg2_ref):
        @pl.when(pl.program_id(0) == 0)
        def _():
            dg3_ref[...] = jnp.zeros_like(dg3_ref)
            dg2_ref[...] = jnp.zeros_like(dg2_ref)

        dx, dgp = _rms_bwd_math(h_ref[...], g3_ref[...], dn_ref[...])
        dh1 = dh2_ref[...] + dx
        dh1_ref[...] = dh1
        dg3_ref[...] += jnp.sum(dgp, axis=0, keepdims=True)
        dm, dgp2 = _rms_bwd_math(m_ref[...], g2_ref[...], dh1)
        dm_ref[...] = dm.astype(bf16)
        dg2_ref[...] += jnp.sum(dgp2, axis=0, keepdims=True)

    return pl.pallas_call(
        body, name="mid_bwd", grid=(M // tm,),
        in_specs=[_row(K, tm), _full((1, K)), _row(K, tm), _row(K, tm), _row(K, tm), _full((1, K))],
        out_specs=[_row(K, tm), _row(K, tm), _full((1, K)), _full((1, K))],
        out_shape=[jax.ShapeDtypeStruct((M, K), f32), jax.ShapeDtypeStruct((M, K), bf16),
                   jax.ShapeDtypeStruct((1, K), f32), jax.ShapeDtypeStruct((1, K), f32)],
        compiler_params=_cparams(("arbitrary",)),
    )(h1, g3, d_hn2, dh2, mix, g2)


HEADS_PER_STEP = 4
CONV_RB = 16


def _conv_block_taps(x_ref, halo, rb, lanes, kw):
    r0 = rb * CONV_RB
    if rb == 0:
        cat = jnp.concatenate([halo, x_ref[0:CONV_RB, lanes]], axis=0)
        first = SUBLANES - (kw - 1)
        return [cat[first + k:first + k + CONV_RB] for k in range(kw)]
    return [x_ref[r0 - (kw - 1) + k:r0 - (kw - 1) + k + CONV_RB, lanes] for k in range(kw)]


def _conv_weighted(taps, w, kw):
    u = None
    for k in range(kw):
        t = taps[k] * w[k:k + 1, :]
        u = t if u is None else u + t
    return u


def _conv_block_dx(du, nxt, w, kw):
    cat = jnp.concatenate([du, nxt], axis=0)
    return _conv_weighted([cat[kw - 1 - k:kw - 1 - k + CONV_RB] for k in range(kw)], w, kw)


def _prev_spec(tm, tc, col_of, row_axis, reversed_tiles=0):
    def imap(*ids):
        i = ids[row_axis]
        if reversed_tiles:
            i = reversed_tiles - 1 - i
        return (jnp.maximum(i * (tm // SUBLANES) - 1, 0), col_of(*ids))
    return pl.BlockSpec((SUBLANES, tc), imap)


def _ssm_conv_fwd(xbc, w, b):
    M, C = xbc.shape
    tm, tc, kw = ROW_TILE, C, SSM_CONV

    def body(x_ref, h_ref, w_ref, b_ref, o_ref):
        i = pl.program_id(0)

        def chunk(j, carry):
            lanes = pl.ds(pl.multiple_of(j * LANES, LANES), LANES)
            halo = jnp.where(i == 0, 0.0, h_ref[:, lanes])
            wv = w_ref[:, lanes]
            bv = b_ref[:, lanes]
            for rb in range(tm // CONV_RB):
                u = _conv_weighted(_conv_block_taps(x_ref, halo, rb, lanes, kw), wv, kw) + bv
                o_ref[rb * CONV_RB:(rb + 1) * CONV_RB, lanes] = _silu(u)
            return carry

        lax.fori_loop(0, tc // LANES, chunk, 0)

    return pl.pallas_call(
        body, name="ssm_conv_fwd", grid=(M // tm, C // tc),
        in_specs=[pl.BlockSpec((tm, tc), lambda i, j: (i, j)),
                  _prev_spec(tm, tc, lambda i, j: j, 0),
                  pl.BlockSpec((SUBLANES, tc), lambda i, j: (0, j)),
                  pl.BlockSpec((1, tc), lambda i, j: (0, j))],
        out_specs=pl.BlockSpec((tm, tc), lambda i, j: (i, j)),
        out_shape=jax.ShapeDtypeStruct((M, C), f32),
        compiler_params=_cparams(("parallel", "parallel")),
    )(xbc, xbc, w, b)


def _ssm_conv_bwd(xbc, w, b, dout):
    M, C = xbc.shape
    tm, tc, kw = ROW_TILE, C // 3, SSM_CONV
    nt = M // tm

    def body(x_ref, h_ref, w_ref, b_ref, d_ref, dx_ref, dw_ref, db_ref, nxt_ref):
        i = pl.program_id(1)

        @pl.when(i == 0)
        def _():
            dw_ref[...] = jnp.zeros_like(dw_ref)
            db_ref[...] = jnp.zeros_like(db_ref)
            nxt_ref[...] = jnp.zeros_like(nxt_ref)

        def chunk(j, carry):
            lanes = pl.ds(pl.multiple_of(j * LANES, LANES), LANES)
            halo = jnp.where(i == nt - 1, 0.0, h_ref[:, lanes])
            wv = w_ref[:, lanes]
            bv = b_ref[:, lanes]
            nxt = nxt_ref[:, lanes]
            db = jnp.zeros((CONV_RB, LANES), f32)
            dw = [jnp.zeros((CONV_RB, LANES), f32) for _ in range(kw)]
            for rb in reversed(range(tm // CONV_RB)):
                rows = slice(rb * CONV_RB, (rb + 1) * CONV_RB)
                taps = _conv_block_taps(x_ref, halo, rb, lanes, kw)
                du = d_ref[rows, lanes] * _dsilu(_conv_weighted(taps, wv, kw) + bv)
                db = db + du
                dw = [dw[k] + du * taps[k] for k in range(kw)]
                dx_ref[rows, lanes] = _conv_block_dx(du, nxt, wv, kw).astype(bf16)
                nxt = du[0:SUBLANES]
            nxt_ref[:, lanes] = nxt
            db_ref[:, lanes] += jnp.sum(db, axis=0, keepdims=True)
            for k in range(kw):
                dw_ref[k:k + 1, lanes] += jnp.sum(dw[k], axis=0, keepdims=True)
            return carry

        lax.fori_loop(0, tc // LANES, chunk, 0)

    tile = pl.BlockSpec((tm, tc), lambda j, i: (nt - 1 - i, j))
    return pl.pallas_call(
        body, name="ssm_conv_bwd", grid=(C // tc, nt),
        in_specs=[tile, _prev_spec(tm, tc, lambda j, i: j, 1, nt),
                  pl.BlockSpec((SUBLANES, tc), lambda j, i: (0, j)),
                  pl.BlockSpec((1, tc), lambda j, i: (0, j)), tile],
        out_specs=[tile, pl.BlockSpec((SUBLANES, tc), lambda j, i: (0, j)),
                   pl.BlockSpec((1, tc), lambda j, i: (0, j))],
        out_shape=[jax.ShapeDtypeStruct((M, C), bf16), jax.ShapeDtypeStruct((SUBLANES, C), f32),
                   jax.ShapeDtypeStruct((1, C), f32)],
        scratch_shapes=[pltpu.VMEM((SUBLANES, tc), f32)],
        compiler_params=_cparams(("parallel", "arbitrary")),
    )(xbc, xbc, w, b, dout)


def _ffn_gate_fwd(up, w, b):
    M = up.shape[0]
    tm, tc, kw = ROW_TILE, D_FF // 2, FFN_CONV
    nc = D_FF // tc

    def body(xg_ref, hg_ref, xv_ref, hv_ref, wg_ref, wv_ref, bg_ref, bv_ref, o_ref):
        i = pl.program_id(0)

        def chunk(j, carry):
            lanes = pl.ds(pl.multiple_of(j * LANES, LANES), LANES)
            halo_g = jnp.where(i == 0, 0.0, hg_ref[:, lanes])
            halo_v = jnp.where(i == 0, 0.0, hv_ref[:, lanes])
            wg, wv = wg_ref[:, lanes], wv_ref[:, lanes]
            bg, bv = bg_ref[:, lanes], bv_ref[:, lanes]
            for rb in range(tm // CONV_RB):
                ug = _conv_weighted(_conv_block_taps(xg_ref, halo_g, rb, lanes, kw), wg, kw) + bg
                uv = _conv_weighted(_conv_block_taps(xv_ref, halo_v, rb, lanes, kw), wv, kw) + bv
                o_ref[rb * CONV_RB:(rb + 1) * CONV_RB, lanes] = (_silu(ug) * uv).astype(bf16)
            return carry

        lax.fori_loop(0, tc // LANES, chunk, 0)

    return pl.pallas_call(
        body, name="ffn_gate_fwd", grid=(M // tm, nc),
        in_specs=[pl.BlockSpec((tm, tc), lambda i, j: (i, j)),
                  _prev_spec(tm, tc, lambda i, j: j, 0),
                  pl.BlockSpec((tm, tc), lambda i, j: (i, j + nc)),
                  _prev_spec(tm, tc, lambda i, j: j + nc, 0),
                  pl.BlockSpec((SUBLANES, tc), lambda i, j: (0, j)),
                  pl.BlockSpec((SUBLANES, tc), lambda i, j: (0, j + nc)),
                  pl.BlockSpec((1, tc), lambda i, j: (0, j)),
                  pl.BlockSpec((1, tc), lambda i, j: (0, j + nc))],
        out_specs=pl.BlockSpec((tm, tc), lambda i, j: (i, j)),
        out_shape=jax.ShapeDtypeStruct((M, D_FF), bf16),
        compiler_params=_cparams(("parallel", "parallel")),
    )(up, up, up, up, w, w, b, b)


def _ffn_gate_bwd(up, w, b, d_act):
    M = up.shape[0]
    tm, tc, kw = ROW_TILE, D_FF // 2, FFN_CONV
    nc = D_FF // tc
    nt = M // tm

    def body(xg_ref, hg_ref, xv_ref, hv_ref, wg_ref, wv_ref, bg_ref, bv_ref, d_ref,
             dxg_ref, dxv_ref, dwg_ref, dwv_ref, dbg_ref, dbv_ref, ng_ref, nv_ref):
        i = pl.program_id(1)

        @pl.when(i == 0)
        def _():
            for r in (dwg_ref, dwv_ref, dbg_ref, dbv_ref, ng_ref, nv_ref):
                r[...] = jnp.zeros_like(r)

        def chunk(j, carry):
            lanes = pl.ds(pl.multiple_of(j * LANES, LANES), LANES)
            halo_g = jnp.where(i == nt - 1, 0.0, hg_ref[:, lanes])
            halo_v = jnp.where(i == nt - 1, 0.0, hv_ref[:, lanes])
            wg, wv = wg_ref[:, lanes], wv_ref[:, lanes]
            bg, bv = bg_ref[:, lanes], bv_ref[:, lanes]
            nxt_g, nxt_v = ng_ref[:, lanes], nv_ref[:, lanes]
            zero = jnp.zeros((CONV_RB, LANES), f32)
            dbg, dbv = zero, zero
            dwg = [zero for _ in range(kw)]
            dwv = [zero for _ in range(kw)]
            for rb in reversed(range(tm // CONV_RB)):
                rows = slice(rb * CONV_RB, (rb + 1) * CONV_RB)
                tg = _conv_block_taps(xg_ref, halo_g, rb, lanes, kw)
                tv = _conv_block_taps(xv_ref, halo_v, rb, lanes, kw)
                ug = _conv_weighted(tg, wg, kw) + bg
                uv = _conv_weighted(tv, wv, kw) + bv
                sg = _sigmoid(ug)
                da = d_ref[rows, lanes]
                dug = da * uv * (sg * (1.0 + ug * (1.0 - sg)))
                duv = da * (ug * sg)
                dbg = dbg + dug
                dbv = dbv + duv
                dwg = [dwg[k] + dug * tg[k] for k in range(kw)]
                dwv = [dwv[k] + duv * tv[k] for k in range(kw)]
                dxg_ref[rows, lanes] = _conv_block_dx(dug, nxt_g, wg, kw).astype(bf16)
                dxv_ref[rows, lanes] = _conv_block_dx(duv, nxt_v, wv, kw).astype(bf16)
                nxt_g, nxt_v = dug[0:SUBLANES], duv[0:SUBLANES]
            ng_ref[:, lanes] = nxt_g
            nv_ref[:, lanes] = nxt_v
            dbg_ref[:, lanes] += jnp.sum(dbg, axis=0, keepdims=True)
            dbv_ref[:, lanes] += jnp.sum(dbv, axis=0, keepdims=True)
            for k in range(kw):
                dwg_ref[k:k + 1, lanes] += jnp.sum(dwg[k], axis=0, keepdims=True)
                dwv_ref[k:k + 1, lanes] += jnp.sum(dwv[k], axis=0, keepdims=True)
            return carry

        lax.fori_loop(0, tc // LANES, chunk, 0)

    tile_g = pl.BlockSpec((tm, tc), lambda j, i: (nt - 1 - i, j))
    tile_v = pl.BlockSpec((tm, tc), lambda j, i: (nt - 1 - i, j + nc))
    ext = pltpu.VMEM((SUBLANES, tc), f32)
    return pl.pallas_call(
        body, name="ffn_gate_bwd", grid=(nc, nt),
        in_specs=[tile_g, _prev_spec(tm, tc, lambda j, i: j, 1, nt),
                  tile_v, _prev_spec(tm, tc, lambda j, i: j + nc, 1, nt),
                  pl.BlockSpec((SUBLANES, tc), lambda j, i: (0, j)),
                  pl.BlockSpec((SUBLANES, tc), lambda j, i: (0, j + nc)),
                  pl.BlockSpec((1, tc), lambda j, i: (0, j)),
                  pl.BlockSpec((1, tc), lambda j, i: (0, j + nc)),
                  tile_g],
        out_specs=[tile_g, tile_g,
                   pl.BlockSpec((SUBLANES, tc), lambda j, i: (0, j)),
                   pl.BlockSpec((SUBLANES, tc), lambda j, i: (0, j)),
                   pl.BlockSpec((1, tc), lambda j, i: (0, j)),
                   pl.BlockSpec((1, tc), lambda j, i: (0, j))],
        out_shape=[jax.ShapeDtypeStruct((M, D_FF), bf16), jax.ShapeDtypeStruct((M, D_FF), bf16),
                   jax.ShapeDtypeStruct((SUBLANES, D_FF), f32), jax.ShapeDtypeStruct((SUBLANES, D_FF), f32),
                   jax.ShapeDtypeStruct((1, D_FF), f32), jax.ShapeDtypeStruct((1, D_FF), f32)],
        scratch_shapes=[ext, ext],
        compiler_params=_cparams(("parallel", "arbitrary")),
    )(up, up, up, up, w, w, b, b, d_act)


def _rope_apply(blk, cos, sin):
    lane = lax.broadcasted_iota(jnp.int32, blk.shape, 1)
    half = QK_ROPE // 2
    partner = jnp.where(lane < half, pltpu.roll(blk, LANES - half, 1), pltpu.roll(blk, half, 1))
    return blk * cos + partner * sin


def _rope_unapply(d, cos, sin):
    t = d * sin
    lane = lax.broadcasted_iota(jnp.int32, d.shape, 1)
    half = QK_ROPE // 2
    partner = jnp.where(lane < half, pltpu.roll(t, LANES - half, 1), pltpu.roll(t, half, 1))
    return d * cos + partner


def _up_q_rope(qn, wuq, cos, sin):
    M, K = qn.shape
    tm = _pick(M, (768, 512, 256))

    hs = HEADS_PER_STEP

    def body(a_ref, b_ref, c_ref, s_ref, o_ref):
        r = _dot(a_ref[...], b_ref[...]) * Q_PRESCALE
        c, s = c_ref[...], s_ref[...]
        for u in range(hs):
            o_ref[u, :, 0:QK_NOPE] = r[:, u * QK_PAD:u * QK_PAD + QK_NOPE].astype(bf16)
            o_ref[u, :, QK_NOPE:QK_PAD] = _rope_apply(r[:, u * QK_PAD + QK_NOPE:(u + 1) * QK_PAD], c, s).astype(bf16)

    return pl.pallas_call(
        body, name="up_q_rope", grid=(M // tm, MLA_HEADS // hs),
        in_specs=[pl.BlockSpec((tm, K), lambda i, h: (i, 0)),
                  pl.BlockSpec((K, hs * QK_PAD), lambda i, h: (0, h)),
                  pl.BlockSpec((tm, LANES), lambda i, h: (i, 0)),
                  pl.BlockSpec((tm, LANES), lambda i, h: (i, 0))],
        out_specs=pl.BlockSpec((hs, tm, QK_PAD), lambda i, h: (h, i, 0)),
        out_shape=jax.ShapeDtypeStruct((MLA_HEADS, M, QK_PAD), bf16),
        compiler_params=_cparams(("parallel", "parallel")),
    )(qn, wuq, cos, sin)


def _up_kv_rope(kvn, wukv, kpe_raw, cos, sin):
    M, K = kvn.shape
    tm = _pick(M, (768, 512, 256))

    hs = HEADS_PER_STEP
    w = QK_NOPE + V_DIM

    def body(a_ref, b_ref, pe_ref, c_ref, s_ref, k_ref, v_ref):
        r = _dot(a_ref[...], b_ref[...])
        pe = _rope_apply(pe_ref[...], c_ref[...], s_ref[...]).astype(bf16)
        for u in range(hs):
            k_ref[u, :, 0:QK_NOPE] = r[:, u * w:u * w + QK_NOPE].astype(bf16)
            k_ref[u, :, QK_NOPE:QK_PAD] = pe
            v_ref[u] = r[:, u * w + QK_NOPE:(u + 1) * w].astype(bf16)

    return pl.pallas_call(
        body, name="up_kv_rope", grid=(M // tm, MLA_HEADS // hs),
        in_specs=[pl.BlockSpec((tm, K), lambda i, h: (i, 0)),
                  pl.BlockSpec((K, hs * w), lambda i, h: (0, h)),
                  pl.BlockSpec((tm, LANES), lambda i, h: (i, 0)),
                  pl.BlockSpec((tm, LANES), lambda i, h: (i, 0)),
                  pl.BlockSpec((tm, LANES), lambda i, h: (i, 0))],
        out_specs=[pl.BlockSpec((hs, tm, QK_PAD), lambda i, h: (h, i, 0)),
                   pl.BlockSpec((hs, tm, V_DIM), lambda i, h: (h, i, 0))],
        out_shape=[jax.ShapeDtypeStruct((MLA_HEADS, M, QK_PAD), bf16),
                   jax.ShapeDtypeStruct((MLA_HEADS, M, V_DIM), bf16)],
        compiler_params=_cparams(("parallel", "parallel")),
    )(kvn, wukv, kpe_raw, cos, sin)


def _rope_q_bwd(dq, cos, sin):
    M = dq.shape[1]
    tm = _rt(M)

    def body(d_ref, c_ref, s_ref, o_ref):
        c, s = c_ref[...], s_ref[...]
        for h in range(MLA_HEADS):
            o_ref[:, h * QK_PAD:h * QK_PAD + QK_NOPE] = (d_ref[h, :, 0:QK_NOPE] * SOFTMAX_SCALE).astype(bf16)
            o_ref[:, h * QK_PAD + QK_NOPE:(h + 1) * QK_PAD] = (_rope_unapply(
                d_ref[h, :, QK_NOPE:QK_PAD], c, s) * SOFTMAX_SCALE).astype(bf16)

    return pl.pallas_call(
        body, name="rope_q_bwd", grid=(M // tm,),
        in_specs=[pl.BlockSpec((MLA_HEADS, tm, QK_PAD), lambda i: (0, i, 0)),
                  _row(LANES, tm), _row(LANES, tm)],
        out_specs=_row(MLA_HEADS * QK_PAD, tm),
        out_shape=jax.ShapeDtypeStruct((M, MLA_HEADS * QK_PAD), bf16),
        compiler_params=_cparams(("parallel",)),
    )(dq, cos, sin)


def _rope_k_bwd(dk, dv, cos, sin):
    M = dk.shape[1]
    tm = _rt(M)
    w = QK_NOPE + V_DIM

    def body(dk_ref, dv_ref, c_ref, s_ref, o_ref, pe_ref):
        pe = None
        for h in range(MLA_HEADS):
            o_ref[:, h * w:h * w + QK_NOPE] = dk_ref[h, :, 0:QK_NOPE].astype(bf16)
            o_ref[:, h * w + QK_NOPE:(h + 1) * w] = dv_ref[h].astype(bf16)
            t = dk_ref[h, :, QK_NOPE:QK_PAD]
            pe = t if pe is None else pe + t
        pe_ref[...] = _rope_unapply(pe, c_ref[...], s_ref[...])

    return pl.pallas_call(
        body, name="rope_k_bwd", grid=(M // tm,),
        in_specs=[pl.BlockSpec((MLA_HEADS, tm, QK_PAD), lambda i: (0, i, 0)),
                  pl.BlockSpec((MLA_HEADS, tm, V_DIM), lambda i: (0, i, 0)),
                  _row(LANES, tm), _row(LANES, tm)],
        out_specs=[_row(MLA_HEADS * w, tm), _row(LANES, tm)],
        out_shape=[jax.ShapeDtypeStruct((M, MLA_HEADS * w), bf16), jax.ShapeDtypeStruct((M, LANES), f32)],
        compiler_params=_cparams(("parallel",)),
    )(dk, dv, cos, sin)


def _attn_tile(M):
    return 768 if (M % 768 == 0 and M >= 4 * 768) else ROW_TILE


def _col_to_row(col):
    return col.T[0:1, :]


def _hosted_exchange(refs_in, refs_out, sems, scatter, first, last):
    copies = _exchange_copies(refs_in, refs_out, *sems, scatter)

    @pl.when(first)
    def _():
        for cp in copies:
            cp.start()

    @pl.when(last)
    def _():
        for cp in copies:
            cp.wait()


def _flash_fwd(q, k, v, carried, scatter):
    H, M, _ = q.shape
    T = _attn_tile(M)
    nq = M // T
    nx = len(carried)

    def body(*refs):
        q_ref, k_ref, v_ref = refs[:3]
        o_ref, lse_ref = refs[3 + nx:5 + nx]
        sa_ref, sb_ref, m_sc, l_sc, acc_sc = refs[5 + 2 * nx:10 + 2 * nx]
        h = pl.program_id(0)
        i = pl.program_id(1)
        _hosted_exchange(refs[3:3 + nx], refs[5 + nx:5 + 2 * nx], refs[10 + 2 * nx:], scatter,
                         (h == 0) & (i == 0), (h == H - 1) & (i == nq - 1))
        qv = q_ref[0]
        m_sc[...] = jnp.full_like(m_sc, NEG)
        l_sc[...] = jnp.zeros_like(l_sc)
        acc_sc[...] = jnp.zeros_like(acc_sc)

        def scores(j, s_ref):
            off = pl.multiple_of(j * T, T)
            s_ref[...] = _dot_nt(qv, k_ref[0, pl.ds(off, T), :])

        def softmax_pv(j, s_ref, masked):
            off = pl.multiple_of(j * T, T)
            s = s_ref[...]
            if masked:
                r = lax.broadcasted_iota(jnp.int32, (T, T), 0)
                c = lax.broadcasted_iota(jnp.int32, (T, T), 1)
                s = jnp.where(r >= c, s, NEG)
            m_prev = m_sc[...]
            m_new = jnp.maximum(m_prev, jnp.max(s, axis=1, keepdims=True))
            alpha = jnp.exp2(m_prev - m_new)
            p = jnp.exp2(s - m_new[:, 0:1])
            l_sc[...] = alpha * l_sc[...] + jnp.sum(p, axis=1, keepdims=True)
            acc_sc[...] = alpha * acc_sc[...] + _dot(p.astype(bf16), v_ref[0, pl.ds(off, T), :])
            m_sc[...] = m_new

        scores(0, sa_ref)

        def pair(jj, c):
            j0 = 2 * jj
            scores(j0 + 1, sb_ref)
            softmax_pv(j0, sa_ref, False)
            scores(j0 + 2, sa_ref)
            softmax_pv(j0 + 1, sb_ref, False)
            return c

        lax.fori_loop(0, i // 2, pair, 0)

        @pl.when(i % 2 == 0)
        def _():
            softmax_pv(i, sa_ref, True)

        @pl.when(i % 2 == 1)
        def _():
            scores(i, sb_ref)
            softmax_pv(i - 1, sa_ref, False)
            softmax_pv(i, sb_ref, True)

        l = l_sc[...]
        o_ref[...] = acc_sc[...] / l
        lse_ref[0, 0] = _col_to_row(m_sc[...] + jnp.log2(l))

    any_spec = pl.BlockSpec(memory_space=pl.ANY)
    return pl.pallas_call(
        body, name="flash_fwd", grid=(H, nq),
        in_specs=[pl.BlockSpec((1, T, QK_PAD), lambda h, i: (h, i, 0)),
                  pl.BlockSpec((1, M, QK_PAD), lambda h, i: (h, 0, 0)),
                  pl.BlockSpec((1, M, V_DIM), lambda h, i: (h, 0, 0))] + [any_spec] * nx,
        out_specs=[pl.BlockSpec((T, V_DIM), lambda h, i: (i, h)),
                   pl.BlockSpec((1, 1, 1, T), lambda h, i: (h, i, 0, 0))] + [any_spec] * nx,
        out_shape=[jax.ShapeDtypeStruct((M, H * V_DIM), f32),
                   jax.ShapeDtypeStruct((H, nq, 1, T), f32)] + _exchange_shapes(carried, scatter),
        scratch_shapes=[pltpu.VMEM((T, T), f32), pltpu.VMEM((T, T), f32),
                        pltpu.VMEM((T, LANES), f32), pltpu.VMEM((T, LANES), f32),
                        pltpu.VMEM((T, V_DIM), f32)] + _exchange_sems(nx),
        compiler_params=_cparams(("arbitrary", "arbitrary")),
    )(q, k, v, *carried)


def _attn_out_bwd(o, g, d_an):
    M, K = o.shape
    H = MLA_HEADS
    T = _attn_tile(M)

    def body(o_ref, g_ref, d_ref, dh_ref, dl_ref, dg_ref):
        @pl.when(pl.program_id(0) == 0)
        def _():
            dg_ref[...] = jnp.zeros_like(dg_ref)

        ov = o_ref[...]
        do, dgp = _rms_bwd_math(ov, g_ref[...], d_ref[...])
        dg_ref[...] += jnp.sum(dgp, axis=0, keepdims=True)
        for h in range(H):
            sl = slice(h * V_DIM, (h + 1) * V_DIM)
            doh = do[:, sl]
            dh_ref[h] = doh.astype(bf16)
            col = jnp.sum(ov[:, sl] * doh, axis=1, keepdims=True) + jnp.zeros((T, LANES), f32)
            dl_ref[h, 0] = _col_to_row(col)

    return pl.pallas_call(
        body, name="attn_out_bwd", grid=(M // T,),
        in_specs=[_row(K, T), _full((1, K)), _row(K, T)],
        out_specs=[pl.BlockSpec((H, T, V_DIM), lambda i: (0, i, 0)),
                   pl.BlockSpec((H, 1, 1, T), lambda i: (0, i, 0, 0)),
                   _full((1, K))],
        out_shape=[jax.ShapeDtypeStruct((H, M, V_DIM), bf16),
                   jax.ShapeDtypeStruct((H, M // T, 1, T), f32),
                   jax.ShapeDtypeStruct((1, K), f32)],
        compiler_params=_cparams(("arbitrary",)),
    )(o, g, d_an)


def _flash_bwd(q, k, v, do, lse, delta, carried, scatter):
    H, M, _ = q.shape
    T = _attn_tile(M)
    nq = M // T
    nx = len(carried)

    def body(*refs):
        q_ref, do_ref, lse_ref, dl_ref, k_ref, v_ref = refs[:6]
        dq_ref, dk_ref, dv_ref = refs[6 + nx:9 + nx]
        dk_sc, dv_sc = refs[9 + 2 * nx:11 + 2 * nx]
        j = pl.program_id(1)
        _hosted_exchange(refs[6:6 + nx], refs[9 + nx:9 + 2 * nx], refs[11 + 2 * nx:], scatter,
                         (pl.program_id(0) == 0) & (j == 0), (pl.program_id(0) == H - 1) & (j == nq - 1))

        @pl.when(j == 0)
        def _():
            dq_ref[...] = jnp.zeros_like(dq_ref)

        kt = k_ref[0]
        vt = v_ref[0]
        dk_sc[...] = jnp.zeros_like(dk_sc)
        dv_sc[...] = jnp.zeros_like(dv_sc)

        def step(i, masked):
            off = pl.multiple_of(i * T, T)
            qt = q_ref[0, pl.ds(off, T), :]
            dot_ = do_ref[0, pl.ds(off, T), :]
            st = _dot_nt(kt, qt)
            if masked:
                r = lax.broadcasted_iota(jnp.int32, (T, T), 0)
                c = lax.broadcasted_iota(jnp.int32, (T, T), 1)
                st = jnp.where(c >= r, st, NEG)
            pt = jnp.exp2(st - lse_ref[0, i])
            dv_sc[...] += _dot(pt.astype(bf16), dot_)
            dpt = _dot_nt(vt, dot_)
            dst = (pt * (dpt - dl_ref[0, i])).astype(bf16)
            dk_sc[...] += _dot(dst, qt)
            dq_ref[0, pl.ds(off, T), :] += _dot_tn(dst, kt)

        step(j, True)

        def loop_body(i, c):
            step(i, False)
            return c

        lax.fori_loop(j + 1, nq, loop_body, 0)
        dk_ref[0] = dk_sc[...] * LN2
        dv_ref[0] = dv_sc[...]

    any_spec = pl.BlockSpec(memory_space=pl.ANY)
    return pl.pallas_call(
        body, name="flash_bwd", grid=(H, nq),
        in_specs=[pl.BlockSpec((1, M, QK_PAD), lambda h, j: (h, 0, 0)),
                  pl.BlockSpec((1, M, V_DIM), lambda h, j: (h, 0, 0)),
                  pl.BlockSpec((1, nq, 1, T), lambda h, j: (h, 0, 0, 0)),
                  pl.BlockSpec((1, nq, 1, T), lambda h, j: (h, 0, 0, 0)),
                  pl.BlockSpec((1, T, QK_PAD), lambda h, j: (h, j, 0)),
                  pl.BlockSpec((1, T, V_DIM), lambda h, j: (h, j, 0))] + [any_spec] * nx,
        out_specs=[pl.BlockSpec((1, M, QK_PAD), lambda h, j: (h, 0, 0)),
                   pl.BlockSpec((1, T, QK_PAD), lambda h, j: (h, j, 0)),
                   pl.BlockSpec((1, T, V_DIM), lambda h, j: (h, j, 0))] + [any_spec] * nx,
        out_shape=[jax.ShapeDtypeStruct((H, M, QK_PAD), f32),
                   jax.ShapeDtypeStruct((H, M, QK_PAD), f32),
                   jax.ShapeDtypeStruct((H, M, V_DIM), f32)] + _exchange_shapes(carried, scatter),
        scratch_shapes=[pltpu.VMEM((T, QK_PAD), f32), pltpu.VMEM((T, V_DIM), f32)] + _exchange_sems(nx),
        compiler_params=_cparams(("arbitrary", "arbitrary")),
    )(q, do, lse, delta, k, v, *carried)


def _dt_fwd(dt_raw, bias, expand):
    M = dt_raw.shape[0]
    tm = _rt(M)

    def body(x_ref, b_ref, e_ref, o_ref, oe_ref):
        u = x_ref[...] + b_ref[...]
        sp = jnp.maximum(u, 0.0) + jnp.log(1.0 + jnp.exp(-jnp.abs(u)))
        lane = lax.broadcasted_iota(jnp.int32, u.shape, 1)
        dtp = jnp.where(lane < SSM_HEADS, sp, 0.0)
        o_ref[...] = dtp
        oe_ref[...] = _dot_hi(dtp, e_ref[...])

    return pl.pallas_call(
        body, name="dt_fwd", grid=(M // tm,),
        in_specs=[_row(LANES, tm), _full((1, LANES)), _full((LANES, D_SSM))],
        out_specs=[_row(LANES, tm), _row(D_SSM, tm)],
        out_shape=[jax.ShapeDtypeStruct((M, LANES), f32), jax.ShapeDtypeStruct((M, D_SSM), f32)],
        compiler_params=_cparams(("parallel",)),
    )(dt_raw, bias, expand)


def _dt_bwd(dt_raw, bias, ddt):
    M = dt_raw.shape[0]
    tm = _rt(M)

    def body(x_ref, b_ref, d_ref, o_ref, db_ref):
        @pl.when(pl.program_id(0) == 0)
        def _():
            db_ref[...] = jnp.zeros_like(db_ref)

        u = x_ref[...] + b_ref[...]
        lane = lax.broadcasted_iota(jnp.int32, u.shape, 1)
        g = jnp.where(lane < SSM_HEADS, d_ref[...] * _sigmoid(u), 0.0)
        o_ref[...] = g
        db_ref[...] += jnp.sum(g, axis=0, keepdims=True)

    return pl.pallas_call(
        body, name="dt_bwd", grid=(M // tm,),
        in_specs=[_row(LANES, tm), _full((1, LANES)), _row(LANES, tm)],
        out_specs=[_row(LANES, tm), _full((1, LANES))],
        out_shape=[jax.ShapeDtypeStruct((M, LANES), f32), jax.ShapeDtypeStruct((1, LANES), f32)],
        compiler_params=_cparams(("arbitrary",)),
    )(dt_raw, bias, ddt)


SSM_GW = SSM_HPG * SSM_P
SSM_PAIRS = SSM_GW // LANES


def _ssd_common(dte_ref, dtt_ref, ae_ref, acol_ref):
    Q = CHUNK
    r = lax.broadcasted_iota(jnp.int32, (Q, Q), 0)
    c = lax.broadcasted_iota(jnp.int32, (Q, Q), 1)
    causal = r >= c
    anti = c >= r
    tril = causal.astype(f32)
    triu = anti.astype(f32)
    dt_e = dte_ref[...]
    cs_e = _dot_hi(tril, dt_e * ae_ref[...])
    cst = _dot_hi(dtt_ref[...] * acol_ref[...], triu)
    cs_last = cs_e[Q - 1:Q, :]
    return causal, anti, triu, dt_e, cs_e, cst, jnp.exp(cs_e), jnp.exp(cs_last - cs_e), jnp.exp(cs_last)


def _half_masks():
    lane = lax.broadcasted_iota(jnp.int32, (CHUNK, LANES), 1)
    lo = lane < SSM_P
    return lo, jnp.logical_not(lo)


def _ssd_fwd(xbc_c, dt_e, dtt, a_e, a_col):
    M = xbc_c.shape[0]
    Q = CHUNK
    nch = M // Q

    def body(x_ref, dte_ref, dtt_ref, ae_ref, acol_ref, y_ref, hin_ref, ht_sc):
        @pl.when(pl.program_id(0) == 0)
        def _():
            ht_sc[...] = jnp.zeros_like(ht_sc)

        causal, _, _, dt_e, cs_e, cst, ecs_e, dte_e, elast_e = _ssd_common(dte_ref, dtt_ref, ae_ref, acol_ref)
        halves = _half_masks()
        for g in range(SSM_GROUPS):
            g0 = g * SSM_GW
            bg = x_ref[:, D_SSM + g * SSM_N:D_SSM + (g + 1) * SSM_N]
            cg = x_ref[:, D_SSM + D_BC + g * SSM_N:D_SSM + D_BC + (g + 1) * SSM_N]
            bg_b = bg.astype(bf16)
            cg_b = cg.astype(bf16)
            cb = _dot_nt(cg_b, bg_b)
            bgt_b = bg.T.astype(bf16)
            xdt_g = x_ref[:, g0:g0 + SSM_GW] * dt_e[:, g0:g0 + SSM_GW]
            ht = ht_sc[g]
            hin_ref[0, g] = ht
            y_off = _dot(cg_b, ht.astype(bf16)) * ecs_e[:, g0:g0 + SSM_GW]
            for pr in range(SSM_PAIRS):
                p0 = pr * LANES
                xdt_p = xdt_g[:, p0:p0 + LANES]
                acc = y_off[:, p0:p0 + LANES]
                for half in range(2):
                    h = g * SSM_HPG + pr * 2 + half
                    seg = cs_e[:, h * SSM_P:h * SSM_P + 1] - cst[h:h + 1, :]
                    lm = jnp.exp(jnp.where(causal, seg, -jnp.inf))
                    xm = jnp.where(halves[half], xdt_p, 0.0).astype(bf16)
                    acc = acc + _dot((cb * lm).astype(bf16), xm)
                y_ref[:, g0 + p0:g0 + p0 + LANES] = acc
            st = _dot(bgt_b, (xdt_g * dte_e[:, g0:g0 + SSM_GW]).astype(bf16))
            ht_sc[g] = ht * elast_e[:, g0:g0 + SSM_GW] + st

    return pl.pallas_call(
        body, name="ssd_fwd", grid=(nch,),
        in_specs=[pl.BlockSpec((Q, D_XBC), lambda c: (c, 0)),
                  pl.BlockSpec((Q, D_SSM), lambda c: (c, 0)),
                  pl.BlockSpec((SSM_HEADS, Q), lambda c: (0, c)),
                  _full((1, D_SSM)), _full((SSM_HEADS, LANES))],
        out_specs=[pl.BlockSpec((Q, D_SSM), lambda c: (c, 0)),
                   pl.BlockSpec((1, SSM_GROUPS, SSM_N, SSM_GW), lambda c: (c, 0, 0, 0))],
        out_shape=[jax.ShapeDtypeStruct((M, D_SSM), f32),
                   jax.ShapeDtypeStruct((nch, SSM_GROUPS, SSM_N, SSM_GW), f32)],
        scratch_shapes=[pltpu.VMEM((SSM_GROUPS, SSM_N, SSM_GW), f32)],
        compiler_params=_cparams(("arbitrary",)),
    )(xbc_c, dt_e, dtt, a_e, a_col)


def _ssd_bwd(xbc_c, dtp, dt_e, dtt, a_row, a_e, a_col, hin, dy, d_exp, head_ind):
    M = xbc_c.shape[0]
    Q = CHUNK
    nch = M // Q
    rev = lambda c: nch - 1 - c

    def body(x_ref, dtp_ref, dte_ref, dtt_ref, arow_ref, ae_ref, acol_ref, hin_ref, dy_ref, dexp_ref,
             ind_ref, dx_ref, ddt_ref, da_ref, dht_sc, z_sc, z1_sc, last_sc, ct_sc):
        @pl.when(pl.program_id(0) == 0)
        def _():
            dht_sc[...] = jnp.zeros_like(dht_sc)
            da_ref[...] = jnp.zeros_like(da_ref)
            last_sc[...] = jnp.zeros_like(last_sc)
            ct_sc[...] = jnp.zeros_like(ct_sc)

        causal, anti, triu, dt_e, cs_e, cst, ecs_e, dte_e, elast_e = _ssd_common(dte_ref, dtt_ref, ae_ref, acol_ref)
        halves = _half_masks()
        lane = lax.broadcasted_iota(jnp.int32, (Q, LANES), 1)
        rsum = jnp.zeros((Q, LANES), f32)
        for g in range(SSM_GROUPS):
            g0 = g * SSM_GW
            gs = slice(g0, g0 + SSM_GW)
            b0 = D_SSM + g * SSM_N
            c0 = D_SSM + D_BC + g * SSM_N
            bg = x_ref[:, b0:b0 + SSM_N]
            cg = x_ref[:, c0:c0 + SSM_N]
            bg_b = bg.astype(bf16)
            cg_b = cg.astype(bf16)
            cgt_b = cg.T.astype(bf16)
            cbt = _dot_nt(bg_b, cg_b)
            cb = _dot_nt(cg_b, bg_b)
            x_g = x_ref[:, gs]
            dt_g = dt_e[:, gs]
            xdt_g = x_g * dt_g
            dy_g = dy_ref[:, gs]
            ht = hin_ref[0, g]
            ht_b = ht.astype(bf16)
            dht = dht_sc[g]
            dht_b = dht.astype(bf16)
            dye_b = (dy_g * ecs_e[:, gs]).astype(bf16)
            dc = _dot_nt(dye_b, ht_b)
            dht_new = dht * elast_e[:, gs] + _dot(cgt_b, dye_b)
            e = _dot(bg_b, dht_b)
            xdtd = xdt_g * dte_e[:, gs]
            db = _dot_nt(xdtd.astype(bf16), dht_b)
            dxdt_state = e * dte_e[:, gs]
            exd = e * xdtd
            z1_sc[:, gs] = dy_g * (_dot(cg_b, ht_b) * ecs_e[:, gs]) - exd
            last_sc[0:1, gs] = (jnp.sum(exd, axis=0, keepdims=True)
                                + jnp.sum(dht * ht, axis=0, keepdims=True) * elast_e[:, gs])
            dg_acc = jnp.zeros((Q, Q), f32)
            for pr in range(SSM_PAIRS):
                p0 = pr * LANES
                ps = slice(g0 + p0, g0 + p0 + LANES)
                dy_p = dy_g[:, p0:p0 + LANES]
                xdt_pb = xdt_g[:, p0:p0 + LANES].astype(bf16)
                acc = dxdt_state[:, p0:p0 + LANES]
                for half in range(2):
                    h = g * SSM_HPG + pr * 2 + half
                    seg = cs_e[:, h * SSM_P:h * SSM_P + 1] - cst[h:h + 1, :]
                    lm = jnp.exp(jnp.where(causal, seg, -jnp.inf))
                    lmt = jnp.exp(jnp.where(anti, -seg, -jnp.inf))
                    dym = jnp.where(halves[half], dy_p, 0.0).astype(bf16)
                    acc = acc + _dot((cbt * lmt).astype(bf16), dym)
                    dml = _dot_nt(dym, xdt_pb) * lm
                    dg_acc = dg_acc + dml
                    w = dml * cb
                    rsum = rsum + jnp.where(lane == h, jnp.sum(w, axis=1, keepdims=True), 0.0)
                    ct_sc[h:h + 1, :] = jnp.sum(w, axis=0, keepdims=True)
                dx_ref[:, ps] = acc * dt_g[:, p0:p0 + LANES] + dexp_ref[:, ps] * dy_p
                z_sc[:, ps] = acc * x_g[:, p0:p0 + LANES]
            dg_b = dg_acc.astype(bf16)
            dx_ref[:, c0:c0 + SSM_N] = dc + _dot(dg_b, bg_b)
            dx_ref[:, b0:b0 + SSM_N] = db + _dot_tn(dg_b, cg_b)
            dht_sc[g] = dht_new
        s1 = _dot_hi(z1_sc[...], ind_ref[...])
        s2 = _dot_hi(z_sc[...], ind_ref[...])
        last = _dot_hi(last_sc[...], ind_ref[...])[0:1, :]
        dtp = dtp_ref[...]
        row = lax.broadcasted_iota(jnp.int32, (Q, LANES), 0)
        dcs = s1 + rsum + jnp.where(row == Q - 1, last, 0.0)
        tril = causal.astype(f32)
        da = _dot_hi(triu, dcs) - _dot_hi(ct_sc[...], tril).T
        ddt_ref[...] = s2 + da * arow_ref[...]
        da_ref[...] += jnp.sum(da * dtp, axis=0, keepdims=True)

    return pl.pallas_call(
        body, name="ssd_bwd", grid=(nch,),
        in_specs=[pl.BlockSpec((Q, D_XBC), lambda c: (rev(c), 0)),
                  pl.BlockSpec((Q, LANES), lambda c: (rev(c), 0)),
                  pl.BlockSpec((Q, D_SSM), lambda c: (rev(c), 0)),
                  pl.BlockSpec((SSM_HEADS, Q), lambda c: (0, rev(c))),
                  _full((1, LANES)), _full((1, D_SSM)), _full((SSM_HEADS, LANES)),
                  pl.BlockSpec((1, SSM_GROUPS, SSM_N, SSM_GW), lambda c: (rev(c), 0, 0, 0)),
                  pl.BlockSpec((Q, D_SSM), lambda c: (rev(c), 0)),
                  _full((1, D_SSM)), _full((D_SSM, LANES))],
        out_specs=[pl.BlockSpec((Q, D_XBC), lambda c: (rev(c), 0)),
                   pl.BlockSpec((Q, LANES), lambda c: (rev(c), 0)),
                   _full((1, LANES))],
        out_shape=[jax.ShapeDtypeStruct((M, D_XBC), f32), jax.ShapeDtypeStruct((M, LANES), f32),
                   jax.ShapeDtypeStruct((1, LANES), f32)],
        scratch_shapes=[pltpu.VMEM((SSM_GROUPS, SSM_N, SSM_GW), f32), pltpu.VMEM((Q, D_SSM), f32),
                        pltpu.VMEM((Q, D_SSM), f32), pltpu.VMEM((SUBLANES, D_SSM), f32),
                        pltpu.VMEM((LANES, Q), f32)],
        compiler_params=_cparams(("arbitrary",)),
    )(xbc_c, dtp, dt_e, dtt, a_row, a_e, a_col, hin, dy, d_exp, head_ind)


def _gate_norm_fwd(y, xbc_c, z, d_exp, g):
    M = y.shape[0]
    tm = _rt(M)
    gw = D_SSM // SSM_GROUPS

    def body(y_ref, x_ref, z_ref, d_ref, g_ref, o_ref):
        yg = (y_ref[...] + d_ref[...] * x_ref[...]) * _silu(z_ref[...])
        for gi in range(SSM_GROUPS):
            blk = yg[:, gi * gw:(gi + 1) * gw]
            o_ref[:, gi * gw:(gi + 1) * gw] = (blk * _rstd(blk) * g_ref[:, gi * gw:(gi + 1) * gw]).astype(bf16)

    return pl.pallas_call(
        body, name="gate_norm_fwd", grid=(M // tm,),
        in_specs=[_row(D_SSM, tm), _row(D_SSM, tm), _row(D_SSM, tm), _full((1, D_SSM)), _full((1, D_SSM))],
        out_specs=_row(D_SSM, tm), out_shape=jax.ShapeDtypeStruct((M, D_SSM), bf16),
        compiler_params=_cparams(("parallel",)),
    )(y, xbc_c, z, d_exp, g)


def _gate_norm_bwd(y, xbc_c, z, d_exp, g, dout, head_ind):
    M = y.shape[0]
    tm = _rt(M)
    nt = M // tm
    gw = D_SSM // SSM_GROUPS

    def body(y_ref, x_ref, z_ref, d_ref, g_ref, do_ref, ind_ref, dy_ref, dz_ref, dg_ref, dd_ref, ddc_sc):
        i = pl.program_id(0)

        @pl.when(i == 0)
        def _():
            dg_ref[...] = jnp.zeros_like(dg_ref)
            ddc_sc[...] = jnp.zeros_like(ddc_sc)

        zv = z_ref[...]
        xv = x_ref[...]
        s = _silu(zv)
        yd = y_ref[...] + d_ref[...] * xv
        yg = yd * s
        dov = do_ref[...]
        for gi in range(SSM_GROUPS):
            sl = slice(gi * gw, (gi + 1) * gw)
            dyg, dgp = _rms_bwd_math(yg[:, sl], g_ref[:, sl], dov[:, sl])
            dg_ref[:, sl] += jnp.sum(dgp, axis=0, keepdims=True)
            dyd = dyg * s[:, sl]
            dy_ref[:, sl] = dyd
            dz_ref[:, sl] = (dyg * yd[:, sl] * _dsilu(zv[:, sl])).astype(bf16)
            ddc_sc[:, sl] += jnp.sum(dyd * xv[:, sl], axis=0, keepdims=True)

        @pl.when(i == nt - 1)
        def _():
            dd_ref[...] = _dot_hi(ddc_sc[...], ind_ref[...])

    return pl.pallas_call(
        body, name="gate_norm_bwd", grid=(nt,),
        in_specs=[_row(D_SSM, tm), _row(D_SSM, tm), _row(D_SSM, tm), _full((1, D_SSM)), _full((1, D_SSM)),
                  _row(D_SSM, tm), _full((D_SSM, LANES))],
        out_specs=[_row(D_SSM, tm), _row(D_SSM, tm), _full((1, D_SSM)), _full((1, LANES))],
        out_shape=[jax.ShapeDtypeStruct((M, D_SSM), f32), jax.ShapeDtypeStruct((M, D_SSM), bf16),
                   jax.ShapeDtypeStruct((1, D_SSM), f32), jax.ShapeDtypeStruct((1, LANES), f32)],
        scratch_shapes=[pltpu.VMEM((1, D_SSM), f32)],
        compiler_params=_cparams(("arbitrary",)),
    )(y, xbc_c, z, d_exp, g, dout, head_ind)


_PEER_FLIPS = [(0, 0, 1), (0, 1, 0), (0, 1, 1), (1, 0, 0), (1, 0, 1), (1, 1, 0), (1, 1, 1)]


def _exchange_copies(ins, outs, send_sems, recv_sems, loc_sems, scatter):
    n = len(ins)
    x, y, c = lax.axis_index("x"), lax.axis_index("y"), lax.axis_index("c")
    me = 4 * x + 2 * y + c
    copies = []
    for a in range(n):
        src = ins[a].at[me] if scatter else ins[a]
        copies.append(pltpu.make_async_copy(src, outs[a].at[me], loc_sems.at[a]))
    for p, (fx, fy, fc) in enumerate(_PEER_FLIPS):
        tx = 1 - x if fx else x
        ty = 1 - y if fy else y
        tc = 1 - c if fc else c
        tgt = 4 * tx + 2 * ty + tc
        for a in range(n):
            src = ins[a].at[tgt] if scatter else ins[a]
            copies.append(pltpu.make_async_remote_copy(
                src_ref=src, dst_ref=outs[a].at[me],
                send_sem=send_sems.at[p * n + a], recv_sem=recv_sems.at[p * n + a],
                device_id=(tx, ty, tc), device_id_type=_MESH))
    return copies


def _exchange_shapes(arrays, scatter):
    return [jax.ShapeDtypeStruct(a.shape if scatter else (N_DEV,) + a.shape, a.dtype) for a in arrays]


def _exchange_sems(n):
    return [pltpu.SemaphoreType.DMA((7 * n,)), pltpu.SemaphoreType.DMA((7 * n,)), pltpu.SemaphoreType.DMA((n,))]


def _exchange(arrays, scatter, name):
    n = len(arrays)

    def body(*refs):
        copies = _exchange_copies(refs[:n], refs[n:2 * n], *refs[2 * n:], scatter)
        for cp in copies:
            cp.start()
        for cp in copies:
            cp.wait()

    any_spec = pl.BlockSpec(memory_space=pl.ANY)
    return pl.pallas_call(
        body, name=name, in_specs=[any_spec] * n, out_specs=[any_spec] * n,
        out_shape=_exchange_shapes(arrays, scatter), scratch_shapes=_exchange_sems(n),
    )(*arrays)


def _exchange_tail(scattered, gathered, name):
    ns, ng = len(scattered), len(gathered)
    n = ns + ng

    def body(*refs):
        sems = refs[2 * n:]
        copies = (_exchange_copies(refs[:ns], refs[n:n + ns], *sems[:3], True)
                  + _exchange_copies(refs[ns:n], refs[n + ns:2 * n], *sems[3:], False))
        for cp in copies:
            cp.start()
        for cp in copies:
            cp.wait()

    any_spec = pl.BlockSpec(memory_space=pl.ANY)
    return pl.pallas_call(
        body, name=name, in_specs=[any_spec] * n, out_specs=[any_spec] * n,
        out_shape=_exchange_shapes(scattered, True) + _exchange_shapes(gathered, False),
        scratch_shapes=_exchange_sems(ns) + _exchange_sems(ng),
    )(*scattered, *gathered)


def _adamw_math(g, w, m, v):
    c1 = 1.0 - ADAM_B1 ** ADAM_STEP
    c2 = 1.0 - ADAM_B2 ** ADAM_STEP
    mn = ADAM_B1 * m + (1.0 - ADAM_B1) * g
    vn = ADAM_B2 * v + (1.0 - ADAM_B2) * (g * g)
    m_hat = mn / c1
    v_hat = vn / c2
    return -ADAM_LR * (m_hat / (jnp.sqrt(v_hat) + ADAM_EPS) + ADAM_WD * w), mn, vn


def _adamw(parts, w, m, v, name):
    R, C = w.shape
    tr = _pick(R, (PACK_ROW_TILE, 64, 32, 16, 8))

    def body(p_ref, w_ref, m_ref, v_ref, g_ref, d_ref, nm_ref, nv_ref):
        g = p_ref[0].astype(f32)
        for s in range(1, N_DEV):
            g = g + p_ref[s].astype(f32)
        g_ref[...] = g
        d_ref[...], nm_ref[...], nv_ref[...] = _adamw_math(g, w_ref[...], m_ref[...], v_ref[...])

    spec = pl.BlockSpec((tr, C), lambda i: (i, 0))
    return pl.pallas_call(
        body, name=name, grid=(R // tr,),
        in_specs=[pl.BlockSpec((N_DEV, tr, C), lambda i: (0, i, 0)), spec, spec, spec],
        out_specs=[spec] * 4, out_shape=[jax.ShapeDtypeStruct((R, C), f32)] * 4,
        compiler_params=_cparams(("parallel",)),
    )(parts, w, m, v)


def _adamw_replicated(parts, ws, ms, vs):
    n = len(ws)
    R = parts.shape[1]
    sizes = [int(w.shape[1]) for w in ws]

    def body(*refs):
        p_ref = refs[0]
        w_refs, m_refs, v_refs = refs[1:1 + n], refs[1 + n:1 + 2 * n], refs[1 + 2 * n:1 + 3 * n]
        loss_ref = refs[1 + 3 * n]
        outs = refs[2 + 3 * n:]
        g_all = p_ref[0]
        for s in range(1, N_DEV):
            g_all = g_all + p_ref[s]
        row = 0
        for p in range(n):
            pieces, left = [], sizes[p]
            while left > 0:
                take = min(left, PACK_W)
                pieces.append(g_all[row:row + 1, 0:take])
                left -= take
                row += 1
            g = pieces[0] if len(pieces) == 1 else jnp.concatenate(pieces, axis=1)
            d, mn, vn = _adamw_math(g, w_refs[p][...], m_refs[p][...], v_refs[p][...])
            outs[4 * p][...] = g
            outs[4 * p + 1][...] = d
            outs[4 * p + 2][...] = mn
            outs[4 * p + 3][...] = vn
        loss_ref[...] = g_all[row:row + 1, 0:LANES]

    in_specs = [_full((N_DEV, R, PACK_W))] + [_full((1, s)) for s in sizes] * 3
    out_specs = [_full((1, LANES))]
    out_shape = [jax.ShapeDtypeStruct((1, LANES), f32)]
    for s in sizes:
        out_specs += [_full((1, s))] * 4
        out_shape += [jax.ShapeDtypeStruct((1, s), f32)] * 4
    res = pl.pallas_call(
        body, name="adamw_replicated", in_specs=in_specs, out_specs=out_specs, out_shape=out_shape,
        compiler_params=pltpu.CompilerParams(vmem_limit_bytes=VMEM_LIMIT),
    )(parts, *ws, *ms, *vs)
    return res[0], [res[1 + 4 * p:5 + 4 * p] for p in range(n)]


def _flat_rows(a, lead_ndim):
    lead = a.shape[:lead_ndim]
    n = int(np.prod(a.shape[lead_ndim:]))
    a = a.reshape(lead + (n,))
    pad = (-n) % PACK_W
    if pad:
        a = jnp.pad(a, [(0, 0)] * lead_ndim + [(0, pad)])
    return a.reshape(lead + ((n + pad) // PACK_W, PACK_W))


def _pack(arrays, lead_ndim, total_rows, dtype):
    rows = [_flat_rows(a.astype(dtype), lead_ndim) for a in arrays]
    cat = jnp.concatenate(rows, axis=lead_ndim)
    pad = total_rows - cat.shape[lead_ndim]
    if pad:
        cat = jnp.pad(cat, [(0, 0)] * lead_ndim + [(0, pad), (0, 0)])
    return cat


def _unpack(buf, shapes, lead_ndim):
    out = []
    r = 0
    lead = buf.shape[:lead_ndim]
    for shp in shapes:
        n = int(np.prod(shp))
        nr = -(-n // PACK_W)
        piece = lax.slice_in_dim(buf, r, r + nr, axis=lead_ndim)
        piece = piece.reshape(lead + (nr * PACK_W,))
        piece = lax.slice_in_dim(piece, 0, n, axis=lead_ndim)
        out.append(piece.reshape(lead + tuple(shp)))
        r += nr
    return out


def _round_up(n, m):
    return -(-n // m) * m


def kernel(x, meta_tokens, norm_mix_pre, norm_mix_post, norm_ffn_pre, norm_ffn_post, w_in, q_a_norm, w_uq, kv_a_norm, w_ukv, attn_out_norm, ssm_conv_w, ssm_conv_b, ssm_dt_bias, ssm_A_log, ssm_D, ssm_norm, w_out, w_up, ffn_conv_w, ffn_conv_b, w_down, loss_target, m_meta_tokens, m_norm_mix_pre, m_norm_mix_post, m_norm_ffn_pre, m_norm_ffn_post, m_w_in, m_q_a_norm, m_w_uq, m_kv_a_norm, m_w_ukv, m_attn_out_norm, m_ssm_conv_w, m_ssm_conv_b, m_ssm_dt_bias, m_ssm_A_log, m_ssm_D, m_ssm_norm, m_w_out, m_w_up, m_ffn_conv_w, m_ffn_conv_b, m_w_down, v_meta_tokens, v_norm_mix_pre, v_norm_mix_post, v_norm_ffn_pre, v_norm_ffn_post, v_w_in, v_q_a_norm, v_w_uq, v_kv_a_norm, v_w_ukv, v_attn_out_norm, v_ssm_conv_w, v_ssm_conv_b, v_ssm_dt_bias, v_ssm_A_log, v_ssm_D, v_ssm_norm, v_w_out, v_w_up, v_ffn_conv_w, v_ffn_conv_b, v_w_down):
    seq = x.shape[1]
    n_real = N_META + seq
    Lp = _round_up(n_real, 768) if n_real > 2048 else _round_up(n_real, ROW_TILE)
    D = D_MODEL

    early_w = [w_uq, w_ukv]
    late_w = [w_out, w_down]
    sharded_s = [meta_tokens, ssm_conv_w, ffn_conv_w]
    grp_a = dict(names=["w_out", "w_down"], w=late_w, m=[m_w_out, m_w_down],
                 v=[v_w_out, v_w_down])
    grp_b = dict(names=["w_uq", "w_ukv", "ssm_conv_w", "ffn_conv_w"],
                 w=early_w + [ssm_conv_w, ffn_conv_w],
                 m=[m_w_uq, m_w_ukv, m_ssm_conv_w, m_ffn_conv_w],
                 v=[v_w_uq, v_w_ukv, v_ssm_conv_w, v_ffn_conv_w])
    grp_meta = dict(names=["meta_tokens"], w=[meta_tokens], m=[m_meta_tokens], v=[v_meta_tokens])
    repl_w = [norm_mix_pre, norm_mix_post, norm_ffn_pre, norm_ffn_post, q_a_norm, kv_a_norm, attn_out_norm,
              ssm_conv_b, ssm_dt_bias, ssm_A_log, ssm_D, ssm_norm, ffn_conv_b]
    repl_m = [m_norm_mix_pre, m_norm_mix_post, m_norm_ffn_pre, m_norm_ffn_post, m_q_a_norm, m_kv_a_norm,
              m_attn_out_norm, m_ssm_conv_b, m_ssm_dt_bias, m_ssm_A_log, m_ssm_D, m_ssm_norm, m_ffn_conv_b]
    repl_v = [v_norm_mix_pre, v_norm_mix_post, v_norm_ffn_pre, v_norm_ffn_post, v_q_a_norm, v_kv_a_norm,
              v_attn_out_norm, v_ssm_conv_b, v_ssm_dt_bias, v_ssm_A_log, v_ssm_D, v_ssm_norm, v_ffn_conv_b]

    def pack_rows(arrs, lead):
        return _round_up(sum(-(-int(np.prod(a.shape[lead:])) // PACK_W) for a in arrs), 16)

    wb = _pack(early_w, 0, pack_rows(early_w, 0), bf16)
    wl = _pack(late_w, 0, pack_rows(late_w, 0), bf16)
    ws = _pack(sharded_s, 0, pack_rows(sharded_s, 0), f32)
    wb_all, ws_all, win_all = _exchange([wb, ws, w_in[0].astype(bf16)], False, "gather_weights")
    g_w_uq, g_w_ukv = _unpack(wb_all, [a.shape for a in early_w], 1)
    g_meta, g_sconv, g_fconv = _unpack(ws_all, [a.shape for a in sharded_s], 1)

    def cols(gathered):
        t = gathered[:, 0]
        return jnp.transpose(t, (1, 0, 2)).reshape(t.shape[1], N_DEV * t.shape[2])

    win = cols(win_all[:, None])
    o = np.cumsum((0, Q_RANK, KV_RANK, QK_ROPE, D_SSM, D_XBC, SSM_HEADS))
    w_q, w_kv = win[:, o[0]:o[1]], win[:, o[1]:o[2]]
    w_rope = jnp.pad(win[:, o[2]:o[3]], ((0, 0), (0, LANES - QK_ROPE)))
    w_z, w_xbc = win[:, o[3]:o[4]], win[:, o[4]:o[5]]
    w_dt = jnp.pad(win[:, o[5]:o[6]], ((0, 0), (0, LANES - SSM_HEADS)))
    wuq = g_w_uq.reshape(Q_RANK, MLA_HEADS, QK_NOPE + QK_ROPE)
    wuq = jnp.pad(wuq, ((0, 0), (0, 0), (0, QK_PAD - QK_NOPE - QK_ROPE))).reshape(Q_RANK, MLA_HEADS * QK_PAD)
    wukv = g_w_ukv.reshape(KV_RANK, MLA_HEADS * (QK_NOPE + V_DIM))
    meta_full = jnp.transpose(g_meta, (1, 0, 2)).reshape(N_META, D)
    sconv_w = jnp.pad(cols(g_sconv), ((0, SUBLANES - SSM_CONV), (0, 0)))
    fconv_w = jnp.pad(cols(g_fconv), ((0, SUBLANES - FFN_CONV), (0, 0)))

    pos = jnp.arange(Lp, dtype=f32)
    inv = ROPE_THETA ** (-jnp.arange(0, QK_ROPE, 2, dtype=f32) / QK_ROPE)
    ang = pos[:, None] * inv[None, :]
    cs_, sn_ = jnp.cos(ang), jnp.sin(ang)
    zpad = jnp.zeros((Lp, LANES - QK_ROPE), f32)
    cos_t = jnp.concatenate([cs_, cs_, zpad], axis=1)
    sin_t = jnp.concatenate([-sn_, sn_, zpad], axis=1)
    dt_bias_p = jnp.pad(ssm_dt_bias, ((0, 0), (0, LANES - SSM_HEADS)))
    a_neg = -jnp.exp(ssm_A_log)
    a_row = jnp.pad(a_neg, ((0, 0), (0, LANES - SSM_HEADS)))
    a_col = jnp.broadcast_to(a_neg.reshape(SSM_HEADS, 1), (SSM_HEADS, LANES))
    d_exp = jnp.repeat(ssm_D, SSM_P, axis=1)
    a_e = jnp.repeat(a_neg, SSM_P, axis=1)
    head_ind = (jnp.arange(D_SSM)[:, None] // SSM_P == jnp.arange(LANES)[None, :]).astype(f32)

    xb = x[0]
    h0 = jnp.concatenate([meta_full, xb, jnp.zeros((Lp - n_real, D), f32)], axis=0)
    tgt = jnp.pad(loss_target[0], ((N_META, Lp - n_real), (0, 0)))
    hn1 = _rms_fwd(h0, norm_mix_pre, bf16, "norm_mix_pre")
    q_c = _mm([(hn1, w_q)], f32, False, "proj_q")
    kv_c = _mm([(hn1, w_kv)], f32, False, "proj_kv")
    kpe_raw = _mm([(hn1, w_rope)], f32, False, "proj_rope")
    z = _mm([(hn1, w_z)], f32, False, "proj_z")
    xbc = _mm([(hn1, w_xbc)], f32, False, "proj_xbc")
    dt_raw = _mm([(hn1, w_dt)], f32, False, "proj_dt")

    qn = _rms_fwd(q_c, q_a_norm, bf16, "norm_q")
    kvn = _rms_fwd(kv_c, kv_a_norm, bf16, "norm_kv")
    qh = _up_q_rope(qn, wuq, cos_t, sin_t)
    kh, vh = _up_kv_rope(kvn, wukv, kpe_raw, cos_t, sin_t)
    attn, lse, wl_all, wup_all = _flash_fwd(qh, kh, vh, [wl, w_up[0].astype(bf16)], False)
    g_w_out, g_w_down = _unpack(wl_all, [a.shape for a in late_w], 1)
    wout = g_w_out.reshape(D_ATTN + D_SSM, D)
    wout_a, wout_s = wout[:D_ATTN], wout[D_ATTN:]
    wup = cols(wup_all[:, None])
    wdown = g_w_down.reshape(D_FF, D)
    an = _rms_fwd(attn, attn_out_norm, bf16, "norm_attn_out")

    xbc_c = _ssm_conv_fwd(xbc, sconv_w, ssm_conv_b)
    dtp, dt_e = _dt_fwd(dt_raw, dt_bias_p, jnp.transpose(head_ind))
    dtt = jnp.transpose(dtp[:, :SSM_HEADS])
    y_ssd, hin = _ssd_fwd(xbc_c, dt_e, dtt, a_e, a_col)
    ssm = _gate_norm_fwd(y_ssd, xbc_c, z, d_exp, ssm_norm)

    mix = _mm([(an, wout_a), (ssm, wout_s)], f32, False, "out_proj")
    h1, hn2 = _resid_norm(h0, mix, norm_mix_post, norm_ffn_pre)
    up = _mm([(hn2, wup)], f32, False, "ffn_up")
    act = _ffn_gate_fwd(up, fconv_w, ffn_conv_b)
    down = _mm([(act, wdown)], f32, False, "ffn_down")
    dh2, d_down, dg_ffn_post, loss_part = _final(h1, down, norm_ffn_post, tgt, n_real)

    d_act = _mm([(d_down, wdown)], f32, True, "ffn_down_dx")
    dw_down = _mm_tn(act, d_down, "ffn_down_dw")
    dup_g, dup_v, dwc_g, dwc_v, dbc_g, dbc_v = _ffn_gate_bwd(up, fconv_w, ffn_conv_b, d_act)
    d_hn2 = _mm([(dup_g, wup[:, :D_FF]), (dup_v, wup[:, D_FF:])], f32, True, "ffn_up_dx")
    dw_up = jnp.concatenate([_mm_tn(hn2, dup_g, "ffn_up_dw_g"), _mm_tn(hn2, dup_v, "ffn_up_dw_v")], axis=1)
    dh1, d_mix, dg_ffn_pre, dg_mix_post = _mid_bwd(h1, norm_ffn_pre, d_hn2, dh2, mix, norm_mix_post)
    d_an = _mm([(d_mix, wout_a)], f32, True, "out_proj_dx_a")
    d_ssm = _mm([(d_mix, wout_s)], f32, True, "out_proj_dx_s")
    dw_out = jnp.concatenate([_mm_tn(an, d_mix, "out_proj_dw_a"), _mm_tn(ssm, d_mix, "out_proj_dw_s")], axis=0)

    do_h, delta, dg_attn_out = _attn_out_bwd(attn, attn_out_norm, d_an)
    def col_blocks(gm):
        r, cc = gm.shape
        return jnp.transpose(gm.reshape(r, N_DEV, cc // N_DEV), (1, 0, 2))

    blocks_a = [dw_out.reshape(N_DEV, (D_ATTN + D_SSM) // N_DEV, D), dw_down.reshape(N_DEV, D_FF // N_DEV, D)]
    gpack_a = _pack(blocks_a, 1, pack_rows(blocks_a, 1), bf16)
    dqh, dkh, dvh, gparts_a, gparts_up = _flash_bwd(qh, kh, vh, do_h, lse, delta,
                                                    [gpack_a, col_blocks(dw_up).astype(bf16)], True)
    dq_full = _rope_q_bwd(dqh, cos_t, sin_t)
    dkv_full, d_kpe_raw = _rope_k_bwd(dkh, dvh, cos_t, sin_t)
    d_qn = _mm([(dq_full, wuq)], f32, True, "up_q_dx")
    dw_uq = _mm_tn(qn, dq_full, "up_q_dw")
    d_kvn = _mm([(dkv_full, wukv)], f32, True, "up_kv_dx")
    dw_ukv = _mm_tn(kvn, dkv_full, "up_kv_dw")
    d_q_c, dg_q = _rms_bwd(q_c, q_a_norm, d_qn, bf16, "norm_q_bwd")
    d_kv_c, dg_kv = _rms_bwd(kv_c, kv_a_norm, d_kvn, bf16, "norm_kv_bwd")

    dy_ssd, dz, dg_ssm, dd_heads = _gate_norm_bwd(y_ssd, xbc_c, z, d_exp, ssm_norm, d_ssm, head_ind)
    d_xbc_c, ddt, da_heads = _ssd_bwd(xbc_c, dtp, dt_e, dtt, a_row, a_e, a_col, hin, dy_ssd, d_exp, head_ind)
    d_xbc, dw_sconv, db_sconv = _ssm_conv_bwd(xbc, sconv_w, ssm_conv_b, d_xbc_c)
    d_dt_raw, d_dt_bias = _dt_bwd(dt_raw, dt_bias_p, ddt)

    dw_q = _mm_tn(hn1, d_q_c, "proj_dw_q")
    dw_kv = _mm_tn(hn1, d_kv_c, "proj_dw_kv")
    dw_rope = _mm_tn(hn1, d_kpe_raw, "proj_dw_rope")
    dw_z = _mm_tn(hn1, dz, "proj_dw_z")
    dw_xbc = _mm_tn(hn1, d_xbc, "proj_dw_xbc")
    dw_dt = _mm_tn(hn1, d_dt_raw, "proj_dw_dt")
    dw_in = jnp.concatenate([dw_q, dw_kv, dw_rope[:, :QK_ROPE], dw_z, dw_xbc, dw_dt[:, :SSM_HEADS]], axis=1)
    dw_uq3 = dw_uq.reshape(Q_RANK, MLA_HEADS, QK_PAD)[:, :, :QK_NOPE + QK_ROPE]
    blocks_b = [
        dw_uq3.reshape(N_DEV, Q_RANK // N_DEV, MLA_HEADS, QK_NOPE + QK_ROPE),
        dw_ukv.reshape(N_DEV, KV_RANK // N_DEV, MLA_HEADS, QK_NOPE + V_DIM),
        col_blocks(dw_sconv[:SSM_CONV]),
        col_blocks(jnp.concatenate([dwc_g, dwc_v], axis=1)[:FFN_CONV]),
    ]
    gpack_b = _pack(blocks_b, 1, pack_rows(blocks_b, 1), bf16)
    segs = [(d_q_c, w_q), (d_kv_c, w_kv), (d_kpe_raw, w_rope), (dz, w_z), (d_xbc, w_xbc), (d_dt_raw, w_dt)]
    d_hn1, gparts_b, gparts_in = _mm(segs, f32, True, "proj_dx",
                                     carried=[gpack_b, col_blocks(dw_in).astype(bf16)], scatter=True)
    dh0, dg_mix_pre = _rms_bwd(h0, norm_mix_pre, d_hn1, f32, "norm_mix_pre_bwd", residual=dh1)

    grad_x = dh0[N_META:n_real][None]
    meta_blocks = col_blocks(dh0[:N_META]).reshape(N_DEV, N_META * D // N_DEV // PACK_W, PACK_W)


    def adam_group(parts, grp, name):
        rows = parts.shape[1]
        packs = [_pack([a[None] for a in grp[k]], 1, rows, f32)[0] for k in ("w", "m", "v")]
        outs = _adamw(parts, *packs, name)
        shapes = [a.shape for a in grp["w"]]
        return [dict(zip(grp["names"], [t[0] for t in _unpack(b[None], shapes, 1)])) for b in outs]

    def adam_own_layout(parts, name, w, m, v):
        return [{name: t[None]} for t in _adamw(parts, w[0], m[0], v[0], "adamw_" + name)]

    sh_a = adam_group(gparts_a, grp_a, "adamw_sharded_a")
    sh_b = adam_group(gparts_b, grp_b, "adamw_sharded_b")
    sh_in = adam_own_layout(gparts_in, "w_in", w_in, m_w_in, v_w_in)
    sh_up = adam_own_layout(gparts_up, "w_up", w_up, m_w_up, v_w_up)

    dg_alog = da_heads[:, :SSM_HEADS] * a_neg
    repl_g = [dg_mix_pre, dg_mix_post, dg_ffn_pre, dg_ffn_post, dg_q, dg_kv, dg_attn_out, db_sconv,
              d_dt_bias[:, :SSM_HEADS], dg_alog, dd_heads[:, :SSM_HEADS], dg_ssm,
              jnp.concatenate([dbc_g, dbc_v], axis=1)]
    loss_vec = loss_part[:, :1]
    small_total = _round_up(sum(-(-int(np.prod(a.shape)) // PACK_W) for a in repl_g) + 1, 16)
    spack = _pack(repl_g + [loss_vec], 0, small_total, f32)
    gparts_meta, sparts = _exchange_tail([meta_blocks], [spack], "exchange_tail")
    sh_meta = adam_group(gparts_meta, grp_meta, "adamw_meta")
    loss_row, repl_out = _adamw_replicated(sparts, repl_w, repl_m, repl_v)
    loss = loss_row[0, 0]

    order = ["meta_tokens", "norm_mix_pre", "norm_mix_post", "norm_ffn_pre", "norm_ffn_post", "w_in", "q_a_norm",
             "w_uq", "kv_a_norm", "w_ukv", "attn_out_norm", "ssm_conv_w", "ssm_conv_b", "ssm_dt_bias", "ssm_A_log",
             "ssm_D", "ssm_norm", "w_out", "w_up", "ffn_conv_w", "ffn_conv_b", "w_down"]
    rp_names = ["norm_mix_pre", "norm_mix_post", "norm_ffn_pre", "norm_ffn_post", "q_a_norm", "kv_a_norm",
                "attn_out_norm", "ssm_conv_b", "ssm_dt_bias", "ssm_A_log", "ssm_D", "ssm_norm", "ffn_conv_b"]

    def lookup(k):
        d = {**sh_a[k], **sh_b[k], **sh_in[k], **sh_up[k], **sh_meta[k],
             **{n: four[k] for n, four in zip(rp_names, repl_out)}}
        return [d[n] for n in order]

    return (loss, grad_x, *lookup(0), *lookup(1), *lookup(2), *lookup(3))
```

```python
import functools
import math

import jax
import jax.numpy as jnp
import numpy as np
from jax import lax
from jax.experimental import pallas as pl
from jax.experimental.pallas import tpu as pltpu

f32 = jnp.float32
bf16 = jnp.bfloat16

D_MODEL = 1024
SEQ = 8192
N_META = 16
MLA_HEADS = 8
QK_NOPE = 128
QK_ROPE = 64
V_DIM = 128
Q_RANK = 384
KV_RANK = 256
ROPE_THETA = 10000.0
SOFTMAX_SCALE = (QK_NOPE + QK_ROPE) ** -0.5
D_ATTN = MLA_HEADS * V_DIM
SSM_HEADS = 16
SSM_P = 64
SSM_GROUPS = 2
SSM_HPG = SSM_HEADS // SSM_GROUPS
SSM_N = 128
SSM_CONV = 4
CHUNK = 128
D_SSM = SSM_HEADS * SSM_P
D_BC = SSM_GROUPS * SSM_N
D_XBC = D_SSM + 2 * D_BC
D_FF = 2816
FFN_CONV = 3
EPS = 1e-6
D_IN = Q_RANK + KV_RANK + QK_ROPE + D_SSM + D_XBC + SSM_HEADS
QK_PAD = 256
N_DEV = 8

ADAM_LR = 0.001
ADAM_B1 = 0.9
ADAM_B2 = 0.999
ADAM_EPS = 1e-08
ADAM_WD = 0.01
ADAM_STEP = 10

LANES = 128
SUBLANES = 8
ROW_TILE = 256
VMEM_LIMIT = 56 * 1024 * 1024
PACK_W = 1024
PACK_ROW_TILE = 128
NEG = -1e30
LOG2E = math.log2(math.e)
LN2 = math.log(2.0)
Q_PRESCALE = SOFTMAX_SCALE * LOG2E

_MESH = pl.DeviceIdType.MESH


def _pick(n, prefs):
    for p in prefs:
        if n % p == 0:
            return p
    return n


def _rt(m):
    return _pick(m, (384, ROW_TILE))


def _cparams(sem):
    return pltpu.CompilerParams(dimension_semantics=sem, vmem_limit_bytes=VMEM_LIMIT)


def _row(spec_cols, tm):
    return pl.BlockSpec((tm, spec_cols), lambda i: (i, 0))


def _full(shape):
    nd = len(shape)
    return pl.BlockSpec(shape, lambda *a: (0,) * nd)


def _sigmoid(x):
    return 1.0 / (1.0 + jnp.exp(-x))


def _silu(x):
    return x * _sigmoid(x)


def _dsilu(x):
    s = _sigmoid(x)
    return s * (1.0 + x * (1.0 - s))


def _dot(a, b):
    return jnp.dot(a, b, preferred_element_type=f32)


def _dot_nt(a, b):
    return lax.dot_general(a, b, (((1,), (1,)), ((), ())), preferred_element_type=f32)


def _dot_tn(a, b):
    return lax.dot_general(a, b, (((0,), (0,)), ((), ())), preferred_element_type=f32)


def _dot_hi(a, b):
    return jnp.dot(a, b, precision=lax.Precision.HIGHEST, preferred_element_type=f32)


def _mm(pairs, out_dtype, trans_b, name, carried=(), scatter=False):
    n = len(pairs)
    nx = len(carried)
    M = pairs[0][0].shape[0]
    N = pairs[0][1].shape[0] if trans_b else pairs[0][1].shape[1]
    tm = _pick(M, (768, 512, 256))
    tn = _pick(N, (512, 1408, 384, 256, 128))
    ni, nj = M // tm, N // tn

    def body(*refs):
        o_ref = refs[2 * n + nx]
        if nx:
            i, j = pl.program_id(0), pl.program_id(1)
            _hosted_exchange(refs[2 * n:2 * n + nx], refs[2 * n + nx + 1:2 * n + 2 * nx + 1],
                             refs[2 * n + 2 * nx + 1:], scatter,
                             (i == 0) & (j == 0), (i == ni - 1) & (j == nj - 1))
        acc = None
        for p in range(n):
            a = refs[2 * p][...].astype(bf16)
            b = refs[2 * p + 1][...].astype(bf16)
            r = _dot_nt(a, b) if trans_b else _dot(a, b)
            acc = r if acc is None else acc + r
        o_ref[...] = acc.astype(out_dtype)

    in_specs, args = [], []
    for a, b in pairs:
        k = a.shape[1]
        in_specs.append(pl.BlockSpec((tm, k), lambda i, j: (i, 0)))
        if trans_b:
            in_specs.append(pl.BlockSpec((tn, k), lambda i, j: (j, 0)))
        else:
            in_specs.append(pl.BlockSpec((k, tn), lambda i, j: (0, j)))
        args += [a, b]
    out_spec = pl.BlockSpec((tm, tn), lambda i, j: (i, j))
    out_shape = jax.ShapeDtypeStruct((M, N), out_dtype)
    if not nx:
        return pl.pallas_call(
            body, name=name, grid=(ni, nj), in_specs=in_specs, out_specs=out_spec, out_shape=out_shape,
            compiler_params=_cparams(("parallel", "parallel")),
        )(*args)
    any_spec = pl.BlockSpec(memory_space=pl.ANY)
    return pl.pallas_call(
        body, name=name, grid=(ni, nj), in_specs=in_specs + [any_spec] * nx,
        out_specs=[out_spec] + [any_spec] * nx,
        out_shape=[out_shape] + _exchange_shapes(carried, scatter),
        scratch_shapes=_exchange_sems(nx),
        compiler_params=_cparams(("arbitrary", "arbitrary")),
    )(*args, *carried)


def _mm_tn(a, g, name):
    M, K = a.shape
    N = g.shape[1]
    tm = _pick(M, (768, 512, 256))
    tk = _pick(K, (1024, 1408, 512, 384, 256))
    tn = _pick(N, (1024, 1408, 512, 384, 256, 128))

    def body(a_ref, g_ref, o_ref):
        @pl.when(pl.program_id(2) == 0)
        def _():
            o_ref[...] = jnp.zeros_like(o_ref)

        o_ref[...] += _dot_tn(a_ref[...].astype(bf16), g_ref[...].astype(bf16))

    return pl.pallas_call(
        body, name=name, grid=(K // tk, N // tn, M // tm),
        in_specs=[pl.BlockSpec((tm, tk), lambda k, j, m: (m, k)),
                  pl.BlockSpec((tm, tn), lambda k, j, m: (m, j))],
        out_specs=pl.BlockSpec((tk, tn), lambda k, j, m: (k, j)),
        out_shape=jax.ShapeDtypeStruct((K, N), f32),
        compiler_params=_cparams(("parallel", "parallel", "arbitrary")),
    )(a, g)


def _rstd(x):
    return lax.rsqrt(jnp.mean(x * x, axis=-1, keepdims=True) + EPS)


def _rms_bwd_math(x, g, dy):
    r = _rstd(x)
    xh = x * r
    dn = dy * g
    dx = r * (dn - xh * jnp.mean(dn * xh, axis=-1, keepdims=True))
    return dx, dy * xh


def _rms_fwd(x, g, out_dtype, name):
    M, K = x.shape
    tm = _rt(M)

    def body(x_ref, g_ref, o_ref):
        xv = x_ref[...]
        o_ref[...] = (xv * _rstd(xv) * g_ref[...]).astype(out_dtype)

    return pl.pallas_call(
        body, name=name, grid=(M // tm,), in_specs=[_row(K, tm), _full((1, K))],
        out_specs=_row(K, tm), out_shape=jax.ShapeDtypeStruct((M, K), out_dtype),
        compiler_params=_cparams(("parallel",)),
    )(x, g)


def _rms_bwd(x, g, dy, out_dtype, name, residual=None):
    M, K = x.shape
    tm = _rt(M)
    has_res = residual is not None

    def body(*refs):
        if has_res:
            x_ref, g_ref, dy_ref, r_ref, dx_ref, dg_ref = refs
        else:
            x_ref, g_ref, dy_ref, dx_ref, dg_ref = refs

        @pl.when(pl.program_id(0) == 0)
        def _():
            dg_ref[...] = jnp.zeros_like(dg_ref)

        dx, dgp = _rms_bwd_math(x_ref[...], g_ref[...], dy_ref[...].astype(f32))
        if has_res:
            dx = dx + r_ref[...]
        dx_ref[...] = dx.astype(out_dtype)
        dg_ref[...] += jnp.sum(dgp, axis=0, keepdims=True)

    ins = [x, g, dy] + ([residual] if has_res else [])
    in_specs = [_row(K, tm), _full((1, K)), _row(K, tm)] + ([_row(K, tm)] if has_res else [])
    return pl.pallas_call(
        body, name=name, grid=(M // tm,), in_specs=in_specs,
        out_specs=[_row(K, tm), _full((1, K))],
        out_shape=[jax.ShapeDtypeStruct((M, K), out_dtype), jax.ShapeDtypeStruct((1, K), f32)],
        compiler_params=_cparams(("arbitrary",)),
    )(*ins)


def _in_proj(h0, g, weights):
    M, K = h0.shape
    tm = _rt(M)
    n = len(weights)
    widths = [int(w.shape[1]) for w in weights]

    def body(x_ref, g_ref, *refs):
        xv = x_ref[...]
        hn = (xv * _rstd(xv) * g_ref[...]).astype(bf16)
        refs[n][...] = hn
        for p in range(n):
            refs[n + 1 + p][...] = _dot(hn, refs[p][...])

    return pl.pallas_call(
        body, name="in_proj", grid=(M // tm,),
        in_specs=[_row(K, tm), _full((1, K))] + [_full((K, wd)) for wd in widths],
        out_specs=[_row(K, tm)] + [_row(wd, tm) for wd in widths],
        out_shape=[jax.ShapeDtypeStruct((M, K), bf16)] + [jax.ShapeDtypeStruct((M, wd), f32) for wd in widths],
        compiler_params=_cparams(("parallel",)),
    )(h0, g, *weights)


def _in_proj_dw(hn, grads):
    M, K = hn.shape
    tm = _rt(M)
    n = len(grads)
    widths = [int(gr.shape[1]) for gr in grads]

    def body(a_ref, *refs):
        @pl.when(pl.program_id(0) == 0)
        def _():
            for p in range(n):
                refs[n + p][...] = jnp.zeros_like(refs[n + p])

        a = a_ref[...]
        for p in range(n):
            refs[n + p][...] += _dot_tn(a, refs[p][...].astype(bf16))

    return pl.pallas_call(
        body, name="in_proj_dw", grid=(M // tm,),
        in_specs=[_row(K, tm)] + [_row(wd, tm) for wd in widths],
        out_specs=[_full((K, wd)) for wd in widths],
        out_shape=[jax.ShapeDtypeStruct((K, wd), f32) for wd in widths],
        compiler_params=_cparams(("arbitrary",)),
    )(hn, *grads)


def _resid_norm(h0, mix, g2, g3):
    M, K = h0.shape
    tm = _rt(M)

    def body(h_ref, m_ref, g2_ref, g3_ref, h1_ref, hn_ref):
        mv = m_ref[...]
        h1 = h_ref[...] + mv * _rstd(mv) * g2_ref[...]
        h1_ref[...] = h1
        hn_ref[...] = (h1 * _rstd(h1) * g3_ref[...]).astype(bf16)

    return pl.pallas_call(
        body, name="resid_norm", grid=(M // tm,),
        in_specs=[_row(K, tm), _row(K, tm), _full((1, K)), _full((1, K))],
        out_specs=[_row(K, tm), _row(K, tm)],
        out_shape=[jax.ShapeDtypeStruct((M, K), f32), jax.ShapeDtypeStruct((M, K), bf16)],
        compiler_params=_cparams(("parallel",)),
    )(h0, mix, g2, g3)


def _final(h1, down, g4, tgt, n_real):
    M, K = h1.shape
    tm = _rt(M)
    nt = M // tm

    def body(h_ref, d_ref, g_ref, t_ref, dh_ref, dd_ref, dg_ref, ls_ref, acc_ref):
        i = pl.program_id(0)

        @pl.when(i == 0)
        def _():
            dg_ref[...] = jnp.zeros_like(dg_ref)
            acc_ref[...] = jnp.zeros_like(acc_ref)

        dv = d_ref[...]
        g = g_ref[...]
        r = _rstd(dv)
        n = dv * r
        h2 = h_ref[...] + n * g
        rows = i * tm + lax.broadcasted_iota(jnp.int32, (tm, 1), 0)
        mask = ((rows >= N_META) & (rows < n_real)).astype(f32)
        diff = (h2 - t_ref[...]) * mask
        acc_ref[...] += jnp.sum(diff * diff, axis=0, keepdims=True)
        dh = diff * (1.0 / K)
        dh_ref[...] = dh
        dn = dh * g
        dd_ref[...] = (r * (dn - n * jnp.mean(dn * n, axis=-1, keepdims=True))).astype(bf16)
        dg_ref[...] += jnp.sum(dh * n, axis=0, keepdims=True)

        @pl.when(i == nt - 1)
        def _():
            ls_ref[...] = jnp.zeros((1, LANES), f32) + jnp.sum(acc_ref[...]) * (0.5 / K)

    return pl.pallas_call(
        body, name="final_loss", grid=(nt,),
        in_specs=[_row(K, tm), _row(K, tm), _full((1, K)), _row(K, tm)],
        out_specs=[_row(K, tm), _row(K, tm), _full((1, K)), _full((1, LANES))],
        out_shape=[jax.ShapeDtypeStruct((M, K), f32), jax.ShapeDtypeStruct((M, K), bf16),
                   jax.ShapeDtypeStruct((1, K), f32), jax.ShapeDtypeStruct((1, LANES), f32)],
        scratch_shapes=[pltpu.VMEM((1, K), f32)],
        compiler_params=_cparams(("arbitrary",)),
    )(h1, down, g4, tgt)


def _mid_bwd(h1, g3, d_hn2, dh2, mix, g2):
    M, K = h1.shape
    tm = _rt(M)

    def body(h_ref, g3_ref, dn_ref, dh2_ref, m_ref, g2_ref, dh1_ref, dm_ref, dg3_ref, dg2_ref):
        @pl.when(pl.program_id(0) == 0)
        def _():
            dg3_ref[...] = jnp.zeros_like(dg3_ref)
            dg2_ref[...] = jnp.zeros_like(dg2_ref)

        dx, dgp = _rms_bwd_math(h_ref[...], g3_ref[...], dn_ref[...])
        dh1 = dh2_ref[...] + dx
        dh1_ref[...] = dh1
        dg3_ref[...] += jnp.sum(dgp, axis=0, keepdims=True)
        dm, dgp2 = _rms_bwd_math(m_ref[...], g2_ref[...], dh1)
        dm_ref[...] = dm.astype(bf16)
        dg2_ref[...] += jnp.sum(dgp2, axis=0, keepdims=True)

    return pl.pallas_call(
        body, name="mid_bwd", grid=(M // tm,),
        in_specs=[_row(K, tm), _full((1, K)), _row(K, tm), _row(K, tm), _row(K, tm), _full((1, K))],
        out_specs=[_row(K, tm), _row(K, tm), _full((1, K)), _full((1, K))],
        out_shape=[jax.ShapeDtypeStruct((M, K), f32), jax.ShapeDtypeStruct((M, K), bf16),
                   jax.ShapeDtypeStruct((1, K), f32), jax.ShapeDtypeStruct((1, K), f32)],
        compiler_params=_cparams(("arbitrary",)),
    )(h1, g3, d_hn2, dh2, mix, g2)


HEADS_PER_STEP = 4
CONV_RB = 16


def _conv_block_taps(x_ref, halo, rb, lanes, kw):
    r0 = rb * CONV_RB
    if rb == 0:
        cat = jnp.concatenate([halo, x_ref[0:CONV_RB, lanes]], axis=0)
        first = SUBLANES - (kw - 1)
        return [cat[first + k:first + k + CONV_RB] for k in range(kw)]
    return [x_ref[r0 - (kw - 1) + k:r0 - (kw - 1) + k + CONV_RB, lanes] for k in range(kw)]


def _conv_weighted(taps, w, kw):
    u = None
    for k in range(kw):
        t = taps[k] * w[k:k + 1, :]
        u = t if u is None else u + t
    return u


def _conv_block_dx(du, nxt, w, kw):
    cat = jnp.concatenate([du, nxt], axis=0)
    return _conv_weighted([cat[kw - 1 - k:kw - 1 - k + CONV_RB] for k in range(kw)], w, kw)


def _prev_spec(tm, tc, col_of, row_axis, reversed_tiles=0):
    def imap(*ids):
        i = ids[row_axis]
        if reversed_tiles:
            i = reversed_tiles - 1 - i
        return (jnp.maximum(i * (tm // SUBLANES) - 1, 0), col_of(*ids))
    return pl.BlockSpec((SUBLANES, tc), imap)


def _ssm_conv_fwd(xbc, w, b):
    M, C = xbc.shape
    tm, tc, kw = ROW_TILE, C, SSM_CONV

    def body(x_ref, h_ref, w_ref, b_ref, o_ref):
        i = pl.program_id(0)

        def chunk(j, carry):
            lanes = pl.ds(pl.multiple_of(j * LANES, LANES), LANES)
            halo = jnp.where(i == 0, 0.0, h_ref[:, lanes])
            wv = w_ref[:, lanes]
            bv = b_ref[:, lanes]
            for rb in range(tm // CONV_RB):
                u = _conv_weighted(_conv_block_taps(x_ref, halo, rb, lanes, kw), wv, kw) + bv
                o_ref[rb * CONV_RB:(rb + 1) * CONV_RB, lanes] = _silu(u)
            return carry

        lax.fori_loop(0, tc // LANES, chunk, 0)

    return pl.pallas_call(
        body, name="ssm_conv_fwd", grid=(M // tm, C // tc),
        in_specs=[pl.BlockSpec((tm, tc), lambda i, j: (i, j)),
                  _prev_spec(tm, tc, lambda i, j: j, 0),
                  pl.BlockSpec((SUBLANES, tc), lambda i, j: (0, j)),
                  pl.BlockSpec((1, tc), lambda i, j: (0, j))],
        out_specs=pl.BlockSpec((tm, tc), lambda i, j: (i, j)),
        out_shape=jax.ShapeDtypeStruct((M, C), f32),
        compiler_params=_cparams(("parallel", "parallel")),
    )(xbc, xbc, w, b)


def _ssm_conv_bwd(xbc, w, b, dout):
    M, C = xbc.shape
    tm, tc, kw = ROW_TILE, C // 3, SSM_CONV
    nt = M // tm

    def body(x_ref, h_ref, w_ref, b_ref, d_ref, dx_ref, dw_ref, db_ref, nxt_ref):
        i = pl.program_id(1)

        @pl.when(i == 0)
        def _():
            dw_ref[...] = jnp.zeros_like(dw_ref)
            db_ref[...] = jnp.zeros_like(db_ref)
            nxt_ref[...] = jnp.zeros_like(nxt_ref)

        def chunk(j, carry):
            lanes = pl.ds(pl.multiple_of(j * LANES, LANES), LANES)
            halo = jnp.where(i == nt - 1, 0.0, h_ref[:, lanes])
            wv = w_ref[:, lanes]
            bv = b_ref[:, lanes]
            nxt = nxt_ref[:, lanes]
            db = jnp.zeros((CONV_RB, LANES), f32)
            dw = [jnp.zeros((CONV_RB, LANES), f32) for _ in range(kw)]
            for rb in reversed(range(tm // CONV_RB)):
                rows = slice(rb * CONV_RB, (rb + 1) * CONV_RB)
                taps = _conv_block_taps(x_ref, halo, rb, lanes, kw)
                du = d_ref[rows, lanes] * _dsilu(_conv_weighted(taps, wv, kw) + bv)
                db = db + du
                dw = [dw[k] + du * taps[k] for k in range(kw)]
                dx_ref[rows, lanes] = _conv_block_dx(du, nxt, wv, kw).astype(bf16)
                nxt = du[0:SUBLANES]
            nxt_ref[:, lanes] = nxt
            db_ref[:, lanes] += jnp.sum(db, axis=0, keepdims=True)
            for k in range(kw):
                dw_ref[k:k + 1, lanes] += jnp.sum(dw[k], axis=0, keepdims=True)
            return carry

        lax.fori_loop(0, tc // LANES, chunk, 0)

    tile = pl.BlockSpec((tm, tc), lambda j, i: (nt - 1 - i, j))
    return pl.pallas_call(
        body, name="ssm_conv_bwd", grid=(C // tc, nt),
        in_specs=[tile, _prev_spec(tm, tc, lambda j, i: j, 1, nt),
                  pl.BlockSpec((SUBLANES, tc), lambda j, i: (0, j)),
                  pl.BlockSpec((1, tc), lambda j, i: (0, j)), tile],
        out_specs=[tile, pl.BlockSpec((SUBLANES, tc), lambda j, i: (0, j)),
                   pl.BlockSpec((1, tc), lambda j, i: (0, j))],
        out_shape=[jax.ShapeDtypeStruct((M, C), bf16), jax.ShapeDtypeStruct((SUBLANES, C), f32),
                   jax.ShapeDtypeStruct((1, C), f32)],
        scratch_shapes=[pltpu.VMEM((SUBLANES, tc), f32)],
        compiler_params=_cparams(("parallel", "arbitrary")),
    )(xbc, xbc, w, b, dout)


def _ffn_gate_fwd(up, w, b):
    M = up.shape[0]
    tm, tc, kw = ROW_TILE, D_FF // 2, FFN_CONV
    nc = D_FF // tc

    def body(xg_ref, hg_ref, xv_ref, hv_ref, wg_ref, wv_ref, bg_ref, bv_ref, o_ref):
        i = pl.program_id(0)

        def chunk(j, carry):
            lanes = pl.ds(pl.multiple_of(j * LANES, LANES), LANES)
            halo_g = jnp.where(i == 0, 0.0, hg_ref[:, lanes])
            halo_v = jnp.where(i == 0, 0.0, hv_ref[:, lanes])
            wg, wv = wg_ref[:, lanes], wv_ref[:, lanes]
            bg, bv = bg_ref[:, lanes], bv_ref[:, lanes]
            for rb in range(tm // CONV_RB):
                ug = _conv_weighted(_conv_block_taps(xg_ref, halo_g, rb, lanes, kw), wg, kw) + bg
                uv = _conv_weighted(_conv_block_taps(xv_ref, halo_v, rb, lanes, kw), wv, kw) + bv
                o_ref[rb * CONV_RB:(rb + 1) * CONV_RB, lanes] = (_silu(ug) * uv).astype(bf16)
            return carry

        lax.fori_loop(0, tc // LANES, chunk, 0)

    return pl.pallas_call(
        body, name="ffn_gate_fwd", grid=(M // tm, nc),
        in_specs=[pl.BlockSpec((tm, tc), lambda i, j: (i, j)),
                  _prev_spec(tm, tc, lambda i, j: j, 0),
                  pl.BlockSpec((tm, tc), lambda i, j: (i, j + nc)),
                  _prev_spec(tm, tc, lambda i, j: j + nc, 0),
                  pl.BlockSpec((SUBLANES, tc), lambda i, j: (0, j)),
                  pl.BlockSpec((SUBLANES, tc), lambda i, j: (0, j + nc)),
                  pl.BlockSpec((1, tc), lambda i, j: (0, j)),
                  pl.BlockSpec((1, tc), lambda i, j: (0, j + nc))],
        out_specs=pl.BlockSpec((tm, tc), lambda i, j: (i, j)),
        out_shape=jax.ShapeDtypeStruct((M, D_FF), bf16),
        compiler_params=_cparams(("parallel", "parallel")),
    )(up, up, up, up, w, w, b, b)


def _ffn_gate_bwd(up, w, b, d_act):
    M = up.shape[0]
    tm, tc, kw = ROW_TILE, D_FF // 2, FFN_CONV
    nc = D_FF // tc
    nt = M // tm

    def body(xg_ref, hg_ref, xv_ref, hv_ref, wg_ref, wv_ref, bg_ref, bv_ref, d_ref,
             dxg_ref, dxv_ref, dwg_ref, dwv_ref, dbg_ref, dbv_ref, ng_ref, nv_ref):
        i = pl.program_id(1)

        @pl.when(i == 0)
        def _():
            for r in (dwg_ref, dwv_ref, dbg_ref, dbv_ref, ng_ref, nv_ref):
                r[...] = jnp.zeros_like(r)

        def chunk(j, carry):
            lanes = pl.ds(pl.multiple_of(j * LANES, LANES), LANES)
            halo_g = jnp.where(i == nt - 1, 0.0, hg_ref[:, lanes])
            halo_v = jnp.where(i == nt - 1, 0.0, hv_ref[:, lanes])
            wg, wv = wg_ref[:, lanes], wv_ref[:, lanes]
            bg, bv = bg_ref[:, lanes], bv_ref[:, lanes]
            nxt_g, nxt_v = ng_ref[:, lanes], nv_ref[:, lanes]
            zero = jnp.zeros((CONV_RB, LANES), f32)
            dbg, dbv = zero, zero
            dwg = [zero for _ in range(kw)]
            dwv = [zero for _ in range(kw)]
            for rb in reversed(range(tm // CONV_RB)):
                rows = slice(rb * CONV_RB, (rb + 1) * CONV_RB)
                tg = _conv_block_taps(xg_ref, halo_g, rb, lanes, kw)
                tv = _conv_block_taps(xv_ref, halo_v, rb, lanes, kw)
                ug = _conv_weighted(tg, wg, kw) + bg
                uv = _conv_weighted(tv, wv, kw) + bv
                sg = _sigmoid(ug)
                da = d_ref[rows, lanes]
                dug = da * uv * (sg * (1.0 + ug * (1.0 - sg)))
                duv = da * (ug * sg)
                dbg = dbg + dug
                dbv = dbv + duv
                dwg = [dwg[k] + dug * tg[k] for k in range(kw)]
                dwv = [dwv[k] + duv * tv[k] for k in range(kw)]
                dxg_ref[rows, lanes] = _conv_block_dx(dug, nxt_g, wg, kw).astype(bf16)
                dxv_ref[rows, lanes] = _conv_block_dx(duv, nxt_v, wv, kw).astype(bf16)
                nxt_g, nxt_v = dug[0:SUBLANES], duv[0:SUBLANES]
            ng_ref[:, lanes] = nxt_g
            nv_ref[:, lanes] = nxt_v
            dbg_ref[:, lanes] += jnp.sum(dbg, axis=0, keepdims=True)
            dbv_ref[:, lanes] += jnp.sum(dbv, axis=0, keepdims=True)
            for k in range(kw):
                dwg_ref[k:k + 1, lanes] += jnp.sum(dwg[k], axis=0, keepdims=True)
                dwv_ref[k:k + 1, lanes] += jnp.sum(dwv[k], axis=0, keepdims=True)
            return carry

        lax.fori_loop(0, tc // LANES, chunk, 0)

    tile_g = pl.BlockSpec((tm, tc), lambda j, i: (nt - 1 - i, j))
    tile_v = pl.BlockSpec((tm, tc), lambda j, i: (nt - 1 - i, j + nc))
    ext = pltpu.VMEM((SUBLANES, tc), f32)
    return pl.pallas_call(
        body, name="ffn_gate_bwd", grid=(nc, nt),
        in_specs=[tile_g, _prev_spec(tm, tc, lambda j, i: j, 1, nt),
                  tile_v, _prev_spec(tm, tc, lambda j, i: j + nc, 1, nt),
                  pl.BlockSpec((SUBLANES, tc), lambda j, i: (0, j)),
                  pl.BlockSpec((SUBLANES, tc), lambda j, i: (0, j + nc)),
                  pl.BlockSpec((1, tc), lambda j, i: (0, j)),
                  pl.BlockSpec((1, tc), lambda j, i: (0, j + nc)),
                  tile_g],
        out_specs=[tile_g, tile_g,
                   pl.BlockSpec((SUBLANES, tc), lambda j, i: (0, j)),
                   pl.BlockSpec((SUBLANES, tc), lambda j, i: (0, j)),
                   pl.BlockSpec((1, tc), lambda j, i: (0, j)),
                   pl.BlockSpec((1, tc), lambda j, i: (0, j))],
        out_shape=[jax.ShapeDtypeStruct((M, D_FF), bf16), jax.ShapeDtypeStruct((M, D_FF), bf16),
                   jax.ShapeDtypeStruct((SUBLANES, D_FF), f32), jax.ShapeDtypeStruct((SUBLANES, D_FF), f32),
                   jax.ShapeDtypeStruct((1, D_FF), f32), jax.ShapeDtypeStruct((1, D_FF), f32)],
        scratch_shapes=[ext, ext],
        compiler_params=_cparams(("parallel", "arbitrary")),
    )(up, up, up, up, w, w, b, b, d_act)


def _rope_apply(blk, cos, sin):
    lane = lax.broadcasted_iota(jnp.int32, blk.shape, 1)
    half = QK_ROPE // 2
    partner = jnp.where(lane < half, pltpu.roll(blk, LANES - half, 1), pltpu.roll(blk, half, 1))
    return blk * cos + partner * sin


def _rope_unapply(d, cos, sin):
    t = d * sin
    lane = lax.broadcasted_iota(jnp.int32, d.shape, 1)
    half = QK_ROPE // 2
    partner = jnp.where(lane < half, pltpu.roll(t, LANES - half, 1), pltpu.roll(t, half, 1))
    return d * cos + partner


def _up_q_rope(qn, wuq, cos, sin):
    M, K = qn.shape
    tm = _pick(M, (768, 512, 256))

    hs = HEADS_PER_STEP

    def body(a_ref, b_ref, c_ref, s_ref, o_ref):
        r = _dot(a_ref[...], b_ref[...]) * Q_PRESCALE
        c, s = c_ref[...], s_ref[...]
        for u in range(hs):
            o_ref[u, :, 0:QK_NOPE] = r[:, u * QK_PAD:u * QK_PAD + QK_NOPE].astype(bf16)
            o_ref[u, :, QK_NOPE:QK_PAD] = _rope_apply(r[:, u * QK_PAD + QK_NOPE:(u + 1) * QK_PAD], c, s).astype(bf16)

    return pl.pallas_call(
        body, name="up_q_rope", grid=(M // tm, MLA_HEADS // hs),
        in_specs=[pl.BlockSpec((tm, K), lambda i, h: (i, 0)),
                  pl.BlockSpec((K, hs * QK_PAD), lambda i, h: (0, h)),
                  pl.BlockSpec((tm, LANES), lambda i, h: (i, 0)),
                  pl.BlockSpec((tm, LANES), lambda i, h: (i, 0))],
        out_specs=pl.BlockSpec((hs, tm, QK_PAD), lambda i, h: (h, i, 0)),
        out_shape=jax.ShapeDtypeStruct((MLA_HEADS, M, QK_PAD), bf16),
        compiler_params=_cparams(("parallel", "parallel")),
    )(qn, wuq, cos, sin)


def _up_kv_rope(kvn, wukv, kpe_raw, cos, sin):
    M, K = kvn.shape
    tm = _pick(M, (768, 512, 256))

    hs = HEADS_PER_STEP
    w = QK_NOPE + V_DIM

    def body(a_ref, b_ref, pe_ref, c_ref, s_ref, k_ref, v_ref):
        r = _dot(a_ref[...], b_ref[...])
        pe = _rope_apply(pe_ref[...], c_ref[...], s_ref[...]).astype(bf16)
        for u in range(hs):
            k_ref[u, :, 0:QK_NOPE] = r[:, u * w:u * w + QK_NOPE].astype(bf16)
            k_ref[u, :, QK_NOPE:QK_PAD] = pe
            v_ref[u] = r[:, u * w + QK_NOPE:(u + 1) * w].astype(bf16)

    return pl.pallas_call(
        body, name="up_kv_rope", grid=(M // tm, MLA_HEADS // hs),
        in_specs=[pl.BlockSpec((tm, K), lambda i, h: (i, 0)),
                  pl.BlockSpec((K, hs * w), lambda i, h: (0, h)),
                  pl.BlockSpec((tm, LANES), lambda i, h: (i, 0)),
                  pl.BlockSpec((tm, LANES), lambda i, h: (i, 0)),
                  pl.BlockSpec((tm, LANES), lambda i, h: (i, 0))],
        out_specs=[pl.BlockSpec((hs, tm, QK_PAD), lambda i, h: (h, i, 0)),
                   pl.BlockSpec((hs, tm, V_DIM), lambda i, h: (h, i, 0))],
        out_shape=[jax.ShapeDtypeStruct((MLA_HEADS, M, QK_PAD), bf16),
                   jax.ShapeDtypeStruct((MLA_HEADS, M, V_DIM), bf16)],
        compiler_params=_cparams(("parallel", "parallel")),
    )(kvn, wukv, kpe_raw, cos, sin)


def _rope_q_bwd(dq, cos, sin):
    M = dq.shape[1]
    tm = _rt(M)

    def body(d_ref, c_ref, s_ref, o_ref):
        c, s = c_ref[...], s_ref[...]
        for h in range(MLA_HEADS):
            o_ref[:, h * QK_PAD:h * QK_PAD + QK_NOPE] = (d_ref[h, :, 0:QK_NOPE] * SOFTMAX_SCALE).astype(bf16)
            o_ref[:, h * QK_PAD + QK_NOPE:(h + 1) * QK_PAD] = (_rope_unapply(
                d_ref[h, :, QK_NOPE:QK_PAD], c, s) * SOFTMAX_SCALE).astype(bf16)

    return pl.pallas_call(
        body, name="rope_q_bwd", grid=(M // tm,),
        in_specs=[pl.BlockSpec((MLA_HEADS, tm, QK_PAD), lambda i: (0, i, 0)),
                  _row(LANES, tm), _row(LANES, tm)],
        out_specs=_row(MLA_HEADS * QK_PAD, tm),
        out_shape=jax.ShapeDtypeStruct((M, MLA_HEADS * QK_PAD), bf16),
        compiler_params=_cparams(("parallel",)),
    )(dq, cos, sin)


def _rope_k_bwd(dk, dv, cos, sin):
    M = dk.shape[1]
    tm = _rt(M)
    w = QK_NOPE + V_DIM

    def body(dk_ref, dv_ref, c_ref, s_ref, o_ref, pe_ref):
        pe = None
        for h in range(MLA_HEADS):
            o_ref[:, h * w:h * w + QK_NOPE] = dk_ref[h, :, 0:QK_NOPE].astype(bf16)
            o_ref[:, h * w + QK_NOPE:(h + 1) * w] = dv_ref[h].astype(bf16)
            t = dk_ref[h, :, QK_NOPE:QK_PAD]
            pe = t if pe is None else pe + t
        pe_ref[...] = _rope_unapply(pe, c_ref[...], s_ref[...])

    return pl.pallas_call(
        body, name="rope_k_bwd", grid=(M // tm,),
        in_specs=[pl.BlockSpec((MLA_HEADS, tm, QK_PAD), lambda i: (0, i, 0)),
                  pl.BlockSpec((MLA_HEADS, tm, V_DIM), lambda i: (0, i, 0)),
                  _row(LANES, tm), _row(LANES, tm)],
        out_specs=[_row(MLA_HEADS * w, tm), _row(LANES, tm)],
        out_shape=[jax.ShapeDtypeStruct((M, MLA_HEADS * w), bf16), jax.ShapeDtypeStruct((M, LANES), f32)],
        compiler_params=_cparams(("parallel",)),
    )(dk, dv, cos, sin)


def _attn_tile(M):
    return 768 if (M % 768 == 0 and M >= 4 * 768) else ROW_TILE


def _col_to_row(col):
    return col.T[0:1, :]


def _hosted_exchange(refs_in, refs_out, sems, scatter, first, last):
    copies = _exchange_copies(refs_in, refs_out, *sems, scatter)

    @pl.when(first)
    def _():
        for cp in copies:
            cp.start()

    @pl.when(last)
    def _():
        for cp in copies:
            cp.wait()


def _flash_fwd(q, k, v, carried, scatter):
    H, M, _ = q.shape
    T = _attn_tile(M)
    nq = M // T
    nx = len(carried)

    def body(*refs):
        q_ref, k_ref, v_ref = refs[:3]
        o_ref, lse_ref = refs[3 + nx:5 + nx]
        sa_ref, sb_ref, m_sc, l_sc, acc_sc = refs[5 + 2 * nx:10 + 2 * nx]
        h = pl.program_id(0)
        i = pl.program_id(1)
        _hosted_exchange(refs[3:3 + nx], refs[5 + nx:5 + 2 * nx], refs[10 + 2 * nx:], scatter,
                         (h == 0) & (i == 0), (h == H - 1) & (i == nq - 1))
        qv = q_ref[0]
        m_sc[...] = jnp.full_like(m_sc, NEG)
        l_sc[...] = jnp.zeros_like(l_sc)
        acc_sc[...] = jnp.zeros_like(acc_sc)

        def scores(j, s_ref):
            off = pl.multiple_of(j * T, T)
            s_ref[...] = _dot_nt(qv, k_ref[0, pl.ds(off, T), :])

        def softmax_pv(j, s_ref, masked):
            off = pl.multiple_of(j * T, T)
            s = s_ref[...]
            if masked:
                r = lax.broadcasted_iota(jnp.int32, (T, T), 0)
                c = lax.broadcasted_iota(jnp.int32, (T, T), 1)
                s = jnp.where(r >= c, s, NEG)
            m_prev = m_sc[...]
            m_new = jnp.maximum(m_prev, jnp.max(s, axis=1, keepdims=True))
            alpha = jnp.exp2(m_prev - m_new)
            p = jnp.exp2(s - m_new[:, 0:1])
            l_sc[...] = alpha * l_sc[...] + jnp.sum(p, axis=1, keepdims=True)
            acc_sc[...] = alpha * acc_sc[...] + _dot(p.astype(bf16), v_ref[0, pl.ds(off, T), :])
            m_sc[...] = m_new

        scores(0, sa_ref)

        def pair(jj, c):
            j0 = 2 * jj
            scores(j0 + 1, sb_ref)
            softmax_pv(j0, sa_ref, False)
            scores(j0 + 2, sa_ref)
            softmax_pv(j0 + 1, sb_ref, False)
            return c

        lax.fori_loop(0, i // 2, pair, 0)

        @pl.when(i % 2 == 0)
        def _():
            softmax_pv(i, sa_ref, True)

        @pl.when(i % 2 == 1)
        def _():
            scores(i, sb_ref)
            softmax_pv(i - 1, sa_ref, False)
            softmax_pv(i, sb_ref, True)

        l = l_sc[...]
        o_ref[...] = acc_sc[...] / l
        lse_ref[0, 0] = _col_to_row(m_sc[...] + jnp.log2(l))

    any_spec = pl.BlockSpec(memory_space=pl.ANY)
    return pl.pallas_call(
        body, name="flash_fwd", grid=(H, nq),
        in_specs=[pl.BlockSpec((1, T, QK_PAD), lambda h, i: (h, i, 0)),
                  pl.BlockSpec((1, M, QK_PAD), lambda h, i: (h, 0, 0)),
                  pl.BlockSpec((1, M, V_DIM), lambda h, i: (h, 0, 0))] + [any_spec] * nx,
        out_specs=[pl.BlockSpec((T, V_DIM), lambda h, i: (i, h)),
                   pl.BlockSpec((1, 1, 1, T), lambda h, i: (h, i, 0, 0))] + [any_spec] * nx,
        out_shape=[jax.ShapeDtypeStruct((M, H * V_DIM), f32),
                   jax.ShapeDtypeStruct((H, nq, 1, T), f32)] + _exchange_shapes(carried, scatter),
        scratch_shapes=[pltpu.VMEM((T, T), f32), pltpu.VMEM((T, T), f32),
                        pltpu.VMEM((T, LANES), f32), pltpu.VMEM((T, LANES), f32),
                        pltpu.VMEM((T, V_DIM), f32)] + _exchange_sems(nx),
        compiler_params=_cparams(("arbitrary", "arbitrary")),
    )(q, k, v, *carried)


def _attn_out_bwd(o, g, d_an):
    M, K = o.shape
    H = MLA_HEADS
    T = _attn_tile(M)

    def body(o_ref, g_ref, d_ref, dh_ref, dl_ref, dg_ref):
        @pl.when(pl.program_id(0) == 0)
        def _():
            dg_ref[...] = jnp.zeros_like(dg_ref)

        ov = o_ref[...]
        do, dgp = _rms_bwd_math(ov, g_ref[...], d_ref[...])
        dg_ref[...] += jnp.sum(dgp, axis=0, keepdims=True)
        for h in range(H):
            sl = slice(h * V_DIM, (h + 1) * V_DIM)
            doh = do[:, sl]
            dh_ref[h] = doh.astype(bf16)
            col = jnp.sum(ov[:, sl] * doh, axis=1, keepdims=True) + jnp.zeros((T, LANES), f32)
            dl_ref[h, 0] = _col_to_row(col)

    return pl.pallas_call(
        body, name="attn_out_bwd", grid=(M // T,),
        in_specs=[_row(K, T), _full((1, K)), _row(K, T)],
        out_specs=[pl.BlockSpec((H, T, V_DIM), lambda i: (0, i, 0)),
                   pl.BlockSpec((H, 1, 1, T), lambda i: (0, i, 0, 0)),
                   _full((1, K))],
        out_shape=[jax.ShapeDtypeStruct((H, M, V_DIM), bf16),
                   jax.ShapeDtypeStruct((H, M // T, 1, T), f32),
                   jax.ShapeDtypeStruct((1, K), f32)],
        compiler_params=_cparams(("arbitrary",)),
    )(o, g, d_an)


def _flash_bwd(q, k, v, do, lse, delta, carried, scatter):
    H, M, _ = q.shape
    T = _attn_tile(M)
    nq = M // T
    nx = len(carried)

    def body(*refs):
        q_ref, do_ref, lse_ref, dl_ref, k_ref, v_ref = refs[:6]
        dq_ref, dk_ref, dv_ref = refs[6 + nx:9 + nx]
        dk_sc, dv_sc = refs[9 + 2 * nx:11 + 2 * nx]
        j = pl.program_id(1)
        _hosted_exchange(refs[6:6 + nx], refs[9 + nx:9 + 2 * nx], refs[11 + 2 * nx:], scatter,
                         (pl.program_id(0) == 0) & (j == 0), (pl.program_id(0) == H - 1) & (j == nq - 1))

        @pl.when(j == 0)
        def _():
            dq_ref[...] = jnp.zeros_like(dq_ref)

        kt = k_ref[0]
        vt = v_ref[0]
        dk_sc[...] = jnp.zeros_like(dk_sc)
        dv_sc[...] = jnp.zeros_like(dv_sc)

        def step(i, masked):
            off = pl.multiple_of(i * T, T)
            qt = q_ref[0, pl.ds(off, T), :]
            dot_ = do_ref[0, pl.ds(off, T), :]
            st = _dot_nt(kt, qt)
            if masked:
                r = lax.broadcasted_iota(jnp.int32, (T, T), 0)
                c = lax.broadcasted_iota(jnp.int32, (T, T), 1)
                st = jnp.where(c >= r, st, NEG)
            pt = jnp.exp2(st - lse_ref[0, i])
            dv_sc[...] += _dot(pt.astype(bf16), dot_)
            dpt = _dot_nt(vt, dot_)
            dst = (pt * (dpt - dl_ref[0, i])).astype(bf16)
            dk_sc[...] += _dot(dst, qt)
            dq_ref[0, pl.ds(off, T), :] += _dot_tn(dst, kt)

        step(j, True)

        def loop_body(i, c):
            step(i, False)
            return c

        lax.fori_loop(j + 1, nq, loop_body, 0)
        dk_ref[0] = dk_sc[...] * LN2
        dv_ref[0] = dv_sc[...]

    any_spec = pl.BlockSpec(memory_space=pl.ANY)
    return pl.pallas_call(
        body, name="flash_bwd", grid=(H, nq),
        in_specs=[pl.BlockSpec((1, M, QK_PAD), lambda h, j: (h, 0, 0)),
                  pl.BlockSpec((1, M, V_DIM), lambda h, j: (h, 0, 0)),
                  pl.BlockSpec((1, nq, 1, T), lambda h, j: (h, 0, 0, 0)),
                  pl.BlockSpec((1, nq, 1, T), lambda h, j: (h, 0, 0, 0)),
                  pl.BlockSpec((1, T, QK_PAD), lambda h, j: (h, j, 0)),
                  pl.BlockSpec((1, T, V_DIM), lambda h, j: (h, j, 0))] + [any_spec] * nx,
        out_specs=[pl.BlockSpec((1, M, QK_PAD), lambda h, j: (h, 0, 0)),
                   pl.BlockSpec((1, T, QK_PAD), lambda h, j: (h, j, 0)),
                   pl.BlockSpec((1, T, V_DIM), lambda h, j: (h, j, 0))] + [any_spec] * nx,
        out_shape=[jax.ShapeDtypeStruct((H, M, QK_PAD), f32),
                   jax.ShapeDtypeStruct((H, M, QK_PAD), f32),
                   jax.ShapeDtypeStruct((H, M, V_DIM), f32)] + _exchange_shapes(carried, scatter),
        scratch_shapes=[pltpu.VMEM((T, QK_PAD), f32), pltpu.VMEM((T, V_DIM), f32)] + _exchange_sems(nx),
        compiler_params=_cparams(("arbitrary", "arbitrary")),
    )(q, do, lse, delta, k, v, *carried)


def _dt_fwd(dt_raw, bias, expand):
    M = dt_raw.shape[0]
    tm = _rt(M)

    def body(x_ref, b_ref, e_ref, o_ref, oe_ref):
        u = x_ref[...] + b_ref[...]
        sp = jnp.maximum(u, 0.0) + jnp.log(1.0 + jnp.exp(-jnp.abs(u)))
        lane = lax.broadcasted_iota(jnp.int32, u.shape, 1)
        dtp = jnp.where(lane < SSM_HEADS, sp, 0.0)
        o_ref[...] = dtp
        oe_ref[...] = _dot_hi(dtp, e_ref[...])

    return pl.pallas_call(
        body, name="dt_fwd", grid=(M // tm,),
        in_specs=[_row(LANES, tm), _full((1, LANES)), _full((LANES, D_SSM))],
        out_specs=[_row(LANES, tm), _row(D_SSM, tm)],
        out_shape=[jax.ShapeDtypeStruct((M, LANES), f32), jax.ShapeDtypeStruct((M, D_SSM), f32)],
        compiler_params=_cparams(("parallel",)),
    )(dt_raw, bias, expand)


def _dt_bwd(dt_raw, bias, ddt):
    M = dt_raw.shape[0]
    tm = _rt(M)

    def body(x_ref, b_ref, d_ref, o_ref, db_ref):
        @pl.when(pl.program_id(0) == 0)
        def _():
            db_ref[...] = jnp.zeros_like(db_ref)

        u = x_ref[...] + b_ref[...]
        lane = lax.broadcasted_iota(jnp.int32, u.shape, 1)
        g = jnp.where(lane < SSM_HEADS, d_ref[...] * _sigmoid(u), 0.0)
        o_ref[...] = g
        db_ref[...] += jnp.sum(g, axis=0, keepdims=True)

    return pl.pallas_call(
        body, name="dt_bwd", grid=(M // tm,),
        in_specs=[_row(LANES, tm), _full((1, LANES)), _row(LANES, tm)],
        out_specs=[_row(LANES, tm), _full((1, LANES))],
        out_shape=[jax.ShapeDtypeStruct((M, LANES), f32), jax.ShapeDtypeStruct((1, LANES), f32)],
        compiler_params=_cparams(("arbitrary",)),
    )(dt_raw, bias, ddt)


SSM_GW = SSM_HPG * SSM_P
SSM_PAIRS = SSM_GW // LANES


def _ssd_common(dte_ref, dtt_ref, ae_ref, acol_ref):
    Q = CHUNK
    r = lax.broadcasted_iota(jnp.int32, (Q, Q), 0)
    c = lax.broadcasted_iota(jnp.int32, (Q, Q), 1)
    causal = r >= c
    anti = c >= r
    tril = causal.astype(f32)
    triu = anti.astype(f32)
    dt_e = dte_ref[...]
    cs_e = _dot_hi(tril, dt_e * ae_ref[...])
    cst = _dot_hi(dtt_ref[...] * acol_ref[...], triu)
    cs_last = cs_e[Q - 1:Q, :]
    return causal, anti, triu, dt_e, cs_e, cst, jnp.exp(cs_e), jnp.exp(cs_last - cs_e), jnp.exp(cs_last)


def _half_masks():
    lane = lax.broadcasted_iota(jnp.int32, (CHUNK, LANES), 1)
    lo = lane < SSM_P
    return lo, jnp.logical_not(lo)


def _ssd_fwd(xbc_c, dt_e, dtt, a_e, a_col):
    M = xbc_c.shape[0]
    Q = CHUNK
    nch = M // Q

    def body(x_ref, dte_ref, dtt_ref, ae_ref, acol_ref, y_ref, hin_ref, ht_sc):
        @pl.when(pl.program_id(0) == 0)
        def _():
            ht_sc[...] = jnp.zeros_like(ht_sc)

        causal, _, _, dt_e, cs_e, cst, ecs_e, dte_e, elast_e = _ssd_common(dte_ref, dtt_ref, ae_ref, acol_ref)
        halves = _half_masks()
        for g in range(SSM_GROUPS):
            g0 = g * SSM_GW
            bg = x_ref[:, D_SSM + g * SSM_N:D_SSM + (g + 1) * SSM_N]
            cg = x_ref[:, D_SSM + D_BC + g * SSM_N:D_SSM + D_BC + (g + 1) * SSM_N]
            bg_b = bg.astype(bf16)
            cg_b = cg.astype(bf16)
            cb = _dot_nt(cg_b, bg_b)
            bgt_b = bg.T.astype(bf16)
            xdt_g = x_ref[:, g0:g0 + SSM_GW] * dt_e[:, g0:g0 + SSM_GW]
            ht = ht_sc[g]
            hin_ref[0, g] = ht
            y_off = _dot(cg_b, ht.astype(bf16)) * ecs_e[:, g0:g0 + SSM_GW]
            for pr in range(SSM_PAIRS):
                p0 = pr * LANES
                xdt_p = xdt_g[:, p0:p0 + LANES]
                acc = y_off[:, p0:p0 + LANES]
                for half in range(2):
                    h = g * SSM_HPG + pr * 2 + half
                    seg = cs_e[:, h * SSM_P:h * SSM_P + 1] - cst[h:h + 1, :]
                    lm = jnp.exp(jnp.where(causal, seg, -jnp.inf))
                    xm = jnp.where(halves[half], xdt_p, 0.0).astype(bf16)
                    acc = acc + _dot((cb * lm).astype(bf16), xm)
                y_ref[:, g0 + p0:g0 + p0 + LANES] = acc
            st = _dot(bgt_b, (xdt_g * dte_e[:, g0:g0 + SSM_GW]).astype(bf16))
            ht_sc[g] = ht * elast_e[:, g0:g0 + SSM_GW] + st

    return pl.pallas_call(
        body, name="ssd_fwd", grid=(nch,),
        in_specs=[pl.BlockSpec((Q, D_XBC), lambda c: (c, 0)),
                  pl.BlockSpec((Q, D_SSM), lambda c: (c, 0)),
                  pl.BlockSpec((SSM_HEADS, Q), lambda c: (0, c)),
                  _full((1, D_SSM)), _full((SSM_HEADS, LANES))],
        out_specs=[pl.BlockSpec((Q, D_SSM), lambda c: (c, 0)),
                   pl.BlockSpec((1, SSM_GROUPS, SSM_N, SSM_GW), lambda c: (c, 0, 0, 0))],
        out_shape=[jax.ShapeDtypeStruct((M, D_SSM), f32),
                   jax.ShapeDtypeStruct((nch, SSM_GROUPS, SSM_N, SSM_GW), f32)],
        scratch_shapes=[pltpu.VMEM((SSM_GROUPS, SSM_N, SSM_GW), f32)],
        compiler_params=_cparams(("arbitrary",)),
    )(xbc_c, dt_e, dtt, a_e, a_col)


def _ssd_bwd(xbc_c, dtp, dt_e, dtt, a_row, a_e, a_col, hin, dy, d_exp, head_ind):
    M = xbc_c.shape[0]
    Q = CHUNK
    nch = M // Q
    rev = lambda c: nch - 1 - c

    def body(x_ref, dtp_ref, dte_ref, dtt_ref, arow_ref, ae_ref, acol_ref, hin_ref, dy_ref, dexp_ref,
             ind_ref, dx_ref, ddt_ref, da_ref, dht_sc, z_sc, z1_sc, last_sc, ct_sc):
        @pl.when(pl.program_id(0) == 0)
        def _():
            dht_sc[...] = jnp.zeros_like(dht_sc)
            da_ref[...] = jnp.zeros_like(da_ref)
            last_sc[...] = jnp.zeros_like(last_sc)
            ct_sc[...] = jnp.zeros_like(ct_sc)

        causal, anti, triu, dt_e, cs_e, cst, ecs_e, dte_e, elast_e = _ssd_common(dte_ref, dtt_ref, ae_ref, acol_ref)
        halves = _half_masks()
        lane = lax.broadcasted_iota(jnp.int32, (Q, LANES), 1)
        rsum = jnp.zeros((Q, LANES), f32)
        for g in range(SSM_GROUPS):
            g0 = g * SSM_GW
            gs = slice(g0, g0 + SSM_GW)
            b0 = D_SSM + g * SSM_N
            c0 = D_SSM + D_BC + g * SSM_N
            bg = x_ref[:, b0:b0 + SSM_N]
            cg = x_ref[:, c0:c0 + SSM_N]
            bg_b = bg.astype(bf16)
            cg_b = cg.astype(bf16)
            cgt_b = cg.T.astype(bf16)
            cbt = _dot_nt(bg_b, cg_b)
            cb = _dot_nt(cg_b, bg_b)
            x_g = x_ref[:, gs]
            dt_g = dt_e[:, gs]
            xdt_g = x_g * dt_g
            dy_g = dy_ref[:, gs]
            ht = hin_ref[0, g]
            ht_b = ht.astype(bf16)
            dht = dht_sc[g]
            dht_b = dht.astype(bf16)
            dye_b = (dy_g * ecs_e[:, gs]).astype(bf16)
            dc = _dot_nt(dye_b, ht_b)
            dht_new = dht * elast_e[:, gs] + _dot(cgt_b, dye_b)
            e = _dot(bg_b, dht_b)
            xdtd = xdt_g * dte_e[:, gs]
            db = _dot_nt(xdtd.astype(bf16), dht_b)
            dxdt_state = e * dte_e[:, gs]
            exd = e * xdtd
            z1_sc[:, gs] = dy_g * (_dot(cg_b, ht_b) * ecs_e[:, gs]) - exd
            last_sc[0:1, gs] = (jnp.sum(exd, axis=0, keepdims=True)
                                + jnp.sum(dht * ht, axis=0, keepdims=True) * elast_e[:, gs])
            dg_acc = jnp.zeros((Q, Q), f32)
            for pr in range(SSM_PAIRS):
                p0 = pr * LANES
                ps = slice(g0 + p0, g0 + p0 + LANES)
                dy_p = dy_g[:, p0:p0 + LANES]
                xdt_pb = xdt_g[:, p0:p0 + LANES].astype(bf16)
                acc = dxdt_state[:, p0:p0 + LANES]
                for half in range(2):
                    h = g * SSM_HPG + pr * 2 + half
                    seg = cs_e[:, h * SSM_P:h * SSM_P + 1] - cst[h:h + 1, :]
                    lm = jnp.exp(jnp.where(causal, seg, -jnp.inf))
                    lmt = jnp.exp(jnp.where(anti, -seg, -jnp.inf))
                    dym = jnp.where(halves[half], dy_p, 0.0).astype(bf16)
                    acc = acc + _dot((cbt * lmt).astype(bf16), dym)
                    dml = _dot_nt(dym, xdt_pb) * lm
                    dg_acc = dg_acc + dml
                    w = dml * cb
                    rsum = rsum + jnp.where(lane == h, jnp.sum(w, axis=1, keepdims=True), 0.0)
                    ct_sc[h:h + 1, :] = jnp.sum(w, axis=0, keepdims=True)
                dx_ref[:, ps] = acc * dt_g[:, p0:p0 + LANES] + dexp_ref[:, ps] * dy_p
                z_sc[:, ps] = acc * x_g[:, p0:p0 + LANES]
            dg_b = dg_acc.astype(bf16)
            dx_ref[:, c0:c0 + SSM_N] = dc + _dot(dg_b, bg_b)
            dx_ref[:, b0:b0 + SSM_N] = db + _dot_tn(dg_b, cg_b)
            dht_sc[g] = dht_new
        s1 = _dot_hi(z1_sc[...], ind_ref[...])
        s2 = _dot_hi(z_sc[...], ind_ref[...])
        last = _dot_hi(last_sc[...], ind_ref[...])[0:1, :]
        dtp = dtp_ref[...]
        row = lax.broadcasted_iota(jnp.int32, (Q, LANES), 0)
        dcs = s1 + rsum + jnp.where(row == Q - 1, last, 0.0)
        tril = causal.astype(f32)
        da = _dot_hi(triu, dcs) - _dot_hi(ct_sc[...], tril).T
        ddt_ref[...] = s2 + da * arow_ref[...]
        da_ref[...] += jnp.sum(da * dtp, axis=0, keepdims=True)

    return pl.pallas_call(
        body, name="ssd_bwd", grid=(nch,),
        in_specs=[pl.BlockSpec((Q, D_XBC), lambda c: (rev(c), 0)),
                  pl.BlockSpec((Q, LANES), lambda c: (rev(c), 0)),
                  pl.BlockSpec((Q, D_SSM), lambda c: (rev(c), 0)),
                  pl.BlockSpec((SSM_HEADS, Q), lambda c: (0, rev(c))),
                  _full((1, LANES)), _full((1, D_SSM)), _full((SSM_HEADS, LANES)),
                  pl.BlockSpec((1, SSM_GROUPS, SSM_N, SSM_GW), lambda c: (rev(c), 0, 0, 0)),
                  pl.BlockSpec((Q, D_SSM), lambda c: (rev(c), 0)),
                  _full((1, D_SSM)), _full((D_SSM, LANES))],
        out_specs=[pl.BlockSpec((Q, D_XBC), lambda c: (rev(c), 0)),
                   pl.BlockSpec((Q, LANES), lambda c: (rev(c), 0)),
                   _full((1, LANES))],
        out_shape=[jax.ShapeDtypeStruct((M, D_XBC), f32), jax.ShapeDtypeStruct((M, LANES), f32),
                   jax.ShapeDtypeStruct((1, LANES), f32)],
        scratch_shapes=[pltpu.VMEM((SSM_GROUPS, SSM_N, SSM_GW), f32), pltpu.VMEM((Q, D_SSM), f32),
                        pltpu.VMEM((Q, D_SSM), f32), pltpu.VMEM((SUBLANES, D_SSM), f32),
                        pltpu.VMEM((LANES, Q), f32)],
        compiler_params=_cparams(("arbitrary",)),
    )(xbc_c, dtp, dt_e, dtt, a_row, a_e, a_col, hin, dy, d_exp, head_ind)


def _gate_norm_fwd(y, xbc_c, z, d_exp, g):
    M = y.shape[0]
    tm = _rt(M)
    gw = D_SSM // SSM_GROUPS

    def body(y_ref, x_ref, z_ref, d_ref, g_ref, o_ref):
        yg = (y_ref[...] + d_ref[...] * x_ref[...]) * _silu(z_ref[...])
        for gi in range(SSM_GROUPS):
            blk = yg[:, gi * gw:(gi + 1) * gw]
            o_ref[:, gi * gw:(gi + 1) * gw] = (blk * _rstd(blk) * g_ref[:, gi * gw:(gi + 1) * gw]).astype(bf16)

    return pl.pallas_call(
        body, name="gate_norm_fwd", grid=(M // tm,),
        in_specs=[_row(D_SSM, tm), _row(D_SSM, tm), _row(D_SSM, tm), _full((1, D_SSM)), _full((1, D_SSM))],
        out_specs=_row(D_SSM, tm), out_shape=jax.ShapeDtypeStruct((M, D_SSM), bf16),
        compiler_params=_cparams(("parallel",)),
    )(y, xbc_c, z, d_exp, g)


def _gate_norm_bwd(y, xbc_c, z, d_exp, g, dout, head_ind):
    M = y.shape[0]
    tm = _rt(M)
    nt = M // tm
    gw = D_SSM // SSM_GROUPS

    def body(y_ref, x_ref, z_ref, d_ref, g_ref, do_ref, ind_ref, dy_ref, dz_ref, dg_ref, dd_ref, ddc_sc):
        i = pl.program_id(0)

        @pl.when(i == 0)
        def _():
            dg_ref[...] = jnp.zeros_like(dg_ref)
            ddc_sc[...] = jnp.zeros_like(ddc_sc)

        zv = z_ref[...]
        xv = x_ref[...]
        s = _silu(zv)
        yd = y_ref[...] + d_ref[...] * xv
        yg = yd * s
        dov = do_ref[...]
        for gi in range(SSM_GROUPS):
            sl = slice(gi * gw, (gi + 1) * gw)
            dyg, dgp = _rms_bwd_math(yg[:, sl], g_ref[:, sl], dov[:, sl])
            dg_ref[:, sl] += jnp.sum(dgp, axis=0, keepdims=True)
            dyd = dyg * s[:, sl]
            dy_ref[:, sl] = dyd
            dz_ref[:, sl] = (dyg * yd[:, sl] * _dsilu(zv[:, sl])).astype(bf16)
            ddc_sc[:, sl] += jnp.sum(dyd * xv[:, sl], axis=0, keepdims=True)

        @pl.when(i == nt - 1)
        def _():
            dd_ref[...] = _dot_hi(ddc_sc[...], ind_ref[...])

    return pl.pallas_call(
        body, name="gate_norm_bwd", grid=(nt,),
        in_specs=[_row(D_SSM, tm), _row(D_SSM, tm), _row(D_SSM, tm), _full((1, D_SSM)), _full((1, D_SSM)),
                  _row(D_SSM, tm), _full((D_SSM, LANES))],
        out_specs=[_row(D_SSM, tm), _row(D_SSM, tm), _full((1, D_SSM)), _full((1, LANES))],
        out_shape=[jax.ShapeDtypeStruct((M, D_SSM), f32), jax.ShapeDtypeStruct((M, D_SSM), bf16),
                   jax.ShapeDtypeStruct((1, D_SSM), f32), jax.ShapeDtypeStruct((1, LANES), f32)],
        scratch_shapes=[pltpu.VMEM((1, D_SSM), f32)],
        compiler_params=_cparams(("arbitrary",)),
    )(y, xbc_c, z, d_exp, g, dout, head_ind)


_PEER_FLIPS = [(0, 0, 1), (0, 1, 0), (0, 1, 1), (1, 0, 0), (1, 0, 1), (1, 1, 0), (1, 1, 1)]


def _exchange_copies(ins, outs, send_sems, recv_sems, loc_sems, scatter):
    n = len(ins)
    x, y, c = lax.axis_index("x"), lax.axis_index("y"), lax.axis_index("c")
    me = 4 * x + 2 * y + c
    copies = []
    for a in range(n):
        src = ins[a].at[me] if scatter else ins[a]
        copies.append(pltpu.make_async_copy(src, outs[a].at[me], loc_sems.at[a]))
    for p, (fx, fy, fc) in enumerate(_PEER_FLIPS):
        tx = 1 - x if fx else x
        ty = 1 - y if fy else y
        tc = 1 - c if fc else c
        tgt = 4 * tx + 2 * ty + tc
        for a in range(n):
            src = ins[a].at[tgt] if scatter else ins[a]
            copies.append(pltpu.make_async_remote_copy(
                src_ref=src, dst_ref=outs[a].at[me],
                send_sem=send_sems.at[p * n + a], recv_sem=recv_sems.at[p * n + a],
                device_id=(tx, ty, tc), device_id_type=_MESH))
    return copies


def _exchange_shapes(arrays, scatter):
    return [jax.ShapeDtypeStruct(a.shape if scatter else (N_DEV,) + a.shape, a.dtype) for a in arrays]


def _exchange_sems(n):
    return [pltpu.SemaphoreType.DMA((7 * n,)), pltpu.SemaphoreType.DMA((7 * n,)), pltpu.SemaphoreType.DMA((n,))]


def _exchange(arrays, scatter, name):
    n = len(arrays)

    def body(*refs):
        copies = _exchange_copies(refs[:n], refs[n:2 * n], *refs[2 * n:], scatter)
        for cp in copies:
            cp.start()
        for cp in copies:
            cp.wait()

    any_spec = pl.BlockSpec(memory_space=pl.ANY)
    return pl.pallas_call(
        body, name=name, in_specs=[any_spec] * n, out_specs=[any_spec] * n,
        out_shape=_exchange_shapes(arrays, scatter), scratch_shapes=_exchange_sems(n),
    )(*arrays)


def _gather_two_level(arrays, name):
    n = len(arrays)

    def body(*refs):
        ins, outs = refs[:n], refs[n:2 * n]
        send_sems, recv_sems, loc_sems = refs[2 * n:]
        x, y, c = lax.axis_index("x"), lax.axis_index("y"), lax.axis_index("c")
        me, sibling = (x, y, c), (x, y, 1 - c)
        chips = [(1 - x, y), (x, 1 - y), (1 - x, 1 - y)]

        def slot(a, dev):
            return outs[a].at[4 * dev[0] + 2 * dev[1] + dev[2]]

        def copy(a, k, block, to, src=None):
            return pltpu.make_async_remote_copy(
                src_ref=slot(a, block) if src is None else src, dst_ref=slot(a, block),
                send_sem=send_sems.at[7 * a + k], recv_sem=recv_sems.at[7 * a + k],
                device_id=to, device_id_type=_MESH)

        mine = [pltpu.make_async_copy(ins[a], slot(a, me), loc_sems.at[a]) for a in range(n)]
        first = []
        for a in range(n):
            first.append(copy(a, 0, me, sibling, src=ins[a]))
            first += [copy(a, 1 + j, me, (*chip, c), src=ins[a]) for j, chip in enumerate(chips)]
        for cp in mine + first:
            cp.start()
        passed = []
        for j, chip in enumerate(chips):
            for a in range(n):
                copy(a, 1 + j, (*chip, c), me).wait_recv()
                cp = copy(a, 4 + j, (*chip, c), sibling)
                cp.start()
                passed.append(cp)
        for a in range(n):
            copy(a, 0, sibling, me).wait_recv()
            for j, chip in enumerate(chips):
                copy(a, 4 + j, (*chip, 1 - c), me).wait_recv()
        for cp in first + passed:
            cp.wait_send()
        for cp in mine:
            cp.wait()

    any_spec = pl.BlockSpec(memory_space=pl.ANY)
    return pl.pallas_call(
        body, name=name, in_specs=[any_spec] * n, out_specs=[any_spec] * n,
        out_shape=_exchange_shapes(arrays, False), scratch_shapes=_exchange_sems(n),
    )(*arrays)


def _exchange_tail(scattered, gathered, name):
    ns, ng = len(scattered), len(gathered)
    n = ns + ng

    def body(*refs):
        sems = refs[2 * n:]
        copies = (_exchange_copies(refs[:ns], refs[n:n + ns], *sems[:3], True)
                  + _exchange_copies(refs[ns:n], refs[n + ns:2 * n], *sems[3:], False))
        for cp in copies:
            cp.start()
        for cp in copies:
            cp.wait()

    any_spec = pl.BlockSpec(memory_space=pl.ANY)
    return pl.pallas_call(
        body, name=name, in_specs=[any_spec] * n, out_specs=[any_spec] * n,
        out_shape=_exchange_shapes(scattered, True) + _exchange_shapes(gathered, False),
        scratch_shapes=_exchange_sems(ns) + _exchange_sems(ng),
    )(*scattered, *gathered)


def _adamw_math(g, w, m, v):
    c1 = 1.0 - ADAM_B1 ** ADAM_STEP
    c2 = 1.0 - ADAM_B2 ** ADAM_STEP
    mn = ADAM_B1 * m + (1.0 - ADAM_B1) * g
    vn = ADAM_B2 * v + (1.0 - ADAM_B2) * (g * g)
    m_hat = mn / c1
    v_hat = vn / c2
    return -ADAM_LR * (m_hat / (jnp.sqrt(v_hat) + ADAM_EPS) + ADAM_WD * w), mn, vn


def _adamw(parts, w, m, v, name):
    R, C = w.shape
    tr = _pick(R, (PACK_ROW_TILE, 64, 32, 16, 8))

    def body(p_ref, w_ref, m_ref, v_ref, g_ref, d_ref, nm_ref, nv_ref):
        g = p_ref[0].astype(f32)
        for s in range(1, N_DEV):
            g = g + p_ref[s].astype(f32)
        g_ref[...] = g
        d_ref[...], nm_ref[...], nv_ref[...] = _adamw_math(g, w_ref[...], m_ref[...], v_ref[...])

    spec = pl.BlockSpec((tr, C), lambda i: (i, 0))
    return pl.pallas_call(
        body, name=name, grid=(R // tr,),
        in_specs=[pl.BlockSpec((N_DEV, tr, C), lambda i: (0, i, 0)), spec, spec, spec],
        out_specs=[spec] * 4, out_shape=[jax.ShapeDtypeStruct((R, C), f32)] * 4,
        compiler_params=_cparams(("parallel",)),
    )(parts, w, m, v)


def _adamw_replicated(parts, ws, ms, vs):
    n = len(ws)
    R = parts.shape[1]
    sizes = [int(w.shape[1]) for w in ws]

    def body(*refs):
        p_ref = refs[0]
        w_refs, m_refs, v_refs = refs[1:1 + n], refs[1 + n:1 + 2 * n], refs[1 + 2 * n:1 + 3 * n]
        loss_ref = refs[1 + 3 * n]
        outs = refs[2 + 3 * n:]
        g_all = p_ref[0]
        for s in range(1, N_DEV):
            g_all = g_all + p_ref[s]
        row = 0
        for p in range(n):
            pieces, left = [], sizes[p]
            while left > 0:
                take = min(left, PACK_W)
                pieces.append(g_all[row:row + 1, 0:take])
                left -= take
                row += 1
            g = pieces[0] if len(pieces) == 1 else jnp.concatenate(pieces, axis=1)
            d, mn, vn = _adamw_math(g, w_refs[p][...], m_refs[p][...], v_refs[p][...])
            outs[4 * p][...] = g
            outs[4 * p + 1][...] = d
            outs[4 * p + 2][...] = mn
            outs[4 * p + 3][...] = vn
        loss_ref[...] = g_all[row:row + 1, 0:LANES]

    in_specs = [_full((N_DEV, R, PACK_W))] + [_full((1, s)) for s in sizes] * 3
    out_specs = [_full((1, LANES))]
    out_shape = [jax.ShapeDtypeStruct((1, LANES), f32)]
    for s in sizes:
        out_specs += [_full((1, s))] * 4
        out_shape += [jax.ShapeDtypeStruct((1, s), f32)] * 4
    res = pl.pallas_call(
        body, name="adamw_replicated", in_specs=in_specs, out_specs=out_specs, out_shape=out_shape,
        compiler_params=pltpu.CompilerParams(vmem_limit_bytes=VMEM_LIMIT),
    )(parts, *ws, *ms, *vs)
    return res[0], [res[1 + 4 * p:5 + 4 * p] for p in range(n)]


def _flat_rows(a, lead_ndim):
    lead = a.shape[:lead_ndim]
    n = int(np.prod(a.shape[lead_ndim:]))
    a = a.reshape(lead + (n,))
    pad = (-n) % PACK_W
    if pad:
        a = jnp.pad(a, [(0, 0)] * lead_ndim + [(0, pad)])
    return a.reshape(lead + ((n + pad) // PACK_W, PACK_W))


def _pack(arrays, lead_ndim, total_rows, dtype):
    rows = [_flat_rows(a.astype(dtype), lead_ndim) for a in arrays]
    cat = jnp.concatenate(rows, axis=lead_ndim)
    pad = total_rows - cat.shape[lead_ndim]
    if pad:
        cat = jnp.pad(cat, [(0, 0)] * lead_ndim + [(0, pad), (0, 0)])
    return cat


def _unpack(buf, shapes, lead_ndim):
    out = []
    r = 0
    lead = buf.shape[:lead_ndim]
    for shp in shapes:
        n = int(np.prod(shp))
        nr = -(-n // PACK_W)
        piece = lax.slice_in_dim(buf, r, r + nr, axis=lead_ndim)
        piece = piece.reshape(lead + (nr * PACK_W,))
        piece = lax.slice_in_dim(piece, 0, n, axis=lead_ndim)
        out.append(piece.reshape(lead + tuple(shp)))
        r += nr
    return out


def _round_up(n, m):
    return -(-n // m) * m


def kernel(x, meta_tokens, norm_mix_pre, norm_mix_post, norm_ffn_pre, norm_ffn_post, w_in, q_a_norm, w_uq, kv_a_norm, w_ukv, attn_out_norm, ssm_conv_w, ssm_conv_b, ssm_dt_bias, ssm_A_log, ssm_D, ssm_norm, w_out, w_up, ffn_conv_w, ffn_conv_b, w_down, loss_target, m_meta_tokens, m_norm_mix_pre, m_norm_mix_post, m_norm_ffn_pre, m_norm_ffn_post, m_w_in, m_q_a_norm, m_w_uq, m_kv_a_norm, m_w_ukv, m_attn_out_norm, m_ssm_conv_w, m_ssm_conv_b, m_ssm_dt_bias, m_ssm_A_log, m_ssm_D, m_ssm_norm, m_w_out, m_w_up, m_ffn_conv_w, m_ffn_conv_b, m_w_down, v_meta_tokens, v_norm_mix_pre, v_norm_mix_post, v_norm_ffn_pre, v_norm_ffn_post, v_w_in, v_q_a_norm, v_w_uq, v_kv_a_norm, v_w_ukv, v_attn_out_norm, v_ssm_conv_w, v_ssm_conv_b, v_ssm_dt_bias, v_ssm_A_log, v_ssm_D, v_ssm_norm, v_w_out, v_w_up, v_ffn_conv_w, v_ffn_conv_b, v_w_down):
    seq = x.shape[1]
    n_real = N_META + seq
    Lp = _round_up(n_real, 768) if n_real > 2048 else _round_up(n_real, ROW_TILE)
    D = D_MODEL

    early_w = [w_uq, w_ukv]
    late_w = [w_out, w_down]
    sharded_s = [meta_tokens, ssm_conv_w, ffn_conv_w]
    grp_a = dict(names=["w_out", "w_down"], w=late_w, m=[m_w_out, m_w_down],
                 v=[v_w_out, v_w_down])
    grp_b = dict(names=["w_uq", "w_ukv", "ssm_conv_w", "ffn_conv_w"],
                 w=early_w + [ssm_conv_w, ffn_conv_w],
                 m=[m_w_uq, m_w_ukv, m_ssm_conv_w, m_ffn_conv_w],
                 v=[v_w_uq, v_w_ukv, v_ssm_conv_w, v_ffn_conv_w])
    grp_meta = dict(names=["meta_tokens"], w=[meta_tokens], m=[m_meta_tokens], v=[v_meta_tokens])
    repl_w = [norm_mix_pre, norm_mix_post, norm_ffn_pre, norm_ffn_post, q_a_norm, kv_a_norm, attn_out_norm,
              ssm_conv_b, ssm_dt_bias, ssm_A_log, ssm_D, ssm_norm, ffn_conv_b]
    repl_m = [m_norm_mix_pre, m_norm_mix_post, m_norm_ffn_pre, m_norm_ffn_post, m_q_a_norm, m_kv_a_norm,
              m_attn_out_norm, m_ssm_conv_b, m_ssm_dt_bias, m_ssm_A_log, m_ssm_D, m_ssm_norm, m_ffn_conv_b]
    repl_v = [v_norm_mix_pre, v_norm_mix_post, v_norm_ffn_pre, v_norm_ffn_post, v_q_a_norm, v_kv_a_norm,
              v_attn_out_norm, v_ssm_conv_b, v_ssm_dt_bias, v_ssm_A_log, v_ssm_D, v_ssm_norm, v_ffn_conv_b]

    def pack_rows(arrs, lead):
        return _round_up(sum(-(-int(np.prod(a.shape[lead:])) // PACK_W) for a in arrs), 16)

    wb = _pack(early_w, 0, pack_rows(early_w, 0), bf16)
    wl = _pack(late_w, 0, pack_rows(late_w, 0), bf16)
    ws = _pack(sharded_s, 0, pack_rows(sharded_s, 0), f32)
    wb_all, ws_all, win_all = _gather_two_level([wb, ws, w_in[0].astype(bf16)], "gather_weights")
    g_w_uq, g_w_ukv = _unpack(wb_all, [a.shape for a in early_w], 1)
    g_meta, g_sconv, g_fconv = _unpack(ws_all, [a.shape for a in sharded_s], 1)

    def cols(gathered):
        t = gathered[:, 0]
        return jnp.transpose(t, (1, 0, 2)).reshape(t.shape[1], N_DEV * t.shape[2])

    win = cols(win_all[:, None])
    o = np.cumsum((0, Q_RANK, KV_RANK, QK_ROPE, D_SSM, D_XBC, SSM_HEADS))
    w_q, w_kv = win[:, o[0]:o[1]], win[:, o[1]:o[2]]
    w_rope = jnp.pad(win[:, o[2]:o[3]], ((0, 0), (0, LANES - QK_ROPE)))
    w_z, w_xbc = win[:, o[3]:o[4]], win[:, o[4]:o[5]]
    w_dt = jnp.pad(win[:, o[5]:o[6]], ((0, 0), (0, LANES - SSM_HEADS)))
    wuq = g_w_uq.reshape(Q_RANK, MLA_HEADS, QK_NOPE + QK_ROPE)
    wuq = jnp.pad(wuq, ((0, 0), (0, 0), (0, QK_PAD - QK_NOPE - QK_ROPE))).reshape(Q_RANK, MLA_HEADS * QK_PAD)
    wukv = g_w_ukv.reshape(KV_RANK, MLA_HEADS * (QK_NOPE + V_DIM))
    meta_full = jnp.transpose(g_meta, (1, 0, 2)).reshape(N_META, D)
    sconv_w = jnp.pad(cols(g_sconv), ((0, SUBLANES - SSM_CONV), (0, 0)))
    fconv_w = jnp.pad(cols(g_fconv), ((0, SUBLANES - FFN_CONV), (0, 0)))

    pos = jnp.arange(Lp, dtype=f32)
    inv = ROPE_THETA ** (-jnp.arange(0, QK_ROPE, 2, dtype=f32) / QK_ROPE)
    ang = pos[:, None] * inv[None, :]
    cs_, sn_ = jnp.cos(ang), jnp.sin(ang)
    zpad = jnp.zeros((Lp, LANES - QK_ROPE), f32)
    cos_t = jnp.concatenate([cs_, cs_, zpad], axis=1)
    sin_t = jnp.concatenate([-sn_, sn_, zpad], axis=1)
    dt_bias_p = jnp.pad(ssm_dt_bias, ((0, 0), (0, LANES - SSM_HEADS)))
    a_neg = -jnp.exp(ssm_A_log)
    a_row = jnp.pad(a_neg, ((0, 0), (0, LANES - SSM_HEADS)))
    a_col = jnp.broadcast_to(a_neg.reshape(SSM_HEADS, 1), (SSM_HEADS, LANES))
    d_exp = jnp.repeat(ssm_D, SSM_P, axis=1)
    a_e = jnp.repeat(a_neg, SSM_P, axis=1)
    head_ind = (jnp.arange(D_SSM)[:, None] // SSM_P == jnp.arange(LANES)[None, :]).astype(f32)

    xb = x[0]
    h0 = jnp.concatenate([meta_full, xb, jnp.zeros((Lp - n_real, D), f32)], axis=0)
    tgt = jnp.pad(loss_target[0], ((N_META, Lp - n_real), (0, 0)))
    hn1, q_c, kv_c, kpe_raw, z, xbc, dt_raw = _in_proj(h0, norm_mix_pre, [w_q, w_kv, w_rope, w_z, w_xbc, w_dt])

    qn = _rms_fwd(q_c, q_a_norm, bf16, "norm_q")
    kvn = _rms_fwd(kv_c, kv_a_norm, bf16, "norm_kv")
    qh = _up_q_rope(qn, wuq, cos_t, sin_t)
    kh, vh = _up_kv_rope(kvn, wukv, kpe_raw, cos_t, sin_t)
    attn, lse, wl_all, wup_all = _flash_fwd(qh, kh, vh, [wl, w_up[0].astype(bf16)], False)
    g_w_out, g_w_down = _unpack(wl_all, [a.shape for a in late_w], 1)
    wout = g_w_out.reshape(D_ATTN + D_SSM, D)
    wout_a, wout_s = wout[:D_ATTN], wout[D_ATTN:]
    wup = cols(wup_all[:, None])
    wdown = g_w_down.reshape(D_FF, D)
    an = _rms_fwd(attn, attn_out_norm, bf16, "norm_attn_out")

    xbc_c = _ssm_conv_fwd(xbc, sconv_w, ssm_conv_b)
    dtp, dt_e = _dt_fwd(dt_raw, dt_bias_p, jnp.transpose(head_ind))
    dtt = jnp.transpose(dtp[:, :SSM_HEADS])
    y_ssd, hin = _ssd_fwd(xbc_c, dt_e, dtt, a_e, a_col)
    ssm = _gate_norm_fwd(y_ssd, xbc_c, z, d_exp, ssm_norm)

    mix = _mm([(an, wout_a), (ssm, wout_s)], f32, False, "out_proj")
    h1, hn2 = _resid_norm(h0, mix, norm_mix_post, norm_ffn_pre)
    up = _mm([(hn2, wup)], f32, False, "ffn_up")
    act = _ffn_gate_fwd(up, fconv_w, ffn_conv_b)
    down = _mm([(act, wdown)], f32, False, "ffn_down")
    dh2, d_down, dg_ffn_post, loss_part = _final(h1, down, norm_ffn_post, tgt, n_real)

    d_act = _mm([(d_down, wdown)], f32, True, "ffn_down_dx")
    dw_down = _mm_tn(act, d_down, "ffn_down_dw")
    dup_g, dup_v, dwc_g, dwc_v, dbc_g, dbc_v = _ffn_gate_bwd(up, fconv_w, ffn_conv_b, d_act)
    d_hn2 = _mm([(dup_g, wup[:, :D_FF]), (dup_v, wup[:, D_FF:])], f32, True, "ffn_up_dx")
    dw_up = jnp.concatenate([_mm_tn(hn2, dup_g, "ffn_up_dw_g"), _mm_tn(hn2, dup_v, "ffn_up_dw_v")], axis=1)
    dh1, d_mix, dg_ffn_pre, dg_mix_post = _mid_bwd(h1, norm_ffn_pre, d_hn2, dh2, mix, norm_mix_post)
    d_an = _mm([(d_mix, wout_a)], f32, True, "out_proj_dx_a")
    d_ssm = _mm([(d_mix, wout_s)], f32, True, "out_proj_dx_s")
    dw_out = jnp.concatenate([_mm_tn(an, d_mix, "out_proj_dw_a"), _mm_tn(ssm, d_mix, "out_proj_dw_s")], axis=0)

    do_h, delta, dg_attn_out = _attn_out_bwd(attn, attn_out_norm, d_an)
    def col_blocks(gm):
        r, cc = gm.shape
        return jnp.transpose(gm.reshape(r, N_DEV, cc // N_DEV), (1, 0, 2))

    blocks_a = [dw_out.reshape(N_DEV, (D_ATTN + D_SSM) // N_DEV, D), dw_down.reshape(N_DEV, D_FF // N_DEV, D)]
    gpack_a = _pack(blocks_a, 1, pack_rows(blocks_a, 1), bf16)
    dqh, dkh, dvh, gparts_a, gparts_up = _flash_bwd(qh, kh, vh, do_h, lse, delta,
                                                    [gpack_a, col_blocks(dw_up).astype(bf16)], True)
    dq_full = _rope_q_bwd(dqh, cos_t, sin_t)
    dkv_full, d_kpe_raw = _rope_k_bwd(dkh, dvh, cos_t, sin_t)
    d_qn = _mm([(dq_full, wuq)], f32, True, "up_q_dx")
    dw_uq = _mm_tn(qn, dq_full, "up_q_dw")
    d_kvn = _mm([(dkv_full, wukv)], f32, True, "up_kv_dx")
    dw_ukv = _mm_tn(kvn, dkv_full, "up_kv_dw")
    d_q_c, dg_q = _rms_bwd(q_c, q_a_norm, d_qn, bf16, "norm_q_bwd")
    d_kv_c, dg_kv = _rms_bwd(kv_c, kv_a_norm, d_kvn, bf16, "norm_kv_bwd")

    dy_ssd, dz, dg_ssm, dd_heads = _gate_norm_bwd(y_ssd, xbc_c, z, d_exp, ssm_norm, d_ssm, head_ind)
    d_xbc_c, ddt, da_heads = _ssd_bwd(xbc_c, dtp, dt_e, dtt, a_row, a_e, a_col, hin, dy_ssd, d_exp, head_ind)
    d_xbc, dw_sconv, db_sconv = _ssm_conv_bwd(xbc, sconv_w, ssm_conv_b, d_xbc_c)
    d_dt_raw, d_dt_bias = _dt_bwd(dt_raw, dt_bias_p, ddt)

    dw_q, dw_kv, dw_rope, dw_z, dw_xbc, dw_dt = _in_proj_dw(hn1, [d_q_c, d_kv_c, d_kpe_raw, dz, d_xbc, d_dt_raw])
    dw_in = jnp.concatenate([dw_q, dw_kv, dw_rope[:, :QK_ROPE], dw_z, dw_xbc, dw_dt[:, :SSM_HEADS]], axis=1)
    dw_uq3 = dw_uq.reshape(Q_RANK, MLA_HEADS, QK_PAD)[:, :, :QK_NOPE + QK_ROPE]
    blocks_b = [
        dw_uq3.reshape(N_DEV, Q_RANK // N_DEV, MLA_HEADS, QK_NOPE + QK_ROPE),
        dw_ukv.reshape(N_DEV, KV_RANK // N_DEV, MLA_HEADS, QK_NOPE + V_DIM),
        col_blocks(dw_sconv[:SSM_CONV]),
        col_blocks(jnp.concatenate([dwc_g, dwc_v], axis=1)[:FFN_CONV]),
    ]
    gpack_b = _pack(blocks_b, 1, pack_rows(blocks_b, 1), bf16)
    segs = [(d_q_c, w_q), (d_kv_c, w_kv), (d_kpe_raw, w_rope), (dz, w_z), (d_xbc, w_xbc), (d_dt_raw, w_dt)]
    d_hn1, gparts_b, gparts_in = _mm(segs, f32, True, "proj_dx",
                                     carried=[gpack_b, col_blocks(dw_in).astype(bf16)], scatter=True)
    dh0, dg_mix_pre = _rms_bwd(h0, norm_mix_pre, d_hn1, f32, "norm_mix_pre_bwd", residual=dh1)

    grad_x = dh0[N_META:n_real][None]
    meta_blocks = col_blocks(dh0[:N_META]).reshape(N_DEV, N_META * D // N_DEV // PACK_W, PACK_W)


    def adam_group(parts, grp, name):
        rows = parts.shape[1]
        packs = [_pack([a[None] for a in grp[k]], 1, rows, f32)[0] for k in ("w", "m", "v")]
        outs = _adamw(parts, *packs, name)
        shapes = [a.shape for a in grp["w"]]
        return [dict(zip(grp["names"], [t[0] for t in _unpack(b[None], shapes, 1)])) for b in outs]

    def adam_own_layout(parts, name, w, m, v):
        return [{name: t[None]} for t in _adamw(parts, w[0], m[0], v[0], "adamw_" + name)]

    sh_a = adam_group(gparts_a, grp_a, "adamw_sharded_a")
    sh_b = adam_group(gparts_b, grp_b, "adamw_sharded_b")
    sh_in = adam_own_layout(gparts_in, "w_in", w_in, m_w_in, v_w_in)
    sh_up = adam_own_layout(gparts_up, "w_up", w_up, m_w_up, v_w_up)

    dg_alog = da_heads[:, :SSM_HEADS] * a_neg
    repl_g = [dg_mix_pre, dg_mix_post, dg_ffn_pre, dg_ffn_post, dg_q, dg_kv, dg_attn_out, db_sconv,
              d_dt_bias[:, :SSM_HEADS], dg_alog, dd_heads[:, :SSM_HEADS], dg_ssm,
              jnp.concatenate([dbc_g, dbc_v], axis=1)]
    loss_vec = loss_part[:, :1]
    small_total = _round_up(sum(-(-int(np.prod(a.shape)) // PACK_W) for a in repl_g) + 1, 16)
    spack = _pack(repl_g + [loss_vec], 0, small_total, f32)
    gparts_meta, sparts = _exchange_tail([meta_blocks], [spack], "exchange_tail")
    sh_meta = adam_group(gparts_meta, grp_meta, "adamw_meta")
    loss_row, repl_out = _adamw_replicated(sparts, repl_w, repl_m, repl_v)
    loss = loss_row[0, 0]

    order = ["meta_tokens", "norm_mix_pre", "norm_mix_post", "norm_ffn_pre", "norm_ffn_post", "w_in", "q_a_norm",
             "w_uq", "kv_a_norm", "w_ukv", "attn_out_norm", "ssm_conv_w", "ssm_conv_b", "ssm_dt_bias", "ssm_A_log",
             "ssm_D", "ssm_norm", "w_out", "w_up", "ffn_conv_w", "ffn_conv_b", "w_down"]
    rp_names = ["norm_mix_pre", "norm_mix_post", "norm_ffn_pre", "norm_ffn_post", "q_a_norm", "kv_a_norm",
                "attn_out_norm", "ssm_conv_b", "ssm_dt_bias", "ssm_A_log", "ssm_D", "ssm_norm", "ffn_conv_b"]

    def lookup(k):
        d = {**sh_a[k], **sh_b[k], **sh_in[k], **sh_up[k], **sh_meta[k],
             **{n: four[k] for n, four in zip(rp_names, repl_out)}}
        return [d[n] for n in order]

    return (loss, grad_x, *lookup(0), *lookup(1), *lookup(2), *lookup(3))
```

```python
import functools
import math

import jax
import jax.numpy as jnp
import numpy as np
from jax import lax
from jax.experimental import pallas as pl
from jax.experimental.pallas import tpu as pltpu

f32 = jnp.float32
bf16 = jnp.bfloat16

D_MODEL = 1024
SEQ = 8192
N_META = 16
MLA_HEADS = 8
QK_NOPE = 128
QK_ROPE = 64
V_DIM = 128
Q_RANK = 384
KV_RANK = 256
ROPE_THETA = 10000.0
SOFTMAX_SCALE = (QK_NOPE + QK_ROPE) ** -0.5
D_ATTN = MLA_HEADS * V_DIM
SSM_HEADS = 16
SSM_P = 64
SSM_GROUPS = 2
SSM_HPG = SSM_HEADS // SSM_GROUPS
SSM_N = 128
SSM_CONV = 4
CHUNK = 128
D_SSM = SSM_HEADS * SSM_P
D_BC = SSM_GROUPS * SSM_N
D_XBC = D_SSM + 2 * D_BC
D_FF = 2816
FFN_CONV = 3
EPS = 1e-6
D_IN = Q_RANK + KV_RANK + QK_ROPE + D_SSM + D_XBC + SSM_HEADS
QK_PAD = 256
N_DEV = 8

ADAM_LR = 0.001
ADAM_B1 = 0.9
ADAM_B2 = 0.999
ADAM_EPS = 1e-08
ADAM_WD = 0.01
ADAM_STEP = 10

LANES = 128
SUBLANES = 8
ROW_TILE = 256
VMEM_LIMIT = 56 * 1024 * 1024
PACK_W = 1024
PACK_ROW_TILE = 128
NEG = -1e30
LOG2E = math.log2(math.e)
LN2 = math.log(2.0)
Q_PRESCALE = SOFTMAX_SCALE * LOG2E

_MESH = pl.DeviceIdType.MESH


def _pick(n, prefs):
    for p in prefs:
        if n % p == 0:
            return p
    return n


def _rt(m):
    return _pick(m, (384, ROW_TILE))


def _cparams(sem):
    return pltpu.CompilerParams(dimension_semantics=sem, vmem_limit_bytes=VMEM_LIMIT)


def _row(spec_cols, tm):
    return pl.BlockSpec((tm, spec_cols), lambda i: (i, 0))


def _full(shape):
    nd = len(shape)
    return pl.BlockSpec(shape, lambda *a: (0,) * nd)


def _sigmoid(x):
    return 1.0 / (1.0 + jnp.exp(-x))


def _silu(x):
    return x * _sigmoid(x)


def _dsilu(x):
    s = _sigmoid(x)
    return s * (1.0 + x * (1.0 - s))


def _dot(a, b):
    return jnp.dot(a, b, preferred_element_type=f32)


def _dot_nt(a, b):
    return lax.dot_general(a, b, (((1,), (1,)), ((), ())), preferred_element_type=f32)


def _dot_tn(a, b):
    return lax.dot_general(a, b, (((0,), (0,)), ((), ())), preferred_element_type=f32)


def _dot_hi(a, b):
    return jnp.dot(a, b, precision=lax.Precision.HIGHEST, preferred_element_type=f32)


def _mm(pairs, out_dtype, trans_b, name, carried=(), scatter=False):
    n = len(pairs)
    nx = len(carried)
    M = pairs[0][0].shape[0]
    N = pairs[0][1].shape[0] if trans_b else pairs[0][1].shape[1]
    tm = _pick(M, (768, 512, 256))
    tn = _pick(N, (512, 1408, 384, 256, 128))
    ni, nj = M // tm, N // tn

    def body(*refs):
        o_ref = refs[2 * n + nx]
        if nx:
            i, j = pl.program_id(0), pl.program_id(1)
            _hosted_exchange(refs[2 * n:2 * n + nx], refs[2 * n + nx + 1:2 * n + 2 * nx + 1],
                             refs[2 * n + 2 * nx + 1:], scatter,
                             (i == 0) & (j == 0), (i == ni - 1) & (j == nj - 1))
        acc = None
        for p in range(n):
            a = refs[2 * p][...].astype(bf16)
            b = refs[2 * p + 1][...].astype(bf16)
            r = _dot_nt(a, b) if trans_b else _dot(a, b)
            acc = r if acc is None else acc + r
        o_ref[...] = acc.astype(out_dtype)

    in_specs, args = [], []
    for a, b in pairs:
        k = a.shape[1]
        in_specs.append(pl.BlockSpec((tm, k), lambda i, j: (i, 0)))
        if trans_b:
            in_specs.append(pl.BlockSpec((tn, k), lambda i, j: (j, 0)))
        else:
            in_specs.append(pl.BlockSpec((k, tn), lambda i, j: (0, j)))
        args += [a, b]
    out_spec = pl.BlockSpec((tm, tn), lambda i, j: (i, j))
    out_shape = jax.ShapeDtypeStruct((M, N), out_dtype)
    if not nx:
        return pl.pallas_call(
            body, name=name, grid=(ni, nj), in_specs=in_specs, out_specs=out_spec, out_shape=out_shape,
            compiler_params=_cparams(("parallel", "parallel")),
        )(*args)
    any_spec = pl.BlockSpec(memory_space=pl.ANY)
    return pl.pallas_call(
        body, name=name, grid=(ni, nj), in_specs=in_specs + [any_spec] * nx,
        out_specs=[out_spec] + [any_spec] * nx,
        out_shape=[out_shape] + _exchange_shapes(carried, scatter),
        scratch_shapes=_exchange_sems(nx),
        compiler_params=_cparams(("arbitrary", "arbitrary")),
    )(*args, *carried)


def _mm_tn(a, g, name):
    M, K = a.shape
    N = g.shape[1]
    tm = _pick(M, (768, 512, 256))
    tk = _pick(K, (1024, 1408, 512, 384, 256))
    tn = _pick(N, (1024, 1408, 512, 384, 256, 128))

    def body(a_ref, g_ref, o_ref):
        @pl.when(pl.program_id(2) == 0)
        def _():
            o_ref[...] = jnp.zeros_like(o_ref)

        o_ref[...] += _dot_tn(a_ref[...].astype(bf16), g_ref[...].astype(bf16))

    return pl.pallas_call(
        body, name=name, grid=(K // tk, N // tn, M // tm),
        in_specs=[pl.BlockSpec((tm, tk), lambda k, j, m: (m, k)),
                  pl.BlockSpec((tm, tn), lambda k, j, m: (m, j))],
        out_specs=pl.BlockSpec((tk, tn), lambda k, j, m: (k, j)),
        out_shape=jax.ShapeDtypeStruct((K, N), f32),
        compiler_params=_cparams(("parallel", "parallel", "arbitrary")),
    )(a, g)


def _rstd(x):
    return lax.rsqrt(jnp.mean(x * x, axis=-1, keepdims=True) + EPS)


def _rms_bwd_math(x, g, dy):
    r = _rstd(x)
    xh = x * r
    dn = dy * g
    dx = r * (dn - xh * jnp.mean(dn * xh, axis=-1, keepdims=True))
    return dx, dy * xh


def _rms_fwd(x, g, out_dtype, name):
    M, K = x.shape
    tm = _rt(M)

    def body(x_ref, g_ref, o_ref):
        xv = x_ref[...]
        o_ref[...] = (xv * _rstd(xv) * g_ref[...]).astype(out_dtype)

    return pl.pallas_call(
        body, name=name, grid=(M // tm,), in_specs=[_row(K, tm), _full((1, K))],
        out_specs=_row(K, tm), out_shape=jax.ShapeDtypeStruct((M, K), out_dtype),
        compiler_params=_cparams(("parallel",)),
    )(x, g)


def _rms_bwd(x, g, dy, out_dtype, name, residual=None):
    M, K = x.shape
    tm = _rt(M)
    has_res = residual is not None

    def body(*refs):
        if has_res:
            x_ref, g_ref, dy_ref, r_ref, dx_ref, dg_ref = refs
        else:
            x_ref, g_ref, dy_ref, dx_ref, dg_ref = refs

        @pl.when(pl.program_id(0) == 0)
        def _():
            dg_ref[...] = jnp.zeros_like(dg_ref)

        dx, dgp = _rms_bwd_math(x_ref[...], g_ref[...], dy_ref[...].astype(f32))
        if has_res:
            dx = dx + r_ref[...]
        dx_ref[...] = dx.astype(out_dtype)
        dg_ref[...] += jnp.sum(dgp, axis=0, keepdims=True)

    ins = [x, g, dy] + ([residual] if has_res else [])
    in_specs = [_row(K, tm), _full((1, K)), _row(K, tm)] + ([_row(K, tm)] if has_res else [])
    return pl.pallas_call(
        body, name=name, grid=(M // tm,), in_specs=in_specs,
        out_specs=[_row(K, tm), _full((1, K))],
        out_shape=[jax.ShapeDtypeStruct((M, K), out_dtype), jax.ShapeDtypeStruct((1, K), f32)],
        compiler_params=_cparams(("arbitrary",)),
    )(*ins)


def _in_proj(h0, g, weights):
    M, K = h0.shape
    tm = _rt(M)
    n = len(weights)
    widths = [int(w.shape[1]) for w in weights]

    def body(x_ref, g_ref, *refs):
        xv = x_ref[...]
        hn = (xv * _rstd(xv) * g_ref[...]).astype(bf16)
        refs[n][...] = hn
        for p in range(n):
            refs[n + 1 + p][...] = _dot(hn, refs[p][...])

    return pl.pallas_call(
        body, name="in_proj", grid=(M // tm,),
        in_specs=[_row(K, tm), _full((1, K))] + [_full((K, wd)) for wd in widths],
        out_specs=[_row(K, tm)] + [_row(wd, tm) for wd in widths],
        out_shape=[jax.ShapeDtypeStruct((M, K), bf16)] + [jax.ShapeDtypeStruct((M, wd), f32) for wd in widths],
        compiler_params=_cparams(("parallel",)),
    )(h0, g, *weights)


def _in_proj_dw(hn, grads):
    M, K = hn.shape
    tm = _rt(M)
    n = len(grads)
    widths = [int(gr.shape[1]) for gr in grads]

    def body(a_ref, *refs):
        @pl.when(pl.program_id(0) == 0)
        def _():
            for p in range(n):
                refs[n + p][...] = jnp.zeros_like(refs[n + p])

        a = a_ref[...]
        for p in range(n):
            refs[n + p][...] += _dot_tn(a, refs[p][...].astype(bf16))

    return pl.pallas_call(
        body, name="in_proj_dw", grid=(M // tm,),
        in_specs=[_row(K, tm)] + [_row(wd, tm) for wd in widths],
        out_specs=[_full((K, wd)) for wd in widths],
        out_shape=[jax.ShapeDtypeStruct((K, wd), f32) for wd in widths],
        compiler_params=_cparams(("arbitrary",)),
    )(hn, *grads)


def _resid_norm(h0, mix, g2, g3):
    M, K = h0.shape
    tm = _rt(M)

    def body(h_ref, m_ref, g2_ref, g3_ref, h1_ref, hn_ref):
        mv = m_ref[...]
        h1 = h_ref[...] + mv * _rstd(mv) * g2_ref[...]
        h1_ref[...] = h1
        hn_ref[...] = (h1 * _rstd(h1) * g3_ref[...]).astype(bf16)

    return pl.pallas_call(
        body, name="resid_norm", grid=(M // tm,),
        in_specs=[_row(K, tm), _row(K, tm), _full((1, K)), _full((1, K))],
        out_specs=[_row(K, tm), _row(K, tm)],
        out_shape=[jax.ShapeDtypeStruct((M, K), f32), jax.ShapeDtypeStruct((M, K), bf16)],
        compiler_params=_cparams(("parallel",)),
    )(h0, mix, g2, g3)


def _final(h1, down, g4, tgt, n_real):
    M, K = h1.shape
    tm = _rt(M)
    nt = M // tm

    def body(h_ref, d_ref, g_ref, t_ref, dh_ref, dd_ref, dg_ref, ls_ref, acc_ref):
        i = pl.program_id(0)

        @pl.when(i == 0)
        def _():
            dg_ref[...] = jnp.zeros_like(dg_ref)
            acc_ref[...] = jnp.zeros_like(acc_ref)

        dv = d_ref[...]
        g = g_ref[...]
        r = _rstd(dv)
        n = dv * r
        h2 = h_ref[...] + n * g
        rows = i * tm + lax.broadcasted_iota(jnp.int32, (tm, 1), 0)
        mask = ((rows >= N_META) & (rows < n_real)).astype(f32)
        diff = (h2 - t_ref[...]) * mask
        acc_ref[...] += jnp.sum(diff * diff, axis=0, keepdims=True)
        dh = diff * (1.0 / K)
        dh_ref[...] = dh
        dn = dh * g
        dd_ref[...] = (r * (dn - n * jnp.mean(dn * n, axis=-1, keepdims=True))).astype(bf16)
        dg_ref[...] += jnp.sum(dh * n, axis=0, keepdims=True)

        @pl.when(i == nt - 1)
        def _():
            ls_ref[...] = jnp.zeros((1, LANES), f32) + jnp.sum(acc_ref[...]) * (0.5 / K)

    return pl.pallas_call(
        body, name="final_loss", grid=(nt,),
        in_specs=[_row(K, tm), _row(K, tm), _full((1, K)), _row(K, tm)],
        out_specs=[_row(K, tm), _row(K, tm), _full((1, K)), _full((1, LANES))],
        out_shape=[jax.ShapeDtypeStruct((M, K), f32), jax.ShapeDtypeStruct((M, K), bf16),
                   jax.ShapeDtypeStruct((1, K), f32), jax.ShapeDtypeStruct((1, LANES), f32)],
        scratch_shapes=[pltpu.VMEM((1, K), f32)],
        compiler_params=_cparams(("arbitrary",)),
    )(h1, down, g4, tgt)


def _mid_bwd(h1, g3, d_hn2, dh2, mix, g2):
    M, K = h1.shape
    tm = _rt(M)

    def body(h_ref, g3_ref, dn_ref, dh2_ref, m_ref, g2_ref, dh1_ref, dm_ref, dg3_ref, dg2_ref):
        @pl.when(pl.program_id(0) == 0)
        def _():
            dg3_ref[...] = jnp.zeros_like(dg3_ref)
            dg2_ref[...] = jnp.zeros_like(dg2_ref)

        dx, dgp = _rms_bwd_math(h_ref[...], g3_ref[...], dn_ref[...])
        dh1 = dh2_ref[...] + dx
        dh1_ref[...] = dh1
        dg3_ref[...] += jnp.sum(dgp, axis=0, keepdims=True)
        dm, dgp2 = _rms_bwd_math(m_ref[...], g2_ref[...], dh1)
        dm_ref[...] = dm.astype(bf16)
        dg2_ref[...] += jnp.sum(dgp2, axis=0, keepdims=True)

    return pl.pallas_call(
        body, name="mid_bwd", grid=(M // tm,),
        in_specs=[_row(K, tm), _full((1, K)), _row(K, tm), _row(K, tm), _row(K, tm), _full((1, K))],
        out_specs=[_row(K, tm), _row(K, tm), _full((1, K)), _full((1, K))],
        out_shape=[jax.ShapeDtypeStruct((M, K), f32), jax.ShapeDtypeStruct((M, K), bf16),
                   jax.ShapeDtypeStruct((1, K), f32), jax.ShapeDtypeStruct((1, K), f32)],
        compiler_params=_cparams(("arbitrary",)),
    )(h1, g3, d_hn2, dh2, mix, g2)


HEADS_PER_STEP = 4
CONV_RB = 16


def _conv_block_taps(x_ref, halo, rb, lanes, kw):
    r0 = rb * CONV_RB
    if rb == 0:
        cat = jnp.concatenate([halo, x_ref[0:CONV_RB, lanes]], axis=0)
        first = SUBLANES - (kw - 1)
        return [cat[first + k:first + k + CONV_RB] for k in range(kw)]
    return [x_ref[r0 - (kw - 1) + k:r0 - (kw - 1) + k + CONV_RB, lanes] for k in range(kw)]


def _conv_weighted(taps, w, kw):
    u = None
    for k in range(kw):
        t = taps[k] * w[k:k + 1, :]
        u = t if u is None else u + t
    return u


def _conv_block_dx(du, nxt, w, kw):
    cat = jnp.concatenate([du, nxt], axis=0)
    return _conv_weighted([cat[kw - 1 - k:kw - 1 - k + CONV_RB] for k in range(kw)], w, kw)


def _prev_spec(tm, tc, col_of, row_axis, reversed_tiles=0):
    def imap(*ids):
        i = ids[row_axis]
        if reversed_tiles:
            i = reversed_tiles - 1 - i
        return (jnp.maximum(i * (tm // SUBLANES) - 1, 0), col_of(*ids))
    return pl.BlockSpec((SUBLANES, tc), imap)


def _ssm_conv_fwd(xbc, w, b):
    M, C = xbc.shape
    tm, tc, kw = ROW_TILE, C, SSM_CONV

    def body(x_ref, h_ref, w_ref, b_ref, o_ref):
        i = pl.program_id(0)

        def chunk(j, carry):
            lanes = pl.ds(pl.multiple_of(j * LANES, LANES), LANES)
            halo = jnp.where(i == 0, 0.0, h_ref[:, lanes])
            wv = w_ref[:, lanes]
            bv = b_ref[:, lanes]
            for rb in range(tm // CONV_RB):
                u = _conv_weighted(_conv_block_taps(x_ref, halo, rb, lanes, kw), wv, kw) + bv
                o_ref[rb * CONV_RB:(rb + 1) * CONV_RB, lanes] = _silu(u)
            return carry

        lax.fori_loop(0, tc // LANES, chunk, 0)

    return pl.pallas_call(
        body, name="ssm_conv_fwd", grid=(M // tm, C // tc),
        in_specs=[pl.BlockSpec((tm, tc), lambda i, j: (i, j)),
                  _prev_spec(tm, tc, lambda i, j: j, 0),
                  pl.BlockSpec((SUBLANES, tc), lambda i, j: (0, j)),
                  pl.BlockSpec((1, tc), lambda i, j: (0, j))],
        out_specs=pl.BlockSpec((tm, tc), lambda i, j: (i, j)),
        out_shape=jax.ShapeDtypeStruct((M, C), f32),
        compiler_params=_cparams(("parallel", "parallel")),
    )(xbc, xbc, w, b)


def _ssm_conv_bwd(xbc, w, b, dout):
    M, C = xbc.shape
    tm, tc, kw = ROW_TILE, C // 3, SSM_CONV
    nt = M // tm

    def body(x_ref, h_ref, w_ref, b_ref, d_ref, dx_ref, dw_ref, db_ref, nxt_ref):
        i = pl.program_id(1)

        @pl.when(i == 0)
        def _():
            dw_ref[...] = jnp.zeros_like(dw_ref)
            db_ref[...] = jnp.zeros_like(db_ref)
            nxt_ref[...] = jnp.zeros_like(nxt_ref)

        def chunk(j, carry):
            lanes = pl.ds(pl.multiple_of(j * LANES, LANES), LANES)
            halo = jnp.where(i == nt - 1, 0.0, h_ref[:, lanes])
            wv = w_ref[:, lanes]
            bv = b_ref[:, lanes]
            nxt = nxt_ref[:, lanes]
            db = jnp.zeros((CONV_RB, LANES), f32)
            dw = [jnp.zeros((CONV_RB, LANES), f32) for _ in range(kw)]
            for rb in reversed(range(tm // CONV_RB)):
                rows = slice(rb * CONV_RB, (rb + 1) * CONV_RB)
                taps = _conv_block_taps(x_ref, halo, rb, lanes, kw)
                du = d_ref[rows, lanes] * _dsilu(_conv_weighted(taps, wv, kw) + bv)
                db = db + du
                dw = [dw[k] + du * taps[k] for k in range(kw)]
                dx_ref[rows, lanes] = _conv_block_dx(du, nxt, wv, kw).astype(bf16)
                nxt = du[0:SUBLANES]
            nxt_ref[:, lanes] = nxt
            db_ref[:, lanes] += jnp.sum(db, axis=0, keepdims=True)
            for k in range(kw):
                dw_ref[k:k + 1, lanes] += jnp.sum(dw[k], axis=0, keepdims=True)
            return carry

        lax.fori_loop(0, tc // LANES, chunk, 0)

    tile = pl.BlockSpec((tm, tc), lambda j, i: (nt - 1 - i, j))
    return pl.pallas_call(
        body, name="ssm_conv_bwd", grid=(C // tc, nt),
        in_specs=[tile, _prev_spec(tm, tc, lambda j, i: j, 1, nt),
                  pl.BlockSpec((SUBLANES, tc), lambda j, i: (0, j)),
                  pl.BlockSpec((1, tc), lambda j, i: (0, j)), tile],
        out_specs=[tile, pl.BlockSpec((SUBLANES, tc), lambda j, i: (0, j)),
                   pl.BlockSpec((1, tc), lambda j, i: (0, j))],
        out_shape=[jax.ShapeDtypeStruct((M, C), bf16), jax.ShapeDtypeStruct((SUBLANES, C), f32),
                   jax.ShapeDtypeStruct((1, C), f32)],
        scratch_shapes=[pltpu.VMEM((SUBLANES, tc), f32)],
        compiler_params=_cparams(("parallel", "arbitrary")),
    )(xbc, xbc, w, b, dout)


def _ffn_gate_fwd(up, w, b):
    M = up.shape[0]
    tm, tc, kw = ROW_TILE, D_FF // 2, FFN_CONV
    nc = D_FF // tc

    def body(xg_ref, hg_ref, xv_ref, hv_ref, wg_ref, wv_ref, bg_ref, bv_ref, o_ref):
        i = pl.program_id(0)

        def chunk(j, carry):
            lanes = pl.ds(pl.multiple_of(j * LANES, LANES), LANES)
            halo_g = jnp.where(i == 0, 0.0, hg_ref[:, lanes])
            halo_v = jnp.where(i == 0, 0.0, hv_ref[:, lanes])
            wg, wv = wg_ref[:, lanes], wv_ref[:, lanes]
            bg, bv = bg_ref[:, lanes], bv_ref[:, lanes]
            for rb in range(tm // CONV_RB):
                ug = _conv_weighted(_conv_block_taps(xg_ref, halo_g, rb, lanes, kw), wg, kw) + bg
                uv = _conv_weighted(_conv_block_taps(xv_ref, halo_v, rb, lanes, kw), wv, kw) + bv
                o_ref[rb * CONV_RB:(rb + 1) * CONV_RB, lanes] = (_silu(ug) * uv).astype(bf16)
            return carry

        lax.fori_loop(0, tc // LANES, chunk, 0)

    return pl.pallas_call(
        body, name="ffn_gate_fwd", grid=(M // tm, nc),
        in_specs=[pl.BlockSpec((tm, tc), lambda i, j: (i, j)),
                  _prev_spec(tm, tc, lambda i, j: j, 0),
                  pl.BlockSpec((tm, tc), lambda i, j: (i, j + nc)),
                  _prev_spec(tm, tc, lambda i, j: j + nc, 0),
                  pl.BlockSpec((SUBLANES, tc), lambda i, j: (0, j)),
                  pl.BlockSpec((SUBLANES, tc), lambda i, j: (0, j + nc)),
                  pl.BlockSpec((1, tc), lambda i, j: (0, j)),
                  pl.BlockSpec((1, tc), lambda i, j: (0, j + nc))],
        out_specs=pl.BlockSpec((tm, tc), lambda i, j: (i, j)),
        out_shape=jax.ShapeDtypeStruct((M, D_FF), bf16),
        compiler_params=_cparams(("parallel", "parallel")),
    )(up, up, up, up, w, w, b, b)


def _ffn_gate_bwd(up, w, b, d_act):
    M = up.shape[0]
    tm, tc, kw = ROW_TILE, D_FF // 2, FFN_CONV
    nc = D_FF // tc
    nt = M // tm

    def body(xg_ref, hg_ref, xv_ref, hv_ref, wg_ref, wv_ref, bg_ref, bv_ref, d_ref,
             dxg_ref, dxv_ref, dwg_ref, dwv_ref, dbg_ref, dbv_ref, ng_ref, nv_ref):
        i = pl.program_id(1)

        @pl.when(i == 0)
        def _():
            for r in (dwg_ref, dwv_ref, dbg_ref, dbv_ref, ng_ref, nv_ref):
                r[...] = jnp.zeros_like(r)

        def chunk(j, carry):
            lanes = pl.ds(pl.multiple_of(j * LANES, LANES), LANES)
            halo_g = jnp.where(i == nt - 1, 0.0, hg_ref[:, lanes])
            halo_v = jnp.where(i == nt - 1, 0.0, hv_ref[:, lanes])
            wg, wv = wg_ref[:, lanes], wv_ref[:, lanes]
            bg, bv = bg_ref[:, lanes], bv_ref[:, lanes]
            nxt_g, nxt_v = ng_ref[:, lanes], nv_ref[:, lanes]
            zero = jnp.zeros((CONV_RB, LANES), f32)
            dbg, dbv = zero, zero
            dwg = [zero for _ in range(kw)]
            dwv = [zero for _ in range(kw)]
            for rb in reversed(range(tm // CONV_RB)):
                rows = slice(rb * CONV_RB, (rb + 1) * CONV_RB)
                tg = _conv_block_taps(xg_ref, halo_g, rb, lanes, kw)
                tv = _conv_block_taps(xv_ref, halo_v, rb, lanes, kw)
                ug = _conv_weighted(tg, wg, kw) + bg
                uv = _conv_weighted(tv, wv, kw) + bv
                sg = _sigmoid(ug)
                da = d_ref[rows, lanes]
                dug = da * uv * (sg * (1.0 + ug * (1.0 - sg)))
                duv = da * (ug * sg)
                dbg = dbg + dug
                dbv = dbv + duv
                dwg = [dwg[k] + dug * tg[k] for k in range(kw)]
                dwv = [dwv[k] + duv * tv[k] for k in range(kw)]
                dxg_ref[rows, lanes] = _conv_block_dx(dug, nxt_g, wg, kw).astype(bf16)
                dxv_ref[rows, lanes] = _conv_block_dx(duv, nxt_v, wv, kw).astype(bf16)
                nxt_g, nxt_v = dug[0:SUBLANES], duv[0:SUBLANES]
            ng_ref[:, lanes] = nxt_g
            nv_ref[:, lanes] = nxt_v
            dbg_ref[:, lanes] += jnp.sum(dbg, axis=0, keepdims=True)
            dbv_ref[:, lanes] += jnp.sum(dbv, axis=0, keepdims=True)
            for k in range(kw):
                dwg_ref[k:k + 1, lanes] += jnp.sum(dwg[k], axis=0, keepdims=True)
                dwv_ref[k:k + 1, lanes] += jnp.sum(dwv[k], axis=0, keepdims=True)
            return carry

        lax.fori_loop(0, tc // LANES, chunk, 0)

    tile_g = pl.BlockSpec((tm, tc), lambda j, i: (nt - 1 - i, j))
    tile_v = pl.BlockSpec((tm, tc), lambda j, i: (nt - 1 - i, j + nc))
    ext = pltpu.VMEM((SUBLANES, tc), f32)
    return pl.pallas_call(
        body, name="ffn_gate_bwd", grid=(nc, nt),
        in_specs=[tile_g, _prev_spec(tm, tc, lambda j, i: j, 1, nt),
                  tile_v, _prev_spec(tm, tc, lambda j, i: j + nc, 1, nt),
                  pl.BlockSpec((SUBLANES, tc), lambda j, i: (0, j)),
                  pl.BlockSpec((SUBLANES, tc), lambda j, i: (0, j + nc)),
                  pl.BlockSpec((1, tc), lambda j, i: (0, j)),
                  pl.BlockSpec((1, tc), lambda j, i: (0, j + nc)),
                  tile_g],
        out_specs=[tile_g, tile_g,
                   pl.BlockSpec((SUBLANES, tc), lambda j, i: (0, j)),
                   pl.BlockSpec((SUBLANES, tc), lambda j, i: (0, j)),
                   pl.BlockSpec((1, tc), lambda j, i: (0, j)),
                   pl.BlockSpec((1, tc), lambda j, i: (0, j))],
        out_shape=[jax.ShapeDtypeStruct((M, D_FF), bf16), jax.ShapeDtypeStruct((M, D_FF), bf16),
                   jax.ShapeDtypeStruct((SUBLANES, D_FF), f32), jax.ShapeDtypeStruct((SUBLANES, D_FF), f32),
                   jax.ShapeDtypeStruct((1, D_FF), f32), jax.ShapeDtypeStruct((1, D_FF), f32)],
        scratch_shapes=[ext, ext],
        compiler_params=_cparams(("parallel", "arbitrary")),
    )(up, up, up, up, w, w, b, b, d_act)


def _rope_apply(blk, cos, sin):
    lane = lax.broadcasted_iota(jnp.int32, blk.shape, 1)
    half = QK_ROPE // 2
    partner = jnp.where(lane < half, pltpu.roll(blk, LANES - half, 1), pltpu.roll(blk, half, 1))
    return blk * cos + partner * sin


def _rope_unapply(d, cos, sin):
    t = d * sin
    lane = lax.broadcasted_iota(jnp.int32, d.shape, 1)
    half = QK_ROPE // 2
    partner = jnp.where(lane < half, pltpu.roll(t, LANES - half, 1), pltpu.roll(t, half, 1))
    return d * cos + partner


def _up_q_rope(q_c, g, wuq, cos, sin):
    M, K = q_c.shape
    tm = _pick(M, (768, 512, 256))

    hs = HEADS_PER_STEP

    def body(x_ref, g_ref, b_ref, c_ref, s_ref, a_ref, o_ref):
        xv = x_ref[...]
        a = (xv * _rstd(xv) * g_ref[...]).astype(bf16)
        a_ref[...] = a
        r = _dot(a, b_ref[...]) * Q_PRESCALE
        c, s = c_ref[...], s_ref[...]
        for u in range(hs):
            o_ref[u, :, 0:QK_NOPE] = r[:, u * QK_PAD:u * QK_PAD + QK_NOPE].astype(bf16)
            o_ref[u, :, QK_NOPE:QK_PAD] = _rope_apply(r[:, u * QK_PAD + QK_NOPE:(u + 1) * QK_PAD], c, s).astype(bf16)

    return pl.pallas_call(
        body, name="up_q_rope", grid=(M // tm, MLA_HEADS // hs),
        in_specs=[pl.BlockSpec((tm, K), lambda i, h: (i, 0)),
                  pl.BlockSpec((1, K), lambda i, h: (0, 0)),
                  pl.BlockSpec((K, hs * QK_PAD), lambda i, h: (0, h)),
                  pl.BlockSpec((tm, LANES), lambda i, h: (i, 0)),
                  pl.BlockSpec((tm, LANES), lambda i, h: (i, 0))],
        out_specs=[pl.BlockSpec((tm, K), lambda i, h: (i, 0)),
                   pl.BlockSpec((hs, tm, QK_PAD), lambda i, h: (h, i, 0))],
        out_shape=[jax.ShapeDtypeStruct((M, K), bf16), jax.ShapeDtypeStruct((MLA_HEADS, M, QK_PAD), bf16)],
        compiler_params=_cparams(("parallel", "arbitrary")),
    )(q_c, g, wuq, cos, sin)


def _up_kv_rope(kv_c, g, wukv, kpe_raw, cos, sin):
    M, K = kv_c.shape
    tm = _pick(M, (768, 512, 256))

    hs = HEADS_PER_STEP
    w = QK_NOPE + V_DIM

    def body(x_ref, g_ref, b_ref, pe_ref, c_ref, s_ref, a_ref, k_ref, v_ref):
        xv = x_ref[...]
        a = (xv * _rstd(xv) * g_ref[...]).astype(bf16)
        a_ref[...] = a
        r = _dot(a, b_ref[...])
        pe = _rope_apply(pe_ref[...], c_ref[...], s_ref[...]).astype(bf16)
        for u in range(hs):
            k_ref[u, :, 0:QK_NOPE] = r[:, u * w:u * w + QK_NOPE].astype(bf16)
            k_ref[u, :, QK_NOPE:QK_PAD] = pe
            v_ref[u] = r[:, u * w + QK_NOPE:(u + 1) * w].astype(bf16)

    return pl.pallas_call(
        body, name="up_kv_rope", grid=(M // tm, MLA_HEADS // hs),
        in_specs=[pl.BlockSpec((tm, K), lambda i, h: (i, 0)),
                  pl.BlockSpec((1, K), lambda i, h: (0, 0)),
                  pl.BlockSpec((K, hs * w), lambda i, h: (0, h)),
                  pl.BlockSpec((tm, LANES), lambda i, h: (i, 0)),
                  pl.BlockSpec((tm, LANES), lambda i, h: (i, 0)),
                  pl.BlockSpec((tm, LANES), lambda i, h: (i, 0))],
        out_specs=[pl.BlockSpec((tm, K), lambda i, h: (i, 0)),
                   pl.BlockSpec((hs, tm, QK_PAD), lambda i, h: (h, i, 0)),
                   pl.BlockSpec((hs, tm, V_DIM), lambda i, h: (h, i, 0))],
        out_shape=[jax.ShapeDtypeStruct((M, K), bf16), jax.ShapeDtypeStruct((MLA_HEADS, M, QK_PAD), bf16),
                   jax.ShapeDtypeStruct((MLA_HEADS, M, V_DIM), bf16)],
        compiler_params=_cparams(("parallel", "arbitrary")),
    )(kv_c, g, wukv, kpe_raw, cos, sin)


def _latent_bwd(d_full_sc, w_ref, x_ref, g_ref, a_ref, dx_ref, dg_ref, dw_ref):
    d_full = d_full_sc[...]
    dx, dgp = _rms_bwd_math(x_ref[...], g_ref[...], _dot_nt(d_full, w_ref[...]))
    dx_ref[...] = dx.astype(bf16)
    dg_ref[...] += jnp.sum(dgp, axis=0, keepdims=True)
    dw_ref[...] += _dot_tn(a_ref[...], d_full)


def _latent_bwd_call(body, name, head_inputs, head_specs, cos, sin, w, x, g, a, extra_out_specs, extra_out_shape):
    M, K = x.shape
    tm = _rt(M)
    N = w.shape[1]
    return pl.pallas_call(
        body, name=name, grid=(M // tm,),
        in_specs=head_specs + [_row(LANES, tm), _row(LANES, tm), _full((K, N)), _row(K, tm), _full((1, K)),
                               _row(K, tm)],
        out_specs=[_row(K, tm), _full((1, K)), _full((K, N))] + extra_out_specs,
        out_shape=[jax.ShapeDtypeStruct((M, K), bf16), jax.ShapeDtypeStruct((1, K), f32),
                   jax.ShapeDtypeStruct((K, N), f32)] + extra_out_shape,
        scratch_shapes=[pltpu.VMEM((tm, N), bf16)],
        compiler_params=_cparams(("arbitrary",)),
    )(*head_inputs, cos, sin, w, x, g, a)


def _q_branch_bwd(dq, cos, sin, wuq, q_c, g, qn):
    tm = _rt(q_c.shape[0])

    def body(d_ref, c_ref, s_ref, w_ref, x_ref, g_ref, a_ref, dx_ref, dg_ref, dw_ref, full_sc):
        @pl.when(pl.program_id(0) == 0)
        def _():
            dg_ref[...] = jnp.zeros_like(dg_ref)
            dw_ref[...] = jnp.zeros_like(dw_ref)

        c, s = c_ref[...], s_ref[...]
        for h in range(MLA_HEADS):
            full_sc[:, h * QK_PAD:h * QK_PAD + QK_NOPE] = (d_ref[h, :, 0:QK_NOPE] * SOFTMAX_SCALE).astype(bf16)
            full_sc[:, h * QK_PAD + QK_NOPE:(h + 1) * QK_PAD] = (_rope_unapply(
                d_ref[h, :, QK_NOPE:QK_PAD], c, s) * SOFTMAX_SCALE).astype(bf16)
        _latent_bwd(full_sc, w_ref, x_ref, g_ref, a_ref, dx_ref, dg_ref, dw_ref)

    return _latent_bwd_call(body, "q_branch_bwd", [dq],
                            [pl.BlockSpec((MLA_HEADS, tm, QK_PAD), lambda i: (0, i, 0))],
                            cos, sin, wuq, q_c, g, qn, [], [])


def _kv_branch_bwd(dk, dv, cos, sin, wukv, kv_c, g, kvn):
    M = kv_c.shape[0]
    tm = _rt(M)
    w = QK_NOPE + V_DIM

    def body(dk_ref, dv_ref, c_ref, s_ref, w_ref, x_ref, g_ref, a_ref, dx_ref, dg_ref, dw_ref, pe_ref, full_sc):
        @pl.when(pl.program_id(0) == 0)
        def _():
            dg_ref[...] = jnp.zeros_like(dg_ref)
            dw_ref[...] = jnp.zeros_like(dw_ref)

        pe = None
        for h in range(MLA_HEADS):
            full_sc[:, h * w:h * w + QK_NOPE] = dk_ref[h, :, 0:QK_NOPE].astype(bf16)
            full_sc[:, h * w + QK_NOPE:(h + 1) * w] = dv_ref[h].astype(bf16)
            t = dk_ref[h, :, QK_NOPE:QK_PAD]
            pe = t if pe is None else pe + t
        pe_ref[...] = _rope_unapply(pe, c_ref[...], s_ref[...])
        _latent_bwd(full_sc, w_ref, x_ref, g_ref, a_ref, dx_ref, dg_ref, dw_ref)

    return _latent_bwd_call(body, "kv_branch_bwd", [dk, dv],
                            [pl.BlockSpec((MLA_HEADS, tm, QK_PAD), lambda i: (0, i, 0)),
                             pl.BlockSpec((MLA_HEADS, tm, V_DIM), lambda i: (0, i, 0))],
                            cos, sin, wukv, kv_c, g, kvn, [_row(LANES, tm)],
                            [jax.ShapeDtypeStruct((M, LANES), f32)])


def _attn_tile(M):
    return 768 if (M % 768 == 0 and M >= 4 * 768) else ROW_TILE


def _col_to_row(col):
    return col.T[0:1, :]


def _hosted_exchange(refs_in, refs_out, sems, scatter, first, last):
    copies = _exchange_copies(refs_in, refs_out, *sems, scatter)

    @pl.when(first)
    def _():
        for cp in copies:
            cp.start()

    @pl.when(last)
    def _():
        for cp in copies:
            cp.wait()


def _flash_fwd(q, k, v, carried, scatter):
    H, M, _ = q.shape
    T = _attn_tile(M)
    nq = M // T
    nx = len(carried)

    def body(*refs):
        q_ref, k_ref, v_ref = refs[:3]
        o_ref, lse_ref = refs[3 + nx:5 + nx]
        sa_ref, sb_ref, m_sc, l_sc, acc_sc = refs[5 + 2 * nx:10 + 2 * nx]
        h = pl.program_id(0)
        i = pl.program_id(1)
        _hosted_exchange(refs[3:3 + nx], refs[5 + nx:5 + 2 * nx], refs[10 + 2 * nx:], scatter,
                         (h == 0) & (i == 0), (h == H - 1) & (i == nq - 1))
        qv = q_ref[0]
        m_sc[...] = jnp.full_like(m_sc, NEG)
        l_sc[...] = jnp.zeros_like(l_sc)
        acc_sc[...] = jnp.zeros_like(acc_sc)

        def scores(j, s_ref):
            off = pl.multiple_of(j * T, T)
            s_ref[...] = _dot_nt(qv, k_ref[0, pl.ds(off, T), :])

        def softmax_pv(j, s_ref, masked):
            off = pl.multiple_of(j * T, T)
            s = s_ref[...]
            if masked:
                r = lax.broadcasted_iota(jnp.int32, (T, T), 0)
                c = lax.broadcasted_iota(jnp.int32, (T, T), 1)
                s = jnp.where(r >= c, s, NEG)
            m_prev = m_sc[...]
            m_new = jnp.maximum(m_prev, jnp.max(s, axis=1, keepdims=True))
            alpha = jnp.exp2(m_prev - m_new)
            p = jnp.exp2(s - m_new[:, 0:1])
            l_sc[...] = alpha * l_sc[...] + jnp.sum(p, axis=1, keepdims=True)
            acc_sc[...] = alpha * acc_sc[...] + _dot(p.astype(bf16), v_ref[0, pl.ds(off, T), :])
            m_sc[...] = m_new

        scores(0, sa_ref)

        def pair(jj, c):
            j0 = 2 * jj
            scores(j0 + 1, sb_ref)
            softmax_pv(j0, sa_ref, False)
            scores(j0 + 2, sa_ref)
            softmax_pv(j0 + 1, sb_ref, False)
            return c

        lax.fori_loop(0, i // 2, pair, 0)

        @pl.when(i % 2 == 0)
        def _():
            softmax_pv(i, sa_ref, True)

        @pl.when(i % 2 == 1)
        def _():
            scores(i, sb_ref)
            softmax_pv(i - 1, sa_ref, False)
            softmax_pv(i, sb_ref, True)

        l = l_sc[...]
        o_ref[...] = acc_sc[...] / l
        lse_ref[0, 0] = _col_to_row(m_sc[...] + jnp.log2(l))

    any_spec = pl.BlockSpec(memory_space=pl.ANY)
    return pl.pallas_call(
        body, name="flash_fwd", grid=(H, nq),
        in_specs=[pl.BlockSpec((1, T, QK_PAD), lambda h, i: (h, i, 0)),
                  pl.BlockSpec((1, M, QK_PAD), lambda h, i: (h, 0, 0)),
                  pl.BlockSpec((1, M, V_DIM), lambda h, i: (h, 0, 0))] + [any_spec] * nx,
        out_specs=[pl.BlockSpec((T, V_DIM), lambda h, i: (i, h)),
                   pl.BlockSpec((1, 1, 1, T), lambda h, i: (h, i, 0, 0))] + [any_spec] * nx,
        out_shape=[jax.ShapeDtypeStruct((M, H * V_DIM), f32),
                   jax.ShapeDtypeStruct((H, nq, 1, T), f32)] + _exchange_shapes(carried, scatter),
        scratch_shapes=[pltpu.VMEM((T, T), f32), pltpu.VMEM((T, T), f32),
                        pltpu.VMEM((T, LANES), f32), pltpu.VMEM((T, LANES), f32),
                        pltpu.VMEM((T, V_DIM), f32)] + _exchange_sems(nx),
        compiler_params=_cparams(("arbitrary", "arbitrary")),
    )(q, k, v, *carried)


def _attn_out_bwd(o, g, d_an):
    M, K = o.shape
    H = MLA_HEADS
    T = _attn_tile(M)

    def body(o_ref, g_ref, d_ref, dh_ref, dl_ref, dg_ref):
        @pl.when(pl.program_id(0) == 0)
        def _():
            dg_ref[...] = jnp.zeros_like(dg_ref)

        ov = o_ref[...]
        do, dgp = _rms_bwd_math(ov, g_ref[...], d_ref[...])
        dg_ref[...] += jnp.sum(dgp, axis=0, keepdims=True)
        for h in range(H):
            sl = slice(h * V_DIM, (h + 1) * V_DIM)
            doh = do[:, sl]
            dh_ref[h] = doh.astype(bf16)
            col = jnp.sum(ov[:, sl] * doh, axis=1, keepdims=True) + jnp.zeros((T, LANES), f32)
            dl_ref[h, 0] = _col_to_row(col)

    return pl.pallas_call(
        body, name="attn_out_bwd", grid=(M // T,),
        in_specs=[_row(K, T), _full((1, K)), _row(K, T)],
        out_specs=[pl.BlockSpec((H, T, V_DIM), lambda i: (0, i, 0)),
                   pl.BlockSpec((H, 1, 1, T), lambda i: (0, i, 0, 0)),
                   _full((1, K))],
        out_shape=[jax.ShapeDtypeStruct((H, M, V_DIM), bf16),
                   jax.ShapeDtypeStruct((H, M // T, 1, T), f32),
                   jax.ShapeDtypeStruct((1, K), f32)],
        compiler_params=_cparams(("arbitrary",)),
    )(o, g, d_an)


def _flash_bwd(q, k, v, do, lse, delta, carried, scatter):
    H, M, _ = q.shape
    T = _attn_tile(M)
    nq = M // T
    nx = len(carried)

    def body(*refs):
        q_ref, do_ref, lse_ref, dl_ref, k_ref, v_ref = refs[:6]
        dq_ref, dk_ref, dv_ref = refs[6 + nx:9 + nx]
        dk_sc, dv_sc = refs[9 + 2 * nx:11 + 2 * nx]
        j = pl.program_id(1)
        _hosted_exchange(refs[6:6 + nx], refs[9 + nx:9 + 2 * nx], refs[11 + 2 * nx:], scatter,
                         (pl.program_id(0) == 0) & (j == 0), (pl.program_id(0) == H - 1) & (j == nq - 1))

        @pl.when(j == 0)
        def _():
            dq_ref[...] = jnp.zeros_like(dq_ref)

        kt = k_ref[0]
        vt = v_ref[0]
        dk_sc[...] = jnp.zeros_like(dk_sc)
        dv_sc[...] = jnp.zeros_like(dv_sc)

        def step(i, masked):
            off = pl.multiple_of(i * T, T)
            qt = q_ref[0, pl.ds(off, T), :]
            dot_ = do_ref[0, pl.ds(off, T), :]
            st = _dot_nt(kt, qt)
            if masked:
                r = lax.broadcasted_iota(jnp.int32, (T, T), 0)
                c = lax.broadcasted_iota(jnp.int32, (T, T), 1)
                st = jnp.where(c >= r, st, NEG)
            pt = jnp.exp2(st - lse_ref[0, i])
            dv_sc[...] += _dot(pt.astype(bf16), dot_)
            dpt = _dot_nt(vt, dot_)
            dst = (pt * (dpt - dl_ref[0, i])).astype(bf16)
            dk_sc[...] += _dot(dst, qt)
            dq_ref[0, pl.ds(off, T), :] += _dot_tn(dst, kt)

        step(j, True)

        def loop_body(i, c):
            step(i, False)
            return c

        lax.fori_loop(j + 1, nq, loop_body, 0)
        dk_ref[0] = dk_sc[...] * LN2
        dv_ref[0] = dv_sc[...]

    any_spec = pl.BlockSpec(memory_space=pl.ANY)
    return pl.pallas_call(
        body, name="flash_bwd", grid=(H, nq),
        in_specs=[pl.BlockSpec((1, M, QK_PAD), lambda h, j: (h, 0, 0)),
                  pl.BlockSpec((1, M, V_DIM), lambda h, j: (h, 0, 0)),
                  pl.BlockSpec((1, nq, 1, T), lambda h, j: (h, 0, 0, 0)),
                  pl.BlockSpec((1, nq, 1, T), lambda h, j: (h, 0, 0, 0)),
                  pl.BlockSpec((1, T, QK_PAD), lambda h, j: (h, j, 0)),
                  pl.BlockSpec((1, T, V_DIM), lambda h, j: (h, j, 0))] + [any_spec] * nx,
        out_specs=[pl.BlockSpec((1, M, QK_PAD), lambda h, j: (h, 0, 0)),
                   pl.BlockSpec((1, T, QK_PAD), lambda h, j: (h, j, 0)),
                   pl.BlockSpec((1, T, V_DIM), lambda h, j: (h, j, 0))] + [any_spec] * nx,
        out_shape=[jax.ShapeDtypeStruct((H, M, QK_PAD), f32),
                   jax.ShapeDtypeStruct((H, M, QK_PAD), f32),
                   jax.ShapeDtypeStruct((H, M, V_DIM), f32)] + _exchange_shapes(carried, scatter),
        scratch_shapes=[pltpu.VMEM((T, QK_PAD), f32), pltpu.VMEM((T, V_DIM), f32)] + _exchange_sems(nx),
        compiler_params=_cparams(("arbitrary", "arbitrary")),
    )(q, do, lse, delta, k, v, *carried)


def _dt_fwd(dt_raw, bias, expand):
    M = dt_raw.shape[0]
    tm = _rt(M)

    def body(x_ref, b_ref, e_ref, o_ref, oe_ref):
        u = x_ref[...] + b_ref[...]
        sp = jnp.maximum(u, 0.0) + jnp.log(1.0 + jnp.exp(-jnp.abs(u)))
        lane = lax.broadcasted_iota(jnp.int32, u.shape, 1)
        dtp = jnp.where(lane < SSM_HEADS, sp, 0.0)
        o_ref[...] = dtp
        oe_ref[...] = _dot_hi(dtp, e_ref[...])

    return pl.pallas_call(
        body, name="dt_fwd", grid=(M // tm,),
        in_specs=[_row(LANES, tm), _full((1, LANES)), _full((LANES, D_SSM))],
        out_specs=[_row(LANES, tm), _row(D_SSM, tm)],
        out_shape=[jax.ShapeDtypeStruct((M, LANES), f32), jax.ShapeDtypeStruct((M, D_SSM), f32)],
        compiler_params=_cparams(("parallel",)),
    )(dt_raw, bias, expand)


def _dt_bwd(dt_raw, bias, ddt):
    M = dt_raw.shape[0]
    tm = _rt(M)

    def body(x_ref, b_ref, d_ref, o_ref, db_ref):
        @pl.when(pl.program_id(0) == 0)
        def _():
            db_ref[...] = jnp.zeros_like(db_ref)

        u = x_ref[...] + b_ref[...]
        lane = lax.broadcasted_iota(jnp.int32, u.shape, 1)
        g = jnp.where(lane < SSM_HEADS, d_ref[...] * _sigmoid(u), 0.0)
        o_ref[...] = g
        db_ref[...] += jnp.sum(g, axis=0, keepdims=True)

    return pl.pallas_call(
        body, name="dt_bwd", grid=(M // tm,),
        in_specs=[_row(LANES, tm), _full((1, LANES)), _row(LANES, tm)],
        out_specs=[_row(LANES, tm), _full((1, LANES))],
        out_shape=[jax.ShapeDtypeStruct((M, LANES), f32), jax.ShapeDtypeStruct((1, LANES), f32)],
        compiler_params=_cparams(("arbitrary",)),
    )(dt_raw, bias, ddt)


SSM_GW = SSM_HPG * SSM_P
SSM_PAIRS = SSM_GW // LANES


def _ssd_common(dte_ref, dtt_ref, ae_ref, acol_ref):
    Q = CHUNK
    r = lax.broadcasted_iota(jnp.int32, (Q, Q), 0)
    c = lax.broadcasted_iota(jnp.int32, (Q, Q), 1)
    causal = r >= c
    anti = c >= r
    tril = causal.astype(f32)
    triu = anti.astype(f32)
    dt_e = dte_ref[...]
    cs_e = _dot_hi(tril, dt_e * ae_ref[...])
    cst = _dot_hi(dtt_ref[...] * acol_ref[...], triu)
    cs_last = cs_e[Q - 1:Q, :]
    return causal, anti, triu, dt_e, cs_e, cst, jnp.exp(cs_e), jnp.exp(cs_last - cs_e), jnp.exp(cs_last)


def _half_masks():
    lane = lax.broadcasted_iota(jnp.int32, (CHUNK, LANES), 1)
    lo = lane < SSM_P
    return lo, jnp.logical_not(lo)


def _ssd_fwd(xbc_c, dt_e, dtt, a_e, a_col):
    M = xbc_c.shape[0]
    Q = CHUNK
    nch = M // Q

    def body(x_ref, dte_ref, dtt_ref, ae_ref, acol_ref, y_ref, hin_ref, ht_sc):
        @pl.when(pl.program_id(0) == 0)
        def _():
            ht_sc[...] = jnp.zeros_like(ht_sc)

        causal, _, _, dt_e, cs_e, cst, ecs_e, dte_e, elast_e = _ssd_common(dte_ref, dtt_ref, ae_ref, acol_ref)
        halves = _half_masks()
        for g in range(SSM_GROUPS):
            g0 = g * SSM_GW
            bg = x_ref[:, D_SSM + g * SSM_N:D_SSM + (g + 1) * SSM_N]
            cg = x_ref[:, D_SSM + D_BC + g * SSM_N:D_SSM + D_BC + (g + 1) * SSM_N]
            bg_b = bg.astype(bf16)
            cg_b = cg.astype(bf16)
            cb = _dot_nt(cg_b, bg_b)
            bgt_b = bg.T.astype(bf16)
            xdt_g = x_ref[:, g0:g0 + SSM_GW] * dt_e[:, g0:g0 + SSM_GW]
            ht = ht_sc[g]
            hin_ref[0, g] = ht
            y_off = _dot(cg_b, ht.astype(bf16)) * ecs_e[:, g0:g0 + SSM_GW]
            for pr in range(SSM_PAIRS):
                p0 = pr * LANES
                xdt_p = xdt_g[:, p0:p0 + LANES]
                acc = y_off[:, p0:p0 + LANES]
                for half in range(2):
                    h = g * SSM_HPG + pr * 2 + half
                    seg = cs_e[:, h * SSM_P:h * SSM_P + 1] - cst[h:h + 1, :]
                    lm = jnp.exp(jnp.where(causal, seg, -jnp.inf))
                    xm = jnp.where(halves[half], xdt_p, 0.0).astype(bf16)
                    acc = acc + _dot((cb * lm).astype(bf16), xm)
                y_ref[:, g0 + p0:g0 + p0 + LANES] = acc
            st = _dot(bgt_b, (xdt_g * dte_e[:, g0:g0 + SSM_GW]).astype(bf16))
            ht_sc[g] = ht * elast_e[:, g0:g0 + SSM_GW] + st

    return pl.pallas_call(
        body, name="ssd_fwd", grid=(nch,),
        in_specs=[pl.BlockSpec((Q, D_XBC), lambda c: (c, 0)),
                  pl.BlockSpec((Q, D_SSM), lambda c: (c, 0)),
                  pl.BlockSpec((SSM_HEADS, Q), lambda c: (0, c)),
                  _full((1, D_SSM)), _full((SSM_HEADS, LANES))],
        out_specs=[pl.BlockSpec((Q, D_SSM), lambda c: (c, 0)),
                   pl.BlockSpec((1, SSM_GROUPS, SSM_N, SSM_GW), lambda c: (c, 0, 0, 0))],
        out_shape=[jax.ShapeDtypeStruct((M, D_SSM), f32),
                   jax.ShapeDtypeStruct((nch, SSM_GROUPS, SSM_N, SSM_GW), f32)],
        scratch_shapes=[pltpu.VMEM((SSM_GROUPS, SSM_N, SSM_GW), f32)],
        compiler_params=_cparams(("arbitrary",)),
    )(xbc_c, dt_e, dtt, a_e, a_col)


def _ssd_bwd(xbc_c, dtp, dt_e, dtt, a_row, a_e, a_col, hin, dy, d_exp, head_ind):
    M = xbc_c.shape[0]
    Q = CHUNK
    nch = M // Q
    rev = lambda c: nch - 1 - c

    def body(x_ref, dtp_ref, dte_ref, dtt_ref, arow_ref, ae_ref, acol_ref, hin_ref, dy_ref, dexp_ref,
             ind_ref, dx_ref, ddt_ref, da_ref, dht_sc, z_sc, z1_sc, last_sc, ct_sc):
        @pl.when(pl.program_id(0) == 0)
        def _():
            dht_sc[...] = jnp.zeros_like(dht_sc)
            da_ref[...] = jnp.zeros_like(da_ref)
            last_sc[...] = jnp.zeros_like(last_sc)
            ct_sc[...] = jnp.zeros_like(ct_sc)

        causal, anti, triu, dt_e, cs_e, cst, ecs_e, dte_e, elast_e = _ssd_common(dte_ref, dtt_ref, ae_ref, acol_ref)
        halves = _half_masks()
        lane = lax.broadcasted_iota(jnp.int32, (Q, LANES), 1)
        rsum = jnp.zeros((Q, LANES), f32)
        for g in range(SSM_GROUPS):
            g0 = g * SSM_GW
            gs = slice(g0, g0 + SSM_GW)
            b0 = D_SSM + g * SSM_N
            c0 = D_SSM + D_BC + g * SSM_N
            bg = x_ref[:, b0:b0 + SSM_N]
            cg = x_ref[:, c0:c0 + SSM_N]
            bg_b = bg.astype(bf16)
            cg_b = cg.astype(bf16)
            cgt_b = cg.T.astype(bf16)
            cbt = _dot_nt(bg_b, cg_b)
            cb = _dot_nt(cg_b, bg_b)
            x_g = x_ref[:, gs]
            dt_g = dt_e[:, gs]
            xdt_g = x_g * dt_g
            dy_g = dy_ref[:, gs]
            ht = hin_ref[0, g]
            ht_b = ht.astype(bf16)
            dht = dht_sc[g]
            dht_b = dht.astype(bf16)
            dye_b = (dy_g * ecs_e[:, gs]).astype(bf16)
            dc = _dot_nt(dye_b, ht_b)
            dht_new = dht * elast_e[:, gs] + _dot(cgt_b, dye_b)
            e = _dot(bg_b, dht_b)
            xdtd = xdt_g * dte_e[:, gs]
            db = _dot_nt(xdtd.astype(bf16), dht_b)
            dxdt_state = e * dte_e[:, gs]
            exd = e * xdtd
            z1_sc[:, gs] = dy_g * (_dot(cg_b, ht_b) * ecs_e[:, gs]) - exd
            last_sc[0:1, gs] = (jnp.sum(exd, axis=0, keepdims=True)
                                + jnp.sum(dht * ht, axis=0, keepdims=True) * elast_e[:, gs])
            dg_acc = jnp.zeros((Q, Q), f32)
            for pr in range(SSM_PAIRS):
                p0 = pr * LANES
                ps = slice(g0 + p0, g0 + p0 + LANES)
                dy_p = dy_g[:, p0:p0 + LANES]
                xdt_pb = xdt_g[:, p0:p0 + LANES].astype(bf16)
                acc = dxdt_state[:, p0:p0 + LANES]
                for half in range(2):
                    h = g * SSM_HPG + pr * 2 + half
                    seg = cs_e[:, h * SSM_P:h * SSM_P + 1] - cst[h:h + 1, :]
                    lm = jnp.exp(jnp.where(causal, seg, -jnp.inf))
                    lmt = jnp.exp(jnp.where(anti, -seg, -jnp.inf))
                    dym = jnp.where(halves[half], dy_p, 0.0).astype(bf16)
                    acc = acc + _dot((cbt * lmt).astype(bf16), dym)
                    dml = _dot_nt(dym, xdt_pb) * lm
                    dg_acc = dg_acc + dml
                    w = dml * cb
                    rsum = rsum + jnp.where(lane == h, jnp.sum(w, axis=1, keepdims=True), 0.0)
                    ct_sc[h:h + 1, :] = jnp.sum(w, axis=0, keepdims=True)
                dx_ref[:, ps] = acc * dt_g[:, p0:p0 + LANES] + dexp_ref[:, ps] * dy_p
                z_sc[:, ps] = acc * x_g[:, p0:p0 + LANES]
            dg_b = dg_acc.astype(bf16)
            dx_ref[:, c0:c0 + SSM_N] = dc + _dot(dg_b, bg_b)
            dx_ref[:, b0:b0 + SSM_N] = db + _dot_tn(dg_b, cg_b)
            dht_sc[g] = dht_new
        s1 = _dot_hi(z1_sc[...], ind_ref[...])
        s2 = _dot_hi(z_sc[...], ind_ref[...])
        last = _dot_hi(last_sc[...], ind_ref[...])[0:1, :]
        dtp = dtp_ref[...]
        row = lax.broadcasted_iota(jnp.int32, (Q, LANES), 0)
        dcs = s1 + rsum + jnp.where(row == Q - 1, last, 0.0)
        tril = causal.astype(f32)
        da = _dot_hi(triu, dcs) - _dot_hi(ct_sc[...], tril).T
        ddt_ref[...] = s2 + da * arow_ref[...]
        da_ref[...] += jnp.sum(da * dtp, axis=0, keepdims=True)

    return pl.pallas_call(
        body, name="ssd_bwd", grid=(nch,),
        in_specs=[pl.BlockSpec((Q, D_XBC), lambda c: (rev(c), 0)),
                  pl.BlockSpec((Q, LANES), lambda c: (rev(c), 0)),
                  pl.BlockSpec((Q, D_SSM), lambda c: (rev(c), 0)),
                  pl.BlockSpec((SSM_HEADS, Q), lambda c: (0, rev(c))),
                  _full((1, LANES)), _full((1, D_SSM)), _full((SSM_HEADS, LANES)),
                  pl.BlockSpec((1, SSM_GROUPS, SSM_N, SSM_GW), lambda c: (rev(c), 0, 0, 0)),
                  pl.BlockSpec((Q, D_SSM), lambda c: (rev(c), 0)),
                  _full((1, D_SSM)), _full((D_SSM, LANES))],
        out_specs=[pl.BlockSpec((Q, D_XBC), lambda c: (rev(c), 0)),
                   pl.BlockSpec((Q, LANES), lambda c: (rev(c), 0)),
                   _full((1, LANES))],
        out_shape=[jax.ShapeDtypeStruct((M, D_XBC), f32), jax.ShapeDtypeStruct((M, LANES), f32),
                   jax.ShapeDtypeStruct((1, LANES), f32)],
        scratch_shapes=[pltpu.VMEM((SSM_GROUPS, SSM_N, SSM_GW), f32), pltpu.VMEM((Q, D_SSM), f32),
                        pltpu.VMEM((Q, D_SSM), f32), pltpu.VMEM((SUBLANES, D_SSM), f32),
                        pltpu.VMEM((LANES, Q), f32)],
        compiler_params=_cparams(("arbitrary",)),
    )(xbc_c, dtp, dt_e, dtt, a_row, a_e, a_col, hin, dy, d_exp, head_ind)


def _gate_norm_fwd(y, xbc_c, z, d_exp, g):
    M = y.shape[0]
    tm = _rt(M)
    gw = D_SSM // SSM_GROUPS

    def body(y_ref, x_ref, z_ref, d_ref, g_ref, o_ref):
        yg = (y_ref[...] + d_ref[...] * x_ref[...]) * _silu(z_ref[...])
        for gi in range(SSM_GROUPS):
            blk = yg[:, gi * gw:(gi + 1) * gw]
            o_ref[:, gi * gw:(gi + 1) * gw] = (blk * _rstd(blk) * g_ref[:, gi * gw:(gi + 1) * gw]).astype(bf16)

    return pl.pallas_call(
        body, name="gate_norm_fwd", grid=(M // tm,),
        in_specs=[_row(D_SSM, tm), _row(D_SSM, tm), _row(D_SSM, tm), _full((1, D_SSM)), _full((1, D_SSM))],
        out_specs=_row(D_SSM, tm), out_shape=jax.ShapeDtypeStruct((M, D_SSM), bf16),
        compiler_params=_cparams(("parallel",)),
    )(y, xbc_c, z, d_exp, g)


def _gate_norm_bwd(y, xbc_c, z, d_exp, g, dout, head_ind):
    M = y.shape[0]
    tm = _rt(M)
    nt = M // tm
    gw = D_SSM // SSM_GROUPS

    def body(y_ref, x_ref, z_ref, d_ref, g_ref, do_ref, ind_ref, dy_ref, dz_ref, dg_ref, dd_ref, ddc_sc):
        i = pl.program_id(0)

        @pl.when(i == 0)
        def _():
            dg_ref[...] = jnp.zeros_like(dg_ref)
            ddc_sc[...] = jnp.zeros_like(ddc_sc)

        zv = z_ref[...]
        xv = x_ref[...]
        s = _silu(zv)
        yd = y_ref[...] + d_ref[...] * xv
        yg = yd * s
        dov = do_ref[...]
        for gi in range(SSM_GROUPS):
            sl = slice(gi * gw, (gi + 1) * gw)
            dyg, dgp = _rms_bwd_math(yg[:, sl], g_ref[:, sl], dov[:, sl])
            dg_ref[:, sl] += jnp.sum(dgp, axis=0, keepdims=True)
            dyd = dyg * s[:, sl]
            dy_ref[:, sl] = dyd
            dz_ref[:, sl] = (dyg * yd[:, sl] * _dsilu(zv[:, sl])).astype(bf16)
            ddc_sc[:, sl] += jnp.sum(dyd * xv[:, sl], axis=0, keepdims=True)

        @pl.when(i == nt - 1)
        def _():
            dd_ref[...] = _dot_hi(ddc_sc[...], ind_ref[...])

    return pl.pallas_call(
        body, name="gate_norm_bwd", grid=(nt,),
        in_specs=[_row(D_SSM, tm), _row(D_SSM, tm), _row(D_SSM, tm), _full((1, D_SSM)), _full((1, D_SSM)),
                  _row(D_SSM, tm), _full((D_SSM, LANES))],
        out_specs=[_row(D_SSM, tm), _row(D_SSM, tm), _full((1, D_SSM)), _full((1, LANES))],
        out_shape=[jax.ShapeDtypeStruct((M, D_SSM), f32), jax.ShapeDtypeStruct((M, D_SSM), bf16),
                   jax.ShapeDtypeStruct((1, D_SSM), f32), jax.ShapeDtypeStruct((1, LANES), f32)],
        scratch_shapes=[pltpu.VMEM((1, D_SSM), f32)],
        compiler_params=_cparams(("arbitrary",)),
    )(y, xbc_c, z, d_exp, g, dout, head_ind)


_PEER_FLIPS = [(0, 0, 1), (0, 1, 0), (0, 1, 1), (1, 0, 0), (1, 0, 1), (1, 1, 0), (1, 1, 1)]


def _exchange_copies(ins, outs, send_sems, recv_sems, loc_sems, scatter):
    n = len(ins)
    x, y, c = lax.axis_index("x"), lax.axis_index("y"), lax.axis_index("c")
    me = 4 * x + 2 * y + c
    copies = []
    for a in range(n):
        src = ins[a].at[me] if scatter else ins[a]
        copies.append(pltpu.make_async_copy(src, outs[a].at[me], loc_sems.at[a]))
    for p, (fx, fy, fc) in enumerate(_PEER_FLIPS):
        tx = 1 - x if fx else x
        ty = 1 - y if fy else y
        tc = 1 - c if fc else c
        tgt = 4 * tx + 2 * ty + tc
        for a in range(n):
            src = ins[a].at[tgt] if scatter else ins[a]
            copies.append(pltpu.make_async_remote_copy(
                src_ref=src, dst_ref=outs[a].at[me],
                send_sem=send_sems.at[p * n + a], recv_sem=recv_sems.at[p * n + a],
                device_id=(tx, ty, tc), device_id_type=_MESH))
    return copies


def _exchange_shapes(arrays, scatter):
    return [jax.ShapeDtypeStruct(a.shape if scatter else (N_DEV,) + a.shape, a.dtype) for a in arrays]


def _exchange_sems(n):
    return [pltpu.SemaphoreType.DMA((7 * n,)), pltpu.SemaphoreType.DMA((7 * n,)), pltpu.SemaphoreType.DMA((n,))]


def _exchange(arrays, scatter, name):
    n = len(arrays)

    def body(*refs):
        copies = _exchange_copies(refs[:n], refs[n:2 * n], *refs[2 * n:], scatter)
        for cp in copies:
            cp.start()
        for cp in copies:
            cp.wait()

    any_spec = pl.BlockSpec(memory_space=pl.ANY)
    return pl.pallas_call(
        body, name=name, in_specs=[any_spec] * n, out_specs=[any_spec] * n,
        out_shape=_exchange_shapes(arrays, scatter), scratch_shapes=_exchange_sems(n),
    )(*arrays)


def _gather_two_level(arrays, name):
    n = len(arrays)

    def body(*refs):
        ins, outs = refs[:n], refs[n:2 * n]
        send_sems, recv_sems, loc_sems = refs[2 * n:]
        x, y, c = lax.axis_index("x"), lax.axis_index("y"), lax.axis_index("c")
        me, sibling = (x, y, c), (x, y, 1 - c)
        chips = [(1 - x, y), (x, 1 - y), (1 - x, 1 - y)]

        def slot(a, dev):
            return outs[a].at[4 * dev[0] + 2 * dev[1] + dev[2]]

        def copy(a, k, block, to, src=None):
            return pltpu.make_async_remote_copy(
                src_ref=slot(a, block) if src is None else src, dst_ref=slot(a, block),
                send_sem=send_sems.at[7 * a + k], recv_sem=recv_sems.at[7 * a + k],
                device_id=to, device_id_type=_MESH)

        mine = [pltpu.make_async_copy(ins[a], slot(a, me), loc_sems.at[a]) for a in range(n)]
        first = []
        for a in range(n):
            first.append(copy(a, 0, me, sibling, src=ins[a]))
            first += [copy(a, 1 + j, me, (*chip, c), src=ins[a]) for j, chip in enumerate(chips)]
        for cp in mine + first:
            cp.start()
        passed = []
        for j, chip in enumerate(chips):
            for a in range(n):
                copy(a, 1 + j, (*chip, c), me).wait_recv()
                cp = copy(a, 4 + j, (*chip, c), sibling)
                cp.start()
                passed.append(cp)
        for a in range(n):
            copy(a, 0, sibling, me).wait_recv()
            for j, chip in enumerate(chips):
                copy(a, 4 + j, (*chip, 1 - c), me).wait_recv()
        for cp in first + passed:
            cp.wait_send()
        for cp in mine:
            cp.wait()

    any_spec = pl.BlockSpec(memory_space=pl.ANY)
    return pl.pallas_call(
        body, name=name, in_specs=[any_spec] * n, out_specs=[any_spec] * n,
        out_shape=_exchange_shapes(arrays, False), scratch_shapes=_exchange_sems(n),
    )(*arrays)


def _exchange_tail(scattered, gathered, name):
    ns, ng = len(scattered), len(gathered)
    n = ns + ng

    def body(*refs):
        sems = refs[2 * n:]
        copies = (_exchange_copies(refs[:ns], refs[n:n + ns], *sems[:3], True)
                  + _exchange_copies(refs[ns:n], refs[n + ns:2 * n], *sems[3:], False))
        for cp in copies:
            cp.start()
        for cp in copies:
            cp.wait()

    any_spec = pl.BlockSpec(memory_space=pl.ANY)
    return pl.pallas_call(
        body, name=name, in_specs=[any_spec] * n, out_specs=[any_spec] * n,
        out_shape=_exchange_shapes(scattered, True) + _exchange_shapes(gathered, False),
        scratch_shapes=_exchange_sems(ns) + _exchange_sems(ng),
    )(*scattered, *gathered)


def _adamw_math(g, w, m, v):
    c1 = 1.0 - ADAM_B1 ** ADAM_STEP
    c2 = 1.0 - ADAM_B2 ** ADAM_STEP
    mn = ADAM_B1 * m + (1.0 - ADAM_B1) * g
    vn = ADAM_B2 * v + (1.0 - ADAM_B2) * (g * g)
    m_hat = mn / c1
    v_hat = vn / c2
    return -ADAM_LR * (m_hat / (jnp.sqrt(v_hat) + ADAM_EPS) + ADAM_WD * w), mn, vn


def _adamw(parts, w, m, v, name):
    R, C = w.shape
    tr = _pick(R, (PACK_ROW_TILE, 64, 32, 16, 8))

    def body(p_ref, w_ref, m_ref, v_ref, g_ref, d_ref, nm_ref, nv_ref):
        g = p_ref[0].astype(f32)
        for s in range(1, N_DEV):
            g = g + p_ref[s].astype(f32)
        g_ref[...] = g
        d_ref[...], nm_ref[...], nv_ref[...] = _adamw_math(g, w_ref[...], m_ref[...], v_ref[...])

    spec = pl.BlockSpec((tr, C), lambda i: (i, 0))
    return pl.pallas_call(
        body, name=name, grid=(R // tr,),
        in_specs=[pl.BlockSpec((N_DEV, tr, C), lambda i: (0, i, 0)), spec, spec, spec],
        out_specs=[spec] * 4, out_shape=[jax.ShapeDtypeStruct((R, C), f32)] * 4,
        compiler_params=_cparams(("parallel",)),
    )(parts, w, m, v)


def _adamw_replicated(parts, ws, ms, vs):
    n = len(ws)
    R = parts.shape[1]
    sizes = [int(w.shape[1]) for w in ws]

    def body(*refs):
        p_ref = refs[0]
        w_refs, m_refs, v_refs = refs[1:1 + n], refs[1 + n:1 + 2 * n], refs[1 + 2 * n:1 + 3 * n]
        loss_ref = refs[1 + 3 * n]
        outs = refs[2 + 3 * n:]
        g_all = p_ref[0]
        for s in range(1, N_DEV):
            g_all = g_all + p_ref[s]
        row = 0
        for p in range(n):
            pieces, left = [], sizes[p]
            while left > 0:
                take = min(left, PACK_W)
                pieces.append(g_all[row:row + 1, 0:take])
                left -= take
                row += 1
            g = pieces[0] if len(pieces) == 1 else jnp.concatenate(pieces, axis=1)
            d, mn, vn = _adamw_math(g, w_refs[p][...], m_refs[p][...], v_refs[p][...])
            outs[4 * p][...] = g
            outs[4 * p + 1][...] = d
            outs[4 * p + 2][...] = mn
            outs[4 * p + 3][...] = vn
        loss_ref[...] = g_all[row:row + 1, 0:LANES]

    in_specs = [_full((N_DEV, R, PACK_W))] + [_full((1, s)) for s in sizes] * 3
    out_specs = [_full((1, LANES))]
    out_shape = [jax.ShapeDtypeStruct((1, LANES), f32)]
    for s in sizes:
        out_specs += [_full((1, s))] * 4
        out_shape += [jax.ShapeDtypeStruct((1, s), f32)] * 4
    res = pl.pallas_call(
        body, name="adamw_replicated", in_specs=in_specs, out_specs=out_specs, out_shape=out_shape,
        compiler_params=pltpu.CompilerParams(vmem_limit_bytes=VMEM_LIMIT),
    )(parts, *ws, *ms, *vs)
    return res[0], [res[1 + 4 * p:5 + 4 * p] for p in range(n)]


def _flat_rows(a, lead_ndim):
    lead = a.shape[:lead_ndim]
    n = int(np.prod(a.shape[lead_ndim:]))
    a = a.reshape(lead + (n,))
    pad = (-n) % PACK_W
    if pad:
        a = jnp.pad(a, [(0, 0)] * lead_ndim + [(0, pad)])
    return a.reshape(lead + ((n + pad) // PACK_W, PACK_W))


def _pack(arrays, lead_ndim, total_rows, dtype):
    rows = [_flat_rows(a.astype(dtype), lead_ndim) for a in arrays]
    cat = jnp.concatenate(rows, axis=lead_ndim)
    pad = total_rows - cat.shape[lead_ndim]
    if pad:
        cat = jnp.pad(cat, [(0, 0)] * lead_ndim + [(0, pad), (0, 0)])
    return cat


def _unpack(buf, shapes, lead_ndim):
    out = []
    r = 0
    lead = buf.shape[:lead_ndim]
    for shp in shapes:
        n = int(np.prod(shp))
        nr = -(-n // PACK_W)
        piece = lax.slice_in_dim(buf, r, r + nr, axis=lead_ndim)
        piece = piece.reshape(lead + (nr * PACK_W,))
        piece = lax.slice_in_dim(piece, 0, n, axis=lead_ndim)
        out.append(piece.reshape(lead + tuple(shp)))
        r += nr
    return out


def _round_up(n, m):
    return -(-n // m) * m


def kernel(x, meta_tokens, norm_mix_pre, norm_mix_post, norm_ffn_pre, norm_ffn_post, w_in, q_a_norm, w_uq, kv_a_norm, w_ukv, attn_out_norm, ssm_conv_w, ssm_conv_b, ssm_dt_bias, ssm_A_log, ssm_D, ssm_norm, w_out, w_up, ffn_conv_w, ffn_conv_b, w_down, loss_target, m_meta_tokens, m_norm_mix_pre, m_norm_mix_post, m_norm_ffn_pre, m_norm_ffn_post, m_w_in, m_q_a_norm, m_w_uq, m_kv_a_norm, m_w_ukv, m_attn_out_norm, m_ssm_conv_w, m_ssm_conv_b, m_ssm_dt_bias, m_ssm_A_log, m_ssm_D, m_ssm_norm, m_w_out, m_w_up, m_ffn_conv_w, m_ffn_conv_b, m_w_down, v_meta_tokens, v_norm_mix_pre, v_norm_mix_post, v_norm_ffn_pre, v_norm_ffn_post, v_w_in, v_q_a_norm, v_w_uq, v_kv_a_norm, v_w_ukv, v_attn_out_norm, v_ssm_conv_w, v_ssm_conv_b, v_ssm_dt_bias, v_ssm_A_log, v_ssm_D, v_ssm_norm, v_w_out, v_w_up, v_ffn_conv_w, v_ffn_conv_b, v_w_down):
    seq = x.shape[1]
    n_real = N_META + seq
    Lp = _round_up(n_real, 768) if n_real > 2048 else _round_up(n_real, ROW_TILE)
    D = D_MODEL

    early_w = [w_uq, w_ukv]
    late_w = [w_out, w_down]
    sharded_s = [meta_tokens, ssm_conv_w, ffn_conv_w]
    grp_a = dict(names=["w_out", "w_down"], w=late_w, m=[m_w_out, m_w_down],
                 v=[v_w_out, v_w_down])
    grp_b = dict(names=["w_uq", "w_ukv", "ssm_conv_w", "ffn_conv_w"],
                 w=early_w + [ssm_conv_w, ffn_conv_w],
                 m=[m_w_uq, m_w_ukv, m_ssm_conv_w, m_ffn_conv_w],
                 v=[v_w_uq, v_w_ukv, v_ssm_conv_w, v_ffn_conv_w])
    grp_meta = dict(names=["meta_tokens"], w=[meta_tokens], m=[m_meta_tokens], v=[v_meta_tokens])
    repl_w = [norm_mix_pre, norm_mix_post, norm_ffn_pre, norm_ffn_post, q_a_norm, kv_a_norm, attn_out_norm,
              ssm_conv_b, ssm_dt_bias, ssm_A_log, ssm_D, ssm_norm, ffn_conv_b]
    repl_m = [m_norm_mix_pre, m_norm_mix_post, m_norm_ffn_pre, m_norm_ffn_post, m_q_a_norm, m_kv_a_norm,
              m_attn_out_norm, m_ssm_conv_b, m_ssm_dt_bias, m_ssm_A_log, m_ssm_D, m_ssm_norm, m_ffn_conv_b]
    repl_v = [v_norm_mix_pre, v_norm_mix_post, v_norm_ffn_pre, v_norm_ffn_post, v_q_a_norm, v_kv_a_norm,
              v_attn_out_norm, v_ssm_conv_b, v_ssm_dt_bias, v_ssm_A_log, v_ssm_D, v_ssm_norm, v_ffn_conv_b]

    def pack_rows(arrs, lead):
        return _round_up(sum(-(-int(np.prod(a.shape[lead:])) // PACK_W) for a in arrs), 16)

    wb = _pack(early_w, 0, pack_rows(early_w, 0), bf16)
    wl = _pack(late_w, 0, pack_rows(late_w, 0), bf16)
    ws = _pack(sharded_s, 0, pack_rows(sharded_s, 0), f32)
    wb_all, ws_all, win_all = _gather_two_level([wb, ws, w_in[0].astype(bf16)], "gather_weights")
    g_w_uq, g_w_ukv = _unpack(wb_all, [a.shape for a in early_w], 1)
    g_meta, g_sconv, g_fconv = _unpack(ws_all, [a.shape for a in sharded_s], 1)

    def cols(gathered):
        t = gathered[:, 0]
        return jnp.transpose(t, (1, 0, 2)).reshape(t.shape[1], N_DEV * t.shape[2])

    win = cols(win_all[:, None])
    o = np.cumsum((0, Q_RANK, KV_RANK, QK_ROPE, D_SSM, D_XBC, SSM_HEADS))
    w_q, w_kv = win[:, o[0]:o[1]], win[:, o[1]:o[2]]
    w_rope = jnp.pad(win[:, o[2]:o[3]], ((0, 0), (0, LANES - QK_ROPE)))
    w_z, w_xbc = win[:, o[3]:o[4]], win[:, o[4]:o[5]]
    w_dt = jnp.pad(win[:, o[5]:o[6]], ((0, 0), (0, LANES - SSM_HEADS)))
    wuq = g_w_uq.reshape(Q_RANK, MLA_HEADS, QK_NOPE + QK_ROPE)
    wuq = jnp.pad(wuq, ((0, 0), (0, 0), (0, QK_PAD - QK_NOPE - QK_ROPE))).reshape(Q_RANK, MLA_HEADS * QK_PAD)
    wukv = g_w_ukv.reshape(KV_RANK, MLA_HEADS * (QK_NOPE + V_DIM))
    meta_full = jnp.transpose(g_meta, (1, 0, 2)).reshape(N_META, D)
    sconv_w = jnp.pad(cols(g_sconv), ((0, SUBLANES - SSM_CONV), (0, 0)))
    fconv_w = jnp.pad(cols(g_fconv), ((0, SUBLANES - FFN_CONV), (0, 0)))

    pos = jnp.arange(Lp, dtype=f32)
    inv = ROPE_THETA ** (-jnp.arange(0, QK_ROPE, 2, dtype=f32) / QK_ROPE)
    ang = pos[:, None] * inv[None, :]
    cs_, sn_ = jnp.cos(ang), jnp.sin(ang)
    zpad = jnp.zeros((Lp, LANES - QK_ROPE), f32)
    cos_t = jnp.concatenate([cs_, cs_, zpad], axis=1)
    sin_t = jnp.concatenate([-sn_, sn_, zpad], axis=1)
    dt_bias_p = jnp.pad(ssm_dt_bias, ((0, 0), (0, LANES - SSM_HEADS)))
    a_neg = -jnp.exp(ssm_A_log)
    a_row = jnp.pad(a_neg, ((0, 0), (0, LANES - SSM_HEADS)))
    a_col = jnp.broadcast_to(a_neg.reshape(SSM_HEADS, 1), (SSM_HEADS, LANES))
    d_exp = jnp.repeat(ssm_D, SSM_P, axis=1)
    a_e = jnp.repeat(a_neg, SSM_P, axis=1)
    head_ind = (jnp.arange(D_SSM)[:, None] // SSM_P == jnp.arange(LANES)[None, :]).astype(f32)

    xb = x[0]
    h0 = jnp.concatenate([meta_full, xb, jnp.zeros((Lp - n_real, D), f32)], axis=0)
    tgt = jnp.pad(loss_target[0], ((N_META, Lp - n_real), (0, 0)))
    hn1, q_c, kv_c, kpe_raw, z, xbc, dt_raw = _in_proj(h0, norm_mix_pre, [w_q, w_kv, w_rope, w_z, w_xbc, w_dt])

    qn, qh = _up_q_rope(q_c, q_a_norm, wuq, cos_t, sin_t)
    kvn, kh, vh = _up_kv_rope(kv_c, kv_a_norm, wukv, kpe_raw, cos_t, sin_t)
    attn, lse, wl_all, wup_all = _flash_fwd(qh, kh, vh, [wl, w_up[0].astype(bf16)], False)
    g_w_out, g_w_down = _unpack(wl_all, [a.shape for a in late_w], 1)
    wout = g_w_out.reshape(D_ATTN + D_SSM, D)
    wout_a, wout_s = wout[:D_ATTN], wout[D_ATTN:]
    wup = cols(wup_all[:, None])
    wdown = g_w_down.reshape(D_FF, D)
    an = _rms_fwd(attn, attn_out_norm, bf16, "norm_attn_out")

    xbc_c = _ssm_conv_fwd(xbc, sconv_w, ssm_conv_b)
    dtp, dt_e = _dt_fwd(dt_raw, dt_bias_p, jnp.transpose(head_ind))
    dtt = jnp.transpose(dtp[:, :SSM_HEADS])
    y_ssd, hin = _ssd_fwd(xbc_c, dt_e, dtt, a_e, a_col)
    ssm = _gate_norm_fwd(y_ssd, xbc_c, z, d_exp, ssm_norm)

    mix = _mm([(an, wout_a), (ssm, wout_s)], f32, False, "out_proj")
    h1, hn2 = _resid_norm(h0, mix, norm_mix_post, norm_ffn_pre)
    up = _mm([(hn2, wup)], f32, False, "ffn_up")
    act = _ffn_gate_fwd(up, fconv_w, ffn_conv_b)
    down = _mm([(act, wdown)], f32, False, "ffn_down")
    dh2, d_down, dg_ffn_post, loss_part = _final(h1, down, norm_ffn_post, tgt, n_real)

    d_act = _mm([(d_down, wdown)], f32, True, "ffn_down_dx")
    dw_down = _mm_tn(act, d_down, "ffn_down_dw")
    dup_g, dup_v, dwc_g, dwc_v, dbc_g, dbc_v = _ffn_gate_bwd(up, fconv_w, ffn_conv_b, d_act)
    d_hn2 = _mm([(dup_g, wup[:, :D_FF]), (dup_v, wup[:, D_FF:])], f32, True, "ffn_up_dx")
    dw_up = jnp.concatenate([_mm_tn(hn2, dup_g, "ffn_up_dw_g"), _mm_tn(hn2, dup_v, "ffn_up_dw_v")], axis=1)
    dh1, d_mix, dg_ffn_pre, dg_mix_post = _mid_bwd(h1, norm_ffn_pre, d_hn2, dh2, mix, norm_mix_post)
    d_an = _mm([(d_mix, wout_a)], f32, True, "out_proj_dx_a")
    d_ssm = _mm([(d_mix, wout_s)], f32, True, "out_proj_dx_s")
    dw_out = jnp.concatenate([_mm_tn(an, d_mix, "out_proj_dw_a"), _mm_tn(ssm, d_mix, "out_proj_dw_s")], axis=0)

    do_h, delta, dg_attn_out = _attn_out_bwd(attn, attn_out_norm, d_an)
    def col_blocks(gm):
        r, cc = gm.shape
        return jnp.transpose(gm.reshape(r, N_DEV, cc // N_DEV), (1, 0, 2))

    blocks_a = [dw_out.reshape(N_DEV, (D_ATTN + D_SSM) // N_DEV, D), dw_down.reshape(N_DEV, D_FF // N_DEV, D)]
    gpack_a = _pack(blocks_a, 1, pack_rows(blocks_a, 1), bf16)
    dqh, dkh, dvh, gparts_a, gparts_up = _flash_bwd(qh, kh, vh, do_h, lse, delta,
                                                    [gpack_a, col_blocks(dw_up).astype(bf16)], True)
    d_q_c, dg_q, dw_uq = _q_branch_bwd(dqh, cos_t, sin_t, wuq, q_c, q_a_norm, qn)
    d_kv_c, dg_kv, dw_ukv, d_kpe_raw = _kv_branch_bwd(dkh, dvh, cos_t, sin_t, wukv, kv_c, kv_a_norm, kvn)

    dy_ssd, dz, dg_ssm, dd_heads = _gate_norm_bwd(y_ssd, xbc_c, z, d_exp, ssm_norm, d_ssm, head_ind)
    d_xbc_c, ddt, da_heads = _ssd_bwd(xbc_c, dtp, dt_e, dtt, a_row, a_e, a_col, hin, dy_ssd, d_exp, head_ind)
    d_xbc, dw_sconv, db_sconv = _ssm_conv_bwd(xbc, sconv_w, ssm_conv_b, d_xbc_c)
    d_dt_raw, d_dt_bias = _dt_bwd(dt_raw, dt_bias_p, ddt)

    dw_q, dw_kv, dw_rope, dw_z, dw_xbc, dw_dt = _in_proj_dw(hn1, [d_q_c, d_kv_c, d_kpe_raw, dz, d_xbc, d_dt_raw])
    dw_in = jnp.concatenate([dw_q, dw_kv, dw_rope[:, :QK_ROPE], dw_z, dw_xbc, dw_dt[:, :SSM_HEADS]], axis=1)
    dw_uq3 = dw_uq.reshape(Q_RANK, MLA_HEADS, QK_PAD)[:, :, :QK_NOPE + QK_ROPE]
    blocks_b = [
        dw_uq3.reshape(N_DEV, Q_RANK // N_DEV, MLA_HEADS, QK_NOPE + QK_ROPE),
        dw_ukv.reshape(N_DEV, KV_RANK // N_DEV, MLA_HEADS, QK_NOPE + V_DIM),
        col_blocks(dw_sconv[:SSM_CONV]),
        col_blocks(jnp.concatenate([dwc_g, dwc_v], axis=1)[:FFN_CONV]),
    ]
    gpack_b = _pack(blocks_b, 1, pack_rows(blocks_b, 1), bf16)
    segs = [(d_q_c, w_q), (d_kv_c, w_kv), (d_kpe_raw, w_rope), (dz, w_z), (d_xbc, w_xbc), (d_dt_raw, w_dt)]
    d_hn1, gparts_b, gparts_in = _mm(segs, f32, True, "proj_dx",
                                     carried=[gpack_b, col_blocks(dw_in).astype(bf16)], scatter=True)
    dh0, dg_mix_pre = _rms_bwd(h0, norm_mix_pre, d_hn1, f32, "norm_mix_pre_bwd", residual=dh1)

    grad_x = dh0[N_META:n_real][None]
    meta_blocks = col_blocks(dh0[:N_META]).reshape(N_DEV, N_META * D // N_DEV // PACK_W, PACK_W)


    def adam_group(parts, grp, name):
        rows = parts.shape[1]
        packs = [_pack([a[None] for a in grp[k]], 1, rows, f32)[0] for k in ("w", "m", "v")]
        outs = _adamw(parts, *packs, name)
        shapes = [a.shape for a in grp["w"]]
        return [dict(zip(grp["names"], [t[0] for t in _unpack(b[None], shapes, 1)])) for b in outs]

    def adam_own_layout(parts, name, w, m, v):
        return [{name: t[None]} for t in _adamw(parts, w[0], m[0], v[0], "adamw_" + name)]

    sh_a = adam_group(gparts_a, grp_a, "adamw_sharded_a")
    sh_b = adam_group(gparts_b, grp_b, "adamw_sharded_b")
    sh_in = adam_own_layout(gparts_in, "w_in", w_in, m_w_in, v_w_in)
    sh_up = adam_own_layout(gparts_up, "w_up", w_up, m_w_up, v_w_up)

    dg_alog = da_heads[:, :SSM_HEADS] * a_neg
    repl_g = [dg_mix_pre, dg_mix_post, dg_ffn_pre, dg_ffn_post, dg_q, dg_kv, dg_attn_out, db_sconv,
              d_dt_bias[:, :SSM_HEADS], dg_alog, dd_heads[:, :SSM_HEADS], dg_ssm,
              jnp.concatenate([dbc_g, dbc_v], axis=1)]
    loss_vec = loss_part[:, :1]
    small_total = _round_up(sum(-(-int(np.prod(a.shape)) // PACK_W) for a in repl_g) + 1, 16)
    spack = _pack(repl_g + [loss_vec], 0, small_total, f32)
    gparts_meta, sparts = _exchange_tail([meta_blocks], [spack], "exchange_tail")
    sh_meta = adam_group(gparts_meta, grp_meta, "adamw_meta")
    loss_row, repl_out = _adamw_replicated(sparts, repl_w, repl_m, repl_v)
    loss = loss_row[0, 0]

    order = ["meta_tokens", "norm_mix_pre", "norm_mix_post", "norm_ffn_pre", "norm_ffn_post", "w_in", "q_a_norm",
             "w_uq", "kv_a_norm", "w_ukv", "attn_out_norm", "ssm_conv_w", "ssm_conv_b", "ssm_dt_bias", "ssm_A_log",
             "ssm_D", "ssm_norm", "w_out", "w_up", "ffn_conv_w", "ffn_conv_b", "w_down"]
    rp_names = ["norm_mix_pre", "norm_mix_post", "norm_ffn_pre", "norm_ffn_post", "q_a_norm", "kv_a_norm",
                "attn_out_norm", "ssm_conv_b", "ssm_dt_bias", "ssm_A_log", "ssm_D", "ssm_norm", "ffn_conv_b"]

    def lookup(k):
        d = {**sh_a[k], **sh_b[k], **sh_in[k], **sh_up[k], **sh_meta[k],
             **{n: four[k] for n, four in zip(rp_names, repl_out)}}
        return [d[n] for n in order]

    return (loss, grad_x, *lookup(0), *lookup(1), *lookup(2), *lookup(3))
```

```python
import functools
import math

import jax
import jax.numpy as jnp
import numpy as np
from jax import lax
from jax.experimental import pallas as pl
from jax.experimental.pallas import tpu as pltpu

f32 = jnp.float32
bf16 = jnp.bfloat16

D_MODEL = 1024
SEQ = 8192
N_META = 16
MLA_HEADS = 8
QK_NOPE = 128
QK_ROPE = 64
V_DIM = 128
Q_RANK = 384
KV_RANK = 256
ROPE_THETA = 10000.0
SOFTMAX_SCALE = (QK_NOPE + QK_ROPE) ** -0.5
D_ATTN = MLA_HEADS * V_DIM
SSM_HEADS = 16
SSM_P = 64
SSM_GROUPS = 2
SSM_HPG = SSM_HEADS // SSM_GROUPS
SSM_N = 128
SSM_CONV = 4
CHUNK = 128
D_SSM = SSM_HEADS * SSM_P
D_BC = SSM_GROUPS * SSM_N
D_XBC = D_SSM + 2 * D_BC
D_FF = 2816
FFN_CONV = 3
EPS = 1e-6
D_IN = Q_RANK + KV_RANK + QK_ROPE + D_SSM + D_XBC + SSM_HEADS
QK_PAD = 256
N_DEV = 8

ADAM_LR = 0.001
ADAM_B1 = 0.9
ADAM_B2 = 0.999
ADAM_EPS = 1e-08
ADAM_WD = 0.01
ADAM_STEP = 10

LANES = 128
SUBLANES = 8
ROW_TILE = 256
VMEM_LIMIT = 56 * 1024 * 1024
PACK_W = 1024
PACK_ROW_TILE = 128
NEG = -1e30
LOG2E = math.log2(math.e)
LN2 = math.log(2.0)
Q_PRESCALE = SOFTMAX_SCALE * LOG2E

_MESH = pl.DeviceIdType.MESH


def _pick(n, prefs):
    for p in prefs:
        if n % p == 0:
            return p
    return n


def _rt(m):
    return _pick(m, (384, ROW_TILE))


def _cparams(sem):
    return pltpu.CompilerParams(dimension_semantics=sem, vmem_limit_bytes=VMEM_LIMIT)


def _row(spec_cols, tm):
    return pl.BlockSpec((tm, spec_cols), lambda i: (i, 0))


def _full(shape):
    nd = len(shape)
    return pl.BlockSpec(shape, lambda *a: (0,) * nd)


def _sigmoid(x):
    return 1.0 / (1.0 + jnp.exp(-x))


def _silu(x):
    return x * _sigmoid(x)


def _dsilu(x):
    s = _sigmoid(x)
    return s * (1.0 + x * (1.0 - s))


def _dot(a, b):
    return jnp.dot(a, b, preferred_element_type=f32)


def _dot_nt(a, b):
    return lax.dot_general(a, b, (((1,), (1,)), ((), ())), preferred_element_type=f32)


def _dot_tn(a, b):
    return lax.dot_general(a, b, (((0,), (0,)), ((), ())), preferred_element_type=f32)


def _dot_hi(a, b):
    return jnp.dot(a, b, precision=lax.Precision.HIGHEST, preferred_element_type=f32)


def _mm(pairs, out_dtype, trans_b, name, carried=(), scatter=False):
    n = len(pairs)
    nx = len(carried)
    M = pairs[0][0].shape[0]
    N = pairs[0][1].shape[0] if trans_b else pairs[0][1].shape[1]
    tm = _pick(M, (768, 512, 256))
    tn = _pick(N, (512, 1408, 384, 256, 128))
    ni, nj = M // tm, N // tn

    def body(*refs):
        o_ref = refs[2 * n + nx]
        if nx:
            i, j = pl.program_id(0), pl.program_id(1)
            _hosted_exchange(refs[2 * n:2 * n + nx], refs[2 * n + nx + 1:2 * n + 2 * nx + 1],
                             refs[2 * n + 2 * nx + 1:], scatter,
                             (i == 0) & (j == 0), (i == ni - 1) & (j == nj - 1))
        acc = None
        for p in range(n):
            a = refs[2 * p][...].astype(bf16)
            b = refs[2 * p + 1][...].astype(bf16)
            r = _dot_nt(a, b) if trans_b else _dot(a, b)
            acc = r if acc is None else acc + r
        o_ref[...] = acc.astype(out_dtype)

    in_specs, args = [], []
    for a, b in pairs:
        k = a.shape[1]
        in_specs.append(pl.BlockSpec((tm, k), lambda i, j: (i, 0)))
        if trans_b:
            in_specs.append(pl.BlockSpec((tn, k), lambda i, j: (j, 0)))
        else:
            in_specs.append(pl.BlockSpec((k, tn), lambda i, j: (0, j)))
        args += [a, b]
    out_spec = pl.BlockSpec((tm, tn), lambda i, j: (i, j))
    out_shape = jax.ShapeDtypeStruct((M, N), out_dtype)
    if not nx:
        return pl.pallas_call(
            body, name=name, grid=(ni, nj), in_specs=in_specs, out_specs=out_spec, out_shape=out_shape,
            compiler_params=_cparams(("parallel", "parallel")),
        )(*args)
    any_spec = pl.BlockSpec(memory_space=pl.ANY)
    return pl.pallas_call(
        body, name=name, grid=(ni, nj), in_specs=in_specs + [any_spec] * nx,
        out_specs=[out_spec] + [any_spec] * nx,
        out_shape=[out_shape] + _exchange_shapes(carried, scatter),
        scratch_shapes=_exchange_sems(nx),
        compiler_params=_cparams(("arbitrary", "arbitrary")),
    )(*args, *carried)


def _mm_tn(a, g, name):
    M, K = a.shape
    N = g.shape[1]
    tm = _pick(M, (768, 512, 256))
    tk = _pick(K, (1024, 1408, 512, 384, 256))
    tn = _pick(N, (1024, 1408, 512, 384, 256, 128))

    def body(a_ref, g_ref, o_ref):
        @pl.when(pl.program_id(2) == 0)
        def _():
            o_ref[...] = jnp.zeros_like(o_ref)

        o_ref[...] += _dot_tn(a_ref[...].astype(bf16), g_ref[...].astype(bf16))

    return pl.pallas_call(
        body, name=name, grid=(K // tk, N // tn, M // tm),
        in_specs=[pl.BlockSpec((tm, tk), lambda k, j, m: (m, k)),
                  pl.BlockSpec((tm, tn), lambda k, j, m: (m, j))],
        out_specs=pl.BlockSpec((tk, tn), lambda k, j, m: (k, j)),
        out_shape=jax.ShapeDtypeStruct((K, N), f32),
        compiler_params=_cparams(("parallel", "parallel", "arbitrary")),
    )(a, g)


def _rstd(x):
    return lax.rsqrt(jnp.mean(x * x, axis=-1, keepdims=True) + EPS)


def _rms_bwd_math(x, g, dy):
    r = _rstd(x)
    xh = x * r
    dn = dy * g
    dx = r * (dn - xh * jnp.mean(dn * xh, axis=-1, keepdims=True))
    return dx, dy * xh


def _rms_fwd(x, g, out_dtype, name):
    M, K = x.shape
    tm = _rt(M)

    def body(x_ref, g_ref, o_ref):
        xv = x_ref[...]
        o_ref[...] = (xv * _rstd(xv) * g_ref[...]).astype(out_dtype)

    return pl.pallas_call(
        body, name=name, grid=(M // tm,), in_specs=[_row(K, tm), _full((1, K))],
        out_specs=_row(K, tm), out_shape=jax.ShapeDtypeStruct((M, K), out_dtype),
        compiler_params=_cparams(("parallel",)),
    )(x, g)


def _rms_bwd(x, g, dy, out_dtype, name, residual=None):
    M, K = x.shape
    tm = _rt(M)
    has_res = residual is not None

    def body(*refs):
        if has_res:
            x_ref, g_ref, dy_ref, r_ref, dx_ref, dg_ref = refs
        else:
            x_ref, g_ref, dy_ref, dx_ref, dg_ref = refs

        @pl.when(pl.program_id(0) == 0)
        def _():
            dg_ref[...] = jnp.zeros_like(dg_ref)

        dx, dgp = _rms_bwd_math(x_ref[...], g_ref[...], dy_ref[...].astype(f32))
        if has_res:
            dx = dx + r_ref[...]
        dx_ref[...] = dx.astype(out_dtype)
        dg_ref[...] += jnp.sum(dgp, axis=0, keepdims=True)

    ins = [x, g, dy] + ([residual] if has_res else [])
    in_specs = [_row(K, tm), _full((1, K)), _row(K, tm)] + ([_row(K, tm)] if has_res else [])
    return pl.pallas_call(
        body, name=name, grid=(M // tm,), in_specs=in_specs,
        out_specs=[_row(K, tm), _full((1, K))],
        out_shape=[jax.ShapeDtypeStruct((M, K), out_dtype), jax.ShapeDtypeStruct((1, K), f32)],
        compiler_params=_cparams(("arbitrary",)),
    )(*ins)


def _in_proj(h0, g, weights):
    M, K = h0.shape
    tm = _rt(M)
    n = len(weights)
    widths = [int(w.shape[1]) for w in weights]

    def body(x_ref, g_ref, *refs):
        xv = x_ref[...]
        hn = (xv * _rstd(xv) * g_ref[...]).astype(bf16)
        refs[n][...] = hn
        for p in range(n):
            refs[n + 1 + p][...] = _dot(hn, refs[p][...])

    return pl.pallas_call(
        body, name="in_proj", grid=(M // tm,),
        in_specs=[_row(K, tm), _full((1, K))] + [_full((K, wd)) for wd in widths],
        out_specs=[_row(K, tm)] + [_row(wd, tm) for wd in widths],
        out_shape=[jax.ShapeDtypeStruct((M, K), bf16)] + [jax.ShapeDtypeStruct((M, wd), f32) for wd in widths],
        compiler_params=_cparams(("parallel",)),
    )(h0, g, *weights)


def _in_proj_dw(hn, grads):
    M, K = hn.shape
    tm = _rt(M)
    n = len(grads)
    widths = [int(gr.shape[1]) for gr in grads]

    def body(a_ref, *refs):
        @pl.when(pl.program_id(0) == 0)
        def _():
            for p in range(n):
                refs[n + p][...] = jnp.zeros_like(refs[n + p])

        a = a_ref[...]
        for p in range(n):
            refs[n + p][...] += _dot_tn(a, refs[p][...].astype(bf16))

    return pl.pallas_call(
        body, name="in_proj_dw", grid=(M // tm,),
        in_specs=[_row(K, tm)] + [_row(wd, tm) for wd in widths],
        out_specs=[_full((K, wd)) for wd in widths],
        out_shape=[jax.ShapeDtypeStruct((K, wd), f32) for wd in widths],
        compiler_params=_cparams(("arbitrary",)),
    )(hn, *grads)


def _resid_norm(h0, mix, g2, g3):
    M, K = h0.shape
    tm = _rt(M)

    def body(h_ref, m_ref, g2_ref, g3_ref, h1_ref, hn_ref):
        mv = m_ref[...]
        h1 = h_ref[...] + mv * _rstd(mv) * g2_ref[...]
        h1_ref[...] = h1
        hn_ref[...] = (h1 * _rstd(h1) * g3_ref[...]).astype(bf16)

    return pl.pallas_call(
        body, name="resid_norm", grid=(M // tm,),
        in_specs=[_row(K, tm), _row(K, tm), _full((1, K)), _full((1, K))],
        out_specs=[_row(K, tm), _row(K, tm)],
        out_shape=[jax.ShapeDtypeStruct((M, K), f32), jax.ShapeDtypeStruct((M, K), bf16)],
        compiler_params=_cparams(("parallel",)),
    )(h0, mix, g2, g3)


def _final(h1, act, wdown, g4, tgt, n_real):
    M, K = h1.shape
    F = act.shape[1]
    tm = _rt(M)
    nt = M // tm

    def body(h_ref, a_ref, w_ref, g_ref, t_ref, dh_ref, dd_ref, dg_ref, ls_ref, acc_ref):
        i = pl.program_id(0)

        @pl.when(i == 0)
        def _():
            dg_ref[...] = jnp.zeros_like(dg_ref)
            acc_ref[...] = jnp.zeros_like(acc_ref)

        dv = _dot(a_ref[...], w_ref[...])
        g = g_ref[...]
        r = _rstd(dv)
        n = dv * r
        h2 = h_ref[...] + n * g
        rows = i * tm + lax.broadcasted_iota(jnp.int32, (tm, 1), 0)
        mask = ((rows >= N_META) & (rows < n_real)).astype(f32)
        diff = (h2 - t_ref[...]) * mask
        acc_ref[...] += jnp.sum(diff * diff, axis=0, keepdims=True)
        dh = diff * (1.0 / K)
        dh_ref[...] = dh
        dn = dh * g
        dd_ref[...] = (r * (dn - n * jnp.mean(dn * n, axis=-1, keepdims=True))).astype(bf16)
        dg_ref[...] += jnp.sum(dh * n, axis=0, keepdims=True)

        @pl.when(i == nt - 1)
        def _():
            ls_ref[...] = jnp.zeros((1, LANES), f32) + jnp.sum(acc_ref[...]) * (0.5 / K)

    return pl.pallas_call(
        body, name="ffn_down_loss", grid=(nt,),
        in_specs=[_row(K, tm), _row(F, tm), _full((F, K)), _full((1, K)), _row(K, tm)],
        out_specs=[_row(K, tm), _row(K, tm), _full((1, K)), _full((1, LANES))],
        out_shape=[jax.ShapeDtypeStruct((M, K), f32), jax.ShapeDtypeStruct((M, K), bf16),
                   jax.ShapeDtypeStruct((1, K), f32), jax.ShapeDtypeStruct((1, LANES), f32)],
        scratch_shapes=[pltpu.VMEM((1, K), f32)],
        compiler_params=_cparams(("arbitrary",)),
    )(h1, act, wdown, g4, tgt)


def _mid_bwd(h1, g3, dup_g, dup_v, wup, dh2, mix, g2):
    M, K = h1.shape
    F = dup_g.shape[1]
    tm = ROW_TILE

    def body(h_ref, g3_ref, ag_ref, av_ref, w_ref, dh2_ref, m_ref, g2_ref, dh1_ref, dm_ref, dg3_ref, dg2_ref):
        @pl.when(pl.program_id(0) == 0)
        def _():
            dg3_ref[...] = jnp.zeros_like(dg3_ref)
            dg2_ref[...] = jnp.zeros_like(dg2_ref)

        d_hn2 = _dot_nt(ag_ref[...], w_ref[:, 0:F]) + _dot_nt(av_ref[...], w_ref[:, F:2 * F])
        dx, dgp = _rms_bwd_math(h_ref[...], g3_ref[...], d_hn2)
        dh1 = dh2_ref[...] + dx
        dh1_ref[...] = dh1
        dg3_ref[...] += jnp.sum(dgp, axis=0, keepdims=True)
        dm, dgp2 = _rms_bwd_math(m_ref[...], g2_ref[...], dh1)
        dm_ref[...] = dm.astype(bf16)
        dg2_ref[...] += jnp.sum(dgp2, axis=0, keepdims=True)

    return pl.pallas_call(
        body, name="ffn_up_dx_mid_bwd", grid=(M // tm,),
        in_specs=[_row(K, tm), _full((1, K)), _row(F, tm), _row(F, tm), _full((K, 2 * F)), _row(K, tm),
                  _row(K, tm), _full((1, K))],
        out_specs=[_row(K, tm), _row(K, tm), _full((1, K)), _full((1, K))],
        out_shape=[jax.ShapeDtypeStruct((M, K), f32), jax.ShapeDtypeStruct((M, K), bf16),
                   jax.ShapeDtypeStruct((1, K), f32), jax.ShapeDtypeStruct((1, K), f32)],
        compiler_params=_cparams(("arbitrary",)),
    )(h1, g3, dup_g, dup_v, wup, dh2, mix, g2)


HEADS_PER_STEP = 4
CONV_RB = 16


def _conv_block_taps(x_ref, halo, rb, lanes, kw):
    r0 = rb * CONV_RB
    if rb == 0:
        cat = jnp.concatenate([halo, x_ref[0:CONV_RB, lanes]], axis=0)
        first = SUBLANES - (kw - 1)
        return [cat[first + k:first + k + CONV_RB] for k in range(kw)]
    return [x_ref[r0 - (kw - 1) + k:r0 - (kw - 1) + k + CONV_RB, lanes] for k in range(kw)]


def _conv_weighted(taps, w, kw):
    u = None
    for k in range(kw):
        t = taps[k] * w[k:k + 1, :]
        u = t if u is None else u + t
    return u


def _conv_block_dx(du, nxt, w, kw):
    cat = jnp.concatenate([du, nxt], axis=0)
    return _conv_weighted([cat[kw - 1 - k:kw - 1 - k + CONV_RB] for k in range(kw)], w, kw)


def _prev_spec(tm, tc, col_of, row_axis, reversed_tiles=0):
    def imap(*ids):
        i = ids[row_axis]
        if reversed_tiles:
            i = reversed_tiles - 1 - i
        return (jnp.maximum(i * (tm // SUBLANES) - 1, 0), col_of(*ids))
    return pl.BlockSpec((SUBLANES, tc), imap)


def _ssm_conv_fwd(xbc, w, b):
    M, C = xbc.shape
    tm, tc, kw = ROW_TILE, C, SSM_CONV

    def body(x_ref, h_ref, w_ref, b_ref, o_ref):
        i = pl.program_id(0)

        def chunk(j, carry):
            lanes = pl.ds(pl.multiple_of(j * LANES, LANES), LANES)
            halo = jnp.where(i == 0, 0.0, h_ref[:, lanes])
            wv = w_ref[:, lanes]
            bv = b_ref[:, lanes]
            for rb in range(tm // CONV_RB):
                u = _conv_weighted(_conv_block_taps(x_ref, halo, rb, lanes, kw), wv, kw) + bv
                o_ref[rb * CONV_RB:(rb + 1) * CONV_RB, lanes] = _silu(u)
            return carry

        lax.fori_loop(0, tc // LANES, chunk, 0)

    return pl.pallas_call(
        body, name="ssm_conv_fwd", grid=(M // tm, C // tc),
        in_specs=[pl.BlockSpec((tm, tc), lambda i, j: (i, j)),
                  _prev_spec(tm, tc, lambda i, j: j, 0),
                  pl.BlockSpec((SUBLANES, tc), lambda i, j: (0, j)),
                  pl.BlockSpec((1, tc), lambda i, j: (0, j))],
        out_specs=pl.BlockSpec((tm, tc), lambda i, j: (i, j)),
        out_shape=jax.ShapeDtypeStruct((M, C), f32),
        compiler_params=_cparams(("parallel", "parallel")),
    )(xbc, xbc, w, b)


def _ssm_conv_bwd(xbc, w, b, dout):
    M, C = xbc.shape
    tm, tc, kw = ROW_TILE, C // 3, SSM_CONV
    nt = M // tm

    def body(x_ref, h_ref, w_ref, b_ref, d_ref, dx_ref, dw_ref, db_ref, nxt_ref):
        i = pl.program_id(1)

        @pl.when(i == 0)
        def _():
            dw_ref[...] = jnp.zeros_like(dw_ref)
            db_ref[...] = jnp.zeros_like(db_ref)
            nxt_ref[...] = jnp.zeros_like(nxt_ref)

        def chunk(j, carry):
            lanes = pl.ds(pl.multiple_of(j * LANES, LANES), LANES)
            halo = jnp.where(i == nt - 1, 0.0, h_ref[:, lanes])
            wv = w_ref[:, lanes]
            bv = b_ref[:, lanes]
            nxt = nxt_ref[:, lanes]
            db = jnp.zeros((CONV_RB, LANES), f32)
            dw = [jnp.zeros((CONV_RB, LANES), f32) for _ in range(kw)]
            for rb in reversed(range(tm // CONV_RB)):
                rows = slice(rb * CONV_RB, (rb + 1) * CONV_RB)
                taps = _conv_block_taps(x_ref, halo, rb, lanes, kw)
                du = d_ref[rows, lanes] * _dsilu(_conv_weighted(taps, wv, kw) + bv)
                db = db + du
                dw = [dw[k] + du * taps[k] for k in range(kw)]
                dx_ref[rows, lanes] = _conv_block_dx(du, nxt, wv, kw).astype(bf16)
                nxt = du[0:SUBLANES]
            nxt_ref[:, lanes] = nxt
            db_ref[:, lanes] += jnp.sum(db, axis=0, keepdims=True)
            for k in range(kw):
                dw_ref[k:k + 1, lanes] += jnp.sum(dw[k], axis=0, keepdims=True)
            return carry

        lax.fori_loop(0, tc // LANES, chunk, 0)

    tile = pl.BlockSpec((tm, tc), lambda j, i: (nt - 1 - i, j))
    return pl.pallas_call(
        body, name="ssm_conv_bwd", grid=(C // tc, nt),
        in_specs=[tile, _prev_spec(tm, tc, lambda j, i: j, 1, nt),
                  pl.BlockSpec((SUBLANES, tc), lambda j, i: (0, j)),
                  pl.BlockSpec((1, tc), lambda j, i: (0, j)), tile],
        out_specs=[tile, pl.BlockSpec((SUBLANES, tc), lambda j, i: (0, j)),
                   pl.BlockSpec((1, tc), lambda j, i: (0, j))],
        out_shape=[jax.ShapeDtypeStruct((M, C), bf16), jax.ShapeDtypeStruct((SUBLANES, C), f32),
                   jax.ShapeDtypeStruct((1, C), f32)],
        scratch_shapes=[pltpu.VMEM((SUBLANES, tc), f32)],
        compiler_params=_cparams(("parallel", "arbitrary")),
    )(xbc, xbc, w, b, dout)


def _ffn_gate_fwd(up, w, b):
    M = up.shape[0]
    tm, tc, kw = ROW_TILE, D_FF // 2, FFN_CONV
    nc = D_FF // tc

    def body(xg_ref, hg_ref, xv_ref, hv_ref, wg_ref, wv_ref, bg_ref, bv_ref, o_ref):
        i = pl.program_id(0)

        def chunk(j, carry):
            lanes = pl.ds(pl.multiple_of(j * LANES, LANES), LANES)
            halo_g = jnp.where(i == 0, 0.0, hg_ref[:, lanes])
            halo_v = jnp.where(i == 0, 0.0, hv_ref[:, lanes])
            wg, wv = wg_ref[:, lanes], wv_ref[:, lanes]
            bg, bv = bg_ref[:, lanes], bv_ref[:, lanes]
            for rb in range(tm // CONV_RB):
                ug = _conv_weighted(_conv_block_taps(xg_ref, halo_g, rb, lanes, kw), wg, kw) + bg
                uv = _conv_weighted(_conv_block_taps(xv_ref, halo_v, rb, lanes, kw), wv, kw) + bv
                o_ref[rb * CONV_RB:(rb + 1) * CONV_RB, lanes] = (_silu(ug) * uv).astype(bf16)
            return carry

        lax.fori_loop(0, tc // LANES, chunk, 0)

    return pl.pallas_call(
        body, name="ffn_gate_fwd", grid=(M // tm, nc),
        in_specs=[pl.BlockSpec((tm, tc), lambda i, j: (i, j)),
                  _prev_spec(tm, tc, lambda i, j: j, 0),
                  pl.BlockSpec((tm, tc), lambda i, j: (i, j + nc)),
                  _prev_spec(tm, tc, lambda i, j: j + nc, 0),
                  pl.BlockSpec((SUBLANES, tc), lambda i, j: (0, j)),
                  pl.BlockSpec((SUBLANES, tc), lambda i, j: (0, j + nc)),
                  pl.BlockSpec((1, tc), lambda i, j: (0, j)),
                  pl.BlockSpec((1, tc), lambda i, j: (0, j + nc))],
        out_specs=pl.BlockSpec((tm, tc), lambda i, j: (i, j)),
        out_shape=jax.ShapeDtypeStruct((M, D_FF), bf16),
        compiler_params=_cparams(("parallel", "parallel")),
    )(up, up, up, up, w, w, b, b)


def _ffn_gate_bwd(up, w, b, d_act):
    M = up.shape[0]
    tm, tc, kw = ROW_TILE, D_FF // 2, FFN_CONV
    nc = D_FF // tc
    nt = M // tm

    def body(xg_ref, hg_ref, xv_ref, hv_ref, wg_ref, wv_ref, bg_ref, bv_ref, d_ref,
             dxg_ref, dxv_ref, dwg_ref, dwv_ref, dbg_ref, dbv_ref, ng_ref, nv_ref):
        i = pl.program_id(1)

        @pl.when(i == 0)
        def _():
            for r in (dwg_ref, dwv_ref, dbg_ref, dbv_ref, ng_ref, nv_ref):
                r[...] = jnp.zeros_like(r)

        def chunk(j, carry):
            lanes = pl.ds(pl.multiple_of(j * LANES, LANES), LANES)
            halo_g = jnp.where(i == nt - 1, 0.0, hg_ref[:, lanes])
            halo_v = jnp.where(i == nt - 1, 0.0, hv_ref[:, lanes])
            wg, wv = wg_ref[:, lanes], wv_ref[:, lanes]
            bg, bv = bg_ref[:, lanes], bv_ref[:, lanes]
            nxt_g, nxt_v = ng_ref[:, lanes], nv_ref[:, lanes]
            zero = jnp.zeros((CONV_RB, LANES), f32)
            dbg, dbv = zero, zero
            dwg = [zero for _ in range(kw)]
            dwv = [zero for _ in range(kw)]
            for rb in reversed(range(tm // CONV_RB)):
                rows = slice(rb * CONV_RB, (rb + 1) * CONV_RB)
                tg = _conv_block_taps(xg_ref, halo_g, rb, lanes, kw)
                tv = _conv_block_taps(xv_ref, halo_v, rb, lanes, kw)
                ug = _conv_weighted(tg, wg, kw) + bg
                uv = _conv_weighted(tv, wv, kw) + bv
                sg = _sigmoid(ug)
                da = d_ref[rows, lanes]
                dug = da * uv * (sg * (1.0 + ug * (1.0 - sg)))
                duv = da * (ug * sg)
                dbg = dbg + dug
                dbv = dbv + duv
                dwg = [dwg[k] + dug * tg[k] for k in range(kw)]
                dwv = [dwv[k] + duv * tv[k] for k in range(kw)]
                dxg_ref[rows, lanes] = _conv_block_dx(dug, nxt_g, wg, kw).astype(bf16)
                dxv_ref[rows, lanes] = _conv_block_dx(duv, nxt_v, wv, kw).astype(bf16)
                nxt_g, nxt_v = dug[0:SUBLANES], duv[0:SUBLANES]
            ng_ref[:, lanes] = nxt_g
            nv_ref[:, lanes] = nxt_v
            dbg_ref[:, lanes] += jnp.sum(dbg, axis=0, keepdims=True)
            dbv_ref[:, lanes] += jnp.sum(dbv, axis=0, keepdims=True)
            for k in range(kw):
                dwg_ref[k:k + 1, lanes] += jnp.sum(dwg[k], axis=0, keepdims=True)
                dwv_ref[k:k + 1, lanes] += jnp.sum(dwv[k], axis=0, keepdims=True)
            return carry

        lax.fori_loop(0, tc // LANES, chunk, 0)

    tile_g = pl.BlockSpec((tm, tc), lambda j, i: (nt - 1 - i, j))
    tile_v = pl.BlockSpec((tm, tc), lambda j, i: (nt - 1 - i, j + nc))
    ext = pltpu.VMEM((SUBLANES, tc), f32)
    return pl.pallas_call(
        body, name="ffn_gate_bwd", grid=(nc, nt),
        in_specs=[tile_g, _prev_spec(tm, tc, lambda j, i: j, 1, nt),
                  tile_v, _prev_spec(tm, tc, lambda j, i: j + nc, 1, nt),
                  pl.BlockSpec((SUBLANES, tc), lambda j, i: (0, j)),
                  pl.BlockSpec((SUBLANES, tc), lambda j, i: (0, j + nc)),
                  pl.BlockSpec((1, tc), lambda j, i: (0, j)),
                  pl.BlockSpec((1, tc), lambda j, i: (0, j + nc)),
                  tile_g],
        out_specs=[tile_g, tile_g,
                   pl.BlockSpec((SUBLANES, tc), lambda j, i: (0, j)),
                   pl.BlockSpec((SUBLANES, tc), lambda j, i: (0, j)),
                   pl.BlockSpec((1, tc), lambda j, i: (0, j)),
                   pl.BlockSpec((1, tc), lambda j, i: (0, j))],
        out_shape=[jax.ShapeDtypeStruct((M, D_FF), bf16), jax.ShapeDtypeStruct((M, D_FF), bf16),
                   jax.ShapeDtypeStruct((SUBLANES, D_FF), f32), jax.ShapeDtypeStruct((SUBLANES, D_FF), f32),
                   jax.ShapeDtypeStruct((1, D_FF), f32), jax.ShapeDtypeStruct((1, D_FF), f32)],
        scratch_shapes=[ext, ext],
        compiler_params=_cparams(("parallel", "arbitrary")),
    )(up, up, up, up, w, w, b, b, d_act)


def _rope_apply(blk, cos, sin):
    lane = lax.broadcasted_iota(jnp.int32, blk.shape, 1)
    half = QK_ROPE // 2
    partner = jnp.where(lane < half, pltpu.roll(blk, LANES - half, 1), pltpu.roll(blk, half, 1))
    return blk * cos + partner * sin


def _rope_unapply(d, cos, sin):
    t = d * sin
    lane = lax.broadcasted_iota(jnp.int32, d.shape, 1)
    half = QK_ROPE // 2
    partner = jnp.where(lane < half, pltpu.roll(t, LANES - half, 1), pltpu.roll(t, half, 1))
    return d * cos + partner


def _up_q_rope(q_c, g, wuq, cos, sin):
    M, K = q_c.shape
    tm = _pick(M, (768, 512, 256))

    hs = HEADS_PER_STEP

    def body(x_ref, g_ref, b_ref, c_ref, s_ref, a_ref, o_ref):
        xv = x_ref[...]
        a = (xv * _rstd(xv) * g_ref[...]).astype(bf16)
        a_ref[...] = a
        r = _dot(a, b_ref[...]) * Q_PRESCALE
        c, s = c_ref[...], s_ref[...]
        for u in range(hs):
            o_ref[u, :, 0:QK_NOPE] = r[:, u * QK_PAD:u * QK_PAD + QK_NOPE].astype(bf16)
            o_ref[u, :, QK_NOPE:QK_PAD] = _rope_apply(r[:, u * QK_PAD + QK_NOPE:(u + 1) * QK_PAD], c, s).astype(bf16)

    return pl.pallas_call(
        body, name="up_q_rope", grid=(M // tm, MLA_HEADS // hs),
        in_specs=[pl.BlockSpec((tm, K), lambda i, h: (i, 0)),
                  pl.BlockSpec((1, K), lambda i, h: (0, 0)),
                  pl.BlockSpec((K, hs * QK_PAD), lambda i, h: (0, h)),
                  pl.BlockSpec((tm, LANES), lambda i, h: (i, 0)),
                  pl.BlockSpec((tm, LANES), lambda i, h: (i, 0))],
        out_specs=[pl.BlockSpec((tm, K), lambda i, h: (i, 0)),
                   pl.BlockSpec((hs, tm, QK_PAD), lambda i, h: (h, i, 0))],
        out_shape=[jax.ShapeDtypeStruct((M, K), bf16), jax.ShapeDtypeStruct((MLA_HEADS, M, QK_PAD), bf16)],
        compiler_params=_cparams(("parallel", "arbitrary")),
    )(q_c, g, wuq, cos, sin)


def _up_kv_rope(kv_c, g, wukv, kpe_raw, cos, sin):
    M, K = kv_c.shape
    tm = _pick(M, (768, 512, 256))

    hs = HEADS_PER_STEP
    w = QK_NOPE + V_DIM

    def body(x_ref, g_ref, b_ref, pe_ref, c_ref, s_ref, a_ref, k_ref, v_ref):
        xv = x_ref[...]
        a = (xv * _rstd(xv) * g_ref[...]).astype(bf16)
        a_ref[...] = a
        r = _dot(a, b_ref[...])
        pe = _rope_apply(pe_ref[...], c_ref[...], s_ref[...]).astype(bf16)
        for u in range(hs):
            k_ref[u, :, 0:QK_NOPE] = r[:, u * w:u * w + QK_NOPE].astype(bf16)
            k_ref[u, :, QK_NOPE:QK_PAD] = pe
            v_ref[u] = r[:, u * w + QK_NOPE:(u + 1) * w].astype(bf16)

    return pl.pallas_call(
        body, name="up_kv_rope", grid=(M // tm, MLA_HEADS // hs),
        in_specs=[pl.BlockSpec((tm, K), lambda i, h: (i, 0)),
                  pl.BlockSpec((1, K), lambda i, h: (0, 0)),
                  pl.BlockSpec((K, hs * w), lambda i, h: (0, h)),
                  pl.BlockSpec((tm, LANES), lambda i, h: (i, 0)),
                  pl.BlockSpec((tm, LANES), lambda i, h: (i, 0)),
                  pl.BlockSpec((tm, LANES), lambda i, h: (i, 0))],
        out_specs=[pl.BlockSpec((tm, K), lambda i, h: (i, 0)),
                   pl.BlockSpec((hs, tm, QK_PAD), lambda i, h: (h, i, 0)),
                   pl.BlockSpec((hs, tm, V_DIM), lambda i, h: (h, i, 0))],
        out_shape=[jax.ShapeDtypeStruct((M, K), bf16), jax.ShapeDtypeStruct((MLA_HEADS, M, QK_PAD), bf16),
                   jax.ShapeDtypeStruct((MLA_HEADS, M, V_DIM), bf16)],
        compiler_params=_cparams(("parallel", "arbitrary")),
    )(kv_c, g, wukv, kpe_raw, cos, sin)


def _latent_bwd(d_full_sc, w_ref, x_ref, g_ref, a_ref, dx_ref, dg_ref, dw_ref):
    d_full = d_full_sc[...]
    dx, dgp = _rms_bwd_math(x_ref[...], g_ref[...], _dot_nt(d_full, w_ref[...]))
    dx_ref[...] = dx.astype(bf16)
    dg_ref[...] += jnp.sum(dgp, axis=0, keepdims=True)
    dw_ref[...] += _dot_tn(a_ref[...], d_full)


def _latent_bwd_call(body, name, head_inputs, head_specs, cos, sin, w, x, g, a, extra_out_specs, extra_out_shape):
    M, K = x.shape
    tm = _rt(M)
    N = w.shape[1]
    return pl.pallas_call(
        body, name=name, grid=(M // tm,),
        in_specs=head_specs + [_row(LANES, tm), _row(LANES, tm), _full((K, N)), _row(K, tm), _full((1, K)),
                               _row(K, tm)],
        out_specs=[_row(K, tm), _full((1, K)), _full((K, N))] + extra_out_specs,
        out_shape=[jax.ShapeDtypeStruct((M, K), bf16), jax.ShapeDtypeStruct((1, K), f32),
                   jax.ShapeDtypeStruct((K, N), f32)] + extra_out_shape,
        scratch_shapes=[pltpu.VMEM((tm, N), bf16)],
        compiler_params=_cparams(("arbitrary",)),
    )(*head_inputs, cos, sin, w, x, g, a)


def _q_branch_bwd(dq, cos, sin, wuq, q_c, g, qn):
    tm = _rt(q_c.shape[0])

    def body(d_ref, c_ref, s_ref, w_ref, x_ref, g_ref, a_ref, dx_ref, dg_ref, dw_ref, full_sc):
        @pl.when(pl.program_id(0) == 0)
        def _():
            dg_ref[...] = jnp.zeros_like(dg_ref)
            dw_ref[...] = jnp.zeros_like(dw_ref)

        c, s = c_ref[...], s_ref[...]
        for h in range(MLA_HEADS):
            full_sc[:, h * QK_PAD:h * QK_PAD + QK_NOPE] = (d_ref[h, :, 0:QK_NOPE] * SOFTMAX_SCALE).astype(bf16)
            full_sc[:, h * QK_PAD + QK_NOPE:(h + 1) * QK_PAD] = (_rope_unapply(
                d_ref[h, :, QK_NOPE:QK_PAD], c, s) * SOFTMAX_SCALE).astype(bf16)
        _latent_bwd(full_sc, w_ref, x_ref, g_ref, a_ref, dx_ref, dg_ref, dw_ref)

    return _latent_bwd_call(body, "q_branch_bwd", [dq],
                            [pl.BlockSpec((MLA_HEADS, tm, QK_PAD), lambda i: (0, i, 0))],
                            cos, sin, wuq, q_c, g, qn, [], [])


def _kv_branch_bwd(dk, dv, cos, sin, wukv, kv_c, g, kvn):
    M = kv_c.shape[0]
    tm = _rt(M)
    w = QK_NOPE + V_DIM

    def body(dk_ref, dv_ref, c_ref, s_ref, w_ref, x_ref, g_ref, a_ref, dx_ref, dg_ref, dw_ref, pe_ref, full_sc):
        @pl.when(pl.program_id(0) == 0)
        def _():
            dg_ref[...] = jnp.zeros_like(dg_ref)
            dw_ref[...] = jnp.zeros_like(dw_ref)

        pe = None
        for h in range(MLA_HEADS):
            full_sc[:, h * w:h * w + QK_NOPE] = dk_ref[h, :, 0:QK_NOPE].astype(bf16)
            full_sc[:, h * w + QK_NOPE:(h + 1) * w] = dv_ref[h].astype(bf16)
            t = dk_ref[h, :, QK_NOPE:QK_PAD]
            pe = t if pe is None else pe + t
        pe_ref[...] = _rope_unapply(pe, c_ref[...], s_ref[...])
        _latent_bwd(full_sc, w_ref, x_ref, g_ref, a_ref, dx_ref, dg_ref, dw_ref)

    return _latent_bwd_call(body, "kv_branch_bwd", [dk, dv],
                            [pl.BlockSpec((MLA_HEADS, tm, QK_PAD), lambda i: (0, i, 0)),
                             pl.BlockSpec((MLA_HEADS, tm, V_DIM), lambda i: (0, i, 0))],
                            cos, sin, wukv, kv_c, g, kvn, [_row(LANES, tm)],
                            [jax.ShapeDtypeStruct((M, LANES), f32)])


def _attn_tile(M):
    return 768 if (M % 768 == 0 and M >= 4 * 768) else ROW_TILE


def _col_to_row(col):
    return col.T[0:1, :]


def _hosted_exchange(refs_in, refs_out, sems, scatter, first, last):
    copies = _exchange_copies(refs_in, refs_out, *sems, scatter)

    @pl.when(first)
    def _():
        for cp in copies:
            cp.start()

    @pl.when(last)
    def _():
        for cp in copies:
            cp.wait()


def _flash_fwd(q, k, v, carried, scatter):
    H, M, _ = q.shape
    T = _attn_tile(M)
    nq = M // T
    nx = len(carried)

    def body(*refs):
        q_ref, k_ref, v_ref = refs[:3]
        o_ref, lse_ref = refs[3 + nx:5 + nx]
        sa_ref, sb_ref, m_sc, l_sc, acc_sc = refs[5 + 2 * nx:10 + 2 * nx]
        h = pl.program_id(0)
        i = pl.program_id(1)
        _hosted_exchange(refs[3:3 + nx], refs[5 + nx:5 + 2 * nx], refs[10 + 2 * nx:], scatter,
                         (h == 0) & (i == 0), (h == H - 1) & (i == nq - 1))
        qv = q_ref[0]
        m_sc[...] = jnp.full_like(m_sc, NEG)
        l_sc[...] = jnp.zeros_like(l_sc)
        acc_sc[...] = jnp.zeros_like(acc_sc)

        def scores(j, s_ref):
            off = pl.multiple_of(j * T, T)
            s_ref[...] = _dot_nt(qv, k_ref[0, pl.ds(off, T), :])

        def softmax_pv(j, s_ref, masked):
            off = pl.multiple_of(j * T, T)
            s = s_ref[...]
            if masked:
                r = lax.broadcasted_iota(jnp.int32, (T, T), 0)
                c = lax.broadcasted_iota(jnp.int32, (T, T), 1)
                s = jnp.where(r >= c, s, NEG)
            m_prev = m_sc[...]
            m_new = jnp.maximum(m_prev, jnp.max(s, axis=1, keepdims=True))
            alpha = jnp.exp2(m_prev - m_new)
            p = jnp.exp2(s - m_new[:, 0:1])
            l_sc[...] = alpha * l_sc[...] + jnp.sum(p, axis=1, keepdims=True)
            acc_sc[...] = alpha * acc_sc[...] + _dot(p.astype(bf16), v_ref[0, pl.ds(off, T), :])
            m_sc[...] = m_new

        scores(0, sa_ref)

        def pair(jj, c):
            j0 = 2 * jj
            scores(j0 + 1, sb_ref)
            softmax_pv(j0, sa_ref, False)
            scores(j0 + 2, sa_ref)
            softmax_pv(j0 + 1, sb_ref, False)
            return c

        lax.fori_loop(0, i // 2, pair, 0)

        @pl.when(i % 2 == 0)
        def _():
            softmax_pv(i, sa_ref, True)

        @pl.when(i % 2 == 1)
        def _():
            scores(i, sb_ref)
            softmax_pv(i - 1, sa_ref, False)
            softmax_pv(i, sb_ref, True)

        l = l_sc[...]
        o_ref[...] = acc_sc[...] / l
        lse_ref[0, 0] = _col_to_row(m_sc[...] + jnp.log2(l))

    any_spec = pl.BlockSpec(memory_space=pl.ANY)
    return pl.pallas_call(
        body, name="flash_fwd", grid=(H, nq),
        in_specs=[pl.BlockSpec((1, T, QK_PAD), lambda h, i: (h, i, 0)),
                  pl.BlockSpec((1, M, QK_PAD), lambda h, i: (h, 0, 0)),
                  pl.BlockSpec((1, M, V_DIM), lambda h, i: (h, 0, 0))] + [any_spec] * nx,
        out_specs=[pl.BlockSpec((T, V_DIM), lambda h, i: (i, h)),
                   pl.BlockSpec((1, 1, 1, T), lambda h, i: (h, i, 0, 0))] + [any_spec] * nx,
        out_shape=[jax.ShapeDtypeStruct((M, H * V_DIM), f32),
                   jax.ShapeDtypeStruct((H, nq, 1, T), f32)] + _exchange_shapes(carried, scatter),
        scratch_shapes=[pltpu.VMEM((T, T), f32), pltpu.VMEM((T, T), f32),
                        pltpu.VMEM((T, LANES), f32), pltpu.VMEM((T, LANES), f32),
                        pltpu.VMEM((T, V_DIM), f32)] + _exchange_sems(nx),
        compiler_params=_cparams(("arbitrary", "arbitrary")),
    )(q, k, v, *carried)


def _attn_out_bwd(o, g, d_an):
    M, K = o.shape
    H = MLA_HEADS
    T = _attn_tile(M)

    def body(o_ref, g_ref, d_ref, dh_ref, dl_ref, dg_ref):
        @pl.when(pl.program_id(0) == 0)
        def _():
            dg_ref[...] = jnp.zeros_like(dg_ref)

        ov = o_ref[...]
        do, dgp = _rms_bwd_math(ov, g_ref[...], d_ref[...])
        dg_ref[...] += jnp.sum(dgp, axis=0, keepdims=True)
        for h in range(H):
            sl = slice(h * V_DIM, (h + 1) * V_DIM)
            doh = do[:, sl]
            dh_ref[h] = doh.astype(bf16)
            col = jnp.sum(ov[:, sl] * doh, axis=1, keepdims=True) + jnp.zeros((T, LANES), f32)
            dl_ref[h, 0] = _col_to_row(col)

    return pl.pallas_call(
        body, name="attn_out_bwd", grid=(M // T,),
        in_specs=[_row(K, T), _full((1, K)), _row(K, T)],
        out_specs=[pl.BlockSpec((H, T, V_DIM), lambda i: (0, i, 0)),
                   pl.BlockSpec((H, 1, 1, T), lambda i: (0, i, 0, 0)),
                   _full((1, K))],
        out_shape=[jax.ShapeDtypeStruct((H, M, V_DIM), bf16),
                   jax.ShapeDtypeStruct((H, M // T, 1, T), f32),
                   jax.ShapeDtypeStruct((1, K), f32)],
        compiler_params=_cparams(("arbitrary",)),
    )(o, g, d_an)


def _flash_bwd(q, k, v, do, lse, delta, carried, scatter):
    H, M, _ = q.shape
    T = _attn_tile(M)
    nq = M // T
    nx = len(carried)

    def body(*refs):
        q_ref, do_ref, lse_ref, dl_ref, k_ref, v_ref = refs[:6]
        dq_ref, dk_ref, dv_ref = refs[6 + nx:9 + nx]
        dk_sc, dv_sc = refs[9 + 2 * nx:11 + 2 * nx]
        j = pl.program_id(1)
        _hosted_exchange(refs[6:6 + nx], refs[9 + nx:9 + 2 * nx], refs[11 + 2 * nx:], scatter,
                         (pl.program_id(0) == 0) & (j == 0), (pl.program_id(0) == H - 1) & (j == nq - 1))

        @pl.when(j == 0)
        def _():
            dq_ref[...] = jnp.zeros_like(dq_ref)

        kt = k_ref[0]
        vt = v_ref[0]
        dk_sc[...] = jnp.zeros_like(dk_sc)
        dv_sc[...] = jnp.zeros_like(dv_sc)

        def step(i, masked):
            off = pl.multiple_of(i * T, T)
            qt = q_ref[0, pl.ds(off, T), :]
            dot_ = do_ref[0, pl.ds(off, T), :]
            st = _dot_nt(kt, qt)
            if masked:
                r = lax.broadcasted_iota(jnp.int32, (T, T), 0)
                c = lax.broadcasted_iota(jnp.int32, (T, T), 1)
                st = jnp.where(c >= r, st, NEG)
            pt = jnp.exp2(st - lse_ref[0, i])
            dv_sc[...] += _dot(pt.astype(bf16), dot_)
            dpt = _dot_nt(vt, dot_)
            dst = (pt * (dpt - dl_ref[0, i])).astype(bf16)
            dk_sc[...] += _dot(dst, qt)
            dq_ref[0, pl.ds(off, T), :] += _dot_tn(dst, kt)

        step(j, True)

        def loop_body(i, c):
            step(i, False)
            return c

        lax.fori_loop(j + 1, nq, loop_body, 0)
        dk_ref[0] = dk_sc[...] * LN2
        dv_ref[0] = dv_sc[...]

    any_spec = pl.BlockSpec(memory_space=pl.ANY)
    return pl.pallas_call(
        body, name="flash_bwd", grid=(H, nq),
        in_specs=[pl.BlockSpec((1, M, QK_PAD), lambda h, j: (h, 0, 0)),
                  pl.BlockSpec((1, M, V_DIM), lambda h, j: (h, 0, 0)),
                  pl.BlockSpec((1, nq, 1, T), lambda h, j: (h, 0, 0, 0)),
                  pl.BlockSpec((1, nq, 1, T), lambda h, j: (h, 0, 0, 0)),
                  pl.BlockSpec((1, T, QK_PAD), lambda h, j: (h, j, 0)),
                  pl.BlockSpec((1, T, V_DIM), lambda h, j: (h, j, 0))] + [any_spec] * nx,
        out_specs=[pl.BlockSpec((1, M, QK_PAD), lambda h, j: (h, 0, 0)),
                   pl.BlockSpec((1, T, QK_PAD), lambda h, j: (h, j, 0)),
                   pl.BlockSpec((1, T, V_DIM), lambda h, j: (h, j, 0))] + [any_spec] * nx,
        out_shape=[jax.ShapeDtypeStruct((H, M, QK_PAD), f32),
                   jax.ShapeDtypeStruct((H, M, QK_PAD), f32),
                   jax.ShapeDtypeStruct((H, M, V_DIM), f32)] + _exchange_shapes(carried, scatter),
        scratch_shapes=[pltpu.VMEM((T, QK_PAD), f32), pltpu.VMEM((T, V_DIM), f32)] + _exchange_sems(nx),
        compiler_params=_cparams(("arbitrary", "arbitrary")),
    )(q, do, lse, delta, k, v, *carried)


def _dt_fwd(dt_raw, bias, expand):
    M = dt_raw.shape[0]
    tm = _rt(M)

    def body(x_ref, b_ref, e_ref, o_ref, oe_ref):
        u = x_ref[...] + b_ref[...]
        sp = jnp.maximum(u, 0.0) + jnp.log(1.0 + jnp.exp(-jnp.abs(u)))
        lane = lax.broadcasted_iota(jnp.int32, u.shape, 1)
        dtp = jnp.where(lane < SSM_HEADS, sp, 0.0)
        o_ref[...] = dtp
        oe_ref[...] = _dot_hi(dtp, e_ref[...])

    return pl.pallas_call(
        body, name="dt_fwd", grid=(M // tm,),
        in_specs=[_row(LANES, tm), _full((1, LANES)), _full((LANES, D_SSM))],
        out_specs=[_row(LANES, tm), _row(D_SSM, tm)],
        out_shape=[jax.ShapeDtypeStruct((M, LANES), f32), jax.ShapeDtypeStruct((M, D_SSM), f32)],
        compiler_params=_cparams(("parallel",)),
    )(dt_raw, bias, expand)


def _dt_bwd(dt_raw, bias, ddt):
    M = dt_raw.shape[0]
    tm = _rt(M)

    def body(x_ref, b_ref, d_ref, o_ref, db_ref):
        @pl.when(pl.program_id(0) == 0)
        def _():
            db_ref[...] = jnp.zeros_like(db_ref)

        u = x_ref[...] + b_ref[...]
        lane = lax.broadcasted_iota(jnp.int32, u.shape, 1)
        g = jnp.where(lane < SSM_HEADS, d_ref[...] * _sigmoid(u), 0.0)
        o_ref[...] = g
        db_ref[...] += jnp.sum(g, axis=0, keepdims=True)

    return pl.pallas_call(
        body, name="dt_bwd", grid=(M // tm,),
        in_specs=[_row(LANES, tm), _full((1, LANES)), _row(LANES, tm)],
        out_specs=[_row(LANES, tm), _full((1, LANES))],
        out_shape=[jax.ShapeDtypeStruct((M, LANES), f32), jax.ShapeDtypeStruct((1, LANES), f32)],
        compiler_params=_cparams(("arbitrary",)),
    )(dt_raw, bias, ddt)


SSM_GW = SSM_HPG * SSM_P
SSM_PAIRS = SSM_GW // LANES


def _ssd_common(dte_ref, dtt_ref, ae_ref, acol_ref):
    Q = CHUNK
    r = lax.broadcasted_iota(jnp.int32, (Q, Q), 0)
    c = lax.broadcasted_iota(jnp.int32, (Q, Q), 1)
    causal = r >= c
    anti = c >= r
    tril = causal.astype(f32)
    triu = anti.astype(f32)
    dt_e = dte_ref[...]
    cs_e = _dot_hi(tril, dt_e * ae_ref[...])
    cst = _dot_hi(dtt_ref[...] * acol_ref[...], triu)
    cs_last = cs_e[Q - 1:Q, :]
    return causal, anti, triu, dt_e, cs_e, cst, jnp.exp(cs_e), jnp.exp(cs_last - cs_e), jnp.exp(cs_last)


def _half_masks():
    lane = lax.broadcasted_iota(jnp.int32, (CHUNK, LANES), 1)
    lo = lane < SSM_P
    return lo, jnp.logical_not(lo)


def _ssd_fwd(xbc_c, dt_e, dtt, a_e, a_col):
    M = xbc_c.shape[0]
    Q = CHUNK
    nch = M // Q

    def body(x_ref, dte_ref, dtt_ref, ae_ref, acol_ref, y_ref, hin_ref, ht_sc):
        @pl.when(pl.program_id(0) == 0)
        def _():
            ht_sc[...] = jnp.zeros_like(ht_sc)

        causal, _, _, dt_e, cs_e, cst, ecs_e, dte_e, elast_e = _ssd_common(dte_ref, dtt_ref, ae_ref, acol_ref)
        halves = _half_masks()
        for g in range(SSM_GROUPS):
            g0 = g * SSM_GW
            bg = x_ref[:, D_SSM + g * SSM_N:D_SSM + (g + 1) * SSM_N]
            cg = x_ref[:, D_SSM + D_BC + g * SSM_N:D_SSM + D_BC + (g + 1) * SSM_N]
            bg_b = bg.astype(bf16)
            cg_b = cg.astype(bf16)
            cb = _dot_nt(cg_b, bg_b)
            bgt_b = bg.T.astype(bf16)
            xdt_g = x_ref[:, g0:g0 + SSM_GW] * dt_e[:, g0:g0 + SSM_GW]
            ht = ht_sc[g]
            hin_ref[0, g] = ht
            y_off = _dot(cg_b, ht.astype(bf16)) * ecs_e[:, g0:g0 + SSM_GW]
            for pr in range(SSM_PAIRS):
                p0 = pr * LANES
                xdt_p = xdt_g[:, p0:p0 + LANES]
                acc = y_off[:, p0:p0 + LANES]
                for half in range(2):
                    h = g * SSM_HPG + pr * 2 + half
                    seg = cs_e[:, h * SSM_P:h * SSM_P + 1] - cst[h:h + 1, :]
                    lm = jnp.exp(jnp.where(causal, seg, -jnp.inf))
                    xm = jnp.where(halves[half], xdt_p, 0.0).astype(bf16)
                    acc = acc + _dot((cb * lm).astype(bf16), xm)
                y_ref[:, g0 + p0:g0 + p0 + LANES] = acc
            st = _dot(bgt_b, (xdt_g * dte_e[:, g0:g0 + SSM_GW]).astype(bf16))
            ht_sc[g] = ht * elast_e[:, g0:g0 + SSM_GW] + st

    return pl.pallas_call(
        body, name="ssd_fwd", grid=(nch,),
        in_specs=[pl.BlockSpec((Q, D_XBC), lambda c: (c, 0)),
                  pl.BlockSpec((Q, D_SSM), lambda c: (c, 0)),
                  pl.BlockSpec((SSM_HEADS, Q), lambda c: (0, c)),
                  _full((1, D_SSM)), _full((SSM_HEADS, LANES))],
        out_specs=[pl.BlockSpec((Q, D_SSM), lambda c: (c, 0)),
                   pl.BlockSpec((1, SSM_GROUPS, SSM_N, SSM_GW), lambda c: (c, 0, 0, 0))],
        out_shape=[jax.ShapeDtypeStruct((M, D_SSM), f32),
                   jax.ShapeDtypeStruct((nch, SSM_GROUPS, SSM_N, SSM_GW), f32)],
        scratch_shapes=[pltpu.VMEM((SSM_GROUPS, SSM_N, SSM_GW), f32)],
        compiler_params=_cparams(("arbitrary",)),
    )(xbc_c, dt_e, dtt, a_e, a_col)


def _ssd_bwd(xbc_c, dtp, dt_e, dtt, a_row, a_e, a_col, hin, dy, d_exp, head_ind):
    M = xbc_c.shape[0]
    Q = CHUNK
    nch = M // Q
    rev = lambda c: nch - 1 - c

    def body(x_ref, dtp_ref, dte_ref, dtt_ref, arow_ref, ae_ref, acol_ref, hin_ref, dy_ref, dexp_ref,
             ind_ref, dx_ref, ddt_ref, da_ref, dht_sc, z_sc, z1_sc, last_sc, ct_sc):
        @pl.when(pl.program_id(0) == 0)
        def _():
            dht_sc[...] = jnp.zeros_like(dht_sc)
            da_ref[...] = jnp.zeros_like(da_ref)
            last_sc[...] = jnp.zeros_like(last_sc)
            ct_sc[...] = jnp.zeros_like(ct_sc)

        causal, anti, triu, dt_e, cs_e, cst, ecs_e, dte_e, elast_e = _ssd_common(dte_ref, dtt_ref, ae_ref, acol_ref)
        halves = _half_masks()
        lane = lax.broadcasted_iota(jnp.int32, (Q, LANES), 1)
        rsum = jnp.zeros((Q, LANES), f32)
        for g in range(SSM_GROUPS):
            g0 = g * SSM_GW
            gs = slice(g0, g0 + SSM_GW)
            b0 = D_SSM + g * SSM_N
            c0 = D_SSM + D_BC + g * SSM_N
            bg = x_ref[:, b0:b0 + SSM_N]
            cg = x_ref[:, c0:c0 + SSM_N]
            bg_b = bg.astype(bf16)
            cg_b = cg.astype(bf16)
            cgt_b = cg.T.astype(bf16)
            cbt = _dot_nt(bg_b, cg_b)
            cb = _dot_nt(cg_b, bg_b)
            x_g = x_ref[:, gs]
            dt_g = dt_e[:, gs]
            xdt_g = x_g * dt_g
            dy_g = dy_ref[:, gs]
            ht = hin_ref[0, g]
            ht_b = ht.astype(bf16)
            dht = dht_sc[g]
            dht_b = dht.astype(bf16)
            dye_b = (dy_g * ecs_e[:, gs]).astype(bf16)
            dc = _dot_nt(dye_b, ht_b)
            dht_new = dht * elast_e[:, gs] + _dot(cgt_b, dye_b)
            e = _dot(bg_b, dht_b)
            xdtd = xdt_g * dte_e[:, gs]
            db = _dot_nt(xdtd.astype(bf16), dht_b)
            dxdt_state = e * dte_e[:, gs]
            exd = e * xdtd
            z1_sc[:, gs] = dy_g * (_dot(cg_b, ht_b) * ecs_e[:, gs]) - exd
            last_sc[0:1, gs] = (jnp.sum(exd, axis=0, keepdims=True)
                                + jnp.sum(dht * ht, axis=0, keepdims=True) * elast_e[:, gs])
            dg_acc = jnp.zeros((Q, Q), f32)
            for pr in range(SSM_PAIRS):
                p0 = pr * LANES
                ps = slice(g0 + p0, g0 + p0 + LANES)
                dy_p = dy_g[:, p0:p0 + LANES]
                xdt_pb = xdt_g[:, p0:p0 + LANES].astype(bf16)
                acc = dxdt_state[:, p0:p0 + LANES]
                for half in range(2):
                    h = g * SSM_HPG + pr * 2 + half
                    seg = cs_e[:, h * SSM_P:h * SSM_P + 1] - cst[h:h + 1, :]
                    lm = jnp.exp(jnp.where(causal, seg, -jnp.inf))
                    lmt = jnp.exp(jnp.where(anti, -seg, -jnp.inf))
                    dym = jnp.where(halves[half], dy_p, 0.0).astype(bf16)
                    acc = acc + _dot((cbt * lmt).astype(bf16), dym)
                    dml = _dot_nt(dym, xdt_pb) * lm
                    dg_acc = dg_acc + dml
                    w = dml * cb
                    rsum = rsum + jnp.where(lane == h, jnp.sum(w, axis=1, keepdims=True), 0.0)
                    ct_sc[h:h + 1, :] = jnp.sum(w, axis=0, keepdims=True)
                dx_ref[:, ps] = acc * dt_g[:, p0:p0 + LANES] + dexp_ref[:, ps] * dy_p
                z_sc[:, ps] = acc * x_g[:, p0:p0 + LANES]
            dg_b = dg_acc.astype(bf16)
            dx_ref[:, c0:c0 + SSM_N] = dc + _dot(dg_b, bg_b)
            dx_ref[:, b0:b0 + SSM_N] = db + _dot_tn(dg_b, cg_b)
            dht_sc[g] = dht_new
        s1 = _dot_hi(z1_sc[...], ind_ref[...])
        s2 = _dot_hi(z_sc[...], ind_ref[...])
        last = _dot_hi(last_sc[...], ind_ref[...])[0:1, :]
        dtp = dtp_ref[...]
        row = lax.broadcasted_iota(jnp.int32, (Q, LANES), 0)
        dcs = s1 + rsum + jnp.where(row == Q - 1, last, 0.0)
        tril = causal.astype(f32)
        da = _dot_hi(triu, dcs) - _dot_hi(ct_sc[...], tril).T
        ddt_ref[...] = s2 + da * arow_ref[...]
        da_ref[...] += jnp.sum(da * dtp, axis=0, keepdims=True)

    return pl.pallas_call(
        body, name="ssd_bwd", grid=(nch,),
        in_specs=[pl.BlockSpec((Q, D_XBC), lambda c: (rev(c), 0)),
                  pl.BlockSpec((Q, LANES), lambda c: (rev(c), 0)),
                  pl.BlockSpec((Q, D_SSM), lambda c: (rev(c), 0)),
                  pl.BlockSpec((SSM_HEADS, Q), lambda c: (0, rev(c))),
                  _full((1, LANES)), _full((1, D_SSM)), _full((SSM_HEADS, LANES)),
                  pl.BlockSpec((1, SSM_GROUPS, SSM_N, SSM_GW), lambda c: (rev(c), 0, 0, 0)),
                  pl.BlockSpec((Q, D_SSM), lambda c: (rev(c), 0)),
                  _full((1, D_SSM)), _full((D_SSM, LANES))],
        out_specs=[pl.BlockSpec((Q, D_XBC), lambda c: (rev(c), 0)),
                   pl.BlockSpec((Q, LANES), lambda c: (rev(c), 0)),
                   _full((1, LANES))],
        out_shape=[jax.ShapeDtypeStruct((M, D_XBC), f32), jax.ShapeDtypeStruct((M, LANES), f32),
                   jax.ShapeDtypeStruct((1, LANES), f32)],
        scratch_shapes=[pltpu.VMEM((SSM_GROUPS, SSM_N, SSM_GW), f32), pltpu.VMEM((Q, D_SSM), f32),
                        pltpu.VMEM((Q, D_SSM), f32), pltpu.VMEM((SUBLANES, D_SSM), f32),
                        pltpu.VMEM((LANES, Q), f32)],
        compiler_params=_cparams(("arbitrary",)),
    )(xbc_c, dtp, dt_e, dtt, a_row, a_e, a_col, hin, dy, d_exp, head_ind)


def _gate_norm_fwd(y, xbc_c, z, d_exp, g):
    M = y.shape[0]
    tm = _rt(M)
    gw = D_SSM // SSM_GROUPS

    def body(y_ref, x_ref, z_ref, d_ref, g_ref, o_ref):
        yg = (y_ref[...] + d_ref[...] * x_ref[...]) * _silu(z_ref[...])
        for gi in range(SSM_GROUPS):
            blk = yg[:, gi * gw:(gi + 1) * gw]
            o_ref[:, gi * gw:(gi + 1) * gw] = (blk * _rstd(blk) * g_ref[:, gi * gw:(gi + 1) * gw]).astype(bf16)

    return pl.pallas_call(
        body, name="gate_norm_fwd", grid=(M // tm,),
        in_specs=[_row(D_SSM, tm), _row(D_SSM, tm), _row(D_SSM, tm), _full((1, D_SSM)), _full((1, D_SSM))],
        out_specs=_row(D_SSM, tm), out_shape=jax.ShapeDtypeStruct((M, D_SSM), bf16),
        compiler_params=_cparams(("parallel",)),
    )(y, xbc_c, z, d_exp, g)


def _gate_norm_bwd(y, xbc_c, z, d_exp, g, dout, head_ind):
    M = y.shape[0]
    tm = _rt(M)
    nt = M // tm
    gw = D_SSM // SSM_GROUPS

    def body(y_ref, x_ref, z_ref, d_ref, g_ref, do_ref, ind_ref, dy_ref, dz_ref, dg_ref, dd_ref, ddc_sc):
        i = pl.program_id(0)

        @pl.when(i == 0)
        def _():
            dg_ref[...] = jnp.zeros_like(dg_ref)
            ddc_sc[...] = jnp.zeros_like(ddc_sc)

        zv = z_ref[...]
        xv = x_ref[...]
        s = _silu(zv)
        yd = y_ref[...] + d_ref[...] * xv
        yg = yd * s
        dov = do_ref[...]
        for gi in range(SSM_GROUPS):
            sl = slice(gi * gw, (gi + 1) * gw)
            dyg, dgp = _rms_bwd_math(yg[:, sl], g_ref[:, sl], dov[:, sl])
            dg_ref[:, sl] += jnp.sum(dgp, axis=0, keepdims=True)
            dyd = dyg * s[:, sl]
            dy_ref[:, sl] = dyd
            dz_ref[:, sl] = (dyg * yd[:, sl] * _dsilu(zv[:, sl])).astype(bf16)
            ddc_sc[:, sl] += jnp.sum(dyd * xv[:, sl], axis=0, keepdims=True)

        @pl.when(i == nt - 1)
        def _():
            dd_ref[...] = _dot_hi(ddc_sc[...], ind_ref[...])

    return pl.pallas_call(
        body, name="gate_norm_bwd", grid=(nt,),
        in_specs=[_row(D_SSM, tm), _row(D_SSM, tm), _row(D_SSM, tm), _full((1, D_SSM)), _full((1, D_SSM)),
                  _row(D_SSM, tm), _full((D_SSM, LANES))],
        out_specs=[_row(D_SSM, tm), _row(D_SSM, tm), _full((1, D_SSM)), _full((1, LANES))],
        out_shape=[jax.ShapeDtypeStruct((M, D_SSM), f32), jax.ShapeDtypeStruct((M, D_SSM), bf16),
                   jax.ShapeDtypeStruct((1, D_SSM), f32), jax.ShapeDtypeStruct((1, LANES), f32)],
        scratch_shapes=[pltpu.VMEM((1, D_SSM), f32)],
        compiler_params=_cparams(("arbitrary",)),
    )(y, xbc_c, z, d_exp, g, dout, head_ind)


_PEER_FLIPS = [(0, 0, 1), (0, 1, 0), (0, 1, 1), (1, 0, 0), (1, 0, 1), (1, 1, 0), (1, 1, 1)]


def _exchange_copies(ins, outs, send_sems, recv_sems, loc_sems, scatter):
    n = len(ins)
    x, y, c = lax.axis_index("x"), lax.axis_index("y"), lax.axis_index("c")
    me = 4 * x + 2 * y + c
    copies = []
    for a in range(n):
        src = ins[a].at[me] if scatter else ins[a]
        copies.append(pltpu.make_async_copy(src, outs[a].at[me], loc_sems.at[a]))
    for p, (fx, fy, fc) in enumerate(_PEER_FLIPS):
        tx = 1 - x if fx else x
        ty = 1 - y if fy else y
        tc = 1 - c if fc else c
        tgt = 4 * tx + 2 * ty + tc
        for a in range(n):
            src = ins[a].at[tgt] if scatter else ins[a]
            copies.append(pltpu.make_async_remote_copy(
                src_ref=src, dst_ref=outs[a].at[me],
                send_sem=send_sems.at[p * n + a], recv_sem=recv_sems.at[p * n + a],
                device_id=(tx, ty, tc), device_id_type=_MESH))
    return copies


def _exchange_shapes(arrays, scatter):
    return [jax.ShapeDtypeStruct(a.shape if scatter else (N_DEV,) + a.shape, a.dtype) for a in arrays]


def _exchange_sems(n):
    return [pltpu.SemaphoreType.DMA((7 * n,)), pltpu.SemaphoreType.DMA((7 * n,)), pltpu.SemaphoreType.DMA((n,))]


def _exchange(arrays, scatter, name):
    n = len(arrays)

    def body(*refs):
        copies = _exchange_copies(refs[:n], refs[n:2 * n], *refs[2 * n:], scatter)
        for cp in copies:
            cp.start()
        for cp in copies:
            cp.wait()

    any_spec = pl.BlockSpec(memory_space=pl.ANY)
    return pl.pallas_call(
        body, name=name, in_specs=[any_spec] * n, out_specs=[any_spec] * n,
        out_shape=_exchange_shapes(arrays, scatter), scratch_shapes=_exchange_sems(n),
    )(*arrays)


def _gather_two_level(arrays, name):
    n = len(arrays)

    def body(*refs):
        ins, outs = refs[:n], refs[n:2 * n]
        send_sems, recv_sems, loc_sems = refs[2 * n:]
        x, y, c = lax.axis_index("x"), lax.axis_index("y"), lax.axis_index("c")
        me, sibling = (x, y, c), (x, y, 1 - c)
        chips = [(1 - x, y), (x, 1 - y), (1 - x, 1 - y)]

        def slot(a, dev):
            return outs[a].at[4 * dev[0] + 2 * dev[1] + dev[2]]

        def copy(a, k, block, to, src=None):
            return pltpu.make_async_remote_copy(
                src_ref=slot(a, block) if src is None else src, dst_ref=slot(a, block),
                send_sem=send_sems.at[7 * a + k], recv_sem=recv_sems.at[7 * a + k],
                device_id=to, device_id_type=_MESH)

        mine = [pltpu.make_async_copy(ins[a], slot(a, me), loc_sems.at[a]) for a in range(n)]
        first = []
        for a in range(n):
            first.append(copy(a, 0, me, sibling, src=ins[a]))
            first += [copy(a, 1 + j, me, (*chip, c), src=ins[a]) for j, chip in enumerate(chips)]
        for cp in mine + first:
            cp.start()
        passed = []
        for j, chip in enumerate(chips):
            for a in range(n):
                copy(a, 1 + j, (*chip, c), me).wait_recv()
                cp = copy(a, 4 + j, (*chip, c), sibling)
                cp.start()
                passed.append(cp)
        for a in range(n):
            copy(a, 0, sibling, me).wait_recv()
            for j, chip in enumerate(chips):
                copy(a, 4 + j, (*chip, 1 - c), me).wait_recv()
        for cp in first + passed:
            cp.wait_send()
        for cp in mine:
            cp.wait()

    any_spec = pl.BlockSpec(memory_space=pl.ANY)
    return pl.pallas_call(
        body, name=name, in_specs=[any_spec] * n, out_specs=[any_spec] * n,
        out_shape=_exchange_shapes(arrays, False), scratch_shapes=_exchange_sems(n),
    )(*arrays)


def _exchange_tail(scattered, gathered, name):
    ns, ng = len(scattered), len(gathered)
    n = ns + ng

    def body(*refs):
        sems = refs[2 * n:]
        copies = (_exchange_copies(refs[:ns], refs[n:n + ns], *sems[:3], True)
                  + _exchange_copies(refs[ns:n], refs[n + ns:2 * n], *sems[3:], False))
        for cp in copies:
            cp.start()
        for cp in copies:
            cp.wait()

    any_spec = pl.BlockSpec(memory_space=pl.ANY)
    return pl.pallas_call(
        body, name=name, in_specs=[any_spec] * n, out_specs=[any_spec] * n,
        out_shape=_exchange_shapes(scattered, True) + _exchange_shapes(gathered, False),
        scratch_shapes=_exchange_sems(ns) + _exchange_sems(ng),
    )(*scattered, *gathered)


def _adamw_math(g, w, m, v):
    c1 = 1.0 - ADAM_B1 ** ADAM_STEP
    c2 = 1.0 - ADAM_B2 ** ADAM_STEP
    mn = ADAM_B1 * m + (1.0 - ADAM_B1) * g
    vn = ADAM_B2 * v + (1.0 - ADAM_B2) * (g * g)
    m_hat = mn / c1
    v_hat = vn / c2
    return -ADAM_LR * (m_hat / (jnp.sqrt(v_hat) + ADAM_EPS) + ADAM_WD * w), mn, vn


def _adamw(parts, w, m, v, name):
    R, C = w.shape
    tr = _pick(R, (PACK_ROW_TILE, 64, 32, 16, 8))

    def body(p_ref, w_ref, m_ref, v_ref, g_ref, d_ref, nm_ref, nv_ref):
        g = p_ref[0].astype(f32)
        for s in range(1, N_DEV):
            g = g + p_ref[s].astype(f32)
        g_ref[...] = g
        d_ref[...], nm_ref[...], nv_ref[...] = _adamw_math(g, w_ref[...], m_ref[...], v_ref[...])

    spec = pl.BlockSpec((tr, C), lambda i: (i, 0))
    return pl.pallas_call(
        body, name=name, grid=(R // tr,),
        in_specs=[pl.BlockSpec((N_DEV, tr, C), lambda i: (0, i, 0)), spec, spec, spec],
        out_specs=[spec] * 4, out_shape=[jax.ShapeDtypeStruct((R, C), f32)] * 4,
        compiler_params=_cparams(("parallel",)),
    )(parts, w, m, v)


def _adamw_replicated(parts, ws, ms, vs):
    n = len(ws)
    R = parts.shape[1]
    sizes = [int(w.shape[1]) for w in ws]

    def body(*refs):
        p_ref = refs[0]
        w_refs, m_refs, v_refs = refs[1:1 + n], refs[1 + n:1 + 2 * n], refs[1 + 2 * n:1 + 3 * n]
        loss_ref = refs[1 + 3 * n]
        outs = refs[2 + 3 * n:]
        g_all = p_ref[0]
        for s in range(1, N_DEV):
            g_all = g_all + p_ref[s]
        row = 0
        for p in range(n):
            pieces, left = [], sizes[p]
            while left > 0:
                take = min(left, PACK_W)
                pieces.append(g_all[row:row + 1, 0:take])
                left -= take
                row += 1
            g = pieces[0] if len(pieces) == 1 else jnp.concatenate(pieces, axis=1)
            d, mn, vn = _adamw_math(g, w_refs[p][...], m_refs[p][...], v_refs[p][...])
            outs[4 * p][...] = g
            outs[4 * p + 1][...] = d
            outs[4 * p + 2][...] = mn
            outs[4 * p + 3][...] = vn
        loss_ref[...] = g_all[row:row + 1, 0:LANES]

    in_specs = [_full((N_DEV, R, PACK_W))] + [_full((1, s)) for s in sizes] * 3
    out_specs = [_full((1, LANES))]
    out_shape = [jax.ShapeDtypeStruct((1, LANES), f32)]
    for s in sizes:
        out_specs += [_full((1, s))] * 4
        out_shape += [jax.ShapeDtypeStruct((1, s), f32)] * 4
    res = pl.pallas_call(
        body, name="adamw_replicated", in_specs=in_specs, out_specs=out_specs, out_shape=out_shape,
        compiler_params=pltpu.CompilerParams(vmem_limit_bytes=VMEM_LIMIT),
    )(parts, *ws, *ms, *vs)
    return res[0], [res[1 + 4 * p:5 + 4 * p] for p in range(n)]


def _flat_rows(a, lead_ndim):
    lead = a.shape[:lead_ndim]
    n = int(np.prod(a.shape[lead_ndim:]))
    a = a.reshape(lead + (n,))
    pad = (-n) % PACK_W
    if pad:
        a = jnp.pad(a, [(0, 0)] * lead_ndim + [(0, pad)])
    return a.reshape(lead + ((n + pad) // PACK_W, PACK_W))


def _pack(arrays, lead_ndim, total_rows, dtype):
    rows = [_flat_rows(a.astype(dtype), lead_ndim) for a in arrays]
    cat = jnp.concatenate(rows, axis=lead_ndim)
    pad = total_rows - cat.shape[lead_ndim]
    if pad:
        cat = jnp.pad(cat, [(0, 0)] * lead_ndim + [(0, pad), (0, 0)])
    return cat


def _unpack(buf, shapes, lead_ndim):
    out = []
    r = 0
    lead = buf.shape[:lead_ndim]
    for shp in shapes:
        n = int(np.prod(shp))
        nr = -(-n // PACK_W)
        piece = lax.slice_in_dim(buf, r, r + nr, axis=lead_ndim)
        piece = piece.reshape(lead + (nr * PACK_W,))
        piece = lax.slice_in_dim(piece, 0, n, axis=lead_ndim)
        out.append(piece.reshape(lead + tuple(shp)))
        r += nr
    return out


def _round_up(n, m):
    return -(-n // m) * m


def kernel(x, meta_tokens, norm_mix_pre, norm_mix_post, norm_ffn_pre, norm_ffn_post, w_in, q_a_norm, w_uq, kv_a_norm, w_ukv, attn_out_norm, ssm_conv_w, ssm_conv_b, ssm_dt_bias, ssm_A_log, ssm_D, ssm_norm, w_out, w_up, ffn_conv_w, ffn_conv_b, w_down, loss_target, m_meta_tokens, m_norm_mix_pre, m_norm_mix_post, m_norm_ffn_pre, m_norm_ffn_post, m_w_in, m_q_a_norm, m_w_uq, m_kv_a_norm, m_w_ukv, m_attn_out_norm, m_ssm_conv_w, m_ssm_conv_b, m_ssm_dt_bias, m_ssm_A_log, m_ssm_D, m_ssm_norm, m_w_out, m_w_up, m_ffn_conv_w, m_ffn_conv_b, m_w_down, v_meta_tokens, v_norm_mix_pre, v_norm_mix_post, v_norm_ffn_pre, v_norm_ffn_post, v_w_in, v_q_a_norm, v_w_uq, v_kv_a_norm, v_w_ukv, v_attn_out_norm, v_ssm_conv_w, v_ssm_conv_b, v_ssm_dt_bias, v_ssm_A_log, v_ssm_D, v_ssm_norm, v_w_out, v_w_up, v_ffn_conv_w, v_ffn_conv_b, v_w_down):
    seq = x.shape[1]
    n_real = N_META + seq
    Lp = _round_up(n_real, 768) if n_real > 2048 else _round_up(n_real, ROW_TILE)
    D = D_MODEL

    early_w = [w_uq, w_ukv]
    late_w = [w_out, w_down]
    sharded_s = [meta_tokens, ssm_conv_w, ffn_conv_w]
    grp_a = dict(names=["w_out", "w_down"], w=late_w, m=[m_w_out, m_w_down],
                 v=[v_w_out, v_w_down])
    grp_b = dict(names=["w_uq", "w_ukv", "ssm_conv_w", "ffn_conv_w"],
                 w=early_w + [ssm_conv_w, ffn_conv_w],
                 m=[m_w_uq, m_w_ukv, m_ssm_conv_w, m_ffn_conv_w],
                 v=[v_w_uq, v_w_ukv, v_ssm_conv_w, v_ffn_conv_w])
    grp_meta = dict(names=["meta_tokens"], w=[meta_tokens], m=[m_meta_tokens], v=[v_meta_tokens])
    repl_w = [norm_mix_pre, norm_mix_post, norm_ffn_pre, norm_ffn_post, q_a_norm, kv_a_norm, attn_out_norm,
              ssm_conv_b, ssm_dt_bias, ssm_A_log, ssm_D, ssm_norm, ffn_conv_b]
    repl_m = [m_norm_mix_pre, m_norm_mix_post, m_norm_ffn_pre, m_norm_ffn_post, m_q_a_norm, m_kv_a_norm,
              m_attn_out_norm, m_ssm_conv_b, m_ssm_dt_bias, m_ssm_A_log, m_ssm_D, m_ssm_norm, m_ffn_conv_b]
    repl_v = [v_norm_mix_pre, v_norm_mix_post, v_norm_ffn_pre, v_norm_ffn_post, v_q_a_norm, v_kv_a_norm,
              v_attn_out_norm, v_ssm_conv_b, v_ssm_dt_bias, v_ssm_A_log, v_ssm_D, v_ssm_norm, v_ffn_conv_b]

    def pack_rows(arrs, lead):
        return _round_up(sum(-(-int(np.prod(a.shape[lead:])) // PACK_W) for a in arrs), 16)

    wb = _pack(early_w, 0, pack_rows(early_w, 0), bf16)
    wl = _pack(late_w, 0, pack_rows(late_w, 0), bf16)
    ws = _pack(sharded_s, 0, pack_rows(sharded_s, 0), f32)
    wb_all, ws_all, win_all = _gather_two_level([wb, ws, w_in[0].astype(bf16)], "gather_weights")
    g_w_uq, g_w_ukv = _unpack(wb_all, [a.shape for a in early_w], 1)
    g_meta, g_sconv, g_fconv = _unpack(ws_all, [a.shape for a in sharded_s], 1)

    def cols(gathered):
        t = gathered[:, 0]
        return jnp.transpose(t, (1, 0, 2)).reshape(t.shape[1], N_DEV * t.shape[2])

    win = cols(win_all[:, None])
    o = np.cumsum((0, Q_RANK, KV_RANK, QK_ROPE, D_SSM, D_XBC, SSM_HEADS))
    w_q, w_kv = win[:, o[0]:o[1]], win[:, o[1]:o[2]]
    w_rope = jnp.pad(win[:, o[2]:o[3]], ((0, 0), (0, LANES - QK_ROPE)))
    w_z, w_xbc = win[:, o[3]:o[4]], win[:, o[4]:o[5]]
    w_dt = jnp.pad(win[:, o[5]:o[6]], ((0, 0), (0, LANES - SSM_HEADS)))
    wuq = g_w_uq.reshape(Q_RANK, MLA_HEADS, QK_NOPE + QK_ROPE)
    wuq = jnp.pad(wuq, ((0, 0), (0, 0), (0, QK_PAD - QK_NOPE - QK_ROPE))).reshape(Q_RANK, MLA_HEADS * QK_PAD)
    wukv = g_w_ukv.reshape(KV_RANK, MLA_HEADS * (QK_NOPE + V_DIM))
    meta_full = jnp.transpose(g_meta, (1, 0, 2)).reshape(N_META, D)
    sconv_w = jnp.pad(cols(g_sconv), ((0, SUBLANES - SSM_CONV), (0, 0)))
    fconv_w = jnp.pad(cols(g_fconv), ((0, SUBLANES - FFN_CONV), (0, 0)))

    pos = jnp.arange(Lp, dtype=f32)
    inv = ROPE_THETA ** (-jnp.arange(0, QK_ROPE, 2, dtype=f32) / QK_ROPE)
    ang = pos[:, None] * inv[None, :]
    cs_, sn_ = jnp.cos(ang), jnp.sin(ang)
    zpad = jnp.zeros((Lp, LANES - QK_ROPE), f32)
    cos_t = jnp.concatenate([cs_, cs_, zpad], axis=1)
    sin_t = jnp.concatenate([-sn_, sn_, zpad], axis=1)
    dt_bias_p = jnp.pad(ssm_dt_bias, ((0, 0), (0, LANES - SSM_HEADS)))
    a_neg = -jnp.exp(ssm_A_log)
    a_row = jnp.pad(a_neg, ((0, 0), (0, LANES - SSM_HEADS)))
    a_col = jnp.broadcast_to(a_neg.reshape(SSM_HEADS, 1), (SSM_HEADS, LANES))
    d_exp = jnp.repeat(ssm_D, SSM_P, axis=1)
    a_e = jnp.repeat(a_neg, SSM_P, axis=1)
    head_ind = (jnp.arange(D_SSM)[:, None] // SSM_P == jnp.arange(LANES)[None, :]).astype(f32)

    xb = x[0]
    h0 = jnp.concatenate([meta_full, xb, jnp.zeros((Lp - n_real, D), f32)], axis=0)
    tgt = jnp.pad(loss_target[0], ((N_META, Lp - n_real), (0, 0)))
    hn1, q_c, kv_c, kpe_raw, z, xbc, dt_raw = _in_proj(h0, norm_mix_pre, [w_q, w_kv, w_rope, w_z, w_xbc, w_dt])

    qn, qh = _up_q_rope(q_c, q_a_norm, wuq, cos_t, sin_t)
    kvn, kh, vh = _up_kv_rope(kv_c, kv_a_norm, wukv, kpe_raw, cos_t, sin_t)
    attn, lse, wl_all, wup_all = _flash_fwd(qh, kh, vh, [wl, w_up[0].astype(bf16)], False)
    g_w_out, g_w_down = _unpack(wl_all, [a.shape for a in late_w], 1)
    wout = g_w_out.reshape(D_ATTN + D_SSM, D)
    wout_a, wout_s = wout[:D_ATTN], wout[D_ATTN:]
    wup = cols(wup_all[:, None])
    wdown = g_w_down.reshape(D_FF, D)
    an = _rms_fwd(attn, attn_out_norm, bf16, "norm_attn_out")

    xbc_c = _ssm_conv_fwd(xbc, sconv_w, ssm_conv_b)
    dtp, dt_e = _dt_fwd(dt_raw, dt_bias_p, jnp.transpose(head_ind))
    dtt = jnp.transpose(dtp[:, :SSM_HEADS])
    y_ssd, hin = _ssd_fwd(xbc_c, dt_e, dtt, a_e, a_col)
    ssm = _gate_norm_fwd(y_ssd, xbc_c, z, d_exp, ssm_norm)

    mix = _mm([(an, wout_a), (ssm, wout_s)], f32, False, "out_proj")
    h1, hn2 = _resid_norm(h0, mix, norm_mix_post, norm_ffn_pre)
    up = _mm([(hn2, wup)], f32, False, "ffn_up")
    act = _ffn_gate_fwd(up, fconv_w, ffn_conv_b)
    dh2, d_down, dg_ffn_post, loss_part = _final(h1, act, wdown, norm_ffn_post, tgt, n_real)

    d_act = _mm([(d_down, wdown)], f32, True, "ffn_down_dx")
    dw_down = _mm_tn(act, d_down, "ffn_down_dw")
    dup_g, dup_v, dwc_g, dwc_v, dbc_g, dbc_v = _ffn_gate_bwd(up, fconv_w, ffn_conv_b, d_act)
    dw_up = jnp.concatenate([_mm_tn(hn2, dup_g, "ffn_up_dw_g"), _mm_tn(hn2, dup_v, "ffn_up_dw_v")], axis=1)
    dh1, d_mix, dg_ffn_pre, dg_mix_post = _mid_bwd(h1, norm_ffn_pre, dup_g, dup_v, wup, dh2, mix, norm_mix_post)
    d_an = _mm([(d_mix, wout_a)], f32, True, "out_proj_dx_a")
    d_ssm = _mm([(d_mix, wout_s)], f32, True, "out_proj_dx_s")
    dw_out = jnp.concatenate([_mm_tn(an, d_mix, "out_proj_dw_a"), _mm_tn(ssm, d_mix, "out_proj_dw_s")], axis=0)

    do_h, delta, dg_attn_out = _attn_out_bwd(attn, attn_out_norm, d_an)
    def col_blocks(gm):
        r, cc = gm.shape
        return jnp.transpose(gm.reshape(r, N_DEV, cc // N_DEV), (1, 0, 2))

    blocks_a = [dw_out.reshape(N_DEV, (D_ATTN + D_SSM) // N_DEV, D), dw_down.reshape(N_DEV, D_FF // N_DEV, D)]
    gpack_a = _pack(blocks_a, 1, pack_rows(blocks_a, 1), bf16)
    dqh, dkh, dvh, gparts_a, gparts_up = _flash_bwd(qh, kh, vh, do_h, lse, delta,
                                                    [gpack_a, col_blocks(dw_up).astype(bf16)], True)
    d_q_c, dg_q, dw_uq = _q_branch_bwd(dqh, cos_t, sin_t, wuq, q_c, q_a_norm, qn)
    d_kv_c, dg_kv, dw_ukv, d_kpe_raw = _kv_branch_bwd(dkh, dvh, cos_t, sin_t, wukv, kv_c, kv_a_norm, kvn)

    dy_ssd, dz, dg_ssm, dd_heads = _gate_norm_bwd(y_ssd, xbc_c, z, d_exp, ssm_norm, d_ssm, head_ind)
    d_xbc_c, ddt, da_heads = _ssd_bwd(xbc_c, dtp, dt_e, dtt, a_row, a_e, a_col, hin, dy_ssd, d_exp, head_ind)
    d_xbc, dw_sconv, db_sconv = _ssm_conv_bwd(xbc, sconv_w, ssm_conv_b, d_xbc_c)
    d_dt_raw, d_dt_bias = _dt_bwd(dt_raw, dt_bias_p, ddt)

    dw_q, dw_kv, dw_rope, dw_z, dw_xbc, dw_dt = _in_proj_dw(hn1, [d_q_c, d_kv_c, d_kpe_raw, dz, d_xbc, d_dt_raw])
    dw_in = jnp.concatenate([dw_q, dw_kv, dw_rope[:, :QK_ROPE], dw_z, dw_xbc, dw_dt[:, :SSM_HEADS]], axis=1)
    dw_uq3 = dw_uq.reshape(Q_RANK, MLA_HEADS, QK_PAD)[:, :, :QK_NOPE + QK_ROPE]
    blocks_b = [
        dw_uq3.reshape(N_DEV, Q_RANK // N_DEV, MLA_HEADS, QK_NOPE + QK_ROPE),
        dw_ukv.reshape(N_DEV, KV_RANK // N_DEV, MLA_HEADS, QK_NOPE + V_DIM),
        col_blocks(dw_sconv[:SSM_CONV]),
        col_blocks(jnp.concatenate([dwc_g, dwc_v], axis=1)[:FFN_CONV]),
    ]
    gpack_b = _pack(blocks_b, 1, pack_rows(blocks_b, 1), bf16)
    segs = [(d_q_c, w_q), (d_kv_c, w_kv), (d_kpe_raw, w_rope), (dz, w_z), (d_xbc, w_xbc), (d_dt_raw, w_dt)]
    d_hn1, gparts_b, gparts_in = _mm(segs, f32, True, "proj_dx",
                                     carried=[gpack_b, col_blocks(dw_in).astype(bf16)], scatter=True)
    dh0, dg_mix_pre = _rms_bwd(h0, norm_mix_pre, d_hn1, f32, "norm_mix_pre_bwd", residual=dh1)

    grad_x = dh0[N_META:n_real][None]
    meta_blocks = col_blocks(dh0[:N_META]).reshape(N_DEV, N_META * D // N_DEV // PACK_W, PACK_W)


    def adam_group(parts, grp, name):
        rows = parts.shape[1]
        packs = [_pack([a[None] for a in grp[k]], 1, rows, f32)[0] for k in ("w", "m", "v")]
        outs = _adamw(parts, *packs, name)
        shapes = [a.shape for a in grp["w"]]
        return [dict(zip(grp["names"], [t[0] for t in _unpack(b[None], shapes, 1)])) for b in outs]

    def adam_own_layout(parts, name, w, m, v):
        return [{name: t[None]} for t in _adamw(parts, w[0], m[0], v[0], "adamw_" + name)]

    sh_a = adam_group(gparts_a, grp_a, "adamw_sharded_a")
    sh_b = adam_group(gparts_b, grp_b, "adamw_sharded_b")
    sh_in = adam_own_layout(gparts_in, "w_in", w_in, m_w_in, v_w_in)
    sh_up = adam_own_layout(gparts_up, "w_up", w_up, m_w_up, v_w_up)

    dg_alog = da_heads[:, :SSM_HEADS] * a_neg
    repl_g = [dg_mix_pre, dg_mix_post, dg_ffn_pre, dg_ffn_post, dg_q, dg_kv, dg_attn_out, db_sconv,
              d_dt_bias[:, :SSM_HEADS], dg_alog, dd_heads[:, :SSM_HEADS], dg_ssm,
              jnp.concatenate([dbc_g, dbc_v], axis=1)]
    loss_vec = loss_part[:, :1]
    small_total = _round_up(sum(-(-int(np.prod(a.shape)) // PACK_W) for a in repl_g) + 1, 16)
    spack = _pack(repl_g + [loss_vec], 0, small_total, f32)
    gparts_meta, sparts = _exchange_tail([meta_blocks], [spack], "exchange_tail")
    sh_meta = adam_group(gparts_meta, grp_meta, "adamw_meta")
    loss_row, repl_out = _adamw_replicated(sparts, repl_w, repl_m, repl_v)
    loss = loss_row[0, 0]

    order = ["meta_tokens", "norm_mix_pre", "norm_mix_post", "norm_ffn_pre", "norm_ffn_post", "w_in", "q_a_norm",
             "w_uq", "kv_a_norm", "w_ukv", "attn_out_norm", "ssm_conv_w", "ssm_conv_b", "ssm_dt_bias", "ssm_A_log",
             "ssm_D", "ssm_norm", "w_out", "w_up", "ffn_conv_w", "ffn_conv_b", "w_down"]
    rp_names = ["norm_mix_pre", "norm_mix_post", "norm_ffn_pre", "norm_ffn_post", "q_a_norm", "kv_a_norm",
                "attn_out_norm", "ssm_conv_b", "ssm_dt_bias", "ssm_A_log", "ssm_D", "ssm_norm", "ffn_conv_b"]

    def lookup(k):
        d = {**sh_a[k], **sh_b[k], **sh_in[k], **sh_up[k], **sh_meta[k],
             **{n: four[k] for n, four in zip(rp_names, repl_out)}}
        return [d[n] for n in order]

    return (loss, grad_x, *lookup(0), *lookup(1), *lookup(2), *lookup(3))
```

```python
import functools
import math

import jax
import jax.numpy as jnp
import numpy as np
from jax import lax
from jax.experimental import pallas as pl
from jax.experimental.pallas import tpu as pltpu

f32 = jnp.float32
bf16 = jnp.bfloat16

D_MODEL = 1024
SEQ = 8192
N_META = 16
MLA_HEADS = 8
QK_NOPE = 128
QK_ROPE = 64
V_DIM = 128
Q_RANK = 384
KV_RANK = 256
ROPE_THETA = 10000.0
SOFTMAX_SCALE = (QK_NOPE + QK_ROPE) ** -0.5
D_ATTN = MLA_HEADS * V_DIM
SSM_HEADS = 16
SSM_P = 64
SSM_GROUPS = 2
SSM_HPG = SSM_HEADS // SSM_GROUPS
SSM_N = 128
SSM_CONV = 4
CHUNK = 128
D_SSM = SSM_HEADS * SSM_P
D_BC = SSM_GROUPS * SSM_N
D_XBC = D_SSM + 2 * D_BC
D_FF = 2816
FFN_CONV = 3
EPS = 1e-6
D_IN = Q_RANK + KV_RANK + QK_ROPE + D_SSM + D_XBC + SSM_HEADS
QK_PAD = 256
N_DEV = 8

ADAM_LR = 0.001
ADAM_B1 = 0.9
ADAM_B2 = 0.999
ADAM_EPS = 1e-08
ADAM_WD = 0.01
ADAM_STEP = 10

LANES = 128
SUBLANES = 8
ROW_TILE = 256
VMEM_LIMIT = 56 * 1024 * 1024
PACK_W = 1024
PACK_ROW_TILE = 128
NEG = -1e30
LOG2E = math.log2(math.e)
LN2 = math.log(2.0)
Q_PRESCALE = SOFTMAX_SCALE * LOG2E

_MESH = pl.DeviceIdType.MESH


def _pick(n, prefs):
    for p in prefs:
        if n % p == 0:
            return p
    return n


def _rt(m):
    return _pick(m, (384, ROW_TILE))


def _cparams(sem):
    return pltpu.CompilerParams(dimension_semantics=sem, vmem_limit_bytes=VMEM_LIMIT)


def _row(spec_cols, tm):
    return pl.BlockSpec((tm, spec_cols), lambda i: (i, 0))


def _full(shape):
    nd = len(shape)
    return pl.BlockSpec(shape, lambda *a: (0,) * nd)


def _sigmoid(x):
    return 1.0 / (1.0 + jnp.exp(-x))


def _silu(x):
    return x * _sigmoid(x)


def _dsilu(x):
    s = _sigmoid(x)
    return s * (1.0 + x * (1.0 - s))


def _dot(a, b):
    return jnp.dot(a, b, preferred_element_type=f32)


def _dot_nt(a, b):
    return lax.dot_general(a, b, (((1,), (1,)), ((), ())), preferred_element_type=f32)


def _dot_tn(a, b):
    return lax.dot_general(a, b, (((0,), (0,)), ((), ())), preferred_element_type=f32)


def _dot_hi(a, b):
    return jnp.dot(a, b, precision=lax.Precision.HIGHEST, preferred_element_type=f32)


def _mm(pairs, out_dtype, trans_b, name, carried=(), scatter=False):
    n = len(pairs)
    nx = len(carried)
    M = pairs[0][0].shape[0]
    N = pairs[0][1].shape[0] if trans_b else pairs[0][1].shape[1]
    tm = _pick(M, (768, 512, 256))
    tn = _pick(N, (512, 1408, 384, 256, 128))
    ni, nj = M // tm, N // tn

    def body(*refs):
        o_ref = refs[2 * n + nx]
        if nx:
            i, j = pl.program_id(0), pl.program_id(1)
            _hosted_exchange(refs[2 * n:2 * n + nx], refs[2 * n + nx + 1:2 * n + 2 * nx + 1],
                             refs[2 * n + 2 * nx + 1:], scatter,
                             (i == 0) & (j == 0), (i == ni - 1) & (j == nj - 1))
        acc = None
        for p in range(n):
            a = refs[2 * p][...].astype(bf16)
            b = refs[2 * p + 1][...].astype(bf16)
            r = _dot_nt(a, b) if trans_b else _dot(a, b)
            acc = r if acc is None else acc + r
        o_ref[...] = acc.astype(out_dtype)

    in_specs, args = [], []
    for a, b in pairs:
        k = a.shape[1]
        in_specs.append(pl.BlockSpec((tm, k), lambda i, j: (i, 0)))
        if trans_b:
            in_specs.append(pl.BlockSpec((tn, k), lambda i, j: (j, 0)))
        else:
            in_specs.append(pl.BlockSpec((k, tn), lambda i, j: (0, j)))
        args += [a, b]
    out_spec = pl.BlockSpec((tm, tn), lambda i, j: (i, j))
    out_shape = jax.ShapeDtypeStruct((M, N), out_dtype)
    if not nx:
        return pl.pallas_call(
            body, name=name, grid=(ni, nj), in_specs=in_specs, out_specs=out_spec, out_shape=out_shape,
            compiler_params=_cparams(("parallel", "parallel")),
        )(*args)
    any_spec = pl.BlockSpec(memory_space=pl.ANY)
    return pl.pallas_call(
        body, name=name, grid=(ni, nj), in_specs=in_specs + [any_spec] * nx,
        out_specs=[out_spec] + [any_spec] * nx,
        out_shape=[out_shape] + _exchange_shapes(carried, scatter),
        scratch_shapes=_exchange_sems(nx),
        compiler_params=_cparams(("arbitrary", "arbitrary")),
    )(*args, *carried)


def _mm_tn(a, g, name):
    M, K = a.shape
    N = g.shape[1]
    tm = _pick(M, (768, 512, 256))
    tk = _pick(K, (1024, 1408, 512, 384, 256))
    tn = _pick(N, (1024, 1408, 512, 384, 256, 128))

    def body(a_ref, g_ref, o_ref):
        @pl.when(pl.program_id(2) == 0)
        def _():
            o_ref[...] = jnp.zeros_like(o_ref)

        o_ref[...] += _dot_tn(a_ref[...].astype(bf16), g_ref[...].astype(bf16))

    return pl.pallas_call(
        body, name=name, grid=(K // tk, N // tn, M // tm),
        in_specs=[pl.BlockSpec((tm, tk), lambda k, j, m: (m, k)),
                  pl.BlockSpec((tm, tn), lambda k, j, m: (m, j))],
        out_specs=pl.BlockSpec((tk, tn), lambda k, j, m: (k, j)),
        out_shape=jax.ShapeDtypeStruct((K, N), f32),
        compiler_params=_cparams(("parallel", "parallel", "arbitrary")),
    )(a, g)


def _rstd(x):
    return lax.rsqrt(jnp.mean(x * x, axis=-1, keepdims=True) + EPS)


def _rms_bwd_math(x, g, dy):
    r = _rstd(x)
    xh = x * r
    dn = dy * g
    dx = r * (dn - xh * jnp.mean(dn * xh, axis=-1, keepdims=True))
    return dx, dy * xh


def _rms_fwd(x, g, out_dtype, name):
    M, K = x.shape
    tm = _rt(M)

    def body(x_ref, g_ref, o_ref):
        xv = x_ref[...]
        o_ref[...] = (xv * _rstd(xv) * g_ref[...]).astype(out_dtype)

    return pl.pallas_call(
        body, name=name, grid=(M // tm,), in_specs=[_row(K, tm), _full((1, K))],
        out_specs=_row(K, tm), out_shape=jax.ShapeDtypeStruct((M, K), out_dtype),
        compiler_params=_cparams(("parallel",)),
    )(x, g)


def _rms_bwd(x, g, dy, out_dtype, name, residual=None):
    M, K = x.shape
    tm = _rt(M)
    has_res = residual is not None

    def body(*refs):
        if has_res:
            x_ref, g_ref, dy_ref, r_ref, dx_ref, dg_ref = refs
        else:
            x_ref, g_ref, dy_ref, dx_ref, dg_ref = refs

        @pl.when(pl.program_id(0) == 0)
        def _():
            dg_ref[...] = jnp.zeros_like(dg_ref)

        dx, dgp = _rms_bwd_math(x_ref[...], g_ref[...], dy_ref[...].astype(f32))
        if has_res:
            dx = dx + r_ref[...]
        dx_ref[...] = dx.astype(out_dtype)
        dg_ref[...] += jnp.sum(dgp, axis=0, keepdims=True)

    ins = [x, g, dy] + ([residual] if has_res else [])
    in_specs = [_row(K, tm), _full((1, K)), _row(K, tm)] + ([_row(K, tm)] if has_res else [])
    return pl.pallas_call(
        body, name=name, grid=(M // tm,), in_specs=in_specs,
        out_specs=[_row(K, tm), _full((1, K))],
        out_shape=[jax.ShapeDtypeStruct((M, K), out_dtype), jax.ShapeDtypeStruct((1, K), f32)],
        compiler_params=_cparams(("arbitrary",)),
    )(*ins)


def _in_proj(h0, g, weights):
    M, K = h0.shape
    tm = _rt(M)
    n = len(weights)
    widths = [int(w.shape[1]) for w in weights]

    def body(x_ref, g_ref, *refs):
        xv = x_ref[...]
        hn = (xv * _rstd(xv) * g_ref[...]).astype(bf16)
        refs[n][...] = hn
        for p in range(n):
            refs[n + 1 + p][...] = _dot(hn, refs[p][...])

    return pl.pallas_call(
        body, name="in_proj", grid=(M // tm,),
        in_specs=[_row(K, tm), _full((1, K))] + [_full((K, wd)) for wd in widths],
        out_specs=[_row(K, tm)] + [_row(wd, tm) for wd in widths],
        out_shape=[jax.ShapeDtypeStruct((M, K), bf16)] + [jax.ShapeDtypeStruct((M, wd), f32) for wd in widths],
        compiler_params=_cparams(("parallel",)),
    )(h0, g, *weights)


def _in_proj_dw(hn, grads):
    M, K = hn.shape
    tm = _rt(M)
    n = len(grads)
    widths = [int(gr.shape[1]) for gr in grads]

    def body(a_ref, *refs):
        @pl.when(pl.program_id(0) == 0)
        def _():
            for p in range(n):
                refs[n + p][...] = jnp.zeros_like(refs[n + p])

        a = a_ref[...]
        for p in range(n):
            refs[n + p][...] += _dot_tn(a, refs[p][...].astype(bf16))

    return pl.pallas_call(
        body, name="in_proj_dw", grid=(M // tm,),
        in_specs=[_row(K, tm)] + [_row(wd, tm) for wd in widths],
        out_specs=[_full((K, wd)) for wd in widths],
        out_shape=[jax.ShapeDtypeStruct((K, wd), f32) for wd in widths],
        compiler_params=_cparams(("arbitrary",)),
    )(hn, *grads)


def _in_proj_dx(grads, weights, h0, g, dh1, carried, scatter):
    M, K = h0.shape
    tm = _rt(M)
    nt = M // tm
    n = len(grads)
    nx = len(carried)
    widths = [int(w.shape[1]) for w in weights]

    def body(*refs):
        g_refs, w_refs = refs[:n], refs[n:2 * n]
        x_ref, gain_ref, r_ref = refs[2 * n:2 * n + 3]
        cin = refs[2 * n + 3:2 * n + 3 + nx]
        dx_ref, dg_ref = refs[2 * n + 3 + nx:2 * n + 5 + nx]
        cout = refs[2 * n + 5 + nx:2 * n + 5 + 2 * nx]
        i = pl.program_id(0)
        _hosted_exchange(cin, cout, refs[2 * n + 5 + 2 * nx:], scatter, i == 0, i == nt - 1)

        @pl.when(i == 0)
        def _():
            dg_ref[...] = jnp.zeros_like(dg_ref)

        d_hn = None
        for p in range(n):
            t = _dot_nt(g_refs[p][...].astype(bf16), w_refs[p][...])
            d_hn = t if d_hn is None else d_hn + t
        dx, dgp = _rms_bwd_math(x_ref[...], gain_ref[...], d_hn)
        dx_ref[...] = dx + r_ref[...]
        dg_ref[...] += jnp.sum(dgp, axis=0, keepdims=True)

    any_spec = pl.BlockSpec(memory_space=pl.ANY)
    return pl.pallas_call(
        body, name="in_proj_dx", grid=(nt,),
        in_specs=([_row(wd, tm) for wd in widths] + [_full((K, wd)) for wd in widths]
                  + [_row(K, tm), _full((1, K)), _row(K, tm)] + [any_spec] * nx),
        out_specs=[_row(K, tm), _full((1, K))] + [any_spec] * nx,
        out_shape=[jax.ShapeDtypeStruct((M, K), f32), jax.ShapeDtypeStruct((1, K), f32)]
        + _exchange_shapes(carried, scatter),
        scratch_shapes=_exchange_sems(nx),
        compiler_params=_cparams(("arbitrary",)),
    )(*grads, *weights, h0, g, dh1, *carried)


def _resid_norm(h0, mix, g2, g3):
    M, K = h0.shape
    tm = _rt(M)

    def body(h_ref, m_ref, g2_ref, g3_ref, h1_ref, hn_ref):
        mv = m_ref[...]
        h1 = h_ref[...] + mv * _rstd(mv) * g2_ref[...]
        h1_ref[...] = h1
        hn_ref[...] = (h1 * _rstd(h1) * g3_ref[...]).astype(bf16)

    return pl.pallas_call(
        body, name="resid_norm", grid=(M // tm,),
        in_specs=[_row(K, tm), _row(K, tm), _full((1, K)), _full((1, K))],
        out_specs=[_row(K, tm), _row(K, tm)],
        out_shape=[jax.ShapeDtypeStruct((M, K), f32), jax.ShapeDtypeStruct((M, K), bf16)],
        compiler_params=_cparams(("parallel",)),
    )(h0, mix, g2, g3)


def _final(h1, act, wdown, g4, tgt, n_real):
    M, K = h1.shape
    F = act.shape[1]
    tm = _rt(M)
    nt = M // tm

    def body(h_ref, a_ref, w_ref, g_ref, t_ref, dh_ref, dd_ref, dg_ref, ls_ref, acc_ref):
        i = pl.program_id(0)

        @pl.when(i == 0)
        def _():
            dg_ref[...] = jnp.zeros_like(dg_ref)
            acc_ref[...] = jnp.zeros_like(acc_ref)

        dv = _dot(a_ref[...], w_ref[...])
        g = g_ref[...]
        r = _rstd(dv)
        n = dv * r
        h2 = h_ref[...] + n * g
        rows = i * tm + lax.broadcasted_iota(jnp.int32, (tm, 1), 0)
        mask = ((rows >= N_META) & (rows < n_real)).astype(f32)
        diff = (h2 - t_ref[...]) * mask
        acc_ref[...] += jnp.sum(diff * diff, axis=0, keepdims=True)
        dh = diff * (1.0 / K)
        dh_ref[...] = dh
        dn = dh * g
        dd_ref[...] = (r * (dn - n * jnp.mean(dn * n, axis=-1, keepdims=True))).astype(bf16)
        dg_ref[...] += jnp.sum(dh * n, axis=0, keepdims=True)

        @pl.when(i == nt - 1)
        def _():
            ls_ref[...] = jnp.zeros((1, LANES), f32) + jnp.sum(acc_ref[...]) * (0.5 / K)

    return pl.pallas_call(
        body, name="ffn_down_loss", grid=(nt,),
        in_specs=[_row(K, tm), _row(F, tm), _full((F, K)), _full((1, K)), _row(K, tm)],
        out_specs=[_row(K, tm), _row(K, tm), _full((1, K)), _full((1, LANES))],
        out_shape=[jax.ShapeDtypeStruct((M, K), f32), jax.ShapeDtypeStruct((M, K), bf16),
                   jax.ShapeDtypeStruct((1, K), f32), jax.ShapeDtypeStruct((1, LANES), f32)],
        scratch_shapes=[pltpu.VMEM((1, K), f32)],
        compiler_params=_cparams(("arbitrary",)),
    )(h1, act, wdown, g4, tgt)


def _mid_bwd(h1, g3, dup_g, dup_v, wup, dh2, mix, g2):
    M, K = h1.shape
    F = dup_g.shape[1]
    tm = ROW_TILE

    def body(h_ref, g3_ref, ag_ref, av_ref, w_ref, dh2_ref, m_ref, g2_ref, dh1_ref, dm_ref, dg3_ref, dg2_ref):
        @pl.when(pl.program_id(0) == 0)
        def _():
            dg3_ref[...] = jnp.zeros_like(dg3_ref)
            dg2_ref[...] = jnp.zeros_like(dg2_ref)

        d_hn2 = _dot_nt(ag_ref[...], w_ref[:, 0:F]) + _dot_nt(av_ref[...], w_ref[:, F:2 * F])
        dx, dgp = _rms_bwd_math(h_ref[...], g3_ref[...], d_hn2)
        dh1 = dh2_ref[...] + dx
        dh1_ref[...] = dh1
        dg3_ref[...] += jnp.sum(dgp, axis=0, keepdims=True)
        dm, dgp2 = _rms_bwd_math(m_ref[...], g2_ref[...], dh1)
        dm_ref[...] = dm.astype(bf16)
        dg2_ref[...] += jnp.sum(dgp2, axis=0, keepdims=True)

    return pl.pallas_call(
        body, name="ffn_up_dx_mid_bwd", grid=(M // tm,),
        in_specs=[_row(K, tm), _full((1, K)), _row(F, tm), _row(F, tm), _full((K, 2 * F)), _row(K, tm),
                  _row(K, tm), _full((1, K))],
        out_specs=[_row(K, tm), _row(K, tm), _full((1, K)), _full((1, K))],
        out_shape=[jax.ShapeDtypeStruct((M, K), f32), jax.ShapeDtypeStruct((M, K), bf16),
                   jax.ShapeDtypeStruct((1, K), f32), jax.ShapeDtypeStruct((1, K), f32)],
        compiler_params=_cparams(("arbitrary",)),
    )(h1, g3, dup_g, dup_v, wup, dh2, mix, g2)


HEADS_PER_STEP = 4
CONV_RB = 16


def _conv_block_taps(x_ref, halo, rb, lanes, kw):
    r0 = rb * CONV_RB
    if rb == 0:
        cat = jnp.concatenate([halo, x_ref[0:CONV_RB, lanes]], axis=0)
        first = SUBLANES - (kw - 1)
        return [cat[first + k:first + k + CONV_RB] for k in range(kw)]
    return [x_ref[r0 - (kw - 1) + k:r0 - (kw - 1) + k + CONV_RB, lanes] for k in range(kw)]


def _conv_weighted(taps, w, kw):
    u = None
    for k in range(kw):
        t = taps[k] * w[k:k + 1, :]
        u = t if u is None else u + t
    return u


def _conv_block_dx(du, nxt, w, kw):
    cat = jnp.concatenate([du, nxt], axis=0)
    return _conv_weighted([cat[kw - 1 - k:kw - 1 - k + CONV_RB] for k in range(kw)], w, kw)


def _prev_spec(tm, tc, col_of, row_axis, reversed_tiles=0):
    def imap(*ids):
        i = ids[row_axis]
        if reversed_tiles:
            i = reversed_tiles - 1 - i
        return (jnp.maximum(i * (tm // SUBLANES) - 1, 0), col_of(*ids))
    return pl.BlockSpec((SUBLANES, tc), imap)


def _ssm_conv_fwd(xbc, w, b):
    M, C = xbc.shape
    tm, tc, kw = ROW_TILE, C, SSM_CONV

    def body(x_ref, h_ref, w_ref, b_ref, o_ref):
        i = pl.program_id(0)

        def chunk(j, carry):
            lanes = pl.ds(pl.multiple_of(j * LANES, LANES), LANES)
            halo = jnp.where(i == 0, 0.0, h_ref[:, lanes])
            wv = w_ref[:, lanes]
            bv = b_ref[:, lanes]
            for rb in range(tm // CONV_RB):
                u = _conv_weighted(_conv_block_taps(x_ref, halo, rb, lanes, kw), wv, kw) + bv
                o_ref[rb * CONV_RB:(rb + 1) * CONV_RB, lanes] = _silu(u)
            return carry

        lax.fori_loop(0, tc // LANES, chunk, 0)

    return pl.pallas_call(
        body, name="ssm_conv_fwd", grid=(M // tm, C // tc),
        in_specs=[pl.BlockSpec((tm, tc), lambda i, j: (i, j)),
                  _prev_spec(tm, tc, lambda i, j: j, 0),
                  pl.BlockSpec((SUBLANES, tc), lambda i, j: (0, j)),
                  pl.BlockSpec((1, tc), lambda i, j: (0, j))],
        out_specs=pl.BlockSpec((tm, tc), lambda i, j: (i, j)),
        out_shape=jax.ShapeDtypeStruct((M, C), f32),
        compiler_params=_cparams(("parallel", "parallel")),
    )(xbc, xbc, w, b)


def _ssm_conv_bwd(xbc, w, b, dout):
    M, C = xbc.shape
    tm, tc, kw = ROW_TILE, C // 3, SSM_CONV
    nt = M // tm

    def body(x_ref, h_ref, w_ref, b_ref, d_ref, dx_ref, dw_ref, db_ref, nxt_ref):
        i = pl.program_id(1)

        @pl.when(i == 0)
        def _():
            dw_ref[...] = jnp.zeros_like(dw_ref)
            db_ref[...] = jnp.zeros_like(db_ref)
            nxt_ref[...] = jnp.zeros_like(nxt_ref)

        def chunk(j, carry):
            lanes = pl.ds(pl.multiple_of(j * LANES, LANES), LANES)
            halo = jnp.where(i == nt - 1, 0.0, h_ref[:, lanes])
            wv = w_ref[:, lanes]
            bv = b_ref[:, lanes]
            nxt = nxt_ref[:, lanes]
            db = jnp.zeros((CONV_RB, LANES), f32)
            dw = [jnp.zeros((CONV_RB, LANES), f32) for _ in range(kw)]
            for rb in reversed(range(tm // CONV_RB)):
                rows = slice(rb * CONV_RB, (rb + 1) * CONV_RB)
                taps = _conv_block_taps(x_ref, halo, rb, lanes, kw)
                du = d_ref[rows, lanes] * _dsilu(_conv_weighted(taps, wv, kw) + bv)
                db = db + du
                dw = [dw[k] + du * taps[k] for k in range(kw)]
                dx_ref[rows, lanes] = _conv_block_dx(du, nxt, wv, kw).astype(bf16)
                nxt = du[0:SUBLANES]
            nxt_ref[:, lanes] = nxt
            db_ref[:, lanes] += jnp.sum(db, axis=0, keepdims=True)
            for k in range(kw):
                dw_ref[k:k + 1, lanes] += jnp.sum(dw[k], axis=0, keepdims=True)
            return carry

        lax.fori_loop(0, tc // LANES, chunk, 0)

    tile = pl.BlockSpec((tm, tc), lambda j, i: (nt - 1 - i, j))
    return pl.pallas_call(
        body, name="ssm_conv_bwd", grid=(C // tc, nt),
        in_specs=[tile, _prev_spec(tm, tc, lambda j, i: j, 1, nt),
                  pl.BlockSpec((SUBLANES, tc), lambda j, i: (0, j)),
                  pl.BlockSpec((1, tc), lambda j, i: (0, j)), tile],
        out_specs=[tile, pl.BlockSpec((SUBLANES, tc), lambda j, i: (0, j)),
                   pl.BlockSpec((1, tc), lambda j, i: (0, j))],
        out_shape=[jax.ShapeDtypeStruct((M, C), bf16), jax.ShapeDtypeStruct((SUBLANES, C), f32),
                   jax.ShapeDtypeStruct((1, C), f32)],
        scratch_shapes=[pltpu.VMEM((SUBLANES, tc), f32)],
        compiler_params=_cparams(("parallel", "arbitrary")),
    )(xbc, xbc, w, b, dout)


def _ffn_gate_fwd(up, w, b):
    M = up.shape[0]
    tm, tc, kw = ROW_TILE, D_FF // 2, FFN_CONV
    nc = D_FF // tc

    def body(xg_ref, hg_ref, xv_ref, hv_ref, wg_ref, wv_ref, bg_ref, bv_ref, o_ref):
        i = pl.program_id(0)

        def chunk(j, carry):
            lanes = pl.ds(pl.multiple_of(j * LANES, LANES), LANES)
            halo_g = jnp.where(i == 0, 0.0, hg_ref[:, lanes])
            halo_v = jnp.where(i == 0, 0.0, hv_ref[:, lanes])
            wg, wv = wg_ref[:, lanes], wv_ref[:, lanes]
            bg, bv = bg_ref[:, lanes], bv_ref[:, lanes]
            for rb in range(tm // CONV_RB):
                ug = _conv_weighted(_conv_block_taps(xg_ref, halo_g, rb, lanes, kw), wg, kw) + bg
                uv = _conv_weighted(_conv_block_taps(xv_ref, halo_v, rb, lanes, kw), wv, kw) + bv
                o_ref[rb * CONV_RB:(rb + 1) * CONV_RB, lanes] = (_silu(ug) * uv).astype(bf16)
            return carry

        lax.fori_loop(0, tc // LANES, chunk, 0)

    return pl.pallas_call(
        body, name="ffn_gate_fwd", grid=(M // tm, nc),
        in_specs=[pl.BlockSpec((tm, tc), lambda i, j: (i, j)),
                  _prev_spec(tm, tc, lambda i, j: j, 0),
                  pl.BlockSpec((tm, tc), lambda i, j: (i, j + nc)),
                  _prev_spec(tm, tc, lambda i, j: j + nc, 0),
                  pl.BlockSpec((SUBLANES, tc), lambda i, j: (0, j)),
                  pl.BlockSpec((SUBLANES, tc), lambda i, j: (0, j + nc)),
                  pl.BlockSpec((1, tc), lambda i, j: (0, j)),
                  pl.BlockSpec((1, tc), lambda i, j: (0, j + nc))],
        out_specs=pl.BlockSpec((tm, tc), lambda i, j: (i, j)),
        out_shape=jax.ShapeDtypeStruct((M, D_FF), bf16),
        compiler_params=_cparams(("parallel", "parallel")),
    )(up, up, up, up, w, w, b, b)


def _ffn_gate_bwd(up, w, b, d_down, wdown):
    M = up.shape[0]
    K = d_down.shape[1]
    tm, tc, kw = ROW_TILE, D_FF // 2, FFN_CONV
    nc = D_FF // tc
    nt = M // tm

    def body(xg_ref, hg_ref, xv_ref, hv_ref, wg_ref, wv_ref, bg_ref, bv_ref, dd_ref, wd_ref,
             dxg_ref, dxv_ref, dwg_ref, dwv_ref, dbg_ref, dbv_ref, ng_ref, nv_ref, d_ref):
        i = pl.program_id(1)

        @pl.when(i == 0)
        def _():
            for r in (dwg_ref, dwv_ref, dbg_ref, dbv_ref, ng_ref, nv_ref):
                r[...] = jnp.zeros_like(r)

        d_ref[...] = _dot_nt(dd_ref[...], wd_ref[...])

        def chunk(j, carry):
            lanes = pl.ds(pl.multiple_of(j * LANES, LANES), LANES)
            halo_g = jnp.where(i == nt - 1, 0.0, hg_ref[:, lanes])
            halo_v = jnp.where(i == nt - 1, 0.0, hv_ref[:, lanes])
            wg, wv = wg_ref[:, lanes], wv_ref[:, lanes]
            bg, bv = bg_ref[:, lanes], bv_ref[:, lanes]
            nxt_g, nxt_v = ng_ref[:, lanes], nv_ref[:, lanes]
            zero = jnp.zeros((CONV_RB, LANES), f32)
            dbg, dbv = zero, zero
            dwg = [zero for _ in range(kw)]
            dwv = [zero for _ in range(kw)]
            for rb in reversed(range(tm // CONV_RB)):
                rows = slice(rb * CONV_RB, (rb + 1) * CONV_RB)
                tg = _conv_block_taps(xg_ref, halo_g, rb, lanes, kw)
                tv = _conv_block_taps(xv_ref, halo_v, rb, lanes, kw)
                ug = _conv_weighted(tg, wg, kw) + bg
                uv = _conv_weighted(tv, wv, kw) + bv
                sg = _sigmoid(ug)
                da = d_ref[rows, lanes]
                dug = da * uv * (sg * (1.0 + ug * (1.0 - sg)))
                duv = da * (ug * sg)
                dbg = dbg + dug
                dbv = dbv + duv
                dwg = [dwg[k] + dug * tg[k] for k in range(kw)]
                dwv = [dwv[k] + duv * tv[k] for k in range(kw)]
                dxg_ref[rows, lanes] = _conv_block_dx(dug, nxt_g, wg, kw).astype(bf16)
                dxv_ref[rows, lanes] = _conv_block_dx(duv, nxt_v, wv, kw).astype(bf16)
                nxt_g, nxt_v = dug[0:SUBLANES], duv[0:SUBLANES]
            ng_ref[:, lanes] = nxt_g
            nv_ref[:, lanes] = nxt_v
            dbg_ref[:, lanes] += jnp.sum(dbg, axis=0, keepdims=True)
            dbv_ref[:, lanes] += jnp.sum(dbv, axis=0, keepdims=True)
            for k in range(kw):
                dwg_ref[k:k + 1, lanes] += jnp.sum(dwg[k], axis=0, keepdims=True)
                dwv_ref[k:k + 1, lanes] += jnp.sum(dwv[k], axis=0, keepdims=True)
            return carry

        lax.fori_loop(0, tc // LANES, chunk, 0)

    tile_g = pl.BlockSpec((tm, tc), lambda j, i: (nt - 1 - i, j))
    tile_v = pl.BlockSpec((tm, tc), lambda j, i: (nt - 1 - i, j + nc))
    ext = pltpu.VMEM((SUBLANES, tc), f32)
    return pl.pallas_call(
        body, name="ffn_gate_bwd", grid=(nc, nt),
        in_specs=[tile_g, _prev_spec(tm, tc, lambda j, i: j, 1, nt),
                  tile_v, _prev_spec(tm, tc, lambda j, i: j + nc, 1, nt),
                  pl.BlockSpec((SUBLANES, tc), lambda j, i: (0, j)),
                  pl.BlockSpec((SUBLANES, tc), lambda j, i: (0, j + nc)),
                  pl.BlockSpec((1, tc), lambda j, i: (0, j)),
                  pl.BlockSpec((1, tc), lambda j, i: (0, j + nc)),
                  pl.BlockSpec((tm, K), lambda j, i: (nt - 1 - i, 0)),
                  pl.BlockSpec((tc, K), lambda j, i: (j, 0))],
        out_specs=[tile_g, tile_g,
                   pl.BlockSpec((SUBLANES, tc), lambda j, i: (0, j)),
                   pl.BlockSpec((SUBLANES, tc), lambda j, i: (0, j)),
                   pl.BlockSpec((1, tc), lambda j, i: (0, j)),
                   pl.BlockSpec((1, tc), lambda j, i: (0, j))],
        out_shape=[jax.ShapeDtypeStruct((M, D_FF), bf16), jax.ShapeDtypeStruct((M, D_FF), bf16),
                   jax.ShapeDtypeStruct((SUBLANES, D_FF), f32), jax.ShapeDtypeStruct((SUBLANES, D_FF), f32),
                   jax.ShapeDtypeStruct((1, D_FF), f32), jax.ShapeDtypeStruct((1, D_FF), f32)],
        scratch_shapes=[ext, ext, pltpu.VMEM((tm, tc), f32)],
        compiler_params=_cparams(("parallel", "arbitrary")),
    )(up, up, up, up, w, w, b, b, d_down, wdown)


def _rope_apply(blk, cos, sin):
    lane = lax.broadcasted_iota(jnp.int32, blk.shape, 1)
    half = QK_ROPE // 2
    partner = jnp.where(lane < half, pltpu.roll(blk, LANES - half, 1), pltpu.roll(blk, half, 1))
    return blk * cos + partner * sin


def _rope_unapply(d, cos, sin):
    t = d * sin
    lane = lax.broadcasted_iota(jnp.int32, d.shape, 1)
    half = QK_ROPE // 2
    partner = jnp.where(lane < half, pltpu.roll(t, LANES - half, 1), pltpu.roll(t, half, 1))
    return d * cos + partner


def _up_q_rope(q_c, g, wuq, cos, sin):
    M, K = q_c.shape
    tm = _pick(M, (768, 512, 256))

    hs = HEADS_PER_STEP

    def body(x_ref, g_ref, b_ref, c_ref, s_ref, a_ref, o_ref):
        xv = x_ref[...]
        a = (xv * _rstd(xv) * g_ref[...]).astype(bf16)
        a_ref[...] = a
        r = _dot(a, b_ref[...]) * Q_PRESCALE
        c, s = c_ref[...], s_ref[...]
        for u in range(hs):
            o_ref[u, :, 0:QK_NOPE] = r[:, u * QK_PAD:u * QK_PAD + QK_NOPE].astype(bf16)
            o_ref[u, :, QK_NOPE:QK_PAD] = _rope_apply(r[:, u * QK_PAD + QK_NOPE:(u + 1) * QK_PAD], c, s).astype(bf16)

    return pl.pallas_call(
        body, name="up_q_rope", grid=(M // tm, MLA_HEADS // hs),
        in_specs=[pl.BlockSpec((tm, K), lambda i, h: (i, 0)),
                  pl.BlockSpec((1, K), lambda i, h: (0, 0)),
                  pl.BlockSpec((K, hs * QK_PAD), lambda i, h: (0, h)),
                  pl.BlockSpec((tm, LANES), lambda i, h: (i, 0)),
                  pl.BlockSpec((tm, LANES), lambda i, h: (i, 0))],
        out_specs=[pl.BlockSpec((tm, K), lambda i, h: (i, 0)),
                   pl.BlockSpec((hs, tm, QK_PAD), lambda i, h: (h, i, 0))],
        out_shape=[jax.ShapeDtypeStruct((M, K), bf16), jax.ShapeDtypeStruct((MLA_HEADS, M, QK_PAD), bf16)],
        compiler_params=_cparams(("parallel", "arbitrary")),
    )(q_c, g, wuq, cos, sin)


def _up_kv_rope(kv_c, g, wukv, kpe_raw, cos, sin):
    M, K = kv_c.shape
    tm = _pick(M, (768, 512, 256))

    hs = HEADS_PER_STEP
    w = QK_NOPE + V_DIM

    def body(x_ref, g_ref, b_ref, pe_ref, c_ref, s_ref, a_ref, k_ref, v_ref):
        xv = x_ref[...]
        a = (xv * _rstd(xv) * g_ref[...]).astype(bf16)
        a_ref[...] = a
        r = _dot(a, b_ref[...])
        pe = _rope_apply(pe_ref[...], c_ref[...], s_ref[...]).astype(bf16)
        for u in range(hs):
            k_ref[u, :, 0:QK_NOPE] = r[:, u * w:u * w + QK_NOPE].astype(bf16)
            k_ref[u, :, QK_NOPE:QK_PAD] = pe
            v_ref[u] = r[:, u * w + QK_NOPE:(u + 1) * w].astype(bf16)

    return pl.pallas_call(
        body, name="up_kv_rope", grid=(M // tm, MLA_HEADS // hs),
        in_specs=[pl.BlockSpec((tm, K), lambda i, h: (i, 0)),
                  pl.BlockSpec((1, K), lambda i, h: (0, 0)),
                  pl.BlockSpec((K, hs * w), lambda i, h: (0, h)),
                  pl.BlockSpec((tm, LANES), lambda i, h: (i, 0)),
                  pl.BlockSpec((tm, LANES), lambda i, h: (i, 0)),
                  pl.BlockSpec((tm, LANES), lambda i, h: (i, 0))],
        out_specs=[pl.BlockSpec((tm, K), lambda i, h: (i, 0)),
                   pl.BlockSpec((hs, tm, QK_PAD), lambda i, h: (h, i, 0)),
                   pl.BlockSpec((hs, tm, V_DIM), lambda i, h: (h, i, 0))],
        out_shape=[jax.ShapeDtypeStruct((M, K), bf16), jax.ShapeDtypeStruct((MLA_HEADS, M, QK_PAD), bf16),
                   jax.ShapeDtypeStruct((MLA_HEADS, M, V_DIM), bf16)],
        compiler_params=_cparams(("parallel", "arbitrary")),
    )(kv_c, g, wukv, kpe_raw, cos, sin)


def _latent_bwd(d_full_sc, w_ref, x_ref, g_ref, a_ref, dx_ref, dg_ref, dw_ref):
    d_full = d_full_sc[...]
    dx, dgp = _rms_bwd_math(x_ref[...], g_ref[...], _dot_nt(d_full, w_ref[...]))
    dx_ref[...] = dx.astype(bf16)
    dg_ref[...] += jnp.sum(dgp, axis=0, keepdims=True)
    dw_ref[...] += _dot_tn(a_ref[...], d_full)


def _latent_bwd_call(body, name, head_inputs, head_specs, cos, sin, w, x, g, a, extra_out_specs, extra_out_shape):
    M, K = x.shape
    tm = _rt(M)
    N = w.shape[1]
    return pl.pallas_call(
        body, name=name, grid=(M // tm,),
        in_specs=head_specs + [_row(LANES, tm), _row(LANES, tm), _full((K, N)), _row(K, tm), _full((1, K)),
                               _row(K, tm)],
        out_specs=[_row(K, tm), _full((1, K)), _full((K, N))] + extra_out_specs,
        out_shape=[jax.ShapeDtypeStruct((M, K), bf16), jax.ShapeDtypeStruct((1, K), f32),
                   jax.ShapeDtypeStruct((K, N), f32)] + extra_out_shape,
        scratch_shapes=[pltpu.VMEM((tm, N), bf16)],
        compiler_params=_cparams(("arbitrary",)),
    )(*head_inputs, cos, sin, w, x, g, a)


def _q_branch_bwd(dq, cos, sin, wuq, q_c, g, qn):
    tm = _rt(q_c.shape[0])

    def body(d_ref, c_ref, s_ref, w_ref, x_ref, g_ref, a_ref, dx_ref, dg_ref, dw_ref, full_sc):
        @pl.when(pl.program_id(0) == 0)
        def _():
            dg_ref[...] = jnp.zeros_like(dg_ref)
            dw_ref[...] = jnp.zeros_like(dw_ref)

        c, s = c_ref[...], s_ref[...]
        for h in range(MLA_HEADS):
            full_sc[:, h * QK_PAD:h * QK_PAD + QK_NOPE] = (d_ref[h, :, 0:QK_NOPE] * SOFTMAX_SCALE).astype(bf16)
            full_sc[:, h * QK_PAD + QK_NOPE:(h + 1) * QK_PAD] = (_rope_unapply(
                d_ref[h, :, QK_NOPE:QK_PAD], c, s) * SOFTMAX_SCALE).astype(bf16)
        _latent_bwd(full_sc, w_ref, x_ref, g_ref, a_ref, dx_ref, dg_ref, dw_ref)

    return _latent_bwd_call(body, "q_branch_bwd", [dq],
                            [pl.BlockSpec((MLA_HEADS, tm, QK_PAD), lambda i: (0, i, 0))],
                            cos, sin, wuq, q_c, g, qn, [], [])


def _kv_branch_bwd(dk, dv, cos, sin, wukv, kv_c, g, kvn):
    M = kv_c.shape[0]
    tm = _rt(M)
    w = QK_NOPE + V_DIM

    def body(dk_ref, dv_ref, c_ref, s_ref, w_ref, x_ref, g_ref, a_ref, dx_ref, dg_ref, dw_ref, pe_ref, full_sc):
        @pl.when(pl.program_id(0) == 0)
        def _():
            dg_ref[...] = jnp.zeros_like(dg_ref)
            dw_ref[...] = jnp.zeros_like(dw_ref)

        pe = None
        for h in range(MLA_HEADS):
            full_sc[:, h * w:h * w + QK_NOPE] = dk_ref[h, :, 0:QK_NOPE].astype(bf16)
            full_sc[:, h * w + QK_NOPE:(h + 1) * w] = dv_ref[h].astype(bf16)
            t = dk_ref[h, :, QK_NOPE:QK_PAD]
            pe = t if pe is None else pe + t
        pe_ref[...] = _rope_unapply(pe, c_ref[...], s_ref[...])
        _latent_bwd(full_sc, w_ref, x_ref, g_ref, a_ref, dx_ref, dg_ref, dw_ref)

    return _latent_bwd_call(body, "kv_branch_bwd", [dk, dv],
                            [pl.BlockSpec((MLA_HEADS, tm, QK_PAD), lambda i: (0, i, 0)),
                             pl.BlockSpec((MLA_HEADS, tm, V_DIM), lambda i: (0, i, 0))],
                            cos, sin, wukv, kv_c, g, kvn, [_row(LANES, tm)],
                            [jax.ShapeDtypeStruct((M, LANES), f32)])


def _attn_tile(M):
    return 768 if (M % 768 == 0 and M >= 4 * 768) else ROW_TILE


def _col_to_row(col):
    return col.T[0:1, :]


def _hosted_exchange(refs_in, refs_out, sems, scatter, first, last):
    copies = _exchange_copies(refs_in, refs_out, *sems, scatter)

    @pl.when(first)
    def _():
        for cp in copies:
            cp.start()

    @pl.when(last)
    def _():
        for cp in copies:
            cp.wait()


def _flash_fwd(q, k, v, carried, scatter):
    H, M, _ = q.shape
    T = _attn_tile(M)
    nq = M // T
    nx = len(carried)

    def body(*refs):
        q_ref, k_ref, v_ref = refs[:3]
        o_ref, lse_ref = refs[3 + nx:5 + nx]
        sa_ref, sb_ref, m_sc, l_sc, acc_sc = refs[5 + 2 * nx:10 + 2 * nx]
        h = pl.program_id(0)
        i = pl.program_id(1)
        _hosted_exchange(refs[3:3 + nx], refs[5 + nx:5 + 2 * nx], refs[10 + 2 * nx:], scatter,
                         (h == 0) & (i == 0), (h == H - 1) & (i == nq - 1))
        qv = q_ref[0]
        m_sc[...] = jnp.full_like(m_sc, NEG)
        l_sc[...] = jnp.zeros_like(l_sc)
        acc_sc[...] = jnp.zeros_like(acc_sc)

        def scores(j, s_ref):
            off = pl.multiple_of(j * T, T)
            s_ref[...] = _dot_nt(qv, k_ref[0, pl.ds(off, T), :])

        def softmax_pv(j, s_ref, masked):
            off = pl.multiple_of(j * T, T)
            s = s_ref[...]
            if masked:
                r = lax.broadcasted_iota(jnp.int32, (T, T), 0)
                c = lax.broadcasted_iota(jnp.int32, (T, T), 1)
                s = jnp.where(r >= c, s, NEG)
            m_prev = m_sc[...]
            m_new = jnp.maximum(m_prev, jnp.max(s, axis=1, keepdims=True))
            alpha = jnp.exp2(m_prev - m_new)
            p = jnp.exp2(s - m_new[:, 0:1])
            l_sc[...] = alpha * l_sc[...] + jnp.sum(p, axis=1, keepdims=True)
            acc_sc[...] = alpha * acc_sc[...] + _dot(p.astype(bf16), v_ref[0, pl.ds(off, T), :])
            m_sc[...] = m_new

        scores(0, sa_ref)

        def pair(jj, c):
            j0 = 2 * jj
            scores(j0 + 1, sb_ref)
            softmax_pv(j0, sa_ref, False)
            scores(j0 + 2, sa_ref)
            softmax_pv(j0 + 1, sb_ref, False)
            return c

        lax.fori_loop(0, i // 2, pair, 0)

        @pl.when(i % 2 == 0)
        def _():
            softmax_pv(i, sa_ref, True)

        @pl.when(i % 2 == 1)
        def _():
            scores(i, sb_ref)
            softmax_pv(i - 1, sa_ref, False)
            softmax_pv(i, sb_ref, True)

        l = l_sc[...]
        o_ref[...] = acc_sc[...] / l
        lse_ref[0, 0] = _col_to_row(m_sc[...] + jnp.log2(l))

    any_spec = pl.BlockSpec(memory_space=pl.ANY)
    return pl.pallas_call(
        body, name="flash_fwd", grid=(H, nq),
        in_specs=[pl.BlockSpec((1, T, QK_PAD), lambda h, i: (h, i, 0)),
                  pl.BlockSpec((1, M, QK_PAD), lambda h, i: (h, 0, 0)),
                  pl.BlockSpec((1, M, V_DIM), lambda h, i: (h, 0, 0))] + [any_spec] * nx,
        out_specs=[pl.BlockSpec((T, V_DIM), lambda h, i: (i, h)),
                   pl.BlockSpec((1, 1, 1, T), lambda h, i: (h, i, 0, 0))] + [any_spec] * nx,
        out_shape=[jax.ShapeDtypeStruct((M, H * V_DIM), f32),
                   jax.ShapeDtypeStruct((H, nq, 1, T), f32)] + _exchange_shapes(carried, scatter),
        scratch_shapes=[pltpu.VMEM((T, T), f32), pltpu.VMEM((T, T), f32),
                        pltpu.VMEM((T, LANES), f32), pltpu.VMEM((T, LANES), f32),
                        pltpu.VMEM((T, V_DIM), f32)] + _exchange_sems(nx),
        compiler_params=_cparams(("arbitrary", "arbitrary")),
    )(q, k, v, *carried)


def _attn_out_bwd(o, g, d_an):
    M, K = o.shape
    H = MLA_HEADS
    T = _attn_tile(M)

    def body(o_ref, g_ref, d_ref, dh_ref, dl_ref, dg_ref):
        @pl.when(pl.program_id(0) == 0)
        def _():
            dg_ref[...] = jnp.zeros_like(dg_ref)

        ov = o_ref[...]
        do, dgp = _rms_bwd_math(ov, g_ref[...], d_ref[...])
        dg_ref[...] += jnp.sum(dgp, axis=0, keepdims=True)
        for h in range(H):
            sl = slice(h * V_DIM, (h + 1) * V_DIM)
            doh = do[:, sl]
            dh_ref[h] = doh.astype(bf16)
            col = jnp.sum(ov[:, sl] * doh, axis=1, keepdims=True) + jnp.zeros((T, LANES), f32)
            dl_ref[h, 0] = _col_to_row(col)

    return pl.pallas_call(
        body, name="attn_out_bwd", grid=(M // T,),
        in_specs=[_row(K, T), _full((1, K)), _row(K, T)],
        out_specs=[pl.BlockSpec((H, T, V_DIM), lambda i: (0, i, 0)),
                   pl.BlockSpec((H, 1, 1, T), lambda i: (0, i, 0, 0)),
                   _full((1, K))],
        out_shape=[jax.ShapeDtypeStruct((H, M, V_DIM), bf16),
                   jax.ShapeDtypeStruct((H, M // T, 1, T), f32),
                   jax.ShapeDtypeStruct((1, K), f32)],
        compiler_params=_cparams(("arbitrary",)),
    )(o, g, d_an)


def _flash_bwd(q, k, v, do, lse, delta, carried, scatter):
    H, M, _ = q.shape
    T = _attn_tile(M)
    nq = M // T
    nx = len(carried)

    def body(*refs):
        q_ref, do_ref, lse_ref, dl_ref, k_ref, v_ref = refs[:6]
        dq_ref, dk_ref, dv_ref = refs[6 + nx:9 + nx]
        dk_sc, dv_sc = refs[9 + 2 * nx:11 + 2 * nx]
        j = pl.program_id(1)
        _hosted_exchange(refs[6:6 + nx], refs[9 + nx:9 + 2 * nx], refs[11 + 2 * nx:], scatter,
                         (pl.program_id(0) == 0) & (j == 0), (pl.program_id(0) == H - 1) & (j == nq - 1))

        @pl.when(j == 0)
        def _():
            dq_ref[...] = jnp.zeros_like(dq_ref)

        kt = k_ref[0]
        vt = v_ref[0]
        dk_sc[...] = jnp.zeros_like(dk_sc)
        dv_sc[...] = jnp.zeros_like(dv_sc)

        def step(i, masked):
            off = pl.multiple_of(i * T, T)
            qt = q_ref[0, pl.ds(off, T), :]
            dot_ = do_ref[0, pl.ds(off, T), :]
            st = _dot_nt(kt, qt)
            if masked:
                r = lax.broadcasted_iota(jnp.int32, (T, T), 0)
                c = lax.broadcasted_iota(jnp.int32, (T, T), 1)
                st = jnp.where(c >= r, st, NEG)
            pt = jnp.exp2(st - lse_ref[0, i])
            dv_sc[...] += _dot(pt.astype(bf16), dot_)
            dpt = _dot_nt(vt, dot_)
            dst = (pt * (dpt - dl_ref[0, i])).astype(bf16)
            dk_sc[...] += _dot(dst, qt)
            dq_ref[0, pl.ds(off, T), :] += _dot_tn(dst, kt)

        step(j, True)

        def loop_body(i, c):
            step(i, False)
            return c

        lax.fori_loop(j + 1, nq, loop_body, 0)
        dk_ref[0] = dk_sc[...] * LN2
        dv_ref[0] = dv_sc[...]

    any_spec = pl.BlockSpec(memory_space=pl.ANY)
    return pl.pallas_call(
        body, name="flash_bwd", grid=(H, nq),
        in_specs=[pl.BlockSpec((1, M, QK_PAD), lambda h, j: (h, 0, 0)),
                  pl.BlockSpec((1, M, V_DIM), lambda h, j: (h, 0, 0)),
                  pl.BlockSpec((1, nq, 1, T), lambda h, j: (h, 0, 0, 0)),
                  pl.BlockSpec((1, nq, 1, T), lambda h, j: (h, 0, 0, 0)),
                  pl.BlockSpec((1, T, QK_PAD), lambda h, j: (h, j, 0)),
                  pl.BlockSpec((1, T, V_DIM), lambda h, j: (h, j, 0))] + [any_spec] * nx,
        out_specs=[pl.BlockSpec((1, M, QK_PAD), lambda h, j: (h, 0, 0)),
                   pl.BlockSpec((1, T, QK_PAD), lambda h, j: (h, j, 0)),
                   pl.BlockSpec((1, T, V_DIM), lambda h, j: (h, j, 0))] + [any_spec] * nx,
        out_shape=[jax.ShapeDtypeStruct((H, M, QK_PAD), f32),
                   jax.ShapeDtypeStruct((H, M, QK_PAD), f32),
                   jax.ShapeDtypeStruct((H, M, V_DIM), f32)] + _exchange_shapes(carried, scatter),
        scratch_shapes=[pltpu.VMEM((T, QK_PAD), f32), pltpu.VMEM((T, V_DIM), f32)] + _exchange_sems(nx),
        compiler_params=_cparams(("arbitrary", "arbitrary")),
    )(q, do, lse, delta, k, v, *carried)


def _dt_fwd(dt_raw, bias, expand):
    M = dt_raw.shape[0]
    tm = _rt(M)

    def body(x_ref, b_ref, e_ref, o_ref, oe_ref):
        u = x_ref[...] + b_ref[...]
        sp = jnp.maximum(u, 0.0) + jnp.log(1.0 + jnp.exp(-jnp.abs(u)))
        lane = lax.broadcasted_iota(jnp.int32, u.shape, 1)
        dtp = jnp.where(lane < SSM_HEADS, sp, 0.0)
        o_ref[...] = dtp
        oe_ref[...] = _dot_hi(dtp, e_ref[...])

    return pl.pallas_call(
        body, name="dt_fwd", grid=(M // tm,),
        in_specs=[_row(LANES, tm), _full((1, LANES)), _full((LANES, D_SSM))],
        out_specs=[_row(LANES, tm), _row(D_SSM, tm)],
        out_shape=[jax.ShapeDtypeStruct((M, LANES), f32), jax.ShapeDtypeStruct((M, D_SSM), f32)],
        compiler_params=_cparams(("parallel",)),
    )(dt_raw, bias, expand)


def _dt_bwd(dt_raw, bias, ddt):
    M = dt_raw.shape[0]
    tm = _rt(M)

    def body(x_ref, b_ref, d_ref, o_ref, db_ref):
        @pl.when(pl.program_id(0) == 0)
        def _():
            db_ref[...] = jnp.zeros_like(db_ref)

        u = x_ref[...] + b_ref[...]
        lane = lax.broadcasted_iota(jnp.int32, u.shape, 1)
        g = jnp.where(lane < SSM_HEADS, d_ref[...] * _sigmoid(u), 0.0)
        o_ref[...] = g
        db_ref[...] += jnp.sum(g, axis=0, keepdims=True)

    return pl.pallas_call(
        body, name="dt_bwd", grid=(M // tm,),
        in_specs=[_row(LANES, tm), _full((1, LANES)), _row(LANES, tm)],
        out_specs=[_row(LANES, tm), _full((1, LANES))],
        out_shape=[jax.ShapeDtypeStruct((M, LANES), f32), jax.ShapeDtypeStruct((1, LANES), f32)],
        compiler_params=_cparams(("arbitrary",)),
    )(dt_raw, bias, ddt)


SSM_GW = SSM_HPG * SSM_P
SSM_PAIRS = SSM_GW // LANES


def _ssd_common(dte_ref, dtt_ref, ae_ref, acol_ref):
    Q = CHUNK
    r = lax.broadcasted_iota(jnp.int32, (Q, Q), 0)
    c = lax.broadcasted_iota(jnp.int32, (Q, Q), 1)
    causal = r >= c
    anti = c >= r
    tril = causal.astype(f32)
    triu = anti.astype(f32)
    dt_e = dte_ref[...]
    cs_e = _dot_hi(tril, dt_e * ae_ref[...])
    cst = _dot_hi(dtt_ref[...] * acol_ref[...], triu)
    cs_last = cs_e[Q - 1:Q, :]
    return causal, anti, triu, dt_e, cs_e, cst, jnp.exp(cs_e), jnp.exp(cs_last - cs_e), jnp.exp(cs_last)


def _half_masks():
    lane = lax.broadcasted_iota(jnp.int32, (CHUNK, LANES), 1)
    lo = lane < SSM_P
    return lo, jnp.logical_not(lo)


def _ssd_fwd(xbc_c, dt_e, dtt, a_e, a_col):
    M = xbc_c.shape[0]
    Q = CHUNK
    nch = M // Q

    def body(x_ref, dte_ref, dtt_ref, ae_ref, acol_ref, y_ref, hin_ref, ht_sc):
        @pl.when(pl.program_id(0) == 0)
        def _():
            ht_sc[...] = jnp.zeros_like(ht_sc)

        causal, _, _, dt_e, cs_e, cst, ecs_e, dte_e, elast_e = _ssd_common(dte_ref, dtt_ref, ae_ref, acol_ref)
        halves = _half_masks()
        for g in range(SSM_GROUPS):
            g0 = g * SSM_GW
            bg = x_ref[:, D_SSM + g * SSM_N:D_SSM + (g + 1) * SSM_N]
            cg = x_ref[:, D_SSM + D_BC + g * SSM_N:D_SSM + D_BC + (g + 1) * SSM_N]
            bg_b = bg.astype(bf16)
            cg_b = cg.astype(bf16)
            cb = _dot_nt(cg_b, bg_b)
            bgt_b = bg.T.astype(bf16)
            xdt_g = x_ref[:, g0:g0 + SSM_GW] * dt_e[:, g0:g0 + SSM_GW]
            ht = ht_sc[g]
            hin_ref[0, g] = ht
            y_off = _dot(cg_b, ht.astype(bf16)) * ecs_e[:, g0:g0 + SSM_GW]
            for pr in range(SSM_PAIRS):
                p0 = pr * LANES
                xdt_p = xdt_g[:, p0:p0 + LANES]
                acc = y_off[:, p0:p0 + LANES]
                for half in range(2):
                    h = g * SSM_HPG + pr * 2 + half
                    seg = cs_e[:, h * SSM_P:h * SSM_P + 1] - cst[h:h + 1, :]
                    lm = jnp.exp(jnp.where(causal, seg, -jnp.inf))
                    xm = jnp.where(halves[half], xdt_p, 0.0).astype(bf16)
                    acc = acc + _dot((cb * lm).astype(bf16), xm)
                y_ref[:, g0 + p0:g0 + p0 + LANES] = acc
            st = _dot(bgt_b, (xdt_g * dte_e[:, g0:g0 + SSM_GW]).astype(bf16))
            ht_sc[g] = ht * elast_e[:, g0:g0 + SSM_GW] + st

    return pl.pallas_call(
        body, name="ssd_fwd", grid=(nch,),
        in_specs=[pl.BlockSpec((Q, D_XBC), lambda c: (c, 0)),
                  pl.BlockSpec((Q, D_SSM), lambda c: (c, 0)),
                  pl.BlockSpec((SSM_HEADS, Q), lambda c: (0, c)),
                  _full((1, D_SSM)), _full((SSM_HEADS, LANES))],
        out_specs=[pl.BlockSpec((Q, D_SSM), lambda c: (c, 0)),
                   pl.BlockSpec((1, SSM_GROUPS, SSM_N, SSM_GW), lambda c: (c, 0, 0, 0))],
        out_shape=[jax.ShapeDtypeStruct((M, D_SSM), f32),
                   jax.ShapeDtypeStruct((nch, SSM_GROUPS, SSM_N, SSM_GW), f32)],
        scratch_shapes=[pltpu.VMEM((SSM_GROUPS, SSM_N, SSM_GW), f32)],
        compiler_params=_cparams(("arbitrary",)),
    )(xbc_c, dt_e, dtt, a_e, a_col)


def _ssd_bwd(xbc_c, dtp, dt_e, dtt, a_row, a_e, a_col, hin, dy, d_exp, head_ind):
    M = xbc_c.shape[0]
    Q = CHUNK
    nch = M // Q
    rev = lambda c: nch - 1 - c

    def body(x_ref, dtp_ref, dte_ref, dtt_ref, arow_ref, ae_ref, acol_ref, hin_ref, dy_ref, dexp_ref,
             ind_ref, dx_ref, ddt_ref, da_ref, dht_sc, z_sc, z1_sc, last_sc, ct_sc):
        @pl.when(pl.program_id(0) == 0)
        def _():
            dht_sc[...] = jnp.zeros_like(dht_sc)
            da_ref[...] = jnp.zeros_like(da_ref)
            last_sc[...] = jnp.zeros_like(last_sc)
            ct_sc[...] = jnp.zeros_like(ct_sc)

        causal, anti, triu, dt_e, cs_e, cst, ecs_e, dte_e, elast_e = _ssd_common(dte_ref, dtt_ref, ae_ref, acol_ref)
        halves = _half_masks()
        lane = lax.broadcasted_iota(jnp.int32, (Q, LANES), 1)
        rsum = jnp.zeros((Q, LANES), f32)
        for g in range(SSM_GROUPS):
            g0 = g * SSM_GW
            gs = slice(g0, g0 + SSM_GW)
            b0 = D_SSM + g * SSM_N
            c0 = D_SSM + D_BC + g * SSM_N
            bg = x_ref[:, b0:b0 + SSM_N]
            cg = x_ref[:, c0:c0 + SSM_N]
            bg_b = bg.astype(bf16)
            cg_b = cg.astype(bf16)
            cgt_b = cg.T.astype(bf16)
            cbt = _dot_nt(bg_b, cg_b)
            cb = _dot_nt(cg_b, bg_b)
            x_g = x_ref[:, gs]
            dt_g = dt_e[:, gs]
            xdt_g = x_g * dt_g
            dy_g = dy_ref[:, gs]
            ht = hin_ref[0, g]
            ht_b = ht.astype(bf16)
            dht = dht_sc[g]
            dht_b = dht.astype(bf16)
            dye_b = (dy_g * ecs_e[:, gs]).astype(bf16)
            dc = _dot_nt(dye_b, ht_b)
            dht_new = dht * elast_e[:, gs] + _dot(cgt_b, dye_b)
            e = _dot(bg_b, dht_b)
            xdtd = xdt_g * dte_e[:, gs]
            db = _dot_nt(xdtd.astype(bf16), dht_b)
            dxdt_state = e * dte_e[:, gs]
            exd = e * xdtd
            z1_sc[:, gs] = dy_g * (_dot(cg_b, ht_b) * ecs_e[:, gs]) - exd
            last_sc[0:1, gs] = (jnp.sum(exd, axis=0, keepdims=True)
                                + jnp.sum(dht * ht, axis=0, keepdims=True) * elast_e[:, gs])
            dg_acc = jnp.zeros((Q, Q), f32)
            for pr in range(SSM_PAIRS):
                p0 = pr * LANES
                ps = slice(g0 + p0, g0 + p0 + LANES)
                dy_p = dy_g[:, p0:p0 + LANES]
                xdt_pb = xdt_g[:, p0:p0 + LANES].astype(bf16)
                acc = dxdt_state[:, p0:p0 + LANES]
                for half in range(2):
                    h = g * SSM_HPG + pr * 2 + half
                    seg = cs_e[:, h * SSM_P:h * SSM_P + 1] - cst[h:h + 1, :]
                    lm = jnp.exp(jnp.where(causal, seg, -jnp.inf))
                    lmt = jnp.exp(jnp.where(anti, -seg, -jnp.inf))
                    dym = jnp.where(halves[half], dy_p, 0.0).astype(bf16)
                    acc = acc + _dot((cbt * lmt).astype(bf16), dym)
                    dml = _dot_nt(dym, xdt_pb) * lm
                    dg_acc = dg_acc + dml
                    w = dml * cb
                    rsum = rsum + jnp.where(lane == h, jnp.sum(w, axis=1, keepdims=True), 0.0)
                    ct_sc[h:h + 1, :] = jnp.sum(w, axis=0, keepdims=True)
                dx_ref[:, ps] = acc * dt_g[:, p0:p0 + LANES] + dexp_ref[:, ps] * dy_p
                z_sc[:, ps] = acc * x_g[:, p0:p0 + LANES]
            dg_b = dg_acc.astype(bf16)
            dx_ref[:, c0:c0 + SSM_N] = dc + _dot(dg_b, bg_b)
            dx_ref[:, b0:b0 + SSM_N] = db + _dot_tn(dg_b, cg_b)
            dht_sc[g] = dht_new
        s1 = _dot_hi(z1_sc[...], ind_ref[...])
        s2 = _dot_hi(z_sc[...], ind_ref[...])
        last = _dot_hi(last_sc[...], ind_ref[...])[0:1, :]
        dtp = dtp_ref[...]
        row = lax.broadcasted_iota(jnp.int32, (Q, LANES), 0)
        dcs = s1 + rsum + jnp.where(row == Q - 1, last, 0.0)
        tril = causal.astype(f32)
        da = _dot_hi(triu, dcs) - _dot_hi(ct_sc[...], tril).T
        ddt_ref[...] = s2 + da * arow_ref[...]
        da_ref[...] += jnp.sum(da * dtp, axis=0, keepdims=True)

    return pl.pallas_call(
        body, name="ssd_bwd", grid=(nch,),
        in_specs=[pl.BlockSpec((Q, D_XBC), lambda c: (rev(c), 0)),
                  pl.BlockSpec((Q, LANES), lambda c: (rev(c), 0)),
                  pl.BlockSpec((Q, D_SSM), lambda c: (rev(c), 0)),
                  pl.BlockSpec((SSM_HEADS, Q), lambda c: (0, rev(c))),
                  _full((1, LANES)), _full((1, D_SSM)), _full((SSM_HEADS, LANES)),
                  pl.BlockSpec((1, SSM_GROUPS, SSM_N, SSM_GW), lambda c: (rev(c), 0, 0, 0)),
                  pl.BlockSpec((Q, D_SSM), lambda c: (rev(c), 0)),
                  _full((1, D_SSM)), _full((D_SSM, LANES))],
        out_specs=[pl.BlockSpec((Q, D_XBC), lambda c: (rev(c), 0)),
                   pl.BlockSpec((Q, LANES), lambda c: (rev(c), 0)),
                   _full((1, LANES))],
        out_shape=[jax.ShapeDtypeStruct((M, D_XBC), f32), jax.ShapeDtypeStruct((M, LANES), f32),
                   jax.ShapeDtypeStruct((1, LANES), f32)],
        scratch_shapes=[pltpu.VMEM((SSM_GROUPS, SSM_N, SSM_GW), f32), pltpu.VMEM((Q, D_SSM), f32),
                        pltpu.VMEM((Q, D_SSM), f32), pltpu.VMEM((SUBLANES, D_SSM), f32),
                        pltpu.VMEM((LANES, Q), f32)],
        compiler_params=_cparams(("arbitrary",)),
    )(xbc_c, dtp, dt_e, dtt, a_row, a_e, a_col, hin, dy, d_exp, head_ind)


def _gate_norm_fwd(y, xbc_c, z, d_exp, g):
    M = y.shape[0]
    tm = _rt(M)
    gw = D_SSM // SSM_GROUPS

    def body(y_ref, x_ref, z_ref, d_ref, g_ref, o_ref):
        yg = (y_ref[...] + d_ref[...] * x_ref[...]) * _silu(z_ref[...])
        for gi in range(SSM_GROUPS):
            blk = yg[:, gi * gw:(gi + 1) * gw]
            o_ref[:, gi * gw:(gi + 1) * gw] = (blk * _rstd(blk) * g_ref[:, gi * gw:(gi + 1) * gw]).astype(bf16)

    return pl.pallas_call(
        body, name="gate_norm_fwd", grid=(M // tm,),
        in_specs=[_row(D_SSM, tm), _row(D_SSM, tm), _row(D_SSM, tm), _full((1, D_SSM)), _full((1, D_SSM))],
        out_specs=_row(D_SSM, tm), out_shape=jax.ShapeDtypeStruct((M, D_SSM), bf16),
        compiler_params=_cparams(("parallel",)),
    )(y, xbc_c, z, d_exp, g)


def _gate_norm_bwd(y, xbc_c, z, d_exp, g, dout, head_ind):
    M = y.shape[0]
    tm = _rt(M)
    nt = M // tm
    gw = D_SSM // SSM_GROUPS

    def body(y_ref, x_ref, z_ref, d_ref, g_ref, do_ref, ind_ref, dy_ref, dz_ref, dg_ref, dd_ref, ddc_sc):
        i = pl.program_id(0)

        @pl.when(i == 0)
        def _():
            dg_ref[...] = jnp.zeros_like(dg_ref)
            ddc_sc[...] = jnp.zeros_like(ddc_sc)

        zv = z_ref[...]
        xv = x_ref[...]
        s = _silu(zv)
        yd = y_ref[...] + d_ref[...] * xv
        yg = yd * s
        dov = do_ref[...]
        for gi in range(SSM_GROUPS):
            sl = slice(gi * gw, (gi + 1) * gw)
            dyg, dgp = _rms_bwd_math(yg[:, sl], g_ref[:, sl], dov[:, sl])
            dg_ref[:, sl] += jnp.sum(dgp, axis=0, keepdims=True)
            dyd = dyg * s[:, sl]
            dy_ref[:, sl] = dyd
            dz_ref[:, sl] = (dyg * yd[:, sl] * _dsilu(zv[:, sl])).astype(bf16)
            ddc_sc[:, sl] += jnp.sum(dyd * xv[:, sl], axis=0, keepdims=True)

        @pl.when(i == nt - 1)
        def _():
            dd_ref[...] = _dot_hi(ddc_sc[...], ind_ref[...])

    return pl.pallas_call(
        body, name="gate_norm_bwd", grid=(nt,),
        in_specs=[_row(D_SSM, tm), _row(D_SSM, tm), _row(D_SSM, tm), _full((1, D_SSM)), _full((1, D_SSM)),
                  _row(D_SSM, tm), _full((D_SSM, LANES))],
        out_specs=[_row(D_SSM, tm), _row(D_SSM, tm), _full((1, D_SSM)), _full((1, LANES))],
        out_shape=[jax.ShapeDtypeStruct((M, D_SSM), f32), jax.ShapeDtypeStruct((M, D_SSM), bf16),
                   jax.ShapeDtypeStruct((1, D_SSM), f32), jax.ShapeDtypeStruct((1, LANES), f32)],
        scratch_shapes=[pltpu.VMEM((1, D_SSM), f32)],
        compiler_params=_cparams(("arbitrary",)),
    )(y, xbc_c, z, d_exp, g, dout, head_ind)


_PEER_FLIPS = [(0, 0, 1), (0, 1, 0), (0, 1, 1), (1, 0, 0), (1, 0, 1), (1, 1, 0), (1, 1, 1)]


def _exchange_copies(ins, outs, send_sems, recv_sems, loc_sems, scatter):
    n = len(ins)
    x, y, c = lax.axis_index("x"), lax.axis_index("y"), lax.axis_index("c")
    me = 4 * x + 2 * y + c
    copies = []
    for a in range(n):
        src = ins[a].at[me] if scatter else ins[a]
        copies.append(pltpu.make_async_copy(src, outs[a].at[me], loc_sems.at[a]))
    for p, (fx, fy, fc) in enumerate(_PEER_FLIPS):
        tx = 1 - x if fx else x
        ty = 1 - y if fy else y
        tc = 1 - c if fc else c
        tgt = 4 * tx + 2 * ty + tc
        for a in range(n):
            src = ins[a].at[tgt] if scatter else ins[a]
            copies.append(pltpu.make_async_remote_copy(
                src_ref=src, dst_ref=outs[a].at[me],
                send_sem=send_sems.at[p * n + a], recv_sem=recv_sems.at[p * n + a],
                device_id=(tx, ty, tc), device_id_type=_MESH))
    return copies


def _exchange_shapes(arrays, scatter):
    return [jax.ShapeDtypeStruct(a.shape if scatter else (N_DEV,) + a.shape, a.dtype) for a in arrays]


def _exchange_sems(n):
    return [pltpu.SemaphoreType.DMA((7 * n,)), pltpu.SemaphoreType.DMA((7 * n,)), pltpu.SemaphoreType.DMA((n,))]


def _exchange(arrays, scatter, name):
    n = len(arrays)

    def body(*refs):
        copies = _exchange_copies(refs[:n], refs[n:2 * n], *refs[2 * n:], scatter)
        for cp in copies:
            cp.start()
        for cp in copies:
            cp.wait()

    any_spec = pl.BlockSpec(memory_space=pl.ANY)
    return pl.pallas_call(
        body, name=name, in_specs=[any_spec] * n, out_specs=[any_spec] * n,
        out_shape=_exchange_shapes(arrays, scatter), scratch_shapes=_exchange_sems(n),
    )(*arrays)


def _gather_two_level(arrays, name):
    n = len(arrays)

    def body(*refs):
        ins, outs = refs[:n], refs[n:2 * n]
        send_sems, recv_sems, loc_sems = refs[2 * n:]
        x, y, c = lax.axis_index("x"), lax.axis_index("y"), lax.axis_index("c")
        me, sibling = (x, y, c), (x, y, 1 - c)
        chips = [(1 - x, y), (x, 1 - y), (1 - x, 1 - y)]

        def slot(a, dev):
            return outs[a].at[4 * dev[0] + 2 * dev[1] + dev[2]]

        def copy(a, k, block, to, src=None):
            return pltpu.make_async_remote_copy(
                src_ref=slot(a, block) if src is None else src, dst_ref=slot(a, block),
                send_sem=send_sems.at[7 * a + k], recv_sem=recv_sems.at[7 * a + k],
                device_id=to, device_id_type=_MESH)

        mine = [pltpu.make_async_copy(ins[a], slot(a, me), loc_sems.at[a]) for a in range(n)]
        first = []
        for a in range(n):
            first.append(copy(a, 0, me, sibling, src=ins[a]))
            first += [copy(a, 1 + j, me, (*chip, c), src=ins[a]) for j, chip in enumerate(chips)]
        for cp in mine + first:
            cp.start()
        passed = []
        for j, chip in enumerate(chips):
            for a in range(n):
                copy(a, 1 + j, (*chip, c), me).wait_recv()
                cp = copy(a, 4 + j, (*chip, c), sibling)
                cp.start()
                passed.append(cp)
        for a in range(n):
            copy(a, 0, sibling, me).wait_recv()
            for j, chip in enumerate(chips):
                copy(a, 4 + j, (*chip, 1 - c), me).wait_recv()
        for cp in first + passed:
            cp.wait_send()
        for cp in mine:
            cp.wait()

    any_spec = pl.BlockSpec(memory_space=pl.ANY)
    return pl.pallas_call(
        body, name=name, in_specs=[any_spec] * n, out_specs=[any_spec] * n,
        out_shape=_exchange_shapes(arrays, False), scratch_shapes=_exchange_sems(n),
    )(*arrays)


def _exchange_tail(scattered, gathered, name):
    ns, ng = len(scattered), len(gathered)
    n = ns + ng

    def body(*refs):
        sems = refs[2 * n:]
        copies = (_exchange_copies(refs[:ns], refs[n:n + ns], *sems[:3], True)
                  + _exchange_copies(refs[ns:n], refs[n + ns:2 * n], *sems[3:], False))
        for cp in copies:
            cp.start()
        for cp in copies:
            cp.wait()

    any_spec = pl.BlockSpec(memory_space=pl.ANY)
    return pl.pallas_call(
        body, name=name, in_specs=[any_spec] * n, out_specs=[any_spec] * n,
        out_shape=_exchange_shapes(scattered, True) + _exchange_shapes(gathered, False),
        scratch_shapes=_exchange_sems(ns) + _exchange_sems(ng),
    )(*scattered, *gathered)


def _adamw_math(g, w, m, v):
    c1 = 1.0 - ADAM_B1 ** ADAM_STEP
    c2 = 1.0 - ADAM_B2 ** ADAM_STEP
    mn = ADAM_B1 * m + (1.0 - ADAM_B1) * g
    vn = ADAM_B2 * v + (1.0 - ADAM_B2) * (g * g)
    m_hat = mn / c1
    v_hat = vn / c2
    return -ADAM_LR * (m_hat / (jnp.sqrt(v_hat) + ADAM_EPS) + ADAM_WD * w), mn, vn


def _adamw(parts, w, m, v, name):
    R, C = w.shape
    tr = _pick(R, (PACK_ROW_TILE, 64, 32, 16, 8))

    def body(p_ref, w_ref, m_ref, v_ref, g_ref, d_ref, nm_ref, nv_ref):
        g = p_ref[0].astype(f32)
        for s in range(1, N_DEV):
            g = g + p_ref[s].astype(f32)
        g_ref[...] = g
        d_ref[...], nm_ref[...], nv_ref[...] = _adamw_math(g, w_ref[...], m_ref[...], v_ref[...])

    spec = pl.BlockSpec((tr, C), lambda i: (i, 0))
    return pl.pallas_call(
        body, name=name, grid=(R // tr,),
        in_specs=[pl.BlockSpec((N_DEV, tr, C), lambda i: (0, i, 0)), spec, spec, spec],
        out_specs=[spec] * 4, out_shape=[jax.ShapeDtypeStruct((R, C), f32)] * 4,
        compiler_params=_cparams(("parallel",)),
    )(parts, w, m, v)


def _adamw_replicated(parts, ws, ms, vs):
    n = len(ws)
    R = parts.shape[1]
    sizes = [int(w.shape[1]) for w in ws]

    def body(*refs):
        p_ref = refs[0]
        w_refs, m_refs, v_refs = refs[1:1 + n], refs[1 + n:1 + 2 * n], refs[1 + 2 * n:1 + 3 * n]
        loss_ref = refs[1 + 3 * n]
        outs = refs[2 + 3 * n:]
        g_all = p_ref[0]
        for s in range(1, N_DEV):
            g_all = g_all + p_ref[s]
        row = 0
        for p in range(n):
            pieces, left = [], sizes[p]
            while left > 0:
                take = min(left, PACK_W)
                pieces.append(g_all[row:row + 1, 0:take])
                left -= take
                row += 1
            g = pieces[0] if len(pieces) == 1 else jnp.concatenate(pieces, axis=1)
            d, mn, vn = _adamw_math(g, w_refs[p][...], m_refs[p][...], v_refs[p][...])
            outs[4 * p][...] = g
            outs[4 * p + 1][...] = d
            outs[4 * p + 2][...] = mn
            outs[4 * p + 3][...] = vn
        loss_ref[...] = g_all[row:row + 1, 0:LANES]

    in_specs = [_full((N_DEV, R, PACK_W))] + [_full((1, s)) for s in sizes] * 3
    out_specs = [_full((1, LANES))]
    out_shape = [jax.ShapeDtypeStruct((1, LANES), f32)]
    for s in sizes:
        out_specs += [_full((1, s))] * 4
        out_shape += [jax.ShapeDtypeStruct((1, s), f32)] * 4
    res = pl.pallas_call(
        body, name="adamw_replicated", in_specs=in_specs, out_specs=out_specs, out_shape=out_shape,
        compiler_params=pltpu.CompilerParams(vmem_limit_bytes=VMEM_LIMIT),
    )(parts, *ws, *ms, *vs)
    return res[0], [res[1 + 4 * p:5 + 4 * p] for p in range(n)]


def _flat_rows(a, lead_ndim):
    lead = a.shape[:lead_ndim]
    n = int(np.prod(a.shape[lead_ndim:]))
    a = a.reshape(lead + (n,))
    pad = (-n) % PACK_W
    if pad:
        a = jnp.pad(a, [(0, 0)] * lead_ndim + [(0, pad)])
    return a.reshape(lead + ((n + pad) // PACK_W, PACK_W))


def _pack(arrays, lead_ndim, total_rows, dtype):
    rows = [_flat_rows(a.astype(dtype), lead_ndim) for a in arrays]
    cat = jnp.concatenate(rows, axis=lead_ndim)
    pad = total_rows - cat.shape[lead_ndim]
    if pad:
        cat = jnp.pad(cat, [(0, 0)] * lead_ndim + [(0, pad), (0, 0)])
    return cat


def _unpack(buf, shapes, lead_ndim):
    out = []
    r = 0
    lead = buf.shape[:lead_ndim]
    for shp in shapes:
        n = int(np.prod(shp))
        nr = -(-n // PACK_W)
        piece = lax.slice_in_dim(buf, r, r + nr, axis=lead_ndim)
        piece = piece.reshape(lead + (nr * PACK_W,))
        piece = lax.slice_in_dim(piece, 0, n, axis=lead_ndim)
        out.append(piece.reshape(lead + tuple(shp)))
        r += nr
    return out


def _round_up(n, m):
    return -(-n // m) * m


def kernel(x, meta_tokens, norm_mix_pre, norm_mix_post, norm_ffn_pre, norm_ffn_post, w_in, q_a_norm, w_uq, kv_a_norm, w_ukv, attn_out_norm, ssm_conv_w, ssm_conv_b, ssm_dt_bias, ssm_A_log, ssm_D, ssm_norm, w_out, w_up, ffn_conv_w, ffn_conv_b, w_down, loss_target, m_meta_tokens, m_norm_mix_pre, m_norm_mix_post, m_norm_ffn_pre, m_norm_ffn_post, m_w_in, m_q_a_norm, m_w_uq, m_kv_a_norm, m_w_ukv, m_attn_out_norm, m_ssm_conv_w, m_ssm_conv_b, m_ssm_dt_bias, m_ssm_A_log, m_ssm_D, m_ssm_norm, m_w_out, m_w_up, m_ffn_conv_w, m_ffn_conv_b, m_w_down, v_meta_tokens, v_norm_mix_pre, v_norm_mix_post, v_norm_ffn_pre, v_norm_ffn_post, v_w_in, v_q_a_norm, v_w_uq, v_kv_a_norm, v_w_ukv, v_attn_out_norm, v_ssm_conv_w, v_ssm_conv_b, v_ssm_dt_bias, v_ssm_A_log, v_ssm_D, v_ssm_norm, v_w_out, v_w_up, v_ffn_conv_w, v_ffn_conv_b, v_w_down):
    seq = x.shape[1]
    n_real = N_META + seq
    Lp = _round_up(n_real, 768) if n_real > 2048 else _round_up(n_real, ROW_TILE)
    D = D_MODEL

    early_w = [w_uq, w_ukv]
    late_w = [w_out, w_down]
    sharded_s = [meta_tokens, ssm_conv_w, ffn_conv_w]
    grp_a = dict(names=["w_out", "w_down"], w=late_w, m=[m_w_out, m_w_down],
                 v=[v_w_out, v_w_down])
    grp_b = dict(names=["w_uq", "w_ukv", "ssm_conv_w", "ffn_conv_w"],
                 w=early_w + [ssm_conv_w, ffn_conv_w],
                 m=[m_w_uq, m_w_ukv, m_ssm_conv_w, m_ffn_conv_w],
                 v=[v_w_uq, v_w_ukv, v_ssm_conv_w, v_ffn_conv_w])
    grp_meta = dict(names=["meta_tokens"], w=[meta_tokens], m=[m_meta_tokens], v=[v_meta_tokens])
    repl_w = [norm_mix_pre, norm_mix_post, norm_ffn_pre, norm_ffn_post, q_a_norm, kv_a_norm, attn_out_norm,
              ssm_conv_b, ssm_dt_bias, ssm_A_log, ssm_D, ssm_norm, ffn_conv_b]
    repl_m = [m_norm_mix_pre, m_norm_mix_post, m_norm_ffn_pre, m_norm_ffn_post, m_q_a_norm, m_kv_a_norm,
              m_attn_out_norm, m_ssm_conv_b, m_ssm_dt_bias, m_ssm_A_log, m_ssm_D, m_ssm_norm, m_ffn_conv_b]
    repl_v = [v_norm_mix_pre, v_norm_mix_post, v_norm_ffn_pre, v_norm_ffn_post, v_q_a_norm, v_kv_a_norm,
              v_attn_out_norm, v_ssm_conv_b, v_ssm_dt_bias, v_ssm_A_log, v_ssm_D, v_ssm_norm, v_ffn_conv_b]

    def pack_rows(arrs, lead):
        return _round_up(sum(-(-int(np.prod(a.shape[lead:])) // PACK_W) for a in arrs), 16)

    wb = _pack(early_w, 0, pack_rows(early_w, 0), bf16)
    wl = _pack(late_w, 0, pack_rows(late_w, 0), bf16)
    ws = _pack(sharded_s, 0, pack_rows(sharded_s, 0), f32)
    wb_all, ws_all, win_all = _gather_two_level([wb, ws, w_in[0].astype(bf16)], "gather_weights")
    g_w_uq, g_w_ukv = _unpack(wb_all, [a.shape for a in early_w], 1)
    g_meta, g_sconv, g_fconv = _unpack(ws_all, [a.shape for a in sharded_s], 1)

    def cols(gathered):
        t = gathered[:, 0]
        return jnp.transpose(t, (1, 0, 2)).reshape(t.shape[1], N_DEV * t.shape[2])

    win = cols(win_all[:, None])
    o = np.cumsum((0, Q_RANK, KV_RANK, QK_ROPE, D_SSM, D_XBC, SSM_HEADS))
    w_q, w_kv = win[:, o[0]:o[1]], win[:, o[1]:o[2]]
    w_rope = jnp.pad(win[:, o[2]:o[3]], ((0, 0), (0, LANES - QK_ROPE)))
    w_z, w_xbc = win[:, o[3]:o[4]], win[:, o[4]:o[5]]
    w_dt = jnp.pad(win[:, o[5]:o[6]], ((0, 0), (0, LANES - SSM_HEADS)))
    wuq = g_w_uq.reshape(Q_RANK, MLA_HEADS, QK_NOPE + QK_ROPE)
    wuq = jnp.pad(wuq, ((0, 0), (0, 0), (0, QK_PAD - QK_NOPE - QK_ROPE))).reshape(Q_RANK, MLA_HEADS * QK_PAD)
    wukv = g_w_ukv.reshape(KV_RANK, MLA_HEADS * (QK_NOPE + V_DIM))
    meta_full = jnp.transpose(g_meta, (1, 0, 2)).reshape(N_META, D)
    sconv_w = jnp.pad(cols(g_sconv), ((0, SUBLANES - SSM_CONV), (0, 0)))
    fconv_w = jnp.pad(cols(g_fconv), ((0, SUBLANES - FFN_CONV), (0, 0)))

    pos = jnp.arange(Lp, dtype=f32)
    inv = ROPE_THETA ** (-jnp.arange(0, QK_ROPE, 2, dtype=f32) / QK_ROPE)
    ang = pos[:, None] * inv[None, :]
    cs_, sn_ = jnp.cos(ang), jnp.sin(ang)
    zpad = jnp.zeros((Lp, LANES - QK_ROPE), f32)
    cos_t = jnp.concatenate([cs_, cs_, zpad], axis=1)
    sin_t = jnp.concatenate([-sn_, sn_, zpad], axis=1)
    dt_bias_p = jnp.pad(ssm_dt_bias, ((0, 0), (0, LANES - SSM_HEADS)))
    a_neg = -jnp.exp(ssm_A_log)
    a_row = jnp.pad(a_neg, ((0, 0), (0, LANES - SSM_HEADS)))
    a_col = jnp.broadcast_to(a_neg.reshape(SSM_HEADS, 1), (SSM_HEADS, LANES))
    d_exp = jnp.repeat(ssm_D, SSM_P, axis=1)
    a_e = jnp.repeat(a_neg, SSM_P, axis=1)
    head_ind = (jnp.arange(D_SSM)[:, None] // SSM_P == jnp.arange(LANES)[None, :]).astype(f32)

    xb = x[0]
    h0 = jnp.concatenate([meta_full, xb, jnp.zeros((Lp - n_real, D), f32)], axis=0)
    tgt = jnp.pad(loss_target[0], ((N_META, Lp - n_real), (0, 0)))
    hn1, q_c, kv_c, kpe_raw, z, xbc, dt_raw = _in_proj(h0, norm_mix_pre, [w_q, w_kv, w_rope, w_z, w_xbc, w_dt])

    qn, qh = _up_q_rope(q_c, q_a_norm, wuq, cos_t, sin_t)
    kvn, kh, vh = _up_kv_rope(kv_c, kv_a_norm, wukv, kpe_raw, cos_t, sin_t)
    attn, lse, wl_all, wup_all = _flash_fwd(qh, kh, vh, [wl, w_up[0].astype(bf16)], False)
    g_w_out, g_w_down = _unpack(wl_all, [a.shape for a in late_w], 1)
    wout = g_w_out.reshape(D_ATTN + D_SSM, D)
    wout_a, wout_s = wout[:D_ATTN], wout[D_ATTN:]
    wup = cols(wup_all[:, None])
    wdown = g_w_down.reshape(D_FF, D)
    an = _rms_fwd(attn, attn_out_norm, bf16, "norm_attn_out")

    xbc_c = _ssm_conv_fwd(xbc, sconv_w, ssm_conv_b)
    dtp, dt_e = _dt_fwd(dt_raw, dt_bias_p, jnp.transpose(head_ind))
    dtt = jnp.transpose(dtp[:, :SSM_HEADS])
    y_ssd, hin = _ssd_fwd(xbc_c, dt_e, dtt, a_e, a_col)
    ssm = _gate_norm_fwd(y_ssd, xbc_c, z, d_exp, ssm_norm)

    mix = _mm([(an, wout_a), (ssm, wout_s)], f32, False, "out_proj")
    h1, hn2 = _resid_norm(h0, mix, norm_mix_post, norm_ffn_pre)
    up = _mm([(hn2, wup)], f32, False, "ffn_up")
    act = _ffn_gate_fwd(up, fconv_w, ffn_conv_b)
    dh2, d_down, dg_ffn_post, loss_part = _final(h1, act, wdown, norm_ffn_post, tgt, n_real)

    dw_down = _mm_tn(act, d_down, "ffn_down_dw")
    dup_g, dup_v, dwc_g, dwc_v, dbc_g, dbc_v = _ffn_gate_bwd(up, fconv_w, ffn_conv_b, d_down, wdown)
    dw_up = jnp.concatenate([_mm_tn(hn2, dup_g, "ffn_up_dw_g"), _mm_tn(hn2, dup_v, "ffn_up_dw_v")], axis=1)
    dh1, d_mix, dg_ffn_pre, dg_mix_post = _mid_bwd(h1, norm_ffn_pre, dup_g, dup_v, wup, dh2, mix, norm_mix_post)
    d_an = _mm([(d_mix, wout_a)], f32, True, "out_proj_dx_a")
    d_ssm = _mm([(d_mix, wout_s)], f32, True, "out_proj_dx_s")
    dw_out = jnp.concatenate([_mm_tn(an, d_mix, "out_proj_dw_a"), _mm_tn(ssm, d_mix, "out_proj_dw_s")], axis=0)

    do_h, delta, dg_attn_out = _attn_out_bwd(attn, attn_out_norm, d_an)
    def col_blocks(gm):
        r, cc = gm.shape
        return jnp.transpose(gm.reshape(r, N_DEV, cc // N_DEV), (1, 0, 2))

    blocks_a = [dw_out.reshape(N_DEV, (D_ATTN + D_SSM) // N_DEV, D), dw_down.reshape(N_DEV, D_FF // N_DEV, D)]
    gpack_a = _pack(blocks_a, 1, pack_rows(blocks_a, 1), bf16)
    dqh, dkh, dvh, gparts_a, gparts_up = _flash_bwd(qh, kh, vh, do_h, lse, delta,
                                                    [gpack_a, col_blocks(dw_up).astype(bf16)], True)
    d_q_c, dg_q, dw_uq = _q_branch_bwd(dqh, cos_t, sin_t, wuq, q_c, q_a_norm, qn)
    d_kv_c, dg_kv, dw_ukv, d_kpe_raw = _kv_branch_bwd(dkh, dvh, cos_t, sin_t, wukv, kv_c, kv_a_norm, kvn)

    dy_ssd, dz, dg_ssm, dd_heads = _gate_norm_bwd(y_ssd, xbc_c, z, d_exp, ssm_norm, d_ssm, head_ind)
    d_xbc_c, ddt, da_heads = _ssd_bwd(xbc_c, dtp, dt_e, dtt, a_row, a_e, a_col, hin, dy_ssd, d_exp, head_ind)
    d_xbc, dw_sconv, db_sconv = _ssm_conv_bwd(xbc, sconv_w, ssm_conv_b, d_xbc_c)
    d_dt_raw, d_dt_bias = _dt_bwd(dt_raw, dt_bias_p, ddt)

    dw_q, dw_kv, dw_rope, dw_z, dw_xbc, dw_dt = _in_proj_dw(hn1, [d_q_c, d_kv_c, d_kpe_raw, dz, d_xbc, d_dt_raw])
    dw_in = jnp.concatenate([dw_q, dw_kv, dw_rope[:, :QK_ROPE], dw_z, dw_xbc, dw_dt[:, :SSM_HEADS]], axis=1)
    dw_uq3 = dw_uq.reshape(Q_RANK, MLA_HEADS, QK_PAD)[:, :, :QK_NOPE + QK_ROPE]
    blocks_b = [
        dw_uq3.reshape(N_DEV, Q_RANK // N_DEV, MLA_HEADS, QK_NOPE + QK_ROPE),
        dw_ukv.reshape(N_DEV, KV_RANK // N_DEV, MLA_HEADS, QK_NOPE + V_DIM),
        col_blocks(dw_sconv[:SSM_CONV]),
        col_blocks(jnp.concatenate([dwc_g, dwc_v], axis=1)[:FFN_CONV]),
    ]
    gpack_b = _pack(blocks_b, 1, pack_rows(blocks_b, 1), bf16)
    dh0, dg_mix_pre, gparts_b, gparts_in = _in_proj_dx(
        [d_q_c, d_kv_c, d_kpe_raw, dz, d_xbc, d_dt_raw], [w_q, w_kv, w_rope, w_z, w_xbc, w_dt],
        h0, norm_mix_pre, dh1, [gpack_b, col_blocks(dw_in).astype(bf16)], True)

    grad_x = dh0[N_META:n_real][None]
    meta_blocks = col_blocks(dh0[:N_META]).reshape(N_DEV, N_META * D // N_DEV // PACK_W, PACK_W)


    def adam_group(parts, grp, name):
        rows = parts.shape[1]
        packs = [_pack([a[None] for a in grp[k]], 1, rows, f32)[0] for k in ("w", "m", "v")]
        outs = _adamw(parts, *packs, name)
        shapes = [a.shape for a in grp["w"]]
        return [dict(zip(grp["names"], [t[0] for t in _unpack(b[None], shapes, 1)])) for b in outs]

    def adam_own_layout(parts, name, w, m, v):
        return [{name: t[None]} for t in _adamw(parts, w[0], m[0], v[0], "adamw_" + name)]

    sh_a = adam_group(gparts_a, grp_a, "adamw_sharded_a")
    sh_b = adam_group(gparts_b, grp_b, "adamw_sharded_b")
    sh_in = adam_own_layout(gparts_in, "w_in", w_in, m_w_in, v_w_in)
    sh_up = adam_own_layout(gparts_up, "w_up", w_up, m_w_up, v_w_up)

    dg_alog = da_heads[:, :SSM_HEADS] * a_neg
    repl_g = [dg_mix_pre, dg_mix_post, dg_ffn_pre, dg_ffn_post, dg_q, dg_kv, dg_attn_out, db_sconv,
              d_dt_bias[:, :SSM_HEADS], dg_alog, dd_heads[:, :SSM_HEADS], dg_ssm,
              jnp.concatenate([dbc_g, dbc_v], axis=1)]
    loss_vec = loss_part[:, :1]
    small_total = _round_up(sum(-(-int(np.prod(a.shape)) // PACK_W) for a in repl_g) + 1, 16)
    spack = _pack(repl_g + [loss_vec], 0, small_total, f32)
    gparts_meta, sparts = _exchange_tail([meta_blocks], [spack], "exchange_tail")
    sh_meta = adam_group(gparts_meta, grp_meta, "adamw_meta")
    loss_row, repl_out = _adamw_replicated(sparts, repl_w, repl_m, repl_v)
    loss = loss_row[0, 0]

    order = ["meta_tokens", "norm_mix_pre", "norm_mix_post", "norm_ffn_pre", "norm_ffn_post", "w_in", "q_a_norm",
             "w_uq", "kv_a_norm", "w_ukv", "attn_out_norm", "ssm_conv_w", "ssm_conv_b", "ssm_dt_bias", "ssm_A_log",
             "ssm_D", "ssm_norm", "w_out", "w_up", "ffn_conv_w", "ffn_conv_b", "w_down"]
    rp_names = ["norm_mix_pre", "norm_mix_post", "norm_ffn_pre", "norm_ffn_post", "q_a_norm", "kv_a_norm",
                "attn_out_norm", "ssm_conv_b", "ssm_dt_bias", "ssm_A_log", "ssm_D", "ssm_norm", "ffn_conv_b"]

    def lookup(k):
        d = {**sh_a[k], **sh_b[k], **sh_in[k], **sh_up[k], **sh_meta[k],
             **{n: four[k] for n, four in zip(rp_names, repl_out)}}
        return [d[n] for n in order]

    return (loss, grad_x, *lookup(0), *lookup(1), *lookup(2), *lookup(3))
```

```python
import math

import jax
import jax.numpy as jnp
import numpy as np
from jax import lax
from jax.experimental import pallas as pl
from jax.experimental.pallas import tpu as pltpu

f32 = jnp.float32
bf16 = jnp.bfloat16

D_MODEL = 1024
N_META = 16
MLA_HEADS = 8
QK_NOPE = 128
QK_ROPE = 64
V_DIM = 128
Q_RANK = 384
KV_RANK = 256
ROPE_THETA = 10000.0
SOFTMAX_SCALE = (QK_NOPE + QK_ROPE) ** -0.5
D_ATTN = MLA_HEADS * V_DIM
SSM_HEADS = 16
SSM_P = 64
SSM_GROUPS = 2
SSM_HPG = SSM_HEADS // SSM_GROUPS
SSM_N = 128
SSM_CONV = 4
CHUNK = 128
D_SSM = SSM_HEADS * SSM_P
D_BC = SSM_GROUPS * SSM_N
D_XBC = D_SSM + 2 * D_BC
D_FF = 2816
FFN_CONV = 3
EPS = 1e-6
QK_PAD = 256
N_DEV = 8

ADAM_LR = 0.001
ADAM_B1 = 0.9
ADAM_B2 = 0.999
ADAM_EPS = 1e-08
ADAM_WD = 0.01
ADAM_STEP = 10

LANES = 128
SUBLANES = 8
ROW_TILE = 256
VMEM_LIMIT = 56 * 1024 * 1024
PACK_W = 1024
PACK_ROW_TILE = 128
NEG = -1e30
LOG2E = math.log2(math.e)
LN2 = math.log(2.0)
Q_PRESCALE = SOFTMAX_SCALE * LOG2E

_MESH = pl.DeviceIdType.MESH


def _pick(n, prefs):
    for p in prefs:
        if n % p == 0:
            return p
    return n


def _rt(m):
    return _pick(m, (384, ROW_TILE))


def _cparams(sem):
    return pltpu.CompilerParams(dimension_semantics=sem, vmem_limit_bytes=VMEM_LIMIT)


def _row(spec_cols, tm):
    return pl.BlockSpec((tm, spec_cols), lambda i: (i, 0))


def _full(shape):
    nd = len(shape)
    return pl.BlockSpec(shape, lambda *a: (0,) * nd)


def _sigmoid(x):
    return 1.0 / (1.0 + jnp.exp(-x))


def _silu(x):
    return x * _sigmoid(x)


def _dsilu(x):
    s = _sigmoid(x)
    return s * (1.0 + x * (1.0 - s))


def _dot(a, b):
    return jnp.dot(a, b, preferred_element_type=f32)


def _dot_nt(a, b):
    return lax.dot_general(a, b, (((1,), (1,)), ((), ())), preferred_element_type=f32)


def _dot_tn(a, b):
    return lax.dot_general(a, b, (((0,), (0,)), ((), ())), preferred_element_type=f32)


def _dot_hi(a, b):
    return jnp.dot(a, b, precision=lax.Precision.HIGHEST, preferred_element_type=f32)


def _mm(a, b, trans_b, name):
    M, K = a.shape
    N = b.shape[0] if trans_b else b.shape[1]
    tm = _pick(M, (768, 512, 256))
    tn = _pick(N, (512, 1408, 384, 256, 128))

    def body(a_ref, b_ref, o_ref):
        o_ref[...] = _dot_nt(a_ref[...], b_ref[...]) if trans_b else _dot(a_ref[...], b_ref[...])

    b_spec = pl.BlockSpec((tn, K), lambda i, j: (j, 0)) if trans_b else pl.BlockSpec((K, tn), lambda i, j: (0, j))
    return pl.pallas_call(
        body, name=name, grid=(M // tm, N // tn),
        in_specs=[pl.BlockSpec((tm, K), lambda i, j: (i, 0)), b_spec],
        out_specs=pl.BlockSpec((tm, tn), lambda i, j: (i, j)),
        out_shape=jax.ShapeDtypeStruct((M, N), f32),
        compiler_params=_cparams(("parallel", "parallel")),
    )(a, b)


def _mm_tn(a, g, name):
    M, K = a.shape
    N = g.shape[1]
    tm = _pick(M, (768, 512, 256))
    tk = _pick(K, (1024, 1408, 512, 384, 256))
    tn = _pick(N, (1024, 1408, 512, 384, 256, 128))

    def body(a_ref, g_ref, o_ref):
        @pl.when(pl.program_id(2) == 0)
        def _():
            o_ref[...] = jnp.zeros_like(o_ref)

        o_ref[...] += _dot_tn(a_ref[...].astype(bf16), g_ref[...].astype(bf16))

    return pl.pallas_call(
        body, name=name, grid=(K // tk, N // tn, M // tm),
        in_specs=[pl.BlockSpec((tm, tk), lambda k, j, m: (m, k)),
                  pl.BlockSpec((tm, tn), lambda k, j, m: (m, j))],
        out_specs=pl.BlockSpec((tk, tn), lambda k, j, m: (k, j)),
        out_shape=jax.ShapeDtypeStruct((K, N), f32),
        compiler_params=_cparams(("parallel", "parallel", "arbitrary")),
    )(a, g)


def _rstd(x):
    return lax.rsqrt(jnp.mean(x * x, axis=-1, keepdims=True) + EPS)


def _rms_bwd_math(x, g, dy):
    r = _rstd(x)
    xh = x * r
    dn = dy * g
    dx = r * (dn - xh * jnp.mean(dn * xh, axis=-1, keepdims=True))
    return dx, dy * xh


def _rms_fwd(x, g, out_dtype, name):
    M, K = x.shape
    tm = _rt(M)

    def body(x_ref, g_ref, o_ref):
        xv = x_ref[...]
        o_ref[...] = (xv * _rstd(xv) * g_ref[...]).astype(out_dtype)

    return pl.pallas_call(
        body, name=name, grid=(M // tm,), in_specs=[_row(K, tm), _full((1, K))],
        out_specs=_row(K, tm), out_shape=jax.ShapeDtypeStruct((M, K), out_dtype),
        compiler_params=_cparams(("parallel",)),
    )(x, g)


def _in_proj(h0, g, weights):
    M, K = h0.shape
    tm = _rt(M)
    n = len(weights)
    widths = [int(w.shape[1]) for w in weights]

    def body(x_ref, g_ref, *refs):
        xv = x_ref[...]
        hn = (xv * _rstd(xv) * g_ref[...]).astype(bf16)
        refs[n][...] = hn
        for p in range(n):
            refs[n + 1 + p][...] = _dot(hn, refs[p][...])

    return pl.pallas_call(
        body, name="in_proj", grid=(M // tm,),
        in_specs=[_row(K, tm), _full((1, K))] + [_full((K, wd)) for wd in widths],
        out_specs=[_row(K, tm)] + [_row(wd, tm) for wd in widths],
        out_shape=[jax.ShapeDtypeStruct((M, K), bf16)] + [jax.ShapeDtypeStruct((M, wd), f32) for wd in widths],
        compiler_params=_cparams(("parallel",)),
    )(h0, g, *weights)


def _in_proj_dw(hn, grads):
    M, K = hn.shape
    tm = _rt(M)
    n = len(grads)
    widths = [int(gr.shape[1]) for gr in grads]

    def body(a_ref, *refs):
        @pl.when(pl.program_id(0) == 0)
        def _():
            for p in range(n):
                refs[n + p][...] = jnp.zeros_like(refs[n + p])

        a = a_ref[...]
        for p in range(n):
            refs[n + p][...] += _dot_tn(a, refs[p][...].astype(bf16))

    return pl.pallas_call(
        body, name="in_proj_dw", grid=(M // tm,),
        in_specs=[_row(K, tm)] + [_row(wd, tm) for wd in widths],
        out_specs=[_full((K, wd)) for wd in widths],
        out_shape=[jax.ShapeDtypeStruct((K, wd), f32) for wd in widths],
        compiler_params=_cparams(("arbitrary",)),
    )(hn, *grads)


def _in_proj_dx(grads, weights, h0, g, dh1, carried, scatter):
    M, K = h0.shape
    tm = _rt(M)
    nt = M // tm
    n = len(grads)
    nx = len(carried)
    widths = [int(w.shape[1]) for w in weights]

    def body(*refs):
        g_refs, w_refs = refs[:n], refs[n:2 * n]
        x_ref, gain_ref, r_ref = refs[2 * n:2 * n + 3]
        cin = refs[2 * n + 3:2 * n + 3 + nx]
        dx_ref, dg_ref = refs[2 * n + 3 + nx:2 * n + 5 + nx]
        cout = refs[2 * n + 5 + nx:2 * n + 5 + 2 * nx]
        i = pl.program_id(0)
        _hosted_exchange(cin, cout, refs[2 * n + 5 + 2 * nx:], scatter, i == 0, i == nt - 1)

        @pl.when(i == 0)
        def _():
            dg_ref[...] = jnp.zeros_like(dg_ref)

        d_hn = None
        for p in range(n):
            t = _dot_nt(g_refs[p][...].astype(bf16), w_refs[p][...])
            d_hn = t if d_hn is None else d_hn + t
        dx, dgp = _rms_bwd_math(x_ref[...], gain_ref[...], d_hn)
        dx_ref[...] = dx + r_ref[...]
        dg_ref[...] += jnp.sum(dgp, axis=0, keepdims=True)

    any_spec = pl.BlockSpec(memory_space=pl.ANY)
    return pl.pallas_call(
        body, name="in_proj_dx", grid=(nt,),
        in_specs=([_row(wd, tm) for wd in widths] + [_full((K, wd)) for wd in widths]
                  + [_row(K, tm), _full((1, K)), _row(K, tm)] + [any_spec] * nx),
        out_specs=[_row(K, tm), _full((1, K))] + [any_spec] * nx,
        out_shape=[jax.ShapeDtypeStruct((M, K), f32), jax.ShapeDtypeStruct((1, K), f32)]
        + _exchange_shapes(carried, scatter),
        scratch_shapes=_exchange_sems(nx),
        compiler_params=_cparams(("arbitrary",)),
    )(*grads, *weights, h0, g, dh1, *carried)


def _out_proj_resid(an, ssm, wa, ws, h0, g2, g3):
    M, K = h0.shape
    tm = _rt(M)

    def body(a_ref, s_ref, wa_ref, ws_ref, h_ref, g2_ref, g3_ref, m_ref, h1_ref, hn_ref):
        mv = _dot(a_ref[...], wa_ref[...]) + _dot(s_ref[...], ws_ref[...])
        m_ref[...] = mv
        h1 = h_ref[...] + mv * _rstd(mv) * g2_ref[...]
        h1_ref[...] = h1
        hn_ref[...] = (h1 * _rstd(h1) * g3_ref[...]).astype(bf16)

    return pl.pallas_call(
        body, name="out_proj_resid", grid=(M // tm,),
        in_specs=[_row(an.shape[1], tm), _row(ssm.shape[1], tm), _full(wa.shape), _full(ws.shape),
                  _row(K, tm), _full((1, K)), _full((1, K))],
        out_specs=[_row(K, tm), _row(K, tm), _row(K, tm)],
        out_shape=[jax.ShapeDtypeStruct((M, K), f32), jax.ShapeDtypeStruct((M, K), f32),
                   jax.ShapeDtypeStruct((M, K), bf16)],
        compiler_params=_cparams(("parallel",)),
    )(an, ssm, wa, ws, h0, g2, g3)


def _final(h1, act, wdown, g4, tgt, n_real):
    M, K = h1.shape
    F = act.shape[1]
    tm = _rt(M)
    nt = M // tm

    def body(h_ref, a_ref, w_ref, g_ref, t_ref, dh_ref, dd_ref, dg_ref, ls_ref, acc_ref):
        i = pl.program_id(0)

        @pl.when(i == 0)
        def _():
            dg_ref[...] = jnp.zeros_like(dg_ref)
            acc_ref[...] = jnp.zeros_like(acc_ref)

        dv = _dot(a_ref[...], w_ref[...])
        g = g_ref[...]
        r = _rstd(dv)
        n = dv * r
        h2 = h_ref[...] + n * g
        rows = i * tm + lax.broadcasted_iota(jnp.int32, (tm, 1), 0)
        mask = ((rows >= N_META) & (rows < n_real)).astype(f32)
        diff = (h2 - t_ref[...]) * mask
        acc_ref[...] += jnp.sum(diff * diff, axis=0, keepdims=True)
        dh = diff * (1.0 / K)
        dh_ref[...] = dh
        dn = dh * g
        dd_ref[...] = (r * (dn - n * jnp.mean(dn * n, axis=-1, keepdims=True))).astype(bf16)
        dg_ref[...] += jnp.sum(dh * n, axis=0, keepdims=True)

        @pl.when(i == nt - 1)
        def _():
            ls_ref[...] = jnp.zeros((1, LANES), f32) + jnp.sum(acc_ref[...]) * (0.5 / K)

    return pl.pallas_call(
        body, name="ffn_down_loss", grid=(nt,),
        in_specs=[_row(K, tm), _row(F, tm), _full((F, K)), _full((1, K)), _row(K, tm)],
        out_specs=[_row(K, tm), _row(K, tm), _full((1, K)), _full((1, LANES))],
        out_shape=[jax.ShapeDtypeStruct((M, K), f32), jax.ShapeDtypeStruct((M, K), bf16),
                   jax.ShapeDtypeStruct((1, K), f32), jax.ShapeDtypeStruct((1, LANES), f32)],
        scratch_shapes=[pltpu.VMEM((1, K), f32)],
        compiler_params=_cparams(("arbitrary",)),
    )(h1, act, wdown, g4, tgt)


def _mid_bwd(h1, g3, dup_g, dup_v, wup, dh2, mix, g2):
    M, K = h1.shape
    F = dup_g.shape[1]
    tm = ROW_TILE

    def body(h_ref, g3_ref, ag_ref, av_ref, w_ref, dh2_ref, m_ref, g2_ref, dh1_ref, dm_ref, dg3_ref, dg2_ref):
        @pl.when(pl.program_id(0) == 0)
        def _():
            dg3_ref[...] = jnp.zeros_like(dg3_ref)
            dg2_ref[...] = jnp.zeros_like(dg2_ref)

        d_hn2 = _dot_nt(ag_ref[...], w_ref[:, 0:F]) + _dot_nt(av_ref[...], w_ref[:, F:2 * F])
        dx, dgp = _rms_bwd_math(h_ref[...], g3_ref[...], d_hn2)
        dh1 = dh2_ref[...] + dx
        dh1_ref[...] = dh1
        dg3_ref[...] += jnp.sum(dgp, axis=0, keepdims=True)
        dm, dgp2 = _rms_bwd_math(m_ref[...], g2_ref[...], dh1)
        dm_ref[...] = dm.astype(bf16)
        dg2_ref[...] += jnp.sum(dgp2, axis=0, keepdims=True)

    return pl.pallas_call(
        body, name="ffn_up_dx_mid_bwd", grid=(M // tm,),
        in_specs=[_row(K, tm), _full((1, K)), _row(F, tm), _row(F, tm), _full((K, 2 * F)), _row(K, tm),
                  _row(K, tm), _full((1, K))],
        out_specs=[_row(K, tm), _row(K, tm), _full((1, K)), _full((1, K))],
        out_shape=[jax.ShapeDtypeStruct((M, K), f32), jax.ShapeDtypeStruct((M, K), bf16),
                   jax.ShapeDtypeStruct((1, K), f32), jax.ShapeDtypeStruct((1, K), f32)],
        compiler_params=_cparams(("arbitrary",)),
    )(h1, g3, dup_g, dup_v, wup, dh2, mix, g2)


HEADS_PER_STEP = 4
CONV_RB = 16


def _conv_block_taps(x_ref, halo, rb, lanes, kw):
    r0 = rb * CONV_RB
    if rb == 0:
        cat = jnp.concatenate([halo, x_ref[0:CONV_RB, lanes]], axis=0)
        first = SUBLANES - (kw - 1)
        return [cat[first + k:first + k + CONV_RB] for k in range(kw)]
    return [x_ref[r0 - (kw - 1) + k:r0 - (kw - 1) + k + CONV_RB, lanes] for k in range(kw)]


def _conv_weighted(taps, w, kw):
    u = None
    for k in range(kw):
        t = taps[k] * w[k:k + 1, :]
        u = t if u is None else u + t
    return u


def _conv_block_dx(du, nxt, w, kw):
    cat = jnp.concatenate([du, nxt], axis=0)
    return _conv_weighted([cat[kw - 1 - k:kw - 1 - k + CONV_RB] for k in range(kw)], w, kw)


def _prev_spec(tm, tc, col_of, row_axis, reversed_tiles=0):
    def imap(*ids):
        i = ids[row_axis]
        if reversed_tiles:
            i = reversed_tiles - 1 - i
        return (jnp.maximum(i * (tm // SUBLANES) - 1, 0), col_of(*ids))
    return pl.BlockSpec((SUBLANES, tc), imap)


def _ssm_conv_fwd(xbc, w, b):
    M, C = xbc.shape
    tm, tc, kw = ROW_TILE, C, SSM_CONV

    def body(x_ref, h_ref, w_ref, b_ref, o_ref):
        i = pl.program_id(0)

        def chunk(j, carry):
            lanes = pl.ds(pl.multiple_of(j * LANES, LANES), LANES)
            halo = jnp.where(i == 0, 0.0, h_ref[:, lanes])
            wv = w_ref[:, lanes]
            bv = b_ref[:, lanes]
            for rb in range(tm // CONV_RB):
                u = _conv_weighted(_conv_block_taps(x_ref, halo, rb, lanes, kw), wv, kw) + bv
                o_ref[rb * CONV_RB:(rb + 1) * CONV_RB, lanes] = _silu(u)
            return carry

        lax.fori_loop(0, tc // LANES, chunk, 0)

    return pl.pallas_call(
        body, name="ssm_conv_fwd", grid=(M // tm, C // tc),
        in_specs=[pl.BlockSpec((tm, tc), lambda i, j: (i, j)),
                  _prev_spec(tm, tc, lambda i, j: j, 0),
                  pl.BlockSpec((SUBLANES, tc), lambda i, j: (0, j)),
                  pl.BlockSpec((1, tc), lambda i, j: (0, j))],
        out_specs=pl.BlockSpec((tm, tc), lambda i, j: (i, j)),
        out_shape=jax.ShapeDtypeStruct((M, C), f32),
        compiler_params=_cparams(("parallel", "parallel")),
    )(xbc, xbc, w, b)


def _ssm_conv_bwd(xbc, w, b, dout):
    M, C = xbc.shape
    tm, tc, kw = ROW_TILE, C // 3, SSM_CONV
    nt = M // tm

    def body(x_ref, h_ref, w_ref, b_ref, d_ref, dx_ref, dw_ref, db_ref, nxt_ref):
        i = pl.program_id(1)

        @pl.when(i == 0)
        def _():
            dw_ref[...] = jnp.zeros_like(dw_ref)
            db_ref[...] = jnp.zeros_like(db_ref)
            nxt_ref[...] = jnp.zeros_like(nxt_ref)

        def chunk(j, carry):
            lanes = pl.ds(pl.multiple_of(j * LANES, LANES), LANES)
            halo = jnp.where(i == nt - 1, 0.0, h_ref[:, lanes])
            wv = w_ref[:, lanes]
            bv = b_ref[:, lanes]
            nxt = nxt_ref[:, lanes]
            db = jnp.zeros((CONV_RB, LANES), f32)
            dw = [jnp.zeros((CONV_RB, LANES), f32) for _ in range(kw)]
            for rb in reversed(range(tm // CONV_RB)):
                rows = slice(rb * CONV_RB, (rb + 1) * CONV_RB)
                taps = _conv_block_taps(x_ref, halo, rb, lanes, kw)
                du = d_ref[rows, lanes] * _dsilu(_conv_weighted(taps, wv, kw) + bv)
                db = db + du
                dw = [dw[k] + du * taps[k] for k in range(kw)]
                dx_ref[rows, lanes] = _conv_block_dx(du, nxt, wv, kw).astype(bf16)
                nxt = du[0:SUBLANES]
            nxt_ref[:, lanes] = nxt
            db_ref[:, lanes] += jnp.sum(db, axis=0, keepdims=True)
            for k in range(kw):
                dw_ref[k:k + 1, lanes] += jnp.sum(dw[k], axis=0, keepdims=True)
            return carry

        lax.fori_loop(0, tc // LANES, chunk, 0)

    tile = pl.BlockSpec((tm, tc), lambda j, i: (nt - 1 - i, j))
    return pl.pallas_call(
        body, name="ssm_conv_bwd", grid=(C // tc, nt),
        in_specs=[tile, _prev_spec(tm, tc, lambda j, i: j, 1, nt),
                  pl.BlockSpec((SUBLANES, tc), lambda j, i: (0, j)),
                  pl.BlockSpec((1, tc), lambda j, i: (0, j)), tile],
        out_specs=[tile, pl.BlockSpec((SUBLANES, tc), lambda j, i: (0, j)),
                   pl.BlockSpec((1, tc), lambda j, i: (0, j))],
        out_shape=[jax.ShapeDtypeStruct((M, C), bf16), jax.ShapeDtypeStruct((SUBLANES, C), f32),
                   jax.ShapeDtypeStruct((1, C), f32)],
        scratch_shapes=[pltpu.VMEM((SUBLANES, tc), f32)],
        compiler_params=_cparams(("parallel", "arbitrary")),
    )(xbc, xbc, w, b, dout)


def _ffn_gate_fwd(up, w, b):
    M = up.shape[0]
    tm, tc, kw = ROW_TILE, D_FF // 2, FFN_CONV
    nc = D_FF // tc

    def body(xg_ref, hg_ref, xv_ref, hv_ref, wg_ref, wv_ref, bg_ref, bv_ref, o_ref):
        i = pl.program_id(0)

        def chunk(j, carry):
            lanes = pl.ds(pl.multiple_of(j * LANES, LANES), LANES)
            halo_g = jnp.where(i == 0, 0.0, hg_ref[:, lanes])
            halo_v = jnp.where(i == 0, 0.0, hv_ref[:, lanes])
            wg, wv = wg_ref[:, lanes], wv_ref[:, lanes]
            bg, bv = bg_ref[:, lanes], bv_ref[:, lanes]
            for rb in range(tm // CONV_RB):
                ug = _conv_weighted(_conv_block_taps(xg_ref, halo_g, rb, lanes, kw), wg, kw) + bg
                uv = _conv_weighted(_conv_block_taps(xv_ref, halo_v, rb, lanes, kw), wv, kw) + bv
                o_ref[rb * CONV_RB:(rb + 1) * CONV_RB, lanes] = (_silu(ug) * uv).astype(bf16)
            return carry

        lax.fori_loop(0, tc // LANES, chunk, 0)

    return pl.pallas_call(
        body, name="ffn_gate_fwd", grid=(M // tm, nc),
        in_specs=[pl.BlockSpec((tm, tc), lambda i, j: (i, j)),
                  _prev_spec(tm, tc, lambda i, j: j, 0),
                  pl.BlockSpec((tm, tc), lambda i, j: (i, j + nc)),
                  _prev_spec(tm, tc, lambda i, j: j + nc, 0),
                  pl.BlockSpec((SUBLANES, tc), lambda i, j: (0, j)),
                  pl.BlockSpec((SUBLANES, tc), lambda i, j: (0, j + nc)),
                  pl.BlockSpec((1, tc), lambda i, j: (0, j)),
                  pl.BlockSpec((1, tc), lambda i, j: (0, j + nc))],
        out_specs=pl.BlockSpec((tm, tc), lambda i, j: (i, j)),
        out_shape=jax.ShapeDtypeStruct((M, D_FF), bf16),
        compiler_params=_cparams(("parallel", "parallel")),
    )(up, up, up, up, w, w, b, b)


def _ffn_gate_bwd(up, w, b, d_down, wdown):
    M = up.shape[0]
    K = d_down.shape[1]
    tm, tc, kw = ROW_TILE, D_FF // 2, FFN_CONV
    nc = D_FF // tc
    nt = M // tm

    def body(xg_ref, hg_ref, xv_ref, hv_ref, wg_ref, wv_ref, bg_ref, bv_ref, dd_ref, wd_ref,
             dxg_ref, dxv_ref, dwg_ref, dwv_ref, dbg_ref, dbv_ref, ng_ref, nv_ref, d_ref):
        i = pl.program_id(1)

        @pl.when(i == 0)
        def _():
            for r in (dwg_ref, dwv_ref, dbg_ref, dbv_ref, ng_ref, nv_ref):
                r[...] = jnp.zeros_like(r)

        d_ref[...] = _dot_nt(dd_ref[...], wd_ref[...])

        def chunk(j, carry):
            lanes = pl.ds(pl.multiple_of(j * LANES, LANES), LANES)
            halo_g = jnp.where(i == nt - 1, 0.0, hg_ref[:, lanes])
            halo_v = jnp.where(i == nt - 1, 0.0, hv_ref[:, lanes])
            wg, wv = wg_ref[:, lanes], wv_ref[:, lanes]
            bg, bv = bg_ref[:, lanes], bv_ref[:, lanes]
            nxt_g, nxt_v = ng_ref[:, lanes], nv_ref[:, lanes]
            zero = jnp.zeros((CONV_RB, LANES), f32)
            dbg, dbv = zero, zero
            dwg = [zero for _ in range(kw)]
            dwv = [zero for _ in range(kw)]
            for rb in reversed(range(tm // CONV_RB)):
                rows = slice(rb * CONV_RB, (rb + 1) * CONV_RB)
                tg = _conv_block_taps(xg_ref, halo_g, rb, lanes, kw)
                tv = _conv_block_taps(xv_ref, halo_v, rb, lanes, kw)
                ug = _conv_weighted(tg, wg, kw) + bg
                uv = _conv_weighted(tv, wv, kw) + bv
                sg = _sigmoid(ug)
                da = d_ref[rows, lanes]
                dug = da * uv * (sg * (1.0 + ug * (1.0 - sg)))
                duv = da * (ug * sg)
                dbg = dbg + dug
                dbv = dbv + duv
                dwg = [dwg[k] + dug * tg[k] for k in range(kw)]
                dwv = [dwv[k] + duv * tv[k] for k in range(kw)]
                dxg_ref[rows, lanes] = _conv_block_dx(dug, nxt_g, wg, kw).astype(bf16)
                dxv_ref[rows, lanes] = _conv_block_dx(duv, nxt_v, wv, kw).astype(bf16)
                nxt_g, nxt_v = dug[0:SUBLANES], duv[0:SUBLANES]
            ng_ref[:, lanes] = nxt_g
            nv_ref[:, lanes] = nxt_v
            dbg_ref[:, lanes] += jnp.sum(dbg, axis=0, keepdims=True)
            dbv_ref[:, lanes] += jnp.sum(dbv, axis=0, keepdims=True)
            for k in range(kw):
                dwg_ref[k:k + 1, lanes] += jnp.sum(dwg[k], axis=0, keepdims=True)
                dwv_ref[k:k + 1, lanes] += jnp.sum(dwv[k], axis=0, keepdims=True)
            return carry

        lax.fori_loop(0, tc // LANES, chunk, 0)

    tile_g = pl.BlockSpec((tm, tc), lambda j, i: (nt - 1 - i, j))
    tile_v = pl.BlockSpec((tm, tc), lambda j, i: (nt - 1 - i, j + nc))
    ext = pltpu.VMEM((SUBLANES, tc), f32)
    return pl.pallas_call(
        body, name="ffn_gate_bwd", grid=(nc, nt),
        in_specs=[tile_g, _prev_spec(tm, tc, lambda j, i: j, 1, nt),
                  tile_v, _prev_spec(tm, tc, lambda j, i: j + nc, 1, nt),
                  pl.BlockSpec((SUBLANES, tc), lambda j, i: (0, j)),
                  pl.BlockSpec((SUBLANES, tc), lambda j, i: (0, j + nc)),
                  pl.BlockSpec((1, tc), lambda j, i: (0, j)),
                  pl.BlockSpec((1, tc), lambda j, i: (0, j + nc)),
                  pl.BlockSpec((tm, K), lambda j, i: (nt - 1 - i, 0)),
                  pl.BlockSpec((tc, K), lambda j, i: (j, 0))],
        out_specs=[tile_g, tile_g,
                   pl.BlockSpec((SUBLANES, tc), lambda j, i: (0, j)),
                   pl.BlockSpec((SUBLANES, tc), lambda j, i: (0, j)),
                   pl.BlockSpec((1, tc), lambda j, i: (0, j)),
                   pl.BlockSpec((1, tc), lambda j, i: (0, j))],
        out_shape=[jax.ShapeDtypeStruct((M, D_FF), bf16), jax.ShapeDtypeStruct((M, D_FF), bf16),
                   jax.ShapeDtypeStruct((SUBLANES, D_FF), f32), jax.ShapeDtypeStruct((SUBLANES, D_FF), f32),
                   jax.ShapeDtypeStruct((1, D_FF), f32), jax.ShapeDtypeStruct((1, D_FF), f32)],
        scratch_shapes=[ext, ext, pltpu.VMEM((tm, tc), f32)],
        compiler_params=_cparams(("parallel", "arbitrary")),
    )(up, up, up, up, w, w, b, b, d_down, wdown)


def _rope_apply(blk, cos, sin):
    lane = lax.broadcasted_iota(jnp.int32, blk.shape, 1)
    half = QK_ROPE // 2
    partner = jnp.where(lane < half, pltpu.roll(blk, LANES - half, 1), pltpu.roll(blk, half, 1))
    return blk * cos + partner * sin


def _rope_unapply(d, cos, sin):
    t = d * sin
    lane = lax.broadcasted_iota(jnp.int32, d.shape, 1)
    half = QK_ROPE // 2
    partner = jnp.where(lane < half, pltpu.roll(t, LANES - half, 1), pltpu.roll(t, half, 1))
    return d * cos + partner


def _up_q_rope(q_c, g, wuq, cos, sin):
    M, K = q_c.shape
    tm = _pick(M, (768, 512, 256))

    hs = HEADS_PER_STEP

    def body(x_ref, g_ref, b_ref, c_ref, s_ref, a_ref, o_ref):
        xv = x_ref[...]
        a = (xv * _rstd(xv) * g_ref[...]).astype(bf16)
        a_ref[...] = a
        r = _dot(a, b_ref[...]) * Q_PRESCALE
        c, s = c_ref[...], s_ref[...]
        for u in range(hs):
            o_ref[u, :, 0:QK_NOPE] = r[:, u * QK_PAD:u * QK_PAD + QK_NOPE].astype(bf16)
            o_ref[u, :, QK_NOPE:QK_PAD] = _rope_apply(r[:, u * QK_PAD + QK_NOPE:(u + 1) * QK_PAD], c, s).astype(bf16)

    return pl.pallas_call(
        body, name="up_q_rope", grid=(M // tm, MLA_HEADS // hs),
        in_specs=[pl.BlockSpec((tm, K), lambda i, h: (i, 0)),
                  pl.BlockSpec((1, K), lambda i, h: (0, 0)),
                  pl.BlockSpec((K, hs * QK_PAD), lambda i, h: (0, h)),
                  pl.BlockSpec((tm, LANES), lambda i, h: (i, 0)),
                  pl.BlockSpec((tm, LANES), lambda i, h: (i, 0))],
        out_specs=[pl.BlockSpec((tm, K), lambda i, h: (i, 0)),
                   pl.BlockSpec((hs, tm, QK_PAD), lambda i, h: (h, i, 0))],
        out_shape=[jax.ShapeDtypeStruct((M, K), bf16), jax.ShapeDtypeStruct((MLA_HEADS, M, QK_PAD), bf16)],
        compiler_params=_cparams(("parallel", "arbitrary")),
    )(q_c, g, wuq, cos, sin)


def _up_kv_rope(kv_c, g, wukv, kpe_raw, cos, sin):
    M, K = kv_c.shape
    tm = _pick(M, (768, 512, 256))

    hs = HEADS_PER_STEP
    w = QK_NOPE + V_DIM

    def body(x_ref, g_ref, b_ref, pe_ref, c_ref, s_ref, a_ref, k_ref, v_ref):
        xv = x_ref[...]
        a = (xv * _rstd(xv) * g_ref[...]).astype(bf16)
        a_ref[...] = a
        r = _dot(a, b_ref[...])
        pe = _rope_apply(pe_ref[...], c_ref[...], s_ref[...]).astype(bf16)
        for u in range(hs):
            k_ref[u, :, 0:QK_NOPE] = r[:, u * w:u * w + QK_NOPE].astype(bf16)
            k_ref[u, :, QK_NOPE:QK_PAD] = pe
            v_ref[u] = r[:, u * w + QK_NOPE:(u + 1) * w].astype(bf16)

    return pl.pallas_call(
        body, name="up_kv_rope", grid=(M // tm, MLA_HEADS // hs),
        in_specs=[pl.BlockSpec((tm, K), lambda i, h: (i, 0)),
                  pl.BlockSpec((1, K), lambda i, h: (0, 0)),
                  pl.BlockSpec((K, hs * w), lambda i, h: (0, h)),
                  pl.BlockSpec((tm, LANES), lambda i, h: (i, 0)),
                  pl.BlockSpec((tm, LANES), lambda i, h: (i, 0)),
                  pl.BlockSpec((tm, LANES), lambda i, h: (i, 0))],
        out_specs=[pl.BlockSpec((tm, K), lambda i, h: (i, 0)),
                   pl.BlockSpec((hs, tm, QK_PAD), lambda i, h: (h, i, 0)),
                   pl.BlockSpec((hs, tm, V_DIM), lambda i, h: (h, i, 0))],
        out_shape=[jax.ShapeDtypeStruct((M, K), bf16), jax.ShapeDtypeStruct((MLA_HEADS, M, QK_PAD), bf16),
                   jax.ShapeDtypeStruct((MLA_HEADS, M, V_DIM), bf16)],
        compiler_params=_cparams(("parallel", "arbitrary")),
    )(kv_c, g, wukv, kpe_raw, cos, sin)


def _latent_bwd(d_full_sc, w_ref, x_ref, g_ref, a_ref, dx_ref, dg_ref, dw_ref):
    d_full = d_full_sc[...]
    dx, dgp = _rms_bwd_math(x_ref[...], g_ref[...], _dot_nt(d_full, w_ref[...]))
    dx_ref[...] = dx.astype(bf16)
    dg_ref[...] += jnp.sum(dgp, axis=0, keepdims=True)
    dw_ref[...] += _dot_tn(a_ref[...], d_full)


def _latent_bwd_call(body, name, head_inputs, head_specs, cos, sin, w, x, g, a, extra_out_specs, extra_out_shape):
    M, K = x.shape
    tm = _rt(M)
    N = w.shape[1]
    return pl.pallas_call(
        body, name=name, grid=(M // tm,),
        in_specs=head_specs + [_row(LANES, tm), _row(LANES, tm), _full((K, N)), _row(K, tm), _full((1, K)),
                               _row(K, tm)],
        out_specs=[_row(K, tm), _full((1, K)), _full((K, N))] + extra_out_specs,
        out_shape=[jax.ShapeDtypeStruct((M, K), bf16), jax.ShapeDtypeStruct((1, K), f32),
                   jax.ShapeDtypeStruct((K, N), f32)] + extra_out_shape,
        scratch_shapes=[pltpu.VMEM((tm, N), bf16)],
        compiler_params=_cparams(("arbitrary",)),
    )(*head_inputs, cos, sin, w, x, g, a)


def _q_branch_bwd(dq, cos, sin, wuq, q_c, g, qn):
    tm = _rt(q_c.shape[0])

    def body(d_ref, c_ref, s_ref, w_ref, x_ref, g_ref, a_ref, dx_ref, dg_ref, dw_ref, full_sc):
        @pl.when(pl.program_id(0) == 0)
        def _():
            dg_ref[...] = jnp.zeros_like(dg_ref)
            dw_ref[...] = jnp.zeros_like(dw_ref)

        c, s = c_ref[...], s_ref[...]
        for h in range(MLA_HEADS):
            full_sc[:, h * QK_PAD:h * QK_PAD + QK_NOPE] = (d_ref[h, :, 0:QK_NOPE] * SOFTMAX_SCALE).astype(bf16)
            full_sc[:, h * QK_PAD + QK_NOPE:(h + 1) * QK_PAD] = (_rope_unapply(
                d_ref[h, :, QK_NOPE:QK_PAD], c, s) * SOFTMAX_SCALE).astype(bf16)
        _latent_bwd(full_sc, w_ref, x_ref, g_ref, a_ref, dx_ref, dg_ref, dw_ref)

    return _latent_bwd_call(body, "q_branch_bwd", [dq],
                            [pl.BlockSpec((MLA_HEADS, tm, QK_PAD), lambda i: (0, i, 0))],
                            cos, sin, wuq, q_c, g, qn, [], [])


def _kv_branch_bwd(dk, dv, cos, sin, wukv, kv_c, g, kvn):
    M = kv_c.shape[0]
    tm = _rt(M)
    w = QK_NOPE + V_DIM

    def body(dk_ref, dv_ref, c_ref, s_ref, w_ref, x_ref, g_ref, a_ref, dx_ref, dg_ref, dw_ref, pe_ref, full_sc):
        @pl.when(pl.program_id(0) == 0)
        def _():
            dg_ref[...] = jnp.zeros_like(dg_ref)
            dw_ref[...] = jnp.zeros_like(dw_ref)

        pe = None
        for h in range(MLA_HEADS):
            full_sc[:, h * w:h * w + QK_NOPE] = dk_ref[h, :, 0:QK_NOPE].astype(bf16)
            full_sc[:, h * w + QK_NOPE:(h + 1) * w] = dv_ref[h].astype(bf16)
            t = dk_ref[h, :, QK_NOPE:QK_PAD]
            pe = t if pe is None else pe + t
        pe_ref[...] = _rope_unapply(pe, c_ref[...], s_ref[...])
        _latent_bwd(full_sc, w_ref, x_ref, g_ref, a_ref, dx_ref, dg_ref, dw_ref)

    return _latent_bwd_call(body, "kv_branch_bwd", [dk, dv],
                            [pl.BlockSpec((MLA_HEADS, tm, QK_PAD), lambda i: (0, i, 0)),
                             pl.BlockSpec((MLA_HEADS, tm, V_DIM), lambda i: (0, i, 0))],
                            cos, sin, wukv, kv_c, g, kvn, [_row(LANES, tm)],
                            [jax.ShapeDtypeStruct((M, LANES), f32)])


def _attn_tile(M):
    return 768 if (M % 768 == 0 and M >= 4 * 768) else ROW_TILE


def _col_to_row(col):
    return col.T[0:1, :]


def _hosted_exchange(refs_in, refs_out, sems, scatter, first, last):
    copies = _exchange_copies(refs_in, refs_out, *sems, scatter)

    @pl.when(first)
    def _():
        for cp in copies:
            cp.start()

    @pl.when(last)
    def _():
        for cp in copies:
            cp.wait()


def _flash_fwd(q, k, v, carried, scatter):
    H, M, _ = q.shape
    T = _attn_tile(M)
    nq = M // T
    nx = len(carried)

    def body(*refs):
        q_ref, k_ref, v_ref = refs[:3]
        o_ref, lse_ref = refs[3 + nx:5 + nx]
        sa_ref, sb_ref, m_sc, l_sc, acc_sc = refs[5 + 2 * nx:10 + 2 * nx]
        h = pl.program_id(0)
        i = pl.program_id(1)
        _hosted_exchange(refs[3:3 + nx], refs[5 + nx:5 + 2 * nx], refs[10 + 2 * nx:], scatter,
                         (h == 0) & (i == 0), (h == H - 1) & (i == nq - 1))
        qv = q_ref[0]
        m_sc[...] = jnp.full_like(m_sc, NEG)
        l_sc[...] = jnp.zeros_like(l_sc)
        acc_sc[...] = jnp.zeros_like(acc_sc)

        def scores(j, s_ref):
            off = pl.multiple_of(j * T, T)
            s_ref[...] = _dot_nt(qv, k_ref[0, pl.ds(off, T), :])

        def softmax_pv(j, s_ref, masked):
            off = pl.multiple_of(j * T, T)
            s = s_ref[...]
            if masked:
                r = lax.broadcasted_iota(jnp.int32, (T, T), 0)
                c = lax.broadcasted_iota(jnp.int32, (T, T), 1)
                s = jnp.where(r >= c, s, NEG)
            m_prev = m_sc[...]
            m_new = jnp.maximum(m_prev, jnp.max(s, axis=1, keepdims=True))
            alpha = jnp.exp2(m_prev - m_new)
            p = jnp.exp2(s - m_new[:, 0:1])
            l_sc[...] = alpha * l_sc[...] + jnp.sum(p, axis=1, keepdims=True)
            acc_sc[...] = alpha * acc_sc[...] + _dot(p.astype(bf16), v_ref[0, pl.ds(off, T), :])
            m_sc[...] = m_new

        scores(0, sa_ref)

        def pair(jj, c):
            j0 = 2 * jj
            scores(j0 + 1, sb_ref)
            softmax_pv(j0, sa_ref, False)
            scores(j0 + 2, sa_ref)
            softmax_pv(j0 + 1, sb_ref, False)
            return c

        lax.fori_loop(0, i // 2, pair, 0)

        @pl.when(i % 2 == 0)
        def _():
            softmax_pv(i, sa_ref, True)

        @pl.when(i % 2 == 1)
        def _():
            scores(i, sb_ref)
            softmax_pv(i - 1, sa_ref, False)
            softmax_pv(i, sb_ref, True)

        l = l_sc[...]
        o_ref[...] = acc_sc[...] / l
        lse_ref[0, 0] = _col_to_row(m_sc[...] + jnp.log2(l))

    any_spec = pl.BlockSpec(memory_space=pl.ANY)
    return pl.pallas_call(
        body, name="flash_fwd", grid=(H, nq),
        in_specs=[pl.BlockSpec((1, T, QK_PAD), lambda h, i: (h, i, 0)),
                  pl.BlockSpec((1, M, QK_PAD), lambda h, i: (h, 0, 0)),
                  pl.BlockSpec((1, M, V_DIM), lambda h, i: (h, 0, 0))] + [any_spec] * nx,
        out_specs=[pl.BlockSpec((T, V_DIM), lambda h, i: (i, h)),
                   pl.BlockSpec((1, 1, 1, T), lambda h, i: (h, i, 0, 0))] + [any_spec] * nx,
        out_shape=[jax.ShapeDtypeStruct((M, H * V_DIM), f32),
                   jax.ShapeDtypeStruct((H, nq, 1, T), f32)] + _exchange_shapes(carried, scatter),
        scratch_shapes=[pltpu.VMEM((T, T), f32), pltpu.VMEM((T, T), f32),
                        pltpu.VMEM((T, LANES), f32), pltpu.VMEM((T, LANES), f32),
                        pltpu.VMEM((T, V_DIM), f32)] + _exchange_sems(nx),
        compiler_params=_cparams(("arbitrary", "arbitrary")),
    )(q, k, v, *carried)


def _attn_out_bwd(o, g, d_an):
    M, K = o.shape
    H = MLA_HEADS
    T = _attn_tile(M)

    def body(o_ref, g_ref, d_ref, dh_ref, dl_ref, dg_ref):
        @pl.when(pl.program_id(0) == 0)
        def _():
            dg_ref[...] = jnp.zeros_like(dg_ref)

        ov = o_ref[...]
        do, dgp = _rms_bwd_math(ov, g_ref[...], d_ref[...])
        dg_ref[...] += jnp.sum(dgp, axis=0, keepdims=True)
        for h in range(H):
            sl = slice(h * V_DIM, (h + 1) * V_DIM)
            doh = do[:, sl]
            dh_ref[h] = doh.astype(bf16)
            col = jnp.sum(ov[:, sl] * doh, axis=1, keepdims=True) + jnp.zeros((T, LANES), f32)
            dl_ref[h, 0] = _col_to_row(col)

    return pl.pallas_call(
        body, name="attn_out_bwd", grid=(M // T,),
        in_specs=[_row(K, T), _full((1, K)), _row(K, T)],
        out_specs=[pl.BlockSpec((H, T, V_DIM), lambda i: (0, i, 0)),
                   pl.BlockSpec((H, 1, 1, T), lambda i: (0, i, 0, 0)),
                   _full((1, K))],
        out_shape=[jax.ShapeDtypeStruct((H, M, V_DIM), bf16),
                   jax.ShapeDtypeStruct((H, M // T, 1, T), f32),
                   jax.ShapeDtypeStruct((1, K), f32)],
        compiler_params=_cparams(("arbitrary",)),
    )(o, g, d_an)


def _flash_bwd(q, k, v, do, lse, delta, carried, scatter):
    H, M, _ = q.shape
    T = _attn_tile(M)
    nq = M // T
    nx = len(carried)

    def body(*refs):
        q_ref, do_ref, lse_ref, dl_ref, k_ref, v_ref = refs[:6]
        dq_ref, dk_ref, dv_ref = refs[6 + nx:9 + nx]
        dk_sc, dv_sc = refs[9 + 2 * nx:11 + 2 * nx]
        j = pl.program_id(1)
        _hosted_exchange(refs[6:6 + nx], refs[9 + nx:9 + 2 * nx], refs[11 + 2 * nx:], scatter,
                         (pl.program_id(0) == 0) & (j == 0), (pl.program_id(0) == H - 1) & (j == nq - 1))

        @pl.when(j == 0)
        def _():
            dq_ref[...] = jnp.zeros_like(dq_ref)

        kt = k_ref[0]
        vt = v_ref[0]
        dk_sc[...] = jnp.zeros_like(dk_sc)
        dv_sc[...] = jnp.zeros_like(dv_sc)

        def step(i, masked):
            off = pl.multiple_of(i * T, T)
            qt = q_ref[0, pl.ds(off, T), :]
            dot_ = do_ref[0, pl.ds(off, T), :]
            st = _dot_nt(kt, qt)
            if masked:
                r = lax.broadcasted_iota(jnp.int32, (T, T), 0)
                c = lax.broadcasted_iota(jnp.int32, (T, T), 1)
                st = jnp.where(c >= r, st, NEG)
            pt = jnp.exp2(st - lse_ref[0, i])
            dv_sc[...] += _dot(pt.astype(bf16), dot_)
            dpt = _dot_nt(vt, dot_)
            dst = (pt * (dpt - dl_ref[0, i])).astype(bf16)
            dk_sc[...] += _dot(dst, qt)
            dq_ref[0, pl.ds(off, T), :] += _dot_tn(dst, kt)

        step(j, True)

        def loop_body(i, c):
            step(i, False)
            return c

        lax.fori_loop(j + 1, nq, loop_body, 0)
        dk_ref[0] = dk_sc[...] * LN2
        dv_ref[0] = dv_sc[...]

    any_spec = pl.BlockSpec(memory_space=pl.ANY)
    return pl.pallas_call(
        body, name="flash_bwd", grid=(H, nq),
        in_specs=[pl.BlockSpec((1, M, QK_PAD), lambda h, j: (h, 0, 0)),
                  pl.BlockSpec((1, M, V_DIM), lambda h, j: (h, 0, 0)),
                  pl.BlockSpec((1, nq, 1, T), lambda h, j: (h, 0, 0, 0)),
                  pl.BlockSpec((1, nq, 1, T), lambda h, j: (h, 0, 0, 0)),
                  pl.BlockSpec((1, T, QK_PAD), lambda h, j: (h, j, 0)),
                  pl.BlockSpec((1, T, V_DIM), lambda h, j: (h, j, 0))] + [any_spec] * nx,
        out_specs=[pl.BlockSpec((1, M, QK_PAD), lambda h, j: (h, 0, 0)),
                   pl.BlockSpec((1, T, QK_PAD), lambda h, j: (h, j, 0)),
                   pl.BlockSpec((1, T, V_DIM), lambda h, j: (h, j, 0))] + [any_spec] * nx,
        out_shape=[jax.ShapeDtypeStruct((H, M, QK_PAD), f32),
                   jax.ShapeDtypeStruct((H, M, QK_PAD), f32),
                   jax.ShapeDtypeStruct((H, M, V_DIM), f32)] + _exchange_shapes(carried, scatter),
        scratch_shapes=[pltpu.VMEM((T, QK_PAD), f32), pltpu.VMEM((T, V_DIM), f32)] + _exchange_sems(nx),
        compiler_params=_cparams(("arbitrary", "arbitrary")),
    )(q, do, lse, delta, k, v, *carried)


def _dt_fwd(dt_raw, bias, expand):
    M = dt_raw.shape[0]
    tm = _rt(M)

    def body(x_ref, b_ref, e_ref, o_ref, oe_ref):
        u = x_ref[...] + b_ref[...]
        sp = jnp.maximum(u, 0.0) + jnp.log(1.0 + jnp.exp(-jnp.abs(u)))
        lane = lax.broadcasted_iota(jnp.int32, u.shape, 1)
        dtp = jnp.where(lane < SSM_HEADS, sp, 0.0)
        o_ref[...] = dtp
        oe_ref[...] = _dot_hi(dtp, e_ref[...])

    return pl.pallas_call(
        body, name="dt_fwd", grid=(M // tm,),
        in_specs=[_row(LANES, tm), _full((1, LANES)), _full((LANES, D_SSM))],
        out_specs=[_row(LANES, tm), _row(D_SSM, tm)],
        out_shape=[jax.ShapeDtypeStruct((M, LANES), f32), jax.ShapeDtypeStruct((M, D_SSM), f32)],
        compiler_params=_cparams(("parallel",)),
    )(dt_raw, bias, expand)


def _dt_bwd(dt_raw, bias, ddt):
    M = dt_raw.shape[0]
    tm = _rt(M)

    def body(x_ref, b_ref, d_ref, o_ref, db_ref):
        @pl.when(pl.program_id(0) == 0)
        def _():
            db_ref[...] = jnp.zeros_like(db_ref)

        u = x_ref[...] + b_ref[...]
        lane = lax.broadcasted_iota(jnp.int32, u.shape, 1)
        g = jnp.where(lane < SSM_HEADS, d_ref[...] * _sigmoid(u), 0.0)
        o_ref[...] = g
        db_ref[...] += jnp.sum(g, axis=0, keepdims=True)

    return pl.pallas_call(
        body, name="dt_bwd", grid=(M // tm,),
        in_specs=[_row(LANES, tm), _full((1, LANES)), _row(LANES, tm)],
        out_specs=[_row(LANES, tm), _full((1, LANES))],
        out_shape=[jax.ShapeDtypeStruct((M, LANES), f32), jax.ShapeDtypeStruct((1, LANES), f32)],
        compiler_params=_cparams(("arbitrary",)),
    )(dt_raw, bias, ddt)


SSM_GW = SSM_HPG * SSM_P
SSM_PAIRS = SSM_GW // LANES


def _ssd_common(dte_ref, dtt_ref, ae_ref, acol_ref):
    Q = CHUNK
    r = lax.broadcasted_iota(jnp.int32, (Q, Q), 0)
    c = lax.broadcasted_iota(jnp.int32, (Q, Q), 1)
    causal = r >= c
    anti = c >= r
    tril = causal.astype(f32)
    triu = anti.astype(f32)
    dt_e = dte_ref[...]
    cs_e = _dot_hi(tril, dt_e * ae_ref[...])
    cst = _dot_hi(dtt_ref[...] * acol_ref[...], triu)
    cs_last = cs_e[Q - 1:Q, :]
    return causal, anti, triu, dt_e, cs_e, cst, jnp.exp(cs_e), jnp.exp(cs_last - cs_e), jnp.exp(cs_last)


def _half_masks():
    lane = lax.broadcasted_iota(jnp.int32, (CHUNK, LANES), 1)
    lo = lane < SSM_P
    return lo, jnp.logical_not(lo)


def _ssd_fwd(xbc_c, dt_e, dtt, a_e, a_col):
    M = xbc_c.shape[0]
    Q = CHUNK
    nch = M // Q

    def body(x_ref, dte_ref, dtt_ref, ae_ref, acol_ref, y_ref, hin_ref, ht_sc):
        @pl.when(pl.program_id(0) == 0)
        def _():
            ht_sc[...] = jnp.zeros_like(ht_sc)

        causal, _, _, dt_e, cs_e, cst, ecs_e, dte_e, elast_e = _ssd_common(dte_ref, dtt_ref, ae_ref, acol_ref)
        halves = _half_masks()
        for g in range(SSM_GROUPS):
            g0 = g * SSM_GW
            bg = x_ref[:, D_SSM + g * SSM_N:D_SSM + (g + 1) * SSM_N]
            cg = x_ref[:, D_SSM + D_BC + g * SSM_N:D_SSM + D_BC + (g + 1) * SSM_N]
            bg_b = bg.astype(bf16)
            cg_b = cg.astype(bf16)
            cb = _dot_nt(cg_b, bg_b)
            bgt_b = bg.T.astype(bf16)
            xdt_g = x_ref[:, g0:g0 + SSM_GW] * dt_e[:, g0:g0 + SSM_GW]
            ht = ht_sc[g]
            hin_ref[0, g] = ht
            y_off = _dot(cg_b, ht.astype(bf16)) * ecs_e[:, g0:g0 + SSM_GW]
            for pr in range(SSM_PAIRS):
                p0 = pr * LANES
                xdt_p = xdt_g[:, p0:p0 + LANES]
                acc = y_off[:, p0:p0 + LANES]
                for half in range(2):
                    h = g * SSM_HPG + pr * 2 + half
                    seg = cs_e[:, h * SSM_P:h * SSM_P + 1] - cst[h:h + 1, :]
                    lm = jnp.exp(jnp.where(causal, seg, -jnp.inf))
                    xm = jnp.where(halves[half], xdt_p, 0.0).astype(bf16)
                    acc = acc + _dot((cb * lm).astype(bf16), xm)
                y_ref[:, g0 + p0:g0 + p0 + LANES] = acc
            st = _dot(bgt_b, (xdt_g * dte_e[:, g0:g0 + SSM_GW]).astype(bf16))
            ht_sc[g] = ht * elast_e[:, g0:g0 + SSM_GW] + st

    return pl.pallas_call(
        body, name="ssd_fwd", grid=(nch,),
        in_specs=[pl.BlockSpec((Q, D_XBC), lambda c: (c, 0)),
                  pl.BlockSpec((Q, D_SSM), lambda c: (c, 0)),
                  pl.BlockSpec((SSM_HEADS, Q), lambda c: (0, c)),
                  _full((1, D_SSM)), _full((SSM_HEADS, LANES))],
        out_specs=[pl.BlockSpec((Q, D_SSM), lambda c: (c, 0)),
                   pl.BlockSpec((1, SSM_GROUPS, SSM_N, SSM_GW), lambda c: (c, 0, 0, 0))],
        out_shape=[jax.ShapeDtypeStruct((M, D_SSM), f32),
                   jax.ShapeDtypeStruct((nch, SSM_GROUPS, SSM_N, SSM_GW), f32)],
        scratch_shapes=[pltpu.VMEM((SSM_GROUPS, SSM_N, SSM_GW), f32)],
        compiler_params=_cparams(("arbitrary",)),
    )(xbc_c, dt_e, dtt, a_e, a_col)


def _ssd_bwd(xbc_c, dtp, dt_e, dtt, a_row, a_e, a_col, hin, dy, d_exp, head_ind):
    M = xbc_c.shape[0]
    Q = CHUNK
    nch = M // Q
    rev = lambda c: nch - 1 - c

    def body(x_ref, dtp_ref, dte_ref, dtt_ref, arow_ref, ae_ref, acol_ref, hin_ref, dy_ref, dexp_ref,
             ind_ref, dx_ref, ddt_ref, da_ref, dht_sc, z_sc, z1_sc, last_sc, ct_sc):
        @pl.when(pl.program_id(0) == 0)
        def _():
            dht_sc[...] = jnp.zeros_like(dht_sc)
            da_ref[...] = jnp.zeros_like(da_ref)
            last_sc[...] = jnp.zeros_like(last_sc)
            ct_sc[...] = jnp.zeros_like(ct_sc)

        causal, anti, triu, dt_e, cs_e, cst, ecs_e, dte_e, elast_e = _ssd_common(dte_ref, dtt_ref, ae_ref, acol_ref)
        halves = _half_masks()
        lane = lax.broadcasted_iota(jnp.int32, (Q, LANES), 1)
        rsum = jnp.zeros((Q, LANES), f32)
        for g in range(SSM_GROUPS):
            g0 = g * SSM_GW
            gs = slice(g0, g0 + SSM_GW)
            b0 = D_SSM + g * SSM_N
            c0 = D_SSM + D_BC + g * SSM_N
            bg = x_ref[:, b0:b0 + SSM_N]
            cg = x_ref[:, c0:c0 + SSM_N]
            bg_b = bg.astype(bf16)
            cg_b = cg.astype(bf16)
            cgt_b = cg.T.astype(bf16)
            cbt = _dot_nt(bg_b, cg_b)
            cb = _dot_nt(cg_b, bg_b)
            x_g = x_ref[:, gs]
            dt_g = dt_e[:, gs]
            xdt_g = x_g * dt_g
            dy_g = dy_ref[:, gs]
            ht = hin_ref[0, g]
            ht_b = ht.astype(bf16)
            dht = dht_sc[g]
            dht_b = dht.astype(bf16)
            dye_b = (dy_g * ecs_e[:, gs]).astype(bf16)
            dc = _dot_nt(dye_b, ht_b)
            dht_new = dht * elast_e[:, gs] + _dot(cgt_b, dye_b)
            e = _dot(bg_b, dht_b)
            xdtd = xdt_g * dte_e[:, gs]
            db = _dot_nt(xdtd.astype(bf16), dht_b)
            dxdt_state = e * dte_e[:, gs]
            exd = e * xdtd
            z1_sc[:, gs] = dy_g * (_dot(cg_b, ht_b) * ecs_e[:, gs]) - exd
            last_sc[0:1, gs] = (jnp.sum(exd, axis=0, keepdims=True)
                                + jnp.sum(dht * ht, axis=0, keepdims=True) * elast_e[:, gs])
            dg_acc = jnp.zeros((Q, Q), f32)
            for pr in range(SSM_PAIRS):
                p0 = pr * LANES
                ps = slice(g0 + p0, g0 + p0 + LANES)
                dy_p = dy_g[:, p0:p0 + LANES]
                xdt_pb = xdt_g[:, p0:p0 + LANES].astype(bf16)
                acc = dxdt_state[:, p0:p0 + LANES]
                for half in range(2):
                    h = g * SSM_HPG + pr * 2 + half
                    seg = cs_e[:, h * SSM_P:h * SSM_P + 1] - cst[h:h + 1, :]
                    lm = jnp.exp(jnp.where(causal, seg, -jnp.inf))
                    lmt = jnp.exp(jnp.where(anti, -seg, -jnp.inf))
                    dym = jnp.where(halves[half], dy_p, 0.0).astype(bf16)
                    acc = acc + _dot((cbt * lmt).astype(bf16), dym)
                    dml = _dot_nt(dym, xdt_pb) * lm
                    dg_acc = dg_acc + dml
                    w = dml * cb
                    rsum = rsum + jnp.where(lane == h, jnp.sum(w, axis=1, keepdims=True), 0.0)
                    ct_sc[h:h + 1, :] = jnp.sum(w, axis=0, keepdims=True)
                dx_ref[:, ps] = acc * dt_g[:, p0:p0 + LANES] + dexp_ref[:, ps] * dy_p
                z_sc[:, ps] = acc * x_g[:, p0:p0 + LANES]
            dg_b = dg_acc.astype(bf16)
            dx_ref[:, c0:c0 + SSM_N] = dc + _dot(dg_b, bg_b)
            dx_ref[:, b0:b0 + SSM_N] = db + _dot_tn(dg_b, cg_b)
            dht_sc[g] = dht_new
        s1 = _dot_hi(z1_sc[...], ind_ref[...])
        s2 = _dot_hi(z_sc[...], ind_ref[...])
        last = _dot_hi(last_sc[...], ind_ref[...])[0:1, :]
        dtp = dtp_ref[...]
        row = lax.broadcasted_iota(jnp.int32, (Q, LANES), 0)
        dcs = s1 + rsum + jnp.where(row == Q - 1, last, 0.0)
        tril = causal.astype(f32)
        da = _dot_hi(triu, dcs) - _dot_hi(ct_sc[...], tril).T
        ddt_ref[...] = s2 + da * arow_ref[...]
        da_ref[...] += jnp.sum(da * dtp, axis=0, keepdims=True)

    return pl.pallas_call(
        body, name="ssd_bwd", grid=(nch,),
        in_specs=[pl.BlockSpec((Q, D_XBC), lambda c: (rev(c), 0)),
                  pl.BlockSpec((Q, LANES), lambda c: (rev(c), 0)),
                  pl.BlockSpec((Q, D_SSM), lambda c: (rev(c), 0)),
                  pl.BlockSpec((SSM_HEADS, Q), lambda c: (0, rev(c))),
                  _full((1, LANES)), _full((1, D_SSM)), _full((SSM_HEADS, LANES)),
                  pl.BlockSpec((1, SSM_GROUPS, SSM_N, SSM_GW), lambda c: (rev(c), 0, 0, 0)),
                  pl.BlockSpec((Q, D_SSM), lambda c: (rev(c), 0)),
                  _full((1, D_SSM)), _full((D_SSM, LANES))],
        out_specs=[pl.BlockSpec((Q, D_XBC), lambda c: (rev(c), 0)),
                   pl.BlockSpec((Q, LANES), lambda c: (rev(c), 0)),
                   _full((1, LANES))],
        out_shape=[jax.ShapeDtypeStruct((M, D_XBC), f32), jax.ShapeDtypeStruct((M, LANES), f32),
                   jax.ShapeDtypeStruct((1, LANES), f32)],
        scratch_shapes=[pltpu.VMEM((SSM_GROUPS, SSM_N, SSM_GW), f32), pltpu.VMEM((Q, D_SSM), f32),
                        pltpu.VMEM((Q, D_SSM), f32), pltpu.VMEM((SUBLANES, D_SSM), f32),
                        pltpu.VMEM((LANES, Q), f32)],
        compiler_params=_cparams(("arbitrary",)),
    )(xbc_c, dtp, dt_e, dtt, a_row, a_e, a_col, hin, dy, d_exp, head_ind)


def _gate_norm_fwd(y, xbc_c, z, d_exp, g):
    M = y.shape[0]
    tm = _rt(M)
    gw = D_SSM // SSM_GROUPS

    def body(y_ref, x_ref, z_ref, d_ref, g_ref, o_ref):
        yg = (y_ref[...] + d_ref[...] * x_ref[...]) * _silu(z_ref[...])
        for gi in range(SSM_GROUPS):
            blk = yg[:, gi * gw:(gi + 1) * gw]
            o_ref[:, gi * gw:(gi + 1) * gw] = (blk * _rstd(blk) * g_ref[:, gi * gw:(gi + 1) * gw]).astype(bf16)

    return pl.pallas_call(
        body, name="gate_norm_fwd", grid=(M // tm,),
        in_specs=[_row(D_SSM, tm), _row(D_SSM, tm), _row(D_SSM, tm), _full((1, D_SSM)), _full((1, D_SSM))],
        out_specs=_row(D_SSM, tm), out_shape=jax.ShapeDtypeStruct((M, D_SSM), bf16),
        compiler_params=_cparams(("parallel",)),
    )(y, xbc_c, z, d_exp, g)


def _gate_norm_bwd(y, xbc_c, z, d_exp, g, dout, head_ind):
    M = y.shape[0]
    tm = _rt(M)
    nt = M // tm
    gw = D_SSM // SSM_GROUPS

    def body(y_ref, x_ref, z_ref, d_ref, g_ref, do_ref, ind_ref, dy_ref, dz_ref, dg_ref, dd_ref, ddc_sc):
        i = pl.program_id(0)

        @pl.when(i == 0)
        def _():
            dg_ref[...] = jnp.zeros_like(dg_ref)
            ddc_sc[...] = jnp.zeros_like(ddc_sc)

        zv = z_ref[...]
        xv = x_ref[...]
        s = _silu(zv)
        yd = y_ref[...] + d_ref[...] * xv
        yg = yd * s
        dov = do_ref[...]
        for gi in range(SSM_GROUPS):
            sl = slice(gi * gw, (gi + 1) * gw)
            dyg, dgp = _rms_bwd_math(yg[:, sl], g_ref[:, sl], dov[:, sl])
            dg_ref[:, sl] += jnp.sum(dgp, axis=0, keepdims=True)
            dyd = dyg * s[:, sl]
            dy_ref[:, sl] = dyd
            dz_ref[:, sl] = (dyg * yd[:, sl] * _dsilu(zv[:, sl])).astype(bf16)
            ddc_sc[:, sl] += jnp.sum(dyd * xv[:, sl], axis=0, keepdims=True)

        @pl.when(i == nt - 1)
        def _():
            dd_ref[...] = _dot_hi(ddc_sc[...], ind_ref[...])

    return pl.pallas_call(
        body, name="gate_norm_bwd", grid=(nt,),
        in_specs=[_row(D_SSM, tm), _row(D_SSM, tm), _row(D_SSM, tm), _full((1, D_SSM)), _full((1, D_SSM)),
                  _row(D_SSM, tm), _full((D_SSM, LANES))],
        out_specs=[_row(D_SSM, tm), _row(D_SSM, tm), _full((1, D_SSM)), _full((1, LANES))],
        out_shape=[jax.ShapeDtypeStruct((M, D_SSM), f32), jax.ShapeDtypeStruct((M, D_SSM), bf16),
                   jax.ShapeDtypeStruct((1, D_SSM), f32), jax.ShapeDtypeStruct((1, LANES), f32)],
        scratch_shapes=[pltpu.VMEM((1, D_SSM), f32)],
        compiler_params=_cparams(("arbitrary",)),
    )(y, xbc_c, z, d_exp, g, dout, head_ind)


_PEER_FLIPS = [(0, 0, 1), (0, 1, 0), (0, 1, 1), (1, 0, 0), (1, 0, 1), (1, 1, 0), (1, 1, 1)]


def _exchange_copies(ins, outs, send_sems, recv_sems, loc_sems, scatter):
    n = len(ins)
    x, y, c = lax.axis_index("x"), lax.axis_index("y"), lax.axis_index("c")
    me = 4 * x + 2 * y + c
    copies = []
    for a in range(n):
        src = ins[a].at[me] if scatter else ins[a]
        copies.append(pltpu.make_async_copy(src, outs[a].at[me], loc_sems.at[a]))
    for p, (fx, fy, fc) in enumerate(_PEER_FLIPS):
        tx = 1 - x if fx else x
        ty = 1 - y if fy else y
        tc = 1 - c if fc else c
        tgt = 4 * tx + 2 * ty + tc
        for a in range(n):
            src = ins[a].at[tgt] if scatter else ins[a]
            copies.append(pltpu.make_async_remote_copy(
                src_ref=src, dst_ref=outs[a].at[me],
                send_sem=send_sems.at[p * n + a], recv_sem=recv_sems.at[p * n + a],
                device_id=(tx, ty, tc), device_id_type=_MESH))
    return copies


def _exchange_shapes(arrays, scatter):
    return [jax.ShapeDtypeStruct(a.shape if scatter else (N_DEV,) + a.shape, a.dtype) for a in arrays]


def _exchange_sems(n):
    return [pltpu.SemaphoreType.DMA((7 * n,)), pltpu.SemaphoreType.DMA((7 * n,)), pltpu.SemaphoreType.DMA((n,))]


def _gather_two_level(arrays, name):
    n = len(arrays)

    def body(*refs):
        ins, outs = refs[:n], refs[n:2 * n]
        send_sems, recv_sems, loc_sems = refs[2 * n:]
        x, y, c = lax.axis_index("x"), lax.axis_index("y"), lax.axis_index("c")
        me, sibling = (x, y, c), (x, y, 1 - c)
        chips = [(1 - x, y), (x, 1 - y), (1 - x, 1 - y)]

        def slot(a, dev):
            return outs[a].at[4 * dev[0] + 2 * dev[1] + dev[2]]

        def copy(a, k, block, to, src=None):
            return pltpu.make_async_remote_copy(
                src_ref=slot(a, block) if src is None else src, dst_ref=slot(a, block),
                send_sem=send_sems.at[7 * a + k], recv_sem=recv_sems.at[7 * a + k],
                device_id=to, device_id_type=_MESH)

        mine = [pltpu.make_async_copy(ins[a], slot(a, me), loc_sems.at[a]) for a in range(n)]
        first = []
        for a in range(n):
            first.append(copy(a, 0, me, sibling, src=ins[a]))
            first += [copy(a, 1 + j, me, (*chip, c), src=ins[a]) for j, chip in enumerate(chips)]
        for cp in mine + first:
            cp.start()
        passed = []
        for j, chip in enumerate(chips):
            for a in range(n):
                copy(a, 1 + j, (*chip, c), me).wait_recv()
                cp = copy(a, 4 + j, (*chip, c), sibling)
                cp.start()
                passed.append(cp)
        for a in range(n):
            copy(a, 0, sibling, me).wait_recv()
            for j, chip in enumerate(chips):
                copy(a, 4 + j, (*chip, 1 - c), me).wait_recv()
        for cp in first + passed:
            cp.wait_send()
        for cp in mine:
            cp.wait()

    any_spec = pl.BlockSpec(memory_space=pl.ANY)
    return pl.pallas_call(
        body, name=name, in_specs=[any_spec] * n, out_specs=[any_spec] * n,
        out_shape=_exchange_shapes(arrays, False), scratch_shapes=_exchange_sems(n),
    )(*arrays)


def _exchange_tail(scattered, gathered, name):
    ns, ng = len(scattered), len(gathered)
    n = ns + ng

    def body(*refs):
        sems = refs[2 * n:]
        copies = (_exchange_copies(refs[:ns], refs[n:n + ns], *sems[:3], True)
                  + _exchange_copies(refs[ns:n], refs[n + ns:2 * n], *sems[3:], False))
        for cp in copies:
            cp.start()
        for cp in copies:
            cp.wait()

    any_spec = pl.BlockSpec(memory_space=pl.ANY)
    return pl.pallas_call(
        body, name=name, in_specs=[any_spec] * n, out_specs=[any_spec] * n,
        out_shape=_exchange_shapes(scattered, True) + _exchange_shapes(gathered, False),
        scratch_shapes=_exchange_sems(ns) + _exchange_sems(ng),
    )(*scattered, *gathered)


def _adamw_math(g, w, m, v):
    c1 = 1.0 - ADAM_B1 ** ADAM_STEP
    c2 = 1.0 - ADAM_B2 ** ADAM_STEP
    mn = ADAM_B1 * m + (1.0 - ADAM_B1) * g
    vn = ADAM_B2 * v + (1.0 - ADAM_B2) * (g * g)
    m_hat = mn / c1
    v_hat = vn / c2
    return -ADAM_LR * (m_hat / (jnp.sqrt(v_hat) + ADAM_EPS) + ADAM_WD * w), mn, vn


def _adamw(parts, w, m, v, name):
    R, C = w.shape
    tr = _pick(R, (PACK_ROW_TILE, 64, 32, 16, 8))

    def body(p_ref, w_ref, m_ref, v_ref, g_ref, d_ref, nm_ref, nv_ref):
        g = p_ref[0].astype(f32)
        for s in range(1, N_DEV):
            g = g + p_ref[s].astype(f32)
        g_ref[...] = g
        d_ref[...], nm_ref[...], nv_ref[...] = _adamw_math(g, w_ref[...], m_ref[...], v_ref[...])

    spec = pl.BlockSpec((tr, C), lambda i: (i, 0))
    return pl.pallas_call(
        body, name=name, grid=(R // tr,),
        in_specs=[pl.BlockSpec((N_DEV, tr, C), lambda i: (0, i, 0)), spec, spec, spec],
        out_specs=[spec] * 4, out_shape=[jax.ShapeDtypeStruct((R, C), f32)] * 4,
        compiler_params=_cparams(("parallel",)),
    )(parts, w, m, v)


def _adamw_replicated(parts, ws, ms, vs):
    n = len(ws)
    R = parts.shape[1]
    sizes = [int(w.shape[1]) for w in ws]

    def body(*refs):
        p_ref = refs[0]
        w_refs, m_refs, v_refs = refs[1:1 + n], refs[1 + n:1 + 2 * n], refs[1 + 2 * n:1 + 3 * n]
        loss_ref = refs[1 + 3 * n]
        outs = refs[2 + 3 * n:]
        g_all = p_ref[0]
        for s in range(1, N_DEV):
            g_all = g_all + p_ref[s]
        row = 0
        for p in range(n):
            pieces, left = [], sizes[p]
            while left > 0:
                take = min(left, PACK_W)
                pieces.append(g_all[row:row + 1, 0:take])
                left -= take
                row += 1
            g = pieces[0] if len(pieces) == 1 else jnp.concatenate(pieces, axis=1)
            d, mn, vn = _adamw_math(g, w_refs[p][...], m_refs[p][...], v_refs[p][...])
            outs[4 * p][...] = g
            outs[4 * p + 1][...] = d
            outs[4 * p + 2][...] = mn
            outs[4 * p + 3][...] = vn
        loss_ref[...] = g_all[row:row + 1, 0:LANES]

    in_specs = [_full((N_DEV, R, PACK_W))] + [_full((1, s)) for s in sizes] * 3
    out_specs = [_full((1, LANES))]
    out_shape = [jax.ShapeDtypeStruct((1, LANES), f32)]
    for s in sizes:
        out_specs += [_full((1, s))] * 4
        out_shape += [jax.ShapeDtypeStruct((1, s), f32)] * 4
    res = pl.pallas_call(
        body, name="adamw_replicated", in_specs=in_specs, out_specs=out_specs, out_shape=out_shape,
        compiler_params=pltpu.CompilerParams(vmem_limit_bytes=VMEM_LIMIT),
    )(parts, *ws, *ms, *vs)
    return res[0], [res[1 + 4 * p:5 + 4 * p] for p in range(n)]


def _flat_rows(a, lead_ndim):
    lead = a.shape[:lead_ndim]
    n = int(np.prod(a.shape[lead_ndim:]))
    a = a.reshape(lead + (n,))
    pad = (-n) % PACK_W
    if pad:
        a = jnp.pad(a, [(0, 0)] * lead_ndim + [(0, pad)])
    return a.reshape(lead + ((n + pad) // PACK_W, PACK_W))


def _pack(arrays, lead_ndim, total_rows, dtype):
    rows = [_flat_rows(a.astype(dtype), lead_ndim) for a in arrays]
    cat = jnp.concatenate(rows, axis=lead_ndim)
    pad = total_rows - cat.shape[lead_ndim]
    if pad:
        cat = jnp.pad(cat, [(0, 0)] * lead_ndim + [(0, pad), (0, 0)])
    return cat


def _unpack(buf, shapes, lead_ndim):
    out = []
    r = 0
    lead = buf.shape[:lead_ndim]
    for shp in shapes:
        n = int(np.prod(shp))
        nr = -(-n // PACK_W)
        piece = lax.slice_in_dim(buf, r, r + nr, axis=lead_ndim)
        piece = piece.reshape(lead + (nr * PACK_W,))
        piece = lax.slice_in_dim(piece, 0, n, axis=lead_ndim)
        out.append(piece.reshape(lead + tuple(shp)))
        r += nr
    return out


def _round_up(n, m):
    return -(-n // m) * m


def kernel(x, meta_tokens, norm_mix_pre, norm_mix_post, norm_ffn_pre, norm_ffn_post, w_in, q_a_norm, w_uq, kv_a_norm, w_ukv, attn_out_norm, ssm_conv_w, ssm_conv_b, ssm_dt_bias, ssm_A_log, ssm_D, ssm_norm, w_out, w_up, ffn_conv_w, ffn_conv_b, w_down, loss_target, m_meta_tokens, m_norm_mix_pre, m_norm_mix_post, m_norm_ffn_pre, m_norm_ffn_post, m_w_in, m_q_a_norm, m_w_uq, m_kv_a_norm, m_w_ukv, m_attn_out_norm, m_ssm_conv_w, m_ssm_conv_b, m_ssm_dt_bias, m_ssm_A_log, m_ssm_D, m_ssm_norm, m_w_out, m_w_up, m_ffn_conv_w, m_ffn_conv_b, m_w_down, v_meta_tokens, v_norm_mix_pre, v_norm_mix_post, v_norm_ffn_pre, v_norm_ffn_post, v_w_in, v_q_a_norm, v_w_uq, v_kv_a_norm, v_w_ukv, v_attn_out_norm, v_ssm_conv_w, v_ssm_conv_b, v_ssm_dt_bias, v_ssm_A_log, v_ssm_D, v_ssm_norm, v_w_out, v_w_up, v_ffn_conv_w, v_ffn_conv_b, v_w_down):
    seq = x.shape[1]
    n_real = N_META + seq
    Lp = _round_up(n_real, 768) if n_real > 2048 else _round_up(n_real, ROW_TILE)
    D = D_MODEL

    early_w = [w_uq, w_ukv]
    late_w = [w_out, w_down]
    sharded_s = [meta_tokens, ssm_conv_w, ffn_conv_w]
    grp_a = dict(names=["w_out", "w_down"], w=late_w, m=[m_w_out, m_w_down],
                 v=[v_w_out, v_w_down])
    grp_b = dict(names=["w_uq", "w_ukv", "ssm_conv_w", "ffn_conv_w"],
                 w=early_w + [ssm_conv_w, ffn_conv_w],
                 m=[m_w_uq, m_w_ukv, m_ssm_conv_w, m_ffn_conv_w],
                 v=[v_w_uq, v_w_ukv, v_ssm_conv_w, v_ffn_conv_w])
    grp_meta = dict(names=["meta_tokens"], w=[meta_tokens], m=[m_meta_tokens], v=[v_meta_tokens])
    repl_w = [norm_mix_pre, norm_mix_post, norm_ffn_pre, norm_ffn_post, q_a_norm, kv_a_norm, attn_out_norm,
              ssm_conv_b, ssm_dt_bias, ssm_A_log, ssm_D, ssm_norm, ffn_conv_b]
    repl_m = [m_norm_mix_pre, m_norm_mix_post, m_norm_ffn_pre, m_norm_ffn_post, m_q_a_norm, m_kv_a_norm,
              m_attn_out_norm, m_ssm_conv_b, m_ssm_dt_bias, m_ssm_A_log, m_ssm_D, m_ssm_norm, m_ffn_conv_b]
    repl_v = [v_norm_mix_pre, v_norm_mix_post, v_norm_ffn_pre, v_norm_ffn_post, v_q_a_norm, v_kv_a_norm,
              v_attn_out_norm, v_ssm_conv_b, v_ssm_dt_bias, v_ssm_A_log, v_ssm_D, v_ssm_norm, v_ffn_conv_b]

    def pack_rows(arrs, lead):
        return _round_up(sum(-(-int(np.prod(a.shape[lead:])) // PACK_W) for a in arrs), 16)

    wb = _pack(early_w, 0, pack_rows(early_w, 0), bf16)
    wl = _pack(late_w, 0, pack_rows(late_w, 0), bf16)
    ws = _pack(sharded_s, 0, pack_rows(sharded_s, 0), f32)
    wb_all, ws_all, win_all = _gather_two_level([wb, ws, w_in[0].astype(bf16)], "gather_weights")
    g_w_uq, g_w_ukv = _unpack(wb_all, [a.shape for a in early_w], 1)
    g_meta, g_sconv, g_fconv = _unpack(ws_all, [a.shape for a in sharded_s], 1)

    def cols(gathered):
        t = gathered[:, 0]
        return jnp.transpose(t, (1, 0, 2)).reshape(t.shape[1], N_DEV * t.shape[2])

    win = cols(win_all[:, None])
    o = np.cumsum((0, Q_RANK, KV_RANK, QK_ROPE, D_SSM, D_XBC, SSM_HEADS))
    w_q, w_kv = win[:, o[0]:o[1]], win[:, o[1]:o[2]]
    w_rope = jnp.pad(win[:, o[2]:o[3]], ((0, 0), (0, LANES - QK_ROPE)))
    w_z, w_xbc = win[:, o[3]:o[4]], win[:, o[4]:o[5]]
    w_dt = jnp.pad(win[:, o[5]:o[6]], ((0, 0), (0, LANES - SSM_HEADS)))
    wuq = g_w_uq.reshape(Q_RANK, MLA_HEADS, QK_NOPE + QK_ROPE)
    wuq = jnp.pad(wuq, ((0, 0), (0, 0), (0, QK_PAD - QK_NOPE - QK_ROPE))).reshape(Q_RANK, MLA_HEADS * QK_PAD)
    wukv = g_w_ukv.reshape(KV_RANK, MLA_HEADS * (QK_NOPE + V_DIM))
    meta_full = jnp.transpose(g_meta, (1, 0, 2)).reshape(N_META, D)
    sconv_w = jnp.pad(cols(g_sconv), ((0, SUBLANES - SSM_CONV), (0, 0)))
    fconv_w = jnp.pad(cols(g_fconv), ((0, SUBLANES - FFN_CONV), (0, 0)))

    pos = jnp.arange(Lp, dtype=f32)
    inv = ROPE_THETA ** (-jnp.arange(0, QK_ROPE, 2, dtype=f32) / QK_ROPE)
    ang = pos[:, None] * inv[None, :]
    cs_, sn_ = jnp.cos(ang), jnp.sin(ang)
    zpad = jnp.zeros((Lp, LANES - QK_ROPE), f32)
    cos_t = jnp.concatenate([cs_, cs_, zpad], axis=1)
    sin_t = jnp.concatenate([-sn_, sn_, zpad], axis=1)
    dt_bias_p = jnp.pad(ssm_dt_bias, ((0, 0), (0, LANES - SSM_HEADS)))
    a_neg = -jnp.exp(ssm_A_log)
    a_row = jnp.pad(a_neg, ((0, 0), (0, LANES - SSM_HEADS)))
    a_col = jnp.broadcast_to(a_neg.reshape(SSM_HEADS, 1), (SSM_HEADS, LANES))
    d_exp = jnp.repeat(ssm_D, SSM_P, axis=1)
    a_e = jnp.repeat(a_neg, SSM_P, axis=1)
    head_ind = (jnp.arange(D_SSM)[:, None] // SSM_P == jnp.arange(LANES)[None, :]).astype(f32)

    xb = x[0]
    h0 = jnp.concatenate([meta_full, xb, jnp.zeros((Lp - n_real, D), f32)], axis=0)
    tgt = jnp.pad(loss_target[0], ((N_META, Lp - n_real), (0, 0)))
    hn1, q_c, kv_c, kpe_raw, z, xbc, dt_raw = _in_proj(h0, norm_mix_pre, [w_q, w_kv, w_rope, w_z, w_xbc, w_dt])

    qn, qh = _up_q_rope(q_c, q_a_norm, wuq, cos_t, sin_t)
    kvn, kh, vh = _up_kv_rope(kv_c, kv_a_norm, wukv, kpe_raw, cos_t, sin_t)
    attn, lse, wl_all, wup_all = _flash_fwd(qh, kh, vh, [wl, w_up[0].astype(bf16)], False)
    g_w_out, g_w_down = _unpack(wl_all, [a.shape for a in late_w], 1)
    wout = g_w_out.reshape(D_ATTN + D_SSM, D)
    wout_a, wout_s = wout[:D_ATTN], wout[D_ATTN:]
    wup = cols(wup_all[:, None])
    wdown = g_w_down.reshape(D_FF, D)
    an = _rms_fwd(attn, attn_out_norm, bf16, "norm_attn_out")

    xbc_c = _ssm_conv_fwd(xbc, sconv_w, ssm_conv_b)
    dtp, dt_e = _dt_fwd(dt_raw, dt_bias_p, jnp.transpose(head_ind))
    dtt = jnp.transpose(dtp[:, :SSM_HEADS])
    y_ssd, hin = _ssd_fwd(xbc_c, dt_e, dtt, a_e, a_col)
    ssm = _gate_norm_fwd(y_ssd, xbc_c, z, d_exp, ssm_norm)

    mix, h1, hn2 = _out_proj_resid(an, ssm, wout_a, wout_s, h0, norm_mix_post, norm_ffn_pre)
    up = _mm(hn2, wup, False, "ffn_up")
    act = _ffn_gate_fwd(up, fconv_w, ffn_conv_b)
    dh2, d_down, dg_ffn_post, loss_part = _final(h1, act, wdown, norm_ffn_post, tgt, n_real)

    dw_down = _mm_tn(act, d_down, "ffn_down_dw")
    dup_g, dup_v, dwc_g, dwc_v, dbc_g, dbc_v = _ffn_gate_bwd(up, fconv_w, ffn_conv_b, d_down, wdown)
    dw_up = jnp.concatenate([_mm_tn(hn2, dup_g, "ffn_up_dw_g"), _mm_tn(hn2, dup_v, "ffn_up_dw_v")], axis=1)
    dh1, d_mix, dg_ffn_pre, dg_mix_post = _mid_bwd(h1, norm_ffn_pre, dup_g, dup_v, wup, dh2, mix, norm_mix_post)
    d_an = _mm(d_mix, wout_a, True, "out_proj_dx_a")
    d_ssm = _mm(d_mix, wout_s, True, "out_proj_dx_s")
    dw_out = jnp.concatenate([_mm_tn(an, d_mix, "out_proj_dw_a"), _mm_tn(ssm, d_mix, "out_proj_dw_s")], axis=0)

    do_h, delta, dg_attn_out = _attn_out_bwd(attn, attn_out_norm, d_an)
    def col_blocks(gm):
        r, cc = gm.shape
        return jnp.transpose(gm.reshape(r, N_DEV, cc // N_DEV), (1, 0, 2))

    blocks_a = [dw_out.reshape(N_DEV, (D_ATTN + D_SSM) // N_DEV, D), dw_down.reshape(N_DEV, D_FF // N_DEV, D)]
    gpack_a = _pack(blocks_a, 1, pack_rows(blocks_a, 1), bf16)
    dqh, dkh, dvh, gparts_a, gparts_up = _flash_bwd(qh, kh, vh, do_h, lse, delta,
                                                    [gpack_a, col_blocks(dw_up).astype(bf16)], True)
    d_q_c, dg_q, dw_uq = _q_branch_bwd(dqh, cos_t, sin_t, wuq, q_c, q_a_norm, qn)
    d_kv_c, dg_kv, dw_ukv, d_kpe_raw = _kv_branch_bwd(dkh, dvh, cos_t, sin_t, wukv, kv_c, kv_a_norm, kvn)

    dy_ssd, dz, dg_ssm, dd_heads = _gate_norm_bwd(y_ssd, xbc_c, z, d_exp, ssm_norm, d_ssm, head_ind)
    d_xbc_c, ddt, da_heads = _ssd_bwd(xbc_c, dtp, dt_e, dtt, a_row, a_e, a_col, hin, dy_ssd, d_exp, head_ind)
    d_xbc, dw_sconv, db_sconv = _ssm_conv_bwd(xbc, sconv_w, ssm_conv_b, d_xbc_c)
    d_dt_raw, d_dt_bias = _dt_bwd(dt_raw, dt_bias_p, ddt)

    dw_q, dw_kv, dw_rope, dw_z, dw_xbc, dw_dt = _in_proj_dw(hn1, [d_q_c, d_kv_c, d_kpe_raw, dz, d_xbc, d_dt_raw])
    dw_in = jnp.concatenate([dw_q, dw_kv, dw_rope[:, :QK_ROPE], dw_z, dw_xbc, dw_dt[:, :SSM_HEADS]], axis=1)
    dw_uq3 = dw_uq.reshape(Q_RANK, MLA_HEADS, QK_PAD)[:, :, :QK_NOPE + QK_ROPE]
    blocks_b = [
        dw_uq3.reshape(N_DEV, Q_RANK // N_DEV, MLA_HEADS, QK_NOPE + QK_ROPE),
        dw_ukv.reshape(N_DEV, KV_RANK // N_DEV, MLA_HEADS, QK_NOPE + V_DIM),
        col_blocks(dw_sconv[:SSM_CONV]),
        col_blocks(jnp.concatenate([dwc_g, dwc_v], axis=1)[:FFN_CONV]),
    ]
    gpack_b = _pack(blocks_b, 1, pack_rows(blocks_b, 1), bf16)
    dh0, dg_mix_pre, gparts_b, gparts_in = _in_proj_dx(
        [d_q_c, d_kv_c, d_kpe_raw, dz, d_xbc, d_dt_raw], [w_q, w_kv, w_rope, w_z, w_xbc, w_dt],
        h0, norm_mix_pre, dh1, [gpack_b, col_blocks(dw_in).astype(bf16)], True)

    grad_x = dh0[N_META:n_real][None]
    meta_blocks = col_blocks(dh0[:N_META]).reshape(N_DEV, N_META * D // N_DEV // PACK_W, PACK_W)


    def adam_group(parts, grp, name):
        rows = parts.shape[1]
        packs = [_pack([a[None] for a in grp[k]], 1, rows, f32)[0] for k in ("w", "m", "v")]
        outs = _adamw(parts, *packs, name)
        shapes = [a.shape for a in grp["w"]]
        return [dict(zip(grp["names"], [t[0] for t in _unpack(b[None], shapes, 1)])) for b in outs]

    def adam_own_layout(parts, name, w, m, v):
        return [{name: t[None]} for t in _adamw(parts, w[0], m[0], v[0], "adamw_" + name)]

    sh_a = adam_group(gparts_a, grp_a, "adamw_sharded_a")
    sh_b = adam_group(gparts_b, grp_b, "adamw_sharded_b")
    sh_in = adam_own_layout(gparts_in, "w_in", w_in, m_w_in, v_w_in)
    sh_up = adam_own_layout(gparts_up, "w_up", w_up, m_w_up, v_w_up)

    dg_alog = da_heads[:, :SSM_HEADS] * a_neg
    repl_g = [dg_mix_pre, dg_mix_post, dg_ffn_pre, dg_ffn_post, dg_q, dg_kv, dg_attn_out, db_sconv,
              d_dt_bias[:, :SSM_HEADS], dg_alog, dd_heads[:, :SSM_HEADS], dg_ssm,
              jnp.concatenate([dbc_g, dbc_v], axis=1)]
    loss_vec = loss_part[:, :1]
    small_total = _round_up(sum(-(-int(np.prod(a.shape)) // PACK_W) for a in repl_g) + 1, 16)
    spack = _pack(repl_g + [loss_vec], 0, small_total, f32)
    gparts_meta, sparts = _exchange_tail([meta_blocks], [spack], "exchange_tail")
    sh_meta = adam_group(gparts_meta, grp_meta, "adamw_meta")
    loss_row, repl_out = _adamw_replicated(sparts, repl_w, repl_m, repl_v)
    loss = loss_row[0, 0]

    order = ["meta_tokens", "norm_mix_pre", "norm_mix_post", "norm_ffn_pre", "norm_ffn_post", "w_in", "q_a_norm",
             "w_uq", "kv_a_norm", "w_ukv", "attn_out_norm", "ssm_conv_w", "ssm_conv_b", "ssm_dt_bias", "ssm_A_log",
             "ssm_D", "ssm_norm", "w_out", "w_up", "ffn_conv_w", "ffn_conv_b", "w_down"]
    rp_names = ["norm_mix_pre", "norm_mix_post", "norm_ffn_pre", "norm_ffn_post", "q_a_norm", "kv_a_norm",
                "attn_out_norm", "ssm_conv_b", "ssm_dt_bias", "ssm_A_log", "ssm_D", "ssm_norm", "ffn_conv_b"]

    def lookup(k):
        d = {**sh_a[k], **sh_b[k], **sh_in[k], **sh_up[k], **sh_meta[k],
             **{n: four[k] for n, four in zip(rp_names, repl_out)}}
        return [d[n] for n in order]

    return (loss, grad_x, *lookup(0), *lookup(1), *lookup(2), *lookup(3))
```

```python
import math

import jax
import jax.numpy as jnp
import numpy as np
from jax import lax
from jax.experimental import pallas as pl
from jax.experimental.pallas import tpu as pltpu

f32 = jnp.float32
bf16 = jnp.bfloat16

D_MODEL = 1024
N_META = 16
MLA_HEADS = 8
QK_NOPE = 128
QK_ROPE = 64
V_DIM = 128
Q_RANK = 384
KV_RANK = 256
ROPE_THETA = 10000.0
SOFTMAX_SCALE = (QK_NOPE + QK_ROPE) ** -0.5
D_ATTN = MLA_HEADS * V_DIM
SSM_HEADS = 16
SSM_P = 64
SSM_GROUPS = 2
SSM_HPG = SSM_HEADS // SSM_GROUPS
SSM_N = 128
SSM_CONV = 4
CHUNK = 128
D_SSM = SSM_HEADS * SSM_P
D_BC = SSM_GROUPS * SSM_N
D_XBC = D_SSM + 2 * D_BC
D_FF = 2816
FFN_CONV = 3
EPS = 1e-6
QK_PAD = 256
N_DEV = 8

ADAM_LR = 0.001
ADAM_B1 = 0.9
ADAM_B2 = 0.999
ADAM_EPS = 1e-08
ADAM_WD = 0.01
ADAM_STEP = 10

LANES = 128
SUBLANES = 8
ROW_TILE = 256
VMEM_LIMIT = 56 * 1024 * 1024
PACK_W = 1024
PACK_ROW_TILE = 128
NEG = -1e30
LOG2E = math.log2(math.e)
LN2 = math.log(2.0)
Q_PRESCALE = SOFTMAX_SCALE * LOG2E

_MESH = pl.DeviceIdType.MESH


def _pick(n, prefs):
    for p in prefs:
        if n % p == 0:
            return p
    return n


def _rt(m):
    return _pick(m, (384, ROW_TILE))


def _cparams(sem):
    return pltpu.CompilerParams(dimension_semantics=sem, vmem_limit_bytes=VMEM_LIMIT)


def _row(spec_cols, tm):
    return pl.BlockSpec((tm, spec_cols), lambda i: (i, 0))


def _full(shape):
    nd = len(shape)
    return pl.BlockSpec(shape, lambda *a: (0,) * nd)


def _sigmoid(x):
    return 1.0 / (1.0 + jnp.exp(-x))


def _silu(x):
    return x * _sigmoid(x)


def _dsilu(x):
    s = _sigmoid(x)
    return s * (1.0 + x * (1.0 - s))


def _dot(a, b):
    return jnp.dot(a, b, preferred_element_type=f32)


def _dot_nt(a, b):
    return lax.dot_general(a, b, (((1,), (1,)), ((), ())), preferred_element_type=f32)


def _dot_tn(a, b):
    return lax.dot_general(a, b, (((0,), (0,)), ((), ())), preferred_element_type=f32)


def _dot_hi(a, b):
    return jnp.dot(a, b, precision=lax.Precision.HIGHEST, preferred_element_type=f32)


def _mm(a, b, trans_b, name):
    M, K = a.shape
    N = b.shape[0] if trans_b else b.shape[1]
    tm = _pick(M, (768, 512, 256))
    tn = _pick(N, (1408, 512, 384, 256, 128))

    def body(a_ref, b_ref, o_ref):
        o_ref[...] = _dot_nt(a_ref[...], b_ref[...]) if trans_b else _dot(a_ref[...], b_ref[...])

    b_spec = pl.BlockSpec((tn, K), lambda i, j: (j, 0)) if trans_b else pl.BlockSpec((K, tn), lambda i, j: (0, j))
    return pl.pallas_call(
        body, name=name, grid=(M // tm, N // tn),
        in_specs=[pl.BlockSpec((tm, K), lambda i, j: (i, 0)), b_spec],
        out_specs=pl.BlockSpec((tm, tn), lambda i, j: (i, j)),
        out_shape=jax.ShapeDtypeStruct((M, N), f32),
        compiler_params=_cparams(("parallel", "parallel")),
    )(a, b)


def _mm_tn(a, g, name):
    M, K = a.shape
    N = g.shape[1]
    tm = _pick(M, (768, 512, 256))
    tk = _pick(K, (1024, 1408, 512, 384, 256))
    tn = _pick(N, (1024, 1408, 512, 384, 256, 128))

    def body(a_ref, g_ref, o_ref):
        @pl.when(pl.program_id(2) == 0)
        def _():
            o_ref[...] = jnp.zeros_like(o_ref)

        o_ref[...] += _dot_tn(a_ref[...].astype(bf16), g_ref[...].astype(bf16))

    return pl.pallas_call(
        body, name=name, grid=(K // tk, N // tn, M // tm),
        in_specs=[pl.BlockSpec((tm, tk), lambda k, j, m: (m, k)),
                  pl.BlockSpec((tm, tn), lambda k, j, m: (m, j))],
        out_specs=pl.BlockSpec((tk, tn), lambda k, j, m: (k, j)),
        out_shape=jax.ShapeDtypeStruct((K, N), f32),
        compiler_params=_cparams(("parallel", "parallel", "arbitrary")),
    )(a, g)


def _rstd(x):
    return lax.rsqrt(jnp.mean(x * x, axis=-1, keepdims=True) + EPS)


def _rms_bwd_math(x, g, dy):
    r = _rstd(x)
    xh = x * r
    dn = dy * g
    dx = r * (dn - xh * jnp.mean(dn * xh, axis=-1, keepdims=True))
    return dx, dy * xh


def _rms_fwd(x, g, out_dtype, name):
    M, K = x.shape
    tm = _rt(M)

    def body(x_ref, g_ref, o_ref):
        xv = x_ref[...]
        o_ref[...] = (xv * _rstd(xv) * g_ref[...]).astype(out_dtype)

    return pl.pallas_call(
        body, name=name, grid=(M // tm,), in_specs=[_row(K, tm), _full((1, K))],
        out_specs=_row(K, tm), out_shape=jax.ShapeDtypeStruct((M, K), out_dtype),
        compiler_params=_cparams(("parallel",)),
    )(x, g)


def _in_proj(h0, g, weights):
    M, K = h0.shape
    tm = _rt(M)
    n = len(weights)
    widths = [int(w.shape[1]) for w in weights]

    def body(x_ref, g_ref, *refs):
        xv = x_ref[...]
        hn = (xv * _rstd(xv) * g_ref[...]).astype(bf16)
        refs[n][...] = hn
        for p in range(n):
            refs[n + 1 + p][...] = _dot(hn, refs[p][...])

    return pl.pallas_call(
        body, name="in_proj", grid=(M // tm,),
        in_specs=[_row(K, tm), _full((1, K))] + [_full((K, wd)) for wd in widths],
        out_specs=[_row(K, tm)] + [_row(wd, tm) for wd in widths],
        out_shape=[jax.ShapeDtypeStruct((M, K), bf16)] + [jax.ShapeDtypeStruct((M, wd), f32) for wd in widths],
        compiler_params=_cparams(("parallel",)),
    )(h0, g, *weights)


def _in_proj_dw(hn, grads):
    M, K = hn.shape
    tm = _rt(M)
    n = len(grads)
    widths = [int(gr.shape[1]) for gr in grads]

    def body(a_ref, *refs):
        @pl.when(pl.program_id(0) == 0)
        def _():
            for p in range(n):
                refs[n + p][...] = jnp.zeros_like(refs[n + p])

        a = a_ref[...]
        for p in range(n):
            refs[n + p][...] += _dot_tn(a, refs[p][...].astype(bf16))

    return pl.pallas_call(
        body, name="in_proj_dw", grid=(M // tm,),
        in_specs=[_row(K, tm)] + [_row(wd, tm) for wd in widths],
        out_specs=[_full((K, wd)) for wd in widths],
        out_shape=[jax.ShapeDtypeStruct((K, wd), f32) for wd in widths],
        compiler_params=_cparams(("arbitrary",)),
    )(hn, *grads)


def _in_proj_dx(grads, weights, h0, g, dh1, carried, scatter):
    M, K = h0.shape
    tm = _rt(M)
    nt = M // tm
    n = len(grads)
    nx = len(carried)
    widths = [int(w.shape[1]) for w in weights]

    def body(*refs):
        g_refs, w_refs = refs[:n], refs[n:2 * n]
        x_ref, gain_ref, r_ref = refs[2 * n:2 * n + 3]
        cin = refs[2 * n + 3:2 * n + 3 + nx]
        dx_ref, dg_ref = refs[2 * n + 3 + nx:2 * n + 5 + nx]
        cout = refs[2 * n + 5 + nx:2 * n + 5 + 2 * nx]
        i = pl.program_id(0)
        _hosted_exchange(cin, cout, refs[2 * n + 5 + 2 * nx:], scatter, i == 0, i == nt - 1)

        @pl.when(i == 0)
        def _():
            dg_ref[...] = jnp.zeros_like(dg_ref)

        d_hn = None
        for p in range(n):
            t = _dot_nt(g_refs[p][...].astype(bf16), w_refs[p][...])
            d_hn = t if d_hn is None else d_hn + t
        dx, dgp = _rms_bwd_math(x_ref[...], gain_ref[...], d_hn)
        dx_ref[...] = dx + r_ref[...]
        dg_ref[...] += jnp.sum(dgp, axis=0, keepdims=True)

    any_spec = pl.BlockSpec(memory_space=pl.ANY)
    return pl.pallas_call(
        body, name="in_proj_dx", grid=(nt,),
        in_specs=([_row(wd, tm) for wd in widths] + [_full((K, wd)) for wd in widths]
                  + [_row(K, tm), _full((1, K)), _row(K, tm)] + [any_spec] * nx),
        out_specs=[_row(K, tm), _full((1, K))] + [any_spec] * nx,
        out_shape=[jax.ShapeDtypeStruct((M, K), f32), jax.ShapeDtypeStruct((1, K), f32)]
        + _exchange_shapes(carried, scatter),
        scratch_shapes=_exchange_sems(nx),
        compiler_params=_cparams(("arbitrary",)),
    )(*grads, *weights, h0, g, dh1, *carried)


def _out_proj_resid(an, ssm, wa, ws, h0, g2, g3):
    M, K = h0.shape
    tm = _rt(M)

    def body(a_ref, s_ref, wa_ref, ws_ref, h_ref, g2_ref, g3_ref, m_ref, h1_ref, hn_ref):
        mv = _dot(a_ref[...], wa_ref[...]) + _dot(s_ref[...], ws_ref[...])
        m_ref[...] = mv
        h1 = h_ref[...] + mv * _rstd(mv) * g2_ref[...]
        h1_ref[...] = h1
        hn_ref[...] = (h1 * _rstd(h1) * g3_ref[...]).astype(bf16)

    return pl.pallas_call(
        body, name="out_proj_resid", grid=(M // tm,),
        in_specs=[_row(an.shape[1], tm), _row(ssm.shape[1], tm), _full(wa.shape), _full(ws.shape),
                  _row(K, tm), _full((1, K)), _full((1, K))],
        out_specs=[_row(K, tm), _row(K, tm), _row(K, tm)],
        out_shape=[jax.ShapeDtypeStruct((M, K), f32), jax.ShapeDtypeStruct((M, K), f32),
                   jax.ShapeDtypeStruct((M, K), bf16)],
        compiler_params=_cparams(("parallel",)),
    )(an, ssm, wa, ws, h0, g2, g3)


def _final(h1, act, wdown, g4, tgt, n_real):
    M, K = h1.shape
    F = act.shape[1]
    tm = _rt(M)
    nt = M // tm

    def body(h_ref, a_ref, w_ref, g_ref, t_ref, dh_ref, dd_ref, dg_ref, ls_ref, acc_ref):
        i = pl.program_id(0)

        @pl.when(i == 0)
        def _():
            dg_ref[...] = jnp.zeros_like(dg_ref)
            acc_ref[...] = jnp.zeros_like(acc_ref)

        dv = _dot(a_ref[...], w_ref[...])
        g = g_ref[...]
        r = _rstd(dv)
        n = dv * r
        h2 = h_ref[...] + n * g
        rows = i * tm + lax.broadcasted_iota(jnp.int32, (tm, 1), 0)
        mask = ((rows >= N_META) & (rows < n_real)).astype(f32)
        diff = (h2 - t_ref[...]) * mask
        acc_ref[...] += jnp.sum(diff * diff, axis=0, keepdims=True)
        dh = diff * (1.0 / K)
        dh_ref[...] = dh
        dn = dh * g
        dd_ref[...] = (r * (dn - n * jnp.mean(dn * n, axis=-1, keepdims=True))).astype(bf16)
        dg_ref[...] += jnp.sum(dh * n, axis=0, keepdims=True)

        @pl.when(i == nt - 1)
        def _():
            ls_ref[...] = jnp.zeros((1, LANES), f32) + jnp.sum(acc_ref[...]) * (0.5 / K)

    return pl.pallas_call(
        body, name="ffn_down_loss", grid=(nt,),
        in_specs=[_row(K, tm), _row(F, tm), _full((F, K)), _full((1, K)), _row(K, tm)],
        out_specs=[_row(K, tm), _row(K, tm), _full((1, K)), _full((1, LANES))],
        out_shape=[jax.ShapeDtypeStruct((M, K), f32), jax.ShapeDtypeStruct((M, K), bf16),
                   jax.ShapeDtypeStruct((1, K), f32), jax.ShapeDtypeStruct((1, LANES), f32)],
        scratch_shapes=[pltpu.VMEM((1, K), f32)],
        compiler_params=_cparams(("arbitrary",)),
    )(h1, act, wdown, g4, tgt)


def _mid_bwd(h1, g3, dup_g, dup_v, wup, dh2, mix, g2):
    M, K = h1.shape
    F = dup_g.shape[1]
    tm = ROW_TILE

    def body(h_ref, g3_ref, ag_ref, av_ref, w_ref, dh2_ref, m_ref, g2_ref, dh1_ref, dm_ref, dg3_ref, dg2_ref):
        @pl.when(pl.program_id(0) == 0)
        def _():
            dg3_ref[...] = jnp.zeros_like(dg3_ref)
            dg2_ref[...] = jnp.zeros_like(dg2_ref)

        d_hn2 = _dot_nt(ag_ref[...], w_ref[:, 0:F]) + _dot_nt(av_ref[...], w_ref[:, F:2 * F])
        dx, dgp = _rms_bwd_math(h_ref[...], g3_ref[...], d_hn2)
        dh1 = dh2_ref[...] + dx
        dh1_ref[...] = dh1
        dg3_ref[...] += jnp.sum(dgp, axis=0, keepdims=True)
        dm, dgp2 = _rms_bwd_math(m_ref[...], g2_ref[...], dh1)
        dm_ref[...] = dm.astype(bf16)
        dg2_ref[...] += jnp.sum(dgp2, axis=0, keepdims=True)

    return pl.pallas_call(
        body, name="ffn_up_dx_mid_bwd", grid=(M // tm,),
        in_specs=[_row(K, tm), _full((1, K)), _row(F, tm), _row(F, tm), _full((K, 2 * F)), _row(K, tm),
                  _row(K, tm), _full((1, K))],
        out_specs=[_row(K, tm), _row(K, tm), _full((1, K)), _full((1, K))],
        out_shape=[jax.ShapeDtypeStruct((M, K), f32), jax.ShapeDtypeStruct((M, K), bf16),
                   jax.ShapeDtypeStruct((1, K), f32), jax.ShapeDtypeStruct((1, K), f32)],
        compiler_params=_cparams(("arbitrary",)),
    )(h1, g3, dup_g, dup_v, wup, dh2, mix, g2)


HEADS_PER_STEP = 4
CONV_RB = 16


def _conv_block_taps(x_ref, halo, rb, lanes, kw):
    r0 = rb * CONV_RB
    if rb == 0:
        cat = jnp.concatenate([halo, x_ref[0:CONV_RB, lanes]], axis=0)
        first = SUBLANES - (kw - 1)
        return [cat[first + k:first + k + CONV_RB] for k in range(kw)]
    return [x_ref[r0 - (kw - 1) + k:r0 - (kw - 1) + k + CONV_RB, lanes] for k in range(kw)]


def _conv_weighted(taps, w, kw):
    u = None
    for k in range(kw):
        t = taps[k] * w[k:k + 1, :]
        u = t if u is None else u + t
    return u


def _conv_block_dx(du, nxt, w, kw):
    cat = jnp.concatenate([du, nxt], axis=0)
    return _conv_weighted([cat[kw - 1 - k:kw - 1 - k + CONV_RB] for k in range(kw)], w, kw)


def _prev_spec(tm, tc, col_of, row_axis, reversed_tiles=0):
    def imap(*ids):
        i = ids[row_axis]
        if reversed_tiles:
            i = reversed_tiles - 1 - i
        return (jnp.maximum(i * (tm // SUBLANES) - 1, 0), col_of(*ids))
    return pl.BlockSpec((SUBLANES, tc), imap)


def _ssm_conv_fwd(xbc, w, b):
    M, C = xbc.shape
    tm, tc, kw = ROW_TILE, C, SSM_CONV

    def body(x_ref, h_ref, w_ref, b_ref, o_ref):
        i = pl.program_id(0)

        def chunk(j, carry):
            lanes = pl.ds(pl.multiple_of(j * LANES, LANES), LANES)
            halo = jnp.where(i == 0, 0.0, h_ref[:, lanes])
            wv = w_ref[:, lanes]
            bv = b_ref[:, lanes]
            for rb in range(tm // CONV_RB):
                u = _conv_weighted(_conv_block_taps(x_ref, halo, rb, lanes, kw), wv, kw) + bv
                o_ref[rb * CONV_RB:(rb + 1) * CONV_RB, lanes] = _silu(u)
            return carry

        lax.fori_loop(0, tc // LANES, chunk, 0)

    return pl.pallas_call(
        body, name="ssm_conv_fwd", grid=(M // tm, C // tc),
        in_specs=[pl.BlockSpec((tm, tc), lambda i, j: (i, j)),
                  _prev_spec(tm, tc, lambda i, j: j, 0),
                  pl.BlockSpec((SUBLANES, tc), lambda i, j: (0, j)),
                  pl.BlockSpec((1, tc), lambda i, j: (0, j))],
        out_specs=pl.BlockSpec((tm, tc), lambda i, j: (i, j)),
        out_shape=jax.ShapeDtypeStruct((M, C), f32),
        compiler_params=_cparams(("parallel", "parallel")),
    )(xbc, xbc, w, b)


def _ssm_conv_bwd(xbc, w, b, dout):
    M, C = xbc.shape
    tm, tc, kw = ROW_TILE, C // 3, SSM_CONV
    nt = M // tm

    def body(x_ref, h_ref, w_ref, b_ref, d_ref, dx_ref, dw_ref, db_ref, nxt_ref):
        i = pl.program_id(1)

        @pl.when(i == 0)
        def _():
            dw_ref[...] = jnp.zeros_like(dw_ref)
            db_ref[...] = jnp.zeros_like(db_ref)
            nxt_ref[...] = jnp.zeros_like(nxt_ref)

        def chunk(j, carry):
            lanes = pl.ds(pl.multiple_of(j * LANES, LANES), LANES)
            halo = jnp.where(i == nt - 1, 0.0, h_ref[:, lanes])
            wv = w_ref[:, lanes]
            bv = b_ref[:, lanes]
            nxt = nxt_ref[:, lanes]
            db = jnp.zeros((CONV_RB, LANES), f32)
            dw = [jnp.zeros((CONV_RB, LANES), f32) for _ in range(kw)]
            for rb in reversed(range(tm // CONV_RB)):
                rows = slice(rb * CONV_RB, (rb + 1) * CONV_RB)
                taps = _conv_block_taps(x_ref, halo, rb, lanes, kw)
                du = d_ref[rows, lanes] * _dsilu(_conv_weighted(taps, wv, kw) + bv)
                db = db + du
                dw = [dw[k] + du * taps[k] for k in range(kw)]
                dx_ref[rows, lanes] = _conv_block_dx(du, nxt, wv, kw).astype(bf16)
                nxt = du[0:SUBLANES]
            nxt_ref[:, lanes] = nxt
            db_ref[:, lanes] += jnp.sum(db, axis=0, keepdims=True)
            for k in range(kw):
                dw_ref[k:k + 1, lanes] += jnp.sum(dw[k], axis=0, keepdims=True)
            return carry

        lax.fori_loop(0, tc // LANES, chunk, 0)

    tile = pl.BlockSpec((tm, tc), lambda j, i: (nt - 1 - i, j))
    return pl.pallas_call(
        body, name="ssm_conv_bwd", grid=(C // tc, nt),
        in_specs=[tile, _prev_spec(tm, tc, lambda j, i: j, 1, nt),
                  pl.BlockSpec((SUBLANES, tc), lambda j, i: (0, j)),
                  pl.BlockSpec((1, tc), lambda j, i: (0, j)), tile],
        out_specs=[tile, pl.BlockSpec((SUBLANES, tc), lambda j, i: (0, j)),
                   pl.BlockSpec((1, tc), lambda j, i: (0, j))],
        out_shape=[jax.ShapeDtypeStruct((M, C), bf16), jax.ShapeDtypeStruct((SUBLANES, C), f32),
                   jax.ShapeDtypeStruct((1, C), f32)],
        scratch_shapes=[pltpu.VMEM((SUBLANES, tc), f32)],
        compiler_params=_cparams(("parallel", "arbitrary")),
    )(xbc, xbc, w, b, dout)


def _ffn_gate_fwd(up, w, b):
    M = up.shape[0]
    tm, tc, kw = ROW_TILE, D_FF // 2, FFN_CONV
    nc = D_FF // tc

    def body(xg_ref, hg_ref, xv_ref, hv_ref, wg_ref, wv_ref, bg_ref, bv_ref, o_ref):
        i = pl.program_id(0)

        def chunk(j, carry):
            lanes = pl.ds(pl.multiple_of(j * LANES, LANES), LANES)
            halo_g = jnp.where(i == 0, 0.0, hg_ref[:, lanes])
            halo_v = jnp.where(i == 0, 0.0, hv_ref[:, lanes])
            wg, wv = wg_ref[:, lanes], wv_ref[:, lanes]
            bg, bv = bg_ref[:, lanes], bv_ref[:, lanes]
            for rb in range(tm // CONV_RB):
                ug = _conv_weighted(_conv_block_taps(xg_ref, halo_g, rb, lanes, kw), wg, kw) + bg
                uv = _conv_weighted(_conv_block_taps(xv_ref, halo_v, rb, lanes, kw), wv, kw) + bv
                o_ref[rb * CONV_RB:(rb + 1) * CONV_RB, lanes] = (_silu(ug) * uv).astype(bf16)
            return carry

        lax.fori_loop(0, tc // LANES, chunk, 0)

    return pl.pallas_call(
        body, name="ffn_gate_fwd", grid=(M // tm, nc),
        in_specs=[pl.BlockSpec((tm, tc), lambda i, j: (i, j)),
                  _prev_spec(tm, tc, lambda i, j: j, 0),
                  pl.BlockSpec((tm, tc), lambda i, j: (i, j + nc)),
                  _prev_spec(tm, tc, lambda i, j: j + nc, 0),
                  pl.BlockSpec((SUBLANES, tc), lambda i, j: (0, j)),
                  pl.BlockSpec((SUBLANES, tc), lambda i, j: (0, j + nc)),
                  pl.BlockSpec((1, tc), lambda i, j: (0, j)),
                  pl.BlockSpec((1, tc), lambda i, j: (0, j + nc))],
        out_specs=pl.BlockSpec((tm, tc), lambda i, j: (i, j)),
        out_shape=jax.ShapeDtypeStruct((M, D_FF), bf16),
        compiler_params=_cparams(("parallel", "parallel")),
    )(up, up, up, up, w, w, b, b)


def _ffn_gate_bwd(up, w, b, d_down, wdown):
    M = up.shape[0]
    K = d_down.shape[1]
    tm, tc, kw = ROW_TILE, D_FF // 2, FFN_CONV
    nc = D_FF // tc
    nt = M // tm

    def body(xg_ref, hg_ref, xv_ref, hv_ref, wg_ref, wv_ref, bg_ref, bv_ref, dd_ref, wd_ref,
             dxg_ref, dxv_ref, dwg_ref, dwv_ref, dbg_ref, dbv_ref, ng_ref, nv_ref, d_ref):
        i = pl.program_id(1)

        @pl.when(i == 0)
        def _():
            for r in (dwg_ref, dwv_ref, dbg_ref, dbv_ref, ng_ref, nv_ref):
                r[...] = jnp.zeros_like(r)

        d_ref[...] = _dot_nt(dd_ref[...], wd_ref[...])

        def chunk(j, carry):
            lanes = pl.ds(pl.multiple_of(j * LANES, LANES), LANES)
            halo_g = jnp.where(i == nt - 1, 0.0, hg_ref[:, lanes])
            halo_v = jnp.where(i == nt - 1, 0.0, hv_ref[:, lanes])
            wg, wv = wg_ref[:, lanes], wv_ref[:, lanes]
            bg, bv = bg_ref[:, lanes], bv_ref[:, lanes]
            nxt_g, nxt_v = ng_ref[:, lanes], nv_ref[:, lanes]
            zero = jnp.zeros((CONV_RB, LANES), f32)
            dbg, dbv = zero, zero
            dwg = [zero for _ in range(kw)]
            dwv = [zero for _ in range(kw)]
            for rb in reversed(range(tm // CONV_RB)):
                rows = slice(rb * CONV_RB, (rb + 1) * CONV_RB)
                tg = _conv_block_taps(xg_ref, halo_g, rb, lanes, kw)
                tv = _conv_block_taps(xv_ref, halo_v, rb, lanes, kw)
                ug = _conv_weighted(tg, wg, kw) + bg
                uv = _conv_weighted(tv, wv, kw) + bv
                sg = _sigmoid(ug)
                da = d_ref[rows, lanes]
                dug = da * uv * (sg * (1.0 + ug * (1.0 - sg)))
                duv = da * (ug * sg)
                dbg = dbg + dug
                dbv = dbv + duv
                dwg = [dwg[k] + dug * tg[k] for k in range(kw)]
                dwv = [dwv[k] + duv * tv[k] for k in range(kw)]
                dxg_ref[rows, lanes] = _conv_block_dx(dug, nxt_g, wg, kw).astype(bf16)
                dxv_ref[rows, lanes] = _conv_block_dx(duv, nxt_v, wv, kw).astype(bf16)
                nxt_g, nxt_v = dug[0:SUBLANES], duv[0:SUBLANES]
            ng_ref[:, lanes] = nxt_g
            nv_ref[:, lanes] = nxt_v
            dbg_ref[:, lanes] += jnp.sum(dbg, axis=0, keepdims=True)
            dbv_ref[:, lanes] += jnp.sum(dbv, axis=0, keepdims=True)
            for k in range(kw):
                dwg_ref[k:k + 1, lanes] += jnp.sum(dwg[k], axis=0, keepdims=True)
                dwv_ref[k:k + 1, lanes] += jnp.sum(dwv[k], axis=0, keepdims=True)
            return carry

        lax.fori_loop(0, tc // LANES, chunk, 0)

    tile_g = pl.BlockSpec((tm, tc), lambda j, i: (nt - 1 - i, j))
    tile_v = pl.BlockSpec((tm, tc), lambda j, i: (nt - 1 - i, j + nc))
    ext = pltpu.VMEM((SUBLANES, tc), f32)
    return pl.pallas_call(
        body, name="ffn_gate_bwd", grid=(nc, nt),
        in_specs=[tile_g, _prev_spec(tm, tc, lambda j, i: j, 1, nt),
                  tile_v, _prev_spec(tm, tc, lambda j, i: j + nc, 1, nt),
                  pl.BlockSpec((SUBLANES, tc), lambda j, i: (0, j)),
                  pl.BlockSpec((SUBLANES, tc), lambda j, i: (0, j + nc)),
                  pl.BlockSpec((1, tc), lambda j, i: (0, j)),
                  pl.BlockSpec((1, tc), lambda j, i: (0, j + nc)),
                  pl.BlockSpec((tm, K), lambda j, i: (nt - 1 - i, 0)),
                  pl.BlockSpec((tc, K), lambda j, i: (j, 0))],
        out_specs=[tile_g, tile_g,
                   pl.BlockSpec((SUBLANES, tc), lambda j, i: (0, j)),
                   pl.BlockSpec((SUBLANES, tc), lambda j, i: (0, j)),
                   pl.BlockSpec((1, tc), lambda j, i: (0, j)),
                   pl.BlockSpec((1, tc), lambda j, i: (0, j))],
        out_shape=[jax.ShapeDtypeStruct((M, D_FF), bf16), jax.ShapeDtypeStruct((M, D_FF), bf16),
                   jax.ShapeDtypeStruct((SUBLANES, D_FF), f32), jax.ShapeDtypeStruct((SUBLANES, D_FF), f32),
                   jax.ShapeDtypeStruct((1, D_FF), f32), jax.ShapeDtypeStruct((1, D_FF), f32)],
        scratch_shapes=[ext, ext, pltpu.VMEM((tm, tc), f32)],
        compiler_params=_cparams(("parallel", "arbitrary")),
    )(up, up, up, up, w, w, b, b, d_down, wdown)


def _rope_apply(blk, cos, sin):
    lane = lax.broadcasted_iota(jnp.int32, blk.shape, 1)
    half = QK_ROPE // 2
    partner = jnp.where(lane < half, pltpu.roll(blk, LANES - half, 1), pltpu.roll(blk, half, 1))
    return blk * cos + partner * sin


def _rope_unapply(d, cos, sin):
    t = d * sin
    lane = lax.broadcasted_iota(jnp.int32, d.shape, 1)
    half = QK_ROPE // 2
    partner = jnp.where(lane < half, pltpu.roll(t, LANES - half, 1), pltpu.roll(t, half, 1))
    return d * cos + partner


def _up_q_rope(q_c, g, wuq, cos, sin):
    M, K = q_c.shape
    tm = _pick(M, (768, 512, 256))

    hs = HEADS_PER_STEP

    def body(x_ref, g_ref, b_ref, c_ref, s_ref, a_ref, o_ref):
        xv = x_ref[...]
        a = (xv * _rstd(xv) * g_ref[...]).astype(bf16)
        a_ref[...] = a
        r = _dot(a, b_ref[...]) * Q_PRESCALE
        c, s = c_ref[...], s_ref[...]
        for u in range(hs):
            o_ref[u, :, 0:QK_NOPE] = r[:, u * QK_PAD:u * QK_PAD + QK_NOPE].astype(bf16)
            o_ref[u, :, QK_NOPE:QK_PAD] = _rope_apply(r[:, u * QK_PAD + QK_NOPE:(u + 1) * QK_PAD], c, s).astype(bf16)

    return pl.pallas_call(
        body, name="up_q_rope", grid=(M // tm, MLA_HEADS // hs),
        in_specs=[pl.BlockSpec((tm, K), lambda i, h: (i, 0)),
                  pl.BlockSpec((1, K), lambda i, h: (0, 0)),
                  pl.BlockSpec((K, hs * QK_PAD), lambda i, h: (0, h)),
                  pl.BlockSpec((tm, LANES), lambda i, h: (i, 0)),
                  pl.BlockSpec((tm, LANES), lambda i, h: (i, 0))],
        out_specs=[pl.BlockSpec((tm, K), lambda i, h: (i, 0)),
                   pl.BlockSpec((hs, tm, QK_PAD), lambda i, h: (h, i, 0))],
        out_shape=[jax.ShapeDtypeStruct((M, K), bf16), jax.ShapeDtypeStruct((MLA_HEADS, M, QK_PAD), bf16)],
        compiler_params=_cparams(("parallel", "arbitrary")),
    )(q_c, g, wuq, cos, sin)


def _up_kv_rope(kv_c, g, wukv, kpe_raw, cos, sin):
    M, K = kv_c.shape
    tm = _pick(M, (768, 512, 256))

    hs = HEADS_PER_STEP
    w = QK_NOPE + V_DIM

    def body(x_ref, g_ref, b_ref, pe_ref, c_ref, s_ref, a_ref, k_ref, v_ref):
        xv = x_ref[...]
        a = (xv * _rstd(xv) * g_ref[...]).astype(bf16)
        a_ref[...] = a
        r = _dot(a, b_ref[...])
        pe = _rope_apply(pe_ref[...], c_ref[...], s_ref[...]).astype(bf16)
        for u in range(hs):
            k_ref[u, :, 0:QK_NOPE] = r[:, u * w:u * w + QK_NOPE].astype(bf16)
            k_ref[u, :, QK_NOPE:QK_PAD] = pe
            v_ref[u] = r[:, u * w + QK_NOPE:(u + 1) * w].astype(bf16)

    return pl.pallas_call(
        body, name="up_kv_rope", grid=(M // tm, MLA_HEADS // hs),
        in_specs=[pl.BlockSpec((tm, K), lambda i, h: (i, 0)),
                  pl.BlockSpec((1, K), lambda i, h: (0, 0)),
                  pl.BlockSpec((K, hs * w), lambda i, h: (0, h)),
                  pl.BlockSpec((tm, LANES), lambda i, h: (i, 0)),
                  pl.BlockSpec((tm, LANES), lambda i, h: (i, 0)),
                  pl.BlockSpec((tm, LANES), lambda i, h: (i, 0))],
        out_specs=[pl.BlockSpec((tm, K), lambda i, h: (i, 0)),
                   pl.BlockSpec((hs, tm, QK_PAD), lambda i, h: (h, i, 0)),
                   pl.BlockSpec((hs, tm, V_DIM), lambda i, h: (h, i, 0))],
        out_shape=[jax.ShapeDtypeStruct((M, K), bf16), jax.ShapeDtypeStruct((MLA_HEADS, M, QK_PAD), bf16),
                   jax.ShapeDtypeStruct((MLA_HEADS, M, V_DIM), bf16)],
        compiler_params=_cparams(("parallel", "arbitrary")),
    )(kv_c, g, wukv, kpe_raw, cos, sin)


def _latent_bwd(d_full_sc, w_ref, x_ref, g_ref, a_ref, dx_ref, dg_ref, dw_ref):
    d_full = d_full_sc[...]
    dx, dgp = _rms_bwd_math(x_ref[...], g_ref[...], _dot_nt(d_full, w_ref[...]))
    dx_ref[...] = dx.astype(bf16)
    dg_ref[...] += jnp.sum(dgp, axis=0, keepdims=True)
    dw_ref[...] += _dot_tn(a_ref[...], d_full)


def _latent_bwd_call(body, name, head_inputs, head_specs, cos, sin, w, x, g, a, extra_out_specs, extra_out_shape):
    M, K = x.shape
    tm = _rt(M)
    N = w.shape[1]
    return pl.pallas_call(
        body, name=name, grid=(M // tm,),
        in_specs=head_specs + [_row(LANES, tm), _row(LANES, tm), _full((K, N)), _row(K, tm), _full((1, K)),
                               _row(K, tm)],
        out_specs=[_row(K, tm), _full((1, K)), _full((K, N))] + extra_out_specs,
        out_shape=[jax.ShapeDtypeStruct((M, K), bf16), jax.ShapeDtypeStruct((1, K), f32),
                   jax.ShapeDtypeStruct((K, N), f32)] + extra_out_shape,
        scratch_shapes=[pltpu.VMEM((tm, N), bf16)],
        compiler_params=_cparams(("arbitrary",)),
    )(*head_inputs, cos, sin, w, x, g, a)


def _q_branch_bwd(dq, cos, sin, wuq, q_c, g, qn):
    tm = _rt(q_c.shape[0])

    def body(d_ref, c_ref, s_ref, w_ref, x_ref, g_ref, a_ref, dx_ref, dg_ref, dw_ref, full_sc):
        @pl.when(pl.program_id(0) == 0)
        def _():
            dg_ref[...] = jnp.zeros_like(dg_ref)
            dw_ref[...] = jnp.zeros_like(dw_ref)

        c, s = c_ref[...], s_ref[...]
        for h in range(MLA_HEADS):
            full_sc[:, h * QK_PAD:h * QK_PAD + QK_NOPE] = (d_ref[h, :, 0:QK_NOPE] * SOFTMAX_SCALE).astype(bf16)
            full_sc[:, h * QK_PAD + QK_NOPE:(h + 1) * QK_PAD] = (_rope_unapply(
                d_ref[h, :, QK_NOPE:QK_PAD], c, s) * SOFTMAX_SCALE).astype(bf16)
        _latent_bwd(full_sc, w_ref, x_ref, g_ref, a_ref, dx_ref, dg_ref, dw_ref)

    return _latent_bwd_call(body, "q_branch_bwd", [dq],
                            [pl.BlockSpec((MLA_HEADS, tm, QK_PAD), lambda i: (0, i, 0))],
                            cos, sin, wuq, q_c, g, qn, [], [])


def _kv_branch_bwd(dk, dv, cos, sin, wukv, kv_c, g, kvn):
    M = kv_c.shape[0]
    tm = _rt(M)
    w = QK_NOPE + V_DIM

    def body(dk_ref, dv_ref, c_ref, s_ref, w_ref, x_ref, g_ref, a_ref, dx_ref, dg_ref, dw_ref, pe_ref, full_sc):
        @pl.when(pl.program_id(0) == 0)
        def _():
            dg_ref[...] = jnp.zeros_like(dg_ref)
            dw_ref[...] = jnp.zeros_like(dw_ref)

        pe = None
        for h in range(MLA_HEADS):
            full_sc[:, h * w:h * w + QK_NOPE] = dk_ref[h, :, 0:QK_NOPE].astype(bf16)
            full_sc[:, h * w + QK_NOPE:(h + 1) * w] = dv_ref[h].astype(bf16)
            t = dk_ref[h, :, QK_NOPE:QK_PAD]
            pe = t if pe is None else pe + t
        pe_ref[...] = _rope_unapply(pe, c_ref[...], s_ref[...])
        _latent_bwd(full_sc, w_ref, x_ref, g_ref, a_ref, dx_ref, dg_ref, dw_ref)

    return _latent_bwd_call(body, "kv_branch_bwd", [dk, dv],
                            [pl.BlockSpec((MLA_HEADS, tm, QK_PAD), lambda i: (0, i, 0)),
                             pl.BlockSpec((MLA_HEADS, tm, V_DIM), lambda i: (0, i, 0))],
                            cos, sin, wukv, kv_c, g, kvn, [_row(LANES, tm)],
                            [jax.ShapeDtypeStruct((M, LANES), f32)])


def _attn_tile(M):
    return 768 if (M % 768 == 0 and M >= 4 * 768) else ROW_TILE


def _col_to_row(col):
    return col.T[0:1, :]


def _hosted_exchange(refs_in, refs_out, sems, scatter, first, last):
    copies = _exchange_copies(refs_in, refs_out, *sems, scatter)

    @pl.when(first)
    def _():
        for cp in copies:
            cp.start()

    @pl.when(last)
    def _():
        for cp in copies:
            cp.wait()


def _flash_fwd(q, k, v, carried, scatter):
    H, M, _ = q.shape
    T = _attn_tile(M)
    nq = M // T
    nx = len(carried)

    def body(*refs):
        q_ref, k_ref, v_ref = refs[:3]
        o_ref, lse_ref = refs[3 + nx:5 + nx]
        sa_ref, sb_ref, m_sc, l_sc, acc_sc = refs[5 + 2 * nx:10 + 2 * nx]
        h = pl.program_id(0)
        i = pl.program_id(1)
        _hosted_exchange(refs[3:3 + nx], refs[5 + nx:5 + 2 * nx], refs[10 + 2 * nx:], scatter,
                         (h == 0) & (i == 0), (h == H - 1) & (i == nq - 1))
        qv = q_ref[0]
        m_sc[...] = jnp.full_like(m_sc, NEG)
        l_sc[...] = jnp.zeros_like(l_sc)
        acc_sc[...] = jnp.zeros_like(acc_sc)

        def scores(j, s_ref):
            off = pl.multiple_of(j * T, T)
            s_ref[...] = _dot_nt(qv, k_ref[0, pl.ds(off, T), :])

        def softmax_pv(j, s_ref, masked):
            off = pl.multiple_of(j * T, T)
            s = s_ref[...]
            if masked:
                r = lax.broadcasted_iota(jnp.int32, (T, T), 0)
                c = lax.broadcasted_iota(jnp.int32, (T, T), 1)
                s = jnp.where(r >= c, s, NEG)
            m_prev = m_sc[...]
            m_new = jnp.maximum(m_prev, jnp.max(s, axis=1, keepdims=True))
            alpha = jnp.exp2(m_prev - m_new)
            p = jnp.exp2(s - m_new[:, 0:1])
            l_sc[...] = alpha * l_sc[...] + jnp.sum(p, axis=1, keepdims=True)
            acc_sc[...] = alpha * acc_sc[...] + _dot(p.astype(bf16), v_ref[0, pl.ds(off, T), :])
            m_sc[...] = m_new

        scores(0, sa_ref)

        def pair(jj, c):
            j0 = 2 * jj
            scores(j0 + 1, sb_ref)
            softmax_pv(j0, sa_ref, False)
            scores(j0 + 2, sa_ref)
            softmax_pv(j0 + 1, sb_ref, False)
            return c

        lax.fori_loop(0, i // 2, pair, 0)

        @pl.when(i % 2 == 0)
        def _():
            softmax_pv(i, sa_ref, True)

        @pl.when(i % 2 == 1)
        def _():
            scores(i, sb_ref)
            softmax_pv(i - 1, sa_ref, False)
            softmax_pv(i, sb_ref, True)

        l = l_sc[...]
        o_ref[...] = acc_sc[...] / l
        lse_ref[0, 0] = _col_to_row(m_sc[...] + jnp.log2(l))

    any_spec = pl.BlockSpec(memory_space=pl.ANY)
    return pl.pallas_call(
        body, name="flash_fwd", grid=(H, nq),
        in_specs=[pl.BlockSpec((1, T, QK_PAD), lambda h, i: (h, i, 0)),
                  pl.BlockSpec((1, M, QK_PAD), lambda h, i: (h, 0, 0)),
                  pl.BlockSpec((1, M, V_DIM), lambda h, i: (h, 0, 0))] + [any_spec] * nx,
        out_specs=[pl.BlockSpec((T, V_DIM), lambda h, i: (i, h)),
                   pl.BlockSpec((1, 1, 1, T), lambda h, i: (h, i, 0, 0))] + [any_spec] * nx,
        out_shape=[jax.ShapeDtypeStruct((M, H * V_DIM), f32),
                   jax.ShapeDtypeStruct((H, nq, 1, T), f32)] + _exchange_shapes(carried, scatter),
        scratch_shapes=[pltpu.VMEM((T, T), f32), pltpu.VMEM((T, T), f32),
                        pltpu.VMEM((T, LANES), f32), pltpu.VMEM((T, LANES), f32),
                        pltpu.VMEM((T, V_DIM), f32)] + _exchange_sems(nx),
        compiler_params=_cparams(("arbitrary", "arbitrary")),
    )(q, k, v, *carried)


def _attn_out_bwd(o, g, d_an):
    M, K = o.shape
    H = MLA_HEADS
    T = _attn_tile(M)

    def body(o_ref, g_ref, d_ref, dh_ref, dl_ref, dg_ref):
        @pl.when(pl.program_id(0) == 0)
        def _():
            dg_ref[...] = jnp.zeros_like(dg_ref)

        ov = o_ref[...]
        do, dgp = _rms_bwd_math(ov, g_ref[...], d_ref[...])
        dg_ref[...] += jnp.sum(dgp, axis=0, keepdims=True)
        for h in range(H):
            sl = slice(h * V_DIM, (h + 1) * V_DIM)
            doh = do[:, sl]
            dh_ref[h] = doh.astype(bf16)
            col = jnp.sum(ov[:, sl] * doh, axis=1, keepdims=True) + jnp.zeros((T, LANES), f32)
            dl_ref[h, 0] = _col_to_row(col)

    return pl.pallas_call(
        body, name="attn_out_bwd", grid=(M // T,),
        in_specs=[_row(K, T), _full((1, K)), _row(K, T)],
        out_specs=[pl.BlockSpec((H, T, V_DIM), lambda i: (0, i, 0)),
                   pl.BlockSpec((H, 1, 1, T), lambda i: (0, i, 0, 0)),
                   _full((1, K))],
        out_shape=[jax.ShapeDtypeStruct((H, M, V_DIM), bf16),
                   jax.ShapeDtypeStruct((H, M // T, 1, T), f32),
                   jax.ShapeDtypeStruct((1, K), f32)],
        compiler_params=_cparams(("arbitrary",)),
    )(o, g, d_an)


def _flash_bwd(q, k, v, do, lse, delta, carried, scatter):
    H, M, _ = q.shape
    T = _attn_tile(M)
    nq = M // T
    nx = len(carried)

    def body(*refs):
        q_ref, do_ref, lse_ref, dl_ref, k_ref, v_ref = refs[:6]
        dq_ref, dk_ref, dv_ref = refs[6 + nx:9 + nx]
        dk_sc, dv_sc = refs[9 + 2 * nx:11 + 2 * nx]
        j = pl.program_id(1)
        _hosted_exchange(refs[6:6 + nx], refs[9 + nx:9 + 2 * nx], refs[11 + 2 * nx:], scatter,
                         (pl.program_id(0) == 0) & (j == 0), (pl.program_id(0) == H - 1) & (j == nq - 1))

        @pl.when(j == 0)
        def _():
            dq_ref[...] = jnp.zeros_like(dq_ref)

        kt = k_ref[0]
        vt = v_ref[0]
        dk_sc[...] = jnp.zeros_like(dk_sc)
        dv_sc[...] = jnp.zeros_like(dv_sc)

        def step(i, masked):
            off = pl.multiple_of(i * T, T)
            qt = q_ref[0, pl.ds(off, T), :]
            dot_ = do_ref[0, pl.ds(off, T), :]
            st = _dot_nt(kt, qt)
            if masked:
                r = lax.broadcasted_iota(jnp.int32, (T, T), 0)
                c = lax.broadcasted_iota(jnp.int32, (T, T), 1)
                st = jnp.where(c >= r, st, NEG)
            pt = jnp.exp2(st - lse_ref[0, i])
            dv_sc[...] += _dot(pt.astype(bf16), dot_)
            dpt = _dot_nt(vt, dot_)
            dst = (pt * (dpt - dl_ref[0, i])).astype(bf16)
            dk_sc[...] += _dot(dst, qt)
            dq_ref[0, pl.ds(off, T), :] += _dot_tn(dst, kt)

        step(j, True)

        def loop_body(i, c):
            step(i, False)
            return c

        lax.fori_loop(j + 1, nq, loop_body, 0)
        dk_ref[0] = dk_sc[...] * LN2
        dv_ref[0] = dv_sc[...]

    any_spec = pl.BlockSpec(memory_space=pl.ANY)
    return pl.pallas_call(
        body, name="flash_bwd", grid=(H, nq),
        in_specs=[pl.BlockSpec((1, M, QK_PAD), lambda h, j: (h, 0, 0)),
                  pl.BlockSpec((1, M, V_DIM), lambda h, j: (h, 0, 0)),
                  pl.BlockSpec((1, nq, 1, T), lambda h, j: (h, 0, 0, 0)),
                  pl.BlockSpec((1, nq, 1, T), lambda h, j: (h, 0, 0, 0)),
                  pl.BlockSpec((1, T, QK_PAD), lambda h, j: (h, j, 0)),
                  pl.BlockSpec((1, T, V_DIM), lambda h, j: (h, j, 0))] + [any_spec] * nx,
        out_specs=[pl.BlockSpec((1, M, QK_PAD), lambda h, j: (h, 0, 0)),
                   pl.BlockSpec((1, T, QK_PAD), lambda h, j: (h, j, 0)),
                   pl.BlockSpec((1, T, V_DIM), lambda h, j: (h, j, 0))] + [any_spec] * nx,
        out_shape=[jax.ShapeDtypeStruct((H, M, QK_PAD), f32),
                   jax.ShapeDtypeStruct((H, M, QK_PAD), f32),
                   jax.ShapeDtypeStruct((H, M, V_DIM), f32)] + _exchange_shapes(carried, scatter),
        scratch_shapes=[pltpu.VMEM((T, QK_PAD), f32), pltpu.VMEM((T, V_DIM), f32)] + _exchange_sems(nx),
        compiler_params=_cparams(("arbitrary", "arbitrary")),
    )(q, do, lse, delta, k, v, *carried)


def _dt_fwd(dt_raw, bias, expand):
    M = dt_raw.shape[0]
    tm = _rt(M)

    def body(x_ref, b_ref, e_ref, o_ref, oe_ref):
        u = x_ref[...] + b_ref[...]
        sp = jnp.maximum(u, 0.0) + jnp.log(1.0 + jnp.exp(-jnp.abs(u)))
        lane = lax.broadcasted_iota(jnp.int32, u.shape, 1)
        dtp = jnp.where(lane < SSM_HEADS, sp, 0.0)
        o_ref[...] = dtp
        oe_ref[...] = _dot_hi(dtp, e_ref[...])

    return pl.pallas_call(
        body, name="dt_fwd", grid=(M // tm,),
        in_specs=[_row(LANES, tm), _full((1, LANES)), _full((LANES, D_SSM))],
        out_specs=[_row(LANES, tm), _row(D_SSM, tm)],
        out_shape=[jax.ShapeDtypeStruct((M, LANES), f32), jax.ShapeDtypeStruct((M, D_SSM), f32)],
        compiler_params=_cparams(("parallel",)),
    )(dt_raw, bias, expand)


def _dt_bwd(dt_raw, bias, ddt):
    M = dt_raw.shape[0]
    tm = _rt(M)

    def body(x_ref, b_ref, d_ref, o_ref, db_ref):
        @pl.when(pl.program_id(0) == 0)
        def _():
            db_ref[...] = jnp.zeros_like(db_ref)

        u = x_ref[...] + b_ref[...]
        lane = lax.broadcasted_iota(jnp.int32, u.shape, 1)
        g = jnp.where(lane < SSM_HEADS, d_ref[...] * _sigmoid(u), 0.0)
        o_ref[...] = g
        db_ref[...] += jnp.sum(g, axis=0, keepdims=True)

    return pl.pallas_call(
        body, name="dt_bwd", grid=(M // tm,),
        in_specs=[_row(LANES, tm), _full((1, LANES)), _row(LANES, tm)],
        out_specs=[_row(LANES, tm), _full((1, LANES))],
        out_shape=[jax.ShapeDtypeStruct((M, LANES), f32), jax.ShapeDtypeStruct((1, LANES), f32)],
        compiler_params=_cparams(("arbitrary",)),
    )(dt_raw, bias, ddt)


SSM_GW = SSM_HPG * SSM_P
SSM_PAIRS = SSM_GW // LANES


def _ssd_common(dte_ref, dtt_ref, ae_ref, acol_ref):
    Q = CHUNK
    r = lax.broadcasted_iota(jnp.int32, (Q, Q), 0)
    c = lax.broadcasted_iota(jnp.int32, (Q, Q), 1)
    causal = r >= c
    anti = c >= r
    tril = causal.astype(f32)
    triu = anti.astype(f32)
    dt_e = dte_ref[...]
    cs_e = _dot_hi(tril, dt_e * ae_ref[...])
    cst = _dot_hi(dtt_ref[...] * acol_ref[...], triu)
    cs_last = cs_e[Q - 1:Q, :]
    return causal, anti, triu, dt_e, cs_e, cst, jnp.exp(cs_e), jnp.exp(cs_last - cs_e), jnp.exp(cs_last)


def _half_masks():
    lane = lax.broadcasted_iota(jnp.int32, (CHUNK, LANES), 1)
    lo = lane < SSM_P
    return lo, jnp.logical_not(lo)


def _ssd_fwd(xbc_c, dt_e, dtt, a_e, a_col):
    M = xbc_c.shape[0]
    Q = CHUNK
    nch = M // Q

    def body(x_ref, dte_ref, dtt_ref, ae_ref, acol_ref, y_ref, hin_ref, ht_sc):
        @pl.when(pl.program_id(0) == 0)
        def _():
            ht_sc[...] = jnp.zeros_like(ht_sc)

        causal, _, _, dt_e, cs_e, cst, ecs_e, dte_e, elast_e = _ssd_common(dte_ref, dtt_ref, ae_ref, acol_ref)
        halves = _half_masks()
        for g in range(SSM_GROUPS):
            g0 = g * SSM_GW
            bg = x_ref[:, D_SSM + g * SSM_N:D_SSM + (g + 1) * SSM_N]
            cg = x_ref[:, D_SSM + D_BC + g * SSM_N:D_SSM + D_BC + (g + 1) * SSM_N]
            bg_b = bg.astype(bf16)
            cg_b = cg.astype(bf16)
            cb = _dot_nt(cg_b, bg_b)
            bgt_b = bg.T.astype(bf16)
            xdt_g = x_ref[:, g0:g0 + SSM_GW] * dt_e[:, g0:g0 + SSM_GW]
            ht = ht_sc[g]
            hin_ref[0, g] = ht
            y_off = _dot(cg_b, ht.astype(bf16)) * ecs_e[:, g0:g0 + SSM_GW]
            for pr in range(SSM_PAIRS):
                p0 = pr * LANES
                xdt_p = xdt_g[:, p0:p0 + LANES]
                acc = y_off[:, p0:p0 + LANES]
                for half in range(2):
                    h = g * SSM_HPG + pr * 2 + half
                    seg = cs_e[:, h * SSM_P:h * SSM_P + 1] - cst[h:h + 1, :]
                    lm = jnp.exp(jnp.where(causal, seg, -jnp.inf))
                    xm = jnp.where(halves[half], xdt_p, 0.0).astype(bf16)
                    acc = acc + _dot((cb * lm).astype(bf16), xm)
                y_ref[:, g0 + p0:g0 + p0 + LANES] = acc
            st = _dot(bgt_b, (xdt_g * dte_e[:, g0:g0 + SSM_GW]).astype(bf16))
            ht_sc[g] = ht * elast_e[:, g0:g0 + SSM_GW] + st

    return pl.pallas_call(
        body, name="ssd_fwd", grid=(nch,),
        in_specs=[pl.BlockSpec((Q, D_XBC), lambda c: (c, 0)),
                  pl.BlockSpec((Q, D_SSM), lambda c: (c, 0)),
                  pl.BlockSpec((SSM_HEADS, Q), lambda c: (0, c)),
                  _full((1, D_SSM)), _full((SSM_HEADS, LANES))],
        out_specs=[pl.BlockSpec((Q, D_SSM), lambda c: (c, 0)),
                   pl.BlockSpec((1, SSM_GROUPS, SSM_N, SSM_GW), lambda c: (c, 0, 0, 0))],
        out_shape=[jax.ShapeDtypeStruct((M, D_SSM), f32),
                   jax.ShapeDtypeStruct((nch, SSM_GROUPS, SSM_N, SSM_GW), f32)],
        scratch_shapes=[pltpu.VMEM((SSM_GROUPS, SSM_N, SSM_GW), f32)],
        compiler_params=_cparams(("arbitrary",)),
    )(xbc_c, dt_e, dtt, a_e, a_col)


def _ssd_bwd(xbc_c, dtp, dt_e, dtt, a_row, a_e, a_col, hin, y, z, d_ssm, g_ssm, d_exp, head_ind):
    M = xbc_c.shape[0]
    Q = CHUNK
    nch = M // Q
    rev = lambda c: nch - 1 - c

    gw = D_SSM // SSM_GROUPS

    def body(x_ref, dtp_ref, dte_ref, dtt_ref, arow_ref, ae_ref, acol_ref, hin_ref, y_ref, zz_ref, do_ref, gn_ref,
             dexp_ref, ind_ref, dx_ref, ddt_ref, da_ref, dz_ref, dgn_ref, dd_ref,
             dht_sc, z_sc, z1_sc, last_sc, ct_sc, dy_ref, ddc_sc):
        @pl.when(pl.program_id(0) == 0)
        def _():
            dht_sc[...] = jnp.zeros_like(dht_sc)
            da_ref[...] = jnp.zeros_like(da_ref)
            last_sc[...] = jnp.zeros_like(last_sc)
            ct_sc[...] = jnp.zeros_like(ct_sc)
            dgn_ref[...] = jnp.zeros_like(dgn_ref)
            ddc_sc[...] = jnp.zeros_like(ddc_sc)

        zv = zz_ref[...]
        xv = x_ref[:, 0:D_SSM]
        sz = _silu(zv)
        yd = y_ref[...] + dexp_ref[...] * xv
        yg = yd * sz
        dov = do_ref[...]
        for gi in range(SSM_GROUPS):
            sl = slice(gi * gw, (gi + 1) * gw)
            dyg, dgp = _rms_bwd_math(yg[:, sl], gn_ref[:, sl], dov[:, sl])
            dgn_ref[:, sl] += jnp.sum(dgp, axis=0, keepdims=True)
            dyd = dyg * sz[:, sl]
            dy_ref[:, sl] = dyd
            dz_ref[:, sl] = (dyg * yd[:, sl] * _dsilu(zv[:, sl])).astype(bf16)
            ddc_sc[:, sl] += jnp.sum(dyd * xv[:, sl], axis=0, keepdims=True)

        @pl.when(pl.program_id(0) == nch - 1)
        def _():
            dd_ref[...] = _dot_hi(ddc_sc[...], ind_ref[...])

        causal, anti, triu, dt_e, cs_e, cst, ecs_e, dte_e, elast_e = _ssd_common(dte_ref, dtt_ref, ae_ref, acol_ref)
        halves = _half_masks()
        lane = lax.broadcasted_iota(jnp.int32, (Q, LANES), 1)
        rsum = jnp.zeros((Q, LANES), f32)
        for g in range(SSM_GROUPS):
            g0 = g * SSM_GW
            gs = slice(g0, g0 + SSM_GW)
            b0 = D_SSM + g * SSM_N
            c0 = D_SSM + D_BC + g * SSM_N
            bg = x_ref[:, b0:b0 + SSM_N]
            cg = x_ref[:, c0:c0 + SSM_N]
            bg_b = bg.astype(bf16)
            cg_b = cg.astype(bf16)
            cgt_b = cg.T.astype(bf16)
            cbt = _dot_nt(bg_b, cg_b)
            cb = _dot_nt(cg_b, bg_b)
            x_g = x_ref[:, gs]
            dt_g = dt_e[:, gs]
            xdt_g = x_g * dt_g
            dy_g = dy_ref[:, gs]
            ht = hin_ref[0, g]
            ht_b = ht.astype(bf16)
            dht = dht_sc[g]
            dht_b = dht.astype(bf16)
            dye_b = (dy_g * ecs_e[:, gs]).astype(bf16)
            dc = _dot_nt(dye_b, ht_b)
            dht_new = dht * elast_e[:, gs] + _dot(cgt_b, dye_b)
            e = _dot(bg_b, dht_b)
            xdtd = xdt_g * dte_e[:, gs]
            db = _dot_nt(xdtd.astype(bf16), dht_b)
            dxdt_state = e * dte_e[:, gs]
            exd = e * xdtd
            z1_sc[:, gs] = dy_g * (_dot(cg_b, ht_b) * ecs_e[:, gs]) - exd
            last_sc[0:1, gs] = (jnp.sum(exd, axis=0, keepdims=True)
                                + jnp.sum(dht * ht, axis=0, keepdims=True) * elast_e[:, gs])
            dg_acc = jnp.zeros((Q, Q), f32)
            for pr in range(SSM_PAIRS):
                p0 = pr * LANES
                ps = slice(g0 + p0, g0 + p0 + LANES)
                dy_p = dy_g[:, p0:p0 + LANES]
                xdt_pb = xdt_g[:, p0:p0 + LANES].astype(bf16)
                acc = dxdt_state[:, p0:p0 + LANES]
                for half in range(2):
                    h = g * SSM_HPG + pr * 2 + half
                    seg = cs_e[:, h * SSM_P:h * SSM_P + 1] - cst[h:h + 1, :]
                    lm = jnp.exp(jnp.where(causal, seg, -jnp.inf))
                    lmt = jnp.exp(jnp.where(anti, -seg, -jnp.inf))
                    dym = jnp.where(halves[half], dy_p, 0.0).astype(bf16)
                    acc = acc + _dot((cbt * lmt).astype(bf16), dym)
                    dml = _dot_nt(dym, xdt_pb) * lm
                    dg_acc = dg_acc + dml
                    w = dml * cb
                    rsum = rsum + jnp.where(lane == h, jnp.sum(w, axis=1, keepdims=True), 0.0)
                    ct_sc[h:h + 1, :] = jnp.sum(w, axis=0, keepdims=True)
                dx_ref[:, ps] = acc * dt_g[:, p0:p0 + LANES] + dexp_ref[:, ps] * dy_p
                z_sc[:, ps] = acc * x_g[:, p0:p0 + LANES]
            dg_b = dg_acc.astype(bf16)
            dx_ref[:, c0:c0 + SSM_N] = dc + _dot(dg_b, bg_b)
            dx_ref[:, b0:b0 + SSM_N] = db + _dot_tn(dg_b, cg_b)
            dht_sc[g] = dht_new
        s1 = _dot_hi(z1_sc[...], ind_ref[...])
        s2 = _dot_hi(z_sc[...], ind_ref[...])
        last = _dot_hi(last_sc[...], ind_ref[...])[0:1, :]
        dtp = dtp_ref[...]
        row = lax.broadcasted_iota(jnp.int32, (Q, LANES), 0)
        dcs = s1 + rsum + jnp.where(row == Q - 1, last, 0.0)
        tril = causal.astype(f32)
        da = _dot_hi(triu, dcs) - _dot_hi(ct_sc[...], tril).T
        ddt_ref[...] = s2 + da * arow_ref[...]
        da_ref[...] += jnp.sum(da * dtp, axis=0, keepdims=True)

    chunk_rows = pl.BlockSpec((Q, D_SSM), lambda c: (rev(c), 0))
    return pl.pallas_call(
        body, name="ssd_bwd", grid=(nch,),
        in_specs=[pl.BlockSpec((Q, D_XBC), lambda c: (rev(c), 0)),
                  pl.BlockSpec((Q, LANES), lambda c: (rev(c), 0)),
                  pl.BlockSpec((Q, D_SSM), lambda c: (rev(c), 0)),
                  pl.BlockSpec((SSM_HEADS, Q), lambda c: (0, rev(c))),
                  _full((1, LANES)), _full((1, D_SSM)), _full((SSM_HEADS, LANES)),
                  pl.BlockSpec((1, SSM_GROUPS, SSM_N, SSM_GW), lambda c: (rev(c), 0, 0, 0)),
                  chunk_rows, chunk_rows, chunk_rows, _full((1, D_SSM)),
                  _full((1, D_SSM)), _full((D_SSM, LANES))],
        out_specs=[pl.BlockSpec((Q, D_XBC), lambda c: (rev(c), 0)),
                   pl.BlockSpec((Q, LANES), lambda c: (rev(c), 0)),
                   _full((1, LANES)), chunk_rows, _full((1, D_SSM)), _full((1, LANES))],
        out_shape=[jax.ShapeDtypeStruct((M, D_XBC), f32), jax.ShapeDtypeStruct((M, LANES), f32),
                   jax.ShapeDtypeStruct((1, LANES), f32), jax.ShapeDtypeStruct((M, D_SSM), bf16),
                   jax.ShapeDtypeStruct((1, D_SSM), f32), jax.ShapeDtypeStruct((1, LANES), f32)],
        scratch_shapes=[pltpu.VMEM((SSM_GROUPS, SSM_N, SSM_GW), f32), pltpu.VMEM((Q, D_SSM), f32),
                        pltpu.VMEM((Q, D_SSM), f32), pltpu.VMEM((SUBLANES, D_SSM), f32),
                        pltpu.VMEM((LANES, Q), f32), pltpu.VMEM((Q, D_SSM), f32), pltpu.VMEM((1, D_SSM), f32)],
        compiler_params=_cparams(("arbitrary",)),
    )(xbc_c, dtp, dt_e, dtt, a_row, a_e, a_col, hin, y, z, d_ssm, g_ssm, d_exp, head_ind)


def _gate_norm_fwd(y, xbc_c, z, d_exp, g):
    M = y.shape[0]
    tm = _rt(M)
    gw = D_SSM // SSM_GROUPS

    def body(y_ref, x_ref, z_ref, d_ref, g_ref, o_ref):
        yg = (y_ref[...] + d_ref[...] * x_ref[...]) * _silu(z_ref[...])
        for gi in range(SSM_GROUPS):
            blk = yg[:, gi * gw:(gi + 1) * gw]
            o_ref[:, gi * gw:(gi + 1) * gw] = (blk * _rstd(blk) * g_ref[:, gi * gw:(gi + 1) * gw]).astype(bf16)

    return pl.pallas_call(
        body, name="gate_norm_fwd", grid=(M // tm,),
        in_specs=[_row(D_SSM, tm), _row(D_SSM, tm), _row(D_SSM, tm), _full((1, D_SSM)), _full((1, D_SSM))],
        out_specs=_row(D_SSM, tm), out_shape=jax.ShapeDtypeStruct((M, D_SSM), bf16),
        compiler_params=_cparams(("parallel",)),
    )(y, xbc_c, z, d_exp, g)


_PEER_FLIPS = [(0, 0, 1), (0, 1, 0), (0, 1, 1), (1, 0, 0), (1, 0, 1), (1, 1, 0), (1, 1, 1)]


def _exchange_copies(ins, outs, send_sems, recv_sems, loc_sems, scatter):
    n = len(ins)
    x, y, c = lax.axis_index("x"), lax.axis_index("y"), lax.axis_index("c")
    me = 4 * x + 2 * y + c
    copies = []
    for a in range(n):
        src = ins[a].at[me] if scatter else ins[a]
        copies.append(pltpu.make_async_copy(src, outs[a].at[me], loc_sems.at[a]))
    for p, (fx, fy, fc) in enumerate(_PEER_FLIPS):
        tx = 1 - x if fx else x
        ty = 1 - y if fy else y
        tc = 1 - c if fc else c
        tgt = 4 * tx + 2 * ty + tc
        for a in range(n):
            src = ins[a].at[tgt] if scatter else ins[a]
            copies.append(pltpu.make_async_remote_copy(
                src_ref=src, dst_ref=outs[a].at[me],
                send_sem=send_sems.at[p * n + a], recv_sem=recv_sems.at[p * n + a],
                device_id=(tx, ty, tc), device_id_type=_MESH))
    return copies


def _exchange_shapes(arrays, scatter):
    return [jax.ShapeDtypeStruct(a.shape if scatter else (N_DEV,) + a.shape, a.dtype) for a in arrays]


def _exchange_sems(n):
    return [pltpu.SemaphoreType.DMA((7 * n,)), pltpu.SemaphoreType.DMA((7 * n,)), pltpu.SemaphoreType.DMA((n,))]


def _gather_two_level(arrays, name):
    n = len(arrays)

    def body(*refs):
        ins, outs = refs[:n], refs[n:2 * n]
        send_sems, recv_sems, loc_sems = refs[2 * n:]
        x, y, c = lax.axis_index("x"), lax.axis_index("y"), lax.axis_index("c")
        me, sibling = (x, y, c), (x, y, 1 - c)
        chips = [(1 - x, y), (x, 1 - y), (1 - x, 1 - y)]

        def slot(a, dev):
            return outs[a].at[4 * dev[0] + 2 * dev[1] + dev[2]]

        def copy(a, k, block, to, src=None):
            return pltpu.make_async_remote_copy(
                src_ref=slot(a, block) if src is None else src, dst_ref=slot(a, block),
                send_sem=send_sems.at[7 * a + k], recv_sem=recv_sems.at[7 * a + k],
                device_id=to, device_id_type=_MESH)

        mine = [pltpu.make_async_copy(ins[a], slot(a, me), loc_sems.at[a]) for a in range(n)]
        first = []
        for a in range(n):
            first.append(copy(a, 0, me, sibling, src=ins[a]))
            first += [copy(a, 1 + j, me, (*chip, c), src=ins[a]) for j, chip in enumerate(chips)]
        for cp in mine + first:
            cp.start()
        passed = []
        for j, chip in enumerate(chips):
            for a in range(n):
                copy(a, 1 + j, (*chip, c), me).wait_recv()
                cp = copy(a, 4 + j, (*chip, c), sibling)
                cp.start()
                passed.append(cp)
        for a in range(n):
            copy(a, 0, sibling, me).wait_recv()
            for j, chip in enumerate(chips):
                copy(a, 4 + j, (*chip, 1 - c), me).wait_recv()
        for cp in first + passed:
            cp.wait_send()
        for cp in mine:
            cp.wait()

    any_spec = pl.BlockSpec(memory_space=pl.ANY)
    return pl.pallas_call(
        body, name=name, in_specs=[any_spec] * n, out_specs=[any_spec] * n,
        out_shape=_exchange_shapes(arrays, False), scratch_shapes=_exchange_sems(n),
    )(*arrays)


def _exchange_tail(scattered, gathered, name):
    ns, ng = len(scattered), len(gathered)
    n = ns + ng

    def body(*refs):
        sems = refs[2 * n:]
        copies = (_exchange_copies(refs[:ns], refs[n:n + ns], *sems[:3], True)
                  + _exchange_copies(refs[ns:n], refs[n + ns:2 * n], *sems[3:], False))
        for cp in copies:
            cp.start()
        for cp in copies:
            cp.wait()

    any_spec = pl.BlockSpec(memory_space=pl.ANY)
    return pl.pallas_call(
        body, name=name, in_specs=[any_spec] * n, out_specs=[any_spec] * n,
        out_shape=_exchange_shapes(scattered, True) + _exchange_shapes(gathered, False),
        scratch_shapes=_exchange_sems(ns) + _exchange_sems(ng),
    )(*scattered, *gathered)


def _adamw_math(g, w, m, v):
    c1 = 1.0 - ADAM_B1 ** ADAM_STEP
    c2 = 1.0 - ADAM_B2 ** ADAM_STEP
    mn = ADAM_B1 * m + (1.0 - ADAM_B1) * g
    vn = ADAM_B2 * v + (1.0 - ADAM_B2) * (g * g)
    m_hat = mn / c1
    v_hat = vn / c2
    return -ADAM_LR * (m_hat / (jnp.sqrt(v_hat) + ADAM_EPS) + ADAM_WD * w), mn, vn


def _adamw(parts, w, m, v, name):
    R, C = w.shape
    tr = _pick(R, (PACK_ROW_TILE, 64, 32, 16, 8))

    def body(p_ref, w_ref, m_ref, v_ref, g_ref, d_ref, nm_ref, nv_ref):
        g = p_ref[0].astype(f32)
        for s in range(1, N_DEV):
            g = g + p_ref[s].astype(f32)
        g_ref[...] = g
        d_ref[...], nm_ref[...], nv_ref[...] = _adamw_math(g, w_ref[...], m_ref[...], v_ref[...])

    spec = pl.BlockSpec((tr, C), lambda i: (i, 0))
    return pl.pallas_call(
        body, name=name, grid=(R // tr,),
        in_specs=[pl.BlockSpec((N_DEV, tr, C), lambda i: (0, i, 0)), spec, spec, spec],
        out_specs=[spec] * 4, out_shape=[jax.ShapeDtypeStruct((R, C), f32)] * 4,
        compiler_params=_cparams(("parallel",)),
    )(parts, w, m, v)


def _adamw_replicated(parts, ws, ms, vs):
    n = len(ws)
    R = parts.shape[1]
    sizes = [int(w.shape[1]) for w in ws]

    def body(*refs):
        p_ref = refs[0]
        w_refs, m_refs, v_refs = refs[1:1 + n], refs[1 + n:1 + 2 * n], refs[1 + 2 * n:1 + 3 * n]
        loss_ref = refs[1 + 3 * n]
        outs = refs[2 + 3 * n:]
        g_all = p_ref[0]
        for s in range(1, N_DEV):
            g_all = g_all + p_ref[s]
        row = 0
        for p in range(n):
            pieces, left = [], sizes[p]
            while left > 0:
                take = min(left, PACK_W)
                pieces.append(g_all[row:row + 1, 0:take])
                left -= take
                row += 1
            g = pieces[0] if len(pieces) == 1 else jnp.concatenate(pieces, axis=1)
            d, mn, vn = _adamw_math(g, w_refs[p][...], m_refs[p][...], v_refs[p][...])
            outs[4 * p][...] = g
            outs[4 * p + 1][...] = d
            outs[4 * p + 2][...] = mn
            outs[4 * p + 3][...] = vn
        loss_ref[...] = g_all[row:row + 1, 0:LANES]

    in_specs = [_full((N_DEV, R, PACK_W))] + [_full((1, s)) for s in sizes] * 3
    out_specs = [_full((1, LANES))]
    out_shape = [jax.ShapeDtypeStruct((1, LANES), f32)]
    for s in sizes:
        out_specs += [_full((1, s))] * 4
        out_shape += [jax.ShapeDtypeStruct((1, s), f32)] * 4
    res = pl.pallas_call(
        body, name="adamw_replicated", in_specs=in_specs, out_specs=out_specs, out_shape=out_shape,
        compiler_params=pltpu.CompilerParams(vmem_limit_bytes=VMEM_LIMIT),
    )(parts, *ws, *ms, *vs)
    return res[0], [res[1 + 4 * p:5 + 4 * p] for p in range(n)]


def _flat_rows(a, lead_ndim):
    lead = a.shape[:lead_ndim]
    n = int(np.prod(a.shape[lead_ndim:]))
    a = a.reshape(lead + (n,))
    pad = (-n) % PACK_W
    if pad:
        a = jnp.pad(a, [(0, 0)] * lead_ndim + [(0, pad)])
    return a.reshape(lead + ((n + pad) // PACK_W, PACK_W))


def _pack(arrays, lead_ndim, total_rows, dtype):
    rows = [_flat_rows(a.astype(dtype), lead_ndim) for a in arrays]
    cat = jnp.concatenate(rows, axis=lead_ndim)
    pad = total_rows - cat.shape[lead_ndim]
    if pad:
        cat = jnp.pad(cat, [(0, 0)] * lead_ndim + [(0, pad), (0, 0)])
    return cat


def _unpack(buf, shapes, lead_ndim):
    out = []
    r = 0
    lead = buf.shape[:lead_ndim]
    for shp in shapes:
        n = int(np.prod(shp))
        nr = -(-n // PACK_W)
        piece = lax.slice_in_dim(buf, r, r + nr, axis=lead_ndim)
        piece = piece.reshape(lead + (nr * PACK_W,))
        piece = lax.slice_in_dim(piece, 0, n, axis=lead_ndim)
        out.append(piece.reshape(lead + tuple(shp)))
        r += nr
    return out


def _round_up(n, m):
    return -(-n // m) * m


def kernel(x, meta_tokens, norm_mix_pre, norm_mix_post, norm_ffn_pre, norm_ffn_post, w_in, q_a_norm, w_uq, kv_a_norm, w_ukv, attn_out_norm, ssm_conv_w, ssm_conv_b, ssm_dt_bias, ssm_A_log, ssm_D, ssm_norm, w_out, w_up, ffn_conv_w, ffn_conv_b, w_down, loss_target, m_meta_tokens, m_norm_mix_pre, m_norm_mix_post, m_norm_ffn_pre, m_norm_ffn_post, m_w_in, m_q_a_norm, m_w_uq, m_kv_a_norm, m_w_ukv, m_attn_out_norm, m_ssm_conv_w, m_ssm_conv_b, m_ssm_dt_bias, m_ssm_A_log, m_ssm_D, m_ssm_norm, m_w_out, m_w_up, m_ffn_conv_w, m_ffn_conv_b, m_w_down, v_meta_tokens, v_norm_mix_pre, v_norm_mix_post, v_norm_ffn_pre, v_norm_ffn_post, v_w_in, v_q_a_norm, v_w_uq, v_kv_a_norm, v_w_ukv, v_attn_out_norm, v_ssm_conv_w, v_ssm_conv_b, v_ssm_dt_bias, v_ssm_A_log, v_ssm_D, v_ssm_norm, v_w_out, v_w_up, v_ffn_conv_w, v_ffn_conv_b, v_w_down):
    seq = x.shape[1]
    n_real = N_META + seq
    Lp = _round_up(n_real, 768) if n_real > 2048 else _round_up(n_real, ROW_TILE)
    D = D_MODEL

    early_w = [w_uq, w_ukv]
    late_w = [w_out, w_down]
    sharded_s = [meta_tokens, ssm_conv_w, ffn_conv_w]
    grp_a = dict(names=["w_out", "w_down"], w=late_w, m=[m_w_out, m_w_down],
                 v=[v_w_out, v_w_down])
    grp_b = dict(names=["w_uq", "w_ukv", "ssm_conv_w", "ffn_conv_w"],
                 w=early_w + [ssm_conv_w, ffn_conv_w],
                 m=[m_w_uq, m_w_ukv, m_ssm_conv_w, m_ffn_conv_w],
                 v=[v_w_uq, v_w_ukv, v_ssm_conv_w, v_ffn_conv_w])
    grp_meta = dict(names=["meta_tokens"], w=[meta_tokens], m=[m_meta_tokens], v=[v_meta_tokens])
    repl_w = [norm_mix_pre, norm_mix_post, norm_ffn_pre, norm_ffn_post, q_a_norm, kv_a_norm, attn_out_norm,
              ssm_conv_b, ssm_dt_bias, ssm_A_log, ssm_D, ssm_norm, ffn_conv_b]
    repl_m = [m_norm_mix_pre, m_norm_mix_post, m_norm_ffn_pre, m_norm_ffn_post, m_q_a_norm, m_kv_a_norm,
              m_attn_out_norm, m_ssm_conv_b, m_ssm_dt_bias, m_ssm_A_log, m_ssm_D, m_ssm_norm, m_ffn_conv_b]
    repl_v = [v_norm_mix_pre, v_norm_mix_post, v_norm_ffn_pre, v_norm_ffn_post, v_q_a_norm, v_kv_a_norm,
              v_attn_out_norm, v_ssm_conv_b, v_ssm_dt_bias, v_ssm_A_log, v_ssm_D, v_ssm_norm, v_ffn_conv_b]

    def pack_rows(arrs, lead):
        return _round_up(sum(-(-int(np.prod(a.shape[lead:])) // PACK_W) for a in arrs), 16)

    wb = _pack(early_w, 0, pack_rows(early_w, 0), bf16)
    wl = _pack(late_w, 0, pack_rows(late_w, 0), bf16)
    ws = _pack(sharded_s, 0, pack_rows(sharded_s, 0), f32)
    wb_all, ws_all, win_all = _gather_two_level([wb, ws, w_in[0].astype(bf16)], "gather_weights")
    g_w_uq, g_w_ukv = _unpack(wb_all, [a.shape for a in early_w], 1)
    g_meta, g_sconv, g_fconv = _unpack(ws_all, [a.shape for a in sharded_s], 1)

    def cols(gathered):
        t = gathered[:, 0]
        return jnp.transpose(t, (1, 0, 2)).reshape(t.shape[1], N_DEV * t.shape[2])

    win = cols(win_all[:, None])
    o = np.cumsum((0, Q_RANK, KV_RANK, QK_ROPE, D_SSM, D_XBC, SSM_HEADS))
    w_q, w_kv = win[:, o[0]:o[1]], win[:, o[1]:o[2]]
    w_rope = jnp.pad(win[:, o[2]:o[3]], ((0, 0), (0, LANES - QK_ROPE)))
    w_z, w_xbc = win[:, o[3]:o[4]], win[:, o[4]:o[5]]
    w_dt = jnp.pad(win[:, o[5]:o[6]], ((0, 0), (0, LANES - SSM_HEADS)))
    wuq = g_w_uq.reshape(Q_RANK, MLA_HEADS, QK_NOPE + QK_ROPE)
    wuq = jnp.pad(wuq, ((0, 0), (0, 0), (0, QK_PAD - QK_NOPE - QK_ROPE))).reshape(Q_RANK, MLA_HEADS * QK_PAD)
    wukv = g_w_ukv.reshape(KV_RANK, MLA_HEADS * (QK_NOPE + V_DIM))
    meta_full = jnp.transpose(g_meta, (1, 0, 2)).reshape(N_META, D)
    sconv_w = jnp.pad(cols(g_sconv), ((0, SUBLANES - SSM_CONV), (0, 0)))
    fconv_w = jnp.pad(cols(g_fconv), ((0, SUBLANES - FFN_CONV), (0, 0)))

    pos = jnp.arange(Lp, dtype=f32)
    inv = ROPE_THETA ** (-jnp.arange(0, QK_ROPE, 2, dtype=f32) / QK_ROPE)
    ang = pos[:, None] * inv[None, :]
    cs_, sn_ = jnp.cos(ang), jnp.sin(ang)
    zpad = jnp.zeros((Lp, LANES - QK_ROPE), f32)
    cos_t = jnp.concatenate([cs_, cs_, zpad], axis=1)
    sin_t = jnp.concatenate([-sn_, sn_, zpad], axis=1)
    dt_bias_p = jnp.pad(ssm_dt_bias, ((0, 0), (0, LANES - SSM_HEADS)))
    a_neg = -jnp.exp(ssm_A_log)
    a_row = jnp.pad(a_neg, ((0, 0), (0, LANES - SSM_HEADS)))
    a_col = jnp.broadcast_to(a_neg.reshape(SSM_HEADS, 1), (SSM_HEADS, LANES))
    d_exp = jnp.repeat(ssm_D, SSM_P, axis=1)
    a_e = jnp.repeat(a_neg, SSM_P, axis=1)
    head_ind = (jnp.arange(D_SSM)[:, None] // SSM_P == jnp.arange(LANES)[None, :]).astype(f32)

    xb = x[0]
    h0 = jnp.concatenate([meta_full, xb, jnp.zeros((Lp - n_real, D), f32)], axis=0)
    tgt = jnp.pad(loss_target[0], ((N_META, Lp - n_real), (0, 0)))
    hn1, q_c, kv_c, kpe_raw, z, xbc, dt_raw = _in_proj(h0, norm_mix_pre, [w_q, w_kv, w_rope, w_z, w_xbc, w_dt])

    qn, qh = _up_q_rope(q_c, q_a_norm, wuq, cos_t, sin_t)
    kvn, kh, vh = _up_kv_rope(kv_c, kv_a_norm, wukv, kpe_raw, cos_t, sin_t)
    attn, lse, wl_all, wup_all = _flash_fwd(qh, kh, vh, [wl, w_up[0].astype(bf16)], False)
    g_w_out, g_w_down = _unpack(wl_all, [a.shape for a in late_w], 1)
    wout = g_w_out.reshape(D_ATTN + D_SSM, D)
    wout_a, wout_s = wout[:D_ATTN], wout[D_ATTN:]
    wup = cols(wup_all[:, None])
    wdown = g_w_down.reshape(D_FF, D)
    an = _rms_fwd(attn, attn_out_norm, bf16, "norm_attn_out")

    xbc_c = _ssm_conv_fwd(xbc, sconv_w, ssm_conv_b)
    dtp, dt_e = _dt_fwd(dt_raw, dt_bias_p, jnp.transpose(head_ind))
    dtt = jnp.transpose(dtp[:, :SSM_HEADS])
    y_ssd, hin = _ssd_fwd(xbc_c, dt_e, dtt, a_e, a_col)
    ssm = _gate_norm_fwd(y_ssd, xbc_c, z, d_exp, ssm_norm)

    mix, h1, hn2 = _out_proj_resid(an, ssm, wout_a, wout_s, h0, norm_mix_post, norm_ffn_pre)
    up = _mm(hn2, wup, False, "ffn_up")
    act = _ffn_gate_fwd(up, fconv_w, ffn_conv_b)
    dh2, d_down, dg_ffn_post, loss_part = _final(h1, act, wdown, norm_ffn_post, tgt, n_real)

    dw_down = _mm_tn(act, d_down, "ffn_down_dw")
    dup_g, dup_v, dwc_g, dwc_v, dbc_g, dbc_v = _ffn_gate_bwd(up, fconv_w, ffn_conv_b, d_down, wdown)
    dw_up = jnp.concatenate([_mm_tn(hn2, dup_g, "ffn_up_dw_g"), _mm_tn(hn2, dup_v, "ffn_up_dw_v")], axis=1)
    dh1, d_mix, dg_ffn_pre, dg_mix_post = _mid_bwd(h1, norm_ffn_pre, dup_g, dup_v, wup, dh2, mix, norm_mix_post)
    d_an = _mm(d_mix, wout_a, True, "out_proj_dx_a")
    d_ssm = _mm(d_mix, wout_s, True, "out_proj_dx_s")
    dw_out = jnp.concatenate([_mm_tn(an, d_mix, "out_proj_dw_a"), _mm_tn(ssm, d_mix, "out_proj_dw_s")], axis=0)

    do_h, delta, dg_attn_out = _attn_out_bwd(attn, attn_out_norm, d_an)
    def col_blocks(gm):
        r, cc = gm.shape
        return jnp.transpose(gm.reshape(r, N_DEV, cc // N_DEV), (1, 0, 2))

    blocks_a = [dw_out.reshape(N_DEV, (D_ATTN + D_SSM) // N_DEV, D), dw_down.reshape(N_DEV, D_FF // N_DEV, D)]
    gpack_a = _pack(blocks_a, 1, pack_rows(blocks_a, 1), bf16)
    dqh, dkh, dvh, gparts_a, gparts_up = _flash_bwd(qh, kh, vh, do_h, lse, delta,
                                                    [gpack_a, col_blocks(dw_up).astype(bf16)], True)
    d_q_c, dg_q, dw_uq = _q_branch_bwd(dqh, cos_t, sin_t, wuq, q_c, q_a_norm, qn)
    d_kv_c, dg_kv, dw_ukv, d_kpe_raw = _kv_branch_bwd(dkh, dvh, cos_t, sin_t, wukv, kv_c, kv_a_norm, kvn)

    d_xbc_c, ddt, da_heads, dz, dg_ssm, dd_heads = _ssd_bwd(
        xbc_c, dtp, dt_e, dtt, a_row, a_e, a_col, hin, y_ssd, z, d_ssm, ssm_norm, d_exp, head_ind)
    d_xbc, dw_sconv, db_sconv = _ssm_conv_bwd(xbc, sconv_w, ssm_conv_b, d_xbc_c)
    d_dt_raw, d_dt_bias = _dt_bwd(dt_raw, dt_bias_p, ddt)

    dw_q, dw_kv, dw_rope, dw_z, dw_xbc, dw_dt = _in_proj_dw(hn1, [d_q_c, d_kv_c, d_kpe_raw, dz, d_xbc, d_dt_raw])
    dw_in = jnp.concatenate([dw_q, dw_kv, dw_rope[:, :QK_ROPE], dw_z, dw_xbc, dw_dt[:, :SSM_HEADS]], axis=1)
    dw_uq3 = dw_uq.reshape(Q_RANK, MLA_HEADS, QK_PAD)[:, :, :QK_NOPE + QK_ROPE]
    blocks_b = [
        dw_uq3.reshape(N_DEV, Q_RANK // N_DEV, MLA_HEADS, QK_NOPE + QK_ROPE),
        dw_ukv.reshape(N_DEV, KV_RANK // N_DEV, MLA_HEADS, QK_NOPE + V_DIM),
        col_blocks(dw_sconv[:SSM_CONV]),
        col_blocks(jnp.concatenate([dwc_g, dwc_v], axis=1)[:FFN_CONV]),
    ]
    gpack_b = _pack(blocks_b, 1, pack_rows(blocks_b, 1), bf16)
    dh0, dg_mix_pre, gparts_b, gparts_in = _in_proj_dx(
        [d_q_c, d_kv_c, d_kpe_raw, dz, d_xbc, d_dt_raw], [w_q, w_kv, w_rope, w_z, w_xbc, w_dt],
        h0, norm_mix_pre, dh1, [gpack_b, col_blocks(dw_in).astype(bf16)], True)

    grad_x = dh0[N_META:n_real][None]
    meta_blocks = col_blocks(dh0[:N_META]).reshape(N_DEV, N_META * D // N_DEV // PACK_W, PACK_W)


    def adam_group(parts, grp, name):
        rows = parts.shape[1]
        packs = [_pack([a[None] for a in grp[k]], 1, rows, f32)[0] for k in ("w", "m", "v")]
        outs = _adamw(parts, *packs, name)
        shapes = [a.shape for a in grp["w"]]
        return [dict(zip(grp["names"], [t[0] for t in _unpack(b[None], shapes, 1)])) for b in outs]

    def adam_own_layout(parts, name, w, m, v):
        return [{name: t[None]} for t in _adamw(parts, w[0], m[0], v[0], "adamw_" + name)]

    sh_a = adam_group(gparts_a, grp_a, "adamw_sharded_a")
    sh_b = adam_group(gparts_b, grp_b, "adamw_sharded_b")
    sh_in = adam_own_layout(gparts_in, "w_in", w_in, m_w_in, v_w_in)
    sh_up = adam_own_layout(gparts_up, "w_up", w_up, m_w_up, v_w_up)

    dg_alog = da_heads[:, :SSM_HEADS] * a_neg
    repl_g = [dg_mix_pre, dg_mix_post, dg_ffn_pre, dg_ffn_post, dg_q, dg_kv, dg_attn_out, db_sconv,
              d_dt_bias[:, :SSM_HEADS], dg_alog, dd_heads[:, :SSM_HEADS], dg_ssm,
              jnp.concatenate([dbc_g, dbc_v], axis=1)]
    loss_vec = loss_part[:, :1]
    small_total = _round_up(sum(-(-int(np.prod(a.shape)) // PACK_W) for a in repl_g) + 1, 16)
    spack = _pack(repl_g + [loss_vec], 0, small_total, f32)
    gparts_meta, sparts = _exchange_tail([meta_blocks], [spack], "exchange_tail")
    sh_meta = adam_group(gparts_meta, grp_meta, "adamw_meta")
    loss_row, repl_out = _adamw_replicated(sparts, repl_w, repl_m, repl_v)
    loss = loss_row[0, 0]

    order = ["meta_tokens", "norm_mix_pre", "norm_mix_post", "norm_ffn_pre", "norm_ffn_post", "w_in", "q_a_norm",
             "w_uq", "kv_a_norm", "w_ukv", "attn_out_norm", "ssm_conv_w", "ssm_conv_b", "ssm_dt_bias", "ssm_A_log",
             "ssm_D", "ssm_norm", "w_out", "w_up", "ffn_conv_w", "ffn_conv_b", "w_down"]
    rp_names = ["norm_mix_pre", "norm_mix_post", "norm_ffn_pre", "norm_ffn_post", "q_a_norm", "kv_a_norm",
                "attn_out_norm", "ssm_conv_b", "ssm_dt_bias", "ssm_A_log", "ssm_D", "ssm_norm", "ffn_conv_b"]

    def lookup(k):
        d = {**sh_a[k], **sh_b[k], **sh_in[k], **sh_up[k], **sh_meta[k],
             **{n: four[k] for n, four in zip(rp_names, repl_out)}}
        return [d[n] for n in order]

    return (loss, grad_x, *lookup(0), *lookup(1), *lookup(2), *lookup(3))
```

```python
import math

import jax
import jax.numpy as jnp
import numpy as np
from jax import lax
from jax.experimental import pallas as pl
from jax.experimental.pallas import tpu as pltpu

f32 = jnp.float32
bf16 = jnp.bfloat16

D_MODEL = 1024
N_META = 16
MLA_HEADS = 8
QK_NOPE = 128
QK_ROPE = 64
V_DIM = 128
Q_RANK = 384
KV_RANK = 256
ROPE_THETA = 10000.0
SOFTMAX_SCALE = (QK_NOPE + QK_ROPE) ** -0.5
D_ATTN = MLA_HEADS * V_DIM
SSM_HEADS = 16
SSM_P = 64
SSM_GROUPS = 2
SSM_HPG = SSM_HEADS // SSM_GROUPS
SSM_N = 128
SSM_CONV = 4
CHUNK = 128
D_SSM = SSM_HEADS * SSM_P
D_BC = SSM_GROUPS * SSM_N
D_XBC = D_SSM + 2 * D_BC
D_FF = 2816
FFN_CONV = 3
EPS = 1e-6
QK_PAD = 256
N_DEV = 8

ADAM_LR = 0.001
ADAM_B1 = 0.9
ADAM_B2 = 0.999
ADAM_EPS = 1e-08
ADAM_WD = 0.01
ADAM_STEP = 10

LANES = 128
SUBLANES = 8
ROW_TILE = 256
VMEM_LIMIT = 56 * 1024 * 1024
PACK_W = 1024
PACK_ROW_TILE = 128
NEG = -1e30
LOG2E = math.log2(math.e)
LN2 = math.log(2.0)
Q_PRESCALE = SOFTMAX_SCALE * LOG2E

_MESH = pl.DeviceIdType.MESH


def _pick(n, prefs):
    for p in prefs:
        if n % p == 0:
            return p
    return n


def _rt(m):
    return _pick(m, (384, ROW_TILE))


def _cparams(sem):
    return pltpu.CompilerParams(dimension_semantics=sem, vmem_limit_bytes=VMEM_LIMIT)


def _row(spec_cols, tm):
    return pl.BlockSpec((tm, spec_cols), lambda i: (i, 0))


def _full(shape):
    nd = len(shape)
    return pl.BlockSpec(shape, lambda *a: (0,) * nd)


def _sigmoid(x):
    return 1.0 / (1.0 + jnp.exp(-x))


def _silu(x):
    return x * _sigmoid(x)


def _dsilu(x):
    s = _sigmoid(x)
    return s * (1.0 + x * (1.0 - s))


def _dot(a, b):
    return jnp.dot(a, b, preferred_element_type=f32)


def _dot_nt(a, b):
    return lax.dot_general(a, b, (((1,), (1,)), ((), ())), preferred_element_type=f32)


def _dot_tn(a, b):
    return lax.dot_general(a, b, (((0,), (0,)), ((), ())), preferred_element_type=f32)


def _dot_hi(a, b):
    return jnp.dot(a, b, precision=lax.Precision.HIGHEST, preferred_element_type=f32)


def _mm(a, b, trans_b, name):
    M, K = a.shape
    N = b.shape[0] if trans_b else b.shape[1]
    tm = _pick(M, (768, 512, 256))
    tn = _pick(N, (1408, 512, 384, 256, 128))

    def body(a_ref, b_ref, o_ref):
        o_ref[...] = _dot_nt(a_ref[...], b_ref[...]) if trans_b else _dot(a_ref[...], b_ref[...])

    b_spec = pl.BlockSpec((tn, K), lambda i, j: (j, 0)) if trans_b else pl.BlockSpec((K, tn), lambda i, j: (0, j))
    return pl.pallas_call(
        body, name=name, grid=(M // tm, N // tn),
        in_specs=[pl.BlockSpec((tm, K), lambda i, j: (i, 0)), b_spec],
        out_specs=pl.BlockSpec((tm, tn), lambda i, j: (i, j)),
        out_shape=jax.ShapeDtypeStruct((M, N), f32),
        compiler_params=_cparams(("parallel", "parallel")),
    )(a, b)


def _mm_tn(a, g, name):
    M, K = a.shape
    N = g.shape[1]
    tm = _pick(M, (768, 512, 256))
    tk = _pick(K, (1024, 1408, 512, 384, 256))
    tn = _pick(N, (1024, 1408, 512, 384, 256, 128))

    def body(a_ref, g_ref, o_ref):
        @pl.when(pl.program_id(2) == 0)
        def _():
            o_ref[...] = jnp.zeros_like(o_ref)

        o_ref[...] += _dot_tn(a_ref[...].astype(bf16), g_ref[...].astype(bf16))

    return pl.pallas_call(
        body, name=name, grid=(K // tk, N // tn, M // tm),
        in_specs=[pl.BlockSpec((tm, tk), lambda k, j, m: (m, k)),
                  pl.BlockSpec((tm, tn), lambda k, j, m: (m, j))],
        out_specs=pl.BlockSpec((tk, tn), lambda k, j, m: (k, j)),
        out_shape=jax.ShapeDtypeStruct((K, N), f32),
        compiler_params=_cparams(("parallel", "parallel", "arbitrary")),
    )(a, g)


def _rstd(x):
    return lax.rsqrt(jnp.mean(x * x, axis=-1, keepdims=True) + EPS)


def _rms_bwd_math(x, g, dy):
    r = _rstd(x)
    xh = x * r
    dn = dy * g
    dx = r * (dn - xh * jnp.mean(dn * xh, axis=-1, keepdims=True))
    return dx, dy * xh


def _in_proj(h0, g, weights):
    M, K = h0.shape
    tm = _rt(M)
    n = len(weights)
    widths = [int(w.shape[1]) for w in weights]

    def body(x_ref, g_ref, *refs):
        xv = x_ref[...]
        hn = (xv * _rstd(xv) * g_ref[...]).astype(bf16)
        refs[n][...] = hn
        for p in range(n):
            refs[n + 1 + p][...] = _dot(hn, refs[p][...])

    return pl.pallas_call(
        body, name="in_proj", grid=(M // tm,),
        in_specs=[_row(K, tm), _full((1, K))] + [_full((K, wd)) for wd in widths],
        out_specs=[_row(K, tm)] + [_row(wd, tm) for wd in widths],
        out_shape=[jax.ShapeDtypeStruct((M, K), bf16)] + [jax.ShapeDtypeStruct((M, wd), f32) for wd in widths],
        compiler_params=_cparams(("parallel",)),
    )(h0, g, *weights)


def _in_proj_dw(hn, grads):
    M, K = hn.shape
    tm = _rt(M)
    n = len(grads)
    widths = [int(gr.shape[1]) for gr in grads]

    def body(a_ref, *refs):
        @pl.when(pl.program_id(0) == 0)
        def _():
            for p in range(n):
                refs[n + p][...] = jnp.zeros_like(refs[n + p])

        a = a_ref[...]
        for p in range(n):
            refs[n + p][...] += _dot_tn(a, refs[p][...].astype(bf16))

    return pl.pallas_call(
        body, name="in_proj_dw", grid=(M // tm,),
        in_specs=[_row(K, tm)] + [_row(wd, tm) for wd in widths],
        out_specs=[_full((K, wd)) for wd in widths],
        out_shape=[jax.ShapeDtypeStruct((K, wd), f32) for wd in widths],
        compiler_params=_cparams(("arbitrary",)),
    )(hn, *grads)


def _in_proj_dx(grads, weights, h0, g, dh1, carried, scatter):
    M, K = h0.shape
    tm = _rt(M)
    nt = M // tm
    n = len(grads)
    nx = len(carried)
    widths = [int(w.shape[1]) for w in weights]

    def body(*refs):
        g_refs, w_refs = refs[:n], refs[n:2 * n]
        x_ref, gain_ref, r_ref = refs[2 * n:2 * n + 3]
        cin = refs[2 * n + 3:2 * n + 3 + nx]
        dx_ref, dg_ref = refs[2 * n + 3 + nx:2 * n + 5 + nx]
        cout = refs[2 * n + 5 + nx:2 * n + 5 + 2 * nx]
        i = pl.program_id(0)
        _hosted_exchange(cin, cout, refs[2 * n + 5 + 2 * nx:], scatter, i == 0, i == nt - 1)

        @pl.when(i == 0)
        def _():
            dg_ref[...] = jnp.zeros_like(dg_ref)

        d_hn = None
        for p in range(n):
            t = _dot_nt(g_refs[p][...].astype(bf16), w_refs[p][...])
            d_hn = t if d_hn is None else d_hn + t
        dx, dgp = _rms_bwd_math(x_ref[...], gain_ref[...], d_hn)
        dx_ref[...] = dx + r_ref[...]
        dg_ref[...] += jnp.sum(dgp, axis=0, keepdims=True)

    any_spec = pl.BlockSpec(memory_space=pl.ANY)
    return pl.pallas_call(
        body, name="in_proj_dx", grid=(nt,),
        in_specs=([_row(wd, tm) for wd in widths] + [_full((K, wd)) for wd in widths]
                  + [_row(K, tm), _full((1, K)), _row(K, tm)] + [any_spec] * nx),
        out_specs=[_row(K, tm), _full((1, K))] + [any_spec] * nx,
        out_shape=[jax.ShapeDtypeStruct((M, K), f32), jax.ShapeDtypeStruct((1, K), f32)]
        + _exchange_shapes(carried, scatter),
        scratch_shapes=_exchange_sems(nx),
        compiler_params=_cparams(("arbitrary",)),
    )(*grads, *weights, h0, g, dh1, *carried)


def _out_proj_resid(attn, ga, ssm, wa, ws, h0, g2, g3):
    M, K = h0.shape
    tm = _rt(M)

    def body(o_ref, ga_ref, s_ref, wa_ref, ws_ref, h_ref, g2_ref, g3_ref, an_ref, m_ref, h1_ref, hn_ref):
        ov = o_ref[...]
        an = (ov * _rstd(ov) * ga_ref[...]).astype(bf16)
        an_ref[...] = an
        mv = _dot(an, wa_ref[...]) + _dot(s_ref[...], ws_ref[...])
        m_ref[...] = mv
        h1 = h_ref[...] + mv * _rstd(mv) * g2_ref[...]
        h1_ref[...] = h1
        hn_ref[...] = (h1 * _rstd(h1) * g3_ref[...]).astype(bf16)

    return pl.pallas_call(
        body, name="out_proj_resid", grid=(M // tm,),
        in_specs=[_row(attn.shape[1], tm), _full((1, attn.shape[1])), _row(ssm.shape[1], tm),
                  _full(wa.shape), _full(ws.shape), _row(K, tm), _full((1, K)), _full((1, K))],
        out_specs=[_row(attn.shape[1], tm), _row(K, tm), _row(K, tm), _row(K, tm)],
        out_shape=[jax.ShapeDtypeStruct(attn.shape, bf16), jax.ShapeDtypeStruct((M, K), f32),
                   jax.ShapeDtypeStruct((M, K), f32), jax.ShapeDtypeStruct((M, K), bf16)],
        compiler_params=_cparams(("parallel",)),
    )(attn, ga, ssm, wa, ws, h0, g2, g3)


def _final(h1, act, wdown, g4, tgt, n_real):
    M, K = h1.shape
    F = act.shape[1]
    tm = _rt(M)
    nt = M // tm

    def body(h_ref, a_ref, w_ref, g_ref, t_ref, dh_ref, dd_ref, dg_ref, ls_ref, acc_ref):
        i = pl.program_id(0)

        @pl.when(i == 0)
        def _():
            dg_ref[...] = jnp.zeros_like(dg_ref)
            acc_ref[...] = jnp.zeros_like(acc_ref)

        dv = _dot(a_ref[...], w_ref[...])
        g = g_ref[...]
        r = _rstd(dv)
        n = dv * r
        h2 = h_ref[...] + n * g
        rows = i * tm + lax.broadcasted_iota(jnp.int32, (tm, 1), 0)
        mask = ((rows >= N_META) & (rows < n_real)).astype(f32)
        diff = (h2 - t_ref[...]) * mask
        acc_ref[...] += jnp.sum(diff * diff, axis=0, keepdims=True)
        dh = diff * (1.0 / K)
        dh_ref[...] = dh
        dn = dh * g
        dd_ref[...] = (r * (dn - n * jnp.mean(dn * n, axis=-1, keepdims=True))).astype(bf16)
        dg_ref[...] += jnp.sum(dh * n, axis=0, keepdims=True)

        @pl.when(i == nt - 1)
        def _():
            ls_ref[...] = jnp.zeros((1, LANES), f32) + jnp.sum(acc_ref[...]) * (0.5 / K)

    return pl.pallas_call(
        body, name="ffn_down_loss", grid=(nt,),
        in_specs=[_row(K, tm), _row(F, tm), _full((F, K)), _full((1, K)), _row(K, tm)],
        out_specs=[_row(K, tm), _row(K, tm), _full((1, K)), _full((1, LANES))],
        out_shape=[jax.ShapeDtypeStruct((M, K), f32), jax.ShapeDtypeStruct((M, K), bf16),
                   jax.ShapeDtypeStruct((1, K), f32), jax.ShapeDtypeStruct((1, LANES), f32)],
        scratch_shapes=[pltpu.VMEM((1, K), f32)],
        compiler_params=_cparams(("arbitrary",)),
    )(h1, act, wdown, g4, tgt)


def _mid_bwd(h1, g3, dup_g, dup_v, wup, dh2, mix, g2):
    M, K = h1.shape
    F = dup_g.shape[1]
    tm = ROW_TILE

    def body(h_ref, g3_ref, ag_ref, av_ref, w_ref, dh2_ref, m_ref, g2_ref, dh1_ref, dm_ref, dg3_ref, dg2_ref):
        @pl.when(pl.program_id(0) == 0)
        def _():
            dg3_ref[...] = jnp.zeros_like(dg3_ref)
            dg2_ref[...] = jnp.zeros_like(dg2_ref)

        d_hn2 = _dot_nt(ag_ref[...], w_ref[:, 0:F]) + _dot_nt(av_ref[...], w_ref[:, F:2 * F])
        dx, dgp = _rms_bwd_math(h_ref[...], g3_ref[...], d_hn2)
        dh1 = dh2_ref[...] + dx
        dh1_ref[...] = dh1
        dg3_ref[...] += jnp.sum(dgp, axis=0, keepdims=True)
        dm, dgp2 = _rms_bwd_math(m_ref[...], g2_ref[...], dh1)
        dm_ref[...] = dm.astype(bf16)
        dg2_ref[...] += jnp.sum(dgp2, axis=0, keepdims=True)

    return pl.pallas_call(
        body, name="ffn_up_dx_mid_bwd", grid=(M // tm,),
        in_specs=[_row(K, tm), _full((1, K)), _row(F, tm), _row(F, tm), _full((K, 2 * F)), _row(K, tm),
                  _row(K, tm), _full((1, K))],
        out_specs=[_row(K, tm), _row(K, tm), _full((1, K)), _full((1, K))],
        out_shape=[jax.ShapeDtypeStruct((M, K), f32), jax.ShapeDtypeStruct((M, K), bf16),
                   jax.ShapeDtypeStruct((1, K), f32), jax.ShapeDtypeStruct((1, K), f32)],
        compiler_params=_cparams(("arbitrary",)),
    )(h1, g3, dup_g, dup_v, wup, dh2, mix, g2)


HEADS_PER_STEP = 4
CONV_RB = 16


def _conv_block_taps(x_ref, halo, rb, lanes, kw):
    r0 = rb * CONV_RB
    if rb == 0:
        cat = jnp.concatenate([halo, x_ref[0:CONV_RB, lanes]], axis=0)
        first = SUBLANES - (kw - 1)
        return [cat[first + k:first + k + CONV_RB] for k in range(kw)]
    return [x_ref[r0 - (kw - 1) + k:r0 - (kw - 1) + k + CONV_RB, lanes] for k in range(kw)]


def _conv_weighted(taps, w, kw):
    u = None
    for k in range(kw):
        t = taps[k] * w[k:k + 1, :]
        u = t if u is None else u + t
    return u


def _conv_block_dx(du, nxt, w, kw):
    cat = jnp.concatenate([du, nxt], axis=0)
    return _conv_weighted([cat[kw - 1 - k:kw - 1 - k + CONV_RB] for k in range(kw)], w, kw)


def _prev_spec(tm, tc, col_of, row_axis, reversed_tiles=0):
    def imap(*ids):
        i = ids[row_axis]
        if reversed_tiles:
            i = reversed_tiles - 1 - i
        return (jnp.maximum(i * (tm // SUBLANES) - 1, 0), col_of(*ids))
    return pl.BlockSpec((SUBLANES, tc), imap)


def _ssm_conv_fwd(xbc, w, b):
    M, C = xbc.shape
    tm, tc, kw = ROW_TILE, C, SSM_CONV

    def body(x_ref, h_ref, w_ref, b_ref, o_ref):
        i = pl.program_id(0)

        def chunk(j, carry):
            lanes = pl.ds(pl.multiple_of(j * LANES, LANES), LANES)
            halo = jnp.where(i == 0, 0.0, h_ref[:, lanes])
            wv = w_ref[:, lanes]
            bv = b_ref[:, lanes]
            for rb in range(tm // CONV_RB):
                u = _conv_weighted(_conv_block_taps(x_ref, halo, rb, lanes, kw), wv, kw) + bv
                o_ref[rb * CONV_RB:(rb + 1) * CONV_RB, lanes] = _silu(u)
            return carry

        lax.fori_loop(0, tc // LANES, chunk, 0)

    return pl.pallas_call(
        body, name="ssm_conv_fwd", grid=(M // tm, C // tc),
        in_specs=[pl.BlockSpec((tm, tc), lambda i, j: (i, j)),
                  _prev_spec(tm, tc, lambda i, j: j, 0),
                  pl.BlockSpec((SUBLANES, tc), lambda i, j: (0, j)),
                  pl.BlockSpec((1, tc), lambda i, j: (0, j))],
        out_specs=pl.BlockSpec((tm, tc), lambda i, j: (i, j)),
        out_shape=jax.ShapeDtypeStruct((M, C), f32),
        compiler_params=_cparams(("parallel", "parallel")),
    )(xbc, xbc, w, b)


def _ssm_conv_bwd(xbc, w, b, dout):
    M, C = xbc.shape
    tm, tc, kw = ROW_TILE, C // 3, SSM_CONV
    nt = M // tm

    def body(x_ref, h_ref, w_ref, b_ref, d_ref, dx_ref, dw_ref, db_ref, nxt_ref):
        i = pl.program_id(1)

        @pl.when(i == 0)
        def _():
            dw_ref[...] = jnp.zeros_like(dw_ref)
            db_ref[...] = jnp.zeros_like(db_ref)
            nxt_ref[...] = jnp.zeros_like(nxt_ref)

        def chunk(j, carry):
            lanes = pl.ds(pl.multiple_of(j * LANES, LANES), LANES)
            halo = jnp.where(i == nt - 1, 0.0, h_ref[:, lanes])
            wv = w_ref[:, lanes]
            bv = b_ref[:, lanes]
            nxt = nxt_ref[:, lanes]
            db = jnp.zeros((CONV_RB, LANES), f32)
            dw = [jnp.zeros((CONV_RB, LANES), f32) for _ in range(kw)]
            for rb in reversed(range(tm // CONV_RB)):
                rows = slice(rb * CONV_RB, (rb + 1) * CONV_RB)
                taps = _conv_block_taps(x_ref, halo, rb, lanes, kw)
                du = d_ref[rows, lanes] * _dsilu(_conv_weighted(taps, wv, kw) + bv)
                db = db + du
                dw = [dw[k] + du * taps[k] for k in range(kw)]
                dx_ref[rows, lanes] = _conv_block_dx(du, nxt, wv, kw).astype(bf16)
                nxt = du[0:SUBLANES]
            nxt_ref[:, lanes] = nxt
            db_ref[:, lanes] += jnp.sum(db, axis=0, keepdims=True)
            for k in range(kw):
                dw_ref[k:k + 1, lanes] += jnp.sum(dw[k], axis=0, keepdims=True)
            return carry

        lax.fori_loop(0, tc // LANES, chunk, 0)

    tile = pl.BlockSpec((tm, tc), lambda j, i: (nt - 1 - i, j))
    return pl.pallas_call(
        body, name="ssm_conv_bwd", grid=(C // tc, nt),
        in_specs=[tile, _prev_spec(tm, tc, lambda j, i: j, 1, nt),
                  pl.BlockSpec((SUBLANES, tc), lambda j, i: (0, j)),
                  pl.BlockSpec((1, tc), lambda j, i: (0, j)), tile],
        out_specs=[tile, pl.BlockSpec((SUBLANES, tc), lambda j, i: (0, j)),
                   pl.BlockSpec((1, tc), lambda j, i: (0, j))],
        out_shape=[jax.ShapeDtypeStruct((M, C), bf16), jax.ShapeDtypeStruct((SUBLANES, C), f32),
                   jax.ShapeDtypeStruct((1, C), f32)],
        scratch_shapes=[pltpu.VMEM((SUBLANES, tc), f32)],
        compiler_params=_cparams(("parallel", "arbitrary")),
    )(xbc, xbc, w, b, dout)


def _ffn_gate_fwd(up, w, b):
    M = up.shape[0]
    tm, tc, kw = ROW_TILE, D_FF // 2, FFN_CONV
    nc = D_FF // tc

    def body(xg_ref, hg_ref, xv_ref, hv_ref, wg_ref, wv_ref, bg_ref, bv_ref, o_ref):
        i = pl.program_id(0)

        def chunk(j, carry):
            lanes = pl.ds(pl.multiple_of(j * LANES, LANES), LANES)
            halo_g = jnp.where(i == 0, 0.0, hg_ref[:, lanes])
            halo_v = jnp.where(i == 0, 0.0, hv_ref[:, lanes])
            wg, wv = wg_ref[:, lanes], wv_ref[:, lanes]
            bg, bv = bg_ref[:, lanes], bv_ref[:, lanes]
            for rb in range(tm // CONV_RB):
                ug = _conv_weighted(_conv_block_taps(xg_ref, halo_g, rb, lanes, kw), wg, kw) + bg
                uv = _conv_weighted(_conv_block_taps(xv_ref, halo_v, rb, lanes, kw), wv, kw) + bv
                o_ref[rb * CONV_RB:(rb + 1) * CONV_RB, lanes] = (_silu(ug) * uv).astype(bf16)
            return carry

        lax.fori_loop(0, tc // LANES, chunk, 0)

    return pl.pallas_call(
        body, name="ffn_gate_fwd", grid=(M // tm, nc),
        in_specs=[pl.BlockSpec((tm, tc), lambda i, j: (i, j)),
                  _prev_spec(tm, tc, lambda i, j: j, 0),
                  pl.BlockSpec((tm, tc), lambda i, j: (i, j + nc)),
                  _prev_spec(tm, tc, lambda i, j: j + nc, 0),
                  pl.BlockSpec((SUBLANES, tc), lambda i, j: (0, j)),
                  pl.BlockSpec((SUBLANES, tc), lambda i, j: (0, j + nc)),
                  pl.BlockSpec((1, tc), lambda i, j: (0, j)),
                  pl.BlockSpec((1, tc), lambda i, j: (0, j + nc))],
        out_specs=pl.BlockSpec((tm, tc), lambda i, j: (i, j)),
        out_shape=jax.ShapeDtypeStruct((M, D_FF), bf16),
        compiler_params=_cparams(("parallel", "parallel")),
    )(up, up, up, up, w, w, b, b)


def _ffn_gate_bwd(up, w, b, d_down, wdown):
    M = up.shape[0]
    K = d_down.shape[1]
    tm, tc, kw = ROW_TILE, D_FF // 2, FFN_CONV
    nc = D_FF // tc
    nt = M // tm

    def body(xg_ref, hg_ref, xv_ref, hv_ref, wg_ref, wv_ref, bg_ref, bv_ref, dd_ref, wd_ref,
             dxg_ref, dxv_ref, dwg_ref, dwv_ref, dbg_ref, dbv_ref, ng_ref, nv_ref, d_ref):
        i = pl.program_id(1)

        @pl.when(i == 0)
        def _():
            for r in (dwg_ref, dwv_ref, dbg_ref, dbv_ref, ng_ref, nv_ref):
                r[...] = jnp.zeros_like(r)

        d_ref[...] = _dot_nt(dd_ref[...], wd_ref[...])

        def chunk(j, carry):
            lanes = pl.ds(pl.multiple_of(j * LANES, LANES), LANES)
            halo_g = jnp.where(i == nt - 1, 0.0, hg_ref[:, lanes])
            halo_v = jnp.where(i == nt - 1, 0.0, hv_ref[:, lanes])
            wg, wv = wg_ref[:, lanes], wv_ref[:, lanes]
            bg, bv = bg_ref[:, lanes], bv_ref[:, lanes]
            nxt_g, nxt_v = ng_ref[:, lanes], nv_ref[:, lanes]
            zero = jnp.zeros((CONV_RB, LANES), f32)
            dbg, dbv = zero, zero
            dwg = [zero for _ in range(kw)]
            dwv = [zero for _ in range(kw)]
            for rb in reversed(range(tm // CONV_RB)):
                rows = slice(rb * CONV_RB, (rb + 1) * CONV_RB)
                tg = _conv_block_taps(xg_ref, halo_g, rb, lanes, kw)
                tv = _conv_block_taps(xv_ref, halo_v, rb, lanes, kw)
                ug = _conv_weighted(tg, wg, kw) + bg
                uv = _conv_weighted(tv, wv, kw) + bv
                sg = _sigmoid(ug)
                da = d_ref[rows, lanes]
                dug = da * uv * (sg * (1.0 + ug * (1.0 - sg)))
                duv = da * (ug * sg)
                dbg = dbg + dug
                dbv = dbv + duv
                dwg = [dwg[k] + dug * tg[k] for k in range(kw)]
                dwv = [dwv[k] + duv * tv[k] for k in range(kw)]
                dxg_ref[rows, lanes] = _conv_block_dx(dug, nxt_g, wg, kw).astype(bf16)
                dxv_ref[rows, lanes] = _conv_block_dx(duv, nxt_v, wv, kw).astype(bf16)
                nxt_g, nxt_v = dug[0:SUBLANES], duv[0:SUBLANES]
            ng_ref[:, lanes] = nxt_g
            nv_ref[:, lanes] = nxt_v
            dbg_ref[:, lanes] += jnp.sum(dbg, axis=0, keepdims=True)
            dbv_ref[:, lanes] += jnp.sum(dbv, axis=0, keepdims=True)
            for k in range(kw):
                dwg_ref[k:k + 1, lanes] += jnp.sum(dwg[k], axis=0, keepdims=True)
                dwv_ref[k:k + 1, lanes] += jnp.sum(dwv[k], axis=0, keepdims=True)
            return carry

        lax.fori_loop(0, tc // LANES, chunk, 0)

    tile_g = pl.BlockSpec((tm, tc), lambda j, i: (nt - 1 - i, j))
    tile_v = pl.BlockSpec((tm, tc), lambda j, i: (nt - 1 - i, j + nc))
    ext = pltpu.VMEM((SUBLANES, tc), f32)
    return pl.pallas_call(
        body, name="ffn_gate_bwd", grid=(nc, nt),
        in_specs=[tile_g, _prev_spec(tm, tc, lambda j, i: j, 1, nt),
                  tile_v, _prev_spec(tm, tc, lambda j, i: j + nc, 1, nt),
                  pl.BlockSpec((SUBLANES, tc), lambda j, i: (0, j)),
                  pl.BlockSpec((SUBLANES, tc), lambda j, i: (0, j + nc)),
                  pl.BlockSpec((1, tc), lambda j, i: (0, j)),
                  pl.BlockSpec((1, tc), lambda j, i: (0, j + nc)),
                  pl.BlockSpec((tm, K), lambda j, i: (nt - 1 - i, 0)),
                  pl.BlockSpec((tc, K), lambda j, i: (j, 0))],
        out_specs=[tile_g, tile_g,
                   pl.BlockSpec((SUBLANES, tc), lambda j, i: (0, j)),
                   pl.BlockSpec((SUBLANES, tc), lambda j, i: (0, j)),
                   pl.BlockSpec((1, tc), lambda j, i: (0, j)),
                   pl.BlockSpec((1, tc), lambda j, i: (0, j))],
        out_shape=[jax.ShapeDtypeStruct((M, D_FF), bf16), jax.ShapeDtypeStruct((M, D_FF), bf16),
                   jax.ShapeDtypeStruct((SUBLANES, D_FF), f32), jax.ShapeDtypeStruct((SUBLANES, D_FF), f32),
                   jax.ShapeDtypeStruct((1, D_FF), f32), jax.ShapeDtypeStruct((1, D_FF), f32)],
        scratch_shapes=[ext, ext, pltpu.VMEM((tm, tc), f32)],
        compiler_params=_cparams(("parallel", "arbitrary")),
    )(up, up, up, up, w, w, b, b, d_down, wdown)


def _rope_apply(blk, cos, sin):
    lane = lax.broadcasted_iota(jnp.int32, blk.shape, 1)
    half = QK_ROPE // 2
    partner = jnp.where(lane < half, pltpu.roll(blk, LANES - half, 1), pltpu.roll(blk, half, 1))
    return blk * cos + partner * sin


def _rope_unapply(d, cos, sin):
    t = d * sin
    lane = lax.broadcasted_iota(jnp.int32, d.shape, 1)
    half = QK_ROPE // 2
    partner = jnp.where(lane < half, pltpu.roll(t, LANES - half, 1), pltpu.roll(t, half, 1))
    return d * cos + partner


def _up_q_rope(q_c, g, wuq, cos, sin):
    M, K = q_c.shape
    tm = _pick(M, (768, 512, 256))

    hs = HEADS_PER_STEP

    def body(x_ref, g_ref, b_ref, c_ref, s_ref, a_ref, o_ref):
        xv = x_ref[...]
        a = (xv * _rstd(xv) * g_ref[...]).astype(bf16)
        a_ref[...] = a
        r = _dot(a, b_ref[...]) * Q_PRESCALE
        c, s = c_ref[...], s_ref[...]
        for u in range(hs):
            o_ref[u, :, 0:QK_NOPE] = r[:, u * QK_PAD:u * QK_PAD + QK_NOPE].astype(bf16)
            o_ref[u, :, QK_NOPE:QK_PAD] = _rope_apply(r[:, u * QK_PAD + QK_NOPE:(u + 1) * QK_PAD], c, s).astype(bf16)

    return pl.pallas_call(
        body, name="up_q_rope", grid=(M // tm, MLA_HEADS // hs),
        in_specs=[pl.BlockSpec((tm, K), lambda i, h: (i, 0)),
                  pl.BlockSpec((1, K), lambda i, h: (0, 0)),
                  pl.BlockSpec((K, hs * QK_PAD), lambda i, h: (0, h)),
                  pl.BlockSpec((tm, LANES), lambda i, h: (i, 0)),
                  pl.BlockSpec((tm, LANES), lambda i, h: (i, 0))],
        out_specs=[pl.BlockSpec((tm, K), lambda i, h: (i, 0)),
                   pl.BlockSpec((hs, tm, QK_PAD), lambda i, h: (h, i, 0))],
        out_shape=[jax.ShapeDtypeStruct((M, K), bf16), jax.ShapeDtypeStruct((MLA_HEADS, M, QK_PAD), bf16)],
        compiler_params=_cparams(("parallel", "arbitrary")),
    )(q_c, g, wuq, cos, sin)


def _up_kv_rope(kv_c, g, wukv, kpe_raw, cos, sin):
    M, K = kv_c.shape
    tm = _pick(M, (768, 512, 256))

    hs = HEADS_PER_STEP
    w = QK_NOPE + V_DIM

    def body(x_ref, g_ref, b_ref, pe_ref, c_ref, s_ref, a_ref, k_ref, v_ref):
        xv = x_ref[...]
        a = (xv * _rstd(xv) * g_ref[...]).astype(bf16)
        a_ref[...] = a
        r = _dot(a, b_ref[...])
        pe = _rope_apply(pe_ref[...], c_ref[...], s_ref[...]).astype(bf16)
        for u in range(hs):
            k_ref[u, :, 0:QK_NOPE] = r[:, u * w:u * w + QK_NOPE].astype(bf16)
            k_ref[u, :, QK_NOPE:QK_PAD] = pe
            v_ref[u] = r[:, u * w + QK_NOPE:(u + 1) * w].astype(bf16)

    return pl.pallas_call(
        body, name="up_kv_rope", grid=(M // tm, MLA_HEADS // hs),
        in_specs=[pl.BlockSpec((tm, K), lambda i, h: (i, 0)),
                  pl.BlockSpec((1, K), lambda i, h: (0, 0)),
                  pl.BlockSpec((K, hs * w), lambda i, h: (0, h)),
                  pl.BlockSpec((tm, LANES), lambda i, h: (i, 0)),
                  pl.BlockSpec((tm, LANES), lambda i, h: (i, 0)),
                  pl.BlockSpec((tm, LANES), lambda i, h: (i, 0))],
        out_specs=[pl.BlockSpec((tm, K), lambda i, h: (i, 0)),
                   pl.BlockSpec((hs, tm, QK_PAD), lambda i, h: (h, i, 0)),
                   pl.BlockSpec((hs, tm, V_DIM), lambda i, h: (h, i, 0))],
        out_shape=[jax.ShapeDtypeStruct((M, K), bf16), jax.ShapeDtypeStruct((MLA_HEADS, M, QK_PAD), bf16),
                   jax.ShapeDtypeStruct((MLA_HEADS, M, V_DIM), bf16)],
        compiler_params=_cparams(("parallel", "arbitrary")),
    )(kv_c, g, wukv, kpe_raw, cos, sin)


def _latent_bwd(d_full_sc, w_ref, x_ref, g_ref, a_ref, dx_ref, dg_ref, dw_ref):
    d_full = d_full_sc[...]
    dx, dgp = _rms_bwd_math(x_ref[...], g_ref[...], _dot_nt(d_full, w_ref[...]))
    dx_ref[...] = dx.astype(bf16)
    dg_ref[...] += jnp.sum(dgp, axis=0, keepdims=True)
    dw_ref[...] += _dot_tn(a_ref[...], d_full)


def _latent_bwd_call(body, name, head_inputs, head_specs, cos, sin, w, x, g, a, extra_out_specs, extra_out_shape):
    M, K = x.shape
    tm = _rt(M)
    N = w.shape[1]
    return pl.pallas_call(
        body, name=name, grid=(M // tm,),
        in_specs=head_specs + [_row(LANES, tm), _row(LANES, tm), _full((K, N)), _row(K, tm), _full((1, K)),
                               _row(K, tm)],
        out_specs=[_row(K, tm), _full((1, K)), _full((K, N))] + extra_out_specs,
        out_shape=[jax.ShapeDtypeStruct((M, K), bf16), jax.ShapeDtypeStruct((1, K), f32),
                   jax.ShapeDtypeStruct((K, N), f32)] + extra_out_shape,
        scratch_shapes=[pltpu.VMEM((tm, N), bf16)],
        compiler_params=_cparams(("arbitrary",)),
    )(*head_inputs, cos, sin, w, x, g, a)


def _q_branch_bwd(dq, cos, sin, wuq, q_c, g, qn):
    tm = _rt(q_c.shape[0])

    def body(d_ref, c_ref, s_ref, w_ref, x_ref, g_ref, a_ref, dx_ref, dg_ref, dw_ref, full_sc):
        @pl.when(pl.program_id(0) == 0)
        def _():
            dg_ref[...] = jnp.zeros_like(dg_ref)
            dw_ref[...] = jnp.zeros_like(dw_ref)

        c, s = c_ref[...], s_ref[...]
        for h in range(MLA_HEADS):
            full_sc[:, h * QK_PAD:h * QK_PAD + QK_NOPE] = (d_ref[h, :, 0:QK_NOPE] * SOFTMAX_SCALE).astype(bf16)
            full_sc[:, h * QK_PAD + QK_NOPE:(h + 1) * QK_PAD] = (_rope_unapply(
                d_ref[h, :, QK_NOPE:QK_PAD], c, s) * SOFTMAX_SCALE).astype(bf16)
        _latent_bwd(full_sc, w_ref, x_ref, g_ref, a_ref, dx_ref, dg_ref, dw_ref)

    return _latent_bwd_call(body, "q_branch_bwd", [dq],
                            [pl.BlockSpec((MLA_HEADS, tm, QK_PAD), lambda i: (0, i, 0))],
                            cos, sin, wuq, q_c, g, qn, [], [])


def _kv_branch_bwd(dk, dv, cos, sin, wukv, kv_c, g, kvn):
    M = kv_c.shape[0]
    tm = _rt(M)
    w = QK_NOPE + V_DIM

    def body(dk_ref, dv_ref, c_ref, s_ref, w_ref, x_ref, g_ref, a_ref, dx_ref, dg_ref, dw_ref, pe_ref, full_sc):
        @pl.when(pl.program_id(0) == 0)
        def _():
            dg_ref[...] = jnp.zeros_like(dg_ref)
            dw_ref[...] = jnp.zeros_like(dw_ref)

        pe = None
        for h in range(MLA_HEADS):
            full_sc[:, h * w:h * w + QK_NOPE] = dk_ref[h, :, 0:QK_NOPE].astype(bf16)
            full_sc[:, h * w + QK_NOPE:(h + 1) * w] = dv_ref[h].astype(bf16)
            t = dk_ref[h, :, QK_NOPE:QK_PAD]
            pe = t if pe is None else pe + t
        pe_ref[...] = _rope_unapply(pe, c_ref[...], s_ref[...])
        _latent_bwd(full_sc, w_ref, x_ref, g_ref, a_ref, dx_ref, dg_ref, dw_ref)

    return _latent_bwd_call(body, "kv_branch_bwd", [dk, dv],
                            [pl.BlockSpec((MLA_HEADS, tm, QK_PAD), lambda i: (0, i, 0)),
                             pl.BlockSpec((MLA_HEADS, tm, V_DIM), lambda i: (0, i, 0))],
                            cos, sin, wukv, kv_c, g, kvn, [_row(LANES, tm)],
                            [jax.ShapeDtypeStruct((M, LANES), f32)])


def _attn_tile(M):
    return 768 if (M % 768 == 0 and M >= 4 * 768) else ROW_TILE


def _col_to_row(col):
    return col.T[0:1, :]


def _hosted_exchange(refs_in, refs_out, sems, scatter, first, last):
    copies = _exchange_copies(refs_in, refs_out, *sems, scatter)

    @pl.when(first)
    def _():
        for cp in copies:
            cp.start()

    @pl.when(last)
    def _():
        for cp in copies:
            cp.wait()


def _flash_fwd(q, k, v, carried, scatter):
    H, M, _ = q.shape
    T = _attn_tile(M)
    nq = M // T
    nx = len(carried)

    def body(*refs):
        q_ref, k_ref, v_ref = refs[:3]
        o_ref, lse_ref = refs[3 + nx:5 + nx]
        sa_ref, sb_ref, m_sc, l_sc, acc_sc = refs[5 + 2 * nx:10 + 2 * nx]
        h = pl.program_id(0)
        i = pl.program_id(1)
        _hosted_exchange(refs[3:3 + nx], refs[5 + nx:5 + 2 * nx], refs[10 + 2 * nx:], scatter,
                         (h == 0) & (i == 0), (h == H - 1) & (i == nq - 1))
        qv = q_ref[0]
        m_sc[...] = jnp.full_like(m_sc, NEG)
        l_sc[...] = jnp.zeros_like(l_sc)
        acc_sc[...] = jnp.zeros_like(acc_sc)

        def scores(j, s_ref):
            off = pl.multiple_of(j * T, T)
            s_ref[...] = _dot_nt(qv, k_ref[0, pl.ds(off, T), :])

        def softmax_pv(j, s_ref, masked):
            off = pl.multiple_of(j * T, T)
            s = s_ref[...]
            if masked:
                r = lax.broadcasted_iota(jnp.int32, (T, T), 0)
                c = lax.broadcasted_iota(jnp.int32, (T, T), 1)
                s = jnp.where(r >= c, s, NEG)
            m_prev = m_sc[...]
            m_new = jnp.maximum(m_prev, jnp.max(s, axis=1, keepdims=True))
            alpha = jnp.exp2(m_prev - m_new)
            p = jnp.exp2(s - m_new[:, 0:1])
            l_sc[...] = alpha * l_sc[...] + jnp.sum(p, axis=1, keepdims=True)
            acc_sc[...] = alpha * acc_sc[...] + _dot(p.astype(bf16), v_ref[0, pl.ds(off, T), :])
            m_sc[...] = m_new

        scores(0, sa_ref)

        def pair(jj, c):
            j0 = 2 * jj
            scores(j0 + 1, sb_ref)
            softmax_pv(j0, sa_ref, False)
            scores(j0 + 2, sa_ref)
            softmax_pv(j0 + 1, sb_ref, False)
            return c

        lax.fori_loop(0, i // 2, pair, 0)

        @pl.when(i % 2 == 0)
        def _():
            softmax_pv(i, sa_ref, True)

        @pl.when(i % 2 == 1)
        def _():
            scores(i, sb_ref)
            softmax_pv(i - 1, sa_ref, False)
            softmax_pv(i, sb_ref, True)

        l = l_sc[...]
        o_ref[...] = acc_sc[...] / l
        lse_ref[0, 0] = _col_to_row(m_sc[...] + jnp.log2(l))

    any_spec = pl.BlockSpec(memory_space=pl.ANY)
    return pl.pallas_call(
        body, name="flash_fwd", grid=(H, nq),
        in_specs=[pl.BlockSpec((1, T, QK_PAD), lambda h, i: (h, i, 0)),
                  pl.BlockSpec((1, M, QK_PAD), lambda h, i: (h, 0, 0)),
                  pl.BlockSpec((1, M, V_DIM), lambda h, i: (h, 0, 0))] + [any_spec] * nx,
        out_specs=[pl.BlockSpec((T, V_DIM), lambda h, i: (i, h)),
                   pl.BlockSpec((1, 1, 1, T), lambda h, i: (h, i, 0, 0))] + [any_spec] * nx,
        out_shape=[jax.ShapeDtypeStruct((M, H * V_DIM), f32),
                   jax.ShapeDtypeStruct((H, nq, 1, T), f32)] + _exchange_shapes(carried, scatter),
        scratch_shapes=[pltpu.VMEM((T, T), f32), pltpu.VMEM((T, T), f32),
                        pltpu.VMEM((T, LANES), f32), pltpu.VMEM((T, LANES), f32),
                        pltpu.VMEM((T, V_DIM), f32)] + _exchange_sems(nx),
        compiler_params=_cparams(("arbitrary", "arbitrary")),
    )(q, k, v, *carried)


def _attn_out_bwd(o, g, d_an):
    M, K = o.shape
    H = MLA_HEADS
    T = _attn_tile(M)

    def body(o_ref, g_ref, d_ref, dh_ref, dl_ref, dg_ref):
        @pl.when(pl.program_id(0) == 0)
        def _():
            dg_ref[...] = jnp.zeros_like(dg_ref)

        ov = o_ref[...]
        do, dgp = _rms_bwd_math(ov, g_ref[...], d_ref[...])
        dg_ref[...] += jnp.sum(dgp, axis=0, keepdims=True)
        for h in range(H):
            sl = slice(h * V_DIM, (h + 1) * V_DIM)
            doh = do[:, sl]
            dh_ref[h] = doh.astype(bf16)
            col = jnp.sum(ov[:, sl] * doh, axis=1, keepdims=True) + jnp.zeros((T, LANES), f32)
            dl_ref[h, 0] = _col_to_row(col)

    return pl.pallas_call(
        body, name="attn_out_bwd", grid=(M // T,),
        in_specs=[_row(K, T), _full((1, K)), _row(K, T)],
        out_specs=[pl.BlockSpec((H, T, V_DIM), lambda i: (0, i, 0)),
                   pl.BlockSpec((H, 1, 1, T), lambda i: (0, i, 0, 0)),
                   _full((1, K))],
        out_shape=[jax.ShapeDtypeStruct((H, M, V_DIM), bf16),
                   jax.ShapeDtypeStruct((H, M // T, 1, T), f32),
                   jax.ShapeDtypeStruct((1, K), f32)],
        compiler_params=_cparams(("arbitrary",)),
    )(o, g, d_an)


def _flash_bwd(q, k, v, do, lse, delta, carried, scatter):
    H, M, _ = q.shape
    T = _attn_tile(M)
    nq = M // T
    nx = len(carried)

    def body(*refs):
        q_ref, do_ref, lse_ref, dl_ref, k_ref, v_ref = refs[:6]
        dq_ref, dk_ref, dv_ref = refs[6 + nx:9 + nx]
        dk_sc, dv_sc = refs[9 + 2 * nx:11 + 2 * nx]
        j = pl.program_id(1)
        _hosted_exchange(refs[6:6 + nx], refs[9 + nx:9 + 2 * nx], refs[11 + 2 * nx:], scatter,
                         (pl.program_id(0) == 0) & (j == 0), (pl.program_id(0) == H - 1) & (j == nq - 1))

        @pl.when(j == 0)
        def _():
            dq_ref[...] = jnp.zeros_like(dq_ref)

        kt = k_ref[0]
        vt = v_ref[0]
        dk_sc[...] = jnp.zeros_like(dk_sc)
        dv_sc[...] = jnp.zeros_like(dv_sc)

        def step(i, masked):
            off = pl.multiple_of(i * T, T)
            qt = q_ref[0, pl.ds(off, T), :]
            dot_ = do_ref[0, pl.ds(off, T), :]
            st = _dot_nt(kt, qt)
            if masked:
                r = lax.broadcasted_iota(jnp.int32, (T, T), 0)
                c = lax.broadcasted_iota(jnp.int32, (T, T), 1)
                st = jnp.where(c >= r, st, NEG)
            pt = jnp.exp2(st - lse_ref[0, i])
            dv_sc[...] += _dot(pt.astype(bf16), dot_)
            dpt = _dot_nt(vt, dot_)
            dst = (pt * (dpt - dl_ref[0, i])).astype(bf16)
            dk_sc[...] += _dot(dst, qt)
            dq_ref[0, pl.ds(off, T), :] += _dot_tn(dst, kt)

        step(j, True)

        def loop_body(i, c):
            step(i, False)
            return c

        lax.fori_loop(j + 1, nq, loop_body, 0)
        dk_ref[0] = dk_sc[...] * LN2
        dv_ref[0] = dv_sc[...]

    any_spec = pl.BlockSpec(memory_space=pl.ANY)
    return pl.pallas_call(
        body, name="flash_bwd", grid=(H, nq),
        in_specs=[pl.BlockSpec((1, M, QK_PAD), lambda h, j: (h, 0, 0)),
                  pl.BlockSpec((1, M, V_DIM), lambda h, j: (h, 0, 0)),
                  pl.BlockSpec((1, nq, 1, T), lambda h, j: (h, 0, 0, 0)),
                  pl.BlockSpec((1, nq, 1, T), lambda h, j: (h, 0, 0, 0)),
                  pl.BlockSpec((1, T, QK_PAD), lambda h, j: (h, j, 0)),
                  pl.BlockSpec((1, T, V_DIM), lambda h, j: (h, j, 0))] + [any_spec] * nx,
        out_specs=[pl.BlockSpec((1, M, QK_PAD), lambda h, j: (h, 0, 0)),
                   pl.BlockSpec((1, T, QK_PAD), lambda h, j: (h, j, 0)),
                   pl.BlockSpec((1, T, V_DIM), lambda h, j: (h, j, 0))] + [any_spec] * nx,
        out_shape=[jax.ShapeDtypeStruct((H, M, QK_PAD), f32),
                   jax.ShapeDtypeStruct((H, M, QK_PAD), f32),
                   jax.ShapeDtypeStruct((H, M, V_DIM), f32)] + _exchange_shapes(carried, scatter),
        scratch_shapes=[pltpu.VMEM((T, QK_PAD), f32), pltpu.VMEM((T, V_DIM), f32)] + _exchange_sems(nx),
        compiler_params=_cparams(("arbitrary", "arbitrary")),
    )(q, do, lse, delta, k, v, *carried)


def _dt_fwd(dt_raw, bias, expand):
    M = dt_raw.shape[0]
    tm = _rt(M)

    def body(x_ref, b_ref, e_ref, o_ref, oe_ref):
        u = x_ref[...] + b_ref[...]
        sp = jnp.maximum(u, 0.0) + jnp.log(1.0 + jnp.exp(-jnp.abs(u)))
        lane = lax.broadcasted_iota(jnp.int32, u.shape, 1)
        dtp = jnp.where(lane < SSM_HEADS, sp, 0.0)
        o_ref[...] = dtp
        oe_ref[...] = _dot_hi(dtp, e_ref[...])

    return pl.pallas_call(
        body, name="dt_fwd", grid=(M // tm,),
        in_specs=[_row(LANES, tm), _full((1, LANES)), _full((LANES, D_SSM))],
        out_specs=[_row(LANES, tm), _row(D_SSM, tm)],
        out_shape=[jax.ShapeDtypeStruct((M, LANES), f32), jax.ShapeDtypeStruct((M, D_SSM), f32)],
        compiler_params=_cparams(("parallel",)),
    )(dt_raw, bias, expand)


def _dt_bwd(dt_raw, bias, ddt):
    M = dt_raw.shape[0]
    tm = _rt(M)

    def body(x_ref, b_ref, d_ref, o_ref, db_ref):
        @pl.when(pl.program_id(0) == 0)
        def _():
            db_ref[...] = jnp.zeros_like(db_ref)

        u = x_ref[...] + b_ref[...]
        lane = lax.broadcasted_iota(jnp.int32, u.shape, 1)
        g = jnp.where(lane < SSM_HEADS, d_ref[...] * _sigmoid(u), 0.0)
        o_ref[...] = g
        db_ref[...] += jnp.sum(g, axis=0, keepdims=True)

    return pl.pallas_call(
        body, name="dt_bwd", grid=(M // tm,),
        in_specs=[_row(LANES, tm), _full((1, LANES)), _row(LANES, tm)],
        out_specs=[_row(LANES, tm), _full((1, LANES))],
        out_shape=[jax.ShapeDtypeStruct((M, LANES), f32), jax.ShapeDtypeStruct((1, LANES), f32)],
        compiler_params=_cparams(("arbitrary",)),
    )(dt_raw, bias, ddt)


SSM_GW = SSM_HPG * SSM_P
SSM_PAIRS = SSM_GW // LANES


def _ssd_common(dte_ref, dtt_ref, ae_ref, acol_ref):
    Q = CHUNK
    r = lax.broadcasted_iota(jnp.int32, (Q, Q), 0)
    c = lax.broadcasted_iota(jnp.int32, (Q, Q), 1)
    causal = r >= c
    anti = c >= r
    tril = causal.astype(f32)
    triu = anti.astype(f32)
    dt_e = dte_ref[...]
    cs_e = _dot_hi(tril, dt_e * ae_ref[...])
    cst = _dot_hi(dtt_ref[...] * acol_ref[...], triu)
    cs_last = cs_e[Q - 1:Q, :]
    return causal, anti, triu, dt_e, cs_e, cst, jnp.exp(cs_e), jnp.exp(cs_last - cs_e), jnp.exp(cs_last)


def _half_masks():
    lane = lax.broadcasted_iota(jnp.int32, (CHUNK, LANES), 1)
    lo = lane < SSM_P
    return lo, jnp.logical_not(lo)


def _ssd_fwd(xbc_c, dt_e, dtt, a_e, a_col, z, d_exp, g_ssm):
    M = xbc_c.shape[0]
    Q = CHUNK
    nch = M // Q
    gw = D_SSM // SSM_GROUPS

    def body(x_ref, dte_ref, dtt_ref, ae_ref, acol_ref, z_ref, dexp_ref, gn_ref, y_ref, hin_ref, o_ref, ht_sc):
        @pl.when(pl.program_id(0) == 0)
        def _():
            ht_sc[...] = jnp.zeros_like(ht_sc)

        causal, _, _, dt_e, cs_e, cst, ecs_e, dte_e, elast_e = _ssd_common(dte_ref, dtt_ref, ae_ref, acol_ref)
        halves = _half_masks()
        for g in range(SSM_GROUPS):
            g0 = g * SSM_GW
            bg = x_ref[:, D_SSM + g * SSM_N:D_SSM + (g + 1) * SSM_N]
            cg = x_ref[:, D_SSM + D_BC + g * SSM_N:D_SSM + D_BC + (g + 1) * SSM_N]
            bg_b = bg.astype(bf16)
            cg_b = cg.astype(bf16)
            cb = _dot_nt(cg_b, bg_b)
            bgt_b = bg.T.astype(bf16)
            xdt_g = x_ref[:, g0:g0 + SSM_GW] * dt_e[:, g0:g0 + SSM_GW]
            ht = ht_sc[g]
            hin_ref[0, g] = ht
            y_off = _dot(cg_b, ht.astype(bf16)) * ecs_e[:, g0:g0 + SSM_GW]
            for pr in range(SSM_PAIRS):
                p0 = pr * LANES
                xdt_p = xdt_g[:, p0:p0 + LANES]
                acc = y_off[:, p0:p0 + LANES]
                for half in range(2):
                    h = g * SSM_HPG + pr * 2 + half
                    seg = cs_e[:, h * SSM_P:h * SSM_P + 1] - cst[h:h + 1, :]
                    lm = jnp.exp(jnp.where(causal, seg, -jnp.inf))
                    xm = jnp.where(halves[half], xdt_p, 0.0).astype(bf16)
                    acc = acc + _dot((cb * lm).astype(bf16), xm)
                y_ref[:, g0 + p0:g0 + p0 + LANES] = acc
            st = _dot(bgt_b, (xdt_g * dte_e[:, g0:g0 + SSM_GW]).astype(bf16))
            ht_sc[g] = ht * elast_e[:, g0:g0 + SSM_GW] + st
        yg = (y_ref[...] + dexp_ref[...] * x_ref[:, 0:D_SSM]) * _silu(z_ref[...])
        for gi in range(SSM_GROUPS):
            blk = yg[:, gi * gw:(gi + 1) * gw]
            o_ref[:, gi * gw:(gi + 1) * gw] = (blk * _rstd(blk) * gn_ref[:, gi * gw:(gi + 1) * gw]).astype(bf16)

    chunk_rows = pl.BlockSpec((Q, D_SSM), lambda c: (c, 0))
    return pl.pallas_call(
        body, name="ssd_fwd", grid=(nch,),
        in_specs=[pl.BlockSpec((Q, D_XBC), lambda c: (c, 0)), chunk_rows,
                  pl.BlockSpec((SSM_HEADS, Q), lambda c: (0, c)),
                  _full((1, D_SSM)), _full((SSM_HEADS, LANES)), chunk_rows, _full((1, D_SSM)), _full((1, D_SSM))],
        out_specs=[chunk_rows, pl.BlockSpec((1, SSM_GROUPS, SSM_N, SSM_GW), lambda c: (c, 0, 0, 0)), chunk_rows],
        out_shape=[jax.ShapeDtypeStruct((M, D_SSM), f32),
                   jax.ShapeDtypeStruct((nch, SSM_GROUPS, SSM_N, SSM_GW), f32),
                   jax.ShapeDtypeStruct((M, D_SSM), bf16)],
        scratch_shapes=[pltpu.VMEM((SSM_GROUPS, SSM_N, SSM_GW), f32)],
        compiler_params=_cparams(("arbitrary",)),
    )(xbc_c, dt_e, dtt, a_e, a_col, z, d_exp, g_ssm)


def _ssd_bwd(xbc_c, dtp, dt_e, dtt, a_row, a_e, a_col, hin, y, z, d_ssm, g_ssm, d_exp, head_ind):
    M = xbc_c.shape[0]
    Q = CHUNK
    nch = M // Q
    rev = lambda c: nch - 1 - c

    gw = D_SSM // SSM_GROUPS

    def body(x_ref, dtp_ref, dte_ref, dtt_ref, arow_ref, ae_ref, acol_ref, hin_ref, y_ref, zz_ref, do_ref, gn_ref,
             dexp_ref, ind_ref, dx_ref, ddt_ref, da_ref, dz_ref, dgn_ref, dd_ref,
             dht_sc, z_sc, z1_sc, last_sc, ct_sc, dy_ref, ddc_sc):
        @pl.when(pl.program_id(0) == 0)
        def _():
            dht_sc[...] = jnp.zeros_like(dht_sc)
            da_ref[...] = jnp.zeros_like(da_ref)
            last_sc[...] = jnp.zeros_like(last_sc)
            ct_sc[...] = jnp.zeros_like(ct_sc)
            dgn_ref[...] = jnp.zeros_like(dgn_ref)
            ddc_sc[...] = jnp.zeros_like(ddc_sc)

        zv = zz_ref[...]
        xv = x_ref[:, 0:D_SSM]
        sz = _silu(zv)
        yd = y_ref[...] + dexp_ref[...] * xv
        yg = yd * sz
        dov = do_ref[...]
        for gi in range(SSM_GROUPS):
            sl = slice(gi * gw, (gi + 1) * gw)
            dyg, dgp = _rms_bwd_math(yg[:, sl], gn_ref[:, sl], dov[:, sl])
            dgn_ref[:, sl] += jnp.sum(dgp, axis=0, keepdims=True)
            dyd = dyg * sz[:, sl]
            dy_ref[:, sl] = dyd
            dz_ref[:, sl] = (dyg * yd[:, sl] * _dsilu(zv[:, sl])).astype(bf16)
            ddc_sc[:, sl] += jnp.sum(dyd * xv[:, sl], axis=0, keepdims=True)

        @pl.when(pl.program_id(0) == nch - 1)
        def _():
            dd_ref[...] = _dot_hi(ddc_sc[...], ind_ref[...])

        causal, anti, triu, dt_e, cs_e, cst, ecs_e, dte_e, elast_e = _ssd_common(dte_ref, dtt_ref, ae_ref, acol_ref)
        halves = _half_masks()
        lane = lax.broadcasted_iota(jnp.int32, (Q, LANES), 1)
        rsum = jnp.zeros((Q, LANES), f32)
        for g in range(SSM_GROUPS):
            g0 = g * SSM_GW
            gs = slice(g0, g0 + SSM_GW)
            b0 = D_SSM + g * SSM_N
            c0 = D_SSM + D_BC + g * SSM_N
            bg = x_ref[:, b0:b0 + SSM_N]
            cg = x_ref[:, c0:c0 + SSM_N]
            bg_b = bg.astype(bf16)
            cg_b = cg.astype(bf16)
            cgt_b = cg.T.astype(bf16)
            cbt = _dot_nt(bg_b, cg_b)
            cb = _dot_nt(cg_b, bg_b)
            x_g = x_ref[:, gs]
            dt_g = dt_e[:, gs]
            xdt_g = x_g * dt_g
            dy_g = dy_ref[:, gs]
            ht = hin_ref[0, g]
            ht_b = ht.astype(bf16)
            dht = dht_sc[g]
            dht_b = dht.astype(bf16)
            dye_b = (dy_g * ecs_e[:, gs]).astype(bf16)
            dc = _dot_nt(dye_b, ht_b)
            dht_new = dht * elast_e[:, gs] + _dot(cgt_b, dye_b)
            e = _dot(bg_b, dht_b)
            xdtd = xdt_g * dte_e[:, gs]
            db = _dot_nt(xdtd.astype(bf16), dht_b)
            dxdt_state = e * dte_e[:, gs]
            exd = e * xdtd
            z1_sc[:, gs] = dy_g * (_dot(cg_b, ht_b) * ecs_e[:, gs]) - exd
            last_sc[0:1, gs] = (jnp.sum(exd, axis=0, keepdims=True)
                                + jnp.sum(dht * ht, axis=0, keepdims=True) * elast_e[:, gs])
            dg_acc = jnp.zeros((Q, Q), f32)
            for pr in range(SSM_PAIRS):
                p0 = pr * LANES
                ps = slice(g0 + p0, g0 + p0 + LANES)
                dy_p = dy_g[:, p0:p0 + LANES]
                xdt_pb = xdt_g[:, p0:p0 + LANES].astype(bf16)
                acc = dxdt_state[:, p0:p0 + LANES]
                for half in range(2):
                    h = g * SSM_HPG + pr * 2 + half
                    seg = cs_e[:, h * SSM_P:h * SSM_P + 1] - cst[h:h + 1, :]
                    lm = jnp.exp(jnp.where(causal, seg, -jnp.inf))
                    lmt = jnp.exp(jnp.where(anti, -seg, -jnp.inf))
                    dym = jnp.where(halves[half], dy_p, 0.0).astype(bf16)
                    acc = acc + _dot((cbt * lmt).astype(bf16), dym)
                    dml = _dot_nt(dym, xdt_pb) * lm
                    dg_acc = dg_acc + dml
                    w = dml * cb
                    rsum = rsum + jnp.where(lane == h, jnp.sum(w, axis=1, keepdims=True), 0.0)
                    ct_sc[h:h + 1, :] = jnp.sum(w, axis=0, keepdims=True)
                dx_ref[:, ps] = acc * dt_g[:, p0:p0 + LANES] + dexp_ref[:, ps] * dy_p
                z_sc[:, ps] = acc * x_g[:, p0:p0 + LANES]
            dg_b = dg_acc.astype(bf16)
            dx_ref[:, c0:c0 + SSM_N] = dc + _dot(dg_b, bg_b)
            dx_ref[:, b0:b0 + SSM_N] = db + _dot_tn(dg_b, cg_b)
            dht_sc[g] = dht_new
        s1 = _dot_hi(z1_sc[...], ind_ref[...])
        s2 = _dot_hi(z_sc[...], ind_ref[...])
        last = _dot_hi(last_sc[...], ind_ref[...])[0:1, :]
        dtp = dtp_ref[...]
        row = lax.broadcasted_iota(jnp.int32, (Q, LANES), 0)
        dcs = s1 + rsum + jnp.where(row == Q - 1, last, 0.0)
        tril = causal.astype(f32)
        da = _dot_hi(triu, dcs) - _dot_hi(ct_sc[...], tril).T
        ddt_ref[...] = s2 + da * arow_ref[...]
        da_ref[...] += jnp.sum(da * dtp, axis=0, keepdims=True)

    chunk_rows = pl.BlockSpec((Q, D_SSM), lambda c: (rev(c), 0))
    return pl.pallas_call(
        body, name="ssd_bwd", grid=(nch,),
        in_specs=[pl.BlockSpec((Q, D_XBC), lambda c: (rev(c), 0)),
                  pl.BlockSpec((Q, LANES), lambda c: (rev(c), 0)),
                  pl.BlockSpec((Q, D_SSM), lambda c: (rev(c), 0)),
                  pl.BlockSpec((SSM_HEADS, Q), lambda c: (0, rev(c))),
                  _full((1, LANES)), _full((1, D_SSM)), _full((SSM_HEADS, LANES)),
                  pl.BlockSpec((1, SSM_GROUPS, SSM_N, SSM_GW), lambda c: (rev(c), 0, 0, 0)),
                  chunk_rows, chunk_rows, chunk_rows, _full((1, D_SSM)),
                  _full((1, D_SSM)), _full((D_SSM, LANES))],
        out_specs=[pl.BlockSpec((Q, D_XBC), lambda c: (rev(c), 0)),
                   pl.BlockSpec((Q, LANES), lambda c: (rev(c), 0)),
                   _full((1, LANES)), chunk_rows, _full((1, D_SSM)), _full((1, LANES))],
        out_shape=[jax.ShapeDtypeStruct((M, D_XBC), f32), jax.ShapeDtypeStruct((M, LANES), f32),
                   jax.ShapeDtypeStruct((1, LANES), f32), jax.ShapeDtypeStruct((M, D_SSM), bf16),
                   jax.ShapeDtypeStruct((1, D_SSM), f32), jax.ShapeDtypeStruct((1, LANES), f32)],
        scratch_shapes=[pltpu.VMEM((SSM_GROUPS, SSM_N, SSM_GW), f32), pltpu.VMEM((Q, D_SSM), f32),
                        pltpu.VMEM((Q, D_SSM), f32), pltpu.VMEM((SUBLANES, D_SSM), f32),
                        pltpu.VMEM((LANES, Q), f32), pltpu.VMEM((Q, D_SSM), f32), pltpu.VMEM((1, D_SSM), f32)],
        compiler_params=_cparams(("arbitrary",)),
    )(xbc_c, dtp, dt_e, dtt, a_row, a_e, a_col, hin, y, z, d_ssm, g_ssm, d_exp, head_ind)


_PEER_FLIPS = [(0, 0, 1), (0, 1, 0), (0, 1, 1), (1, 0, 0), (1, 0, 1), (1, 1, 0), (1, 1, 1)]


def _exchange_copies(ins, outs, send_sems, recv_sems, loc_sems, scatter):
    n = len(ins)
    x, y, c = lax.axis_index("x"), lax.axis_index("y"), lax.axis_index("c")
    me = 4 * x + 2 * y + c
    copies = []
    for a in range(n):
        src = ins[a].at[me] if scatter else ins[a]
        copies.append(pltpu.make_async_copy(src, outs[a].at[me], loc_sems.at[a]))
    for p, (fx, fy, fc) in enumerate(_PEER_FLIPS):
        tx = 1 - x if fx else x
        ty = 1 - y if fy else y
        tc = 1 - c if fc else c
        tgt = 4 * tx + 2 * ty + tc
        for a in range(n):
            src = ins[a].at[tgt] if scatter else ins[a]
            copies.append(pltpu.make_async_remote_copy(
                src_ref=src, dst_ref=outs[a].at[me],
                send_sem=send_sems.at[p * n + a], recv_sem=recv_sems.at[p * n + a],
                device_id=(tx, ty, tc), device_id_type=_MESH))
    return copies


def _exchange_shapes(arrays, scatter):
    return [jax.ShapeDtypeStruct(a.shape if scatter else (N_DEV,) + a.shape, a.dtype) for a in arrays]


def _exchange_sems(n):
    return [pltpu.SemaphoreType.DMA((7 * n,)), pltpu.SemaphoreType.DMA((7 * n,)), pltpu.SemaphoreType.DMA((n,))]


def _gather_two_level(arrays, name):
    n = len(arrays)

    def body(*refs):
        ins, outs = refs[:n], refs[n:2 * n]
        send_sems, recv_sems, loc_sems = refs[2 * n:]
        x, y, c = lax.axis_index("x"), lax.axis_index("y"), lax.axis_index("c")
        me, sibling = (x, y, c), (x, y, 1 - c)
        chips = [(1 - x, y), (x, 1 - y), (1 - x, 1 - y)]

        def slot(a, dev):
            return outs[a].at[4 * dev[0] + 2 * dev[1] + dev[2]]

        def copy(a, k, block, to, src=None):
            return pltpu.make_async_remote_copy(
                src_ref=slot(a, block) if src is None else src, dst_ref=slot(a, block),
                send_sem=send_sems.at[7 * a + k], recv_sem=recv_sems.at[7 * a + k],
                device_id=to, device_id_type=_MESH)

        mine = [pltpu.make_async_copy(ins[a], slot(a, me), loc_sems.at[a]) for a in range(n)]
        first = []
        for a in range(n):
            first.append(copy(a, 0, me, sibling, src=ins[a]))
            first += [copy(a, 1 + j, me, (*chip, c), src=ins[a]) for j, chip in enumerate(chips)]
        for cp in mine + first:
            cp.start()
        passed = []
        for j, chip in enumerate(chips):
            for a in range(n):
                copy(a, 1 + j, (*chip, c), me).wait_recv()
                cp = copy(a, 4 + j, (*chip, c), sibling)
                cp.start()
                passed.append(cp)
        for a in range(n):
            copy(a, 0, sibling, me).wait_recv()
            for j, chip in enumerate(chips):
                copy(a, 4 + j, (*chip, 1 - c), me).wait_recv()
        for cp in first + passed:
            cp.wait_send()
        for cp in mine:
            cp.wait()

    any_spec = pl.BlockSpec(memory_space=pl.ANY)
    return pl.pallas_call(
        body, name=name, in_specs=[any_spec] * n, out_specs=[any_spec] * n,
        out_shape=_exchange_shapes(arrays, False), scratch_shapes=_exchange_sems(n),
    )(*arrays)


def _exchange_tail(scattered, gathered, name):
    ns, ng = len(scattered), len(gathered)
    n = ns + ng

    def body(*refs):
        sems = refs[2 * n:]
        copies = (_exchange_copies(refs[:ns], refs[n:n + ns], *sems[:3], True)
                  + _exchange_copies(refs[ns:n], refs[n + ns:2 * n], *sems[3:], False))
        for cp in copies:
            cp.start()
        for cp in copies:
            cp.wait()

    any_spec = pl.BlockSpec(memory_space=pl.ANY)
    return pl.pallas_call(
        body, name=name, in_specs=[any_spec] * n, out_specs=[any_spec] * n,
        out_shape=_exchange_shapes(scattered, True) + _exchange_shapes(gathered, False),
        scratch_shapes=_exchange_sems(ns) + _exchange_sems(ng),
    )(*scattered, *gathered)


def _adamw_math(g, w, m, v):
    c1 = 1.0 - ADAM_B1 ** ADAM_STEP
    c2 = 1.0 - ADAM_B2 ** ADAM_STEP
    mn = ADAM_B1 * m + (1.0 - ADAM_B1) * g
    vn = ADAM_B2 * v + (1.0 - ADAM_B2) * (g * g)
    m_hat = mn / c1
    v_hat = vn / c2
    return -ADAM_LR * (m_hat / (jnp.sqrt(v_hat) + ADAM_EPS) + ADAM_WD * w), mn, vn


def _adamw(parts, w, m, v, name):
    R, C = w.shape
    tr = _pick(R, (PACK_ROW_TILE, 64, 32, 16, 8))

    def body(p_ref, w_ref, m_ref, v_ref, g_ref, d_ref, nm_ref, nv_ref):
        g = p_ref[0].astype(f32)
        for s in range(1, N_DEV):
            g = g + p_ref[s].astype(f32)
        g_ref[...] = g
        d_ref[...], nm_ref[...], nv_ref[...] = _adamw_math(g, w_ref[...], m_ref[...], v_ref[...])

    spec = pl.BlockSpec((tr, C), lambda i: (i, 0))
    return pl.pallas_call(
        body, name=name, grid=(R // tr,),
        in_specs=[pl.BlockSpec((N_DEV, tr, C), lambda i: (0, i, 0)), spec, spec, spec],
        out_specs=[spec] * 4, out_shape=[jax.ShapeDtypeStruct((R, C), f32)] * 4,
        compiler_params=_cparams(("parallel",)),
    )(parts, w, m, v)


def _adamw_replicated(parts, ws, ms, vs):
    n = len(ws)
    R = parts.shape[1]
    sizes = [int(w.shape[1]) for w in ws]

    def body(*refs):
        p_ref = refs[0]
        w_refs, m_refs, v_refs = refs[1:1 + n], refs[1 + n:1 + 2 * n], refs[1 + 2 * n:1 + 3 * n]
        loss_ref = refs[1 + 3 * n]
        outs = refs[2 + 3 * n:]
        g_all = p_ref[0]
        for s in range(1, N_DEV):
            g_all = g_all + p_ref[s]
        row = 0
        for p in range(n):
            pieces, left = [], sizes[p]
            while left > 0:
                take = min(left, PACK_W)
                pieces.append(g_all[row:row + 1, 0:take])
                left -= take
                row += 1
            g = pieces[0] if len(pieces) == 1 else jnp.concatenate(pieces, axis=1)
            d, mn, vn = _adamw_math(g, w_refs[p][...], m_refs[p][...], v_refs[p][...])
            outs[4 * p][...] = g
            outs[4 * p + 1][...] = d
            outs[4 * p + 2][...] = mn
            outs[4 * p + 3][...] = vn
        loss_ref[...] = g_all[row:row + 1, 0:LANES]

    in_specs = [_full((N_DEV, R, PACK_W))] + [_full((1, s)) for s in sizes] * 3
    out_specs = [_full((1, LANES))]
    out_shape = [jax.ShapeDtypeStruct((1, LANES), f32)]
    for s in sizes:
        out_specs += [_full((1, s))] * 4
        out_shape += [jax.ShapeDtypeStruct((1, s), f32)] * 4
    res = pl.pallas_call(
        body, name="adamw_replicated", in_specs=in_specs, out_specs=out_specs, out_shape=out_shape,
        compiler_params=pltpu.CompilerParams(vmem_limit_bytes=VMEM_LIMIT),
    )(parts, *ws, *ms, *vs)
    return res[0], [res[1 + 4 * p:5 + 4 * p] for p in range(n)]


def _flat_rows(a, lead_ndim):
    lead = a.shape[:lead_ndim]
    n = int(np.prod(a.shape[lead_ndim:]))
    a = a.reshape(lead + (n,))
    pad = (-n) % PACK_W
    if pad:
        a = jnp.pad(a, [(0, 0)] * lead_ndim + [(0, pad)])
    return a.reshape(lead + ((n + pad) // PACK_W, PACK_W))


def _pack(arrays, lead_ndim, total_rows, dtype):
    rows = [_flat_rows(a.astype(dtype), lead_ndim) for a in arrays]
    cat = jnp.concatenate(rows, axis=lead_ndim)
    pad = total_rows - cat.shape[lead_ndim]
    if pad:
        cat = jnp.pad(cat, [(0, 0)] * lead_ndim + [(0, pad), (0, 0)])
    return cat


def _unpack(buf, shapes, lead_ndim):
    out = []
    r = 0
    lead = buf.shape[:lead_ndim]
    for shp in shapes:
        n = int(np.prod(shp))
        nr = -(-n // PACK_W)
        piece = lax.slice_in_dim(buf, r, r + nr, axis=lead_ndim)
        piece = piece.reshape(lead + (nr * PACK_W,))
        piece = lax.slice_in_dim(piece, 0, n, axis=lead_ndim)
        out.append(piece.reshape(lead + tuple(shp)))
        r += nr
    return out


def _round_up(n, m):
    return -(-n // m) * m


def kernel(x, meta_tokens, norm_mix_pre, norm_mix_post, norm_ffn_pre, norm_ffn_post, w_in, q_a_norm, w_uq, kv_a_norm, w_ukv, attn_out_norm, ssm_conv_w, ssm_conv_b, ssm_dt_bias, ssm_A_log, ssm_D, ssm_norm, w_out, w_up, ffn_conv_w, ffn_conv_b, w_down, loss_target, m_meta_tokens, m_norm_mix_pre, m_norm_mix_post, m_norm_ffn_pre, m_norm_ffn_post, m_w_in, m_q_a_norm, m_w_uq, m_kv_a_norm, m_w_ukv, m_attn_out_norm, m_ssm_conv_w, m_ssm_conv_b, m_ssm_dt_bias, m_ssm_A_log, m_ssm_D, m_ssm_norm, m_w_out, m_w_up, m_ffn_conv_w, m_ffn_conv_b, m_w_down, v_meta_tokens, v_norm_mix_pre, v_norm_mix_post, v_norm_ffn_pre, v_norm_ffn_post, v_w_in, v_q_a_norm, v_w_uq, v_kv_a_norm, v_w_ukv, v_attn_out_norm, v_ssm_conv_w, v_ssm_conv_b, v_ssm_dt_bias, v_ssm_A_log, v_ssm_D, v_ssm_norm, v_w_out, v_w_up, v_ffn_conv_w, v_ffn_conv_b, v_w_down):
    seq = x.shape[1]
    n_real = N_META + seq
    Lp = _round_up(n_real, 768) if n_real > 2048 else _round_up(n_real, ROW_TILE)
    D = D_MODEL

    early_w = [w_uq, w_ukv]
    late_w = [w_out, w_down]
    sharded_s = [meta_tokens, ssm_conv_w, ffn_conv_w]
    grp_a = dict(names=["w_out", "w_down"], w=late_w, m=[m_w_out, m_w_down],
                 v=[v_w_out, v_w_down])
    grp_b = dict(names=["w_uq", "w_ukv", "ssm_conv_w", "ffn_conv_w"],
                 w=early_w + [ssm_conv_w, ffn_conv_w],
                 m=[m_w_uq, m_w_ukv, m_ssm_conv_w, m_ffn_conv_w],
                 v=[v_w_uq, v_w_ukv, v_ssm_conv_w, v_ffn_conv_w])
    grp_meta = dict(names=["meta_tokens"], w=[meta_tokens], m=[m_meta_tokens], v=[v_meta_tokens])
    repl_w = [norm_mix_pre, norm_mix_post, norm_ffn_pre, norm_ffn_post, q_a_norm, kv_a_norm, attn_out_norm,
              ssm_conv_b, ssm_dt_bias, ssm_A_log, ssm_D, ssm_norm, ffn_conv_b]
    repl_m = [m_norm_mix_pre, m_norm_mix_post, m_norm_ffn_pre, m_norm_ffn_post, m_q_a_norm, m_kv_a_norm,
              m_attn_out_norm, m_ssm_conv_b, m_ssm_dt_bias, m_ssm_A_log, m_ssm_D, m_ssm_norm, m_ffn_conv_b]
    repl_v = [v_norm_mix_pre, v_norm_mix_post, v_norm_ffn_pre, v_norm_ffn_post, v_q_a_norm, v_kv_a_norm,
              v_attn_out_norm, v_ssm_conv_b, v_ssm_dt_bias, v_ssm_A_log, v_ssm_D, v_ssm_norm, v_ffn_conv_b]

    def pack_rows(arrs, lead):
        return _round_up(sum(-(-int(np.prod(a.shape[lead:])) // PACK_W) for a in arrs), 16)

    wb = _pack(early_w, 0, pack_rows(early_w, 0), bf16)
    wl = _pack(late_w, 0, pack_rows(late_w, 0), bf16)
    ws = _pack(sharded_s, 0, pack_rows(sharded_s, 0), f32)
    wb_all, ws_all, win_all = _gather_two_level([wb, ws, w_in[0].astype(bf16)], "gather_weights")
    g_w_uq, g_w_ukv = _unpack(wb_all, [a.shape for a in early_w], 1)
    g_meta, g_sconv, g_fconv = _unpack(ws_all, [a.shape for a in sharded_s], 1)

    def cols(gathered):
        t = gathered[:, 0]
        return jnp.transpose(t, (1, 0, 2)).reshape(t.shape[1], N_DEV * t.shape[2])

    win = cols(win_all[:, None])
    o = np.cumsum((0, Q_RANK, KV_RANK, QK_ROPE, D_SSM, D_XBC, SSM_HEADS))
    w_q, w_kv = win[:, o[0]:o[1]], win[:, o[1]:o[2]]
    w_rope = jnp.pad(win[:, o[2]:o[3]], ((0, 0), (0, LANES - QK_ROPE)))
    w_z, w_xbc = win[:, o[3]:o[4]], win[:, o[4]:o[5]]
    w_dt = jnp.pad(win[:, o[5]:o[6]], ((0, 0), (0, LANES - SSM_HEADS)))
    wuq = g_w_uq.reshape(Q_RANK, MLA_HEADS, QK_NOPE + QK_ROPE)
    wuq = jnp.pad(wuq, ((0, 0), (0, 0), (0, QK_PAD - QK_NOPE - QK_ROPE))).reshape(Q_RANK, MLA_HEADS * QK_PAD)
    wukv = g_w_ukv.reshape(KV_RANK, MLA_HEADS * (QK_NOPE + V_DIM))
    meta_full = jnp.transpose(g_meta, (1, 0, 2)).reshape(N_META, D)
    sconv_w = jnp.pad(cols(g_sconv), ((0, SUBLANES - SSM_CONV), (0, 0)))
    fconv_w = jnp.pad(cols(g_fconv), ((0, SUBLANES - FFN_CONV), (0, 0)))

    pos = jnp.arange(Lp, dtype=f32)
    inv = ROPE_THETA ** (-jnp.arange(0, QK_ROPE, 2, dtype=f32) / QK_ROPE)
    ang = pos[:, None] * inv[None, :]
    cs_, sn_ = jnp.cos(ang), jnp.sin(ang)
    zpad = jnp.zeros((Lp, LANES - QK_ROPE), f32)
    cos_t = jnp.concatenate([cs_, cs_, zpad], axis=1)
    sin_t = jnp.concatenate([-sn_, sn_, zpad], axis=1)
    dt_bias_p = jnp.pad(ssm_dt_bias, ((0, 0), (0, LANES - SSM_HEADS)))
    a_neg = -jnp.exp(ssm_A_log)
    a_row = jnp.pad(a_neg, ((0, 0), (0, LANES - SSM_HEADS)))
    a_col = jnp.broadcast_to(a_neg.reshape(SSM_HEADS, 1), (SSM_HEADS, LANES))
    d_exp = jnp.repeat(ssm_D, SSM_P, axis=1)
    a_e = jnp.repeat(a_neg, SSM_P, axis=1)
    head_ind = (jnp.arange(D_SSM)[:, None] // SSM_P == jnp.arange(LANES)[None, :]).astype(f32)

    xb = x[0]
    h0 = jnp.concatenate([meta_full, xb, jnp.zeros((Lp - n_real, D), f32)], axis=0)
    tgt = jnp.pad(loss_target[0], ((N_META, Lp - n_real), (0, 0)))
    hn1, q_c, kv_c, kpe_raw, z, xbc, dt_raw = _in_proj(h0, norm_mix_pre, [w_q, w_kv, w_rope, w_z, w_xbc, w_dt])

    qn, qh = _up_q_rope(q_c, q_a_norm, wuq, cos_t, sin_t)
    kvn, kh, vh = _up_kv_rope(kv_c, kv_a_norm, wukv, kpe_raw, cos_t, sin_t)
    attn, lse, wl_all, wup_all = _flash_fwd(qh, kh, vh, [wl, w_up[0].astype(bf16)], False)
    g_w_out, g_w_down = _unpack(wl_all, [a.shape for a in late_w], 1)
    wout = g_w_out.reshape(D_ATTN + D_SSM, D)
    wout_a, wout_s = wout[:D_ATTN], wout[D_ATTN:]
    wup = cols(wup_all[:, None])
    wdown = g_w_down.reshape(D_FF, D)

    xbc_c = _ssm_conv_fwd(xbc, sconv_w, ssm_conv_b)
    dtp, dt_e = _dt_fwd(dt_raw, dt_bias_p, jnp.transpose(head_ind))
    dtt = jnp.transpose(dtp[:, :SSM_HEADS])
    y_ssd, hin, ssm = _ssd_fwd(xbc_c, dt_e, dtt, a_e, a_col, z, d_exp, ssm_norm)

    an, mix, h1, hn2 = _out_proj_resid(attn, attn_out_norm, ssm, wout_a, wout_s, h0, norm_mix_post, norm_ffn_pre)
    up = _mm(hn2, wup, False, "ffn_up")
    act = _ffn_gate_fwd(up, fconv_w, ffn_conv_b)
    dh2, d_down, dg_ffn_post, loss_part = _final(h1, act, wdown, norm_ffn_post, tgt, n_real)

    dw_down = _mm_tn(act, d_down, "ffn_down_dw")
    dup_g, dup_v, dwc_g, dwc_v, dbc_g, dbc_v = _ffn_gate_bwd(up, fconv_w, ffn_conv_b, d_down, wdown)
    dw_up = jnp.concatenate([_mm_tn(hn2, dup_g, "ffn_up_dw_g"), _mm_tn(hn2, dup_v, "ffn_up_dw_v")], axis=1)
    dh1, d_mix, dg_ffn_pre, dg_mix_post = _mid_bwd(h1, norm_ffn_pre, dup_g, dup_v, wup, dh2, mix, norm_mix_post)
    d_an = _mm(d_mix, wout_a, True, "out_proj_dx_a")
    d_ssm = _mm(d_mix, wout_s, True, "out_proj_dx_s")
    dw_out = jnp.concatenate([_mm_tn(an, d_mix, "out_proj_dw_a"), _mm_tn(ssm, d_mix, "out_proj_dw_s")], axis=0)

    do_h, delta, dg_attn_out = _attn_out_bwd(attn, attn_out_norm, d_an)
    def col_blocks(gm):
        r, cc = gm.shape
        return jnp.transpose(gm.reshape(r, N_DEV, cc // N_DEV), (1, 0, 2))

    blocks_a = [dw_out.reshape(N_DEV, (D_ATTN + D_SSM) // N_DEV, D), dw_down.reshape(N_DEV, D_FF // N_DEV, D)]
    gpack_a = _pack(blocks_a, 1, pack_rows(blocks_a, 1), bf16)
    dqh, dkh, dvh, gparts_a, gparts_up = _flash_bwd(qh, kh, vh, do_h, lse, delta,
                                                    [gpack_a, col_blocks(dw_up).astype(bf16)], True)
    d_q_c, dg_q, dw_uq = _q_branch_bwd(dqh, cos_t, sin_t, wuq, q_c, q_a_norm, qn)
    d_kv_c, dg_kv, dw_ukv, d_kpe_raw = _kv_branch_bwd(dkh, dvh, cos_t, sin_t, wukv, kv_c, kv_a_norm, kvn)

    d_xbc_c, ddt, da_heads, dz, dg_ssm, dd_heads = _ssd_bwd(
        xbc_c, dtp, dt_e, dtt, a_row, a_e, a_col, hin, y_ssd, z, d_ssm, ssm_norm, d_exp, head_ind)
    d_xbc, dw_sconv, db_sconv = _ssm_conv_bwd(xbc, sconv_w, ssm_conv_b, d_xbc_c)
    d_dt_raw, d_dt_bias = _dt_bwd(dt_raw, dt_bias_p, ddt)

    dw_q, dw_kv, dw_rope, dw_z, dw_xbc, dw_dt = _in_proj_dw(hn1, [d_q_c, d_kv_c, d_kpe_raw, dz, d_xbc, d_dt_raw])
    dw_in = jnp.concatenate([dw_q, dw_kv, dw_rope[:, :QK_ROPE], dw_z, dw_xbc, dw_dt[:, :SSM_HEADS]], axis=1)
    dw_uq3 = dw_uq.reshape(Q_RANK, MLA_HEADS, QK_PAD)[:, :, :QK_NOPE + QK_ROPE]
    blocks_b = [
        dw_uq3.reshape(N_DEV, Q_RANK // N_DEV, MLA_HEADS, QK_NOPE + QK_ROPE),
        dw_ukv.reshape(N_DEV, KV_RANK // N_DEV, MLA_HEADS, QK_NOPE + V_DIM),
        col_blocks(dw_sconv[:SSM_CONV]),
        col_blocks(jnp.concatenate([dwc_g, dwc_v], axis=1)[:FFN_CONV]),
    ]
    gpack_b = _pack(blocks_b, 1, pack_rows(blocks_b, 1), bf16)
    dh0, dg_mix_pre, gparts_b, gparts_in = _in_proj_dx(
        [d_q_c, d_kv_c, d_kpe_raw, dz, d_xbc, d_dt_raw], [w_q, w_kv, w_rope, w_z, w_xbc, w_dt],
        h0, norm_mix_pre, dh1, [gpack_b, col_blocks(dw_in).astype(bf16)], True)

    grad_x = dh0[N_META:n_real][None]
    meta_blocks = col_blocks(dh0[:N_META]).reshape(N_DEV, N_META * D // N_DEV // PACK_W, PACK_W)


    def adam_group(parts, grp, name):
        rows = parts.shape[1]
        packs = [_pack([a[None] for a in grp[k]], 1, rows, f32)[0] for k in ("w", "m", "v")]
        outs = _adamw(parts, *packs, name)
        shapes = [a.shape for a in grp["w"]]
        return [dict(zip(grp["names"], [t[0] for t in _unpack(b[None], shapes, 1)])) for b in outs]

    def adam_own_layout(parts, name, w, m, v):
        return [{name: t[None]} for t in _adamw(parts, w[0], m[0], v[0], "adamw_" + name)]

    sh_a = adam_group(gparts_a, grp_a, "adamw_sharded_a")
    sh_b = adam_group(gparts_b, grp_b, "adamw_sharded_b")
    sh_in = adam_own_layout(gparts_in, "w_in", w_in, m_w_in, v_w_in)
    sh_up = adam_own_layout(gparts_up, "w_up", w_up, m_w_up, v_w_up)

    dg_alog = da_heads[:, :SSM_HEADS] * a_neg
    repl_g = [dg_mix_pre, dg_mix_post, dg_ffn_pre, dg_ffn_post, dg_q, dg_kv, dg_attn_out, db_sconv,
              d_dt_bias[:, :SSM_HEADS], dg_alog, dd_heads[:, :SSM_HEADS], dg_ssm,
              jnp.concatenate([dbc_g, dbc_v], axis=1)]
    loss_vec = loss_part[:, :1]
    small_total = _round_up(sum(-(-int(np.prod(a.shape)) // PACK_W) for a in repl_g) + 1, 16)
    spack = _pack(repl_g + [loss_vec], 0, small_total, f32)
    gparts_meta, sparts = _exchange_tail([meta_blocks], [spack], "exchange_tail")
    sh_meta = adam_group(gparts_meta, grp_meta, "adamw_meta")
    loss_row, repl_out = _adamw_replicated(sparts, repl_w, repl_m, repl_v)
    loss = loss_row[0, 0]

    order = ["meta_tokens", "norm_mix_pre", "norm_mix_post", "norm_ffn_pre", "norm_ffn_post", "w_in", "q_a_norm",
             "w_uq", "kv_a_norm", "w_ukv", "attn_out_norm", "ssm_conv_w", "ssm_conv_b", "ssm_dt_bias", "ssm_A_log",
             "ssm_D", "ssm_norm", "w_out", "w_up", "ffn_conv_w", "ffn_conv_b", "w_down"]
    rp_names = ["norm_mix_pre", "norm_mix_post", "norm_ffn_pre", "norm_ffn_post", "q_a_norm", "kv_a_norm",
                "attn_out_norm", "ssm_conv_b", "ssm_dt_bias", "ssm_A_log", "ssm_D", "ssm_norm", "ffn_conv_b"]

    def lookup(k):
        d = {**sh_a[k], **sh_b[k], **sh_in[k], **sh_up[k], **sh_meta[k],
             **{n: four[k] for n, four in zip(rp_names, repl_out)}}
        return [d[n] for n in order]

    return (loss, grad_x, *lookup(0), *lookup(1), *lookup(2), *lookup(3))
```

```python
import math

import jax
import jax.numpy as jnp
import numpy as np
from jax import lax
from jax.experimental import pallas as pl
from jax.experimental.pallas import tpu as pltpu

f32 = jnp.float32
bf16 = jnp.bfloat16

D_MODEL = 1024
N_META = 16
MLA_HEADS = 8
QK_NOPE = 128
QK_ROPE = 64
V_DIM = 128
Q_RANK = 384
KV_RANK = 256
ROPE_THETA = 10000.0
SOFTMAX_SCALE = (QK_NOPE + QK_ROPE) ** -0.5
D_ATTN = MLA_HEADS * V_DIM
SSM_HEADS = 16
SSM_P = 64
SSM_GROUPS = 2
SSM_HPG = SSM_HEADS // SSM_GROUPS
SSM_N = 128
SSM_CONV = 4
CHUNK = 128
D_SSM = SSM_HEADS * SSM_P
D_BC = SSM_GROUPS * SSM_N
D_XBC = D_SSM + 2 * D_BC
D_FF = 2816
FFN_CONV = 3
EPS = 1e-6
QK_PAD = 256
N_DEV = 8

ADAM_LR = 0.001
ADAM_B1 = 0.9
ADAM_B2 = 0.999
ADAM_EPS = 1e-08
ADAM_WD = 0.01
ADAM_STEP = 10

LANES = 128
SUBLANES = 8
ROW_TILE = 256
VMEM_LIMIT = 56 * 1024 * 1024
PACK_W = 1024
PACK_ROW_TILE = 128
NEG = -1e30
LOG2E = math.log2(math.e)
LN2 = math.log(2.0)
Q_PRESCALE = SOFTMAX_SCALE * LOG2E

_MESH = pl.DeviceIdType.MESH


def _pick(n, prefs):
    for p in prefs:
        if n % p == 0:
            return p
    return n


def _rt(m):
    return _pick(m, (384, ROW_TILE))


def _cparams(sem):
    return pltpu.CompilerParams(dimension_semantics=sem, vmem_limit_bytes=VMEM_LIMIT)


def _row(spec_cols, tm):
    return pl.BlockSpec((tm, spec_cols), lambda i: (i, 0))


def _full(shape):
    nd = len(shape)
    return pl.BlockSpec(shape, lambda *a: (0,) * nd)


def _sigmoid(x):
    return 1.0 / (1.0 + jnp.exp(-x))


def _silu(x):
    return x * _sigmoid(x)


def _dsilu(x):
    s = _sigmoid(x)
    return s * (1.0 + x * (1.0 - s))


def _dot(a, b):
    return jnp.dot(a, b, preferred_element_type=f32)


def _dot_nt(a, b):
    return lax.dot_general(a, b, (((1,), (1,)), ((), ())), preferred_element_type=f32)


def _dot_tn(a, b):
    return lax.dot_general(a, b, (((0,), (0,)), ((), ())), preferred_element_type=f32)


def _dot_hi(a, b):
    return jnp.dot(a, b, precision=lax.Precision.HIGHEST, preferred_element_type=f32)


def _mm(a, b, trans_b, name):
    M, K = a.shape
    N = b.shape[0] if trans_b else b.shape[1]
    tm = _pick(M, (768, 512, 256))
    tn = _pick(N, (1408, 512, 384, 256, 128))

    def body(a_ref, b_ref, o_ref):
        o_ref[...] = _dot_nt(a_ref[...], b_ref[...]) if trans_b else _dot(a_ref[...], b_ref[...])

    b_spec = pl.BlockSpec((tn, K), lambda i, j: (j, 0)) if trans_b else pl.BlockSpec((K, tn), lambda i, j: (0, j))
    return pl.pallas_call(
        body, name=name, grid=(M // tm, N // tn),
        in_specs=[pl.BlockSpec((tm, K), lambda i, j: (i, 0)), b_spec],
        out_specs=pl.BlockSpec((tm, tn), lambda i, j: (i, j)),
        out_shape=jax.ShapeDtypeStruct((M, N), f32),
        compiler_params=_cparams(("parallel", "parallel")),
    )(a, b)


def _mm_tn(a, g, name):
    M, K = a.shape
    N = g.shape[1]
    tm = _pick(M, (768, 512, 256))
    tk = _pick(K, (1024, 1408, 512, 384, 256))
    tn = _pick(N, (1024, 1408, 512, 384, 256, 128))

    def body(a_ref, g_ref, o_ref):
        @pl.when(pl.program_id(2) == 0)
        def _():
            o_ref[...] = jnp.zeros_like(o_ref)

        o_ref[...] += _dot_tn(a_ref[...].astype(bf16), g_ref[...].astype(bf16))

    return pl.pallas_call(
        body, name=name, grid=(K // tk, N // tn, M // tm),
        in_specs=[pl.BlockSpec((tm, tk), lambda k, j, m: (m, k)),
                  pl.BlockSpec((tm, tn), lambda k, j, m: (m, j))],
        out_specs=pl.BlockSpec((tk, tn), lambda k, j, m: (k, j)),
        out_shape=jax.ShapeDtypeStruct((K, N), f32),
        compiler_params=_cparams(("parallel", "parallel", "arbitrary")),
    )(a, g)


def _rstd(x):
    return lax.rsqrt(jnp.mean(x * x, axis=-1, keepdims=True) + EPS)


def _rms_bwd_math(x, g, dy):
    r = _rstd(x)
    xh = x * r
    dn = dy * g
    dx = r * (dn - xh * jnp.mean(dn * xh, axis=-1, keepdims=True))
    return dx, dy * xh


def _in_proj(h0, g, weights):
    M, K = h0.shape
    tm = _rt(M)
    n = len(weights)
    widths = [int(w.shape[1]) for w in weights]

    def body(x_ref, g_ref, *refs):
        xv = x_ref[...]
        hn = (xv * _rstd(xv) * g_ref[...]).astype(bf16)
        refs[n][...] = hn
        for p in range(n):
            refs[n + 1 + p][...] = _dot(hn, refs[p][...])

    return pl.pallas_call(
        body, name="in_proj", grid=(M // tm,),
        in_specs=[_row(K, tm), _full((1, K))] + [_full((K, wd)) for wd in widths],
        out_specs=[_row(K, tm)] + [_row(wd, tm) for wd in widths],
        out_shape=[jax.ShapeDtypeStruct((M, K), bf16)] + [jax.ShapeDtypeStruct((M, wd), f32) for wd in widths],
        compiler_params=_cparams(("parallel",)),
    )(h0, g, *weights)


def _in_proj_dw(hn, grads):
    M, K = hn.shape
    tm = _rt(M)
    n = len(grads)
    widths = [int(gr.shape[1]) for gr in grads]

    def body(a_ref, *refs):
        @pl.when(pl.program_id(0) == 0)
        def _():
            for p in range(n):
                refs[n + p][...] = jnp.zeros_like(refs[n + p])

        a = a_ref[...]
        for p in range(n):
            refs[n + p][...] += _dot_tn(a, refs[p][...].astype(bf16))

    return pl.pallas_call(
        body, name="in_proj_dw", grid=(M // tm,),
        in_specs=[_row(K, tm)] + [_row(wd, tm) for wd in widths],
        out_specs=[_full((K, wd)) for wd in widths],
        out_shape=[jax.ShapeDtypeStruct((K, wd), f32) for wd in widths],
        compiler_params=_cparams(("arbitrary",)),
    )(hn, *grads)


def _in_proj_dx(grads, weights, h0, g, dh1, carried, scatter):
    M, K = h0.shape
    tm = _rt(M)
    nt = M // tm
    n = len(grads)
    nx = len(carried)
    widths = [int(w.shape[1]) for w in weights]

    def body(*refs):
        g_refs, w_refs = refs[:n], refs[n:2 * n]
        x_ref, gain_ref, r_ref = refs[2 * n:2 * n + 3]
        cin = refs[2 * n + 3:2 * n + 3 + nx]
        dx_ref, dg_ref = refs[2 * n + 3 + nx:2 * n + 5 + nx]
        cout = refs[2 * n + 5 + nx:2 * n + 5 + 2 * nx]
        i = pl.program_id(0)
        _hosted_exchange(cin, cout, refs[2 * n + 5 + 2 * nx:], scatter, i == 0, i == nt - 1)

        @pl.when(i == 0)
        def _():
            dg_ref[...] = jnp.zeros_like(dg_ref)

        d_hn = None
        for p in range(n):
            t = _dot_nt(g_refs[p][...].astype(bf16), w_refs[p][...])
            d_hn = t if d_hn is None else d_hn + t
        dx, dgp = _rms_bwd_math(x_ref[...], gain_ref[...], d_hn)
        dx_ref[...] = dx + r_ref[...]
        dg_ref[...] += jnp.sum(dgp, axis=0, keepdims=True)

    any_spec = pl.BlockSpec(memory_space=pl.ANY)
    return pl.pallas_call(
        body, name="in_proj_dx", grid=(nt,),
        in_specs=([_row(wd, tm) for wd in widths] + [_full((K, wd)) for wd in widths]
                  + [_row(K, tm), _full((1, K)), _row(K, tm)] + [any_spec] * nx),
        out_specs=[_row(K, tm), _full((1, K))] + [any_spec] * nx,
        out_shape=[jax.ShapeDtypeStruct((M, K), f32), jax.ShapeDtypeStruct((1, K), f32)]
        + _exchange_shapes(carried, scatter),
        scratch_shapes=_exchange_sems(nx),
        compiler_params=_cparams(("arbitrary",)),
    )(*grads, *weights, h0, g, dh1, *carried)


def _out_proj_resid(attn, ga, ssm, wa, ws, h0, g2, g3):
    M, K = h0.shape
    tm = _rt(M)

    def body(o_ref, ga_ref, s_ref, wa_ref, ws_ref, h_ref, g2_ref, g3_ref, an_ref, m_ref, h1_ref, hn_ref):
        ov = o_ref[...]
        an = (ov * _rstd(ov) * ga_ref[...]).astype(bf16)
        an_ref[...] = an
        mv = _dot(an, wa_ref[...]) + _dot(s_ref[...], ws_ref[...])
        m_ref[...] = mv
        h1 = h_ref[...] + mv * _rstd(mv) * g2_ref[...]
        h1_ref[...] = h1
        hn_ref[...] = (h1 * _rstd(h1) * g3_ref[...]).astype(bf16)

    return pl.pallas_call(
        body, name="out_proj_resid", grid=(M // tm,),
        in_specs=[_row(attn.shape[1], tm), _full((1, attn.shape[1])), _row(ssm.shape[1], tm),
                  _full(wa.shape), _full(ws.shape), _row(K, tm), _full((1, K)), _full((1, K))],
        out_specs=[_row(attn.shape[1], tm), _row(K, tm), _row(K, tm), _row(K, tm)],
        out_shape=[jax.ShapeDtypeStruct(attn.shape, bf16), jax.ShapeDtypeStruct((M, K), f32),
                   jax.ShapeDtypeStruct((M, K), f32), jax.ShapeDtypeStruct((M, K), bf16)],
        compiler_params=_cparams(("parallel",)),
    )(attn, ga, ssm, wa, ws, h0, g2, g3)


def _final(h1, act, wdown, g4, tgt, n_real):
    M, K = h1.shape
    F = act.shape[1]
    tm = _rt(M)
    nt = M // tm

    def body(h_ref, a_ref, w_ref, g_ref, t_ref, dh_ref, dd_ref, dg_ref, ls_ref, acc_ref):
        i = pl.program_id(0)

        @pl.when(i == 0)
        def _():
            dg_ref[...] = jnp.zeros_like(dg_ref)
            acc_ref[...] = jnp.zeros_like(acc_ref)

        dv = _dot(a_ref[...], w_ref[...])
        g = g_ref[...]
        r = _rstd(dv)
        n = dv * r
        h2 = h_ref[...] + n * g
        rows = i * tm + lax.broadcasted_iota(jnp.int32, (tm, 1), 0)
        mask = ((rows >= N_META) & (rows < n_real)).astype(f32)
        diff = (h2 - t_ref[...]) * mask
        acc_ref[...] += jnp.sum(diff * diff, axis=0, keepdims=True)
        dh = diff * (1.0 / K)
        dh_ref[...] = dh
        dn = dh * g
        dd_ref[...] = (r * (dn - n * jnp.mean(dn * n, axis=-1, keepdims=True))).astype(bf16)
        dg_ref[...] += jnp.sum(dh * n, axis=0, keepdims=True)

        @pl.when(i == nt - 1)
        def _():
            ls_ref[...] = jnp.zeros((1, LANES), f32) + jnp.sum(acc_ref[...]) * (0.5 / K)

    return pl.pallas_call(
        body, name="ffn_down_loss", grid=(nt,),
        in_specs=[_row(K, tm), _row(F, tm), _full((F, K)), _full((1, K)), _row(K, tm)],
        out_specs=[_row(K, tm), _row(K, tm), _full((1, K)), _full((1, LANES))],
        out_shape=[jax.ShapeDtypeStruct((M, K), f32), jax.ShapeDtypeStruct((M, K), bf16),
                   jax.ShapeDtypeStruct((1, K), f32), jax.ShapeDtypeStruct((1, LANES), f32)],
        scratch_shapes=[pltpu.VMEM((1, K), f32)],
        compiler_params=_cparams(("arbitrary",)),
    )(h1, act, wdown, g4, tgt)


def _mid_bwd(h1, g3, dup_g, dup_v, wup, dh2, mix, g2):
    M, K = h1.shape
    F = dup_g.shape[1]
    tm = ROW_TILE

    def body(h_ref, g3_ref, ag_ref, av_ref, w_ref, dh2_ref, m_ref, g2_ref, dh1_ref, dm_ref, dg3_ref, dg2_ref):
        @pl.when(pl.program_id(0) == 0)
        def _():
            dg3_ref[...] = jnp.zeros_like(dg3_ref)
            dg2_ref[...] = jnp.zeros_like(dg2_ref)

        d_hn2 = _dot_nt(ag_ref[...], w_ref[:, 0:F]) + _dot_nt(av_ref[...], w_ref[:, F:2 * F])
        dx, dgp = _rms_bwd_math(h_ref[...], g3_ref[...], d_hn2)
        dh1 = dh2_ref[...] + dx
        dh1_ref[...] = dh1
        dg3_ref[...] += jnp.sum(dgp, axis=0, keepdims=True)
        dm, dgp2 = _rms_bwd_math(m_ref[...], g2_ref[...], dh1)
        dm_ref[...] = dm.astype(bf16)
        dg2_ref[...] += jnp.sum(dgp2, axis=0, keepdims=True)

    return pl.pallas_call(
        body, name="ffn_up_dx_mid_bwd", grid=(M // tm,),
        in_specs=[_row(K, tm), _full((1, K)), _row(F, tm), _row(F, tm), _full((K, 2 * F)), _row(K, tm),
                  _row(K, tm), _full((1, K))],
        out_specs=[_row(K, tm), _row(K, tm), _full((1, K)), _full((1, K))],
        out_shape=[jax.ShapeDtypeStruct((M, K), f32), jax.ShapeDtypeStruct((M, K), bf16),
                   jax.ShapeDtypeStruct((1, K), f32), jax.ShapeDtypeStruct((1, K), f32)],
        compiler_params=_cparams(("arbitrary",)),
    )(h1, g3, dup_g, dup_v, wup, dh2, mix, g2)


HEADS_PER_STEP = 4
CONV_RB = 16


def _conv_block_taps(x_ref, halo, rb, lanes, kw):
    r0 = rb * CONV_RB
    if rb == 0:
        cat = jnp.concatenate([halo, x_ref[0:CONV_RB, lanes]], axis=0)
        first = SUBLANES - (kw - 1)
        return [cat[first + k:first + k + CONV_RB] for k in range(kw)]
    return [x_ref[r0 - (kw - 1) + k:r0 - (kw - 1) + k + CONV_RB, lanes] for k in range(kw)]


def _conv_weighted(taps, w, kw):
    u = None
    for k in range(kw):
        t = taps[k] * w[k:k + 1, :]
        u = t if u is None else u + t
    return u


def _conv_block_dx(du, nxt, w, kw):
    cat = jnp.concatenate([du, nxt], axis=0)
    return _conv_weighted([cat[kw - 1 - k:kw - 1 - k + CONV_RB] for k in range(kw)], w, kw)


def _prev_spec(tm, tc, col_of, row_axis, reversed_tiles=0):
    def imap(*ids):
        i = ids[row_axis]
        if reversed_tiles:
            i = reversed_tiles - 1 - i
        return (jnp.maximum(i * (tm // SUBLANES) - 1, 0), col_of(*ids))
    return pl.BlockSpec((SUBLANES, tc), imap)


def _ssm_conv_fwd(xbc, w, b):
    M, C = xbc.shape
    tm, tc, kw = ROW_TILE, C, SSM_CONV

    def body(x_ref, h_ref, w_ref, b_ref, o_ref):
        i = pl.program_id(0)

        def chunk(j, carry):
            lanes = pl.ds(pl.multiple_of(j * LANES, LANES), LANES)
            halo = jnp.where(i == 0, 0.0, h_ref[:, lanes])
            wv = w_ref[:, lanes]
            bv = b_ref[:, lanes]
            for rb in range(tm // CONV_RB):
                u = _conv_weighted(_conv_block_taps(x_ref, halo, rb, lanes, kw), wv, kw) + bv
                o_ref[rb * CONV_RB:(rb + 1) * CONV_RB, lanes] = _silu(u)
            return carry

        lax.fori_loop(0, tc // LANES, chunk, 0)

    return pl.pallas_call(
        body, name="ssm_conv_fwd", grid=(M // tm, C // tc),
        in_specs=[pl.BlockSpec((tm, tc), lambda i, j: (i, j)),
                  _prev_spec(tm, tc, lambda i, j: j, 0),
                  pl.BlockSpec((SUBLANES, tc), lambda i, j: (0, j)),
                  pl.BlockSpec((1, tc), lambda i, j: (0, j))],
        out_specs=pl.BlockSpec((tm, tc), lambda i, j: (i, j)),
        out_shape=jax.ShapeDtypeStruct((M, C), f32),
        compiler_params=_cparams(("parallel", "parallel")),
    )(xbc, xbc, w, b)


def _ssm_conv_bwd(xbc, w, b, dout):
    M, C = xbc.shape
    tm, tc, kw = ROW_TILE, C // 3, SSM_CONV
    nt = M // tm

    def body(x_ref, h_ref, w_ref, b_ref, d_ref, dx_ref, dw_ref, db_ref, nxt_ref):
        i = pl.program_id(1)

        @pl.when(i == 0)
        def _():
            dw_ref[...] = jnp.zeros_like(dw_ref)
            db_ref[...] = jnp.zeros_like(db_ref)
            nxt_ref[...] = jnp.zeros_like(nxt_ref)

        def chunk(j, carry):
            lanes = pl.ds(pl.multiple_of(j * LANES, LANES), LANES)
            halo = jnp.where(i == nt - 1, 0.0, h_ref[:, lanes])
            wv = w_ref[:, lanes]
            bv = b_ref[:, lanes]
            nxt = nxt_ref[:, lanes]
            db = jnp.zeros((CONV_RB, LANES), f32)
            dw = [jnp.zeros((CONV_RB, LANES), f32) for _ in range(kw)]
            for rb in reversed(range(tm // CONV_RB)):
                rows = slice(rb * CONV_RB, (rb + 1) * CONV_RB)
                taps = _conv_block_taps(x_ref, halo, rb, lanes, kw)
                du = d_ref[rows, lanes] * _dsilu(_conv_weighted(taps, wv, kw) + bv)
                db = db + du
                dw = [dw[k] + du * taps[k] for k in range(kw)]
                dx_ref[rows, lanes] = _conv_block_dx(du, nxt, wv, kw).astype(bf16)
                nxt = du[0:SUBLANES]
            nxt_ref[:, lanes] = nxt
            db_ref[:, lanes] += jnp.sum(db, axis=0, keepdims=True)
            for k in range(kw):
                dw_ref[k:k + 1, lanes] += jnp.sum(dw[k], axis=0, keepdims=True)
            return carry

        lax.fori_loop(0, tc // LANES, chunk, 0)

    tile = pl.BlockSpec((tm, tc), lambda j, i: (nt - 1 - i, j))
    return pl.pallas_call(
        body, name="ssm_conv_bwd", grid=(C // tc, nt),
        in_specs=[tile, _prev_spec(tm, tc, lambda j, i: j, 1, nt),
                  pl.BlockSpec((SUBLANES, tc), lambda j, i: (0, j)),
                  pl.BlockSpec((1, tc), lambda j, i: (0, j)), tile],
        out_specs=[tile, pl.BlockSpec((SUBLANES, tc), lambda j, i: (0, j)),
                   pl.BlockSpec((1, tc), lambda j, i: (0, j))],
        out_shape=[jax.ShapeDtypeStruct((M, C), bf16), jax.ShapeDtypeStruct((SUBLANES, C), f32),
                   jax.ShapeDtypeStruct((1, C), f32)],
        scratch_shapes=[pltpu.VMEM((SUBLANES, tc), f32)],
        compiler_params=_cparams(("parallel", "arbitrary")),
    )(xbc, xbc, w, b, dout)


def _ffn_gate_fwd(up, w, b):
    M = up.shape[0]
    tm, tc, kw = ROW_TILE, D_FF // 2, FFN_CONV
    nc = D_FF // tc

    def body(xg_ref, hg_ref, xv_ref, hv_ref, wg_ref, wv_ref, bg_ref, bv_ref, o_ref):
        i = pl.program_id(0)

        def chunk(j, carry):
            lanes = pl.ds(pl.multiple_of(j * LANES, LANES), LANES)
            halo_g = jnp.where(i == 0, 0.0, hg_ref[:, lanes])
            halo_v = jnp.where(i == 0, 0.0, hv_ref[:, lanes])
            wg, wv = wg_ref[:, lanes], wv_ref[:, lanes]
            bg, bv = bg_ref[:, lanes], bv_ref[:, lanes]
            for rb in range(tm // CONV_RB):
                ug = _conv_weighted(_conv_block_taps(xg_ref, halo_g, rb, lanes, kw), wg, kw) + bg
                uv = _conv_weighted(_conv_block_taps(xv_ref, halo_v, rb, lanes, kw), wv, kw) + bv
                o_ref[rb * CONV_RB:(rb + 1) * CONV_RB, lanes] = (_silu(ug) * uv).astype(bf16)
            return carry

        lax.fori_loop(0, tc // LANES, chunk, 0)

    return pl.pallas_call(
        body, name="ffn_gate_fwd", grid=(M // tm, nc),
        in_specs=[pl.BlockSpec((tm, tc), lambda i, j: (i, j)),
                  _prev_spec(tm, tc, lambda i, j: j, 0),
                  pl.BlockSpec((tm, tc), lambda i, j: (i, j + nc)),
                  _prev_spec(tm, tc, lambda i, j: j + nc, 0),
                  pl.BlockSpec((SUBLANES, tc), lambda i, j: (0, j)),
                  pl.BlockSpec((SUBLANES, tc), lambda i, j: (0, j + nc)),
                  pl.BlockSpec((1, tc), lambda i, j: (0, j)),
                  pl.BlockSpec((1, tc), lambda i, j: (0, j + nc))],
        out_specs=pl.BlockSpec((tm, tc), lambda i, j: (i, j)),
        out_shape=jax.ShapeDtypeStruct((M, D_FF), bf16),
        compiler_params=_cparams(("parallel", "parallel")),
    )(up, up, up, up, w, w, b, b)


def _ffn_gate_bwd(up, w, b, d_down, wdown):
    M = up.shape[0]
    K = d_down.shape[1]
    tm, tc, kw = ROW_TILE, D_FF // 2, FFN_CONV
    nc = D_FF // tc
    nt = M // tm

    def body(xg_ref, hg_ref, xv_ref, hv_ref, wg_ref, wv_ref, bg_ref, bv_ref, dd_ref, wd_ref,
             dxg_ref, dxv_ref, dwg_ref, dwv_ref, dbg_ref, dbv_ref, ng_ref, nv_ref, d_ref):
        i = pl.program_id(1)

        @pl.when(i == 0)
        def _():
            for r in (dwg_ref, dwv_ref, dbg_ref, dbv_ref, ng_ref, nv_ref):
                r[...] = jnp.zeros_like(r)

        d_ref[...] = _dot_nt(dd_ref[...], wd_ref[...])

        def chunk(j, carry):
            lanes = pl.ds(pl.multiple_of(j * LANES, LANES), LANES)
            halo_g = jnp.where(i == nt - 1, 0.0, hg_ref[:, lanes])
            halo_v = jnp.where(i == nt - 1, 0.0, hv_ref[:, lanes])
            wg, wv = wg_ref[:, lanes], wv_ref[:, lanes]
            bg, bv = bg_ref[:, lanes], bv_ref[:, lanes]
            nxt_g, nxt_v = ng_ref[:, lanes], nv_ref[:, lanes]
            zero = jnp.zeros((CONV_RB, LANES), f32)
            dbg, dbv = zero, zero
            dwg = [zero for _ in range(kw)]
            dwv = [zero for _ in range(kw)]
            for rb in reversed(range(tm // CONV_RB)):
                rows = slice(rb * CONV_RB, (rb + 1) * CONV_RB)
                tg = _conv_block_taps(xg_ref, halo_g, rb, lanes, kw)
                tv = _conv_block_taps(xv_ref, halo_v, rb, lanes, kw)
                ug = _conv_weighted(tg, wg, kw) + bg
                uv = _conv_weighted(tv, wv, kw) + bv
                sg = _sigmoid(ug)
                da = d_ref[rows, lanes]
                dug = da * uv * (sg * (1.0 + ug * (1.0 - sg)))
                duv = da * (ug * sg)
                dbg = dbg + dug
                dbv = dbv + duv
                dwg = [dwg[k] + dug * tg[k] for k in range(kw)]
                dwv = [dwv[k] + duv * tv[k] for k in range(kw)]
                dxg_ref[rows, lanes] = _conv_block_dx(dug, nxt_g, wg, kw).astype(bf16)
                dxv_ref[rows, lanes] = _conv_block_dx(duv, nxt_v, wv, kw).astype(bf16)
                nxt_g, nxt_v = dug[0:SUBLANES], duv[0:SUBLANES]
            ng_ref[:, lanes] = nxt_g
            nv_ref[:, lanes] = nxt_v
            dbg_ref[:, lanes] += jnp.sum(dbg, axis=0, keepdims=True)
            dbv_ref[:, lanes] += jnp.sum(dbv, axis=0, keepdims=True)
            for k in range(kw):
                dwg_ref[k:k + 1, lanes] += jnp.sum(dwg[k], axis=0, keepdims=True)
                dwv_ref[k:k + 1, lanes] += jnp.sum(dwv[k], axis=0, keepdims=True)
            return carry

        lax.fori_loop(0, tc // LANES, chunk, 0)

    tile_g = pl.BlockSpec((tm, tc), lambda j, i: (nt - 1 - i, j))
    tile_v = pl.BlockSpec((tm, tc), lambda j, i: (nt - 1 - i, j + nc))
    ext = pltpu.VMEM((SUBLANES, tc), f32)
    return pl.pallas_call(
        body, name="ffn_gate_bwd", grid=(nc, nt),
        in_specs=[tile_g, _prev_spec(tm, tc, lambda j, i: j, 1, nt),
                  tile_v, _prev_spec(tm, tc, lambda j, i: j + nc, 1, nt),
                  pl.BlockSpec((SUBLANES, tc), lambda j, i: (0, j)),
                  pl.BlockSpec((SUBLANES, tc), lambda j, i: (0, j + nc)),
                  pl.BlockSpec((1, tc), lambda j, i: (0, j)),
                  pl.BlockSpec((1, tc), lambda j, i: (0, j + nc)),
                  pl.BlockSpec((tm, K), lambda j, i: (nt - 1 - i, 0)),
                  pl.BlockSpec((tc, K), lambda j, i: (j, 0))],
        out_specs=[tile_g, tile_g,
                   pl.BlockSpec((SUBLANES, tc), lambda j, i: (0, j)),
                   pl.BlockSpec((SUBLANES, tc), lambda j, i: (0, j)),
                   pl.BlockSpec((1, tc), lambda j, i: (0, j)),
                   pl.BlockSpec((1, tc), lambda j, i: (0, j))],
        out_shape=[jax.ShapeDtypeStruct((M, D_FF), bf16), jax.ShapeDtypeStruct((M, D_FF), bf16),
                   jax.ShapeDtypeStruct((SUBLANES, D_FF), f32), jax.ShapeDtypeStruct((SUBLANES, D_FF), f32),
                   jax.ShapeDtypeStruct((1, D_FF), f32), jax.ShapeDtypeStruct((1, D_FF), f32)],
        scratch_shapes=[ext, ext, pltpu.VMEM((tm, tc), f32)],
        compiler_params=_cparams(("parallel", "arbitrary")),
    )(up, up, up, up, w, w, b, b, d_down, wdown)


def _rope_apply(blk, cos, sin):
    lane = lax.broadcasted_iota(jnp.int32, blk.shape, 1)
    half = QK_ROPE // 2
    partner = jnp.where(lane < half, pltpu.roll(blk, LANES - half, 1), pltpu.roll(blk, half, 1))
    return blk * cos + partner * sin


def _rope_unapply(d, cos, sin):
    t = d * sin
    lane = lax.broadcasted_iota(jnp.int32, d.shape, 1)
    half = QK_ROPE // 2
    partner = jnp.where(lane < half, pltpu.roll(t, LANES - half, 1), pltpu.roll(t, half, 1))
    return d * cos + partner


def _up_q_rope(q_c, g, wuq, cos, sin):
    M, K = q_c.shape
    tm = _pick(M, (768, 512, 256))

    hs = HEADS_PER_STEP

    def body(x_ref, g_ref, b_ref, c_ref, s_ref, a_ref, o_ref):
        xv = x_ref[...]
        a = (xv * _rstd(xv) * g_ref[...]).astype(bf16)
        a_ref[...] = a
        r = _dot(a, b_ref[...]) * Q_PRESCALE
        c, s = c_ref[...], s_ref[...]
        for u in range(hs):
            o_ref[u, :, 0:QK_NOPE] = r[:, u * QK_PAD:u * QK_PAD + QK_NOPE].astype(bf16)
            o_ref[u, :, QK_NOPE:QK_PAD] = _rope_apply(r[:, u * QK_PAD + QK_NOPE:(u + 1) * QK_PAD], c, s).astype(bf16)

    return pl.pallas_call(
        body, name="up_q_rope", grid=(M // tm, MLA_HEADS // hs),
        in_specs=[pl.BlockSpec((tm, K), lambda i, h: (i, 0)),
                  pl.BlockSpec((1, K), lambda i, h: (0, 0)),
                  pl.BlockSpec((K, hs * QK_PAD), lambda i, h: (0, h)),
                  pl.BlockSpec((tm, LANES), lambda i, h: (i, 0)),
                  pl.BlockSpec((tm, LANES), lambda i, h: (i, 0))],
        out_specs=[pl.BlockSpec((tm, K), lambda i, h: (i, 0)),
                   pl.BlockSpec((hs, tm, QK_PAD), lambda i, h: (h, i, 0))],
        out_shape=[jax.ShapeDtypeStruct((M, K), bf16), jax.ShapeDtypeStruct((MLA_HEADS, M, QK_PAD), bf16)],
        compiler_params=_cparams(("parallel", "arbitrary")),
    )(q_c, g, wuq, cos, sin)


def _up_kv_rope(kv_c, g, wukv, kpe_raw, cos, sin):
    M, K = kv_c.shape
    tm = _pick(M, (768, 512, 256))

    hs = HEADS_PER_STEP
    w = QK_NOPE + V_DIM

    def body(x_ref, g_ref, b_ref, pe_ref, c_ref, s_ref, a_ref, k_ref, v_ref):
        xv = x_ref[...]
        a = (xv * _rstd(xv) * g_ref[...]).astype(bf16)
        a_ref[...] = a
        r = _dot(a, b_ref[...])
        pe = _rope_apply(pe_ref[...], c_ref[...], s_ref[...]).astype(bf16)
        for u in range(hs):
            k_ref[u, :, 0:QK_NOPE] = r[:, u * w:u * w + QK_NOPE].astype(bf16)
            k_ref[u, :, QK_NOPE:QK_PAD] = pe
            v_ref[u] = r[:, u * w + QK_NOPE:(u + 1) * w].astype(bf16)

    return pl.pallas_call(
        body, name="up_kv_rope", grid=(M // tm, MLA_HEADS // hs),
        in_specs=[pl.BlockSpec((tm, K), lambda i, h: (i, 0)),
                  pl.BlockSpec((1, K), lambda i, h: (0, 0)),
                  pl.BlockSpec((K, hs * w), lambda i, h: (0, h)),
                  pl.BlockSpec((tm, LANES), lambda i, h: (i, 0)),
                  pl.BlockSpec((tm, LANES), lambda i, h: (i, 0)),
                  pl.BlockSpec((tm, LANES), lambda i, h: (i, 0))],
        out_specs=[pl.BlockSpec((tm, K), lambda i, h: (i, 0)),
                   pl.BlockSpec((hs, tm, QK_PAD), lambda i, h: (h, i, 0)),
                   pl.BlockSpec((hs, tm, V_DIM), lambda i, h: (h, i, 0))],
        out_shape=[jax.ShapeDtypeStruct((M, K), bf16), jax.ShapeDtypeStruct((MLA_HEADS, M, QK_PAD), bf16),
                   jax.ShapeDtypeStruct((MLA_HEADS, M, V_DIM), bf16)],
        compiler_params=_cparams(("parallel", "arbitrary")),
    )(kv_c, g, wukv, kpe_raw, cos, sin)


def _latent_bwd(d_full_sc, w_ref, x_ref, g_ref, a_ref, dx_ref, dg_ref, dw_ref):
    d_full = d_full_sc[...]
    dx, dgp = _rms_bwd_math(x_ref[...], g_ref[...], _dot_nt(d_full, w_ref[...]))
    dx_ref[...] = dx.astype(bf16)
    dg_ref[...] += jnp.sum(dgp, axis=0, keepdims=True)
    dw_ref[...] += _dot_tn(a_ref[...], d_full)


def _latent_bwd_call(body, name, head_inputs, head_specs, cos, sin, w, x, g, a, extra_out_specs, extra_out_shape):
    M, K = x.shape
    tm = _rt(M)
    N = w.shape[1]
    return pl.pallas_call(
        body, name=name, grid=(M // tm,),
        in_specs=head_specs + [_row(LANES, tm), _row(LANES, tm), _full((K, N)), _row(K, tm), _full((1, K)),
                               _row(K, tm)],
        out_specs=[_row(K, tm), _full((1, K)), _full((K, N))] + extra_out_specs,
        out_shape=[jax.ShapeDtypeStruct((M, K), bf16), jax.ShapeDtypeStruct((1, K), f32),
                   jax.ShapeDtypeStruct((K, N), f32)] + extra_out_shape,
        scratch_shapes=[pltpu.VMEM((tm, N), bf16)],
        compiler_params=_cparams(("arbitrary",)),
    )(*head_inputs, cos, sin, w, x, g, a)


def _q_branch_bwd(dq, cos, sin, wuq, q_c, g, qn):
    tm = _rt(q_c.shape[0])

    def body(d_ref, c_ref, s_ref, w_ref, x_ref, g_ref, a_ref, dx_ref, dg_ref, dw_ref, full_sc):
        @pl.when(pl.program_id(0) == 0)
        def _():
            dg_ref[...] = jnp.zeros_like(dg_ref)
            dw_ref[...] = jnp.zeros_like(dw_ref)

        c, s = c_ref[...], s_ref[...]
        for h in range(MLA_HEADS):
            full_sc[:, h * QK_PAD:h * QK_PAD + QK_NOPE] = (d_ref[h, :, 0:QK_NOPE] * SOFTMAX_SCALE).astype(bf16)
            full_sc[:, h * QK_PAD + QK_NOPE:(h + 1) * QK_PAD] = (_rope_unapply(
                d_ref[h, :, QK_NOPE:QK_PAD], c, s) * SOFTMAX_SCALE).astype(bf16)
        _latent_bwd(full_sc, w_ref, x_ref, g_ref, a_ref, dx_ref, dg_ref, dw_ref)

    return _latent_bwd_call(body, "q_branch_bwd", [dq],
                            [pl.BlockSpec((MLA_HEADS, tm, QK_PAD), lambda i: (0, i, 0))],
                            cos, sin, wuq, q_c, g, qn, [], [])


def _kv_branch_bwd(dk, dv, cos, sin, wukv, kv_c, g, kvn):
    M = kv_c.shape[0]
    tm = _rt(M)
    w = QK_NOPE + V_DIM

    def body(dk_ref, dv_ref, c_ref, s_ref, w_ref, x_ref, g_ref, a_ref, dx_ref, dg_ref, dw_ref, pe_ref, full_sc):
        @pl.when(pl.program_id(0) == 0)
        def _():
            dg_ref[...] = jnp.zeros_like(dg_ref)
            dw_ref[...] = jnp.zeros_like(dw_ref)

        pe = None
        for h in range(MLA_HEADS):
            full_sc[:, h * w:h * w + QK_NOPE] = dk_ref[h, :, 0:QK_NOPE].astype(bf16)
            full_sc[:, h * w + QK_NOPE:(h + 1) * w] = dv_ref[h].astype(bf16)
            t = dk_ref[h, :, QK_NOPE:QK_PAD]
            pe = t if pe is None else pe + t
        pe_ref[...] = _rope_unapply(pe, c_ref[...], s_ref[...])
        _latent_bwd(full_sc, w_ref, x_ref, g_ref, a_ref, dx_ref, dg_ref, dw_ref)

    return _latent_bwd_call(body, "kv_branch_bwd", [dk, dv],
                            [pl.BlockSpec((MLA_HEADS, tm, QK_PAD), lambda i: (0, i, 0)),
                             pl.BlockSpec((MLA_HEADS, tm, V_DIM), lambda i: (0, i, 0))],
                            cos, sin, wukv, kv_c, g, kvn, [_row(LANES, tm)],
                            [jax.ShapeDtypeStruct((M, LANES), f32)])


def _attn_tile(M):
    return 768 if (M % 768 == 0 and M >= 4 * 768) else ROW_TILE


def _col_to_row(col):
    return col.T[0:1, :]


def _hosted_exchange(refs_in, refs_out, sems, scatter, first, last):
    copies = _exchange_copies(refs_in, refs_out, *sems, scatter)

    @pl.when(first)
    def _():
        for cp in copies:
            cp.start()

    @pl.when(last)
    def _():
        for cp in copies:
            cp.wait()


def _flash_fwd(q, k, v, carried, scatter):
    H, M, _ = q.shape
    T = _attn_tile(M)
    nq = M // T
    nx = len(carried)

    def body(*refs):
        q_ref, k_ref, v_ref = refs[:3]
        o_ref, lse_ref = refs[3 + nx:5 + nx]
        sa_ref, sb_ref, m_sc, l_sc, acc_sc = refs[5 + 2 * nx:10 + 2 * nx]
        h = pl.program_id(0)
        i = pl.program_id(1)
        _hosted_exchange(refs[3:3 + nx], refs[5 + nx:5 + 2 * nx], refs[10 + 2 * nx:], scatter,
                         (h == 0) & (i == 0), (h == H - 1) & (i == nq - 1))
        qv = q_ref[0]
        m_sc[...] = jnp.full_like(m_sc, NEG)
        l_sc[...] = jnp.zeros_like(l_sc)
        acc_sc[...] = jnp.zeros_like(acc_sc)

        def scores(j, s_ref):
            off = pl.multiple_of(j * T, T)
            s_ref[...] = _dot_nt(qv, k_ref[0, pl.ds(off, T), :])

        def softmax_pv(j, s_ref, masked):
            off = pl.multiple_of(j * T, T)
            s = s_ref[...]
            if masked:
                r = lax.broadcasted_iota(jnp.int32, (T, T), 0)
                c = lax.broadcasted_iota(jnp.int32, (T, T), 1)
                s = jnp.where(r >= c, s, NEG)
            m_prev = m_sc[...]
            m_new = jnp.maximum(m_prev, jnp.max(s, axis=1, keepdims=True))
            alpha = jnp.exp2(m_prev - m_new)
            p = jnp.exp2(s - m_new[:, 0:1])
            l_sc[...] = alpha * l_sc[...] + jnp.sum(p, axis=1, keepdims=True)
            acc_sc[...] = alpha * acc_sc[...] + _dot(p.astype(bf16), v_ref[0, pl.ds(off, T), :])
            m_sc[...] = m_new

        scores(0, sa_ref)

        def pair(jj, c):
            j0 = 2 * jj
            scores(j0 + 1, sb_ref)
            softmax_pv(j0, sa_ref, False)
            scores(j0 + 2, sa_ref)
            softmax_pv(j0 + 1, sb_ref, False)
            return c

        lax.fori_loop(0, i // 2, pair, 0)

        @pl.when(i % 2 == 0)
        def _():
            softmax_pv(i, sa_ref, True)

        @pl.when(i % 2 == 1)
        def _():
            scores(i, sb_ref)
            softmax_pv(i - 1, sa_ref, False)
            softmax_pv(i, sb_ref, True)

        l = l_sc[...]
        o_ref[...] = acc_sc[...] / l
        lse_ref[0, 0] = _col_to_row(m_sc[...] + jnp.log2(l))

    any_spec = pl.BlockSpec(memory_space=pl.ANY)
    return pl.pallas_call(
        body, name="flash_fwd", grid=(H, nq),
        in_specs=[pl.BlockSpec((1, T, QK_PAD), lambda h, i: (h, i, 0)),
                  pl.BlockSpec((1, M, QK_PAD), lambda h, i: (h, 0, 0)),
                  pl.BlockSpec((1, M, V_DIM), lambda h, i: (h, 0, 0))] + [any_spec] * nx,
        out_specs=[pl.BlockSpec((T, V_DIM), lambda h, i: (i, h)),
                   pl.BlockSpec((1, 1, 1, T), lambda h, i: (h, i, 0, 0))] + [any_spec] * nx,
        out_shape=[jax.ShapeDtypeStruct((M, H * V_DIM), f32),
                   jax.ShapeDtypeStruct((H, nq, 1, T), f32)] + _exchange_shapes(carried, scatter),
        scratch_shapes=[pltpu.VMEM((T, T), f32), pltpu.VMEM((T, T), f32),
                        pltpu.VMEM((T, LANES), f32), pltpu.VMEM((T, LANES), f32),
                        pltpu.VMEM((T, V_DIM), f32)] + _exchange_sems(nx),
        compiler_params=_cparams(("arbitrary", "arbitrary")),
    )(q, k, v, *carried)


def _attn_out_bwd(o, g, d_mix, wa):
    M, K = o.shape
    H = MLA_HEADS
    T = _attn_tile(M)

    def body(o_ref, g_ref, dm_ref, w_ref, dh_ref, dl_ref, dg_ref):
        @pl.when(pl.program_id(0) == 0)
        def _():
            dg_ref[...] = jnp.zeros_like(dg_ref)

        ov = o_ref[...]
        do, dgp = _rms_bwd_math(ov, g_ref[...], _dot_nt(dm_ref[...], w_ref[...]))
        dg_ref[...] += jnp.sum(dgp, axis=0, keepdims=True)
        for h in range(H):
            sl = slice(h * V_DIM, (h + 1) * V_DIM)
            doh = do[:, sl]
            dh_ref[h] = doh.astype(bf16)
            col = jnp.sum(ov[:, sl] * doh, axis=1, keepdims=True) + jnp.zeros((T, LANES), f32)
            dl_ref[h, 0] = _col_to_row(col)

    return pl.pallas_call(
        body, name="attn_out_bwd", grid=(M // T,),
        in_specs=[_row(K, T), _full((1, K)), _row(d_mix.shape[1], T), _full(wa.shape)],
        out_specs=[pl.BlockSpec((H, T, V_DIM), lambda i: (0, i, 0)),
                   pl.BlockSpec((H, 1, 1, T), lambda i: (0, i, 0, 0)),
                   _full((1, K))],
        out_shape=[jax.ShapeDtypeStruct((H, M, V_DIM), bf16),
                   jax.ShapeDtypeStruct((H, M // T, 1, T), f32),
                   jax.ShapeDtypeStruct((1, K), f32)],
        compiler_params=_cparams(("arbitrary",)),
    )(o, g, d_mix, wa)


def _flash_bwd(q, k, v, do, lse, delta, carried, scatter):
    H, M, _ = q.shape
    T = _attn_tile(M)
    nq = M // T
    nx = len(carried)

    def body(*refs):
        q_ref, do_ref, lse_ref, dl_ref, k_ref, v_ref = refs[:6]
        dq_ref, dk_ref, dv_ref = refs[6 + nx:9 + nx]
        dk_sc, dv_sc = refs[9 + 2 * nx:11 + 2 * nx]
        j = pl.program_id(1)
        _hosted_exchange(refs[6:6 + nx], refs[9 + nx:9 + 2 * nx], refs[11 + 2 * nx:], scatter,
                         (pl.program_id(0) == 0) & (j == 0), (pl.program_id(0) == H - 1) & (j == nq - 1))

        @pl.when(j == 0)
        def _():
            dq_ref[...] = jnp.zeros_like(dq_ref)

        kt = k_ref[0]
        vt = v_ref[0]
        dk_sc[...] = jnp.zeros_like(dk_sc)
        dv_sc[...] = jnp.zeros_like(dv_sc)

        def step(i, masked):
            off = pl.multiple_of(i * T, T)
            qt = q_ref[0, pl.ds(off, T), :]
            dot_ = do_ref[0, pl.ds(off, T), :]
            st = _dot_nt(kt, qt)
            if masked:
                r = lax.broadcasted_iota(jnp.int32, (T, T), 0)
                c = lax.broadcasted_iota(jnp.int32, (T, T), 1)
                st = jnp.where(c >= r, st, NEG)
            pt = jnp.exp2(st - lse_ref[0, i])
            dv_sc[...] += _dot(pt.astype(bf16), dot_)
            dpt = _dot_nt(vt, dot_)
            dst = (pt * (dpt - dl_ref[0, i])).astype(bf16)
            dk_sc[...] += _dot(dst, qt)
            dq_ref[0, pl.ds(off, T), :] += _dot_tn(dst, kt)

        step(j, True)

        def loop_body(i, c):
            step(i, False)
            return c

        lax.fori_loop(j + 1, nq, loop_body, 0)
        dk_ref[0] = dk_sc[...] * LN2
        dv_ref[0] = dv_sc[...]

    any_spec = pl.BlockSpec(memory_space=pl.ANY)
    return pl.pallas_call(
        body, name="flash_bwd", grid=(H, nq),
        in_specs=[pl.BlockSpec((1, M, QK_PAD), lambda h, j: (h, 0, 0)),
                  pl.BlockSpec((1, M, V_DIM), lambda h, j: (h, 0, 0)),
                  pl.BlockSpec((1, nq, 1, T), lambda h, j: (h, 0, 0, 0)),
                  pl.BlockSpec((1, nq, 1, T), lambda h, j: (h, 0, 0, 0)),
                  pl.BlockSpec((1, T, QK_PAD), lambda h, j: (h, j, 0)),
                  pl.BlockSpec((1, T, V_DIM), lambda h, j: (h, j, 0))] + [any_spec] * nx,
        out_specs=[pl.BlockSpec((1, M, QK_PAD), lambda h, j: (h, 0, 0)),
                   pl.BlockSpec((1, T, QK_PAD), lambda h, j: (h, j, 0)),
                   pl.BlockSpec((1, T, V_DIM), lambda h, j: (h, j, 0))] + [any_spec] * nx,
        out_shape=[jax.ShapeDtypeStruct((H, M, QK_PAD), f32),
                   jax.ShapeDtypeStruct((H, M, QK_PAD), f32),
                   jax.ShapeDtypeStruct((H, M, V_DIM), f32)] + _exchange_shapes(carried, scatter),
        scratch_shapes=[pltpu.VMEM((T, QK_PAD), f32), pltpu.VMEM((T, V_DIM), f32)] + _exchange_sems(nx),
        compiler_params=_cparams(("arbitrary", "arbitrary")),
    )(q, do, lse, delta, k, v, *carried)


def _dt_fwd(dt_raw, bias, expand):
    M = dt_raw.shape[0]
    tm = _rt(M)

    def body(x_ref, b_ref, e_ref, o_ref, oe_ref):
        u = x_ref[...] + b_ref[...]
        sp = jnp.maximum(u, 0.0) + jnp.log(1.0 + jnp.exp(-jnp.abs(u)))
        lane = lax.broadcasted_iota(jnp.int32, u.shape, 1)
        dtp = jnp.where(lane < SSM_HEADS, sp, 0.0)
        o_ref[...] = dtp
        oe_ref[...] = _dot_hi(dtp, e_ref[...])

    return pl.pallas_call(
        body, name="dt_fwd", grid=(M // tm,),
        in_specs=[_row(LANES, tm), _full((1, LANES)), _full((LANES, D_SSM))],
        out_specs=[_row(LANES, tm), _row(D_SSM, tm)],
        out_shape=[jax.ShapeDtypeStruct((M, LANES), f32), jax.ShapeDtypeStruct((M, D_SSM), f32)],
        compiler_params=_cparams(("parallel",)),
    )(dt_raw, bias, expand)


def _dt_bwd(dt_raw, bias, ddt):
    M = dt_raw.shape[0]
    tm = _rt(M)

    def body(x_ref, b_ref, d_ref, o_ref, db_ref):
        @pl.when(pl.program_id(0) == 0)
        def _():
            db_ref[...] = jnp.zeros_like(db_ref)

        u = x_ref[...] + b_ref[...]
        lane = lax.broadcasted_iota(jnp.int32, u.shape, 1)
        g = jnp.where(lane < SSM_HEADS, d_ref[...] * _sigmoid(u), 0.0)
        o_ref[...] = g
        db_ref[...] += jnp.sum(g, axis=0, keepdims=True)

    return pl.pallas_call(
        body, name="dt_bwd", grid=(M // tm,),
        in_specs=[_row(LANES, tm), _full((1, LANES)), _row(LANES, tm)],
        out_specs=[_row(LANES, tm), _full((1, LANES))],
        out_shape=[jax.ShapeDtypeStruct((M, LANES), f32), jax.ShapeDtypeStruct((1, LANES), f32)],
        compiler_params=_cparams(("arbitrary",)),
    )(dt_raw, bias, ddt)


SSM_GW = SSM_HPG * SSM_P
SSM_PAIRS = SSM_GW // LANES


def _ssd_common(dte_ref, dtt_ref, ae_ref, acol_ref):
    Q = CHUNK
    r = lax.broadcasted_iota(jnp.int32, (Q, Q), 0)
    c = lax.broadcasted_iota(jnp.int32, (Q, Q), 1)
    causal = r >= c
    anti = c >= r
    tril = causal.astype(f32)
    triu = anti.astype(f32)
    dt_e = dte_ref[...]
    cs_e = _dot_hi(tril, dt_e * ae_ref[...])
    cst = _dot_hi(dtt_ref[...] * acol_ref[...], triu)
    cs_last = cs_e[Q - 1:Q, :]
    return causal, anti, triu, dt_e, cs_e, cst, jnp.exp(cs_e), jnp.exp(cs_last - cs_e), jnp.exp(cs_last)


def _half_masks():
    lane = lax.broadcasted_iota(jnp.int32, (CHUNK, LANES), 1)
    lo = lane < SSM_P
    return lo, jnp.logical_not(lo)


def _ssd_fwd(xbc_c, dt_e, dtt, a_e, a_col, z, d_exp, g_ssm):
    M = xbc_c.shape[0]
    Q = CHUNK
    nch = M // Q
    gw = D_SSM // SSM_GROUPS

    def body(x_ref, dte_ref, dtt_ref, ae_ref, acol_ref, z_ref, dexp_ref, gn_ref, y_ref, hin_ref, o_ref, ht_sc):
        @pl.when(pl.program_id(0) == 0)
        def _():
            ht_sc[...] = jnp.zeros_like(ht_sc)

        causal, _, _, dt_e, cs_e, cst, ecs_e, dte_e, elast_e = _ssd_common(dte_ref, dtt_ref, ae_ref, acol_ref)
        halves = _half_masks()
        for g in range(SSM_GROUPS):
            g0 = g * SSM_GW
            bg = x_ref[:, D_SSM + g * SSM_N:D_SSM + (g + 1) * SSM_N]
            cg = x_ref[:, D_SSM + D_BC + g * SSM_N:D_SSM + D_BC + (g + 1) * SSM_N]
            bg_b = bg.astype(bf16)
            cg_b = cg.astype(bf16)
            cb = _dot_nt(cg_b, bg_b)
            bgt_b = bg.T.astype(bf16)
            xdt_g = x_ref[:, g0:g0 + SSM_GW] * dt_e[:, g0:g0 + SSM_GW]
            ht = ht_sc[g]
            hin_ref[0, g] = ht
            y_off = _dot(cg_b, ht.astype(bf16)) * ecs_e[:, g0:g0 + SSM_GW]
            for pr in range(SSM_PAIRS):
                p0 = pr * LANES
                xdt_p = xdt_g[:, p0:p0 + LANES]
                acc = y_off[:, p0:p0 + LANES]
                for half in range(2):
                    h = g * SSM_HPG + pr * 2 + half
                    seg = cs_e[:, h * SSM_P:h * SSM_P + 1] - cst[h:h + 1, :]
                    lm = jnp.exp(jnp.where(causal, seg, -jnp.inf))
                    xm = jnp.where(halves[half], xdt_p, 0.0).astype(bf16)
                    acc = acc + _dot((cb * lm).astype(bf16), xm)
                y_ref[:, g0 + p0:g0 + p0 + LANES] = acc
            st = _dot(bgt_b, (xdt_g * dte_e[:, g0:g0 + SSM_GW]).astype(bf16))
            ht_sc[g] = ht * elast_e[:, g0:g0 + SSM_GW] + st
        yg = (y_ref[...] + dexp_ref[...] * x_ref[:, 0:D_SSM]) * _silu(z_ref[...])
        for gi in range(SSM_GROUPS):
            blk = yg[:, gi * gw:(gi + 1) * gw]
            o_ref[:, gi * gw:(gi + 1) * gw] = (blk * _rstd(blk) * gn_ref[:, gi * gw:(gi + 1) * gw]).astype(bf16)

    chunk_rows = pl.BlockSpec((Q, D_SSM), lambda c: (c, 0))
    return pl.pallas_call(
        body, name="ssd_fwd", grid=(nch,),
        in_specs=[pl.BlockSpec((Q, D_XBC), lambda c: (c, 0)), chunk_rows,
                  pl.BlockSpec((SSM_HEADS, Q), lambda c: (0, c)),
                  _full((1, D_SSM)), _full((SSM_HEADS, LANES)), chunk_rows, _full((1, D_SSM)), _full((1, D_SSM))],
        out_specs=[chunk_rows, pl.BlockSpec((1, SSM_GROUPS, SSM_N, SSM_GW), lambda c: (c, 0, 0, 0)), chunk_rows],
        out_shape=[jax.ShapeDtypeStruct((M, D_SSM), f32),
                   jax.ShapeDtypeStruct((nch, SSM_GROUPS, SSM_N, SSM_GW), f32),
                   jax.ShapeDtypeStruct((M, D_SSM), bf16)],
        scratch_shapes=[pltpu.VMEM((SSM_GROUPS, SSM_N, SSM_GW), f32)],
        compiler_params=_cparams(("arbitrary",)),
    )(xbc_c, dt_e, dtt, a_e, a_col, z, d_exp, g_ssm)


def _ssd_bwd(xbc_c, dtp, dt_e, dtt, a_row, a_e, a_col, hin, y, z, d_ssm, g_ssm, d_exp, head_ind):
    M = xbc_c.shape[0]
    Q = CHUNK
    nch = M // Q
    rev = lambda c: nch - 1 - c

    gw = D_SSM // SSM_GROUPS

    def body(x_ref, dtp_ref, dte_ref, dtt_ref, arow_ref, ae_ref, acol_ref, hin_ref, y_ref, zz_ref, do_ref, gn_ref,
             dexp_ref, ind_ref, dx_ref, ddt_ref, da_ref, dz_ref, dgn_ref, dd_ref,
             dht_sc, z_sc, z1_sc, last_sc, ct_sc, dy_ref, ddc_sc):
        @pl.when(pl.program_id(0) == 0)
        def _():
            dht_sc[...] = jnp.zeros_like(dht_sc)
            da_ref[...] = jnp.zeros_like(da_ref)
            last_sc[...] = jnp.zeros_like(last_sc)
            ct_sc[...] = jnp.zeros_like(ct_sc)
            dgn_ref[...] = jnp.zeros_like(dgn_ref)
            ddc_sc[...] = jnp.zeros_like(ddc_sc)

        zv = zz_ref[...]
        xv = x_ref[:, 0:D_SSM]
        sz = _silu(zv)
        yd = y_ref[...] + dexp_ref[...] * xv
        yg = yd * sz
        dov = do_ref[...]
        for gi in range(SSM_GROUPS):
            sl = slice(gi * gw, (gi + 1) * gw)
            dyg, dgp = _rms_bwd_math(yg[:, sl], gn_ref[:, sl], dov[:, sl])
            dgn_ref[:, sl] += jnp.sum(dgp, axis=0, keepdims=True)
            dyd = dyg * sz[:, sl]
            dy_ref[:, sl] = dyd
            dz_ref[:, sl] = (dyg * yd[:, sl] * _dsilu(zv[:, sl])).astype(bf16)
            ddc_sc[:, sl] += jnp.sum(dyd * xv[:, sl], axis=0, keepdims=True)

        @pl.when(pl.program_id(0) == nch - 1)
        def _():
            dd_ref[...] = _dot_hi(ddc_sc[...], ind_ref[...])

        causal, anti, triu, dt_e, cs_e, cst, ecs_e, dte_e, elast_e = _ssd_common(dte_ref, dtt_ref, ae_ref, acol_ref)
        halves = _half_masks()
        lane = lax.broadcasted_iota(jnp.int32, (Q, LANES), 1)
        rsum = jnp.zeros((Q, LANES), f32)
        for g in range(SSM_GROUPS):
            g0 = g * SSM_GW
            gs = slice(g0, g0 + SSM_GW)
            b0 = D_SSM + g * SSM_N
            c0 = D_SSM + D_BC + g * SSM_N
            bg = x_ref[:, b0:b0 + SSM_N]
            cg = x_ref[:, c0:c0 + SSM_N]
            bg_b = bg.astype(bf16)
            cg_b = cg.astype(bf16)
            cgt_b = cg.T.astype(bf16)
            cbt = _dot_nt(bg_b, cg_b)
            cb = _dot_nt(cg_b, bg_b)
            x_g = x_ref[:, gs]
            dt_g = dt_e[:, gs]
            xdt_g = x_g * dt_g
            dy_g = dy_ref[:, gs]
            ht = hin_ref[0, g]
            ht_b = ht.astype(bf16)
            dht = dht_sc[g]
            dht_b = dht.astype(bf16)
            dye_b = (dy_g * ecs_e[:, gs]).astype(bf16)
            dc = _dot_nt(dye_b, ht_b)
            dht_new = dht * elast_e[:, gs] + _dot(cgt_b, dye_b)
            e = _dot(bg_b, dht_b)
            xdtd = xdt_g * dte_e[:, gs]
            db = _dot_nt(xdtd.astype(bf16), dht_b)
            dxdt_state = e * dte_e[:, gs]
            exd = e * xdtd
            z1_sc[:, gs] = dy_g * (_dot(cg_b, ht_b) * ecs_e[:, gs]) - exd
            last_sc[0:1, gs] = (jnp.sum(exd, axis=0, keepdims=True)
                                + jnp.sum(dht * ht, axis=0, keepdims=True) * elast_e[:, gs])
            dg_acc = jnp.zeros((Q, Q), f32)
            for pr in range(SSM_PAIRS):
                p0 = pr * LANES
                ps = slice(g0 + p0, g0 + p0 + LANES)
                dy_p = dy_g[:, p0:p0 + LANES]
                xdt_pb = xdt_g[:, p0:p0 + LANES].astype(bf16)
                acc = dxdt_state[:, p0:p0 + LANES]
                for half in range(2):
                    h = g * SSM_HPG + pr * 2 + half
                    seg = cs_e[:, h * SSM_P:h * SSM_P + 1] - cst[h:h + 1, :]
                    lm = jnp.exp(jnp.where(causal, seg, -jnp.inf))
                    lmt = jnp.exp(jnp.where(anti, -seg, -jnp.inf))
                    dym = jnp.where(halves[half], dy_p, 0.0).astype(bf16)
                    acc = acc + _dot((cbt * lmt).astype(bf16), dym)
                    dml = _dot_nt(dym, xdt_pb) * lm
                    dg_acc = dg_acc + dml
                    w = dml * cb
                    rsum = rsum + jnp.where(lane == h, jnp.sum(w, axis=1, keepdims=True), 0.0)
                    ct_sc[h:h + 1, :] = jnp.sum(w, axis=0, keepdims=True)
                dx_ref[:, ps] = acc * dt_g[:, p0:p0 + LANES] + dexp_ref[:, ps] * dy_p
                z_sc[:, ps] = acc * x_g[:, p0:p0 + LANES]
            dg_b = dg_acc.astype(bf16)
            dx_ref[:, c0:c0 + SSM_N] = dc + _dot(dg_b, bg_b)
            dx_ref[:, b0:b0 + SSM_N] = db + _dot_tn(dg_b, cg_b)
            dht_sc[g] = dht_new
        s1 = _dot_hi(z1_sc[...], ind_ref[...])
        s2 = _dot_hi(z_sc[...], ind_ref[...])
        last = _dot_hi(last_sc[...], ind_ref[...])[0:1, :]
        dtp = dtp_ref[...]
        row = lax.broadcasted_iota(jnp.int32, (Q, LANES), 0)
        dcs = s1 + rsum + jnp.where(row == Q - 1, last, 0.0)
        tril = causal.astype(f32)
        da = _dot_hi(triu, dcs) - _dot_hi(ct_sc[...], tril).T
        ddt_ref[...] = s2 + da * arow_ref[...]
        da_ref[...] += jnp.sum(da * dtp, axis=0, keepdims=True)

    chunk_rows = pl.BlockSpec((Q, D_SSM), lambda c: (rev(c), 0))
    return pl.pallas_call(
        body, name="ssd_bwd", grid=(nch,),
        in_specs=[pl.BlockSpec((Q, D_XBC), lambda c: (rev(c), 0)),
                  pl.BlockSpec((Q, LANES), lambda c: (rev(c), 0)),
                  pl.BlockSpec((Q, D_SSM), lambda c: (rev(c), 0)),
                  pl.BlockSpec((SSM_HEADS, Q), lambda c: (0, rev(c))),
                  _full((1, LANES)), _full((1, D_SSM)), _full((SSM_HEADS, LANES)),
                  pl.BlockSpec((1, SSM_GROUPS, SSM_N, SSM_GW), lambda c: (rev(c), 0, 0, 0)),
                  chunk_rows, chunk_rows, chunk_rows, _full((1, D_SSM)),
                  _full((1, D_SSM)), _full((D_SSM, LANES))],
        out_specs=[pl.BlockSpec((Q, D_XBC), lambda c: (rev(c), 0)),
                   pl.BlockSpec((Q, LANES), lambda c: (rev(c), 0)),
                   _full((1, LANES)), chunk_rows, _full((1, D_SSM)), _full((1, LANES))],
        out_shape=[jax.ShapeDtypeStruct((M, D_XBC), f32), jax.ShapeDtypeStruct((M, LANES), f32),
                   jax.ShapeDtypeStruct((1, LANES), f32), jax.ShapeDtypeStruct((M, D_SSM), bf16),
                   jax.ShapeDtypeStruct((1, D_SSM), f32), jax.ShapeDtypeStruct((1, LANES), f32)],
        scratch_shapes=[pltpu.VMEM((SSM_GROUPS, SSM_N, SSM_GW), f32), pltpu.VMEM((Q, D_SSM), f32),
                        pltpu.VMEM((Q, D_SSM), f32), pltpu.VMEM((SUBLANES, D_SSM), f32),
                        pltpu.VMEM((LANES, Q), f32), pltpu.VMEM((Q, D_SSM), f32), pltpu.VMEM((1, D_SSM), f32)],
        compiler_params=_cparams(("arbitrary",)),
    )(xbc_c, dtp, dt_e, dtt, a_row, a_e, a_col, hin, y, z, d_ssm, g_ssm, d_exp, head_ind)


_PEER_FLIPS = [(0, 0, 1), (0, 1, 0), (0, 1, 1), (1, 0, 0), (1, 0, 1), (1, 1, 0), (1, 1, 1)]


def _exchange_copies(ins, outs, send_sems, recv_sems, loc_sems, scatter):
    n = len(ins)
    x, y, c = lax.axis_index("x"), lax.axis_index("y"), lax.axis_index("c")
    me = 4 * x + 2 * y + c
    copies = []
    for a in range(n):
        src = ins[a].at[me] if scatter else ins[a]
        copies.append(pltpu.make_async_copy(src, outs[a].at[me], loc_sems.at[a]))
    for p, (fx, fy, fc) in enumerate(_PEER_FLIPS):
        tx = 1 - x if fx else x
        ty = 1 - y if fy else y
        tc = 1 - c if fc else c
        tgt = 4 * tx + 2 * ty + tc
        for a in range(n):
            src = ins[a].at[tgt] if scatter else ins[a]
            copies.append(pltpu.make_async_remote_copy(
                src_ref=src, dst_ref=outs[a].at[me],
                send_sem=send_sems.at[p * n + a], recv_sem=recv_sems.at[p * n + a],
                device_id=(tx, ty, tc), device_id_type=_MESH))
    return copies


def _exchange_shapes(arrays, scatter):
    return [jax.ShapeDtypeStruct(a.shape if scatter else (N_DEV,) + a.shape, a.dtype) for a in arrays]


def _exchange_sems(n):
    return [pltpu.SemaphoreType.DMA((7 * n,)), pltpu.SemaphoreType.DMA((7 * n,)), pltpu.SemaphoreType.DMA((n,))]


def _gather_two_level(arrays, name):
    n = len(arrays)

    def body(*refs):
        ins, outs = refs[:n], refs[n:2 * n]
        send_sems, recv_sems, loc_sems = refs[2 * n:]
        x, y, c = lax.axis_index("x"), lax.axis_index("y"), lax.axis_index("c")
        me, sibling = (x, y, c), (x, y, 1 - c)
        chips = [(1 - x, y), (x, 1 - y), (1 - x, 1 - y)]

        def slot(a, dev):
            return outs[a].at[4 * dev[0] + 2 * dev[1] + dev[2]]

        def copy(a, k, block, to, src=None):
            return pltpu.make_async_remote_copy(
                src_ref=slot(a, block) if src is None else src, dst_ref=slot(a, block),
                send_sem=send_sems.at[7 * a + k], recv_sem=recv_sems.at[7 * a + k],
                device_id=to, device_id_type=_MESH)

        mine = [pltpu.make_async_copy(ins[a], slot(a, me), loc_sems.at[a]) for a in range(n)]
        first = []
        for a in range(n):
            first.append(copy(a, 0, me, sibling, src=ins[a]))
            first += [copy(a, 1 + j, me, (*chip, c), src=ins[a]) for j, chip in enumerate(chips)]
        for cp in mine + first:
            cp.start()
        passed = []
        for j, chip in enumerate(chips):
            for a in range(n):
                copy(a, 1 + j, (*chip, c), me).wait_recv()
                cp = copy(a, 4 + j, (*chip, c), sibling)
                cp.start()
                passed.append(cp)
        for a in range(n):
            copy(a, 0, sibling, me).wait_recv()
            for j, chip in enumerate(chips):
                copy(a, 4 + j, (*chip, 1 - c), me).wait_recv()
        for cp in first + passed:
            cp.wait_send()
        for cp in mine:
            cp.wait()

    any_spec = pl.BlockSpec(memory_space=pl.ANY)
    return pl.pallas_call(
        body, name=name, in_specs=[any_spec] * n, out_specs=[any_spec] * n,
        out_shape=_exchange_shapes(arrays, False), scratch_shapes=_exchange_sems(n),
    )(*arrays)


def _exchange_tail(scattered, gathered, name):
    ns, ng = len(scattered), len(gathered)
    n = ns + ng

    def body(*refs):
        sems = refs[2 * n:]
        copies = (_exchange_copies(refs[:ns], refs[n:n + ns], *sems[:3], True)
                  + _exchange_copies(refs[ns:n], refs[n + ns:2 * n], *sems[3:], False))
        for cp in copies:
            cp.start()
        for cp in copies:
            cp.wait()

    any_spec = pl.BlockSpec(memory_space=pl.ANY)
    return pl.pallas_call(
        body, name=name, in_specs=[any_spec] * n, out_specs=[any_spec] * n,
        out_shape=_exchange_shapes(scattered, True) + _exchange_shapes(gathered, False),
        scratch_shapes=_exchange_sems(ns) + _exchange_sems(ng),
    )(*scattered, *gathered)


def _adamw_math(g, w, m, v):
    c1 = 1.0 - ADAM_B1 ** ADAM_STEP
    c2 = 1.0 - ADAM_B2 ** ADAM_STEP
    mn = ADAM_B1 * m + (1.0 - ADAM_B1) * g
    vn = ADAM_B2 * v + (1.0 - ADAM_B2) * (g * g)
    m_hat = mn / c1
    v_hat = vn / c2
    return -ADAM_LR * (m_hat / (jnp.sqrt(v_hat) + ADAM_EPS) + ADAM_WD * w), mn, vn


def _adamw(parts, w, m, v, name):
    R, C = w.shape
    tr = _pick(R, (PACK_ROW_TILE, 64, 32, 16, 8))

    def body(p_ref, w_ref, m_ref, v_ref, g_ref, d_ref, nm_ref, nv_ref):
        g = p_ref[0].astype(f32)
        for s in range(1, N_DEV):
            g = g + p_ref[s].astype(f32)
        g_ref[...] = g
        d_ref[...], nm_ref[...], nv_ref[...] = _adamw_math(g, w_ref[...], m_ref[...], v_ref[...])

    spec = pl.BlockSpec((tr, C), lambda i: (i, 0))
    return pl.pallas_call(
        body, name=name, grid=(R // tr,),
        in_specs=[pl.BlockSpec((N_DEV, tr, C), lambda i: (0, i, 0)), spec, spec, spec],
        out_specs=[spec] * 4, out_shape=[jax.ShapeDtypeStruct((R, C), f32)] * 4,
        compiler_params=_cparams(("parallel",)),
    )(parts, w, m, v)


def _adamw_replicated(parts, ws, ms, vs):
    n = len(ws)
    R = parts.shape[1]
    sizes = [int(w.shape[1]) for w in ws]

    def body(*refs):
        p_ref = refs[0]
        w_refs, m_refs, v_refs = refs[1:1 + n], refs[1 + n:1 + 2 * n], refs[1 + 2 * n:1 + 3 * n]
        loss_ref = refs[1 + 3 * n]
        outs = refs[2 + 3 * n:]
        g_all = p_ref[0]
        for s in range(1, N_DEV):
            g_all = g_all + p_ref[s]
        row = 0
        for p in range(n):
            pieces, left = [], sizes[p]
            while left > 0:
                take = min(left, PACK_W)
                pieces.append(g_all[row:row + 1, 0:take])
                left -= take
                row += 1
            g = pieces[0] if len(pieces) == 1 else jnp.concatenate(pieces, axis=1)
            d, mn, vn = _adamw_math(g, w_refs[p][...], m_refs[p][...], v_refs[p][...])
            outs[4 * p][...] = g
            outs[4 * p + 1][...] = d
            outs[4 * p + 2][...] = mn
            outs[4 * p + 3][...] = vn
        loss_ref[...] = g_all[row:row + 1, 0:LANES]

    in_specs = [_full((N_DEV, R, PACK_W))] + [_full((1, s)) for s in sizes] * 3
    out_specs = [_full((1, LANES))]
    out_shape = [jax.ShapeDtypeStruct((1, LANES), f32)]
    for s in sizes:
        out_specs += [_full((1, s))] * 4
        out_shape += [jax.ShapeDtypeStruct((1, s), f32)] * 4
    res = pl.pallas_call(
        body, name="adamw_replicated", in_specs=in_specs, out_specs=out_specs, out_shape=out_shape,
        compiler_params=pltpu.CompilerParams(vmem_limit_bytes=VMEM_LIMIT),
    )(parts, *ws, *ms, *vs)
    return res[0], [res[1 + 4 * p:5 + 4 * p] for p in range(n)]


def _flat_rows(a, lead_ndim):
    lead = a.shape[:lead_ndim]
    n = int(np.prod(a.shape[lead_ndim:]))
    a = a.reshape(lead + (n,))
    pad = (-n) % PACK_W
    if pad:
        a = jnp.pad(a, [(0, 0)] * lead_ndim + [(0, pad)])
    return a.reshape(lead + ((n + pad) // PACK_W, PACK_W))


def _pack(arrays, lead_ndim, total_rows, dtype):
    rows = [_flat_rows(a.astype(dtype), lead_ndim) for a in arrays]
    cat = jnp.concatenate(rows, axis=lead_ndim)
    pad = total_rows - cat.shape[lead_ndim]
    if pad:
        cat = jnp.pad(cat, [(0, 0)] * lead_ndim + [(0, pad), (0, 0)])
    return cat


def _unpack(buf, shapes, lead_ndim):
    out = []
    r = 0
    lead = buf.shape[:lead_ndim]
    for shp in shapes:
        n = int(np.prod(shp))
        nr = -(-n // PACK_W)
        piece = lax.slice_in_dim(buf, r, r + nr, axis=lead_ndim)
        piece = piece.reshape(lead + (nr * PACK_W,))
        piece = lax.slice_in_dim(piece, 0, n, axis=lead_ndim)
        out.append(piece.reshape(lead + tuple(shp)))
        r += nr
    return out


def _round_up(n, m):
    return -(-n // m) * m


def kernel(x, meta_tokens, norm_mix_pre, norm_mix_post, norm_ffn_pre, norm_ffn_post, w_in, q_a_norm, w_uq, kv_a_norm, w_ukv, attn_out_norm, ssm_conv_w, ssm_conv_b, ssm_dt_bias, ssm_A_log, ssm_D, ssm_norm, w_out, w_up, ffn_conv_w, ffn_conv_b, w_down, loss_target, m_meta_tokens, m_norm_mix_pre, m_norm_mix_post, m_norm_ffn_pre, m_norm_ffn_post, m_w_in, m_q_a_norm, m_w_uq, m_kv_a_norm, m_w_ukv, m_attn_out_norm, m_ssm_conv_w, m_ssm_conv_b, m_ssm_dt_bias, m_ssm_A_log, m_ssm_D, m_ssm_norm, m_w_out, m_w_up, m_ffn_conv_w, m_ffn_conv_b, m_w_down, v_meta_tokens, v_norm_mix_pre, v_norm_mix_post, v_norm_ffn_pre, v_norm_ffn_post, v_w_in, v_q_a_norm, v_w_uq, v_kv_a_norm, v_w_ukv, v_attn_out_norm, v_ssm_conv_w, v_ssm_conv_b, v_ssm_dt_bias, v_ssm_A_log, v_ssm_D, v_ssm_norm, v_w_out, v_w_up, v_ffn_conv_w, v_ffn_conv_b, v_w_down):
    seq = x.shape[1]
    n_real = N_META + seq
    Lp = _round_up(n_real, 768) if n_real > 2048 else _round_up(n_real, ROW_TILE)
    D = D_MODEL

    early_w = [w_uq, w_ukv]
    late_w = [w_out, w_down]
    sharded_s = [meta_tokens, ssm_conv_w, ffn_conv_w]
    grp_a = dict(names=["w_out", "w_down"], w=late_w, m=[m_w_out, m_w_down],
                 v=[v_w_out, v_w_down])
    grp_b = dict(names=["w_uq", "w_ukv", "ssm_conv_w", "ffn_conv_w"],
                 w=early_w + [ssm_conv_w, ffn_conv_w],
                 m=[m_w_uq, m_w_ukv, m_ssm_conv_w, m_ffn_conv_w],
                 v=[v_w_uq, v_w_ukv, v_ssm_conv_w, v_ffn_conv_w])
    grp_meta = dict(names=["meta_tokens"], w=[meta_tokens], m=[m_meta_tokens], v=[v_meta_tokens])
    repl_w = [norm_mix_pre, norm_mix_post, norm_ffn_pre, norm_ffn_post, q_a_norm, kv_a_norm, attn_out_norm,
              ssm_conv_b, ssm_dt_bias, ssm_A_log, ssm_D, ssm_norm, ffn_conv_b]
    repl_m = [m_norm_mix_pre, m_norm_mix_post, m_norm_ffn_pre, m_norm_ffn_post, m_q_a_norm, m_kv_a_norm,
              m_attn_out_norm, m_ssm_conv_b, m_ssm_dt_bias, m_ssm_A_log, m_ssm_D, m_ssm_norm, m_ffn_conv_b]
    repl_v = [v_norm_mix_pre, v_norm_mix_post, v_norm_ffn_pre, v_norm_ffn_post, v_q_a_norm, v_kv_a_norm,
              v_attn_out_norm, v_ssm_conv_b, v_ssm_dt_bias, v_ssm_A_log, v_ssm_D, v_ssm_norm, v_ffn_conv_b]

    def pack_rows(arrs, lead):
        return _round_up(sum(-(-int(np.prod(a.shape[lead:])) // PACK_W) for a in arrs), 16)

    wb = _pack(early_w, 0, pack_rows(early_w, 0), bf16)
    wl = _pack(late_w, 0, pack_rows(late_w, 0), bf16)
    ws = _pack(sharded_s, 0, pack_rows(sharded_s, 0), f32)
    wb_all, ws_all, win_all = _gather_two_level([wb, ws, w_in[0].astype(bf16)], "gather_weights")
    g_w_uq, g_w_ukv = _unpack(wb_all, [a.shape for a in early_w], 1)
    g_meta, g_sconv, g_fconv = _unpack(ws_all, [a.shape for a in sharded_s], 1)

    def cols(gathered):
        t = gathered[:, 0]
        return jnp.transpose(t, (1, 0, 2)).reshape(t.shape[1], N_DEV * t.shape[2])

    win = cols(win_all[:, None])
    o = np.cumsum((0, Q_RANK, KV_RANK, QK_ROPE, D_SSM, D_XBC, SSM_HEADS))
    w_q, w_kv = win[:, o[0]:o[1]], win[:, o[1]:o[2]]
    w_rope = jnp.pad(win[:, o[2]:o[3]], ((0, 0), (0, LANES - QK_ROPE)))
    w_z, w_xbc = win[:, o[3]:o[4]], win[:, o[4]:o[5]]
    w_dt = jnp.pad(win[:, o[5]:o[6]], ((0, 0), (0, LANES - SSM_HEADS)))
    wuq = g_w_uq.reshape(Q_RANK, MLA_HEADS, QK_NOPE + QK_ROPE)
    wuq = jnp.pad(wuq, ((0, 0), (0, 0), (0, QK_PAD - QK_NOPE - QK_ROPE))).reshape(Q_RANK, MLA_HEADS * QK_PAD)
    wukv = g_w_ukv.reshape(KV_RANK, MLA_HEADS * (QK_NOPE + V_DIM))
    meta_full = jnp.transpose(g_meta, (1, 0, 2)).reshape(N_META, D)
    sconv_w = jnp.pad(cols(g_sconv), ((0, SUBLANES - SSM_CONV), (0, 0)))
    fconv_w = jnp.pad(cols(g_fconv), ((0, SUBLANES - FFN_CONV), (0, 0)))

    pos = jnp.arange(Lp, dtype=f32)
    inv = ROPE_THETA ** (-jnp.arange(0, QK_ROPE, 2, dtype=f32) / QK_ROPE)
    ang = pos[:, None] * inv[None, :]
    cs_, sn_ = jnp.cos(ang), jnp.sin(ang)
    zpad = jnp.zeros((Lp, LANES - QK_ROPE), f32)
    cos_t = jnp.concatenate([cs_, cs_, zpad], axis=1)
    sin_t = jnp.concatenate([-sn_, sn_, zpad], axis=1)
    dt_bias_p = jnp.pad(ssm_dt_bias, ((0, 0), (0, LANES - SSM_HEADS)))
    a_neg = -jnp.exp(ssm_A_log)
    a_row = jnp.pad(a_neg, ((0, 0), (0, LANES - SSM_HEADS)))
    a_col = jnp.broadcast_to(a_neg.reshape(SSM_HEADS, 1), (SSM_HEADS, LANES))
    d_exp = jnp.repeat(ssm_D, SSM_P, axis=1)
    a_e = jnp.repeat(a_neg, SSM_P, axis=1)
    head_ind = (jnp.arange(D_SSM)[:, None] // SSM_P == jnp.arange(LANES)[None, :]).astype(f32)

    xb = x[0]
    h0 = jnp.concatenate([meta_full, xb, jnp.zeros((Lp - n_real, D), f32)], axis=0)
    tgt = jnp.pad(loss_target[0], ((N_META, Lp - n_real), (0, 0)))
    hn1, q_c, kv_c, kpe_raw, z, xbc, dt_raw = _in_proj(h0, norm_mix_pre, [w_q, w_kv, w_rope, w_z, w_xbc, w_dt])

    qn, qh = _up_q_rope(q_c, q_a_norm, wuq, cos_t, sin_t)
    kvn, kh, vh = _up_kv_rope(kv_c, kv_a_norm, wukv, kpe_raw, cos_t, sin_t)
    attn, lse, wl_all, wup_all = _flash_fwd(qh, kh, vh, [wl, w_up[0].astype(bf16)], False)
    g_w_out, g_w_down = _unpack(wl_all, [a.shape for a in late_w], 1)
    wout = g_w_out.reshape(D_ATTN + D_SSM, D)
    wout_a, wout_s = wout[:D_ATTN], wout[D_ATTN:]
    wup = cols(wup_all[:, None])
    wdown = g_w_down.reshape(D_FF, D)

    xbc_c = _ssm_conv_fwd(xbc, sconv_w, ssm_conv_b)
    dtp, dt_e = _dt_fwd(dt_raw, dt_bias_p, jnp.transpose(head_ind))
    dtt = jnp.transpose(dtp[:, :SSM_HEADS])
    y_ssd, hin, ssm = _ssd_fwd(xbc_c, dt_e, dtt, a_e, a_col, z, d_exp, ssm_norm)

    an, mix, h1, hn2 = _out_proj_resid(attn, attn_out_norm, ssm, wout_a, wout_s, h0, norm_mix_post, norm_ffn_pre)
    up = _mm(hn2, wup, False, "ffn_up")
    act = _ffn_gate_fwd(up, fconv_w, ffn_conv_b)
    dh2, d_down, dg_ffn_post, loss_part = _final(h1, act, wdown, norm_ffn_post, tgt, n_real)

    dw_down = _mm_tn(act, d_down, "ffn_down_dw")
    dup_g, dup_v, dwc_g, dwc_v, dbc_g, dbc_v = _ffn_gate_bwd(up, fconv_w, ffn_conv_b, d_down, wdown)
    dw_up = jnp.concatenate([_mm_tn(hn2, dup_g, "ffn_up_dw_g"), _mm_tn(hn2, dup_v, "ffn_up_dw_v")], axis=1)
    dh1, d_mix, dg_ffn_pre, dg_mix_post = _mid_bwd(h1, norm_ffn_pre, dup_g, dup_v, wup, dh2, mix, norm_mix_post)
    d_ssm = _mm(d_mix, wout_s, True, "out_proj_dx_s")
    dw_out = jnp.concatenate([_mm_tn(an, d_mix, "out_proj_dw_a"), _mm_tn(ssm, d_mix, "out_proj_dw_s")], axis=0)

    do_h, delta, dg_attn_out = _attn_out_bwd(attn, attn_out_norm, d_mix, wout_a)
    def col_blocks(gm):
        r, cc = gm.shape
        return jnp.transpose(gm.reshape(r, N_DEV, cc // N_DEV), (1, 0, 2))

    blocks_a = [dw_out.reshape(N_DEV, (D_ATTN + D_SSM) // N_DEV, D), dw_down.reshape(N_DEV, D_FF // N_DEV, D)]
    gpack_a = _pack(blocks_a, 1, pack_rows(blocks_a, 1), bf16)
    dqh, dkh, dvh, gparts_a, gparts_up = _flash_bwd(qh, kh, vh, do_h, lse, delta,
                                                    [gpack_a, col_blocks(dw_up).astype(bf16)], True)
    d_q_c, dg_q, dw_uq = _q_branch_bwd(dqh, cos_t, sin_t, wuq, q_c, q_a_norm, qn)
    d_kv_c, dg_kv, dw_ukv, d_kpe_raw = _kv_branch_bwd(dkh, dvh, cos_t, sin_t, wukv, kv_c, kv_a_norm, kvn)

    d_xbc_c, ddt, da_heads, dz, dg_ssm, dd_heads = _ssd_bwd(
        xbc_c, dtp, dt_e, dtt, a_row, a_e, a_col, hin, y_ssd, z, d_ssm, ssm_norm, d_exp, head_ind)
    d_xbc, dw_sconv, db_sconv = _ssm_conv_bwd(xbc, sconv_w, ssm_conv_b, d_xbc_c)
    d_dt_raw, d_dt_bias = _dt_bwd(dt_raw, dt_bias_p, ddt)

    dw_q, dw_kv, dw_rope, dw_z, dw_xbc, dw_dt = _in_proj_dw(hn1, [d_q_c, d_kv_c, d_kpe_raw, dz, d_xbc, d_dt_raw])
    dw_in = jnp.concatenate([dw_q, dw_kv, dw_rope[:, :QK_ROPE], dw_z, dw_xbc, dw_dt[:, :SSM_HEADS]], axis=1)
    dw_uq3 = dw_uq.reshape(Q_RANK, MLA_HEADS, QK_PAD)[:, :, :QK_NOPE + QK_ROPE]
    blocks_b = [
        dw_uq3.reshape(N_DEV, Q_RANK // N_DEV, MLA_HEADS, QK_NOPE + QK_ROPE),
        dw_ukv.reshape(N_DEV, KV_RANK // N_DEV, MLA_HEADS, QK_NOPE + V_DIM),
        col_blocks(dw_sconv[:SSM_CONV]),
        col_blocks(jnp.concatenate([dwc_g, dwc_v], axis=1)[:FFN_CONV]),
    ]
    gpack_b = _pack(blocks_b, 1, pack_rows(blocks_b, 1), bf16)
    dh0, dg_mix_pre, gparts_b, gparts_in = _in_proj_dx(
        [d_q_c, d_kv_c, d_kpe_raw, dz, d_xbc, d_dt_raw], [w_q, w_kv, w_rope, w_z, w_xbc, w_dt],
        h0, norm_mix_pre, dh1, [gpack_b, col_blocks(dw_in).astype(bf16)], True)

    grad_x = dh0[N_META:n_real][None]
    meta_blocks = col_blocks(dh0[:N_META]).reshape(N_DEV, N_META * D // N_DEV // PACK_W, PACK_W)


    def adam_group(parts, grp, name):
        rows = parts.shape[1]
        packs = [_pack([a[None] for a in grp[k]], 1, rows, f32)[0] for k in ("w", "m", "v")]
        outs = _adamw(parts, *packs, name)
        shapes = [a.shape for a in grp["w"]]
        return [dict(zip(grp["names"], [t[0] for t in _unpack(b[None], shapes, 1)])) for b in outs]

    def adam_own_layout(parts, name, w, m, v):
        return [{name: t[None]} for t in _adamw(parts, w[0], m[0], v[0], "adamw_" + name)]

    sh_a = adam_group(gparts_a, grp_a, "adamw_sharded_a")
    sh_b = adam_group(gparts_b, grp_b, "adamw_sharded_b")
    sh_in = adam_own_layout(gparts_in, "w_in", w_in, m_w_in, v_w_in)
    sh_up = adam_own_layout(gparts_up, "w_up", w_up, m_w_up, v_w_up)

    dg_alog = da_heads[:, :SSM_HEADS] * a_neg
    repl_g = [dg_mix_pre, dg_mix_post, dg_ffn_pre, dg_ffn_post, dg_q, dg_kv, dg_attn_out, db_sconv,
              d_dt_bias[:, :SSM_HEADS], dg_alog, dd_heads[:, :SSM_HEADS], dg_ssm,
              jnp.concatenate([dbc_g, dbc_v], axis=1)]
    loss_vec = loss_part[:, :1]
    small_total = _round_up(sum(-(-int(np.prod(a.shape)) // PACK_W) for a in repl_g) + 1, 16)
    spack = _pack(repl_g + [loss_vec], 0, small_total, f32)
    gparts_meta, sparts = _exchange_tail([meta_blocks], [spack], "exchange_tail")
    sh_meta = adam_group(gparts_meta, grp_meta, "adamw_meta")
    loss_row, repl_out = _adamw_replicated(sparts, repl_w, repl_m, repl_v)
    loss = loss_row[0, 0]

    order = ["meta_tokens", "norm_mix_pre", "norm_mix_post", "norm_ffn_pre", "norm_ffn_post", "w_in", "q_a_norm",
             "w_uq", "kv_a_norm", "w_ukv", "attn_out_norm", "ssm_conv_w", "ssm_conv_b", "ssm_dt_bias", "ssm_A_log",
             "ssm_D", "ssm_norm", "w_out", "w_up", "ffn_conv_w", "ffn_conv_b", "w_down"]
    rp_names = ["norm_mix_pre", "norm_mix_post", "norm_ffn_pre", "norm_ffn_post", "q_a_norm", "kv_a_norm",
                "attn_out_norm", "ssm_conv_b", "ssm_dt_bias", "ssm_A_log", "ssm_D", "ssm_norm", "ffn_conv_b"]

    def lookup(k):
        d = {**sh_a[k], **sh_b[k], **sh_in[k], **sh_up[k], **sh_meta[k],
             **{n: four[k] for n, four in zip(rp_names, repl_out)}}
        return [d[n] for n in order]

    return (loss, grad_x, *lookup(0), *lookup(1), *lookup(2), *lookup(3))
```

```python
import math

import jax
import jax.numpy as jnp
import numpy as np
from jax import lax
from jax.experimental import pallas as pl
from jax.experimental.pallas import tpu as pltpu

f32 = jnp.float32
bf16 = jnp.bfloat16

D_MODEL = 1024
N_META = 16
MLA_HEADS = 8
QK_NOPE = 128
QK_ROPE = 64
V_DIM = 128
Q_RANK = 384
KV_RANK = 256
ROPE_THETA = 10000.0
SOFTMAX_SCALE = (QK_NOPE + QK_ROPE) ** -0.5
D_ATTN = MLA_HEADS * V_DIM
SSM_HEADS = 16
SSM_P = 64
SSM_GROUPS = 2
SSM_HPG = SSM_HEADS // SSM_GROUPS
SSM_N = 128
SSM_CONV = 4
CHUNK = 128
D_SSM = SSM_HEADS * SSM_P
D_BC = SSM_GROUPS * SSM_N
D_XBC = D_SSM + 2 * D_BC
D_FF = 2816
FFN_CONV = 3
EPS = 1e-6
QK_PAD = 256
N_DEV = 8

ADAM_LR = 0.001
ADAM_B1 = 0.9
ADAM_B2 = 0.999
ADAM_EPS = 1e-08
ADAM_WD = 0.01
ADAM_STEP = 10

LANES = 128
SUBLANES = 8
ROW_TILE = 256
VMEM_LIMIT = 56 * 1024 * 1024
PACK_W = 1024
PACK_ROW_TILE = 128
NEG = -1e30
LOG2E = math.log2(math.e)
LN2 = math.log(2.0)
Q_PRESCALE = SOFTMAX_SCALE * LOG2E

_MESH = pl.DeviceIdType.MESH


def _pick(n, prefs):
    for p in prefs:
        if n % p == 0:
            return p
    return n


def _rt(m):
    return _pick(m, (384, ROW_TILE))


def _cparams(sem):
    return pltpu.CompilerParams(dimension_semantics=sem, vmem_limit_bytes=VMEM_LIMIT)


def _row(spec_cols, tm):
    return pl.BlockSpec((tm, spec_cols), lambda i: (i, 0))


def _full(shape):
    nd = len(shape)
    return pl.BlockSpec(shape, lambda *a: (0,) * nd)


def _sigmoid(x):
    return 1.0 / (1.0 + jnp.exp(-x))


def _silu(x):
    return x * _sigmoid(x)


def _dsilu(x):
    s = _sigmoid(x)
    return s * (1.0 + x * (1.0 - s))


def _dot(a, b):
    return jnp.dot(a, b, preferred_element_type=f32)


def _dot_nt(a, b):
    return lax.dot_general(a, b, (((1,), (1,)), ((), ())), preferred_element_type=f32)


def _dot_tn(a, b):
    return lax.dot_general(a, b, (((0,), (0,)), ((), ())), preferred_element_type=f32)


def _dot_hi(a, b):
    return jnp.dot(a, b, precision=lax.Precision.HIGHEST, preferred_element_type=f32)


def _mm(a, b, trans_b, name):
    M, K = a.shape
    N = b.shape[0] if trans_b else b.shape[1]
    tm = _pick(M, (768, 512, 256))
    tn = _pick(N, (1408, 512, 384, 256, 128))

    def body(a_ref, b_ref, o_ref):
        o_ref[...] = _dot_nt(a_ref[...], b_ref[...]) if trans_b else _dot(a_ref[...], b_ref[...])

    b_spec = pl.BlockSpec((tn, K), lambda i, j: (j, 0)) if trans_b else pl.BlockSpec((K, tn), lambda i, j: (0, j))
    return pl.pallas_call(
        body, name=name, grid=(M // tm, N // tn),
        in_specs=[pl.BlockSpec((tm, K), lambda i, j: (i, 0)), b_spec],
        out_specs=pl.BlockSpec((tm, tn), lambda i, j: (i, j)),
        out_shape=jax.ShapeDtypeStruct((M, N), f32),
        compiler_params=_cparams(("parallel", "parallel")),
    )(a, b)


def _mm_tn(a, g, name):
    M, K = a.shape
    N = g.shape[1]
    tm = _pick(M, (768, 512, 256))
    tk = _pick(K, (1024, 1408, 512, 384, 256))
    tn = _pick(N, (1024, 1408, 512, 384, 256, 128))

    def body(a_ref, g_ref, o_ref):
        @pl.when(pl.program_id(2) == 0)
        def _():
            o_ref[...] = jnp.zeros_like(o_ref)

        o_ref[...] += _dot_tn(a_ref[...].astype(bf16), g_ref[...].astype(bf16))

    return pl.pallas_call(
        body, name=name, grid=(K // tk, N // tn, M // tm),
        in_specs=[pl.BlockSpec((tm, tk), lambda k, j, m: (m, k)),
                  pl.BlockSpec((tm, tn), lambda k, j, m: (m, j))],
        out_specs=pl.BlockSpec((tk, tn), lambda k, j, m: (k, j)),
        out_shape=jax.ShapeDtypeStruct((K, N), f32),
        compiler_params=_cparams(("parallel", "parallel", "arbitrary")),
    )(a, g)


def _rstd(x):
    return lax.rsqrt(jnp.mean(x * x, axis=-1, keepdims=True) + EPS)


def _rms_bwd_math(x, g, dy):
    r = _rstd(x)
    xh = x * r
    dn = dy * g
    dx = r * (dn - xh * jnp.mean(dn * xh, axis=-1, keepdims=True))
    return dx, dy * xh


def _in_proj(h0, g, weights):
    M, K = h0.shape
    tm = _rt(M)
    n = len(weights)
    widths = [int(w.shape[1]) for w in weights]

    def body(x_ref, g_ref, *refs):
        xv = x_ref[...]
        hn = (xv * _rstd(xv) * g_ref[...]).astype(bf16)
        refs[n][...] = hn
        for p in range(n):
            refs[n + 1 + p][...] = _dot(hn, refs[p][...])

    return pl.pallas_call(
        body, name="in_proj", grid=(M // tm,),
        in_specs=[_row(K, tm), _full((1, K))] + [_full((K, wd)) for wd in widths],
        out_specs=[_row(K, tm)] + [_row(wd, tm) for wd in widths],
        out_shape=[jax.ShapeDtypeStruct((M, K), bf16)] + [jax.ShapeDtypeStruct((M, wd), f32) for wd in widths],
        compiler_params=_cparams(("parallel",)),
    )(h0, g, *weights)


def _in_proj_dw(hn, grads):
    M, K = hn.shape
    tm = _rt(M)
    n = len(grads)
    widths = [int(gr.shape[1]) for gr in grads]

    def body(a_ref, *refs):
        @pl.when(pl.program_id(0) == 0)
        def _():
            for p in range(n):
                refs[n + p][...] = jnp.zeros_like(refs[n + p])

        a = a_ref[...]
        for p in range(n):
            refs[n + p][...] += _dot_tn(a, refs[p][...].astype(bf16))

    return pl.pallas_call(
        body, name="in_proj_dw", grid=(M // tm,),
        in_specs=[_row(K, tm)] + [_row(wd, tm) for wd in widths],
        out_specs=[_full((K, wd)) for wd in widths],
        out_shape=[jax.ShapeDtypeStruct((K, wd), f32) for wd in widths],
        compiler_params=_cparams(("arbitrary",)),
    )(hn, *grads)


def _in_proj_dx(grads, weights, h0, g, dh1, carried, scatter):
    M, K = h0.shape
    tm = _rt(M)
    nt = M // tm
    n = len(grads)
    nx = len(carried)
    widths = [int(w.shape[1]) for w in weights]

    def body(*refs):
        g_refs, w_refs = refs[:n], refs[n:2 * n]
        x_ref, gain_ref, r_ref = refs[2 * n:2 * n + 3]
        cin = refs[2 * n + 3:2 * n + 3 + nx]
        dx_ref, dg_ref = refs[2 * n + 3 + nx:2 * n + 5 + nx]
        cout = refs[2 * n + 5 + nx:2 * n + 5 + 2 * nx]
        i = pl.program_id(0)
        _hosted_exchange(cin, cout, refs[2 * n + 5 + 2 * nx:], scatter, i == 0, i == nt - 1)

        @pl.when(i == 0)
        def _():
            dg_ref[...] = jnp.zeros_like(dg_ref)

        d_hn = None
        for p in range(n):
            t = _dot_nt(g_refs[p][...].astype(bf16), w_refs[p][...])
            d_hn = t if d_hn is None else d_hn + t
        dx, dgp = _rms_bwd_math(x_ref[...], gain_ref[...], d_hn)
        dx_ref[...] = dx + r_ref[...]
        dg_ref[...] += jnp.sum(dgp, axis=0, keepdims=True)

    any_spec = pl.BlockSpec(memory_space=pl.ANY)
    return pl.pallas_call(
        body, name="in_proj_dx", grid=(nt,),
        in_specs=([_row(wd, tm) for wd in widths] + [_full((K, wd)) for wd in widths]
                  + [_row(K, tm), _full((1, K)), _row(K, tm)] + [any_spec] * nx),
        out_specs=[_row(K, tm), _full((1, K))] + [any_spec] * nx,
        out_shape=[jax.ShapeDtypeStruct((M, K), f32), jax.ShapeDtypeStruct((1, K), f32)]
        + _exchange_shapes(carried, scatter),
        scratch_shapes=_exchange_sems(nx),
        compiler_params=_cparams(("arbitrary",)),
    )(*grads, *weights, h0, g, dh1, *carried)


def _out_proj_resid(attn, ga, ssm, wa, ws, h0, g2, g3):
    M, K = h0.shape
    tm = _rt(M)

    def body(o_ref, ga_ref, s_ref, wa_ref, ws_ref, h_ref, g2_ref, g3_ref, an_ref, m_ref, h1_ref, hn_ref):
        ov = o_ref[...]
        an = (ov * _rstd(ov) * ga_ref[...]).astype(bf16)
        an_ref[...] = an
        mv = _dot(an, wa_ref[...]) + _dot(s_ref[...], ws_ref[...])
        m_ref[...] = mv
        h1 = h_ref[...] + mv * _rstd(mv) * g2_ref[...]
        h1_ref[...] = h1
        hn_ref[...] = (h1 * _rstd(h1) * g3_ref[...]).astype(bf16)

    return pl.pallas_call(
        body, name="out_proj_resid", grid=(M // tm,),
        in_specs=[_row(attn.shape[1], tm), _full((1, attn.shape[1])), _row(ssm.shape[1], tm),
                  _full(wa.shape), _full(ws.shape), _row(K, tm), _full((1, K)), _full((1, K))],
        out_specs=[_row(attn.shape[1], tm), _row(K, tm), _row(K, tm), _row(K, tm)],
        out_shape=[jax.ShapeDtypeStruct(attn.shape, bf16), jax.ShapeDtypeStruct((M, K), f32),
                   jax.ShapeDtypeStruct((M, K), f32), jax.ShapeDtypeStruct((M, K), bf16)],
        compiler_params=_cparams(("parallel",)),
    )(attn, ga, ssm, wa, ws, h0, g2, g3)


def _final(h1, act, wdown, g4, tgt, n_real):
    M, K = h1.shape
    F = act.shape[1]
    tm = _rt(M)
    nt = M // tm

    def body(h_ref, a_ref, w_ref, g_ref, t_ref, dh_ref, dd_ref, dg_ref, ls_ref, acc_ref):
        i = pl.program_id(0)

        @pl.when(i == 0)
        def _():
            dg_ref[...] = jnp.zeros_like(dg_ref)
            acc_ref[...] = jnp.zeros_like(acc_ref)

        dv = _dot(a_ref[...], w_ref[...])
        g = g_ref[...]
        r = _rstd(dv)
        n = dv * r
        h2 = h_ref[...] + n * g
        rows = i * tm + lax.broadcasted_iota(jnp.int32, (tm, 1), 0)
        mask = ((rows >= N_META) & (rows < n_real)).astype(f32)
        diff = (h2 - t_ref[...]) * mask
        acc_ref[...] += jnp.sum(diff * diff, axis=0, keepdims=True)
        dh = diff * (1.0 / K)
        dh_ref[...] = dh
        dn = dh * g
        dd_ref[...] = (r * (dn - n * jnp.mean(dn * n, axis=-1, keepdims=True))).astype(bf16)
        dg_ref[...] += jnp.sum(dh * n, axis=0, keepdims=True)

        @pl.when(i == nt - 1)
        def _():
            ls_ref[...] = jnp.zeros((1, LANES), f32) + jnp.sum(acc_ref[...]) * (0.5 / K)

    return pl.pallas_call(
        body, name="ffn_down_loss", grid=(nt,),
        in_specs=[_row(K, tm), _row(F, tm), _full((F, K)), _full((1, K)), _row(K, tm)],
        out_specs=[_row(K, tm), _row(K, tm), _full((1, K)), _full((1, LANES))],
        out_shape=[jax.ShapeDtypeStruct((M, K), f32), jax.ShapeDtypeStruct((M, K), bf16),
                   jax.ShapeDtypeStruct((1, K), f32), jax.ShapeDtypeStruct((1, LANES), f32)],
        scratch_shapes=[pltpu.VMEM((1, K), f32)],
        compiler_params=_cparams(("arbitrary",)),
    )(h1, act, wdown, g4, tgt)


def _mid_bwd(h1, g3, dup_g, dup_v, wup, dh2, mix, g2):
    M, K = h1.shape
    F = dup_g.shape[1]
    tm = ROW_TILE

    def body(h_ref, g3_ref, ag_ref, av_ref, w_ref, dh2_ref, m_ref, g2_ref, dh1_ref, dm_ref, dg3_ref, dg2_ref):
        @pl.when(pl.program_id(0) == 0)
        def _():
            dg3_ref[...] = jnp.zeros_like(dg3_ref)
            dg2_ref[...] = jnp.zeros_like(dg2_ref)

        d_hn2 = _dot_nt(ag_ref[...], w_ref[:, 0:F]) + _dot_nt(av_ref[...], w_ref[:, F:2 * F])
        dx, dgp = _rms_bwd_math(h_ref[...], g3_ref[...], d_hn2)
        dh1 = dh2_ref[...] + dx
        dh1_ref[...] = dh1
        dg3_ref[...] += jnp.sum(dgp, axis=0, keepdims=True)
        dm, dgp2 = _rms_bwd_math(m_ref[...], g2_ref[...], dh1)
        dm_ref[...] = dm.astype(bf16)
        dg2_ref[...] += jnp.sum(dgp2, axis=0, keepdims=True)

    return pl.pallas_call(
        body, name="ffn_up_dx_mid_bwd", grid=(M // tm,),
        in_specs=[_row(K, tm), _full((1, K)), _row(F, tm), _row(F, tm), _full((K, 2 * F)), _row(K, tm),
                  _row(K, tm), _full((1, K))],
        out_specs=[_row(K, tm), _row(K, tm), _full((1, K)), _full((1, K))],
        out_shape=[jax.ShapeDtypeStruct((M, K), f32), jax.ShapeDtypeStruct((M, K), bf16),
                   jax.ShapeDtypeStruct((1, K), f32), jax.ShapeDtypeStruct((1, K), f32)],
        compiler_params=_cparams(("arbitrary",)),
    )(h1, g3, dup_g, dup_v, wup, dh2, mix, g2)


HEADS_PER_STEP = 4
CONV_RB = 16


def _conv_block_taps(x_ref, halo, rb, lanes, kw):
    r0 = rb * CONV_RB
    if rb == 0:
        cat = jnp.concatenate([halo, x_ref[0:CONV_RB, lanes]], axis=0)
        first = SUBLANES - (kw - 1)
        return [cat[first + k:first + k + CONV_RB] for k in range(kw)]
    return [x_ref[r0 - (kw - 1) + k:r0 - (kw - 1) + k + CONV_RB, lanes] for k in range(kw)]


def _conv_weighted(taps, w, kw):
    u = None
    for k in range(kw):
        t = taps[k] * w[k:k + 1, :]
        u = t if u is None else u + t
    return u


def _conv_block_dx(du, nxt, w, kw):
    cat = jnp.concatenate([du, nxt], axis=0)
    return _conv_weighted([cat[kw - 1 - k:kw - 1 - k + CONV_RB] for k in range(kw)], w, kw)


def _prev_spec(tm, tc, col_of, row_axis, reversed_tiles=0):
    def imap(*ids):
        i = ids[row_axis]
        if reversed_tiles:
            i = reversed_tiles - 1 - i
        return (jnp.maximum(i * (tm // SUBLANES) - 1, 0), col_of(*ids))
    return pl.BlockSpec((SUBLANES, tc), imap)


def _ssm_conv_fwd(xbc, w, b):
    M, C = xbc.shape
    tm, tc, kw = ROW_TILE, C, SSM_CONV

    def body(x_ref, h_ref, w_ref, b_ref, o_ref):
        i = pl.program_id(0)

        def chunk(j, carry):
            lanes = pl.ds(pl.multiple_of(j * LANES, LANES), LANES)
            halo = jnp.where(i == 0, 0.0, h_ref[:, lanes])
            wv = w_ref[:, lanes]
            bv = b_ref[:, lanes]
            for rb in range(tm // CONV_RB):
                u = _conv_weighted(_conv_block_taps(x_ref, halo, rb, lanes, kw), wv, kw) + bv
                o_ref[rb * CONV_RB:(rb + 1) * CONV_RB, lanes] = _silu(u)
            return carry

        lax.fori_loop(0, tc // LANES, chunk, 0)

    return pl.pallas_call(
        body, name="ssm_conv_fwd", grid=(M // tm, C // tc),
        in_specs=[pl.BlockSpec((tm, tc), lambda i, j: (i, j)),
                  _prev_spec(tm, tc, lambda i, j: j, 0),
                  pl.BlockSpec((SUBLANES, tc), lambda i, j: (0, j)),
                  pl.BlockSpec((1, tc), lambda i, j: (0, j))],
        out_specs=pl.BlockSpec((tm, tc), lambda i, j: (i, j)),
        out_shape=jax.ShapeDtypeStruct((M, C), f32),
        compiler_params=_cparams(("parallel", "parallel")),
    )(xbc, xbc, w, b)


def _ssm_conv_bwd(xbc, w, b, dout):
    M, C = xbc.shape
    tm, tc, kw = ROW_TILE, C // 3, SSM_CONV
    nt = M // tm

    def body(x_ref, h_ref, w_ref, b_ref, d_ref, dx_ref, dw_ref, db_ref, nxt_ref):
        i = pl.program_id(1)

        @pl.when(i == 0)
        def _():
            dw_ref[...] = jnp.zeros_like(dw_ref)
            db_ref[...] = jnp.zeros_like(db_ref)
            nxt_ref[...] = jnp.zeros_like(nxt_ref)

        def chunk(j, carry):
            lanes = pl.ds(pl.multiple_of(j * LANES, LANES), LANES)
            halo = jnp.where(i == nt - 1, 0.0, h_ref[:, lanes])
            wv = w_ref[:, lanes]
            bv = b_ref[:, lanes]
            nxt = nxt_ref[:, lanes]
            db = jnp.zeros((CONV_RB, LANES), f32)
            dw = [jnp.zeros((CONV_RB, LANES), f32) for _ in range(kw)]
            for rb in reversed(range(tm // CONV_RB)):
                rows = slice(rb * CONV_RB, (rb + 1) * CONV_RB)
                taps = _conv_block_taps(x_ref, halo, rb, lanes, kw)
                du = d_ref[rows, lanes] * _dsilu(_conv_weighted(taps, wv, kw) + bv)
                db = db + du
                dw = [dw[k] + du * taps[k] for k in range(kw)]
                dx_ref[rows, lanes] = _conv_block_dx(du, nxt, wv, kw).astype(bf16)
                nxt = du[0:SUBLANES]
            nxt_ref[:, lanes] = nxt
            db_ref[:, lanes] += jnp.sum(db, axis=0, keepdims=True)
            for k in range(kw):
                dw_ref[k:k + 1, lanes] += jnp.sum(dw[k], axis=0, keepdims=True)
            return carry

        lax.fori_loop(0, tc // LANES, chunk, 0)

    tile = pl.BlockSpec((tm, tc), lambda j, i: (nt - 1 - i, j))
    return pl.pallas_call(
        body, name="ssm_conv_bwd", grid=(C // tc, nt),
        in_specs=[tile, _prev_spec(tm, tc, lambda j, i: j, 1, nt),
                  pl.BlockSpec((SUBLANES, tc), lambda j, i: (0, j)),
                  pl.BlockSpec((1, tc), lambda j, i: (0, j)), tile],
        out_specs=[tile, pl.BlockSpec((SUBLANES, tc), lambda j, i: (0, j)),
                   pl.BlockSpec((1, tc), lambda j, i: (0, j))],
        out_shape=[jax.ShapeDtypeStruct((M, C), bf16), jax.ShapeDtypeStruct((SUBLANES, C), f32),
                   jax.ShapeDtypeStruct((1, C), f32)],
        scratch_shapes=[pltpu.VMEM((SUBLANES, tc), f32)],
        compiler_params=_cparams(("parallel", "arbitrary")),
    )(xbc, xbc, w, b, dout)


def _ffn_gate_fwd(up, w, b):
    M = up.shape[0]
    tm, tc, kw = ROW_TILE, D_FF // 2, FFN_CONV
    nc = D_FF // tc

    def body(xg_ref, hg_ref, xv_ref, hv_ref, wg_ref, wv_ref, bg_ref, bv_ref, o_ref):
        i = pl.program_id(0)

        def chunk(j, carry):
            lanes = pl.ds(pl.multiple_of(j * LANES, LANES), LANES)
            halo_g = jnp.where(i == 0, 0.0, hg_ref[:, lanes])
            halo_v = jnp.where(i == 0, 0.0, hv_ref[:, lanes])
            wg, wv = wg_ref[:, lanes], wv_ref[:, lanes]
            bg, bv = bg_ref[:, lanes], bv_ref[:, lanes]
            for rb in range(tm // CONV_RB):
                ug = _conv_weighted(_conv_block_taps(xg_ref, halo_g, rb, lanes, kw), wg, kw) + bg
                uv = _conv_weighted(_conv_block_taps(xv_ref, halo_v, rb, lanes, kw), wv, kw) + bv
                o_ref[rb * CONV_RB:(rb + 1) * CONV_RB, lanes] = (_silu(ug) * uv).astype(bf16)
            return carry

        lax.fori_loop(0, tc // LANES, chunk, 0)

    return pl.pallas_call(
        body, name="ffn_gate_fwd", grid=(M // tm, nc),
        in_specs=[pl.BlockSpec((tm, tc), lambda i, j: (i, j)),
                  _prev_spec(tm, tc, lambda i, j: j, 0),
                  pl.BlockSpec((tm, tc), lambda i, j: (i, j + nc)),
                  _prev_spec(tm, tc, lambda i, j: j + nc, 0),
                  pl.BlockSpec((SUBLANES, tc), lambda i, j: (0, j)),
                  pl.BlockSpec((SUBLANES, tc), lambda i, j: (0, j + nc)),
                  pl.BlockSpec((1, tc), lambda i, j: (0, j)),
                  pl.BlockSpec((1, tc), lambda i, j: (0, j + nc))],
        out_specs=pl.BlockSpec((tm, tc), lambda i, j: (i, j)),
        out_shape=jax.ShapeDtypeStruct((M, D_FF), bf16),
        compiler_params=_cparams(("parallel", "parallel")),
    )(up, up, up, up, w, w, b, b)


def _ffn_gate_bwd(up, w, b, d_down, wdown):
    M = up.shape[0]
    K = d_down.shape[1]
    tm, tc, kw = ROW_TILE, D_FF // 2, FFN_CONV
    nc = D_FF // tc
    nt = M // tm

    def body(xg_ref, hg_ref, xv_ref, hv_ref, wg_ref, wv_ref, bg_ref, bv_ref, dd_ref, wd_ref,
             dxg_ref, dxv_ref, dwg_ref, dwv_ref, dbg_ref, dbv_ref, ng_ref, nv_ref, d_ref):
        i = pl.program_id(1)

        @pl.when(i == 0)
        def _():
            for r in (dwg_ref, dwv_ref, dbg_ref, dbv_ref, ng_ref, nv_ref):
                r[...] = jnp.zeros_like(r)

        d_ref[...] = _dot_nt(dd_ref[...], wd_ref[...])

        def chunk(j, carry):
            lanes = pl.ds(pl.multiple_of(j * LANES, LANES), LANES)
            halo_g = jnp.where(i == nt - 1, 0.0, hg_ref[:, lanes])
            halo_v = jnp.where(i == nt - 1, 0.0, hv_ref[:, lanes])
            wg, wv = wg_ref[:, lanes], wv_ref[:, lanes]
            bg, bv = bg_ref[:, lanes], bv_ref[:, lanes]
            nxt_g, nxt_v = ng_ref[:, lanes], nv_ref[:, lanes]
            zero = jnp.zeros((CONV_RB, LANES), f32)
            dbg, dbv = zero, zero
            dwg = [zero for _ in range(kw)]
            dwv = [zero for _ in range(kw)]
            for rb in reversed(range(tm // CONV_RB)):
                rows = slice(rb * CONV_RB, (rb + 1) * CONV_RB)
                tg = _conv_block_taps(xg_ref, halo_g, rb, lanes, kw)
                tv = _conv_block_taps(xv_ref, halo_v, rb, lanes, kw)
                ug = _conv_weighted(tg, wg, kw) + bg
                uv = _conv_weighted(tv, wv, kw) + bv
                sg = _sigmoid(ug)
                da = d_ref[rows, lanes]
                dug = da * uv * (sg * (1.0 + ug * (1.0 - sg)))
                duv = da * (ug * sg)
                dbg = dbg + dug
                dbv = dbv + duv
                dwg = [dwg[k] + dug * tg[k] for k in range(kw)]
                dwv = [dwv[k] + duv * tv[k] for k in range(kw)]
                dxg_ref[rows, lanes] = _conv_block_dx(dug, nxt_g, wg, kw).astype(bf16)
                dxv_ref[rows, lanes] = _conv_block_dx(duv, nxt_v, wv, kw).astype(bf16)
                nxt_g, nxt_v = dug[0:SUBLANES], duv[0:SUBLANES]
            ng_ref[:, lanes] = nxt_g
            nv_ref[:, lanes] = nxt_v
            dbg_ref[:, lanes] += jnp.sum(dbg, axis=0, keepdims=True)
            dbv_ref[:, lanes] += jnp.sum(dbv, axis=0, keepdims=True)
            for k in range(kw):
                dwg_ref[k:k + 1, lanes] += jnp.sum(dwg[k], axis=0, keepdims=True)
                dwv_ref[k:k + 1, lanes] += jnp.sum(dwv[k], axis=0, keepdims=True)
            return carry

        lax.fori_loop(0, tc // LANES, chunk, 0)

    tile_g = pl.BlockSpec((tm, tc), lambda j, i: (nt - 1 - i, j))
    tile_v = pl.BlockSpec((tm, tc), lambda j, i: (nt - 1 - i, j + nc))
    ext = pltpu.VMEM((SUBLANES, tc), f32)
    return pl.pallas_call(
        body, name="ffn_gate_bwd", grid=(nc, nt),
        in_specs=[tile_g, _prev_spec(tm, tc, lambda j, i: j, 1, nt),
                  tile_v, _prev_spec(tm, tc, lambda j, i: j + nc, 1, nt),
                  pl.BlockSpec((SUBLANES, tc), lambda j, i: (0, j)),
                  pl.BlockSpec((SUBLANES, tc), lambda j, i: (0, j + nc)),
                  pl.BlockSpec((1, tc), lambda j, i: (0, j)),
                  pl.BlockSpec((1, tc), lambda j, i: (0, j + nc)),
                  pl.BlockSpec((tm, K), lambda j, i: (nt - 1 - i, 0)),
                  pl.BlockSpec((tc, K), lambda j, i: (j, 0))],
        out_specs=[tile_g, tile_g,
                   pl.BlockSpec((SUBLANES, tc), lambda j, i: (0, j)),
                   pl.BlockSpec((SUBLANES, tc), lambda j, i: (0, j)),
                   pl.BlockSpec((1, tc), lambda j, i: (0, j)),
                   pl.BlockSpec((1, tc), lambda j, i: (0, j))],
        out_shape=[jax.ShapeDtypeStruct((M, D_FF), bf16), jax.ShapeDtypeStruct((M, D_FF), bf16),
                   jax.ShapeDtypeStruct((SUBLANES, D_FF), f32), jax.ShapeDtypeStruct((SUBLANES, D_FF), f32),
                   jax.ShapeDtypeStruct((1, D_FF), f32), jax.ShapeDtypeStruct((1, D_FF), f32)],
        scratch_shapes=[ext, ext, pltpu.VMEM((tm, tc), f32)],
        compiler_params=_cparams(("parallel", "arbitrary")),
    )(up, up, up, up, w, w, b, b, d_down, wdown)


def _rope_apply(blk, cos, sin):
    lane = lax.broadcasted_iota(jnp.int32, blk.shape, 1)
    half = QK_ROPE // 2
    partner = jnp.where(lane < half, pltpu.roll(blk, LANES - half, 1), pltpu.roll(blk, half, 1))
    return blk * cos + partner * sin


def _rope_unapply(d, cos, sin):
    t = d * sin
    lane = lax.broadcasted_iota(jnp.int32, d.shape, 1)
    half = QK_ROPE // 2
    partner = jnp.where(lane < half, pltpu.roll(t, LANES - half, 1), pltpu.roll(t, half, 1))
    return d * cos + partner


def _up_q_rope(q_c, g, wuq, cos, sin):
    M, K = q_c.shape
    tm = _pick(M, (768, 512, 256))

    hs = HEADS_PER_STEP

    def body(x_ref, g_ref, b_ref, c_ref, s_ref, a_ref, o_ref):
        xv = x_ref[...]
        a = (xv * _rstd(xv) * g_ref[...]).astype(bf16)
        a_ref[...] = a
        r = _dot(a, b_ref[...]) * Q_PRESCALE
        c, s = c_ref[...], s_ref[...]
        for u in range(hs):
            o_ref[u, :, 0:QK_NOPE] = r[:, u * QK_PAD:u * QK_PAD + QK_NOPE].astype(bf16)
            o_ref[u, :, QK_NOPE:QK_PAD] = _rope_apply(r[:, u * QK_PAD + QK_NOPE:(u + 1) * QK_PAD], c, s).astype(bf16)

    return pl.pallas_call(
        body, name="up_q_rope", grid=(M // tm, MLA_HEADS // hs),
        in_specs=[pl.BlockSpec((tm, K), lambda i, h: (i, 0)),
                  pl.BlockSpec((1, K), lambda i, h: (0, 0)),
                  pl.BlockSpec((K, hs * QK_PAD), lambda i, h: (0, h)),
                  pl.BlockSpec((tm, LANES), lambda i, h: (i, 0)),
                  pl.BlockSpec((tm, LANES), lambda i, h: (i, 0))],
        out_specs=[pl.BlockSpec((tm, K), lambda i, h: (i, 0)),
                   pl.BlockSpec((hs, tm, QK_PAD), lambda i, h: (h, i, 0))],
        out_shape=[jax.ShapeDtypeStruct((M, K), bf16), jax.ShapeDtypeStruct((MLA_HEADS, M, QK_PAD), bf16)],
        compiler_params=_cparams(("parallel", "arbitrary")),
    )(q_c, g, wuq, cos, sin)


def _up_kv_rope(kv_c, g, wukv, kpe_raw, cos, sin):
    M, K = kv_c.shape
    tm = _pick(M, (768, 512, 256))

    hs = HEADS_PER_STEP
    w = QK_NOPE + V_DIM

    def body(x_ref, g_ref, b_ref, pe_ref, c_ref, s_ref, a_ref, k_ref, v_ref):
        xv = x_ref[...]
        a = (xv * _rstd(xv) * g_ref[...]).astype(bf16)
        a_ref[...] = a
        r = _dot(a, b_ref[...])
        pe = _rope_apply(pe_ref[...], c_ref[...], s_ref[...]).astype(bf16)
        for u in range(hs):
            k_ref[u, :, 0:QK_NOPE] = r[:, u * w:u * w + QK_NOPE].astype(bf16)
            k_ref[u, :, QK_NOPE:QK_PAD] = pe
            v_ref[u] = r[:, u * w + QK_NOPE:(u + 1) * w].astype(bf16)

    return pl.pallas_call(
        body, name="up_kv_rope", grid=(M // tm, MLA_HEADS // hs),
        in_specs=[pl.BlockSpec((tm, K), lambda i, h: (i, 0)),
                  pl.BlockSpec((1, K), lambda i, h: (0, 0)),
                  pl.BlockSpec((K, hs * w), lambda i, h: (0, h)),
                  pl.BlockSpec((tm, LANES), lambda i, h: (i, 0)),
                  pl.BlockSpec((tm, LANES), lambda i, h: (i, 0)),
                  pl.BlockSpec((tm, LANES), lambda i, h: (i, 0))],
        out_specs=[pl.BlockSpec((tm, K), lambda i, h: (i, 0)),
                   pl.BlockSpec((hs, tm, QK_PAD), lambda i, h: (h, i, 0)),
                   pl.BlockSpec((hs, tm, V_DIM), lambda i, h: (h, i, 0))],
        out_shape=[jax.ShapeDtypeStruct((M, K), bf16), jax.ShapeDtypeStruct((MLA_HEADS, M, QK_PAD), bf16),
                   jax.ShapeDtypeStruct((MLA_HEADS, M, V_DIM), bf16)],
        compiler_params=_cparams(("parallel", "arbitrary")),
    )(kv_c, g, wukv, kpe_raw, cos, sin)


def _latent_bwd(d_full_sc, w_ref, x_ref, g_ref, a_ref, dx_ref, dg_ref, dw_ref):
    d_full = d_full_sc[...]
    dx, dgp = _rms_bwd_math(x_ref[...], g_ref[...], _dot_nt(d_full, w_ref[...]))
    dx_ref[...] = dx.astype(bf16)
    dg_ref[...] += jnp.sum(dgp, axis=0, keepdims=True)
    dw_ref[...] += _dot_tn(a_ref[...], d_full)


def _latent_bwd_call(body, name, head_inputs, head_specs, cos, sin, w, x, g, a, extra_out_specs, extra_out_shape):
    M, K = x.shape
    tm = _rt(M)
    N = w.shape[1]
    return pl.pallas_call(
        body, name=name, grid=(M // tm,),
        in_specs=head_specs + [_row(LANES, tm), _row(LANES, tm), _full((K, N)), _row(K, tm), _full((1, K)),
                               _row(K, tm)],
        out_specs=[_row(K, tm), _full((1, K)), _full((K, N))] + extra_out_specs,
        out_shape=[jax.ShapeDtypeStruct((M, K), bf16), jax.ShapeDtypeStruct((1, K), f32),
                   jax.ShapeDtypeStruct((K, N), f32)] + extra_out_shape,
        scratch_shapes=[pltpu.VMEM((tm, N), bf16)],
        compiler_params=_cparams(("arbitrary",)),
    )(*head_inputs, cos, sin, w, x, g, a)


def _q_branch_bwd(dq, cos, sin, wuq, q_c, g, qn):
    tm = _rt(q_c.shape[0])

    def body(d_ref, c_ref, s_ref, w_ref, x_ref, g_ref, a_ref, dx_ref, dg_ref, dw_ref, full_sc):
        @pl.when(pl.program_id(0) == 0)
        def _():
            dg_ref[...] = jnp.zeros_like(dg_ref)
            dw_ref[...] = jnp.zeros_like(dw_ref)

        c, s = c_ref[...], s_ref[...]
        for h in range(MLA_HEADS):
            full_sc[:, h * QK_PAD:h * QK_PAD + QK_NOPE] = (d_ref[h, :, 0:QK_NOPE] * SOFTMAX_SCALE).astype(bf16)
            full_sc[:, h * QK_PAD + QK_NOPE:(h + 1) * QK_PAD] = (_rope_unapply(
                d_ref[h, :, QK_NOPE:QK_PAD], c, s) * SOFTMAX_SCALE).astype(bf16)
        _latent_bwd(full_sc, w_ref, x_ref, g_ref, a_ref, dx_ref, dg_ref, dw_ref)

    return _latent_bwd_call(body, "q_branch_bwd", [dq],
                            [pl.BlockSpec((MLA_HEADS, tm, QK_PAD), lambda i: (0, i, 0))],
                            cos, sin, wuq, q_c, g, qn, [], [])


def _kv_branch_bwd(dk, dv, cos, sin, wukv, kv_c, g, kvn):
    M = kv_c.shape[0]
    tm = _rt(M)
    w = QK_NOPE + V_DIM

    def body(dk_ref, dv_ref, c_ref, s_ref, w_ref, x_ref, g_ref, a_ref, dx_ref, dg_ref, dw_ref, pe_ref, full_sc):
        @pl.when(pl.program_id(0) == 0)
        def _():
            dg_ref[...] = jnp.zeros_like(dg_ref)
            dw_ref[...] = jnp.zeros_like(dw_ref)

        pe = None
        for h in range(MLA_HEADS):
            full_sc[:, h * w:h * w + QK_NOPE] = dk_ref[h, :, 0:QK_NOPE].astype(bf16)
            full_sc[:, h * w + QK_NOPE:(h + 1) * w] = dv_ref[h].astype(bf16)
            t = dk_ref[h, :, QK_NOPE:QK_PAD]
            pe = t if pe is None else pe + t
        pe_ref[...] = _rope_unapply(pe, c_ref[...], s_ref[...])
        _latent_bwd(full_sc, w_ref, x_ref, g_ref, a_ref, dx_ref, dg_ref, dw_ref)

    return _latent_bwd_call(body, "kv_branch_bwd", [dk, dv],
                            [pl.BlockSpec((MLA_HEADS, tm, QK_PAD), lambda i: (0, i, 0)),
                             pl.BlockSpec((MLA_HEADS, tm, V_DIM), lambda i: (0, i, 0))],
                            cos, sin, wukv, kv_c, g, kvn, [_row(LANES, tm)],
                            [jax.ShapeDtypeStruct((M, LANES), f32)])


def _attn_tile(M):
    return 768 if (M % 768 == 0 and M >= 4 * 768) else ROW_TILE


def _col_to_row(col):
    return col.T[0:1, :]


def _hosted_exchange(refs_in, refs_out, sems, scatter, first, last):
    copies = _exchange_copies(refs_in, refs_out, *sems, scatter)

    @pl.when(first)
    def _():
        for cp in copies:
            cp.start()

    @pl.when(last)
    def _():
        for cp in copies:
            cp.wait()


def _flash_fwd(q, k, v, carried, scatter):
    H, M, _ = q.shape
    T = _attn_tile(M)
    nq = M // T
    nx = len(carried)

    def body(*refs):
        q_ref, k_ref, v_ref = refs[:3]
        o_ref, lse_ref = refs[3 + nx:5 + nx]
        sa_ref, sb_ref, m_sc, l_sc, acc_sc = refs[5 + 2 * nx:10 + 2 * nx]
        h = pl.program_id(0)
        i = pl.program_id(1)
        _hosted_exchange(refs[3:3 + nx], refs[5 + nx:5 + 2 * nx], refs[10 + 2 * nx:], scatter,
                         (h == 0) & (i == 0), (h == H - 1) & (i == nq - 1))
        qv = q_ref[0]
        m_sc[...] = jnp.full_like(m_sc, NEG)
        l_sc[...] = jnp.zeros_like(l_sc)
        acc_sc[...] = jnp.zeros_like(acc_sc)

        def scores(j, s_ref):
            off = pl.multiple_of(j * T, T)
            s_ref[...] = _dot_nt(qv, k_ref[0, pl.ds(off, T), :])

        def softmax_pv(j, s_ref, masked):
            off = pl.multiple_of(j * T, T)
            s = s_ref[...]
            if masked:
                r = lax.broadcasted_iota(jnp.int32, (T, T), 0)
                c = lax.broadcasted_iota(jnp.int32, (T, T), 1)
                s = jnp.where(r >= c, s, NEG)
            m_prev = m_sc[...]
            m_new = jnp.maximum(m_prev, jnp.max(s, axis=1, keepdims=True))
            alpha = jnp.exp2(m_prev - m_new)
            p = jnp.exp2(s - m_new[:, 0:1])
            l_sc[...] = alpha * l_sc[...] + jnp.sum(p, axis=1, keepdims=True)
            acc_sc[...] = alpha * acc_sc[...] + _dot(p.astype(bf16), v_ref[0, pl.ds(off, T), :])
            m_sc[...] = m_new

        scores(0, sa_ref)

        def pair(jj, c):
            j0 = 2 * jj
            scores(j0 + 1, sb_ref)
            softmax_pv(j0, sa_ref, False)
            scores(j0 + 2, sa_ref)
            softmax_pv(j0 + 1, sb_ref, False)
            return c

        lax.fori_loop(0, i // 2, pair, 0)

        @pl.when(i % 2 == 0)
        def _():
            softmax_pv(i, sa_ref, True)

        @pl.when(i % 2 == 1)
        def _():
            scores(i, sb_ref)
            softmax_pv(i - 1, sa_ref, False)
            softmax_pv(i, sb_ref, True)

        l = l_sc[...]
        o_ref[...] = acc_sc[...] / l
        lse_ref[0, 0] = _col_to_row(m_sc[...] + jnp.log2(l))

    any_spec = pl.BlockSpec(memory_space=pl.ANY)
    return pl.pallas_call(
        body, name="flash_fwd", grid=(H, nq),
        in_specs=[pl.BlockSpec((1, T, QK_PAD), lambda h, i: (h, i, 0)),
                  pl.BlockSpec((1, M, QK_PAD), lambda h, i: (h, 0, 0)),
                  pl.BlockSpec((1, M, V_DIM), lambda h, i: (h, 0, 0))] + [any_spec] * nx,
        out_specs=[pl.BlockSpec((T, V_DIM), lambda h, i: (i, h)),
                   pl.BlockSpec((1, 1, 1, T), lambda h, i: (h, i, 0, 0))] + [any_spec] * nx,
        out_shape=[jax.ShapeDtypeStruct((M, H * V_DIM), f32),
                   jax.ShapeDtypeStruct((H, nq, 1, T), f32)] + _exchange_shapes(carried, scatter),
        scratch_shapes=[pltpu.VMEM((T, T), f32), pltpu.VMEM((T, T), f32),
                        pltpu.VMEM((T, LANES), f32), pltpu.VMEM((T, LANES), f32),
                        pltpu.VMEM((T, V_DIM), f32)] + _exchange_sems(nx),
        compiler_params=_cparams(("arbitrary", "arbitrary")),
    )(q, k, v, *carried)


def _attn_out_bwd(o, g, d_mix, wa):
    M, K = o.shape
    H = MLA_HEADS
    T = _attn_tile(M)

    def body(o_ref, g_ref, dm_ref, w_ref, dh_ref, dl_ref, dg_ref):
        @pl.when(pl.program_id(0) == 0)
        def _():
            dg_ref[...] = jnp.zeros_like(dg_ref)

        ov = o_ref[...]
        do, dgp = _rms_bwd_math(ov, g_ref[...], _dot_nt(dm_ref[...], w_ref[...]))
        dg_ref[...] += jnp.sum(dgp, axis=0, keepdims=True)
        for h in range(H):
            sl = slice(h * V_DIM, (h + 1) * V_DIM)
            doh = do[:, sl]
            dh_ref[h] = doh.astype(bf16)
            col = jnp.sum(ov[:, sl] * doh, axis=1, keepdims=True) + jnp.zeros((T, LANES), f32)
            dl_ref[h, 0] = _col_to_row(col)

    return pl.pallas_call(
        body, name="attn_out_bwd", grid=(M // T,),
        in_specs=[_row(K, T), _full((1, K)), _row(d_mix.shape[1], T), _full(wa.shape)],
        out_specs=[pl.BlockSpec((H, T, V_DIM), lambda i: (0, i, 0)),
                   pl.BlockSpec((H, 1, 1, T), lambda i: (0, i, 0, 0)),
                   _full((1, K))],
        out_shape=[jax.ShapeDtypeStruct((H, M, V_DIM), bf16),
                   jax.ShapeDtypeStruct((H, M // T, 1, T), f32),
                   jax.ShapeDtypeStruct((1, K), f32)],
        compiler_params=_cparams(("arbitrary",)),
    )(o, g, d_mix, wa)


def _flash_bwd(q, k, v, do, lse, delta, carried, scatter):
    H, M, _ = q.shape
    T = _attn_tile(M)
    nq = M // T
    nx = len(carried)

    def body(*refs):
        q_ref, do_ref, lse_ref, dl_ref, k_ref, v_ref = refs[:6]
        dq_ref, dk_ref, dv_ref = refs[6 + nx:9 + nx]
        dk_sc, dv_sc = refs[9 + 2 * nx:11 + 2 * nx]
        j = pl.program_id(1)
        _hosted_exchange(refs[6:6 + nx], refs[9 + nx:9 + 2 * nx], refs[11 + 2 * nx:], scatter,
                         (pl.program_id(0) == 0) & (j == 0), (pl.program_id(0) == H - 1) & (j == nq - 1))

        @pl.when(j == 0)
        def _():
            dq_ref[...] = jnp.zeros_like(dq_ref)

        kt = k_ref[0]
        vt = v_ref[0]
        dk_sc[...] = jnp.zeros_like(dk_sc)
        dv_sc[...] = jnp.zeros_like(dv_sc)

        def step(i, masked):
            off = pl.multiple_of(i * T, T)
            qt = q_ref[0, pl.ds(off, T), :]
            dot_ = do_ref[0, pl.ds(off, T), :]
            st = _dot_nt(kt, qt)
            if masked:
                r = lax.broadcasted_iota(jnp.int32, (T, T), 0)
                c = lax.broadcasted_iota(jnp.int32, (T, T), 1)
                st = jnp.where(c >= r, st, NEG)
            pt = jnp.exp2(st - lse_ref[0, i])
            dv_sc[...] += _dot(pt.astype(bf16), dot_)
            dpt = _dot_nt(vt, dot_)
            dst = (pt * (dpt - dl_ref[0, i])).astype(bf16)
            dk_sc[...] += _dot(dst, qt)
            dq_ref[0, pl.ds(off, T), :] += _dot_tn(dst, kt)

        step(j, True)

        def loop_body(i, c):
            step(i, False)
            return c

        lax.fori_loop(j + 1, nq, loop_body, 0)
        dk_ref[0] = dk_sc[...] * LN2
        dv_ref[0] = dv_sc[...]

    any_spec = pl.BlockSpec(memory_space=pl.ANY)
    return pl.pallas_call(
        body, name="flash_bwd", grid=(H, nq),
        in_specs=[pl.BlockSpec((1, M, QK_PAD), lambda h, j: (h, 0, 0)),
                  pl.BlockSpec((1, M, V_DIM), lambda h, j: (h, 0, 0)),
                  pl.BlockSpec((1, nq, 1, T), lambda h, j: (h, 0, 0, 0)),
                  pl.BlockSpec((1, nq, 1, T), lambda h, j: (h, 0, 0, 0)),
                  pl.BlockSpec((1, T, QK_PAD), lambda h, j: (h, j, 0)),
                  pl.BlockSpec((1, T, V_DIM), lambda h, j: (h, j, 0))] + [any_spec] * nx,
        out_specs=[pl.BlockSpec((1, M, QK_PAD), lambda h, j: (h, 0, 0)),
                   pl.BlockSpec((1, T, QK_PAD), lambda h, j: (h, j, 0)),
                   pl.BlockSpec((1, T, V_DIM), lambda h, j: (h, j, 0))] + [any_spec] * nx,
        out_shape=[jax.ShapeDtypeStruct((H, M, QK_PAD), f32),
                   jax.ShapeDtypeStruct((H, M, QK_PAD), f32),
                   jax.ShapeDtypeStruct((H, M, V_DIM), f32)] + _exchange_shapes(carried, scatter),
        scratch_shapes=[pltpu.VMEM((T, QK_PAD), f32), pltpu.VMEM((T, V_DIM), f32)] + _exchange_sems(nx),
        compiler_params=_cparams(("arbitrary", "arbitrary")),
    )(q, do, lse, delta, k, v, *carried)


def _dt_fwd(dt_raw, bias, expand):
    M = dt_raw.shape[0]
    tm = _rt(M)

    def body(x_ref, b_ref, e_ref, o_ref, oe_ref):
        u = x_ref[...] + b_ref[...]
        sp = jnp.maximum(u, 0.0) + jnp.log(1.0 + jnp.exp(-jnp.abs(u)))
        lane = lax.broadcasted_iota(jnp.int32, u.shape, 1)
        dtp = jnp.where(lane < SSM_HEADS, sp, 0.0)
        o_ref[...] = dtp
        oe_ref[...] = _dot_hi(dtp, e_ref[...])

    return pl.pallas_call(
        body, name="dt_fwd", grid=(M // tm,),
        in_specs=[_row(LANES, tm), _full((1, LANES)), _full((LANES, D_SSM))],
        out_specs=[_row(LANES, tm), _row(D_SSM, tm)],
        out_shape=[jax.ShapeDtypeStruct((M, LANES), f32), jax.ShapeDtypeStruct((M, D_SSM), f32)],
        compiler_params=_cparams(("parallel",)),
    )(dt_raw, bias, expand)


def _dt_bwd(dt_raw, bias, ddt):
    M = dt_raw.shape[0]
    tm = _rt(M)

    def body(x_ref, b_ref, d_ref, o_ref, db_ref):
        @pl.when(pl.program_id(0) == 0)
        def _():
            db_ref[...] = jnp.zeros_like(db_ref)

        u = x_ref[...] + b_ref[...]
        lane = lax.broadcasted_iota(jnp.int32, u.shape, 1)
        g = jnp.where(lane < SSM_HEADS, d_ref[...] * _sigmoid(u), 0.0)
        o_ref[...] = g
        db_ref[...] += jnp.sum(g, axis=0, keepdims=True)

    return pl.pallas_call(
        body, name="dt_bwd", grid=(M // tm,),
        in_specs=[_row(LANES, tm), _full((1, LANES)), _row(LANES, tm)],
        out_specs=[_row(LANES, tm), _full((1, LANES))],
        out_shape=[jax.ShapeDtypeStruct((M, LANES), f32), jax.ShapeDtypeStruct((1, LANES), f32)],
        compiler_params=_cparams(("arbitrary",)),
    )(dt_raw, bias, ddt)


SSM_GW = SSM_HPG * SSM_P
SSM_PAIRS = SSM_GW // LANES


def _ssd_common(dte_ref, dtt_ref, ae_ref, acol_ref):
    Q = CHUNK
    r = lax.broadcasted_iota(jnp.int32, (Q, Q), 0)
    c = lax.broadcasted_iota(jnp.int32, (Q, Q), 1)
    causal = r >= c
    anti = c >= r
    tril = causal.astype(f32)
    triu = anti.astype(f32)
    dt_e = dte_ref[...]
    cs_e = _dot_hi(tril, dt_e * ae_ref[...])
    cst = _dot_hi(dtt_ref[...] * acol_ref[...], triu)
    cs_last = cs_e[Q - 1:Q, :]
    return causal, anti, triu, dt_e, cs_e, cst, jnp.exp(cs_e), jnp.exp(cs_last - cs_e), jnp.exp(cs_last)


def _half_masks():
    lane = lax.broadcasted_iota(jnp.int32, (CHUNK, LANES), 1)
    lo = lane < SSM_P
    return lo, jnp.logical_not(lo)


def _ssd_fwd(xbc_c, dt_e, dtt, a_e, a_col, z, d_exp, g_ssm):
    M = xbc_c.shape[0]
    Q = CHUNK
    nch = M // Q
    gw = D_SSM // SSM_GROUPS

    def body(x_ref, dte_ref, dtt_ref, ae_ref, acol_ref, z_ref, dexp_ref, gn_ref, y_ref, hin_ref, o_ref, ht_sc):
        @pl.when(pl.program_id(0) == 0)
        def _():
            ht_sc[...] = jnp.zeros_like(ht_sc)

        causal, _, _, dt_e, cs_e, cst, ecs_e, dte_e, elast_e = _ssd_common(dte_ref, dtt_ref, ae_ref, acol_ref)
        halves = _half_masks()
        for g in range(SSM_GROUPS):
            g0 = g * SSM_GW
            bg = x_ref[:, D_SSM + g * SSM_N:D_SSM + (g + 1) * SSM_N]
            cg = x_ref[:, D_SSM + D_BC + g * SSM_N:D_SSM + D_BC + (g + 1) * SSM_N]
            bg_b = bg.astype(bf16)
            cg_b = cg.astype(bf16)
            cb = _dot_nt(cg_b, bg_b)
            bgt_b = bg.T.astype(bf16)
            xdt_g = x_ref[:, g0:g0 + SSM_GW] * dt_e[:, g0:g0 + SSM_GW]
            ht = ht_sc[g]
            hin_ref[0, g] = ht
            y_off = _dot(cg_b, ht.astype(bf16)) * ecs_e[:, g0:g0 + SSM_GW]
            for pr in range(SSM_PAIRS):
                p0 = pr * LANES
                xdt_p = xdt_g[:, p0:p0 + LANES]
                acc = y_off[:, p0:p0 + LANES]
                for half in range(2):
                    h = g * SSM_HPG + pr * 2 + half
                    seg = cs_e[:, h * SSM_P:h * SSM_P + 1] - cst[h:h + 1, :]
                    lm = jnp.exp(jnp.where(causal, seg, -jnp.inf))
                    xm = jnp.where(halves[half], xdt_p, 0.0).astype(bf16)
                    acc = acc + _dot((cb * lm).astype(bf16), xm)
                y_ref[:, g0 + p0:g0 + p0 + LANES] = acc
            st = _dot(bgt_b, (xdt_g * dte_e[:, g0:g0 + SSM_GW]).astype(bf16))
            ht_sc[g] = ht * elast_e[:, g0:g0 + SSM_GW] + st
        yg = (y_ref[...] + dexp_ref[...] * x_ref[:, 0:D_SSM]) * _silu(z_ref[...])
        for gi in range(SSM_GROUPS):
            blk = yg[:, gi * gw:(gi + 1) * gw]
            o_ref[:, gi * gw:(gi + 1) * gw] = (blk * _rstd(blk) * gn_ref[:, gi * gw:(gi + 1) * gw]).astype(bf16)

    chunk_rows = pl.BlockSpec((Q, D_SSM), lambda c: (c, 0))
    return pl.pallas_call(
        body, name="ssd_fwd", grid=(nch,),
        in_specs=[pl.BlockSpec((Q, D_XBC), lambda c: (c, 0)), chunk_rows,
                  pl.BlockSpec((SSM_HEADS, Q), lambda c: (0, c)),
                  _full((1, D_SSM)), _full((SSM_HEADS, LANES)), chunk_rows, _full((1, D_SSM)), _full((1, D_SSM))],
        out_specs=[chunk_rows, pl.BlockSpec((1, SSM_GROUPS, SSM_N, SSM_GW), lambda c: (c, 0, 0, 0)), chunk_rows],
        out_shape=[jax.ShapeDtypeStruct((M, D_SSM), f32),
                   jax.ShapeDtypeStruct((nch, SSM_GROUPS, SSM_N, SSM_GW), f32),
                   jax.ShapeDtypeStruct((M, D_SSM), bf16)],
        scratch_shapes=[pltpu.VMEM((SSM_GROUPS, SSM_N, SSM_GW), f32)],
        compiler_params=_cparams(("arbitrary",)),
    )(xbc_c, dt_e, dtt, a_e, a_col, z, d_exp, g_ssm)


def _ssd_bwd(xbc_c, dtp, dt_e, dtt, a_row, a_e, a_col, hin, y, z, d_mix, ws, g_ssm, d_exp, head_ind):
    M = xbc_c.shape[0]
    Q = CHUNK
    nch = M // Q
    rev = lambda c: nch - 1 - c

    gw = D_SSM // SSM_GROUPS

    def body(x_ref, dtp_ref, dte_ref, dtt_ref, arow_ref, ae_ref, acol_ref, hin_ref, y_ref, zz_ref, dm_ref, ws_ref,
             gn_ref, dexp_ref, ind_ref, dx_ref, ddt_ref, da_ref, dz_ref, dgn_ref, dd_ref,
             dht_sc, z_sc, z1_sc, last_sc, ct_sc, dy_ref, ddc_sc):
        @pl.when(pl.program_id(0) == 0)
        def _():
            dht_sc[...] = jnp.zeros_like(dht_sc)
            da_ref[...] = jnp.zeros_like(da_ref)
            last_sc[...] = jnp.zeros_like(last_sc)
            ct_sc[...] = jnp.zeros_like(ct_sc)
            dgn_ref[...] = jnp.zeros_like(dgn_ref)
            ddc_sc[...] = jnp.zeros_like(ddc_sc)

        zv = zz_ref[...]
        xv = x_ref[:, 0:D_SSM]
        sz = _silu(zv)
        yd = y_ref[...] + dexp_ref[...] * xv
        yg = yd * sz
        dov = _dot_nt(dm_ref[...], ws_ref[...])
        for gi in range(SSM_GROUPS):
            sl = slice(gi * gw, (gi + 1) * gw)
            dyg, dgp = _rms_bwd_math(yg[:, sl], gn_ref[:, sl], dov[:, sl])
            dgn_ref[:, sl] += jnp.sum(dgp, axis=0, keepdims=True)
            dyd = dyg * sz[:, sl]
            dy_ref[:, sl] = dyd
            dz_ref[:, sl] = (dyg * yd[:, sl] * _dsilu(zv[:, sl])).astype(bf16)
            ddc_sc[:, sl] += jnp.sum(dyd * xv[:, sl], axis=0, keepdims=True)

        @pl.when(pl.program_id(0) == nch - 1)
        def _():
            dd_ref[...] = _dot_hi(ddc_sc[...], ind_ref[...])

        causal, anti, triu, dt_e, cs_e, cst, ecs_e, dte_e, elast_e = _ssd_common(dte_ref, dtt_ref, ae_ref, acol_ref)
        halves = _half_masks()
        lane = lax.broadcasted_iota(jnp.int32, (Q, LANES), 1)
        rsum = jnp.zeros((Q, LANES), f32)
        for g in range(SSM_GROUPS):
            g0 = g * SSM_GW
            gs = slice(g0, g0 + SSM_GW)
            b0 = D_SSM + g * SSM_N
            c0 = D_SSM + D_BC + g * SSM_N
            bg = x_ref[:, b0:b0 + SSM_N]
            cg = x_ref[:, c0:c0 + SSM_N]
            bg_b = bg.astype(bf16)
            cg_b = cg.astype(bf16)
            cgt_b = cg.T.astype(bf16)
            cbt = _dot_nt(bg_b, cg_b)
            cb = _dot_nt(cg_b, bg_b)
            x_g = x_ref[:, gs]
            dt_g = dt_e[:, gs]
            xdt_g = x_g * dt_g
            dy_g = dy_ref[:, gs]
            ht = hin_ref[0, g]
            ht_b = ht.astype(bf16)
            dht = dht_sc[g]
            dht_b = dht.astype(bf16)
            dye_b = (dy_g * ecs_e[:, gs]).astype(bf16)
            dc = _dot_nt(dye_b, ht_b)
            dht_new = dht * elast_e[:, gs] + _dot(cgt_b, dye_b)
            e = _dot(bg_b, dht_b)
            xdtd = xdt_g * dte_e[:, gs]
            db = _dot_nt(xdtd.astype(bf16), dht_b)
            dxdt_state = e * dte_e[:, gs]
            exd = e * xdtd
            z1_sc[:, gs] = dy_g * (_dot(cg_b, ht_b) * ecs_e[:, gs]) - exd
            last_sc[0:1, gs] = (jnp.sum(exd, axis=0, keepdims=True)
                                + jnp.sum(dht * ht, axis=0, keepdims=True) * elast_e[:, gs])
            dg_acc = jnp.zeros((Q, Q), f32)
            for pr in range(SSM_PAIRS):
                p0 = pr * LANES
                ps = slice(g0 + p0, g0 + p0 + LANES)
                dy_p = dy_g[:, p0:p0 + LANES]
                xdt_pb = xdt_g[:, p0:p0 + LANES].astype(bf16)
                acc = dxdt_state[:, p0:p0 + LANES]
                for half in range(2):
                    h = g * SSM_HPG + pr * 2 + half
                    seg = cs_e[:, h * SSM_P:h * SSM_P + 1] - cst[h:h + 1, :]
                    lm = jnp.exp(jnp.where(causal, seg, -jnp.inf))
                    lmt = jnp.exp(jnp.where(anti, -seg, -jnp.inf))
                    dym = jnp.where(halves[half], dy_p, 0.0).astype(bf16)
                    acc = acc + _dot((cbt * lmt).astype(bf16), dym)
                    dml = _dot_nt(dym, xdt_pb) * lm
                    dg_acc = dg_acc + dml
                    w = dml * cb
                    rsum = rsum + jnp.where(lane == h, jnp.sum(w, axis=1, keepdims=True), 0.0)
                    ct_sc[h:h + 1, :] = jnp.sum(w, axis=0, keepdims=True)
                dx_ref[:, ps] = acc * dt_g[:, p0:p0 + LANES] + dexp_ref[:, ps] * dy_p
                z_sc[:, ps] = acc * x_g[:, p0:p0 + LANES]
            dg_b = dg_acc.astype(bf16)
            dx_ref[:, c0:c0 + SSM_N] = dc + _dot(dg_b, bg_b)
            dx_ref[:, b0:b0 + SSM_N] = db + _dot_tn(dg_b, cg_b)
            dht_sc[g] = dht_new
        s1 = _dot_hi(z1_sc[...], ind_ref[...])
        s2 = _dot_hi(z_sc[...], ind_ref[...])
        last = _dot_hi(last_sc[...], ind_ref[...])[0:1, :]
        dtp = dtp_ref[...]
        row = lax.broadcasted_iota(jnp.int32, (Q, LANES), 0)
        dcs = s1 + rsum + jnp.where(row == Q - 1, last, 0.0)
        tril = causal.astype(f32)
        da = _dot_hi(triu, dcs) - _dot_hi(ct_sc[...], tril).T
        ddt_ref[...] = s2 + da * arow_ref[...]
        da_ref[...] += jnp.sum(da * dtp, axis=0, keepdims=True)

    chunk_rows = pl.BlockSpec((Q, D_SSM), lambda c: (rev(c), 0))
    return pl.pallas_call(
        body, name="ssd_bwd", grid=(nch,),
        in_specs=[pl.BlockSpec((Q, D_XBC), lambda c: (rev(c), 0)),
                  pl.BlockSpec((Q, LANES), lambda c: (rev(c), 0)),
                  pl.BlockSpec((Q, D_SSM), lambda c: (rev(c), 0)),
                  pl.BlockSpec((SSM_HEADS, Q), lambda c: (0, rev(c))),
                  _full((1, LANES)), _full((1, D_SSM)), _full((SSM_HEADS, LANES)),
                  pl.BlockSpec((1, SSM_GROUPS, SSM_N, SSM_GW), lambda c: (rev(c), 0, 0, 0)),
                  chunk_rows, chunk_rows, pl.BlockSpec((Q, d_mix.shape[1]), lambda c: (rev(c), 0)), _full(ws.shape),
                  _full((1, D_SSM)), _full((1, D_SSM)), _full((D_SSM, LANES))],
        out_specs=[pl.BlockSpec((Q, D_XBC), lambda c: (rev(c), 0)),
                   pl.BlockSpec((Q, LANES), lambda c: (rev(c), 0)),
                   _full((1, LANES)), chunk_rows, _full((1, D_SSM)), _full((1, LANES))],
        out_shape=[jax.ShapeDtypeStruct((M, D_XBC), f32), jax.ShapeDtypeStruct((M, LANES), f32),
                   jax.ShapeDtypeStruct((1, LANES), f32), jax.ShapeDtypeStruct((M, D_SSM), bf16),
                   jax.ShapeDtypeStruct((1, D_SSM), f32), jax.ShapeDtypeStruct((1, LANES), f32)],
        scratch_shapes=[pltpu.VMEM((SSM_GROUPS, SSM_N, SSM_GW), f32), pltpu.VMEM((Q, D_SSM), f32),
                        pltpu.VMEM((Q, D_SSM), f32), pltpu.VMEM((SUBLANES, D_SSM), f32),
                        pltpu.VMEM((LANES, Q), f32), pltpu.VMEM((Q, D_SSM), f32), pltpu.VMEM((1, D_SSM), f32)],
        compiler_params=_cparams(("arbitrary",)),
    )(xbc_c, dtp, dt_e, dtt, a_row, a_e, a_col, hin, y, z, d_mix, ws, g_ssm, d_exp, head_ind)


_PEER_FLIPS = [(0, 0, 1), (0, 1, 0), (0, 1, 1), (1, 0, 0), (1, 0, 1), (1, 1, 0), (1, 1, 1)]


def _exchange_copies(ins, outs, send_sems, recv_sems, loc_sems, scatter):
    n = len(ins)
    x, y, c = lax.axis_index("x"), lax.axis_index("y"), lax.axis_index("c")
    me = 4 * x + 2 * y + c
    copies = []
    for a in range(n):
        src = ins[a].at[me] if scatter else ins[a]
        copies.append(pltpu.make_async_copy(src, outs[a].at[me], loc_sems.at[a]))
    for p, (fx, fy, fc) in enumerate(_PEER_FLIPS):
        tx = 1 - x if fx else x
        ty = 1 - y if fy else y
        tc = 1 - c if fc else c
        tgt = 4 * tx + 2 * ty + tc
        for a in range(n):
            src = ins[a].at[tgt] if scatter else ins[a]
            copies.append(pltpu.make_async_remote_copy(
                src_ref=src, dst_ref=outs[a].at[me],
                send_sem=send_sems.at[p * n + a], recv_sem=recv_sems.at[p * n + a],
                device_id=(tx, ty, tc), device_id_type=_MESH))
    return copies


def _exchange_shapes(arrays, scatter):
    return [jax.ShapeDtypeStruct(a.shape if scatter else (N_DEV,) + a.shape, a.dtype) for a in arrays]


def _exchange_sems(n):
    return [pltpu.SemaphoreType.DMA((7 * n,)), pltpu.SemaphoreType.DMA((7 * n,)), pltpu.SemaphoreType.DMA((n,))]


def _gather_two_level(arrays, name):
    n = len(arrays)

    def body(*refs):
        ins, outs = refs[:n], refs[n:2 * n]
        send_sems, recv_sems, loc_sems = refs[2 * n:]
        x, y, c = lax.axis_index("x"), lax.axis_index("y"), lax.axis_index("c")
        me, sibling = (x, y, c), (x, y, 1 - c)
        chips = [(1 - x, y), (x, 1 - y), (1 - x, 1 - y)]

        def slot(a, dev):
            return outs[a].at[4 * dev[0] + 2 * dev[1] + dev[2]]

        def copy(a, k, block, to, src=None):
            return pltpu.make_async_remote_copy(
                src_ref=slot(a, block) if src is None else src, dst_ref=slot(a, block),
                send_sem=send_sems.at[7 * a + k], recv_sem=recv_sems.at[7 * a + k],
                device_id=to, device_id_type=_MESH)

        mine = [pltpu.make_async_copy(ins[a], slot(a, me), loc_sems.at[a]) for a in range(n)]
        first = []
        for a in range(n):
            first.append(copy(a, 0, me, sibling, src=ins[a]))
            first += [copy(a, 1 + j, me, (*chip, c), src=ins[a]) for j, chip in enumerate(chips)]
        for cp in mine + first:
            cp.start()
        passed = []
        for j, chip in enumerate(chips):
            for a in range(n):
                copy(a, 1 + j, (*chip, c), me).wait_recv()
                cp = copy(a, 4 + j, (*chip, c), sibling)
                cp.start()
                passed.append(cp)
        for a in range(n):
            copy(a, 0, sibling, me).wait_recv()
            for j, chip in enumerate(chips):
                copy(a, 4 + j, (*chip, 1 - c), me).wait_recv()
        for cp in first + passed:
            cp.wait_send()
        for cp in mine:
            cp.wait()

    any_spec = pl.BlockSpec(memory_space=pl.ANY)
    return pl.pallas_call(
        body, name=name, in_specs=[any_spec] * n, out_specs=[any_spec] * n,
        out_shape=_exchange_shapes(arrays, False), scratch_shapes=_exchange_sems(n),
    )(*arrays)


def _exchange_tail(scattered, gathered, name):
    ns, ng = len(scattered), len(gathered)
    n = ns + ng

    def body(*refs):
        sems = refs[2 * n:]
        copies = (_exchange_copies(refs[:ns], refs[n:n + ns], *sems[:3], True)
                  + _exchange_copies(refs[ns:n], refs[n + ns:2 * n], *sems[3:], False))
        for cp in copies:
            cp.start()
        for cp in copies:
            cp.wait()

    any_spec = pl.BlockSpec(memory_space=pl.ANY)
    return pl.pallas_call(
        body, name=name, in_specs=[any_spec] * n, out_specs=[any_spec] * n,
        out_shape=_exchange_shapes(scattered, True) + _exchange_shapes(gathered, False),
        scratch_shapes=_exchange_sems(ns) + _exchange_sems(ng),
    )(*scattered, *gathered)


def _adamw_math(g, w, m, v):
    c1 = 1.0 - ADAM_B1 ** ADAM_STEP
    c2 = 1.0 - ADAM_B2 ** ADAM_STEP
    mn = ADAM_B1 * m + (1.0 - ADAM_B1) * g
    vn = ADAM_B2 * v + (1.0 - ADAM_B2) * (g * g)
    m_hat = mn / c1
    v_hat = vn / c2
    return -ADAM_LR * (m_hat / (jnp.sqrt(v_hat) + ADAM_EPS) + ADAM_WD * w), mn, vn


def _adamw(parts, w, m, v, name):
    R, C = w.shape
    tr = _pick(R, (PACK_ROW_TILE, 64, 32, 16, 8))

    def body(p_ref, w_ref, m_ref, v_ref, g_ref, d_ref, nm_ref, nv_ref):
        g = p_ref[0].astype(f32)
        for s in range(1, N_DEV):
            g = g + p_ref[s].astype(f32)
        g_ref[...] = g
        d_ref[...], nm_ref[...], nv_ref[...] = _adamw_math(g, w_ref[...], m_ref[...], v_ref[...])

    spec = pl.BlockSpec((tr, C), lambda i: (i, 0))
    return pl.pallas_call(
        body, name=name, grid=(R // tr,),
        in_specs=[pl.BlockSpec((N_DEV, tr, C), lambda i: (0, i, 0)), spec, spec, spec],
        out_specs=[spec] * 4, out_shape=[jax.ShapeDtypeStruct((R, C), f32)] * 4,
        compiler_params=_cparams(("parallel",)),
    )(parts, w, m, v)


def _adamw_replicated(parts, ws, ms, vs):
    n = len(ws)
    R = parts.shape[1]
    sizes = [int(w.shape[1]) for w in ws]

    def body(*refs):
        p_ref = refs[0]
        w_refs, m_refs, v_refs = refs[1:1 + n], refs[1 + n:1 + 2 * n], refs[1 + 2 * n:1 + 3 * n]
        loss_ref = refs[1 + 3 * n]
        outs = refs[2 + 3 * n:]
        g_all = p_ref[0]
        for s in range(1, N_DEV):
            g_all = g_all + p_ref[s]
        row = 0
        for p in range(n):
            pieces, left = [], sizes[p]
            while left > 0:
                take = min(left, PACK_W)
                pieces.append(g_all[row:row + 1, 0:take])
                left -= take
                row += 1
            g = pieces[0] if len(pieces) == 1 else jnp.concatenate(pieces, axis=1)
            d, mn, vn = _adamw_math(g, w_refs[p][...], m_refs[p][...], v_refs[p][...])
            outs[4 * p][...] = g
            outs[4 * p + 1][...] = d
            outs[4 * p + 2][...] = mn
            outs[4 * p + 3][...] = vn
        loss_ref[...] = g_all[row:row + 1, 0:LANES]

    in_specs = [_full((N_DEV, R, PACK_W))] + [_full((1, s)) for s in sizes] * 3
    out_specs = [_full((1, LANES))]
    out_shape = [jax.ShapeDtypeStruct((1, LANES), f32)]
    for s in sizes:
        out_specs += [_full((1, s))] * 4
        out_shape += [jax.ShapeDtypeStruct((1, s), f32)] * 4
    res = pl.pallas_call(
        body, name="adamw_replicated", in_specs=in_specs, out_specs=out_specs, out_shape=out_shape,
        compiler_params=pltpu.CompilerParams(vmem_limit_bytes=VMEM_LIMIT),
    )(parts, *ws, *ms, *vs)
    return res[0], [res[1 + 4 * p:5 + 4 * p] for p in range(n)]


def _flat_rows(a, lead_ndim):
    lead = a.shape[:lead_ndim]
    n = int(np.prod(a.shape[lead_ndim:]))
    a = a.reshape(lead + (n,))
    pad = (-n) % PACK_W
    if pad:
        a = jnp.pad(a, [(0, 0)] * lead_ndim + [(0, pad)])
    return a.reshape(lead + ((n + pad) // PACK_W, PACK_W))


def _pack(arrays, lead_ndim, total_rows, dtype):
    rows = [_flat_rows(a.astype(dtype), lead_ndim) for a in arrays]
    cat = jnp.concatenate(rows, axis=lead_ndim)
    pad = total_rows - cat.shape[lead_ndim]
    if pad:
        cat = jnp.pad(cat, [(0, 0)] * lead_ndim + [(0, pad), (0, 0)])
    return cat


def _unpack(buf, shapes, lead_ndim):
    out = []
    r = 0
    lead = buf.shape[:lead_ndim]
    for shp in shapes:
        n = int(np.prod(shp))
        nr = -(-n // PACK_W)
        piece = lax.slice_in_dim(buf, r, r + nr, axis=lead_ndim)
        piece = piece.reshape(lead + (nr * PACK_W,))
        piece = lax.slice_in_dim(piece, 0, n, axis=lead_ndim)
        out.append(piece.reshape(lead + tuple(shp)))
        r += nr
    return out


def _round_up(n, m):
    return -(-n // m) * m


def kernel(x, meta_tokens, norm_mix_pre, norm_mix_post, norm_ffn_pre, norm_ffn_post, w_in, q_a_norm, w_uq, kv_a_norm, w_ukv, attn_out_norm, ssm_conv_w, ssm_conv_b, ssm_dt_bias, ssm_A_log, ssm_D, ssm_norm, w_out, w_up, ffn_conv_w, ffn_conv_b, w_down, loss_target, m_meta_tokens, m_norm_mix_pre, m_norm_mix_post, m_norm_ffn_pre, m_norm_ffn_post, m_w_in, m_q_a_norm, m_w_uq, m_kv_a_norm, m_w_ukv, m_attn_out_norm, m_ssm_conv_w, m_ssm_conv_b, m_ssm_dt_bias, m_ssm_A_log, m_ssm_D, m_ssm_norm, m_w_out, m_w_up, m_ffn_conv_w, m_ffn_conv_b, m_w_down, v_meta_tokens, v_norm_mix_pre, v_norm_mix_post, v_norm_ffn_pre, v_norm_ffn_post, v_w_in, v_q_a_norm, v_w_uq, v_kv_a_norm, v_w_ukv, v_attn_out_norm, v_ssm_conv_w, v_ssm_conv_b, v_ssm_dt_bias, v_ssm_A_log, v_ssm_D, v_ssm_norm, v_w_out, v_w_up, v_ffn_conv_w, v_ffn_conv_b, v_w_down):
    seq = x.shape[1]
    n_real = N_META + seq
    Lp = _round_up(n_real, 768) if n_real > 2048 else _round_up(n_real, ROW_TILE)
    D = D_MODEL

    early_w = [w_uq, w_ukv]
    late_w = [w_out, w_down]
    sharded_s = [meta_tokens, ssm_conv_w, ffn_conv_w]
    grp_a = dict(names=["w_out", "w_down"], w=late_w, m=[m_w_out, m_w_down],
                 v=[v_w_out, v_w_down])
    grp_b = dict(names=["w_uq", "w_ukv", "ssm_conv_w", "ffn_conv_w"],
                 w=early_w + [ssm_conv_w, ffn_conv_w],
                 m=[m_w_uq, m_w_ukv, m_ssm_conv_w, m_ffn_conv_w],
                 v=[v_w_uq, v_w_ukv, v_ssm_conv_w, v_ffn_conv_w])
    grp_meta = dict(names=["meta_tokens"], w=[meta_tokens], m=[m_meta_tokens], v=[v_meta_tokens])
    repl_w = [norm_mix_pre, norm_mix_post, norm_ffn_pre, norm_ffn_post, q_a_norm, kv_a_norm, attn_out_norm,
              ssm_conv_b, ssm_dt_bias, ssm_A_log, ssm_D, ssm_norm, ffn_conv_b]
    repl_m = [m_norm_mix_pre, m_norm_mix_post, m_norm_ffn_pre, m_norm_ffn_post, m_q_a_norm, m_kv_a_norm,
              m_attn_out_norm, m_ssm_conv_b, m_ssm_dt_bias, m_ssm_A_log, m_ssm_D, m_ssm_norm, m_ffn_conv_b]
    repl_v = [v_norm_mix_pre, v_norm_mix_post, v_norm_ffn_pre, v_norm_ffn_post, v_q_a_norm, v_kv_a_norm,
              v_attn_out_norm, v_ssm_conv_b, v_ssm_dt_bias, v_ssm_A_log, v_ssm_D, v_ssm_norm, v_ffn_conv_b]

    def pack_rows(arrs, lead):
        return _round_up(sum(-(-int(np.prod(a.shape[lead:])) // PACK_W) for a in arrs), 16)

    wb = _pack(early_w, 0, pack_rows(early_w, 0), bf16)
    wl = _pack(late_w, 0, pack_rows(late_w, 0), bf16)
    ws = _pack(sharded_s, 0, pack_rows(sharded_s, 0), f32)
    wb_all, ws_all, win_all = _gather_two_level([wb, ws, w_in[0].astype(bf16)], "gather_weights")
    g_w_uq, g_w_ukv = _unpack(wb_all, [a.shape for a in early_w], 1)
    g_meta, g_sconv, g_fconv = _unpack(ws_all, [a.shape for a in sharded_s], 1)

    def cols(gathered):
        t = gathered[:, 0]
        return jnp.transpose(t, (1, 0, 2)).reshape(t.shape[1], N_DEV * t.shape[2])

    win = cols(win_all[:, None])
    o = np.cumsum((0, Q_RANK, KV_RANK, QK_ROPE, D_SSM, D_XBC, SSM_HEADS))
    w_q, w_kv = win[:, o[0]:o[1]], win[:, o[1]:o[2]]
    w_rope = jnp.pad(win[:, o[2]:o[3]], ((0, 0), (0, LANES - QK_ROPE)))
    w_z, w_xbc = win[:, o[3]:o[4]], win[:, o[4]:o[5]]
    w_dt = jnp.pad(win[:, o[5]:o[6]], ((0, 0), (0, LANES - SSM_HEADS)))
    wuq = g_w_uq.reshape(Q_RANK, MLA_HEADS, QK_NOPE + QK_ROPE)
    wuq = jnp.pad(wuq, ((0, 0), (0, 0), (0, QK_PAD - QK_NOPE - QK_ROPE))).reshape(Q_RANK, MLA_HEADS * QK_PAD)
    wukv = g_w_ukv.reshape(KV_RANK, MLA_HEADS * (QK_NOPE + V_DIM))
    meta_full = jnp.transpose(g_meta, (1, 0, 2)).reshape(N_META, D)
    sconv_w = jnp.pad(cols(g_sconv), ((0, SUBLANES - SSM_CONV), (0, 0)))
    fconv_w = jnp.pad(cols(g_fconv), ((0, SUBLANES - FFN_CONV), (0, 0)))

    pos = jnp.arange(Lp, dtype=f32)
    inv = ROPE_THETA ** (-jnp.arange(0, QK_ROPE, 2, dtype=f32) / QK_ROPE)
    ang = pos[:, None] * inv[None, :]
    cs_, sn_ = jnp.cos(ang), jnp.sin(ang)
    zpad = jnp.zeros((Lp, LANES - QK_ROPE), f32)
    cos_t = jnp.concatenate([cs_, cs_, zpad], axis=1)
    sin_t = jnp.concatenate([-sn_, sn_, zpad], axis=1)
    dt_bias_p = jnp.pad(ssm_dt_bias, ((0, 0), (0, LANES - SSM_HEADS)))
    a_neg = -jnp.exp(ssm_A_log)
    a_row = jnp.pad(a_neg, ((0, 0), (0, LANES - SSM_HEADS)))
    a_col = jnp.broadcast_to(a_neg.reshape(SSM_HEADS, 1), (SSM_HEADS, LANES))
    d_exp = jnp.repeat(ssm_D, SSM_P, axis=1)
    a_e = jnp.repeat(a_neg, SSM_P, axis=1)
    head_ind = (jnp.arange(D_SSM)[:, None] // SSM_P == jnp.arange(LANES)[None, :]).astype(f32)

    xb = x[0]
    h0 = jnp.concatenate([meta_full, xb, jnp.zeros((Lp - n_real, D), f32)], axis=0)
    tgt = jnp.pad(loss_target[0], ((N_META, Lp - n_real), (0, 0)))
    hn1, q_c, kv_c, kpe_raw, z, xbc, dt_raw = _in_proj(h0, norm_mix_pre, [w_q, w_kv, w_rope, w_z, w_xbc, w_dt])

    qn, qh = _up_q_rope(q_c, q_a_norm, wuq, cos_t, sin_t)
    kvn, kh, vh = _up_kv_rope(kv_c, kv_a_norm, wukv, kpe_raw, cos_t, sin_t)
    attn, lse, wl_all, wup_all = _flash_fwd(qh, kh, vh, [wl, w_up[0].astype(bf16)], False)
    g_w_out, g_w_down = _unpack(wl_all, [a.shape for a in late_w], 1)
    wout = g_w_out.reshape(D_ATTN + D_SSM, D)
    wout_a, wout_s = wout[:D_ATTN], wout[D_ATTN:]
    wup = cols(wup_all[:, None])
    wdown = g_w_down.reshape(D_FF, D)

    xbc_c = _ssm_conv_fwd(xbc, sconv_w, ssm_conv_b)
    dtp, dt_e = _dt_fwd(dt_raw, dt_bias_p, jnp.transpose(head_ind))
    dtt = jnp.transpose(dtp[:, :SSM_HEADS])
    y_ssd, hin, ssm = _ssd_fwd(xbc_c, dt_e, dtt, a_e, a_col, z, d_exp, ssm_norm)

    an, mix, h1, hn2 = _out_proj_resid(attn, attn_out_norm, ssm, wout_a, wout_s, h0, norm_mix_post, norm_ffn_pre)
    up = _mm(hn2, wup, False, "ffn_up")
    act = _ffn_gate_fwd(up, fconv_w, ffn_conv_b)
    dh2, d_down, dg_ffn_post, loss_part = _final(h1, act, wdown, norm_ffn_post, tgt, n_real)

    dw_down = _mm_tn(act, d_down, "ffn_down_dw")
    dup_g, dup_v, dwc_g, dwc_v, dbc_g, dbc_v = _ffn_gate_bwd(up, fconv_w, ffn_conv_b, d_down, wdown)
    dw_up = jnp.concatenate([_mm_tn(hn2, dup_g, "ffn_up_dw_g"), _mm_tn(hn2, dup_v, "ffn_up_dw_v")], axis=1)
    dh1, d_mix, dg_ffn_pre, dg_mix_post = _mid_bwd(h1, norm_ffn_pre, dup_g, dup_v, wup, dh2, mix, norm_mix_post)
    dw_out = jnp.concatenate([_mm_tn(an, d_mix, "out_proj_dw_a"), _mm_tn(ssm, d_mix, "out_proj_dw_s")], axis=0)

    do_h, delta, dg_attn_out = _attn_out_bwd(attn, attn_out_norm, d_mix, wout_a)
    def col_blocks(gm):
        r, cc = gm.shape
        return jnp.transpose(gm.reshape(r, N_DEV, cc // N_DEV), (1, 0, 2))

    blocks_a = [dw_out.reshape(N_DEV, (D_ATTN + D_SSM) // N_DEV, D), dw_down.reshape(N_DEV, D_FF // N_DEV, D)]
    gpack_a = _pack(blocks_a, 1, pack_rows(blocks_a, 1), bf16)
    dqh, dkh, dvh, gparts_a, gparts_up = _flash_bwd(qh, kh, vh, do_h, lse, delta,
                                                    [gpack_a, col_blocks(dw_up).astype(bf16)], True)
    d_q_c, dg_q, dw_uq = _q_branch_bwd(dqh, cos_t, sin_t, wuq, q_c, q_a_norm, qn)
    d_kv_c, dg_kv, dw_ukv, d_kpe_raw = _kv_branch_bwd(dkh, dvh, cos_t, sin_t, wukv, kv_c, kv_a_norm, kvn)

    d_xbc_c, ddt, da_heads, dz, dg_ssm, dd_heads = _ssd_bwd(
        xbc_c, dtp, dt_e, dtt, a_row, a_e, a_col, hin, y_ssd, z, d_mix, wout_s, ssm_norm, d_exp, head_ind)
    d_xbc, dw_sconv, db_sconv = _ssm_conv_bwd(xbc, sconv_w, ssm_conv_b, d_xbc_c)
    d_dt_raw, d_dt_bias = _dt_bwd(dt_raw, dt_bias_p, ddt)

    dw_q, dw_kv, dw_rope, dw_z, dw_xbc, dw_dt = _in_proj_dw(hn1, [d_q_c, d_kv_c, d_kpe_raw, dz, d_xbc, d_dt_raw])
    dw_in = jnp.concatenate([dw_q, dw_kv, dw_rope[:, :QK_ROPE], dw_z, dw_xbc, dw_dt[:, :SSM_HEADS]], axis=1)
    dw_uq3 = dw_uq.reshape(Q_RANK, MLA_HEADS, QK_PAD)[:, :, :QK_NOPE + QK_ROPE]
    blocks_b = [
        dw_uq3.reshape(N_DEV, Q_RANK // N_DEV, MLA_HEADS, QK_NOPE + QK_ROPE),
        dw_ukv.reshape(N_DEV, KV_RANK // N_DEV, MLA_HEADS, QK_NOPE + V_DIM),
        col_blocks(dw_sconv[:SSM_CONV]),
        col_blocks(jnp.concatenate([dwc_g, dwc_v], axis=1)[:FFN_CONV]),
    ]
    gpack_b = _pack(blocks_b, 1, pack_rows(blocks_b, 1), bf16)
    dh0, dg_mix_pre, gparts_b, gparts_in = _in_proj_dx(
        [d_q_c, d_kv_c, d_kpe_raw, dz, d_xbc, d_dt_raw], [w_q, w_kv, w_rope, w_z, w_xbc, w_dt],
        h0, norm_mix_pre, dh1, [gpack_b, col_blocks(dw_in).astype(bf16)], True)

    grad_x = dh0[N_META:n_real][None]
    meta_blocks = col_blocks(dh0[:N_META]).reshape(N_DEV, N_META * D // N_DEV // PACK_W, PACK_W)


    def adam_group(parts, grp, name):
        rows = parts.shape[1]
        packs = [_pack([a[None] for a in grp[k]], 1, rows, f32)[0] for k in ("w", "m", "v")]
        outs = _adamw(parts, *packs, name)
        shapes = [a.shape for a in grp["w"]]
        return [dict(zip(grp["names"], [t[0] for t in _unpack(b[None], shapes, 1)])) for b in outs]

    def adam_own_layout(parts, name, w, m, v):
        return [{name: t[None]} for t in _adamw(parts, w[0], m[0], v[0], "adamw_" + name)]

    sh_a = adam_group(gparts_a, grp_a, "adamw_sharded_a")
    sh_b = adam_group(gparts_b, grp_b, "adamw_sharded_b")
    sh_in = adam_own_layout(gparts_in, "w_in", w_in, m_w_in, v_w_in)
    sh_up = adam_own_layout(gparts_up, "w_up", w_up, m_w_up, v_w_up)

    dg_alog = da_heads[:, :SSM_HEADS] * a_neg
    repl_g = [dg_mix_pre, dg_mix_post, dg_ffn_pre, dg_ffn_post, dg_q, dg_kv, dg_attn_out, db_sconv,
              d_dt_bias[:, :SSM_HEADS], dg_alog, dd_heads[:, :SSM_HEADS], dg_ssm,
              jnp.concatenate([dbc_g, dbc_v], axis=1)]
    loss_vec = loss_part[:, :1]
    small_total = _round_up(sum(-(-int(np.prod(a.shape)) // PACK_W) for a in repl_g) + 1, 16)
    spack = _pack(repl_g + [loss_vec], 0, small_total, f32)
    gparts_meta, sparts = _exchange_tail([meta_blocks], [spack], "exchange_tail")
    sh_meta = adam_group(gparts_meta, grp_meta, "adamw_meta")
    loss_row, repl_out = _adamw_replicated(sparts, repl_w, repl_m, repl_v)
    loss = loss_row[0, 0]

    order = ["meta_tokens", "norm_mix_pre", "norm_mix_post", "norm_ffn_pre", "norm_ffn_post", "w_in", "q_a_norm",
             "w_uq", "kv_a_norm", "w_ukv", "attn_out_norm", "ssm_conv_w", "ssm_conv_b", "ssm_dt_bias", "ssm_A_log",
             "ssm_D", "ssm_norm", "w_out", "w_up", "ffn_conv_w", "ffn_conv_b", "w_down"]
    rp_names = ["norm_mix_pre", "norm_mix_post", "norm_ffn_pre", "norm_ffn_post", "q_a_norm", "kv_a_norm",
                "attn_out_norm", "ssm_conv_b", "ssm_dt_bias", "ssm_A_log", "ssm_D", "ssm_norm", "ffn_conv_b"]

    def lookup(k):
        d = {**sh_a[k], **sh_b[k], **sh_in[k], **sh_up[k], **sh_meta[k],
             **{n: four[k] for n, four in zip(rp_names, repl_out)}}
        return [d[n] for n in order]

    return (loss, grad_x, *lookup(0), *lookup(1), *lookup(2), *lookup(3))
```

```python
import math

import jax
import jax.numpy as jnp
import numpy as np
from jax import lax
from jax.experimental import pallas as pl
from jax.experimental.pallas import tpu as pltpu

f32 = jnp.float32
bf16 = jnp.bfloat16

D_MODEL = 1024
N_META = 16
MLA_HEADS = 8
QK_NOPE = 128
QK_ROPE = 64
V_DIM = 128
Q_RANK = 384
KV_RANK = 256
ROPE_THETA = 10000.0
SOFTMAX_SCALE = (QK_NOPE + QK_ROPE) ** -0.5
D_ATTN = MLA_HEADS * V_DIM
SSM_HEADS = 16
SSM_P = 64
SSM_GROUPS = 2
SSM_HPG = SSM_HEADS // SSM_GROUPS
SSM_N = 128
SSM_CONV = 4
CHUNK = 128
D_SSM = SSM_HEADS * SSM_P
D_BC = SSM_GROUPS * SSM_N
D_XBC = D_SSM + 2 * D_BC
D_FF = 2816
FFN_CONV = 3
EPS = 1e-6
QK_PAD = 256
N_DEV = 8

ADAM_LR = 0.001
ADAM_B1 = 0.9
ADAM_B2 = 0.999
ADAM_EPS = 1e-08
ADAM_WD = 0.01
ADAM_STEP = 10

LANES = 128
SUBLANES = 8
ROW_TILE = 256
VMEM_LIMIT = 56 * 1024 * 1024
PACK_W = 1024
PACK_ROW_TILE = 128
NEG = -1e30
LOG2E = math.log2(math.e)
LN2 = math.log(2.0)
Q_PRESCALE = SOFTMAX_SCALE * LOG2E

_MESH = pl.DeviceIdType.MESH


def _pick(n, prefs):
    for p in prefs:
        if n % p == 0:
            return p
    return n


def _rt(m):
    return _pick(m, (384, ROW_TILE))


def _cparams(sem):
    return pltpu.CompilerParams(dimension_semantics=sem, vmem_limit_bytes=VMEM_LIMIT)


def _row(spec_cols, tm):
    return pl.BlockSpec((tm, spec_cols), lambda i: (i, 0))


def _full(shape):
    nd = len(shape)
    return pl.BlockSpec(shape, lambda *a: (0,) * nd)


def _sigmoid(x):
    return 1.0 / (1.0 + jnp.exp(-x))


def _silu(x):
    return x * _sigmoid(x)


def _dsilu(x):
    s = _sigmoid(x)
    return s * (1.0 + x * (1.0 - s))


def _dot(a, b):
    return jnp.dot(a, b, preferred_element_type=f32)


def _dot_nt(a, b):
    return lax.dot_general(a, b, (((1,), (1,)), ((), ())), preferred_element_type=f32)


def _dot_tn(a, b):
    return lax.dot_general(a, b, (((0,), (0,)), ((), ())), preferred_element_type=f32)


def _dot_hi(a, b):
    return jnp.dot(a, b, precision=lax.Precision.HIGHEST, preferred_element_type=f32)


def _mm(a, b, trans_b, name):
    M, K = a.shape
    N = b.shape[0] if trans_b else b.shape[1]
    tm = _pick(M, (768, 512, 256))
    tn = _pick(N, (1408, 512, 384, 256, 128))

    def body(a_ref, b_ref, o_ref):
        o_ref[...] = _dot_nt(a_ref[...], b_ref[...]) if trans_b else _dot(a_ref[...], b_ref[...])

    b_spec = pl.BlockSpec((tn, K), lambda j, i: (j, 0)) if trans_b else pl.BlockSpec((K, tn), lambda j, i: (0, j))
    return pl.pallas_call(
        body, name=name, grid=(N // tn, M // tm),
        in_specs=[pl.BlockSpec((tm, K), lambda j, i: (i, 0)), b_spec],
        out_specs=pl.BlockSpec((tm, tn), lambda j, i: (i, j)),
        out_shape=jax.ShapeDtypeStruct((M, N), f32),
        compiler_params=_cparams(("parallel", "parallel")),
    )(a, b)


def _mm_tn(a, g, name):
    M, K = a.shape
    N = g.shape[1]
    tm = _pick(M, (768, 512, 256))
    tk = _pick(K, (1024, 1408, 512, 384, 256))
    tn = _pick(N, (1024, 1408, 512, 384, 256, 128))

    def body(a_ref, g_ref, o_ref):
        @pl.when(pl.program_id(2) == 0)
        def _():
            o_ref[...] = jnp.zeros_like(o_ref)

        o_ref[...] += _dot_tn(a_ref[...].astype(bf16), g_ref[...].astype(bf16))

    return pl.pallas_call(
        body, name=name, grid=(K // tk, N // tn, M // tm),
        in_specs=[pl.BlockSpec((tm, tk), lambda k, j, m: (m, k)),
                  pl.BlockSpec((tm, tn), lambda k, j, m: (m, j))],
        out_specs=pl.BlockSpec((tk, tn), lambda k, j, m: (k, j)),
        out_shape=jax.ShapeDtypeStruct((K, N), f32),
        compiler_params=_cparams(("parallel", "parallel", "arbitrary")),
    )(a, g)


def _rstd(x):
    return lax.rsqrt(jnp.mean(x * x, axis=-1, keepdims=True) + EPS)


def _rms_bwd_math(x, g, dy):
    r = _rstd(x)
    xh = x * r
    dn = dy * g
    dx = r * (dn - xh * jnp.mean(dn * xh, axis=-1, keepdims=True))
    return dx, dy * xh


def _in_proj(h0, g, weights):
    M, K = h0.shape
    tm = _rt(M)
    n = len(weights)
    widths = [int(w.shape[1]) for w in weights]

    def body(x_ref, g_ref, *refs):
        xv = x_ref[...]
        hn = (xv * _rstd(xv) * g_ref[...]).astype(bf16)
        refs[n][...] = hn
        for p in range(n):
            refs[n + 1 + p][...] = _dot(hn, refs[p][...])

    return pl.pallas_call(
        body, name="in_proj", grid=(M // tm,),
        in_specs=[_row(K, tm), _full((1, K))] + [_full((K, wd)) for wd in widths],
        out_specs=[_row(K, tm)] + [_row(wd, tm) for wd in widths],
        out_shape=[jax.ShapeDtypeStruct((M, K), bf16)] + [jax.ShapeDtypeStruct((M, wd), f32) for wd in widths],
        compiler_params=_cparams(("parallel",)),
    )(h0, g, *weights)


def _in_proj_dw(hn, grads):
    M, K = hn.shape
    tm = _rt(M)
    n = len(grads)
    widths = [int(gr.shape[1]) for gr in grads]

    def body(a_ref, *refs):
        @pl.when(pl.program_id(0) == 0)
        def _():
            for p in range(n):
                refs[n + p][...] = jnp.zeros_like(refs[n + p])

        a = a_ref[...]
        for p in range(n):
            refs[n + p][...] += _dot_tn(a, refs[p][...].astype(bf16))

    return pl.pallas_call(
        body, name="in_proj_dw", grid=(M // tm,),
        in_specs=[_row(K, tm)] + [_row(wd, tm) for wd in widths],
        out_specs=[_full((K, wd)) for wd in widths],
        out_shape=[jax.ShapeDtypeStruct((K, wd), f32) for wd in widths],
        compiler_params=_cparams(("arbitrary",)),
    )(hn, *grads)


def _in_proj_dx(grads, weights, h0, g, dh1, carried, scatter):
    M, K = h0.shape
    tm = _rt(M)
    nt = M // tm
    n = len(grads)
    nx = len(carried)
    widths = [int(w.shape[1]) for w in weights]

    def body(*refs):
        g_refs, w_refs = refs[:n], refs[n:2 * n]
        x_ref, gain_ref, r_ref = refs[2 * n:2 * n + 3]
        cin = refs[2 * n + 3:2 * n + 3 + nx]
        dx_ref, dg_ref = refs[2 * n + 3 + nx:2 * n + 5 + nx]
        cout = refs[2 * n + 5 + nx:2 * n + 5 + 2 * nx]
        i = pl.program_id(0)
        _hosted_exchange(cin, cout, refs[2 * n + 5 + 2 * nx:], scatter, i == 0, i == nt - 1)

        @pl.when(i == 0)
        def _():
            dg_ref[...] = jnp.zeros_like(dg_ref)

        d_hn = None
        for p in range(n):
            t = _dot_nt(g_refs[p][...].astype(bf16), w_refs[p][...])
            d_hn = t if d_hn is None else d_hn + t
        dx, dgp = _rms_bwd_math(x_ref[...], gain_ref[...], d_hn)
        dx_ref[...] = dx + r_ref[...]
        dg_ref[...] += jnp.sum(dgp, axis=0, keepdims=True)

    any_spec = pl.BlockSpec(memory_space=pl.ANY)
    return pl.pallas_call(
        body, name="in_proj_dx", grid=(nt,),
        in_specs=([_row(wd, tm) for wd in widths] + [_full((K, wd)) for wd in widths]
                  + [_row(K, tm), _full((1, K)), _row(K, tm)] + [any_spec] * nx),
        out_specs=[_row(K, tm), _full((1, K))] + [any_spec] * nx,
        out_shape=[jax.ShapeDtypeStruct((M, K), f32), jax.ShapeDtypeStruct((1, K), f32)]
        + _exchange_shapes(carried, scatter),
        scratch_shapes=_exchange_sems(nx),
        compiler_params=_cparams(("arbitrary",)),
    )(*grads, *weights, h0, g, dh1, *carried)


def _out_proj_resid(attn, ga, ssm, wa, ws, h0, g2, g3):
    M, K = h0.shape
    tm = _rt(M)

    def body(o_ref, ga_ref, s_ref, wa_ref, ws_ref, h_ref, g2_ref, g3_ref, an_ref, m_ref, h1_ref, hn_ref):
        ov = o_ref[...]
        an = (ov * _rstd(ov) * ga_ref[...]).astype(bf16)
        an_ref[...] = an
        mv = _dot(an, wa_ref[...]) + _dot(s_ref[...], ws_ref[...])
        m_ref[...] = mv
        h1 = h_ref[...] + mv * _rstd(mv) * g2_ref[...]
        h1_ref[...] = h1
        hn_ref[...] = (h1 * _rstd(h1) * g3_ref[...]).astype(bf16)

    return pl.pallas_call(
        body, name="out_proj_resid", grid=(M // tm,),
        in_specs=[_row(attn.shape[1], tm), _full((1, attn.shape[1])), _row(ssm.shape[1], tm),
                  _full(wa.shape), _full(ws.shape), _row(K, tm), _full((1, K)), _full((1, K))],
        out_specs=[_row(attn.shape[1], tm), _row(K, tm), _row(K, tm), _row(K, tm)],
        out_shape=[jax.ShapeDtypeStruct(attn.shape, bf16), jax.ShapeDtypeStruct((M, K), f32),
                   jax.ShapeDtypeStruct((M, K), f32), jax.ShapeDtypeStruct((M, K), bf16)],
        compiler_params=_cparams(("parallel",)),
    )(attn, ga, ssm, wa, ws, h0, g2, g3)


def _final(h1, act, wdown, g4, tgt, n_real):
    M, K = h1.shape
    F = act.shape[1]
    tm = _rt(M)
    nt = M // tm

    def body(h_ref, a_ref, w_ref, g_ref, t_ref, dh_ref, dd_ref, dg_ref, ls_ref, acc_ref):
        i = pl.program_id(0)

        @pl.when(i == 0)
        def _():
            dg_ref[...] = jnp.zeros_like(dg_ref)
            acc_ref[...] = jnp.zeros_like(acc_ref)

        dv = _dot(a_ref[...], w_ref[...])
        g = g_ref[...]
        r = _rstd(dv)
        n = dv * r
        h2 = h_ref[...] + n * g
        rows = i * tm + lax.broadcasted_iota(jnp.int32, (tm, 1), 0)
        mask = ((rows >= N_META) & (rows < n_real)).astype(f32)
        diff = (h2 - t_ref[...]) * mask
        acc_ref[...] += jnp.sum(diff * diff, axis=0, keepdims=True)
        dh = diff * (1.0 / K)
        dh_ref[...] = dh
        dn = dh * g
        dd_ref[...] = (r * (dn - n * jnp.mean(dn * n, axis=-1, keepdims=True))).astype(bf16)
        dg_ref[...] += jnp.sum(dh * n, axis=0, keepdims=True)

        @pl.when(i == nt - 1)
        def _():
            ls_ref[...] = jnp.zeros((1, LANES), f32) + jnp.sum(acc_ref[...]) * (0.5 / K)

    return pl.pallas_call(
        body, name="ffn_down_loss", grid=(nt,),
        in_specs=[_row(K, tm), _row(F, tm), _full((F, K)), _full((1, K)), _row(K, tm)],
        out_specs=[_row(K, tm), _row(K, tm), _full((1, K)), _full((1, LANES))],
        out_shape=[jax.ShapeDtypeStruct((M, K), f32), jax.ShapeDtypeStruct((M, K), bf16),
                   jax.ShapeDtypeStruct((1, K), f32), jax.ShapeDtypeStruct((1, LANES), f32)],
        scratch_shapes=[pltpu.VMEM((1, K), f32)],
        compiler_params=_cparams(("arbitrary",)),
    )(h1, act, wdown, g4, tgt)


def _mid_bwd(h1, g3, dup_g, dup_v, wup, dh2, mix, g2):
    M, K = h1.shape
    F = dup_g.shape[1]
    tm = ROW_TILE

    def body(h_ref, g3_ref, ag_ref, av_ref, w_ref, dh2_ref, m_ref, g2_ref, dh1_ref, dm_ref, dg3_ref, dg2_ref):
        @pl.when(pl.program_id(0) == 0)
        def _():
            dg3_ref[...] = jnp.zeros_like(dg3_ref)
            dg2_ref[...] = jnp.zeros_like(dg2_ref)

        d_hn2 = _dot_nt(ag_ref[...], w_ref[:, 0:F]) + _dot_nt(av_ref[...], w_ref[:, F:2 * F])
        dx, dgp = _rms_bwd_math(h_ref[...], g3_ref[...], d_hn2)
        dh1 = dh2_ref[...] + dx
        dh1_ref[...] = dh1
        dg3_ref[...] += jnp.sum(dgp, axis=0, keepdims=True)
        dm, dgp2 = _rms_bwd_math(m_ref[...], g2_ref[...], dh1)
        dm_ref[...] = dm.astype(bf16)
        dg2_ref[...] += jnp.sum(dgp2, axis=0, keepdims=True)

    return pl.pallas_call(
        body, name="ffn_up_dx_mid_bwd", grid=(M // tm,),
        in_specs=[_row(K, tm), _full((1, K)), _row(F, tm), _row(F, tm), _full((K, 2 * F)), _row(K, tm),
                  _row(K, tm), _full((1, K))],
        out_specs=[_row(K, tm), _row(K, tm), _full((1, K)), _full((1, K))],
        out_shape=[jax.ShapeDtypeStruct((M, K), f32), jax.ShapeDtypeStruct((M, K), bf16),
                   jax.ShapeDtypeStruct((1, K), f32), jax.ShapeDtypeStruct((1, K), f32)],
        compiler_params=_cparams(("arbitrary",)),
    )(h1, g3, dup_g, dup_v, wup, dh2, mix, g2)


HEADS_PER_STEP = 4
CONV_RB = 16


def _conv_block_taps(x_ref, halo, rb, lanes, kw):
    r0 = rb * CONV_RB
    if rb == 0:
        cat = jnp.concatenate([halo, x_ref[0:CONV_RB, lanes]], axis=0)
        first = SUBLANES - (kw - 1)
        return [cat[first + k:first + k + CONV_RB] for k in range(kw)]
    return [x_ref[r0 - (kw - 1) + k:r0 - (kw - 1) + k + CONV_RB, lanes] for k in range(kw)]


def _conv_weighted(taps, w, kw):
    u = None
    for k in range(kw):
        t = taps[k] * w[k:k + 1, :]
        u = t if u is None else u + t
    return u


def _conv_block_dx(du, nxt, w, kw):
    cat = jnp.concatenate([du, nxt], axis=0)
    return _conv_weighted([cat[kw - 1 - k:kw - 1 - k + CONV_RB] for k in range(kw)], w, kw)


def _prev_spec(tm, tc, col_of, row_axis, reversed_tiles=0):
    def imap(*ids):
        i = ids[row_axis]
        if reversed_tiles:
            i = reversed_tiles - 1 - i
        return (jnp.maximum(i * (tm // SUBLANES) - 1, 0), col_of(*ids))
    return pl.BlockSpec((SUBLANES, tc), imap)


def _ssm_conv_fwd(xbc, w, b):
    M, C = xbc.shape
    tm, tc, kw = ROW_TILE, C, SSM_CONV

    def body(x_ref, h_ref, w_ref, b_ref, o_ref):
        i = pl.program_id(0)

        def chunk(j, carry):
            lanes = pl.ds(pl.multiple_of(j * LANES, LANES), LANES)
            halo = jnp.where(i == 0, 0.0, h_ref[:, lanes])
            wv = w_ref[:, lanes]
            bv = b_ref[:, lanes]
            for rb in range(tm // CONV_RB):
                u = _conv_weighted(_conv_block_taps(x_ref, halo, rb, lanes, kw), wv, kw) + bv
                o_ref[rb * CONV_RB:(rb + 1) * CONV_RB, lanes] = _silu(u)
            return carry

        lax.fori_loop(0, tc // LANES, chunk, 0)

    return pl.pallas_call(
        body, name="ssm_conv_fwd", grid=(M // tm, C // tc),
        in_specs=[pl.BlockSpec((tm, tc), lambda i, j: (i, j)),
                  _prev_spec(tm, tc, lambda i, j: j, 0),
                  pl.BlockSpec((SUBLANES, tc), lambda i, j: (0, j)),
                  pl.BlockSpec((1, tc), lambda i, j: (0, j))],
        out_specs=pl.BlockSpec((tm, tc), lambda i, j: (i, j)),
        out_shape=jax.ShapeDtypeStruct((M, C), f32),
        compiler_params=_cparams(("parallel", "parallel")),
    )(xbc, xbc, w, b)


def _ssm_conv_bwd(xbc, w, b, dout):
    M, C = xbc.shape
    tm, tc, kw = ROW_TILE, C // 3, SSM_CONV
    nt = M // tm

    def body(x_ref, h_ref, w_ref, b_ref, d_ref, dx_ref, dw_ref, db_ref, nxt_ref):
        i = pl.program_id(1)

        @pl.when(i == 0)
        def _():
            dw_ref[...] = jnp.zeros_like(dw_ref)
            db_ref[...] = jnp.zeros_like(db_ref)
            nxt_ref[...] = jnp.zeros_like(nxt_ref)

        def chunk(j, carry):
            lanes = pl.ds(pl.multiple_of(j * LANES, LANES), LANES)
            halo = jnp.where(i == nt - 1, 0.0, h_ref[:, lanes])
            wv = w_ref[:, lanes]
            bv = b_ref[:, lanes]
            nxt = nxt_ref[:, lanes]
            db = jnp.zeros((CONV_RB, LANES), f32)
            dw = [jnp.zeros((CONV_RB, LANES), f32) for _ in range(kw)]
            for rb in reversed(range(tm // CONV_RB)):
                rows = slice(rb * CONV_RB, (rb + 1) * CONV_RB)
                taps = _conv_block_taps(x_ref, halo, rb, lanes, kw)
                du = d_ref[rows, lanes] * _dsilu(_conv_weighted(taps, wv, kw) + bv)
                db = db + du
                dw = [dw[k] + du * taps[k] for k in range(kw)]
                dx_ref[rows, lanes] = _conv_block_dx(du, nxt, wv, kw).astype(bf16)
                nxt = du[0:SUBLANES]
            nxt_ref[:, lanes] = nxt
            db_ref[:, lanes] += jnp.sum(db, axis=0, keepdims=True)
            for k in range(kw):
                dw_ref[k:k + 1, lanes] += jnp.sum(dw[k], axis=0, keepdims=True)
            return carry

        lax.fori_loop(0, tc // LANES, chunk, 0)

    tile = pl.BlockSpec((tm, tc), lambda j, i: (nt - 1 - i, j))
    return pl.pallas_call(
        body, name="ssm_conv_bwd", grid=(C // tc, nt),
        in_specs=[tile, _prev_spec(tm, tc, lambda j, i: j, 1, nt),
                  pl.BlockSpec((SUBLANES, tc), lambda j, i: (0, j)),
                  pl.BlockSpec((1, tc), lambda j, i: (0, j)), tile],
        out_specs=[tile, pl.BlockSpec((SUBLANES, tc), lambda j, i: (0, j)),
                   pl.BlockSpec((1, tc), lambda j, i: (0, j))],
        out_shape=[jax.ShapeDtypeStruct((M, C), bf16), jax.ShapeDtypeStruct((SUBLANES, C), f32),
                   jax.ShapeDtypeStruct((1, C), f32)],
        scratch_shapes=[pltpu.VMEM((SUBLANES, tc), f32)],
        compiler_params=_cparams(("parallel", "arbitrary")),
    )(xbc, xbc, w, b, dout)


def _ffn_gate_fwd(up, w, b):
    M = up.shape[0]
    tm, tc, kw = ROW_TILE, D_FF // 2, FFN_CONV
    nc = D_FF // tc

    def body(xg_ref, hg_ref, xv_ref, hv_ref, wg_ref, wv_ref, bg_ref, bv_ref, o_ref):
        i = pl.program_id(0)

        def chunk(j, carry):
            lanes = pl.ds(pl.multiple_of(j * LANES, LANES), LANES)
            halo_g = jnp.where(i == 0, 0.0, hg_ref[:, lanes])
            halo_v = jnp.where(i == 0, 0.0, hv_ref[:, lanes])
            wg, wv = wg_ref[:, lanes], wv_ref[:, lanes]
            bg, bv = bg_ref[:, lanes], bv_ref[:, lanes]
            for rb in range(tm // CONV_RB):
                ug = _conv_weighted(_conv_block_taps(xg_ref, halo_g, rb, lanes, kw), wg, kw) + bg
                uv = _conv_weighted(_conv_block_taps(xv_ref, halo_v, rb, lanes, kw), wv, kw) + bv
                o_ref[rb * CONV_RB:(rb + 1) * CONV_RB, lanes] = (_silu(ug) * uv).astype(bf16)
            return carry

        lax.fori_loop(0, tc // LANES, chunk, 0)

    return pl.pallas_call(
        body, name="ffn_gate_fwd", grid=(M // tm, nc),
        in_specs=[pl.BlockSpec((tm, tc), lambda i, j: (i, j)),
                  _prev_spec(tm, tc, lambda i, j: j, 0),
                  pl.BlockSpec((tm, tc), lambda i, j: (i, j + nc)),
                  _prev_spec(tm, tc, lambda i, j: j + nc, 0),
                  pl.BlockSpec((SUBLANES, tc), lambda i, j: (0, j)),
                  pl.BlockSpec((SUBLANES, tc), lambda i, j: (0, j + nc)),
                  pl.BlockSpec((1, tc), lambda i, j: (0, j)),
                  pl.BlockSpec((1, tc), lambda i, j: (0, j + nc))],
        out_specs=pl.BlockSpec((tm, tc), lambda i, j: (i, j)),
        out_shape=jax.ShapeDtypeStruct((M, D_FF), bf16),
        compiler_params=_cparams(("parallel", "parallel")),
    )(up, up, up, up, w, w, b, b)


def _ffn_gate_bwd(up, w, b, d_down, wdown):
    M = up.shape[0]
    K = d_down.shape[1]
    tm, tc, kw = ROW_TILE, D_FF // 2, FFN_CONV
    nc = D_FF // tc
    nt = M // tm

    def body(xg_ref, hg_ref, xv_ref, hv_ref, wg_ref, wv_ref, bg_ref, bv_ref, dd_ref, wd_ref,
             dxg_ref, dxv_ref, dwg_ref, dwv_ref, dbg_ref, dbv_ref, ng_ref, nv_ref, d_ref):
        i = pl.program_id(1)

        @pl.when(i == 0)
        def _():
            for r in (dwg_ref, dwv_ref, dbg_ref, dbv_ref, ng_ref, nv_ref):
                r[...] = jnp.zeros_like(r)

        d_ref[...] = _dot_nt(dd_ref[...], wd_ref[...])

        def chunk(j, carry):
            lanes = pl.ds(pl.multiple_of(j * LANES, LANES), LANES)
            halo_g = jnp.where(i == nt - 1, 0.0, hg_ref[:, lanes])
            halo_v = jnp.where(i == nt - 1, 0.0, hv_ref[:, lanes])
            wg, wv = wg_ref[:, lanes], wv_ref[:, lanes]
            bg, bv = bg_ref[:, lanes], bv_ref[:, lanes]
            nxt_g, nxt_v = ng_ref[:, lanes], nv_ref[:, lanes]
            zero = jnp.zeros((CONV_RB, LANES), f32)
            dbg, dbv = zero, zero
            dwg = [zero for _ in range(kw)]
            dwv = [zero for _ in range(kw)]
            for rb in reversed(range(tm // CONV_RB)):
                rows = slice(rb * CONV_RB, (rb + 1) * CONV_RB)
                tg = _conv_block_taps(xg_ref, halo_g, rb, lanes, kw)
                tv = _conv_block_taps(xv_ref, halo_v, rb, lanes, kw)
                ug = _conv_weighted(tg, wg, kw) + bg
                uv = _conv_weighted(tv, wv, kw) + bv
                sg = _sigmoid(ug)
                da = d_ref[rows, lanes]
                dug = da * uv * (sg * (1.0 + ug * (1.0 - sg)))
                duv = da * (ug * sg)
                dbg = dbg + dug
                dbv = dbv + duv
                dwg = [dwg[k] + dug * tg[k] for k in range(kw)]
                dwv = [dwv[k] + duv * tv[k] for k in range(kw)]
                dxg_ref[rows, lanes] = _conv_block_dx(dug, nxt_g, wg, kw).astype(bf16)
                dxv_ref[rows, lanes] = _conv_block_dx(duv, nxt_v, wv, kw).astype(bf16)
                nxt_g, nxt_v = dug[0:SUBLANES], duv[0:SUBLANES]
            ng_ref[:, lanes] = nxt_g
            nv_ref[:, lanes] = nxt_v
            dbg_ref[:, lanes] += jnp.sum(dbg, axis=0, keepdims=True)
            dbv_ref[:, lanes] += jnp.sum(dbv, axis=0, keepdims=True)
            for k in range(kw):
                dwg_ref[k:k + 1, lanes] += jnp.sum(dwg[k], axis=0, keepdims=True)
                dwv_ref[k:k + 1, lanes] += jnp.sum(dwv[k], axis=0, keepdims=True)
            return carry

        lax.fori_loop(0, tc // LANES, chunk, 0)

    tile_g = pl.BlockSpec((tm, tc), lambda j, i: (nt - 1 - i, j))
    tile_v = pl.BlockSpec((tm, tc), lambda j, i: (nt - 1 - i, j + nc))
    ext = pltpu.VMEM((SUBLANES, tc), f32)
    return pl.pallas_call(
        body, name="ffn_gate_bwd", grid=(nc, nt),
        in_specs=[tile_g, _prev_spec(tm, tc, lambda j, i: j, 1, nt),
                  tile_v, _prev_spec(tm, tc, lambda j, i: j + nc, 1, nt),
                  pl.BlockSpec((SUBLANES, tc), lambda j, i: (0, j)),
                  pl.BlockSpec((SUBLANES, tc), lambda j, i: (0, j + nc)),
                  pl.BlockSpec((1, tc), lambda j, i: (0, j)),
                  pl.BlockSpec((1, tc), lambda j, i: (0, j + nc)),
                  pl.BlockSpec((tm, K), lambda j, i: (nt - 1 - i, 0)),
                  pl.BlockSpec((tc, K), lambda j, i: (j, 0))],
        out_specs=[tile_g, tile_g,
                   pl.BlockSpec((SUBLANES, tc), lambda j, i: (0, j)),
                   pl.BlockSpec((SUBLANES, tc), lambda j, i: (0, j)),
                   pl.BlockSpec((1, tc), lambda j, i: (0, j)),
                   pl.BlockSpec((1, tc), lambda j, i: (0, j))],
        out_shape=[jax.ShapeDtypeStruct((M, D_FF), bf16), jax.ShapeDtypeStruct((M, D_FF), bf16),
                   jax.ShapeDtypeStruct((SUBLANES, D_FF), f32), jax.ShapeDtypeStruct((SUBLANES, D_FF), f32),
                   jax.ShapeDtypeStruct((1, D_FF), f32), jax.ShapeDtypeStruct((1, D_FF), f32)],
        scratch_shapes=[ext, ext, pltpu.VMEM((tm, tc), f32)],
        compiler_params=_cparams(("parallel", "arbitrary")),
    )(up, up, up, up, w, w, b, b, d_down, wdown)


def _rope_apply(blk, cos, sin):
    lane = lax.broadcasted_iota(jnp.int32, blk.shape, 1)
    half = QK_ROPE // 2
    partner = jnp.where(lane < half, pltpu.roll(blk, LANES - half, 1), pltpu.roll(blk, half, 1))
    return blk * cos + partner * sin


def _rope_unapply(d, cos, sin):
    t = d * sin
    lane = lax.broadcasted_iota(jnp.int32, d.shape, 1)
    half = QK_ROPE // 2
    partner = jnp.where(lane < half, pltpu.roll(t, LANES - half, 1), pltpu.roll(t, half, 1))
    return d * cos + partner


def _up_q_rope(q_c, g, wuq, cos, sin):
    M, K = q_c.shape
    tm = _pick(M, (768, 512, 256))

    hs = HEADS_PER_STEP

    def body(x_ref, g_ref, b_ref, c_ref, s_ref, a_ref, o_ref):
        xv = x_ref[...]
        a = (xv * _rstd(xv) * g_ref[...]).astype(bf16)
        a_ref[...] = a
        r = _dot(a, b_ref[...]) * Q_PRESCALE
        c, s = c_ref[...], s_ref[...]
        for u in range(hs):
            o_ref[u, :, 0:QK_NOPE] = r[:, u * QK_PAD:u * QK_PAD + QK_NOPE].astype(bf16)
            o_ref[u, :, QK_NOPE:QK_PAD] = _rope_apply(r[:, u * QK_PAD + QK_NOPE:(u + 1) * QK_PAD], c, s).astype(bf16)

    return pl.pallas_call(
        body, name="up_q_rope", grid=(M // tm, MLA_HEADS // hs),
        in_specs=[pl.BlockSpec((tm, K), lambda i, h: (i, 0)),
                  pl.BlockSpec((1, K), lambda i, h: (0, 0)),
                  pl.BlockSpec((K, hs * QK_PAD), lambda i, h: (0, h)),
                  pl.BlockSpec((tm, LANES), lambda i, h: (i, 0)),
                  pl.BlockSpec((tm, LANES), lambda i, h: (i, 0))],
        out_specs=[pl.BlockSpec((tm, K), lambda i, h: (i, 0)),
                   pl.BlockSpec((hs, tm, QK_PAD), lambda i, h: (h, i, 0))],
        out_shape=[jax.ShapeDtypeStruct((M, K), bf16), jax.ShapeDtypeStruct((MLA_HEADS, M, QK_PAD), bf16)],
        compiler_params=_cparams(("parallel", "arbitrary")),
    )(q_c, g, wuq, cos, sin)


def _up_kv_rope(kv_c, g, wukv, kpe_raw, cos, sin):
    M, K = kv_c.shape
    tm = _pick(M, (768, 512, 256))

    hs = HEADS_PER_STEP
    w = QK_NOPE + V_DIM

    def body(x_ref, g_ref, b_ref, pe_ref, c_ref, s_ref, a_ref, k_ref, v_ref):
        xv = x_ref[...]
        a = (xv * _rstd(xv) * g_ref[...]).astype(bf16)
        a_ref[...] = a
        r = _dot(a, b_ref[...])
        pe = _rope_apply(pe_ref[...], c_ref[...], s_ref[...]).astype(bf16)
        for u in range(hs):
            k_ref[u, :, 0:QK_NOPE] = r[:, u * w:u * w + QK_NOPE].astype(bf16)
            k_ref[u, :, QK_NOPE:QK_PAD] = pe
            v_ref[u] = r[:, u * w + QK_NOPE:(u + 1) * w].astype(bf16)

    return pl.pallas_call(
        body, name="up_kv_rope", grid=(M // tm, MLA_HEADS // hs),
        in_specs=[pl.BlockSpec((tm, K), lambda i, h: (i, 0)),
                  pl.BlockSpec((1, K), lambda i, h: (0, 0)),
                  pl.BlockSpec((K, hs * w), lambda i, h: (0, h)),
                  pl.BlockSpec((tm, LANES), lambda i, h: (i, 0)),
                  pl.BlockSpec((tm, LANES), lambda i, h: (i, 0)),
                  pl.BlockSpec((tm, LANES), lambda i, h: (i, 0))],
        out_specs=[pl.BlockSpec((tm, K), lambda i, h: (i, 0)),
                   pl.BlockSpec((hs, tm, QK_PAD), lambda i, h: (h, i, 0)),
                   pl.BlockSpec((hs, tm, V_DIM), lambda i, h: (h, i, 0))],
        out_shape=[jax.ShapeDtypeStruct((M, K), bf16), jax.ShapeDtypeStruct((MLA_HEADS, M, QK_PAD), bf16),
                   jax.ShapeDtypeStruct((MLA_HEADS, M, V_DIM), bf16)],
        compiler_params=_cparams(("parallel", "arbitrary")),
    )(kv_c, g, wukv, kpe_raw, cos, sin)


def _latent_bwd(d_full_sc, w_ref, x_ref, g_ref, a_ref, dx_ref, dg_ref, dw_ref):
    d_full = d_full_sc[...]
    dx, dgp = _rms_bwd_math(x_ref[...], g_ref[...], _dot_nt(d_full, w_ref[...]))
    dx_ref[...] = dx.astype(bf16)
    dg_ref[...] += jnp.sum(dgp, axis=0, keepdims=True)
    dw_ref[...] += _dot_tn(a_ref[...], d_full)


def _latent_bwd_call(body, name, head_inputs, head_specs, cos, sin, w, x, g, a, extra_out_specs, extra_out_shape):
    M, K = x.shape
    tm = _rt(M)
    N = w.shape[1]
    return pl.pallas_call(
        body, name=name, grid=(M // tm,),
        in_specs=head_specs + [_row(LANES, tm), _row(LANES, tm), _full((K, N)), _row(K, tm), _full((1, K)),
                               _row(K, tm)],
        out_specs=[_row(K, tm), _full((1, K)), _full((K, N))] + extra_out_specs,
        out_shape=[jax.ShapeDtypeStruct((M, K), bf16), jax.ShapeDtypeStruct((1, K), f32),
                   jax.ShapeDtypeStruct((K, N), f32)] + extra_out_shape,
        scratch_shapes=[pltpu.VMEM((tm, N), bf16)],
        compiler_params=_cparams(("arbitrary",)),
    )(*head_inputs, cos, sin, w, x, g, a)


def _q_branch_bwd(dq, cos, sin, wuq, q_c, g, qn):
    tm = _rt(q_c.shape[0])

    def body(d_ref, c_ref, s_ref, w_ref, x_ref, g_ref, a_ref, dx_ref, dg_ref, dw_ref, full_sc):
        @pl.when(pl.program_id(0) == 0)
        def _():
            dg_ref[...] = jnp.zeros_like(dg_ref)
            dw_ref[...] = jnp.zeros_like(dw_ref)

        c, s = c_ref[...], s_ref[...]
        for h in range(MLA_HEADS):
            full_sc[:, h * QK_PAD:h * QK_PAD + QK_NOPE] = (d_ref[h, :, 0:QK_NOPE] * SOFTMAX_SCALE).astype(bf16)
            full_sc[:, h * QK_PAD + QK_NOPE:(h + 1) * QK_PAD] = (_rope_unapply(
                d_ref[h, :, QK_NOPE:QK_PAD], c, s) * SOFTMAX_SCALE).astype(bf16)
        _latent_bwd(full_sc, w_ref, x_ref, g_ref, a_ref, dx_ref, dg_ref, dw_ref)

    return _latent_bwd_call(body, "q_branch_bwd", [dq],
                            [pl.BlockSpec((MLA_HEADS, tm, QK_PAD), lambda i: (0, i, 0))],
                            cos, sin, wuq, q_c, g, qn, [], [])


def _kv_branch_bwd(dk, dv, cos, sin, wukv, kv_c, g, kvn):
    M = kv_c.shape[0]
    tm = _rt(M)
    w = QK_NOPE + V_DIM

    def body(dk_ref, dv_ref, c_ref, s_ref, w_ref, x_ref, g_ref, a_ref, dx_ref, dg_ref, dw_ref, pe_ref, full_sc):
        @pl.when(pl.program_id(0) == 0)
        def _():
            dg_ref[...] = jnp.zeros_like(dg_ref)
            dw_ref[...] = jnp.zeros_like(dw_ref)

        pe = None
        for h in range(MLA_HEADS):
            full_sc[:, h * w:h * w + QK_NOPE] = dk_ref[h, :, 0:QK_NOPE].astype(bf16)
            full_sc[:, h * w + QK_NOPE:(h + 1) * w] = dv_ref[h].astype(bf16)
            t = dk_ref[h, :, QK_NOPE:QK_PAD]
            pe = t if pe is None else pe + t
        pe_ref[...] = _rope_unapply(pe, c_ref[...], s_ref[...])
        _latent_bwd(full_sc, w_ref, x_ref, g_ref, a_ref, dx_ref, dg_ref, dw_ref)

    return _latent_bwd_call(body, "kv_branch_bwd", [dk, dv],
                            [pl.BlockSpec((MLA_HEADS, tm, QK_PAD), lambda i: (0, i, 0)),
                             pl.BlockSpec((MLA_HEADS, tm, V_DIM), lambda i: (0, i, 0))],
                            cos, sin, wukv, kv_c, g, kvn, [_row(LANES, tm)],
                            [jax.ShapeDtypeStruct((M, LANES), f32)])


def _attn_tile(M):
    return 768 if (M % 768 == 0 and M >= 4 * 768) else ROW_TILE


def _col_to_row(col):
    return col.T[0:1, :]


def _hosted_exchange(refs_in, refs_out, sems, scatter, first, last):
    copies = _exchange_copies(refs_in, refs_out, *sems, scatter)

    @pl.when(first)
    def _():
        for cp in copies:
            cp.start()

    @pl.when(last)
    def _():
        for cp in copies:
            cp.wait()


def _flash_fwd(q, k, v, carried, scatter):
    H, M, _ = q.shape
    T = _attn_tile(M)
    nq = M // T
    nx = len(carried)

    def body(*refs):
        q_ref, k_ref, v_ref = refs[:3]
        o_ref, lse_ref = refs[3 + nx:5 + nx]
        sa_ref, sb_ref, m_sc, l_sc, acc_sc = refs[5 + 2 * nx:10 + 2 * nx]
        h = pl.program_id(0)
        i = pl.program_id(1)
        _hosted_exchange(refs[3:3 + nx], refs[5 + nx:5 + 2 * nx], refs[10 + 2 * nx:], scatter,
                         (h == 0) & (i == 0), (h == H - 1) & (i == nq - 1))
        qv = q_ref[0]
        m_sc[...] = jnp.full_like(m_sc, NEG)
        l_sc[...] = jnp.zeros_like(l_sc)
        acc_sc[...] = jnp.zeros_like(acc_sc)

        def scores(j, s_ref):
            off = pl.multiple_of(j * T, T)
            s_ref[...] = _dot_nt(qv, k_ref[0, pl.ds(off, T), :])

        def softmax_pv(j, s_ref, masked):
            off = pl.multiple_of(j * T, T)
            s = s_ref[...]
            if masked:
                r = lax.broadcasted_iota(jnp.int32, (T, T), 0)
                c = lax.broadcasted_iota(jnp.int32, (T, T), 1)
                s = jnp.where(r >= c, s, NEG)
            m_prev = m_sc[...]
            m_new = jnp.maximum(m_prev, jnp.max(s, axis=1, keepdims=True))
            alpha = jnp.exp2(m_prev - m_new)
            p = jnp.exp2(s - m_new[:, 0:1])
            l_sc[...] = alpha * l_sc[...] + jnp.sum(p, axis=1, keepdims=True)
            acc_sc[...] = alpha * acc_sc[...] + _dot(p.astype(bf16), v_ref[0, pl.ds(off, T), :])
            m_sc[...] = m_new

        scores(0, sa_ref)

        def pair(jj, c):
            j0 = 2 * jj
            scores(j0 + 1, sb_ref)
            softmax_pv(j0, sa_ref, False)
            scores(j0 + 2, sa_ref)
            softmax_pv(j0 + 1, sb_ref, False)
            return c

        lax.fori_loop(0, i // 2, pair, 0)

        @pl.when(i % 2 == 0)
        def _():
            softmax_pv(i, sa_ref, True)

        @pl.when(i % 2 == 1)
        def _():
            scores(i, sb_ref)
            softmax_pv(i - 1, sa_ref, False)
            softmax_pv(i, sb_ref, True)

        l = l_sc[...]
        o_ref[...] = acc_sc[...] / l
        lse_ref[0, 0] = _col_to_row(m_sc[...] + jnp.log2(l))

    any_spec = pl.BlockSpec(memory_space=pl.ANY)
    return pl.pallas_call(
        body, name="flash_fwd", grid=(H, nq),
        in_specs=[pl.BlockSpec((1, T, QK_PAD), lambda h, i: (h, i, 0)),
                  pl.BlockSpec((1, M, QK_PAD), lambda h, i: (h, 0, 0)),
                  pl.BlockSpec((1, M, V_DIM), lambda h, i: (h, 0, 0))] + [any_spec] * nx,
        out_specs=[pl.BlockSpec((T, V_DIM), lambda h, i: (i, h)),
                   pl.BlockSpec((1, 1, 1, T), lambda h, i: (h, i, 0, 0))] + [any_spec] * nx,
        out_shape=[jax.ShapeDtypeStruct((M, H * V_DIM), f32),
                   jax.ShapeDtypeStruct((H, nq, 1, T), f32)] + _exchange_shapes(carried, scatter),
        scratch_shapes=[pltpu.VMEM((T, T), f32), pltpu.VMEM((T, T), f32),
                        pltpu.VMEM((T, LANES), f32), pltpu.VMEM((T, LANES), f32),
                        pltpu.VMEM((T, V_DIM), f32)] + _exchange_sems(nx),
        compiler_params=_cparams(("arbitrary", "arbitrary")),
    )(q, k, v, *carried)


def _attn_out_bwd(o, g, d_mix, wa):
    M, K = o.shape
    H = MLA_HEADS
    T = _attn_tile(M)

    def body(o_ref, g_ref, dm_ref, w_ref, dh_ref, dl_ref, dg_ref):
        @pl.when(pl.program_id(0) == 0)
        def _():
            dg_ref[...] = jnp.zeros_like(dg_ref)

        ov = o_ref[...]
        do, dgp = _rms_bwd_math(ov, g_ref[...], _dot_nt(dm_ref[...], w_ref[...]))
        dg_ref[...] += jnp.sum(dgp, axis=0, keepdims=True)
        for h in range(H):
            sl = slice(h * V_DIM, (h + 1) * V_DIM)
            doh = do[:, sl]
            dh_ref[h] = doh.astype(bf16)
            col = jnp.sum(ov[:, sl] * doh, axis=1, keepdims=True) + jnp.zeros((T, LANES), f32)
            dl_ref[h, 0] = _col_to_row(col)

    return pl.pallas_call(
        body, name="attn_out_bwd", grid=(M // T,),
        in_specs=[_row(K, T), _full((1, K)), _row(d_mix.shape[1], T), _full(wa.shape)],
        out_specs=[pl.BlockSpec((H, T, V_DIM), lambda i: (0, i, 0)),
                   pl.BlockSpec((H, 1, 1, T), lambda i: (0, i, 0, 0)),
                   _full((1, K))],
        out_shape=[jax.ShapeDtypeStruct((H, M, V_DIM), bf16),
                   jax.ShapeDtypeStruct((H, M // T, 1, T), f32),
                   jax.ShapeDtypeStruct((1, K), f32)],
        compiler_params=_cparams(("arbitrary",)),
    )(o, g, d_mix, wa)


def _flash_bwd(q, k, v, do, lse, delta, carried, scatter):
    H, M, _ = q.shape
    T = _attn_tile(M)
    nq = M // T
    nx = len(carried)

    def body(*refs):
        q_ref, do_ref, lse_ref, dl_ref, k_ref, v_ref = refs[:6]
        dq_ref, dk_ref, dv_ref = refs[6 + nx:9 + nx]
        dk_sc, dv_sc = refs[9 + 2 * nx:11 + 2 * nx]
        j = pl.program_id(1)
        _hosted_exchange(refs[6:6 + nx], refs[9 + nx:9 + 2 * nx], refs[11 + 2 * nx:], scatter,
                         (pl.program_id(0) == 0) & (j == 0), (pl.program_id(0) == H - 1) & (j == nq - 1))

        @pl.when(j == 0)
        def _():
            dq_ref[...] = jnp.zeros_like(dq_ref)

        kt = k_ref[0]
        vt = v_ref[0]
        dk_sc[...] = jnp.zeros_like(dk_sc)
        dv_sc[...] = jnp.zeros_like(dv_sc)

        def step(i, masked):
            off = pl.multiple_of(i * T, T)
            qt = q_ref[0, pl.ds(off, T), :]
            dot_ = do_ref[0, pl.ds(off, T), :]
            st = _dot_nt(kt, qt)
            if masked:
                r = lax.broadcasted_iota(jnp.int32, (T, T), 0)
                c = lax.broadcasted_iota(jnp.int32, (T, T), 1)
                st = jnp.where(c >= r, st, NEG)
            pt = jnp.exp2(st - lse_ref[0, i])
            dv_sc[...] += _dot(pt.astype(bf16), dot_)
            dpt = _dot_nt(vt, dot_)
            dst = (pt * (dpt - dl_ref[0, i])).astype(bf16)
            dk_sc[...] += _dot(dst, qt)
            dq_ref[0, pl.ds(off, T), :] += _dot_tn(dst, kt)

        step(j, True)

        def loop_body(i, c):
            step(i, False)
            return c

        lax.fori_loop(j + 1, nq, loop_body, 0)
        dk_ref[0] = dk_sc[...] * LN2
        dv_ref[0] = dv_sc[...]

    any_spec = pl.BlockSpec(memory_space=pl.ANY)
    return pl.pallas_call(
        body, name="flash_bwd", grid=(H, nq),
        in_specs=[pl.BlockSpec((1, M, QK_PAD), lambda h, j: (h, 0, 0)),
                  pl.BlockSpec((1, M, V_DIM), lambda h, j: (h, 0, 0)),
                  pl.BlockSpec((1, nq, 1, T), lambda h, j: (h, 0, 0, 0)),
                  pl.BlockSpec((1, nq, 1, T), lambda h, j: (h, 0, 0, 0)),
                  pl.BlockSpec((1, T, QK_PAD), lambda h, j: (h, j, 0)),
                  pl.BlockSpec((1, T, V_DIM), lambda h, j: (h, j, 0))] + [any_spec] * nx,
        out_specs=[pl.BlockSpec((1, M, QK_PAD), lambda h, j: (h, 0, 0)),
                   pl.BlockSpec((1, T, QK_PAD), lambda h, j: (h, j, 0)),
                   pl.BlockSpec((1, T, V_DIM), lambda h, j: (h, j, 0))] + [any_spec] * nx,
        out_shape=[jax.ShapeDtypeStruct((H, M, QK_PAD), f32),
                   jax.ShapeDtypeStruct((H, M, QK_PAD), f32),
                   jax.ShapeDtypeStruct((H, M, V_DIM), f32)] + _exchange_shapes(carried, scatter),
        scratch_shapes=[pltpu.VMEM((T, QK_PAD), f32), pltpu.VMEM((T, V_DIM), f32)] + _exchange_sems(nx),
        compiler_params=_cparams(("arbitrary", "arbitrary")),
    )(q, do, lse, delta, k, v, *carried)


def _dt_fwd(dt_raw, bias, expand):
    M = dt_raw.shape[0]
    tm = _rt(M)

    def body(x_ref, b_ref, e_ref, o_ref, oe_ref):
        u = x_ref[...] + b_ref[...]
        sp = jnp.maximum(u, 0.0) + jnp.log(1.0 + jnp.exp(-jnp.abs(u)))
        lane = lax.broadcasted_iota(jnp.int32, u.shape, 1)
        dtp = jnp.where(lane < SSM_HEADS, sp, 0.0)
        o_ref[...] = dtp
        oe_ref[...] = _dot_hi(dtp, e_ref[...])

    return pl.pallas_call(
        body, name="dt_fwd", grid=(M // tm,),
        in_specs=[_row(LANES, tm), _full((1, LANES)), _full((LANES, D_SSM))],
        out_specs=[_row(LANES, tm), _row(D_SSM, tm)],
        out_shape=[jax.ShapeDtypeStruct((M, LANES), f32), jax.ShapeDtypeStruct((M, D_SSM), f32)],
        compiler_params=_cparams(("parallel",)),
    )(dt_raw, bias, expand)


def _dt_bwd(dt_raw, bias, ddt):
    M = dt_raw.shape[0]
    tm = _rt(M)

    def body(x_ref, b_ref, d_ref, o_ref, db_ref):
        @pl.when(pl.program_id(0) == 0)
        def _():
            db_ref[...] = jnp.zeros_like(db_ref)

        u = x_ref[...] + b_ref[...]
        lane = lax.broadcasted_iota(jnp.int32, u.shape, 1)
        g = jnp.where(lane < SSM_HEADS, d_ref[...] * _sigmoid(u), 0.0)
        o_ref[...] = g
        db_ref[...] += jnp.sum(g, axis=0, keepdims=True)

    return pl.pallas_call(
        body, name="dt_bwd", grid=(M // tm,),
        in_specs=[_row(LANES, tm), _full((1, LANES)), _row(LANES, tm)],
        out_specs=[_row(LANES, tm), _full((1, LANES))],
        out_shape=[jax.ShapeDtypeStruct((M, LANES), f32), jax.ShapeDtypeStruct((1, LANES), f32)],
        compiler_params=_cparams(("arbitrary",)),
    )(dt_raw, bias, ddt)


SSM_GW = SSM_HPG * SSM_P
SSM_PAIRS = SSM_GW // LANES


def _ssd_common(dte_ref, dtt_ref, ae_ref, acol_ref):
    Q = CHUNK
    r = lax.broadcasted_iota(jnp.int32, (Q, Q), 0)
    c = lax.broadcasted_iota(jnp.int32, (Q, Q), 1)
    causal = r >= c
    anti = c >= r
    tril = causal.astype(f32)
    triu = anti.astype(f32)
    dt_e = dte_ref[...]
    cs_e = _dot_hi(tril, dt_e * ae_ref[...])
    cst = _dot_hi(dtt_ref[...] * acol_ref[...], triu)
    cs_last = cs_e[Q - 1:Q, :]
    return causal, anti, triu, dt_e, cs_e, cst, jnp.exp(cs_e), jnp.exp(cs_last - cs_e), jnp.exp(cs_last)


def _half_masks():
    lane = lax.broadcasted_iota(jnp.int32, (CHUNK, LANES), 1)
    lo = lane < SSM_P
    return lo, jnp.logical_not(lo)


def _ssd_fwd(xbc_c, dt_e, dtt, a_e, a_col, z, d_exp, g_ssm):
    M = xbc_c.shape[0]
    Q = CHUNK
    nch = M // Q
    gw = D_SSM // SSM_GROUPS

    def body(x_ref, dte_ref, dtt_ref, ae_ref, acol_ref, z_ref, dexp_ref, gn_ref, y_ref, hin_ref, o_ref, ht_sc):
        @pl.when(pl.program_id(0) == 0)
        def _():
            ht_sc[...] = jnp.zeros_like(ht_sc)

        causal, _, _, dt_e, cs_e, cst, ecs_e, dte_e, elast_e = _ssd_common(dte_ref, dtt_ref, ae_ref, acol_ref)
        halves = _half_masks()
        for g in range(SSM_GROUPS):
            g0 = g * SSM_GW
            bg = x_ref[:, D_SSM + g * SSM_N:D_SSM + (g + 1) * SSM_N]
            cg = x_ref[:, D_SSM + D_BC + g * SSM_N:D_SSM + D_BC + (g + 1) * SSM_N]
            bg_b = bg.astype(bf16)
            cg_b = cg.astype(bf16)
            cb = _dot_nt(cg_b, bg_b)
            bgt_b = bg.T.astype(bf16)
            xdt_g = x_ref[:, g0:g0 + SSM_GW] * dt_e[:, g0:g0 + SSM_GW]
            ht = ht_sc[g]
            hin_ref[0, g] = ht
            y_off = _dot(cg_b, ht.astype(bf16)) * ecs_e[:, g0:g0 + SSM_GW]
            for pr in range(SSM_PAIRS):
                p0 = pr * LANES
                xdt_p = xdt_g[:, p0:p0 + LANES]
                acc = y_off[:, p0:p0 + LANES]
                for half in range(2):
                    h = g * SSM_HPG + pr * 2 + half
                    seg = cs_e[:, h * SSM_P:h * SSM_P + 1] - cst[h:h + 1, :]
                    lm = jnp.exp(jnp.where(causal, seg, -jnp.inf))
                    xm = jnp.where(halves[half], xdt_p, 0.0).astype(bf16)
                    acc = acc + _dot((cb * lm).astype(bf16), xm)
                y_ref[:, g0 + p0:g0 + p0 + LANES] = acc
            st = _dot(bgt_b, (xdt_g * dte_e[:, g0:g0 + SSM_GW]).astype(bf16))
            ht_sc[g] = ht * elast_e[:, g0:g0 + SSM_GW] + st
        yg = (y_ref[...] + dexp_ref[...] * x_ref[:, 0:D_SSM]) * _silu(z_ref[...])
        for gi in range(SSM_GROUPS):
            blk = yg[:, gi * gw:(gi + 1) * gw]
            o_ref[:, gi * gw:(gi + 1) * gw] = (blk * _rstd(blk) * gn_ref[:, gi * gw:(gi + 1) * gw]).astype(bf16)

    chunk_rows = pl.BlockSpec((Q, D_SSM), lambda c: (c, 0))
    return pl.pallas_call(
        body, name="ssd_fwd", grid=(nch,),
        in_specs=[pl.BlockSpec((Q, D_XBC), lambda c: (c, 0)), chunk_rows,
                  pl.BlockSpec((SSM_HEADS, Q), lambda c: (0, c)),
                  _full((1, D_SSM)), _full((SSM_HEADS, LANES)), chunk_rows, _full((1, D_SSM)), _full((1, D_SSM))],
        out_specs=[chunk_rows, pl.BlockSpec((1, SSM_GROUPS, SSM_N, SSM_GW), lambda c: (c, 0, 0, 0)), chunk_rows],
        out_shape=[jax.ShapeDtypeStruct((M, D_SSM), f32),
                   jax.ShapeDtypeStruct((nch, SSM_GROUPS, SSM_N, SSM_GW), f32),
                   jax.ShapeDtypeStruct((M, D_SSM), bf16)],
        scratch_shapes=[pltpu.VMEM((SSM_GROUPS, SSM_N, SSM_GW), f32)],
        compiler_params=_cparams(("arbitrary",)),
    )(xbc_c, dt_e, dtt, a_e, a_col, z, d_exp, g_ssm)


def _ssd_bwd(xbc_c, dtp, dt_e, dtt, a_row, a_e, a_col, hin, y, z, d_ssm, g_ssm, d_exp, head_ind):
    M = xbc_c.shape[0]
    Q = CHUNK
    nch = M // Q
    rev = lambda c: nch - 1 - c

    gw = D_SSM // SSM_GROUPS

    def body(x_ref, dtp_ref, dte_ref, dtt_ref, arow_ref, ae_ref, acol_ref, hin_ref, y_ref, zz_ref, do_ref, gn_ref,
             dexp_ref, ind_ref, dx_ref, ddt_ref, da_ref, dz_ref, dgn_ref, dd_ref,
             dht_sc, z_sc, z1_sc, last_sc, ct_sc, dy_ref, ddc_sc):
        @pl.when(pl.program_id(0) == 0)
        def _():
            dht_sc[...] = jnp.zeros_like(dht_sc)
            da_ref[...] = jnp.zeros_like(da_ref)
            last_sc[...] = jnp.zeros_like(last_sc)
            ct_sc[...] = jnp.zeros_like(ct_sc)
            dgn_ref[...] = jnp.zeros_like(dgn_ref)
            ddc_sc[...] = jnp.zeros_like(ddc_sc)

        zv = zz_ref[...]
        xv = x_ref[:, 0:D_SSM]
        sz = _silu(zv)
        yd = y_ref[...] + dexp_ref[...] * xv
        yg = yd * sz
        dov = do_ref[...]
        for gi in range(SSM_GROUPS):
            sl = slice(gi * gw, (gi + 1) * gw)
            dyg, dgp = _rms_bwd_math(yg[:, sl], gn_ref[:, sl], dov[:, sl])
            dgn_ref[:, sl] += jnp.sum(dgp, axis=0, keepdims=True)
            dyd = dyg * sz[:, sl]
            dy_ref[:, sl] = dyd
            dz_ref[:, sl] = (dyg * yd[:, sl] * _dsilu(zv[:, sl])).astype(bf16)
            ddc_sc[:, sl] += jnp.sum(dyd * xv[:, sl], axis=0, keepdims=True)

        @pl.when(pl.program_id(0) == nch - 1)
        def _():
            dd_ref[...] = _dot_hi(ddc_sc[...], ind_ref[...])

        causal, anti, triu, dt_e, cs_e, cst, ecs_e, dte_e, elast_e = _ssd_common(dte_ref, dtt_ref, ae_ref, acol_ref)
        halves = _half_masks()
        lane = lax.broadcasted_iota(jnp.int32, (Q, LANES), 1)
        rsum = jnp.zeros((Q, LANES), f32)
        for g in range(SSM_GROUPS):
            g0 = g * SSM_GW
            gs = slice(g0, g0 + SSM_GW)
            b0 = D_SSM + g * SSM_N
            c0 = D_SSM + D_BC + g * SSM_N
            bg = x_ref[:, b0:b0 + SSM_N]
            cg = x_ref[:, c0:c0 + SSM_N]
            bg_b = bg.astype(bf16)
            cg_b = cg.astype(bf16)
            cgt_b = cg.T.astype(bf16)
            cbt = _dot_nt(bg_b, cg_b)
            cb = _dot_nt(cg_b, bg_b)
            x_g = x_ref[:, gs]
            dt_g = dt_e[:, gs]
            xdt_g = x_g * dt_g
            dy_g = dy_ref[:, gs]
            ht = hin_ref[0, g]
            ht_b = ht.astype(bf16)
            dht = dht_sc[g]
            dht_b = dht.astype(bf16)
            dye_b = (dy_g * ecs_e[:, gs]).astype(bf16)
            dc = _dot_nt(dye_b, ht_b)
            dht_new = dht * elast_e[:, gs] + _dot(cgt_b, dye_b)
            e = _dot(bg_b, dht_b)
            xdtd = xdt_g * dte_e[:, gs]
            db = _dot_nt(xdtd.astype(bf16), dht_b)
            dxdt_state = e * dte_e[:, gs]
            exd = e * xdtd
            z1_sc[:, gs] = dy_g * (_dot(cg_b, ht_b) * ecs_e[:, gs]) - exd
            last_sc[0:1, gs] = (jnp.sum(exd, axis=0, keepdims=True)
                                + jnp.sum(dht * ht, axis=0, keepdims=True) * elast_e[:, gs])
            dg_acc = jnp.zeros((Q, Q), f32)
            for pr in range(SSM_PAIRS):
                p0 = pr * LANES
                ps = slice(g0 + p0, g0 + p0 + LANES)
                dy_p = dy_g[:, p0:p0 + LANES]
                xdt_pb = xdt_g[:, p0:p0 + LANES].astype(bf16)
                acc = dxdt_state[:, p0:p0 + LANES]
                for half in range(2):
                    h = g * SSM_HPG + pr * 2 + half
                    seg = cs_e[:, h * SSM_P:h * SSM_P + 1] - cst[h:h + 1, :]
                    lm = jnp.exp(jnp.where(causal, seg, -jnp.inf))
                    lmt = jnp.exp(jnp.where(anti, -seg, -jnp.inf))
                    dym = jnp.where(halves[half], dy_p, 0.0).astype(bf16)
                    acc = acc + _dot((cbt * lmt).astype(bf16), dym)
                    dml = _dot_nt(dym, xdt_pb) * lm
                    dg_acc = dg_acc + dml
                    w = dml * cb
                    rsum = rsum + jnp.where(lane == h, jnp.sum(w, axis=1, keepdims=True), 0.0)
                    ct_sc[h:h + 1, :] = jnp.sum(w, axis=0, keepdims=True)
                dx_ref[:, ps] = acc * dt_g[:, p0:p0 + LANES] + dexp_ref[:, ps] * dy_p
                z_sc[:, ps] = acc * x_g[:, p0:p0 + LANES]
            dg_b = dg_acc.astype(bf16)
            dx_ref[:, c0:c0 + SSM_N] = dc + _dot(dg_b, bg_b)
            dx_ref[:, b0:b0 + SSM_N] = db + _dot_tn(dg_b, cg_b)
            dht_sc[g] = dht_new
        s1 = _dot_hi(z1_sc[...], ind_ref[...])
        s2 = _dot_hi(z_sc[...], ind_ref[...])
        last = _dot_hi(last_sc[...], ind_ref[...])[0:1, :]
        dtp = dtp_ref[...]
        row = lax.broadcasted_iota(jnp.int32, (Q, LANES), 0)
        dcs = s1 + rsum + jnp.where(row == Q - 1, last, 0.0)
        tril = causal.astype(f32)
        da = _dot_hi(triu, dcs) - _dot_hi(ct_sc[...], tril).T
        ddt_ref[...] = s2 + da * arow_ref[...]
        da_ref[...] += jnp.sum(da * dtp, axis=0, keepdims=True)

    chunk_rows = pl.BlockSpec((Q, D_SSM), lambda c: (rev(c), 0))
    return pl.pallas_call(
        body, name="ssd_bwd", grid=(nch,),
        in_specs=[pl.BlockSpec((Q, D_XBC), lambda c: (rev(c), 0)),
                  pl.BlockSpec((Q, LANES), lambda c: (rev(c), 0)),
                  pl.BlockSpec((Q, D_SSM), lambda c: (rev(c), 0)),
                  pl.BlockSpec((SSM_HEADS, Q), lambda c: (0, rev(c))),
                  _full((1, LANES)), _full((1, D_SSM)), _full((SSM_HEADS, LANES)),
                  pl.BlockSpec((1, SSM_GROUPS, SSM_N, SSM_GW), lambda c: (rev(c), 0, 0, 0)),
                  chunk_rows, chunk_rows, chunk_rows, _full((1, D_SSM)),
                  _full((1, D_SSM)), _full((D_SSM, LANES))],
        out_specs=[pl.BlockSpec((Q, D_XBC), lambda c: (rev(c), 0)),
                   pl.BlockSpec((Q, LANES), lambda c: (rev(c), 0)),
                   _full((1, LANES)), chunk_rows, _full((1, D_SSM)), _full((1, LANES))],
        out_shape=[jax.ShapeDtypeStruct((M, D_XBC), f32), jax.ShapeDtypeStruct((M, LANES), f32),
                   jax.ShapeDtypeStruct((1, LANES), f32), jax.ShapeDtypeStruct((M, D_SSM), bf16),
                   jax.ShapeDtypeStruct((1, D_SSM), f32), jax.ShapeDtypeStruct((1, LANES), f32)],
        scratch_shapes=[pltpu.VMEM((SSM_GROUPS, SSM_N, SSM_GW), f32), pltpu.VMEM((Q, D_SSM), f32),
                        pltpu.VMEM((Q, D_SSM), f32), pltpu.VMEM((SUBLANES, D_SSM), f32),
                        pltpu.VMEM((LANES, Q), f32), pltpu.VMEM((Q, D_SSM), f32), pltpu.VMEM((1, D_SSM), f32)],
        compiler_params=_cparams(("arbitrary",)),
    )(xbc_c, dtp, dt_e, dtt, a_row, a_e, a_col, hin, y, z, d_ssm, g_ssm, d_exp, head_ind)


_PEER_FLIPS = [(0, 0, 1), (0, 1, 0), (0, 1, 1), (1, 0, 0), (1, 0, 1), (1, 1, 0), (1, 1, 1)]


def _exchange_copies(ins, outs, send_sems, recv_sems, loc_sems, scatter):
    n = len(ins)
    x, y, c = lax.axis_index("x"), lax.axis_index("y"), lax.axis_index("c")
    me = 4 * x + 2 * y + c
    copies = []
    for a in range(n):
        src = ins[a].at[me] if scatter else ins[a]
        copies.append(pltpu.make_async_copy(src, outs[a].at[me], loc_sems.at[a]))
    for p, (fx, fy, fc) in enumerate(_PEER_FLIPS):
        tx = 1 - x if fx else x
        ty = 1 - y if fy else y
        tc = 1 - c if fc else c
        tgt = 4 * tx + 2 * ty + tc
        for a in range(n):
            src = ins[a].at[tgt] if scatter else ins[a]
            copies.append(pltpu.make_async_remote_copy(
                src_ref=src, dst_ref=outs[a].at[me],
                send_sem=send_sems.at[p * n + a], recv_sem=recv_sems.at[p * n + a],
                device_id=(tx, ty, tc), device_id_type=_MESH))
    return copies


def _exchange_shapes(arrays, scatter):
    return [jax.ShapeDtypeStruct(a.shape if scatter else (N_DEV,) + a.shape, a.dtype) for a in arrays]


def _exchange_sems(n):
    return [pltpu.SemaphoreType.DMA((7 * n,)), pltpu.SemaphoreType.DMA((7 * n,)), pltpu.SemaphoreType.DMA((n,))]


def _gather_two_level(arrays, name):
    n = len(arrays)

    def body(*refs):
        ins, outs = refs[:n], refs[n:2 * n]
        send_sems, recv_sems, loc_sems = refs[2 * n:]
        x, y, c = lax.axis_index("x"), lax.axis_index("y"), lax.axis_index("c")
        me, sibling = (x, y, c), (x, y, 1 - c)
        chips = [(1 - x, y), (x, 1 - y), (1 - x, 1 - y)]

        def slot(a, dev):
            return outs[a].at[4 * dev[0] + 2 * dev[1] + dev[2]]

        def copy(a, k, block, to, src=None):
            return pltpu.make_async_remote_copy(
                src_ref=slot(a, block) if src is None else src, dst_ref=slot(a, block),
                send_sem=send_sems.at[7 * a + k], recv_sem=recv_sems.at[7 * a + k],
                device_id=to, device_id_type=_MESH)

        mine = [pltpu.make_async_copy(ins[a], slot(a, me), loc_sems.at[a]) for a in range(n)]
        first = []
        for a in range(n):
            first.append(copy(a, 0, me, sibling, src=ins[a]))
            first += [copy(a, 1 + j, me, (*chip, c), src=ins[a]) for j, chip in enumerate(chips)]
        for cp in mine + first:
            cp.start()
        passed = []
        for j, chip in enumerate(chips):
            for a in range(n):
                copy(a, 1 + j, (*chip, c), me).wait_recv()
                cp = copy(a, 4 + j, (*chip, c), sibling)
                cp.start()
                passed.append(cp)
        for a in range(n):
            copy(a, 0, sibling, me).wait_recv()
            for j, chip in enumerate(chips):
                copy(a, 4 + j, (*chip, 1 - c), me).wait_recv()
        for cp in first + passed:
            cp.wait_send()
        for cp in mine:
            cp.wait()

    any_spec = pl.BlockSpec(memory_space=pl.ANY)
    return pl.pallas_call(
        body, name=name, in_specs=[any_spec] * n, out_specs=[any_spec] * n,
        out_shape=_exchange_shapes(arrays, False), scratch_shapes=_exchange_sems(n),
    )(*arrays)


def _exchange_tail(scattered, gathered, name):
    ns, ng = len(scattered), len(gathered)
    n = ns + ng

    def body(*refs):
        sems = refs[2 * n:]
        copies = (_exchange_copies(refs[:ns], refs[n:n + ns], *sems[:3], True)
                  + _exchange_copies(refs[ns:n], refs[n + ns:2 * n], *sems[3:], False))
        for cp in copies:
            cp.start()
        for cp in copies:
            cp.wait()

    any_spec = pl.BlockSpec(memory_space=pl.ANY)
    return pl.pallas_call(
        body, name=name, in_specs=[any_spec] * n, out_specs=[any_spec] * n,
        out_shape=_exchange_shapes(scattered, True) + _exchange_shapes(gathered, False),
        scratch_shapes=_exchange_sems(ns) + _exchange_sems(ng),
    )(*scattered, *gathered)


def _adamw_math(g, w, m, v):
    c1 = 1.0 - ADAM_B1 ** ADAM_STEP
    c2 = 1.0 - ADAM_B2 ** ADAM_STEP
    mn = ADAM_B1 * m + (1.0 - ADAM_B1) * g
    vn = ADAM_B2 * v + (1.0 - ADAM_B2) * (g * g)
    m_hat = mn / c1
    v_hat = vn / c2
    return -ADAM_LR * (m_hat / (jnp.sqrt(v_hat) + ADAM_EPS) + ADAM_WD * w), mn, vn


def _adamw(parts, w, m, v, name):
    R, C = w.shape
    tr = _pick(R, (PACK_ROW_TILE, 64, 32, 16, 8))

    def body(p_ref, w_ref, m_ref, v_ref, g_ref, d_ref, nm_ref, nv_ref):
        g = p_ref[0].astype(f32)
        for s in range(1, N_DEV):
            g = g + p_ref[s].astype(f32)
        g_ref[...] = g
        d_ref[...], nm_ref[...], nv_ref[...] = _adamw_math(g, w_ref[...], m_ref[...], v_ref[...])

    spec = pl.BlockSpec((tr, C), lambda i: (i, 0))
    return pl.pallas_call(
        body, name=name, grid=(R // tr,),
        in_specs=[pl.BlockSpec((N_DEV, tr, C), lambda i: (0, i, 0)), spec, spec, spec],
        out_specs=[spec] * 4, out_shape=[jax.ShapeDtypeStruct((R, C), f32)] * 4,
        compiler_params=_cparams(("parallel",)),
    )(parts, w, m, v)


def _adamw_replicated(parts, ws, ms, vs):
    n = len(ws)
    R = parts.shape[1]
    sizes = [int(w.shape[1]) for w in ws]

    def body(*refs):
        p_ref = refs[0]
        w_refs, m_refs, v_refs = refs[1:1 + n], refs[1 + n:1 + 2 * n], refs[1 + 2 * n:1 + 3 * n]
        loss_ref = refs[1 + 3 * n]
        outs = refs[2 + 3 * n:]
        g_all = p_ref[0]
        for s in range(1, N_DEV):
            g_all = g_all + p_ref[s]
        row = 0
        for p in range(n):
            pieces, left = [], sizes[p]
            while left > 0:
                take = min(left, PACK_W)
                pieces.append(g_all[row:row + 1, 0:take])
                left -= take
                row += 1
            g = pieces[0] if len(pieces) == 1 else jnp.concatenate(pieces, axis=1)
            d, mn, vn = _adamw_math(g, w_refs[p][...], m_refs[p][...], v_refs[p][...])
            outs[4 * p][...] = g
            outs[4 * p + 1][...] = d
            outs[4 * p + 2][...] = mn
            outs[4 * p + 3][...] = vn
        loss_ref[...] = g_all[row:row + 1, 0:LANES]

    in_specs = [_full((N_DEV, R, PACK_W))] + [_full((1, s)) for s in sizes] * 3
    out_specs = [_full((1, LANES))]
    out_shape = [jax.ShapeDtypeStruct((1, LANES), f32)]
    for s in sizes:
        out_specs += [_full((1, s))] * 4
        out_shape += [jax.ShapeDtypeStruct((1, s), f32)] * 4
    res = pl.pallas_call(
        body, name="adamw_replicated", in_specs=in_specs, out_specs=out_specs, out_shape=out_shape,
        compiler_params=pltpu.CompilerParams(vmem_limit_bytes=VMEM_LIMIT),
    )(parts, *ws, *ms, *vs)
    return res[0], [res[1 + 4 * p:5 + 4 * p] for p in range(n)]


def _flat_rows(a, lead_ndim):
    lead = a.shape[:lead_ndim]
    n = int(np.prod(a.shape[lead_ndim:]))
    a = a.reshape(lead + (n,))
    pad = (-n) % PACK_W
    if pad:
        a = jnp.pad(a, [(0, 0)] * lead_ndim + [(0, pad)])
    return a.reshape(lead + ((n + pad) // PACK_W, PACK_W))


def _pack(arrays, lead_ndim, total_rows, dtype):
    rows = [_flat_rows(a.astype(dtype), lead_ndim) for a in arrays]
    cat = jnp.concatenate(rows, axis=lead_ndim)
    pad = total_rows - cat.shape[lead_ndim]
    if pad:
        cat = jnp.pad(cat, [(0, 0)] * lead_ndim + [(0, pad), (0, 0)])
    return cat


def _unpack(buf, shapes, lead_ndim):
    out = []
    r = 0
    lead = buf.shape[:lead_ndim]
    for shp in shapes:
        n = int(np.prod(shp))
        nr = -(-n // PACK_W)
        piece = lax.slice_in_dim(buf, r, r + nr, axis=lead_ndim)
        piece = piece.reshape(lead + (nr * PACK_W,))
        piece = lax.slice_in_dim(piece, 0, n, axis=lead_ndim)
        out.append(piece.reshape(lead + tuple(shp)))
        r += nr
    return out


def _round_up(n, m):
    return -(-n // m) * m


def kernel(x, meta_tokens, norm_mix_pre, norm_mix_post, norm_ffn_pre, norm_ffn_post, w_in, q_a_norm, w_uq, kv_a_norm, w_ukv, attn_out_norm, ssm_conv_w, ssm_conv_b, ssm_dt_bias, ssm_A_log, ssm_D, ssm_norm, w_out, w_up, ffn_conv_w, ffn_conv_b, w_down, loss_target, m_meta_tokens, m_norm_mix_pre, m_norm_mix_post, m_norm_ffn_pre, m_norm_ffn_post, m_w_in, m_q_a_norm, m_w_uq, m_kv_a_norm, m_w_ukv, m_attn_out_norm, m_ssm_conv_w, m_ssm_conv_b, m_ssm_dt_bias, m_ssm_A_log, m_ssm_D, m_ssm_norm, m_w_out, m_w_up, m_ffn_conv_w, m_ffn_conv_b, m_w_down, v_meta_tokens, v_norm_mix_pre, v_norm_mix_post, v_norm_ffn_pre, v_norm_ffn_post, v_w_in, v_q_a_norm, v_w_uq, v_kv_a_norm, v_w_ukv, v_attn_out_norm, v_ssm_conv_w, v_ssm_conv_b, v_ssm_dt_bias, v_ssm_A_log, v_ssm_D, v_ssm_norm, v_w_out, v_w_up, v_ffn_conv_w, v_ffn_conv_b, v_w_down):
    seq = x.shape[1]
    n_real = N_META + seq
    Lp = _round_up(n_real, 768) if n_real > 2048 else _round_up(n_real, ROW_TILE)
    D = D_MODEL

    early_w = [w_uq, w_ukv]
    late_w = [w_out, w_down]
    sharded_s = [meta_tokens, ssm_conv_w, ffn_conv_w]
    grp_a = dict(names=["w_out", "w_down"], w=late_w, m=[m_w_out, m_w_down],
                 v=[v_w_out, v_w_down])
    grp_b = dict(names=["w_uq", "w_ukv", "ssm_conv_w", "ffn_conv_w"],
                 w=early_w + [ssm_conv_w, ffn_conv_w],
                 m=[m_w_uq, m_w_ukv, m_ssm_conv_w, m_ffn_conv_w],
                 v=[v_w_uq, v_w_ukv, v_ssm_conv_w, v_ffn_conv_w])
    grp_meta = dict(names=["meta_tokens"], w=[meta_tokens], m=[m_meta_tokens], v=[v_meta_tokens])
    repl_w = [norm_mix_pre, norm_mix_post, norm_ffn_pre, norm_ffn_post, q_a_norm, kv_a_norm, attn_out_norm,
              ssm_conv_b, ssm_dt_bias, ssm_A_log, ssm_D, ssm_norm, ffn_conv_b]
    repl_m = [m_norm_mix_pre, m_norm_mix_post, m_norm_ffn_pre, m_norm_ffn_post, m_q_a_norm, m_kv_a_norm,
              m_attn_out_norm, m_ssm_conv_b, m_ssm_dt_bias, m_ssm_A_log, m_ssm_D, m_ssm_norm, m_ffn_conv_b]
    repl_v = [v_norm_mix_pre, v_norm_mix_post, v_norm_ffn_pre, v_norm_ffn_post, v_q_a_norm, v_kv_a_norm,
              v_attn_out_norm, v_ssm_conv_b, v_ssm_dt_bias, v_ssm_A_log, v_ssm_D, v_ssm_norm, v_ffn_conv_b]

    def pack_rows(arrs, lead):
        return _round_up(sum(-(-int(np.prod(a.shape[lead:])) // PACK_W) for a in arrs), 16)

    wb = _pack(early_w, 0, pack_rows(early_w, 0), bf16)
    wl = _pack(late_w, 0, pack_rows(late_w, 0), bf16)
    ws = _pack(sharded_s, 0, pack_rows(sharded_s, 0), f32)
    wb_all, ws_all, win_all = _gather_two_level([wb, ws, w_in[0].astype(bf16)], "gather_weights")
    g_w_uq, g_w_ukv = _unpack(wb_all, [a.shape for a in early_w], 1)
    g_meta, g_sconv, g_fconv = _unpack(ws_all, [a.shape for a in sharded_s], 1)

    def cols(gathered):
        t = gathered[:, 0]
        return jnp.transpose(t, (1, 0, 2)).reshape(t.shape[1], N_DEV * t.shape[2])

    win = cols(win_all[:, None])
    o = np.cumsum((0, Q_RANK, KV_RANK, QK_ROPE, D_SSM, D_XBC, SSM_HEADS))
    w_q, w_kv = win[:, o[0]:o[1]], win[:, o[1]:o[2]]
    w_rope = jnp.pad(win[:, o[2]:o[3]], ((0, 0), (0, LANES - QK_ROPE)))
    w_z, w_xbc = win[:, o[3]:o[4]], win[:, o[4]:o[5]]
    w_dt = jnp.pad(win[:, o[5]:o[6]], ((0, 0), (0, LANES - SSM_HEADS)))
    wuq = g_w_uq.reshape(Q_RANK, MLA_HEADS, QK_NOPE + QK_ROPE)
    wuq = jnp.pad(wuq, ((0, 0), (0, 0), (0, QK_PAD - QK_NOPE - QK_ROPE))).reshape(Q_RANK, MLA_HEADS * QK_PAD)
    wukv = g_w_ukv.reshape(KV_RANK, MLA_HEADS * (QK_NOPE + V_DIM))
    meta_full = jnp.transpose(g_meta, (1, 0, 2)).reshape(N_META, D)
    sconv_w = jnp.pad(cols(g_sconv), ((0, SUBLANES - SSM_CONV), (0, 0)))
    fconv_w = jnp.pad(cols(g_fconv), ((0, SUBLANES - FFN_CONV), (0, 0)))

    pos = jnp.arange(Lp, dtype=f32)
    inv = ROPE_THETA ** (-jnp.arange(0, QK_ROPE, 2, dtype=f32) / QK_ROPE)
    ang = pos[:, None] * inv[None, :]
    cs_, sn_ = jnp.cos(ang), jnp.sin(ang)
    zpad = jnp.zeros((Lp, LANES - QK_ROPE), f32)
    cos_t = jnp.concatenate([cs_, cs_, zpad], axis=1)
    sin_t = jnp.concatenate([-sn_, sn_, zpad], axis=1)
    dt_bias_p = jnp.pad(ssm_dt_bias, ((0, 0), (0, LANES - SSM_HEADS)))
    a_neg = -jnp.exp(ssm_A_log)
    a_row = jnp.pad(a_neg, ((0, 0), (0, LANES - SSM_HEADS)))
    a_col = jnp.broadcast_to(a_neg.reshape(SSM_HEADS, 1), (SSM_HEADS, LANES))
    d_exp = jnp.repeat(ssm_D, SSM_P, axis=1)
    a_e = jnp.repeat(a_neg, SSM_P, axis=1)
    head_ind = (jnp.arange(D_SSM)[:, None] // SSM_P == jnp.arange(LANES)[None, :]).astype(f32)

    xb = x[0]
    h0 = jnp.concatenate([meta_full, xb, jnp.zeros((Lp - n_real, D), f32)], axis=0)
    tgt = jnp.pad(loss_target[0], ((N_META, Lp - n_real), (0, 0)))
    hn1, q_c, kv_c, kpe_raw, z, xbc, dt_raw = _in_proj(h0, norm_mix_pre, [w_q, w_kv, w_rope, w_z, w_xbc, w_dt])

    qn, qh = _up_q_rope(q_c, q_a_norm, wuq, cos_t, sin_t)
    kvn, kh, vh = _up_kv_rope(kv_c, kv_a_norm, wukv, kpe_raw, cos_t, sin_t)
    attn, lse, wl_all, wup_all = _flash_fwd(qh, kh, vh, [wl, w_up[0].astype(bf16)], False)
    g_w_out, g_w_down = _unpack(wl_all, [a.shape for a in late_w], 1)
    wout = g_w_out.reshape(D_ATTN + D_SSM, D)
    wout_a, wout_s = wout[:D_ATTN], wout[D_ATTN:]
    wup = cols(wup_all[:, None])
    wdown = g_w_down.reshape(D_FF, D)

    xbc_c = _ssm_conv_fwd(xbc, sconv_w, ssm_conv_b)
    dtp, dt_e = _dt_fwd(dt_raw, dt_bias_p, jnp.transpose(head_ind))
    dtt = jnp.transpose(dtp[:, :SSM_HEADS])
    y_ssd, hin, ssm = _ssd_fwd(xbc_c, dt_e, dtt, a_e, a_col, z, d_exp, ssm_norm)

    an, mix, h1, hn2 = _out_proj_resid(attn, attn_out_norm, ssm, wout_a, wout_s, h0, norm_mix_post, norm_ffn_pre)
    up = _mm(hn2, wup, False, "ffn_up")
    act = _ffn_gate_fwd(up, fconv_w, ffn_conv_b)
    dh2, d_down, dg_ffn_post, loss_part = _final(h1, act, wdown, norm_ffn_post, tgt, n_real)

    dw_down = _mm_tn(act, d_down, "ffn_down_dw")
    dup_g, dup_v, dwc_g, dwc_v, dbc_g, dbc_v = _ffn_gate_bwd(up, fconv_w, ffn_conv_b, d_down, wdown)
    dw_up = jnp.concatenate([_mm_tn(hn2, dup_g, "ffn_up_dw_g"), _mm_tn(hn2, dup_v, "ffn_up_dw_v")], axis=1)
    dh1, d_mix, dg_ffn_pre, dg_mix_post = _mid_bwd(h1, norm_ffn_pre, dup_g, dup_v, wup, dh2, mix, norm_mix_post)
    d_ssm = _mm(d_mix, wout_s, True, "out_proj_dx_s")
    dw_out = jnp.concatenate([_mm_tn(an, d_mix, "out_proj_dw_a"), _mm_tn(ssm, d_mix, "out_proj_dw_s")], axis=0)

    do_h, delta, dg_attn_out = _attn_out_bwd(attn, attn_out_norm, d_mix, wout_a)
    def col_blocks(gm):
        r, cc = gm.shape
        return jnp.transpose(gm.reshape(r, N_DEV, cc // N_DEV), (1, 0, 2))

    blocks_a = [dw_out.reshape(N_DEV, (D_ATTN + D_SSM) // N_DEV, D), dw_down.reshape(N_DEV, D_FF // N_DEV, D)]
    gpack_a = _pack(blocks_a, 1, pack_rows(blocks_a, 1), bf16)
    dqh, dkh, dvh, gparts_a, gparts_up = _flash_bwd(qh, kh, vh, do_h, lse, delta,
                                                    [gpack_a, col_blocks(dw_up).astype(bf16)], True)
    d_q_c, dg_q, dw_uq = _q_branch_bwd(dqh, cos_t, sin_t, wuq, q_c, q_a_norm, qn)
    d_kv_c, dg_kv, dw_ukv, d_kpe_raw = _kv_branch_bwd(dkh, dvh, cos_t, sin_t, wukv, kv_c, kv_a_norm, kvn)

    d_xbc_c, ddt, da_heads, dz, dg_ssm, dd_heads = _ssd_bwd(
        xbc_c, dtp, dt_e, dtt, a_row, a_e, a_col, hin, y_ssd, z, d_ssm, ssm_norm, d_exp, head_ind)
    d_xbc, dw_sconv, db_sconv = _ssm_conv_bwd(xbc, sconv_w, ssm_conv_b, d_xbc_c)
    d_dt_raw, d_dt_bias = _dt_bwd(dt_raw, dt_bias_p, ddt)

    dw_q, dw_kv, dw_rope, dw_z, dw_xbc, dw_dt = _in_proj_dw(hn1, [d_q_c, d_kv_c, d_kpe_raw, dz, d_xbc, d_dt_raw])
    dw_in = jnp.concatenate([dw_q, dw_kv, dw_rope[:, :QK_ROPE], dw_z, dw_xbc, dw_dt[:, :SSM_HEADS]], axis=1)
    dw_uq3 = dw_uq.reshape(Q_RANK, MLA_HEADS, QK_PAD)[:, :, :QK_NOPE + QK_ROPE]
    blocks_b = [
        dw_uq3.reshape(N_DEV, Q_RANK // N_DEV, MLA_HEADS, QK_NOPE + QK_ROPE),
        dw_ukv.reshape(N_DEV, KV_RANK // N_DEV, MLA_HEADS, QK_NOPE + V_DIM),
        col_blocks(dw_sconv[:SSM_CONV]),
        col_blocks(jnp.concatenate([dwc_g, dwc_v], axis=1)[:FFN_CONV]),
    ]
    gpack_b = _pack(blocks_b, 1, pack_rows(blocks_b, 1), bf16)
    dh0, dg_mix_pre, gparts_b, gparts_in = _in_proj_dx(
        [d_q_c, d_kv_c, d_kpe_raw, dz, d_xbc, d_dt_raw], [w_q, w_kv, w_rope, w_z, w_xbc, w_dt],
        h0, norm_mix_pre, dh1, [gpack_b, col_blocks(dw_in).astype(bf16)], True)

    grad_x = dh0[N_META:n_real][None]
    meta_blocks = col_blocks(dh0[:N_META]).reshape(N_DEV, N_META * D // N_DEV // PACK_W, PACK_W)


    def adam_group(parts, grp, name):
        rows = parts.shape[1]
        packs = [_pack([a[None] for a in grp[k]], 1, rows, f32)[0] for k in ("w", "m", "v")]
        outs = _adamw(parts, *packs, name)
        shapes = [a.shape for a in grp["w"]]
        return [dict(zip(grp["names"], [t[0] for t in _unpack(b[None], shapes, 1)])) for b in outs]

    def adam_own_layout(parts, name, w, m, v):
        return [{name: t[None]} for t in _adamw(parts, w[0], m[0], v[0], "adamw_" + name)]

    sh_a = adam_group(gparts_a, grp_a, "adamw_sharded_a")
    sh_b = adam_group(gparts_b, grp_b, "adamw_sharded_b")
    sh_in = adam_own_layout(gparts_in, "w_in", w_in, m_w_in, v_w_in)
    sh_up = adam_own_layout(gparts_up, "w_up", w_up, m_w_up, v_w_up)

    dg_alog = da_heads[:, :SSM_HEADS] * a_neg
    repl_g = [dg_mix_pre, dg_mix_post, dg_ffn_pre, dg_ffn_post, dg_q, dg_kv, dg_attn_out, db_sconv,
              d_dt_bias[:, :SSM_HEADS], dg_alog, dd_heads[:, :SSM_HEADS], dg_ssm,
              jnp.concatenate([dbc_g, dbc_v], axis=1)]
    loss_vec = loss_part[:, :1]
    small_total = _round_up(sum(-(-int(np.prod(a.shape)) // PACK_W) for a in repl_g) + 1, 16)
    spack = _pack(repl_g + [loss_vec], 0, small_total, f32)
    gparts_meta, sparts = _exchange_tail([meta_blocks], [spack], "exchange_tail")
    sh_meta = adam_group(gparts_meta, grp_meta, "adamw_meta")
    loss_row, repl_out = _adamw_replicated(sparts, repl_w, repl_m, repl_v)
    loss = loss_row[0, 0]

    order = ["meta_tokens", "norm_mix_pre", "norm_mix_post", "norm_ffn_pre", "norm_ffn_post", "w_in", "q_a_norm",
             "w_uq", "kv_a_norm", "w_ukv", "attn_out_norm", "ssm_conv_w", "ssm_conv_b", "ssm_dt_bias", "ssm_A_log",
             "ssm_D", "ssm_norm", "w_out", "w_up", "ffn_conv_w", "ffn_conv_b", "w_down"]
    rp_names = ["norm_mix_pre", "norm_mix_post", "norm_ffn_pre", "norm_ffn_post", "q_a_norm", "kv_a_norm",
                "attn_out_norm", "ssm_conv_b", "ssm_dt_bias", "ssm_A_log", "ssm_D", "ssm_norm", "ffn_conv_b"]

    def lookup(k):
        d = {**sh_a[k], **sh_b[k], **sh_in[k], **sh_up[k], **sh_meta[k],
             **{n: four[k] for n, four in zip(rp_names, repl_out)}}
        return [d[n] for n in order]

    return (loss, grad_x, *lookup(0), *lookup(1), *lookup(2), *lookup(3))
```
